```python
import math
import jax, jax.numpy as jnp
from jax import lax
import numpy as np

D_MODEL = 2048
BATCH = 8
SEQ = 2048
DEPTH = 4

N_MIXERS = 4
MIX_WIDTH = D_MODEL
GROUP_W = MIX_WIDTH // N_MIXERS
S5_CH_PER_GROUP = 16
S5_GROUPS = GROUP_W // S5_CH_PER_GROUP
S5_STATE = 64
S5_DT_MIN = 0.001
S5_DT_MAX = 0.1
POOL_WINDOWS = (2, 4, 8, 16)
POOL_CH = GROUP_W // len(POOL_WINDOWS)
CONV_WIDTH = 31
ATT_HEADS = 8
ATT_HEAD_DIM = GROUP_W // ATT_HEADS
DILATED_PATTERNS = ((128, 1), (512, 4), (2048, 16))
ATT_BLOCK = 128
REL_BUCKETS = 32
REL_MAX_DIST = 2048
MEM_LEN = 256
X_HEADS = 4
X_HEAD_DIM = 128
X_WIDTH = X_HEADS * X_HEAD_DIM
D_FF = 4 * D_MODEL
NORM_EPS = 1e-6
NEG_INF = -1e30
IN_WIDTH = GROUP_W + GROUP_W + 2 * GROUP_W + 3 * GROUP_W

kernel_name = 'hymba_style_multimixer_trunk'

F32 = jnp.float32


def rmsnorm(x, g):
    xf = x.astype(F32)
    y = xf * lax.rsqrt(jnp.mean(xf * xf, axis=-1, keepdims=True) + NORM_EPS)
    return (y * g.astype(F32)).astype(x.dtype)


def layernorm(x, g, b):
    xf = x.astype(F32)
    xc = xf - jnp.mean(xf, axis=-1, keepdims=True)
    y = xc * lax.rsqrt(jnp.mean(xc * xc, axis=-1, keepdims=True) + NORM_EPS)
    return (y * g.astype(F32) + b.astype(F32)).astype(x.dtype)


def group_rmsnorm(y, g, out_dtype):
    Bsz, L, _ = y.shape
    yf = y.astype(F32).reshape(Bsz, L, N_MIXERS, GROUP_W)
    yf = yf * lax.rsqrt(jnp.mean(yf * yf, axis=-1, keepdims=True) + NORM_EPS)
    return (yf.reshape(Bsz, L, MIX_WIDTH) * g.astype(F32)).astype(out_dtype)


def _cmul(ar, ai, br, bi):
    return ar * br - ai * bi, ar * bi + ai * br


def s5_mixer(u, lam_re, lam_im, log_dt, b_re, b_im, c_re, c_im, d_skip, w_glu):
    Bsz, L, _ = u.shape
    uf = u.astype(F32).reshape(Bsz, L, S5_GROUPS, S5_CH_PER_GROUP)
    lr = lam_re.astype(F32)
    li = lam_im.astype(F32)
    dt = jnp.exp(log_dt.astype(F32))[:, None]
    mag = jnp.exp(lr * dt)
    ab_r, ab_i = mag * jnp.cos(li * dt), mag * jnp.sin(li * dt)
    den = lr * lr + li * li
    nr, ni = ab_r - 1.0, ab_i
    f_r = (nr * lr + ni * li) / den
    f_i = (ni * lr - nr * li) / den
    bb_r, bb_i = _cmul(f_r[..., None], f_i[..., None], b_re.astype(F32), b_im.astype(F32))
    bu_r = jnp.einsum('gnc,blgc->blgn', bb_r, uf)
    bu_i = jnp.einsum('gnc,blgc->blgn', bb_i, uf)
    a_r = jnp.broadcast_to(ab_r, bu_r.shape)
    a_i = jnp.broadcast_to(ab_i, bu_i.shape)

    def combine(e1, e2):
        a1r, a1i, b1r, b1i = e1
        a2r, a2i, b2r, b2i = e2
        ar, ai = _cmul(a2r, a2i, a1r, a1i)
        br, bi = _cmul(a2r, a2i, b1r, b1i)
        return ar, ai, br + b2r, bi + b2i

    _, _, xr, xi = lax.associative_scan(combine, (a_r, a_i, bu_r, bu_i), axis=1)
    y = jnp.einsum('gcn,blgn->blgc', c_re.astype(F32), xr) - jnp.einsum('gcn,blgn->blgc', c_im.astype(F32), xi)
    y = y.reshape(Bsz, L, GROUP_W) + d_skip.astype(F32) * uf.reshape(Bsz, L, GROUP_W)
    g = jax.nn.gelu(y)
    out = g * jax.nn.sigmoid(jnp.einsum('blc,cd->bld', g, w_glu.astype(F32)))
    return out.astype(u.dtype)


def pool_mixer(u, pool_w, pool_scale):
    Bsz, L, _ = u.shape
    uf = u.astype(F32).reshape(Bsz, L, len(POOL_WINDOWS), POOL_CH)
    cs = jnp.cumsum(uf, axis=1)
    t = jnp.arange(L)
    pooled = []
    for gi, w in enumerate(POOL_WINDOWS):
        c = cs[:, :, gi]
        shifted = jnp.pad(c, ((0, 0), (w, 0), (0, 0)))[:, :L]
        cnt = jnp.minimum(t + 1, w).astype(F32)[None, :, None]
        pooled.append((c - shifted) / cnt - uf[:, :, gi])
    p = jnp.stack(pooled, axis=2)
    y = jnp.einsum('blgc,gcd->blgd', p, pool_w.astype(F32)).reshape(Bsz, L, GROUP_W)
    return (y * pool_scale.astype(F32)).astype(u.dtype)


def conv_mixer(u, w_dw, b_dw, ln_g, ln_b, w_pw):
    val, gate = jnp.split(u, 2, axis=-1)
    h = val * jax.nn.sigmoid(gate)
    h = lax.conv_general_dilated(h, w_dw[:, None, :], window_strides=(1,),
                                 padding=((CONV_WIDTH - 1, 0),),
                                 dimension_numbers=('NWC', 'WIO', 'NWC'),
                                 feature_group_count=GROUP_W) + b_dw
    h = jax.nn.silu(layernorm(h, ln_g, ln_b))
    return jnp.einsum('blc,cd->bld', h, w_pw)


def _t5_bucket(dist):
    n = np.maximum(dist, 0)
    max_exact = REL_BUCKETS // 2
    large = max_exact + (np.log(np.maximum(n, 1) / max_exact) / np.log(REL_MAX_DIST / max_exact)
                         * (REL_BUCKETS - max_exact)).astype(np.int64)
    large = np.minimum(large, REL_BUCKETS - 1)
    return np.where(n < max_exact, n, large).astype(np.int32)


def _dilated_branch(q, k, v, rel_bias, window, dilation):
    Bsz, L, H, E = q.shape
    Ls = L // dilation
    nb = -(-Ls // ATT_BLOCK)
    pad = nb * ATT_BLOCK - Ls

    def to_sub(t):
        t = t.reshape(Bsz, Ls, dilation, H, E).transpose(0, 2, 3, 1, 4)
        return jnp.pad(t, ((0, 0), (0, 0), (0, 0), (0, pad), (0, 0)))

    def band(t):
        t = jnp.pad(t, ((0, 0), (0, 0), (0, 0), (ATT_BLOCK, 0), (0, 0)))
        t = t.reshape(Bsz, dilation, H, nb + 1, ATT_BLOCK, E)
        return jnp.concatenate([t[:, :, :, :-1], t[:, :, :, 1:]], axis=4)

    qb = to_sub(q).reshape(Bsz, dilation, H, nb, ATT_BLOCK, E)
    kb = band(to_sub(k))
    vb = band(to_sub(v))
    s = jnp.einsum('bdhnqe,bdhnke->bdhnqk', qb, kb, preferred_element_type=F32) * (E ** -0.5)
    a_idx = np.arange(ATT_BLOCK)[:, None]
    b_idx = np.arange(2 * ATT_BLOCK)[None, :]
    sub_dist = a_idx + ATT_BLOCK - b_idx
    key_idx = np.arange(nb)[:, None, None] * ATT_BLOCK - ATT_BLOCK + b_idx[None]
    valid = (sub_dist >= 0) & (sub_dist <= window // dilation) & (key_idx >= 0)
    bucket = _t5_bucket(sub_dist * dilation)
    bias = jnp.transpose(rel_bias[jnp.asarray(bucket)], (2, 0, 1)).astype(F32)
    s = s + bias[None, None, :, None]
    s = jnp.where(jnp.asarray(valid)[None, None, None], s, NEG_INF)
    m = jnp.max(s, axis=-1, keepdims=True)
    p = jnp.exp(s - m)
    den = jnp.sum(p, axis=-1, keepdims=True)
    o = jnp.einsum('bdhnqk,bdhnke->bdhnqe', p, vb.astype(F32)) / den
    lse = (m + jnp.log(den))[..., 0]
    o = o.reshape(Bsz, dilation, H, nb * ATT_BLOCK, E)[:, :, :, :Ls]
    o = o.transpose(0, 3, 1, 2, 4).reshape(Bsz, L, H, E)
    lse = lse.reshape(Bsz, dilation, H, nb * ATT_BLOCK)[:, :, :, :Ls]
    lse = lse.transpose(0, 3, 1, 2).reshape(Bsz, L, H)
    return o, lse


def dilated_attention(qkv, rel_bias):
    Bsz, L, _ = qkv.shape
    q, k, v = [t.reshape(Bsz, L, ATT_HEADS, ATT_HEAD_DIM) for t in jnp.split(qkv, 3, axis=-1)]
    outs, lses = [], []
    for window, dilation in DILATED_PATTERNS:
        o, lse = _dilated_branch(q, k, v, rel_bias, window, dilation)
        outs.append(o)
        lses.append(lse)
    wts = jax.nn.softmax(jnp.stack(lses, axis=0), axis=0)
    o = jnp.einsum('pblh,pblhe->blhe', wts, jnp.stack(outs, axis=0))
    return o.reshape(Bsz, L, GROUP_W).astype(qkv.dtype)


def cross_attention(h, mem_n, w_xq, w_xk, w_xv, w_xo):
    Bsz, L, _ = h.shape
    M = mem_n.shape[1]
    q = jnp.einsum('bld,de->ble', h, w_xq).reshape(Bsz, L, X_HEADS, X_HEAD_DIM)
    k = jnp.einsum('bmd,de->bme', mem_n, w_xk).reshape(Bsz, M, X_HEADS, X_HEAD_DIM)
    v = jnp.einsum('bmd,de->bme', mem_n, w_xv).reshape(Bsz, M, X_HEADS, X_HEAD_DIM)
    s = jnp.einsum('blhe,bmhe->bhlm', q, k, preferred_element_type=F32) * (X_HEAD_DIM ** -0.5)
    p = jax.nn.softmax(s, axis=-1)
    o = jnp.einsum('bhlm,bmhe->blhe', p, v.astype(F32)).reshape(Bsz, L, X_WIDTH).astype(h.dtype)
    return jnp.einsum('ble,ed->bld', o, w_xo)


def _fwd_setup_inputs(seed: int = 0) -> dict:
    key = jax.random.key(seed)
    ks = iter(jax.random.split(key, 48))

    def nrm(shape, scale):
        return jax.random.normal(next(ks), shape, F32) * scale

    def gain(shape):
        return 1.0 + nrm(shape, 0.02)

    x = nrm((BATCH, SEQ, D_MODEL), 1.0)
    mem = nrm((BATCH, MEM_LEN, D_MODEL), 1.0)
    rel_bias = nrm((REL_BUCKETS, ATT_HEADS), 0.2)
    mem_norm_g = gain((D_MODEL,))
    norm_mix_g = gain((DEPTH, D_MODEL))
    w_in = nrm((DEPTH, D_MODEL, IN_WIDTH), D_MODEL ** -0.5)
    s5_lam_re = -0.5 * jnp.exp(nrm((DEPTH, S5_GROUPS, S5_STATE), 0.02))
    s5_lam_im = math.pi * jnp.arange(S5_STATE, dtype=F32) + nrm((DEPTH, S5_GROUPS, S5_STATE), 0.02)
    s5_log_dt = jax.random.uniform(next(ks), (DEPTH, S5_GROUPS), F32,
                                   math.log(S5_DT_MIN), math.log(S5_DT_MAX))
    s5_b_re = nrm((DEPTH, S5_GROUPS, S5_STATE, S5_CH_PER_GROUP), (2 * S5_CH_PER_GROUP) ** -0.5)
    s5_b_im = nrm((DEPTH, S5_GROUPS, S5_STATE, S5_CH_PER_GROUP), (2 * S5_CH_PER_GROUP) ** -0.5)
    s5_c_re = nrm((DEPTH, S5_GROUPS, S5_CH_PER_GROUP, S5_STATE), 0.5)
    s5_c_im = nrm((DEPTH, S5_GROUPS, S5_CH_PER_GROUP, S5_STATE), 0.5)
    s5_d = nrm((DEPTH, GROUP_W), 1.0)
    s5_w_glu = nrm((DEPTH, GROUP_W, GROUP_W), GROUP_W ** -0.5)
    pool_w = nrm((DEPTH, len(POOL_WINDOWS), POOL_CH, POOL_CH), POOL_CH ** -0.5)
    pool_scale = gain((DEPTH, GROUP_W))
    conv_w_dw = nrm((DEPTH, CONV_WIDTH, GROUP_W), CONV_WIDTH ** -0.5)
    conv_b_dw = nrm((DEPTH, GROUP_W), 0.02)
    conv_ln_g = gain((DEPTH, GROUP_W))
    conv_ln_b = nrm((DEPTH, GROUP_W), 0.02)
    conv_w_pw = nrm((DEPTH, GROUP_W, GROUP_W), GROUP_W ** -0.5)
    grp_norm_g = gain((DEPTH, MIX_WIDTH))
    w_out = nrm((DEPTH, MIX_WIDTH, D_MODEL), MIX_WIDTH ** -0.5)
    norm_x_g = gain((DEPTH, D_MODEL))
    w_xq = nrm((DEPTH, D_MODEL, X_WIDTH), D_MODEL ** -0.5)
    w_xk = nrm((DEPTH, D_MODEL, X_WIDTH), D_MODEL ** -0.5)
    w_xv = nrm((DEPTH, D_MODEL, X_WIDTH), D_MODEL ** -0.5)
    w_xo = nrm((DEPTH, X_WIDTH, D_MODEL), X_WIDTH ** -0.5)
    norm_mlp_g = gain((DEPTH, D_MODEL))
    w_up = nrm((DEPTH, D_MODEL, D_FF), D_MODEL ** -0.5)
    w_down = nrm((DEPTH, D_FF, D_MODEL), D_FF ** -0.5)
    norm_final_g = gain((D_MODEL,))
    return {'x': x, 'mem': mem, 'rel_bias': rel_bias, 'mem_norm_g': mem_norm_g,
            'norm_mix_g': norm_mix_g, 'w_in': w_in,
            's5_lam_re': s5_lam_re, 's5_lam_im': s5_lam_im, 's5_log_dt': s5_log_dt,
            's5_b_re': s5_b_re, 's5_b_im': s5_b_im, 's5_c_re': s5_c_re, 's5_c_im': s5_c_im,
            's5_d': s5_d, 's5_w_glu': s5_w_glu,
            'pool_w': pool_w, 'pool_scale': pool_scale,
            'conv_w_dw': conv_w_dw, 'conv_b_dw': conv_b_dw, 'conv_ln_g': conv_ln_g,
            'conv_ln_b': conv_ln_b, 'conv_w_pw': conv_w_pw,
            'grp_norm_g': grp_norm_g, 'w_out': w_out,
            'norm_x_g': norm_x_g, 'w_xq': w_xq, 'w_xk': w_xk, 'w_xv': w_xv, 'w_xo': w_xo,
            'norm_mlp_g': norm_mlp_g, 'w_up': w_up, 'w_down': w_down,
            'norm_final_g': norm_final_g}


def _fwd_reference(x, mem, rel_bias, mem_norm_g, norm_mix_g, w_in,
              s5_lam_re, s5_lam_im, s5_log_dt, s5_b_re, s5_b_im, s5_c_re, s5_c_im, s5_d, s5_w_glu,
              pool_w, pool_scale, conv_w_dw, conv_b_dw, conv_ln_g, conv_ln_b, conv_w_pw,
              grp_norm_g, w_out, norm_x_g, w_xq, w_xk, w_xv, w_xo,
              norm_mlp_g, w_up, w_down, norm_final_g):
    mem_n = rmsnorm(mem, mem_norm_g)
    h = x
    for l in range(DEPTH):
        xn = rmsnorm(h, norm_mix_g[l])
        proj = jnp.einsum('bld,dc->blc', xn, w_in[l])
        u_a, u_b, u_c, qkv = jnp.split(proj, [GROUP_W, 2 * GROUP_W, 4 * GROUP_W], axis=-1)
        y_a = s5_mixer(u_a, s5_lam_re[l], s5_lam_im[l], s5_log_dt[l], s5_b_re[l], s5_b_im[l],
                       s5_c_re[l], s5_c_im[l], s5_d[l], s5_w_glu[l])
        y_b = pool_mixer(u_b, pool_w[l], pool_scale[l])
        y_c = conv_mixer(u_c, conv_w_dw[l], conv_b_dw[l], conv_ln_g[l], conv_ln_b[l], conv_w_pw[l])
        y_d = dilated_attention(qkv, rel_bias)
        y = group_rmsnorm(jnp.concatenate([y_a, y_b, y_c, y_d], axis=-1), grp_norm_g[l], h.dtype)
        h = h + jnp.einsum('blc,cd->bld', y, w_out[l])
        h = h + cross_attention(rmsnorm(h, norm_x_g[l]), mem_n, w_xq[l], w_xk[l], w_xv[l], w_xo[l])
        hn = rmsnorm(h, norm_mlp_g[l])
        h = h + jnp.einsum('blf,fd->bld', jnp.square(jax.nn.relu(jnp.einsum('bld,df->blf', hn, w_up[l]))), w_down[l])
    return rmsnorm(h, norm_final_g)


import jax as _jax
import jax.numpy as _jnp

TWIN_FORMAT = 'train_step'
FWD_PARAMS = ['x', 'mem', 'rel_bias', 'mem_norm_g', 'norm_mix_g', 'w_in', 's5_lam_re', 's5_lam_im', 's5_log_dt', 's5_b_re', 's5_b_im', 's5_c_re', 's5_c_im', 's5_d', 's5_w_glu', 'pool_w', 'pool_scale', 'conv_w_dw', 'conv_b_dw', 'conv_ln_g', 'conv_ln_b', 'conv_w_pw', 'grp_norm_g', 'w_out', 'norm_x_g', 'w_xq', 'w_xk', 'w_xv', 'w_xo', 'norm_mlp_g', 'w_up', 'w_down', 'norm_final_g']
TWIN_WEIGHTS = ['rel_bias', 'mem_norm_g', 'norm_mix_g', 'w_in', 's5_lam_re', 's5_lam_im', 's5_log_dt', 's5_b_re', 's5_b_im', 's5_c_re', 's5_c_im', 's5_d', 's5_w_glu', 'pool_w', 'pool_scale', 'conv_w_dw', 'conv_b_dw', 'conv_ln_g', 'conv_ln_b', 'conv_w_pw', 'grp_norm_g', 'w_out', 'norm_x_g', 'w_xq', 'w_xk', 'w_xv', 'w_xo', 'norm_mlp_g', 'w_up', 'w_down', 'norm_final_g']
TWIN_DIFF_INPUT = 'x'
TWIN_INPUTS = ['x', 'mem', 'rel_bias', 'mem_norm_g', 'norm_mix_g', 'w_in', 's5_lam_re', 's5_lam_im', 's5_log_dt', 's5_b_re', 's5_b_im', 's5_c_re', 's5_c_im', 's5_d', 's5_w_glu', 'pool_w', 'pool_scale', 'conv_w_dw', 'conv_b_dw', 'conv_ln_g', 'conv_ln_b', 'conv_w_pw', 'grp_norm_g', 'w_out', 'norm_x_g', 'w_xq', 'w_xk', 'w_xv', 'w_xo', 'norm_mlp_g', 'w_up', 'w_down', 'norm_final_g', 'loss_target', 'm_rel_bias', 'm_mem_norm_g', 'm_norm_mix_g', 'm_w_in', 'm_s5_lam_re', 'm_s5_lam_im', 'm_s5_log_dt', 'm_s5_b_re', 'm_s5_b_im', 'm_s5_c_re', 'm_s5_c_im', 'm_s5_d', 'm_s5_w_glu', 'm_pool_w', 'm_pool_scale', 'm_conv_w_dw', 'm_conv_b_dw', 'm_conv_ln_g', 'm_conv_ln_b', 'm_conv_w_pw', 'm_grp_norm_g', 'm_w_out', 'm_norm_x_g', 'm_w_xq', 'm_w_xk', 'm_w_xv', 'm_w_xo', 'm_norm_mlp_g', 'm_w_up', 'm_w_down', 'm_norm_final_g', 'v_rel_bias', 'v_mem_norm_g', 'v_norm_mix_g', 'v_w_in', 'v_s5_lam_re', 'v_s5_lam_im', 'v_s5_log_dt', 'v_s5_b_re', 'v_s5_b_im', 'v_s5_c_re', 'v_s5_c_im', 'v_s5_d', 'v_s5_w_glu', 'v_pool_w', 'v_pool_scale', 'v_conv_w_dw', 'v_conv_b_dw', 'v_conv_ln_g', 'v_conv_ln_b', 'v_conv_w_pw', 'v_grp_norm_g', 'v_w_out', 'v_norm_x_g', 'v_w_xq', 'v_w_xk', 'v_w_xv', 'v_w_xo', 'v_norm_mlp_g', 'v_w_up', 'v_w_down', 'v_norm_final_g']
TWIN_OUTPUTS = ['loss', 'grad_x', 'grad_rel_bias', 'grad_mem_norm_g', 'grad_norm_mix_g', 'grad_w_in', 'grad_s5_lam_re', 'grad_s5_lam_im', 'grad_s5_log_dt', 'grad_s5_b_re', 'grad_s5_b_im', 'grad_s5_c_re', 'grad_s5_c_im', 'grad_s5_d', 'grad_s5_w_glu', 'grad_pool_w', 'grad_pool_scale', 'grad_conv_w_dw', 'grad_conv_b_dw', 'grad_conv_ln_g', 'grad_conv_ln_b', 'grad_conv_w_pw', 'grad_grp_norm_g', 'grad_w_out', 'grad_norm_x_g', 'grad_w_xq', 'grad_w_xk', 'grad_w_xv', 'grad_w_xo', 'grad_norm_mlp_g', 'grad_w_up', 'grad_w_down', 'grad_norm_final_g', 'delta_rel_bias', 'delta_mem_norm_g', 'delta_norm_mix_g', 'delta_w_in', 'delta_s5_lam_re', 'delta_s5_lam_im', 'delta_s5_log_dt', 'delta_s5_b_re', 'delta_s5_b_im', 'delta_s5_c_re', 'delta_s5_c_im', 'delta_s5_d', 'delta_s5_w_glu', 'delta_pool_w', 'delta_pool_scale', 'delta_conv_w_dw', 'delta_conv_b_dw', 'delta_conv_ln_g', 'delta_conv_ln_b', 'delta_conv_w_pw', 'delta_grp_norm_g', 'delta_w_out', 'delta_norm_x_g', 'delta_w_xq', 'delta_w_xk', 'delta_w_xv', 'delta_w_xo', 'delta_norm_mlp_g', 'delta_w_up', 'delta_w_down', 'delta_norm_final_g', 'new_m_rel_bias', 'new_m_mem_norm_g', 'new_m_norm_mix_g', 'new_m_w_in', 'new_m_s5_lam_re', 'new_m_s5_lam_im', 'new_m_s5_log_dt', 'new_m_s5_b_re', 'new_m_s5_b_im', 'new_m_s5_c_re', 'new_m_s5_c_im', 'new_m_s5_d', 'new_m_s5_w_glu', 'new_m_pool_w', 'new_m_pool_scale', 'new_m_conv_w_dw', 'new_m_conv_b_dw', 'new_m_conv_ln_g', 'new_m_conv_ln_b', 'new_m_conv_w_pw', 'new_m_grp_norm_g', 'new_m_w_out', 'new_m_norm_x_g', 'new_m_w_xq', 'new_m_w_xk', 'new_m_w_xv', 'new_m_w_xo', 'new_m_norm_mlp_g', 'new_m_w_up', 'new_m_w_down', 'new_m_norm_final_g', 'new_v_rel_bias', 'new_v_mem_norm_g', 'new_v_norm_mix_g', 'new_v_w_in', 'new_v_s5_lam_re', 'new_v_s5_lam_im', 'new_v_s5_log_dt', 'new_v_s5_b_re', 'new_v_s5_b_im', 'new_v_s5_c_re', 'new_v_s5_c_im', 'new_v_s5_d', 'new_v_s5_w_glu', 'new_v_pool_w', 'new_v_pool_scale', 'new_v_conv_w_dw', 'new_v_conv_b_dw', 'new_v_conv_ln_g', 'new_v_conv_ln_b', 'new_v_conv_w_pw', 'new_v_grp_norm_g', 'new_v_w_out', 'new_v_norm_x_g', 'new_v_w_xq', 'new_v_w_xk', 'new_v_w_xv', 'new_v_w_xo', 'new_v_norm_mlp_g', 'new_v_w_up', 'new_v_w_down', 'new_v_norm_final_g']
TWIN_LEAF_KINDS = {'loss': 'loss', 'grad_x': 'grad_x', 'grad_rel_bias': 'grad_w', 'grad_mem_norm_g': 'grad_w', 'grad_norm_mix_g': 'grad_w', 'grad_w_in': 'grad_w', 'grad_s5_lam_re': 'grad_w', 'grad_s5_lam_im': 'grad_w', 'grad_s5_log_dt': 'grad_w', 'grad_s5_b_re': 'grad_w', 'grad_s5_b_im': 'grad_w', 'grad_s5_c_re': 'grad_w', 'grad_s5_c_im': 'grad_w', 'grad_s5_d': 'grad_w', 'grad_s5_w_glu': 'grad_w', 'grad_pool_w': 'grad_w', 'grad_pool_scale': 'grad_w', 'grad_conv_w_dw': 'grad_w', 'grad_conv_b_dw': 'grad_w', 'grad_conv_ln_g': 'grad_w', 'grad_conv_ln_b': 'grad_w', 'grad_conv_w_pw': 'grad_w', 'grad_grp_norm_g': 'grad_w', 'grad_w_out': 'grad_w', 'grad_norm_x_g': 'grad_w', 'grad_w_xq': 'grad_w', 'grad_w_xk': 'grad_w', 'grad_w_xv': 'grad_w', 'grad_w_xo': 'grad_w', 'grad_norm_mlp_g': 'grad_w', 'grad_w_up': 'grad_w', 'grad_w_down': 'grad_w', 'grad_norm_final_g': 'grad_w', 'delta_rel_bias': 'delta_w', 'delta_mem_norm_g': 'delta_w', 'delta_norm_mix_g': 'delta_w', 'delta_w_in': 'delta_w', 'delta_s5_lam_re': 'delta_w', 'delta_s5_lam_im': 'delta_w', 'delta_s5_log_dt': 'delta_w', 'delta_s5_b_re': 'delta_w', 'delta_s5_b_im': 'delta_w', 'delta_s5_c_re': 'delta_w', 'delta_s5_c_im': 'delta_w', 'delta_s5_d': 'delta_w', 'delta_s5_w_glu': 'delta_w', 'delta_pool_w': 'delta_w', 'delta_pool_scale': 'delta_w', 'delta_conv_w_dw': 'delta_w', 'delta_conv_b_dw': 'delta_w', 'delta_conv_ln_g': 'delta_w', 'delta_conv_ln_b': 'delta_w', 'delta_conv_w_pw': 'delta_w', 'delta_grp_norm_g': 'delta_w', 'delta_w_out': 'delta_w', 'delta_norm_x_g': 'delta_w', 'delta_w_xq': 'delta_w', 'delta_w_xk': 'delta_w', 'delta_w_xv': 'delta_w', 'delta_w_xo': 'delta_w', 'delta_norm_mlp_g': 'delta_w', 'delta_w_up': 'delta_w', 'delta_w_down': 'delta_w', 'delta_norm_final_g': 'delta_w', 'new_m_rel_bias': 'new_m', 'new_m_mem_norm_g': 'new_m', 'new_m_norm_mix_g': 'new_m', 'new_m_w_in': 'new_m', 'new_m_s5_lam_re': 'new_m', 'new_m_s5_lam_im': 'new_m', 'new_m_s5_log_dt': 'new_m', 'new_m_s5_b_re': 'new_m', 'new_m_s5_b_im': 'new_m', 'new_m_s5_c_re': 'new_m', 'new_m_s5_c_im': 'new_m', 'new_m_s5_d': 'new_m', 'new_m_s5_w_glu': 'new_m', 'new_m_pool_w': 'new_m', 'new_m_pool_scale': 'new_m', 'new_m_conv_w_dw': 'new_m', 'new_m_conv_b_dw': 'new_m', 'new_m_conv_ln_g': 'new_m', 'new_m_conv_ln_b': 'new_m', 'new_m_conv_w_pw': 'new_m', 'new_m_grp_norm_g': 'new_m', 'new_m_w_out': 'new_m', 'new_m_norm_x_g': 'new_m', 'new_m_w_xq': 'new_m', 'new_m_w_xk': 'new_m', 'new_m_w_xv': 'new_m', 'new_m_w_xo': 'new_m', 'new_m_norm_mlp_g': 'new_m', 'new_m_w_up': 'new_m', 'new_m_w_down': 'new_m', 'new_m_norm_final_g': 'new_m', 'new_v_rel_bias': 'new_v', 'new_v_mem_norm_g': 'new_v', 'new_v_norm_mix_g': 'new_v', 'new_v_w_in': 'new_v', 'new_v_s5_lam_re': 'new_v', 'new_v_s5_lam_im': 'new_v', 'new_v_s5_log_dt': 'new_v', 'new_v_s5_b_re': 'new_v', 'new_v_s5_b_im': 'new_v', 'new_v_s5_c_re': 'new_v', 'new_v_s5_c_im': 'new_v', 'new_v_s5_d': 'new_v', 'new_v_s5_w_glu': 'new_v', 'new_v_pool_w': 'new_v', 'new_v_pool_scale': 'new_v', 'new_v_conv_w_dw': 'new_v', 'new_v_conv_b_dw': 'new_v', 'new_v_conv_ln_g': 'new_v', 'new_v_conv_ln_b': 'new_v', 'new_v_conv_w_pw': 'new_v', 'new_v_grp_norm_g': 'new_v', 'new_v_w_out': 'new_v', 'new_v_norm_x_g': 'new_v', 'new_v_w_xq': 'new_v', 'new_v_w_xk': 'new_v', 'new_v_w_xv': 'new_v', 'new_v_w_xo': 'new_v', 'new_v_norm_mlp_g': 'new_v', 'new_v_w_up': 'new_v', 'new_v_w_down': 'new_v', 'new_v_norm_final_g': 'new_v'}


def _forward(args):
    return _fwd_reference(*[args[k] for k in FWD_PARAMS])


def _output_shape():
    out = _jax.eval_shape(lambda: _forward(_fwd_setup_inputs(0)))
    return out.shape, out.dtype

N_MICROBATCH = 1
ADAM_LR = 0.001
ADAM_B1 = 0.9
ADAM_B2 = 0.999
ADAM_EPS = 1e-08
ADAM_WD = 0.01
ADAM_STEP = 10
PER_EXAMPLE_BATCH_AXIS = {'x': 0, 'mem': 0, 'loss_target': 0}
SHARED_INPUTS = []
_WEIGHT_DTYPES = {'rel_bias': _jnp.float32, 'mem_norm_g': _jnp.float32, 'norm_mix_g': _jnp.float32, 'w_in': _jnp.float32, 's5_lam_re': _jnp.float32, 's5_lam_im': _jnp.float32, 's5_log_dt': _jnp.float32, 's5_b_re': _jnp.float32, 's5_b_im': _jnp.float32, 's5_c_re': _jnp.float32, 's5_c_im': _jnp.float32, 's5_d': _jnp.float32, 's5_w_glu': _jnp.float32, 'pool_w': _jnp.float32, 'pool_scale': _jnp.float32, 'conv_w_dw': _jnp.float32, 'conv_b_dw': _jnp.float32, 'conv_ln_g': _jnp.float32, 'conv_ln_b': _jnp.float32, 'conv_w_pw': _jnp.float32, 'grp_norm_g': _jnp.float32, 'w_out': _jnp.float32, 'norm_x_g': _jnp.float32, 'w_xq': _jnp.float32, 'w_xk': _jnp.float32, 'w_xv': _jnp.float32, 'w_xo': _jnp.float32, 'norm_mlp_g': _jnp.float32, 'w_up': _jnp.float32, 'w_down': _jnp.float32, 'norm_final_g': _jnp.float32}
MOMENT_SCALE = {'rel_bias': 6.952880e-02, 'mem_norm_g': 1.128086e-02, 'norm_mix_g': 4.553329e-02, 'w_in': 3.381071e-02, 's5_lam_re': 1.505261e-02, 's5_lam_im': 1.255113e-02, 's5_log_dt': 6.353655e+00, 's5_b_re': 8.528314e-03, 's5_b_im': 9.431437e-03, 's5_c_re': 2.756916e-03, 's5_c_im': 3.138197e-03, 's5_d': 4.358091e-02, 's5_w_glu': 1.244594e-02, 'pool_w': 3.443530e-02, 'pool_scale': 3.553494e-02, 'conv_w_dw': 3.721712e-02, 'conv_b_dw': 9.458170e-02, 'conv_ln_g': 4.821860e-02, 'conv_ln_b': 5.559717e-02, 'conv_w_pw': 4.047714e-02, 'grp_norm_g': 4.266292e-02, 'w_out': 4.259436e-02, 'norm_x_g': 3.503303e-03, 'w_xq': 7.107555e-03, 'w_xk': 7.108733e-03, 'w_xv': 8.154675e-03, 'w_xo': 4.099327e-03, 'norm_mlp_g': 3.976314e-02, 'w_up': 2.013798e-02, 'w_down': 4.385656e-02, 'norm_final_g': 8.428546e+00}


def _to_microbatches(a, axis):
    t = _jnp.moveaxis(a, axis, 0)
    t = t.reshape((N_MICROBATCH, t.shape[0] // N_MICROBATCH) + t.shape[1:])
    return _jnp.moveaxis(t, 1, axis + 1)


def setup_inputs(seed: int = 0) -> dict:
    inp = _fwd_setup_inputs(seed)
    key = _jax.random.fold_in(_jax.random.key(seed), 7919)
    shape, _ = _output_shape()
    out = dict(inp)
    out["loss_target"] = _jax.random.normal(_jax.random.fold_in(key, 0), shape, _jnp.float32)
    for i, name in enumerate(TWIN_WEIGHTS):
        w = inp[name].astype(_jnp.float32)
        if MOMENT_SCALE is None:
            s = _jnp.sqrt(_jnp.mean(_jnp.square(w)) + 1e-30)
        else:
            s = MOMENT_SCALE[name]
        km, kv = _jax.random.split(_jax.random.fold_in(key, i + 1))
        out[name] = w
        out["m_" + name] = s * _jax.random.normal(km, w.shape, _jnp.float32)
        out["v_" + name] = (s * s) * _jax.random.uniform(kv, w.shape, _jnp.float32, 0.5, 1.5)
    if N_MICROBATCH > 1:
        for name, axis in PER_EXAMPLE_BATCH_AXIS.items():
            out[name] = _to_microbatches(out[name], axis)
    return {'x': out['x'], 'mem': out['mem'], 'rel_bias': out['rel_bias'], 'mem_norm_g': out['mem_norm_g'], 'norm_mix_g': out['norm_mix_g'], 'w_in': out['w_in'], 's5_lam_re': out['s5_lam_re'], 's5_lam_im': out['s5_lam_im'], 's5_log_dt': out['s5_log_dt'], 's5_b_re': out['s5_b_re'], 's5_b_im': out['s5_b_im'], 's5_c_re': out['s5_c_re'], 's5_c_im': out['s5_c_im'], 's5_d': out['s5_d'], 's5_w_glu': out['s5_w_glu'], 'pool_w': out['pool_w'], 'pool_scale': out['pool_scale'], 'conv_w_dw': out['conv_w_dw'], 'conv_b_dw': out['conv_b_dw'], 'conv_ln_g': out['conv_ln_g'], 'conv_ln_b': out['conv_ln_b'], 'conv_w_pw': out['conv_w_pw'], 'grp_norm_g': out['grp_norm_g'], 'w_out': out['w_out'], 'norm_x_g': out['norm_x_g'], 'w_xq': out['w_xq'], 'w_xk': out['w_xk'], 'w_xv': out['w_xv'], 'w_xo': out['w_xo'], 'norm_mlp_g': out['norm_mlp_g'], 'w_up': out['w_up'], 'w_down': out['w_down'], 'norm_final_g': out['norm_final_g'], 'loss_target': out['loss_target'], 'm_rel_bias': out['m_rel_bias'], 'm_mem_norm_g': out['m_mem_norm_g'], 'm_norm_mix_g': out['m_norm_mix_g'], 'm_w_in': out['m_w_in'], 'm_s5_lam_re': out['m_s5_lam_re'], 'm_s5_lam_im': out['m_s5_lam_im'], 'm_s5_log_dt': out['m_s5_log_dt'], 'm_s5_b_re': out['m_s5_b_re'], 'm_s5_b_im': out['m_s5_b_im'], 'm_s5_c_re': out['m_s5_c_re'], 'm_s5_c_im': out['m_s5_c_im'], 'm_s5_d': out['m_s5_d'], 'm_s5_w_glu': out['m_s5_w_glu'], 'm_pool_w': out['m_pool_w'], 'm_pool_scale': out['m_pool_scale'], 'm_conv_w_dw': out['m_conv_w_dw'], 'm_conv_b_dw': out['m_conv_b_dw'], 'm_conv_ln_g': out['m_conv_ln_g'], 'm_conv_ln_b': out['m_conv_ln_b'], 'm_conv_w_pw': out['m_conv_w_pw'], 'm_grp_norm_g': out['m_grp_norm_g'], 'm_w_out': out['m_w_out'], 'm_norm_x_g': out['m_norm_x_g'], 'm_w_xq': out['m_w_xq'], 'm_w_xk': out['m_w_xk'], 'm_w_xv': out['m_w_xv'], 'm_w_xo': out['m_w_xo'], 'm_norm_mlp_g': out['m_norm_mlp_g'], 'm_w_up': out['m_w_up'], 'm_w_down': out['m_w_down'], 'm_norm_final_g': out['m_norm_final_g'], 'v_rel_bias': out['v_rel_bias'], 'v_mem_norm_g': out['v_mem_norm_g'], 'v_norm_mix_g': out['v_norm_mix_g'], 'v_w_in': out['v_w_in'], 'v_s5_lam_re': out['v_s5_lam_re'], 'v_s5_lam_im': out['v_s5_lam_im'], 'v_s5_log_dt': out['v_s5_log_dt'], 'v_s5_b_re': out['v_s5_b_re'], 'v_s5_b_im': out['v_s5_b_im'], 'v_s5_c_re': out['v_s5_c_re'], 'v_s5_c_im': out['v_s5_c_im'], 'v_s5_d': out['v_s5_d'], 'v_s5_w_glu': out['v_s5_w_glu'], 'v_pool_w': out['v_pool_w'], 'v_pool_scale': out['v_pool_scale'], 'v_conv_w_dw': out['v_conv_w_dw'], 'v_conv_b_dw': out['v_conv_b_dw'], 'v_conv_ln_g': out['v_conv_ln_g'], 'v_conv_ln_b': out['v_conv_ln_b'], 'v_conv_w_pw': out['v_conv_w_pw'], 'v_grp_norm_g': out['v_grp_norm_g'], 'v_w_out': out['v_w_out'], 'v_norm_x_g': out['v_norm_x_g'], 'v_w_xq': out['v_w_xq'], 'v_w_xk': out['v_w_xk'], 'v_w_xv': out['v_w_xv'], 'v_w_xo': out['v_w_xo'], 'v_norm_mlp_g': out['v_norm_mlp_g'], 'v_w_up': out['v_w_up'], 'v_w_down': out['v_w_down'], 'v_norm_final_g': out['v_norm_final_g']}


def _loss(weights, diff, rest, loss_target):
    with _jax.named_scope("forward"):
        args = {**rest, TWIN_DIFF_INPUT: diff, **{k: w.astype(_WEIGHT_DTYPES[k]) for k, w in weights.items()}}
        y = _forward(args)
    with _jax.named_scope("loss_head"):
        err = _jnp.square(y.astype(_jnp.float32) - loss_target)
        return 0.5 * _jnp.sum(_jnp.mean(err, axis=-1)) if err.ndim else 0.5 * err


def _adamw(w, g, m, v):
    m = ADAM_B1 * m + (1.0 - ADAM_B1) * g
    v = ADAM_B2 * v + (1.0 - ADAM_B2) * _jnp.square(g)
    m_hat = m / (1.0 - ADAM_B1 ** ADAM_STEP)
    v_hat = v / (1.0 - ADAM_B2 ** ADAM_STEP)
    delta = -ADAM_LR * (m_hat / (_jnp.sqrt(v_hat) + ADAM_EPS) + ADAM_WD * w)
    return delta, m, v


def reference(x, mem, rel_bias, mem_norm_g, norm_mix_g, w_in, s5_lam_re, s5_lam_im, s5_log_dt, s5_b_re, s5_b_im, s5_c_re, s5_c_im, s5_d, s5_w_glu, pool_w, pool_scale, conv_w_dw, conv_b_dw, conv_ln_g, conv_ln_b, conv_w_pw, grp_norm_g, w_out, norm_x_g, w_xq, w_xk, w_xv, w_xo, norm_mlp_g, w_up, w_down, norm_final_g, loss_target, m_rel_bias, m_mem_norm_g, m_norm_mix_g, m_w_in, m_s5_lam_re, m_s5_lam_im, m_s5_log_dt, m_s5_b_re, m_s5_b_im, m_s5_c_re, m_s5_c_im, m_s5_d, m_s5_w_glu, m_pool_w, m_pool_scale, m_conv_w_dw, m_conv_b_dw, m_conv_ln_g, m_conv_ln_b, m_conv_w_pw, m_grp_norm_g, m_w_out, m_norm_x_g, m_w_xq, m_w_xk, m_w_xv, m_w_xo, m_norm_mlp_g, m_w_up, m_w_down, m_norm_final_g, v_rel_bias, v_mem_norm_g, v_norm_mix_g, v_w_in, v_s5_lam_re, v_s5_lam_im, v_s5_log_dt, v_s5_b_re, v_s5_b_im, v_s5_c_re, v_s5_c_im, v_s5_d, v_s5_w_glu, v_pool_w, v_pool_scale, v_conv_w_dw, v_conv_b_dw, v_conv_ln_g, v_conv_ln_b, v_conv_w_pw, v_grp_norm_g, v_w_out, v_norm_x_g, v_w_xq, v_w_xk, v_w_xv, v_w_xo, v_norm_mlp_g, v_w_up, v_w_down, v_norm_final_g):
    given = dict(x=x, mem=mem, rel_bias=rel_bias, mem_norm_g=mem_norm_g, norm_mix_g=norm_mix_g, w_in=w_in, s5_lam_re=s5_lam_re, s5_lam_im=s5_lam_im, s5_log_dt=s5_log_dt, s5_b_re=s5_b_re, s5_b_im=s5_b_im, s5_c_re=s5_c_re, s5_c_im=s5_c_im, s5_d=s5_d, s5_w_glu=s5_w_glu, pool_w=pool_w, pool_scale=pool_scale, conv_w_dw=conv_w_dw, conv_b_dw=conv_b_dw, conv_ln_g=conv_ln_g, conv_ln_b=conv_ln_b, conv_w_pw=conv_w_pw, grp_norm_g=grp_norm_g, w_out=w_out, norm_x_g=norm_x_g, w_xq=w_xq, w_xk=w_xk, w_xv=w_xv, w_xo=w_xo, norm_mlp_g=norm_mlp_g, w_up=w_up, w_down=w_down, norm_final_g=norm_final_g, loss_target=loss_target, m_rel_bias=m_rel_bias, m_mem_norm_g=m_mem_norm_g, m_norm_mix_g=m_norm_mix_g, m_w_in=m_w_in, m_s5_lam_re=m_s5_lam_re, m_s5_lam_im=m_s5_lam_im, m_s5_log_dt=m_s5_log_dt, m_s5_b_re=m_s5_b_re, m_s5_b_im=m_s5_b_im, m_s5_c_re=m_s5_c_re, m_s5_c_im=m_s5_c_im, m_s5_d=m_s5_d, m_s5_w_glu=m_s5_w_glu, m_pool_w=m_pool_w, m_pool_scale=m_pool_scale, m_conv_w_dw=m_conv_w_dw, m_conv_b_dw=m_conv_b_dw, m_conv_ln_g=m_conv_ln_g, m_conv_ln_b=m_conv_ln_b, m_conv_w_pw=m_conv_w_pw, m_grp_norm_g=m_grp_norm_g, m_w_out=m_w_out, m_norm_x_g=m_norm_x_g, m_w_xq=m_w_xq, m_w_xk=m_w_xk, m_w_xv=m_w_xv, m_w_xo=m_w_xo, m_norm_mlp_g=m_norm_mlp_g, m_w_up=m_w_up, m_w_down=m_w_down, m_norm_final_g=m_norm_final_g, v_rel_bias=v_rel_bias, v_mem_norm_g=v_mem_norm_g, v_norm_mix_g=v_norm_mix_g, v_w_in=v_w_in, v_s5_lam_re=v_s5_lam_re, v_s5_lam_im=v_s5_lam_im, v_s5_log_dt=v_s5_log_dt, v_s5_b_re=v_s5_b_re, v_s5_b_im=v_s5_b_im, v_s5_c_re=v_s5_c_re, v_s5_c_im=v_s5_c_im, v_s5_d=v_s5_d, v_s5_w_glu=v_s5_w_glu, v_pool_w=v_pool_w, v_pool_scale=v_pool_scale, v_conv_w_dw=v_conv_w_dw, v_conv_b_dw=v_conv_b_dw, v_conv_ln_g=v_conv_ln_g, v_conv_ln_b=v_conv_ln_b, v_conv_w_pw=v_conv_w_pw, v_grp_norm_g=v_grp_norm_g, v_w_out=v_w_out, v_norm_x_g=v_norm_x_g, v_w_xq=v_w_xq, v_w_xk=v_w_xk, v_w_xv=v_w_xv, v_w_xo=v_w_xo, v_norm_mlp_g=v_norm_mlp_g, v_w_up=v_w_up, v_w_down=v_w_down, v_norm_final_g=v_norm_final_g)
    weights = {n: given[n] for n in TWIN_WEIGHTS}
    shared = {n: given[n] for n in SHARED_INPUTS}
    per_example = {n: given[n] for n in ['x', 'mem']}
    grad_fn = _jax.value_and_grad(_loss, argnums=(0, 1))

    def one_microbatch(ex, loss_target):
        ex = dict(ex)
        diff = ex.pop(TWIN_DIFF_INPUT)
        return grad_fn(weights, diff, {**shared, **ex}, loss_target)

    if N_MICROBATCH == 1:
        loss, (grad_w, grad_x) = one_microbatch(per_example, given["loss_target"])
    else:
        def body(carry, xs):
            loss_sum, grad_sum = carry
            l_k, (gw_k, gx_k) = one_microbatch(xs[0], xs[1])
            with _jax.named_scope("update"):
                return (loss_sum + l_k, _jax.tree.map(_jnp.add, grad_sum, gw_k)), gx_k

        init = (_jnp.zeros((), _jnp.float32), _jax.tree.map(_jnp.zeros_like, weights))
        (loss, grad_w), grad_x = _jax.lax.scan(body, init, (per_example, given["loss_target"]))
    with _jax.named_scope("update"):
        delta_w, new_m, new_v = {}, {}, {}
        for n in TWIN_WEIGHTS:
            delta_w[n], new_m[n], new_v[n] = _adamw(weights[n], grad_w[n], given["m_" + n], given["v_" + n])
    return (loss, grad_x, *[grad_w[n] for n in TWIN_WEIGHTS], *[delta_w[n] for n in TWIN_WEIGHTS],
            *[new_m[n] for n in TWIN_WEIGHTS], *[new_v[n] for n in TWIN_WEIGHTS])
```

```python
import functools
import math

import jax
import jax.numpy as jnp
import numpy as np
from jax import lax
from jax.experimental import pallas as pl
from jax.experimental.pallas import tpu as pltpu

F32 = jnp.float32
BF16 = jnp.bfloat16
MESH = pl.DeviceIdType.MESH

N_MIXERS = 4
S5_CH = 16
S5_STATE = 64
POOL_WINDOWS = (2, 4, 8, 16)
CONV_WIDTH = 31
CONV_PAD = 32
ATT_HEADS = 8
ATT_BLOCK = 128
DILATED_PATTERNS = ((128, 1), (512, 4), (2048, 16))
REL_BUCKETS = 32
REL_MAX_DIST = 2048
X_HEADS = 4
X_HEAD_DIM = 128
NORM_EPS = 1e-6
NEG_INF = -1e30
ADAM_LR, ADAM_B1, ADAM_B2, ADAM_EPS, ADAM_WD, ADAM_STEP = 0.001, 0.9, 0.999, 1e-08, 0.01, 10
N_CHIPS = 4
VMEM_LIMIT = 56 * 1024 * 1024


def _cparams(sem=None, vmem=None):
    return pltpu.CompilerParams(dimension_semantics=sem, vmem_limit_bytes=vmem)


def _tile(n, pref):
    if n <= pref:
        return n
    t = pref
    while t >= 128:
        if n % t == 0:
            return t
        t -= 128
    return n


def _spec(arr, tr, tc, rc):
    if arr.ndim == 2:
        return pl.BlockSpec((tr, tc), lambda *g: rc(*g))
    per = arr.shape[2] // tc
    assert arr.shape[2] % tc == 0

    def imap(*g):
        r, c = rc(*g)
        return (c // per, r, c % per)

    return pl.BlockSpec((None, tr, tc), imap)


def _lshape(arr):
    return arr.shape if arr.ndim == 2 else (arr.shape[1], arr.shape[0] * arr.shape[2])


def _mm(a, b, mode, *, name, out_dtype=F32, res=None, epi=None, aux=None, out_stack=None, tm=1024, tn=1024, tk=512):
    la, lb = _lshape(a), _lshape(b)
    if mode == "nn":
        (M, K), (K2, N) = la, lb
    elif mode == "nt":
        (M, K), (N, K2) = la, lb
    else:
        (K, M), (K2, N) = la, lb
    assert K == K2, (la, lb, mode)
    lim = {"m": M, "n": N // out_stack if out_stack else N, "k": K}
    stacked = [(a, "m" if mode == "tn" else "k"), (b, "k" if mode == "nt" else "n"), (res, "n"), (aux, "n")]
    for arr, dim in stacked:
        if arr is not None and arr.ndim == 3:
            lim[dim] = math.gcd(lim[dim], arr.shape[2])
    tm, tn, tk = _tile(lim["m"], tm), _tile(lim["n"], tn), _tile(lim["k"], tk)
    nk = K // tk
    grid = (M // tm, N // tn, nk)
    a_spec = _spec(a, tk, tm, lambda i, j, k: (k, i)) if mode == "tn" else _spec(a, tm, tk, lambda i, j, k: (i, k))
    b_spec = _spec(b, tn, tk, lambda i, j, k: (j, k)) if mode == "nt" else _spec(b, tk, tn, lambda i, j, k: (k, j))
    dims = {"nn": ((1,), (0,)), "nt": ((1,), (1,)), "tn": ((0,), (0,))}[mode]
    ins, in_specs = [a, b], [a_spec, b_spec]
    for extra in (res, aux):
        if extra is not None:
            ins.append(extra)
            in_specs.append(_spec(extra, tm, tn, lambda i, j, k: (i, j)))
    if out_stack:
        o_shape = jax.ShapeDtypeStruct((out_stack, M, N // out_stack), out_dtype)
    else:
        o_shape = jax.ShapeDtypeStruct((M, N), out_dtype)
    o_spec = _spec(o_shape, tm, tn, lambda i, j, k: (i, j))
    n_out = 2 if epi == "relu2" else 1
    has_res, has_aux = res is not None, aux is not None

    def body(*refs):
        a_ref, b_ref = refs[0], refs[1]
        pos = 2
        res_ref = aux_ref = None
        if has_res:
            res_ref = refs[pos]
            pos += 1
        if has_aux:
            aux_ref = refs[pos]
            pos += 1
        outs = refs[pos:pos + n_out]
        acc_ref = refs[pos + n_out]
        k = pl.program_id(2)

        @pl.when(k == 0)
        def _():
            acc_ref[...] = jnp.zeros_like(acc_ref)

        acc_ref[...] += lax.dot_general(a_ref[...].astype(BF16), b_ref[...].astype(BF16), (dims, ((), ())),
                                        preferred_element_type=F32)

        @pl.when(k == nk - 1)
        def _():
            o = acc_ref[...]
            if has_res:
                o = o + res_ref[...].astype(F32)
            if epi == "relu2":
                outs[0][...] = o.astype(outs[0].dtype)
                r = jnp.maximum(o, 0.0)
                outs[1][...] = (r * r).astype(outs[1].dtype)
            elif epi == "relu2_bwd":
                outs[0][...] = (o * (2.0 * jnp.maximum(aux_ref[...].astype(F32), 0.0))).astype(outs[0].dtype)
            else:
                outs[0][...] = o.astype(outs[0].dtype)

    out = pl.pallas_call(
        body, grid=grid, in_specs=in_specs,
        out_specs=[o_spec] * n_out if n_out > 1 else o_spec,
        out_shape=[o_shape] * n_out if n_out > 1 else o_shape,
        scratch_shapes=[pltpu.VMEM((tm, tn), F32)],
        compiler_params=_cparams(("parallel", "parallel", "arbitrary"), VMEM_LIMIT), name=name,
    )(*ins)
    return out


def _rms_fwd(x, g, *, name, out_dtype=BF16):
    L, D = x.shape
    tr = _tile(L, 256)

    def body(x_ref, g_ref, o_ref):
        xv = x_ref[...]
        r = lax.rsqrt(jnp.mean(xv * xv, axis=-1, keepdims=True) + NORM_EPS)
        o_ref[...] = (xv * r * g_ref[...]).astype(o_ref.dtype)

    return pl.pallas_call(
        body, grid=(L // tr,),
        in_specs=[pl.BlockSpec((tr, D), lambda i: (i, 0)), pl.BlockSpec((1, D), lambda i: (0, 0))],
        out_specs=pl.BlockSpec((tr, D), lambda i: (i, 0)),
        out_shape=jax.ShapeDtypeStruct((L, D), out_dtype),
        compiler_params=_cparams(("parallel",)), name=name)(x, g)


def _rms_bwd(x, g, dy, dres, *, name):
    L, D = x.shape
    tr = _tile(L, 256)
    has_res = dres is not None

    def body(*refs):
        x_ref, g_ref, dy_ref = refs[:3]
        dres_ref = refs[3] if has_res else None
        dx_ref, dg_ref = refs[-2:]
        xv = x_ref[...]
        dyv = dy_ref[...].astype(F32)
        r = lax.rsqrt(jnp.mean(xv * xv, axis=-1, keepdims=True) + NORM_EPS)
        xh = xv * r
        dyg = dyv * g_ref[...]
        dx = r * (dyg - xh * jnp.mean(dyg * xh, axis=-1, keepdims=True))
        if has_res:
            dx = dx + dres_ref[...]
        dx_ref[...] = dx

        @pl.when(pl.program_id(0) == 0)
        def _():
            dg_ref[...] = jnp.zeros_like(dg_ref)

        dg_ref[...] += jnp.sum(dyv * xh, axis=0, keepdims=True)

    row = pl.BlockSpec((tr, D), lambda i: (i, 0))
    vec = pl.BlockSpec((1, D), lambda i: (0, 0))
    ins = [x, g, dy] + ([dres] if has_res else [])
    return pl.pallas_call(
        body, grid=(L // tr,), in_specs=[row, vec, row] + ([row] if has_res else []),
        out_specs=[row, vec],
        out_shape=[jax.ShapeDtypeStruct((L, D), F32), jax.ShapeDtypeStruct((1, D), F32)],
        compiler_params=_cparams(("arbitrary",)), name=name)(*ins)


def _loss_head(h, g, target, *, name):
    L, D = h.shape
    tr = _tile(L, 256)

    def body(x_ref, g_ref, t_ref, loss_ref, dx_ref, dg_ref):
        xv = x_ref[...]
        r = lax.rsqrt(jnp.mean(xv * xv, axis=-1, keepdims=True) + NORM_EPS)
        xh = xv * r
        err = xh * g_ref[...] - t_ref[...]
        dyv = err * (1.0 / D)
        dyg = dyv * g_ref[...]
        dx_ref[...] = r * (dyg - xh * jnp.mean(dyg * xh, axis=-1, keepdims=True))

        @pl.when(pl.program_id(0) == 0)
        def _():
            dg_ref[...] = jnp.zeros_like(dg_ref)
            loss_ref[...] = jnp.zeros_like(loss_ref)

        dg_ref[...] += jnp.sum(dyv * xh, axis=0, keepdims=True)
        loss_ref[...] += 0.5 * jnp.sum(jnp.sum(err * err, axis=1, keepdims=True), axis=0, keepdims=True) * (1.0 / D)

    row = pl.BlockSpec((tr, D), lambda i: (i, 0))
    vec = pl.BlockSpec((1, D), lambda i: (0, 0))
    return pl.pallas_call(
        body, grid=(L // tr,), in_specs=[row, vec, row],
        out_specs=[pl.BlockSpec((1, 1), lambda i: (0, 0)), row, vec],
        out_shape=[jax.ShapeDtypeStruct((1, 1), F32), jax.ShapeDtypeStruct((L, D), F32), jax.ShapeDtypeStruct((1, D), F32)],
        compiler_params=_cparams(("arbitrary",)), name=name)(h, g, target)


S5_TL = 256
S5_LW = 512


def _dot(a, b, dims):
    return lax.dot_general(a, b, (dims, ((), ())), preferred_element_type=F32)


_NN, _NT, _TN = ((1,), (0,)), ((1,), (1,)), ((0,), (0,))


def _gelu_and_grad(y):
    k = math.sqrt(2.0 / math.pi)
    y2 = y * y
    th = jnp.tanh(k * (y + 0.044715 * y * y2))
    g = 0.5 * y * (1.0 + th)
    gp = 0.5 * (1.0 + th) + 0.5 * y * (1.0 - th * th) * k * (1.0 + 3.0 * 0.044715 * y2)
    return g, gp


def _scan_tiles(re_s, im_s, ls, lw, pw_ref, tr_ref, ti_ref, carry_s, nt, reverse, extra=None):
    row = lax.broadcasted_iota(jnp.int32, (8, lw), 0)
    a = [pw_ref[k:k + 1, ls] for k in range(6)]
    tr_t, ti_t = tr_ref[:, ls], ti_ref[:, ls]
    edge = 0 if reverse else 7

    def tile(jj, carry):
        cr, ci, acc = carry
        j = (nt - 1 - jj) if reverse else jj
        off = pl.multiple_of(j * 8, 8)
        sr, si = re_s[pl.ds(off, 8), ls], im_s[pl.ds(off, 8), ls]
        for n, k in enumerate((1, 2, 4)):
            akr, aki = a[2 * n], a[2 * n + 1]
            if reverse:
                keep, sh = row < 8 - k, 8 - k
            else:
                keep, sh = row >= k, k
            shr = jnp.where(keep, pltpu.roll(sr, sh, 0), 0.0)
            shi = jnp.where(keep, pltpu.roll(si, sh, 0), 0.0)
            sr, si = sr + akr * shr - aki * shi, si + akr * shi + aki * shr
        xr = sr + tr_t * cr - ti_t * ci
        xi = si + tr_t * ci + ti_t * cr
        re_s[pl.ds(off, 8), ls] = xr
        im_s[pl.ds(off, 8), ls] = xi
        if extra is not None:
            acc = extra(off, xr, xi, acc, row)
        return xr[edge:edge + 1, :], xi[edge:edge + 1, :], acc

    acc0 = (jnp.zeros((8, lw), F32), jnp.zeros((8, lw), F32)) if extra is not None else 0
    cr, ci, acc = lax.fori_loop(0, nt, tile, (carry_s[0:1, ls], carry_s[1:2, ls], acc0))
    carry_s[0:1, ls] = cr
    carry_s[1:2, ls] = ci
    return acc


def _s5_fwd(proj, bdr, bdi, pw, tr, ti, cdr, cdi, dskip, wglu, *, name):
    L = proj.shape[0]
    GW, S = bdr.shape
    TL = _tile(L, S5_TL)
    lw = min(S, S5_LW)

    def body(u_ref, bdr_ref, bdi_ref, pw_ref, tr_ref, ti_ref, cdr_ref, cdi_ref, d_ref, w_ref,
             y_ref, xr_ref, xi_ref, re_s, im_s, carry_s):
        @pl.when(pl.program_id(0) == 0)
        def _():
            carry_s[...] = jnp.zeros_like(carry_s)

        u = u_ref[...]
        ub = u.astype(BF16)
        re_s[...] = _dot(ub, bdr_ref[...], _NN)
        im_s[...] = _dot(ub, bdi_ref[...], _NN)
        for lc in range(S // lw):
            _scan_tiles(re_s, im_s, slice(lc * lw, (lc + 1) * lw), lw, pw_ref, tr_ref, ti_ref, carry_s, TL // 8, False)
        xrb, xib = re_s[...].astype(BF16), im_s[...].astype(BF16)
        xr_ref[...] = xrb
        xi_ref[...] = xib
        y = _dot(xrb, cdr_ref[...], _NN) - _dot(xib, cdi_ref[...], _NN) + d_ref[...] * u
        g, _ = _gelu_and_grad(y)
        z = _dot(g.astype(BF16), w_ref[...], _NN)
        y_ref[...] = g * jax.nn.sigmoid(z)

    full = lambda arr: pl.BlockSpec(arr.shape, lambda i: (0,) * arr.ndim)
    return pl.pallas_call(
        body, grid=(L // TL,),
        in_specs=[pl.BlockSpec((TL, GW), lambda i: (i, 0))] + [full(t) for t in (bdr, bdi, pw, tr, ti, cdr, cdi, dskip, wglu)],
        out_specs=[pl.BlockSpec((TL, GW), lambda i: (i, 0)), pl.BlockSpec((TL, S), lambda i: (i, 0)), pl.BlockSpec((TL, S), lambda i: (i, 0))],
        out_shape=[jax.ShapeDtypeStruct((L, GW), F32), jax.ShapeDtypeStruct((L, S), BF16), jax.ShapeDtypeStruct((L, S), BF16)],
        scratch_shapes=[pltpu.VMEM((TL, S), F32), pltpu.VMEM((TL, S), F32), pltpu.VMEM((8, S), F32)],
        compiler_params=_cparams(("arbitrary",), VMEM_LIMIT), name=name,
    )(proj, bdr, bdi, pw, tr, ti, cdr, cdi, dskip, wglu)


def _s5_bwd(proj, xr, xi, dout, bdr, bdi, pwc, trc, tic, cdr, cdi, dskip, wglu, *, name):
    L = proj.shape[0]
    GW, S = bdr.shape
    TL = _tile(L, S5_TL)
    nc = L // TL
    lw = min(S, S5_LW)

    def body(u_ref, xr_ref, xi_ref, xrp_ref, xip_ref, do_ref, bdr_ref, bdi_ref, pw_ref, tr_ref, ti_ref, cdr_ref, cdi_ref,
             d_ref, w_ref, du_ref, dw_ref, dd_ref, dcdr_ref, dcdi_ref, dbdr_ref, dbdi_ref, dar_ref, dai_ref,
             gr_s, gi_s, xfr, xfi, carry_s):
        i = pl.program_id(0)

        @pl.when(i == 0)
        def _():
            carry_s[...] = jnp.zeros_like(carry_s)
            for r in (dw_ref, dd_ref, dcdr_ref, dcdi_ref, dbdr_ref, dbdi_ref, dar_ref, dai_ref):
                r[...] = jnp.zeros_like(r)

        u = u_ref[...]
        ub = u.astype(BF16)
        xrb, xib = xr_ref[...], xi_ref[...]
        y = _dot(xrb, cdr_ref[...], _NN) - _dot(xib, cdi_ref[...], _NN) + d_ref[...] * u
        g, gp = _gelu_and_grad(y)
        gb = g.astype(BF16)
        sg = jax.nn.sigmoid(_dot(gb, w_ref[...], _NN))
        dout = do_ref[...]
        dz = (dout * g * sg * (1.0 - sg)).astype(BF16)
        dg = dout * sg + _dot(dz, w_ref[...], _NT)
        dw_ref[...] += _dot(gb, dz, _TN)
        dy = dg * gp
        dyb = dy.astype(BF16)
        dd_ref[...] += jnp.sum(dy * u, axis=0, keepdims=True)
        gr_s[...] = _dot(dyb, cdr_ref[...], _NT)
        gi_s[...] = -_dot(dyb, cdi_ref[...], _NT)
        dcdr_ref[...] += _dot(xrb, dyb, _TN)
        dcdi_ref[...] -= _dot(xib, dyb, _TN)
        live = (i < nc - 1).astype(F32)
        xfr[0:8, :] = xrp_ref[...].astype(F32) * live
        xfi[0:8, :] = xip_ref[...].astype(F32) * live
        xfr[8:, :] = xrb.astype(F32)
        xfi[8:, :] = xib.astype(F32)
        for lc in range(S // lw):
            ls = slice(lc * lw, (lc + 1) * lw)

            def extra(off, lr, li, acc, row, ls=ls):
                cur_r, cur_i = xfr[pl.ds(off + 8, 8), ls], xfi[pl.ds(off + 8, 8), ls]
                prv_r, prv_i = xfr[pl.ds(off, 8), ls], xfi[pl.ds(off, 8), ls]
                xpr = jnp.where(row >= 1, pltpu.roll(cur_r, 1, 0), prv_r[7:8, :])
                xpi = jnp.where(row >= 1, pltpu.roll(cur_i, 1, 0), prv_i[7:8, :])
                return acc[0] + lr * xpr + li * xpi, acc[1] - lr * xpi + li * xpr

            acc = _scan_tiles(gr_s, gi_s, ls, lw, pw_ref, tr_ref, ti_ref, carry_s, TL // 8, True, extra)
            dar_ref[:, ls] += acc[0]
            dai_ref[:, ls] += acc[1]
        lrb, lib = gr_s[...].astype(BF16), gi_s[...].astype(BF16)
        du_ref[...] = dy * d_ref[...] + _dot(lrb, bdr_ref[...], _NT) + _dot(lib, bdi_ref[...], _NT)
        dbdr_ref[...] += _dot(ub, lrb, _TN)
        dbdi_ref[...] += _dot(ub, lib, _TN)

    rev = lambda i: nc - 1 - i
    full = lambda arr: pl.BlockSpec(arr.shape, lambda i: (0,) * arr.ndim)
    chunk = lambda w: pl.BlockSpec((TL, w), lambda i: (rev(i), 0))
    prev8 = pl.BlockSpec((8, S), lambda i: (jnp.maximum(rev(i) * (TL // 8) - 1, 0), 0))
    acc_shapes = [(GW, GW), (1, GW), (S, GW), (S, GW), (GW, S), (GW, S), (8, S), (8, S)]
    return pl.pallas_call(
        body, grid=(nc,),
        in_specs=[chunk(GW), chunk(S), chunk(S), prev8, prev8, chunk(GW)] + [full(t) for t in (bdr, bdi, pwc, trc, tic, cdr, cdi, dskip, wglu)],
        out_specs=[chunk(GW)] + [pl.BlockSpec(s, lambda i: (0, 0)) for s in acc_shapes],
        out_shape=[jax.ShapeDtypeStruct((L, GW), F32)] + [jax.ShapeDtypeStruct(s, F32) for s in acc_shapes],
        scratch_shapes=[pltpu.VMEM((TL, S), F32), pltpu.VMEM((TL, S), F32), pltpu.VMEM((TL + 8, S), F32),
                        pltpu.VMEM((TL + 8, S), F32), pltpu.VMEM((8, S), F32)],
        compiler_params=_cparams(("arbitrary",), VMEM_LIMIT), name=name,
    )(proj, xr, xi, xr, xi, dout, bdr, bdi, pwc, trc, tic, cdr, cdi, dskip, wglu)


def _cpow_tables(ar, ai):
    def cm(a, b):
        return a[0] * b[0] - a[1] * b[1], a[0] * b[1] + a[1] * b[0]
    p = [(ar, ai)]
    for _ in range(7):
        p.append(cm(p[-1], p[0]))
    z = jnp.zeros_like(ar)
    pw = jnp.stack([p[0][0], p[0][1], p[1][0], p[1][1], p[3][0], p[3][1], z, z])
    return pw, jnp.stack([q[0] for q in p]), jnp.stack([q[1] for q in p])


def _s5_disc(lam_re, lam_im, log_dt, b_re, b_im):
    dt = jnp.exp(log_dt)[:, None]
    mag = jnp.exp(lam_re * dt)
    ab_r, ab_i = mag * jnp.cos(lam_im * dt), mag * jnp.sin(lam_im * dt)
    den = lam_re * lam_re + lam_im * lam_im
    nr, ni = ab_r - 1.0, ab_i
    f_r = ((nr * lam_re + ni * lam_im) / den)[..., None]
    f_i = ((ni * lam_re - nr * lam_im) / den)[..., None]
    return ab_r, ab_i, f_r * b_re - f_i * b_im, f_r * b_im + f_i * b_re


def _bd(t):
    G, A, B = t.shape
    return (t[:, :, None, :] * jnp.eye(G, dtype=t.dtype)[:, None, :, None]).reshape(G * A, G * B)


def _bd_extract(m, G):
    A, B = m.shape[0] // G, m.shape[1] // G
    idx = jnp.arange(G)
    return m.reshape(G, A, G, B)[idx, :, idx, :]


POOL_TQ = 256


def _band(shape, row0, col0, win, transpose):
    r = lax.broadcasted_iota(jnp.int32, shape, 0) + row0
    c = lax.broadcasted_iota(jnp.int32, shape, 1) + col0
    d = (c - r) if transpose else (r - c)
    return ((d >= 0) & (d < win)).astype(BF16)


def _band_dot(band, v):
    hi = v.astype(BF16)
    lo = (v - hi.astype(F32)).astype(BF16)
    return _dot(band, hi, _NN) + _dot(band, lo, _NN)


def _pool_p(u_prev, u_cur, i, win, tq):
    t0 = i * tq
    cnt = jnp.minimum(lax.broadcasted_iota(jnp.int32, (tq, 1), 0) + t0 + 1, win).astype(F32)
    live = (i > 0).astype(F32)
    acc = _band_dot(_band((tq, tq), t0, t0, win, False), u_cur)
    acc += _band_dot(_band((tq, tq), t0, t0 - tq, win, False), u_prev * live)
    return acc / cnt - u_cur


def _pool_fwd(proj, pool_w, pool_scale3, *, name):
    L = proj.shape[0]
    nw, PC, _ = pool_w.shape
    tq = _tile(L, POOL_TQ)

    def body(up_ref, uc_ref, w_ref, s_ref, y_ref):
        g, i = pl.program_id(0), pl.program_id(1)
        win = jnp.left_shift(2, g)
        p = _pool_p(up_ref[...], uc_ref[...], i, win, tq)
        y_ref[...] = _dot(p.astype(BF16), w_ref[...].astype(BF16), _NN) * s_ref[...]

    return pl.pallas_call(
        body, grid=(nw, L // tq),
        in_specs=[pl.BlockSpec((tq, PC), lambda g, i: (jnp.maximum(i - 1, 0), nw + g)),
                  pl.BlockSpec((tq, PC), lambda g, i: (i, nw + g)),
                  pl.BlockSpec((None, PC, PC), lambda g, i: (g, 0, 0)),
                  pl.BlockSpec((None, 1, PC), lambda g, i: (g, 0, 0))],
        out_specs=pl.BlockSpec((tq, PC), lambda g, i: (i, g)),
        out_shape=jax.ShapeDtypeStruct((L, nw * PC), F32),
        compiler_params=_cparams(("parallel", "parallel")), name=name)(proj, proj, pool_w, pool_scale3)


def _pool_bwd(proj, dy, pool_w, pool_scale3, *, name):
    L = proj.shape[0]
    nw, PC, _ = pool_w.shape
    tq = _tile(L, POOL_TQ)
    nq = L // tq

    def body(up_ref, uc_ref, dyc_ref, dyn_ref, w_ref, s_ref, du_ref, dw_ref, ds_ref):
        g, i = pl.program_id(0), pl.program_id(1)
        win = jnp.left_shift(2, g)

        @pl.when(i == 0)
        def _():
            dw_ref[...] = jnp.zeros_like(dw_ref)
            ds_ref[...] = jnp.zeros_like(ds_ref)

        wb = w_ref[...].astype(BF16)
        p = _pool_p(up_ref[...], uc_ref[...], i, win, tq).astype(BF16)
        dyc = dyc_ref[...]
        ds_ref[...] += jnp.sum(dyc * _dot(p, wb, _NN), axis=0, keepdims=True)
        dys = (dyc * s_ref[...]).astype(BF16)
        dw_ref[...] += _dot(p, dys, _TN)
        t0 = i * tq
        rows = lax.broadcasted_iota(jnp.int32, (tq, 1), 0)
        dp_c = _dot(dys, wb, _NT)
        q_c = dp_c / jnp.minimum(rows + t0 + 1, win).astype(F32)
        live = (i < nq - 1).astype(F32)
        dp_n = _dot((dyn_ref[...] * s_ref[...] * live).astype(BF16), wb, _NT)
        q_n = dp_n / jnp.minimum(rows + t0 + tq + 1, win).astype(F32)
        du = _band_dot(_band((tq, tq), t0, t0, win, True), q_c) + _band_dot(_band((tq, tq), t0, t0 + tq, win, True), q_n)
        du_ref[...] = du - dp_c

    return pl.pallas_call(
        body, grid=(nw, nq),
        in_specs=[pl.BlockSpec((tq, PC), lambda g, i: (jnp.maximum(i - 1, 0), nw + g)),
                  pl.BlockSpec((tq, PC), lambda g, i: (i, nw + g)),
                  pl.BlockSpec((tq, PC), lambda g, i: (i, g)),
                  pl.BlockSpec((tq, PC), lambda g, i: (jnp.minimum(i + 1, nq - 1), g)),
                  pl.BlockSpec((None, PC, PC), lambda g, i: (g, 0, 0)),
                  pl.BlockSpec((None, 1, PC), lambda g, i: (g, 0, 0))],
        out_specs=[pl.BlockSpec((tq, PC), lambda g, i: (i, g)),
                   pl.BlockSpec((None, PC, PC), lambda g, i: (g, 0, 0)),
                   pl.BlockSpec((None, 1, PC), lambda g, i: (g, 0, 0))],
        out_shape=[jax.ShapeDtypeStruct((L, nw * PC), F32), jax.ShapeDtypeStruct((nw, PC, PC), F32),
                   jax.ShapeDtypeStruct((nw, 1, PC), F32)],
        compiler_params=_cparams(("parallel", "arbitrary")), name=name)(proj, proj, dy, dy, pool_w, pool_scale3)


CONV_RC = 256


def _conv1_fwd(proj, w_dw, b_dw, *, name):
    L = proj.shape[0]
    GW = w_dw.shape[1]
    CB = min(GW, 128)
    nb = GW // CB
    RC = _tile(L, CONV_RC)
    n = RC + CONV_PAD

    def body(val_ref, gate_ref, w_ref, b_ref, o_ref, hp):
        hp[0:CONV_PAD, :] = jnp.zeros((CONV_PAD, CB), F32)
        hp[CONV_PAD:, :] = val_ref[...] * jax.nn.sigmoid(gate_ref[...])
        wv = w_ref[...]

        def chunk(ci, carry):
            c0 = pl.multiple_of(ci * RC, 8)
            win = hp[pl.ds(c0, n), :]
            acc = jnp.zeros((RC, CB), F32) + b_ref[...]
            for k in range(CONV_WIDTH):
                acc = acc + wv[k:k + 1, :] * pltpu.roll(win, n - (k + 2), 0)[0:RC]
            o_ref[pl.ds(c0, RC), :] = acc
            return carry

        lax.fori_loop(0, L // RC, chunk, 0)

    col = lambda off: pl.BlockSpec((L, CB), lambda c: (0, off * nb + c))
    return pl.pallas_call(
        body, grid=(nb,),
        in_specs=[col(2), col(3), pl.BlockSpec((CONV_PAD, CB), lambda c: (0, c)), pl.BlockSpec((1, CB), lambda c: (0, c))],
        out_specs=pl.BlockSpec((L, CB), lambda c: (0, c)),
        out_shape=jax.ShapeDtypeStruct((L, GW), F32),
        scratch_shapes=[pltpu.VMEM((L + CONV_PAD, CB), F32)],
        compiler_params=_cparams(("parallel",)), name=name)(proj, proj, w_dw, b_dw)


def _conv1_bwd(proj, dhc, w_dw, *, name):
    L = proj.shape[0]
    GW = w_dw.shape[1]
    CB = min(GW, 128)
    nb = GW // CB
    RC = _tile(L, CONV_RC)
    n = RC + CONV_PAD

    def body(val_ref, gate_ref, d_ref, w_ref, dval_ref, dgate_ref, dw_ref, db_ref, hp, dp):
        val = val_ref[...]
        sg = jax.nn.sigmoid(gate_ref[...])
        hp[0:CONV_PAD, :] = jnp.zeros((CONV_PAD, CB), F32)
        hp[CONV_PAD:, :] = val * sg
        dp[0:L, :] = d_ref[...]
        dp[L:, :] = jnp.zeros((CONV_PAD, CB), F32)
        dw_ref[...] = jnp.zeros_like(dw_ref)
        db_ref[...] = jnp.sum(d_ref[...], axis=0, keepdims=True)
        wv = w_ref[...]

        def chunk(ci, carry):
            c0 = pl.multiple_of(ci * RC, 8)
            win = hp[pl.ds(c0, n), :]
            dwin = dp[pl.ds(c0, n), :]
            d = dwin[0:RC]
            dh = jnp.zeros((RC, CB), F32)
            for k in range(CONV_WIDTH):
                dw_ref[k:k + 1, :] += jnp.sum(d * pltpu.roll(win, n - (k + 2), 0)[0:RC], axis=0, keepdims=True)
                sh = CONV_WIDTH - 1 - k
                dh = dh + wv[k:k + 1, :] * (pltpu.roll(dwin, n - sh, 0)[0:RC] if sh else d)
            v, s = val_ref[pl.ds(c0, RC), :], jax.nn.sigmoid(gate_ref[pl.ds(c0, RC), :])
            dval_ref[pl.ds(c0, RC), :] = dh * s
            dgate_ref[pl.ds(c0, RC), :] = dh * v * s * (1.0 - s)
            return carry

        lax.fori_loop(0, L // RC, chunk, 0)

    col = lambda off: pl.BlockSpec((L, CB), lambda c: (0, off * nb + c))
    own = pl.BlockSpec((L, CB), lambda c: (0, c))
    return pl.pallas_call(
        body, grid=(nb,),
        in_specs=[col(2), col(3), own, pl.BlockSpec((CONV_PAD, CB), lambda c: (0, c))],
        out_specs=[own, own, pl.BlockSpec((CONV_PAD, CB), lambda c: (0, c)), pl.BlockSpec((1, CB), lambda c: (0, c))],
        out_shape=[jax.ShapeDtypeStruct((L, GW), F32), jax.ShapeDtypeStruct((L, GW), F32),
                   jax.ShapeDtypeStruct((CONV_PAD, GW), F32), jax.ShapeDtypeStruct((1, GW), F32)],
        scratch_shapes=[pltpu.VMEM((L + CONV_PAD, CB), F32), pltpu.VMEM((L + CONV_PAD, CB), F32)],
        compiler_params=_cparams(("parallel",)), name=name)(proj, proj, dhc, w_dw)


def _ln_silu(hc, g, b):
    mu = jnp.mean(hc, axis=-1, keepdims=True)
    xc = hc - mu
    r = lax.rsqrt(jnp.mean(xc * xc, axis=-1, keepdims=True) + NORM_EPS)
    xh = xc * r
    a = xh * g + b
    sg = jax.nn.sigmoid(a)
    return xh, r, a, sg


def _conv2_fwd(hc, ln_g, ln_b, w_pw, *, name):
    L, GW = hc.shape
    tr = _tile(L, 256)

    def body(h_ref, g_ref, b_ref, w_ref, y_ref):
        _, _, a, sg = _ln_silu(h_ref[...], g_ref[...], b_ref[...])
        y_ref[...] = _dot((a * sg).astype(BF16), w_ref[...], _NN)

    row = pl.BlockSpec((tr, GW), lambda i: (i, 0))
    vec = pl.BlockSpec((1, GW), lambda i: (0, 0))
    return pl.pallas_call(
        body, grid=(L // tr,), in_specs=[row, vec, vec, pl.BlockSpec((GW, GW), lambda i: (0, 0))],
        out_specs=row, out_shape=jax.ShapeDtypeStruct((L, GW), F32),
        compiler_params=_cparams(("parallel",)), name=name)(hc, ln_g, ln_b, w_pw)


def _conv2_bwd(hc, ln_g, ln_b, w_pw, dy, *, name):
    L, GW = hc.shape
    tr = _tile(L, 256)

    def body(h_ref, g_ref, b_ref, w_ref, dy_ref, dh_ref, dg_ref, db_ref, dw_ref):
        @pl.when(pl.program_id(0) == 0)
        def _():
            for r in (dg_ref, db_ref, dw_ref):
                r[...] = jnp.zeros_like(r)

        xh, r, a, sg = _ln_silu(h_ref[...], g_ref[...], b_ref[...])
        dyb = dy_ref[...].astype(BF16)
        dw_ref[...] += _dot((a * sg).astype(BF16), dyb, _TN)
        da = _dot(dyb, w_ref[...], _NT) * (sg + a * sg * (1.0 - sg))
        dg_ref[...] += jnp.sum(da * xh, axis=0, keepdims=True)
        db_ref[...] += jnp.sum(da, axis=0, keepdims=True)
        dxh = da * g_ref[...]
        dh_ref[...] = r * (dxh - jnp.mean(dxh, axis=-1, keepdims=True) - xh * jnp.mean(dxh * xh, axis=-1, keepdims=True))

    row = pl.BlockSpec((tr, GW), lambda i: (i, 0))
    vec = pl.BlockSpec((1, GW), lambda i: (0, 0))
    mat = pl.BlockSpec((GW, GW), lambda i: (0, 0))
    return pl.pallas_call(
        body, grid=(L // tr,), in_specs=[row, vec, vec, mat, row],
        out_specs=[row, vec, vec, mat],
        out_shape=[jax.ShapeDtypeStruct((L, GW), F32), jax.ShapeDtypeStruct((1, GW), F32), jax.ShapeDtypeStruct((1, GW), F32),
                   jax.ShapeDtypeStruct((GW, GW), F32)],
        compiler_params=_cparams(("arbitrary",)), name=name)(hc, ln_g, ln_b, w_pw, dy)


def _grp_fwd(ys, g, *, name):
    L, GW = ys[0].shape
    tr = _tile(L, 256)

    def body(*refs):
        g_ref, o_ref = refs[4], refs[5]
        for m in range(N_MIXERS):
            yv = refs[m][...]
            r = lax.rsqrt(jnp.mean(yv * yv, axis=-1, keepdims=True) + NORM_EPS)
            o_ref[:, m * GW:(m + 1) * GW] = (yv * r * g_ref[:, m * GW:(m + 1) * GW]).astype(o_ref.dtype)

    row = pl.BlockSpec((tr, GW), lambda i: (i, 0))
    return pl.pallas_call(
        body, grid=(L // tr,), in_specs=[row] * 4 + [pl.BlockSpec((1, N_MIXERS * GW), lambda i: (0, 0))],
        out_specs=pl.BlockSpec((tr, N_MIXERS * GW), lambda i: (i, 0)),
        out_shape=jax.ShapeDtypeStruct((L, N_MIXERS * GW), BF16),
        compiler_params=_cparams(("parallel",)), name=name)(*ys, g)


def _grp_bwd(ys, g, dy, *, name):
    L, GW = ys[0].shape
    tr = _tile(L, 256)

    def body(*refs):
        g_ref, dy_ref = refs[4], refs[5]
        outs, dg_ref = refs[6:10], refs[10]

        @pl.when(pl.program_id(0) == 0)
        def _():
            dg_ref[...] = jnp.zeros_like(dg_ref)

        for m in range(N_MIXERS):
            cs = slice(m * GW, (m + 1) * GW)
            yv = refs[m][...]
            r = lax.rsqrt(jnp.mean(yv * yv, axis=-1, keepdims=True) + NORM_EPS)
            xh = yv * r
            dyv = dy_ref[:, cs]
            dyg = dyv * g_ref[:, cs]
            outs[m][...] = r * (dyg - xh * jnp.mean(dyg * xh, axis=-1, keepdims=True))
            dg_ref[:, cs] += jnp.sum(dyv * xh, axis=0, keepdims=True)

    row = pl.BlockSpec((tr, GW), lambda i: (i, 0))
    wide = pl.BlockSpec((tr, N_MIXERS * GW), lambda i: (i, 0))
    vec = pl.BlockSpec((1, N_MIXERS * GW), lambda i: (0, 0))
    return pl.pallas_call(
        body, grid=(L // tr,), in_specs=[row] * 4 + [vec, wide],
        out_specs=[row] * 4 + [vec],
        out_shape=[jax.ShapeDtypeStruct((L, GW), F32)] * 4 + [jax.ShapeDtypeStruct((1, N_MIXERS * GW), F32)],
        compiler_params=_cparams(("arbitrary",)), name=name)(*ys, g, dy)


def _att_first(j, br, nblk):
    per = lax.shift_right_logical(jnp.int32(nblk), 2 * br)
    return jnp.bitwise_and(j, per - 1) == 0


def _att_fwd(q, k, vx, bias, *, name):
    nbr, H, L, E = q.shape
    B = ATT_BLOCK
    nblk = L // B

    def body(q_ref, kp_ref, kc_ref, vp_ref, vc_ref, b_ref, o_ref):
        br, j = pl.program_id(0), pl.program_id(2)
        qv = q_ref[...]
        s_p = _dot(qv, kp_ref[...], _NT) + b_ref[:, 0:B]
        s_c = _dot(qv, kc_ref[...], _NT) + b_ref[:, B:2 * B]
        s_p = jnp.where(_att_first(j, br, nblk), NEG_INF, s_p)
        m = jnp.maximum(jnp.max(s_p, axis=-1, keepdims=True), jnp.max(s_c, axis=-1, keepdims=True))
        p_p, p_c = jnp.exp(s_p - m), jnp.exp(s_c - m)
        den = jnp.sum(p_p, axis=-1, keepdims=True) + jnp.sum(p_c, axis=-1, keepdims=True)
        acc = _dot(p_p.astype(BF16), vp_ref[...], _NN) + _dot(p_c.astype(BF16), vc_ref[...], _NN)
        lane = lax.broadcasted_iota(jnp.int32, (B, 2 * E), 1)
        o_ref[...] = jnp.where(lane < E, acc / den, m + jnp.log(den))

    cur = lambda w: pl.BlockSpec((None, None, B, w), lambda b, h, j: (b, h, j, 0))
    prev = lambda w: pl.BlockSpec((None, None, B, w), lambda b, h, j: (b, h, jnp.maximum(j - 1, 0), 0))
    return pl.pallas_call(
        body, grid=(nbr, H, nblk),
        in_specs=[cur(E), prev(E), cur(E), prev(2 * E), cur(2 * E),
                  pl.BlockSpec((None, None, B, 2 * B), lambda b, h, j: (b, h, 0, 0))],
        out_specs=cur(2 * E), out_shape=jax.ShapeDtypeStruct((nbr, H, L, 2 * E), F32),
        compiler_params=_cparams(("parallel", "parallel", "parallel")), name=name)(q, k, k, vx, vx, bias)


def _att_bwd(q, k, vx, bias, ox, dox, *, name):
    nbr, H, L, E = q.shape
    B = ATT_BLOCK
    nblk = L // B

    def body(q_ref, k_ref, v_ref, b_ref, o_ref, do_ref, dq_ref, dk_ref, dv_ref, db_ref):
        br = pl.program_id(0)
        dk_ref[...] = jnp.zeros_like(dk_ref)
        dv_ref[...] = jnp.zeros_like(dv_ref)
        db_ref[...] = jnp.zeros_like(db_ref)
        lane = lax.broadcasted_iota(jnp.int32, (B, 2 * E), 1)

        def blk(j, carry):
            off = pl.multiple_of(j * B, B)
            poff = pl.multiple_of(jnp.maximum(j - 1, 0) * B, B)
            qv, kc, kp = q_ref[pl.ds(off, B), :], k_ref[pl.ds(off, B), :], k_ref[pl.ds(poff, B), :]
            vc, vp = v_ref[pl.ds(off, B), :], v_ref[pl.ds(poff, B), :]
            oe, de = o_ref[pl.ds(off, B), :], do_ref[pl.ds(off, B), :]
            lse, dlse = oe[:, E:E + 1], de[:, E:E + 1]
            do = jnp.where(lane < E, de, 0.0)
            dm = jnp.sum(do * oe, axis=-1, keepdims=True) - dlse
            dob = do.astype(BF16)
            s_p = _dot(qv, kp, _NT) + b_ref[:, 0:B]
            s_c = _dot(qv, kc, _NT) + b_ref[:, B:2 * B]
            s_p = jnp.where(_att_first(j, br, nblk), NEG_INF, s_p)
            p_p, p_c = jnp.exp(s_p - lse), jnp.exp(s_c - lse)
            ds_p = p_p * (_dot(dob, vp, _NT) - dm)
            ds_c = p_c * (_dot(dob, vc, _NT) - dm)
            db_ref[:, 0:B] += ds_p
            db_ref[:, B:2 * B] += ds_c
            dsb_p, dsb_c = ds_p.astype(BF16), ds_c.astype(BF16)
            dq_ref[pl.ds(off, B), :] = _dot(dsb_p, kp, _NN) + _dot(dsb_c, kc, _NN)
            dk_ref[pl.ds(poff, B), :] += _dot(dsb_p, qv, _TN)
            dk_ref[pl.ds(off, B), :] += _dot(dsb_c, qv, _TN)
            dv_ref[pl.ds(poff, B), :] += _dot(p_p.astype(BF16), dob, _TN)
            dv_ref[pl.ds(off, B), :] += _dot(p_c.astype(BF16), dob, _TN)
            return carry

        lax.fori_loop(0, nblk, blk, 0)

    seq = lambda w: pl.BlockSpec((None, None, L, w), lambda b, h: (b, h, 0, 0))
    bsp = pl.BlockSpec((None, None, B, 2 * B), lambda b, h: (b, h, 0, 0))
    return pl.pallas_call(
        body, grid=(nbr, H),
        in_specs=[seq(E), seq(E), seq(2 * E), bsp, seq(2 * E), seq(2 * E)],
        out_specs=[seq(E), seq(E), seq(2 * E), bsp],
        out_shape=[jax.ShapeDtypeStruct((nbr, H, L, E), F32), jax.ShapeDtypeStruct((nbr, H, L, E), F32),
                   jax.ShapeDtypeStruct((nbr, H, L, 2 * E), F32), jax.ShapeDtypeStruct((nbr, H, B, 2 * B), F32)],
        compiler_params=_cparams(("parallel", "parallel")), name=name)(q, k, vx, bias, ox, dox)


def _mix_weights(xs, lane, E, W):
    al = [jnp.where(lane < E, pltpu.roll(x, W - E, 1), x) for x in xs]
    m = functools.reduce(jnp.maximum, al)
    es = [jnp.exp(a - m) for a in al]
    tot = functools.reduce(lambda a, b: a + b, es)
    return [e / tot for e in es]


def _mix_fwd(ox, *, name):
    nbr, L, W = ox.shape
    E = W // ATT_HEADS // 2
    tr = _tile(L, 256)

    def body(x_ref, o_ref):
        lane = lax.broadcasted_iota(jnp.int32, (tr, W), 1) % (2 * E)
        xs = [x_ref[b] for b in range(nbr)]
        ws = _mix_weights(xs, lane, E, W)
        o_ref[...] = functools.reduce(lambda a, b: a + b, [w * x for w, x in zip(ws, xs)])

    return pl.pallas_call(
        body, grid=(L // tr,), in_specs=[pl.BlockSpec((nbr, tr, W), lambda i: (0, i, 0))],
        out_specs=pl.BlockSpec((tr, W), lambda i: (i, 0)), out_shape=jax.ShapeDtypeStruct((L, W), F32),
        compiler_params=_cparams(("parallel",)), name=name)(ox)


def _mix_bwd(ox, dy, *, name):
    nbr, L, W = ox.shape
    E = W // ATT_HEADS // 2
    tr = _tile(L, 256)

    def body(x_ref, dy_ref, o_ref):
        lane = lax.broadcasted_iota(jnp.int32, (tr, W), 1) % (2 * E)
        xs = [x_ref[b] for b in range(nbr)]
        ws = _mix_weights(xs, lane, E, W)
        mix = functools.reduce(lambda a, b: a + b, [w * x for w, x in zip(ws, xs)])
        dyv = jnp.where(lane < E, dy_ref[...], 0.0)
        r = lax.broadcasted_iota(jnp.int32, (W, W), 0)
        c = lax.broadcasted_iota(jnp.int32, (W, W), 1)
        seg = ((r // (2 * E) == c // (2 * E)) & (r % (2 * E) < E)).astype(F32)
        for b in range(nbr):
            t = dyv * (xs[b] - mix)
            dl = ws[b] * jnp.dot(t, seg, preferred_element_type=F32, precision=lax.Precision.HIGHEST)
            o_ref[b] = jnp.where(lane < E, ws[b] * dyv, dl)

    return pl.pallas_call(
        body, grid=(L // tr,),
        in_specs=[pl.BlockSpec((nbr, tr, W), lambda i: (0, i, 0)), pl.BlockSpec((tr, W), lambda i: (i, 0))],
        out_specs=pl.BlockSpec((nbr, tr, W), lambda i: (0, i, 0)), out_shape=jax.ShapeDtypeStruct((nbr, L, W), F32),
        compiler_params=_cparams(("parallel",), VMEM_LIMIT), name=name)(ox, dy)


def _t5_bucket(dist):
    n = np.maximum(dist, 0)
    max_exact = REL_BUCKETS // 2
    large = max_exact + (np.log(np.maximum(n, 1) / max_exact) / np.log(REL_MAX_DIST / max_exact)
                         * (REL_BUCKETS - max_exact)).astype(np.int64)
    large = np.minimum(large, REL_BUCKETS - 1)
    return np.where(n < max_exact, n, large).astype(np.int32)


def _att_tables():
    a = np.arange(ATT_BLOCK)[:, None]
    b = np.arange(2 * ATT_BLOCK)[None, :]
    sub = a + ATT_BLOCK - b
    buckets, valids = [], []
    for window, dil in DILATED_PATTERNS:
        buckets.append(_t5_bucket(sub * dil))
        valids.append((sub >= 0) & (sub <= window // dil))
    return np.stack(buckets), np.stack(valids)


def _to_branches(t, H):
    L = t.shape[0]
    E = t.shape[1] // H
    out = []
    for _, dil in DILATED_PATTERNS:
        out.append(t.reshape(L // dil, dil, H, E).transpose(2, 1, 0, 3).reshape(H, L, E))
    return jnp.stack(out)


def _from_branches(t):
    _, H, L, E = t.shape
    out = []
    for b, (_, dil) in enumerate(DILATED_PATTERNS):
        out.append(t[b].reshape(H, dil, L // dil, E).transpose(2, 1, 0, 3).reshape(L, H * E))
    return jnp.stack(out)


def _xatt_fwd(q, k, v, *, name):
    L, XW = q.shape
    M = k.shape[0]
    tr = _tile(L, 512)
    scale = X_HEAD_DIM ** -0.5

    def body(q_ref, k_ref, v_ref, o_ref):
        s = _dot(q_ref[...], k_ref[...], _NT) * scale
        p = jnp.exp(s - jnp.max(s, axis=-1, keepdims=True))
        den = jnp.sum(p, axis=-1, keepdims=True)
        o_ref[...] = (_dot(p.astype(BF16), v_ref[...], _NN) / den).astype(o_ref.dtype)

    qs = pl.BlockSpec((tr, X_HEAD_DIM), lambda h, i: (i, h))
    ks = pl.BlockSpec((M, X_HEAD_DIM), lambda h, i: (0, h))
    return pl.pallas_call(
        body, grid=(XW // X_HEAD_DIM, L // tr), in_specs=[qs, ks, ks], out_specs=qs,
        out_shape=jax.ShapeDtypeStruct((L, XW), BF16),
        compiler_params=_cparams(("parallel", "parallel")), name=name)(q, k, v)


def _xatt_bwd(q, k, v, do, *, name):
    L, XW = q.shape
    M = k.shape[0]
    tr = _tile(L, 512)
    scale = X_HEAD_DIM ** -0.5

    def body(q_ref, k_ref, v_ref, do_ref, dq_ref, dk_ref, dv_ref):
        @pl.when(pl.program_id(1) == 0)
        def _():
            dk_ref[...] = jnp.zeros_like(dk_ref)
            dv_ref[...] = jnp.zeros_like(dv_ref)

        qv, kv, vv = q_ref[...], k_ref[...], v_ref[...]
        s = _dot(qv, kv, _NT) * scale
        p = jnp.exp(s - jnp.max(s, axis=-1, keepdims=True))
        p = p / jnp.sum(p, axis=-1, keepdims=True)
        dob = do_ref[...].astype(BF16)
        pb = p.astype(BF16)
        dv_ref[...] += _dot(pb, dob, _TN)
        dp = _dot(dob, vv, _NT)
        ds = (p * (dp - jnp.sum(dp * p, axis=-1, keepdims=True)) * scale).astype(BF16)
        dq_ref[...] = _dot(ds, kv, _NN)
        dk_ref[...] += _dot(ds, qv, _TN)

    qs = pl.BlockSpec((tr, X_HEAD_DIM), lambda h, i: (i, h))
    ks = pl.BlockSpec((M, X_HEAD_DIM), lambda h, i: (0, h))
    return pl.pallas_call(
        body, grid=(XW // X_HEAD_DIM, L // tr), in_specs=[qs, ks, ks, qs], out_specs=[qs, ks, ks],
        out_shape=[jax.ShapeDtypeStruct((L, XW), F32), jax.ShapeDtypeStruct((M, XW), F32), jax.ShapeDtypeStruct((M, XW), F32)],
        compiler_params=_cparams(("parallel", "arbitrary")), name=name)(q, k, v, do)


_ANY = pl.BlockSpec(memory_space=pl.ANY)


def _place():
    mx, my, mc = lax.axis_index("x"), lax.axis_index("y"), lax.axis_index("c")
    chips = [(1 - mx, my), (mx, 1 - my), (1 - mx, 1 - my)]
    return mx, my, mc, chips


def _rcopy(src, dst, ssem, rsem, dev):
    return pltpu.make_async_remote_copy(src_ref=src, dst_ref=dst, send_sem=ssem, recv_sem=rsem, device_id=dev,
                                        device_id_type=MESH)


def _ag_w(x, *, name):
    R, C = x.shape
    R2 = R // 2

    def body(x_ref, o_ref, ssem, rsem, lsem):
        mx, my, mc, chips = _place()
        me = 2 * mx + my
        sib = (mx, my, 1 - mc)
        locs = [pltpu.make_async_copy(x_ref.at[pl.ds(h * R2, R2)], o_ref.at[me, h], lsem.at[h]) for h in range(2)]
        for cp in locs:
            cp.start()
        mine = x_ref.at[pl.ds(mc * R2, R2)]
        sends = [_rcopy(mine, o_ref.at[me, mc], ssem.at[j], rsem.at[j], (px, py, mc)) for j, (px, py) in enumerate(chips)]
        for cp in sends:
            cp.start()
        passed = []
        for j, (px, py) in enumerate(chips):
            got = o_ref.at[2 * px + py, mc]
            _rcopy(mine, got, ssem.at[j], rsem.at[j], (px, py, mc)).wait_recv()
            cp = _rcopy(got, got, ssem.at[3 + j], rsem.at[3 + j], sib)
            cp.start()
            passed.append(cp)
        for j, (px, py) in enumerate(chips):
            theirs = o_ref.at[2 * px + py, 1 - mc]
            _rcopy(theirs, theirs, ssem.at[3 + j], rsem.at[3 + j], sib).wait_recv()
        for cp in sends + passed:
            cp.wait_send()
        for cp in locs:
            cp.wait()

    return pl.pallas_call(
        body, in_specs=[_ANY], out_specs=_ANY, out_shape=jax.ShapeDtypeStruct((N_CHIPS, 2, R2, C), x.dtype),
        scratch_shapes=[pltpu.SemaphoreType.DMA((6,)), pltpu.SemaphoreType.DMA((6,)), pltpu.SemaphoreType.DMA((2,))],
        name=name)(x)


def _ag4(x, *, name):
    def body(x_ref, o_ref, ssem, rsem, lsem):
        mx, my, mc, chips = _place()
        me = 2 * mx + my
        loc = pltpu.make_async_copy(x_ref, o_ref.at[me], lsem)
        loc.start()
        sends = [_rcopy(x_ref, o_ref.at[me], ssem.at[j], rsem.at[j], (px, py, mc)) for j, (px, py) in enumerate(chips)]
        for cp in sends:
            cp.start()
        for j, (px, py) in enumerate(chips):
            _rcopy(x_ref, o_ref.at[2 * px + py], ssem.at[j], rsem.at[j], (px, py, mc)).wait_recv()
        for cp in sends:
            cp.wait_send()
        loc.wait()

    return pl.pallas_call(
        body, in_specs=[_ANY], out_specs=_ANY, out_shape=jax.ShapeDtypeStruct((N_CHIPS,) + x.shape, x.dtype),
        scratch_shapes=[pltpu.SemaphoreType.DMA((3,)), pltpu.SemaphoreType.DMA((3,)), pltpu.SemaphoreType.DMA(())],
        name=name)(x)


def _a2a4(h, *, name):
    def body(h_ref, o_ref, ssem, rsem, lsem):
        mx, my, mc, chips = _place()
        me = 2 * mx + my
        loc = pltpu.make_async_copy(h_ref.at[me], o_ref.at[me], lsem)
        loc.start()
        sends = [_rcopy(h_ref.at[2 * px + py], o_ref.at[me], ssem.at[j], rsem.at[j], (px, py, mc))
                 for j, (px, py) in enumerate(chips)]
        for cp in sends:
            cp.start()
        for j, (px, py) in enumerate(chips):
            _rcopy(h_ref.at[me], o_ref.at[2 * px + py], ssem.at[j], rsem.at[j], (px, py, mc)).wait_recv()
        for cp in sends:
            cp.wait_send()
        loc.wait()

    return pl.pallas_call(
        body, in_specs=[_ANY], out_specs=_ANY, out_shape=jax.ShapeDtypeStruct(h.shape, h.dtype),
        scratch_shapes=[pltpu.SemaphoreType.DMA((3,)), pltpu.SemaphoreType.DMA((3,)), pltpu.SemaphoreType.DMA(())],
        name=name)(h)


def _rs_sib(gs, *, name):
    nl = len(gs)
    _, R, C = gs[0].shape
    R2 = R // 2

    def body(*refs):
        g_refs, (mine_ref, theirs_ref, ssem, rsem, lsem) = refs[:nl], refs[nl:]
        mx, my, mc, _ = _place()
        sib = (mx, my, 1 - mc)
        locs, sends = [], []
        for l in range(nl):
            locs.append(pltpu.make_async_copy(g_refs[l].at[:, pl.ds(mc * R2, R2)], mine_ref.at[:, l], lsem.at[l]))
            sends.append(_rcopy(g_refs[l].at[:, pl.ds((1 - mc) * R2, R2)], theirs_ref.at[:, l], ssem.at[l], rsem.at[l], sib))
        for cp in locs + sends:
            cp.start()
        for l in range(nl):
            _rcopy(mine_ref.at[:, l], theirs_ref.at[:, l], ssem.at[l], rsem.at[l], sib).wait_recv()
        for cp in sends:
            cp.wait_send()
        for cp in locs:
            cp.wait()

    shp = jax.ShapeDtypeStruct((N_CHIPS, nl, R2, C), gs[0].dtype)
    return pl.pallas_call(
        body, in_specs=[_ANY] * nl, out_specs=[_ANY, _ANY], out_shape=[shp, shp],
        scratch_shapes=[pltpu.SemaphoreType.DMA((nl,)), pltpu.SemaphoreType.DMA((nl,)), pltpu.SemaphoreType.DMA((nl,))],
        name=name)(*gs)


def _sib_gather(s, *, name):
    def body(s_ref, o_ref, ssem, rsem, lsem):
        mx, my, mc, _ = _place()
        sib = (mx, my, 1 - mc)
        loc = pltpu.make_async_copy(s_ref, o_ref.at[mc], lsem)
        loc.start()
        cp = _rcopy(s_ref, o_ref.at[mc], ssem, rsem, sib)
        cp.start()
        _rcopy(s_ref, o_ref.at[1 - mc], ssem, rsem, sib).wait_recv()
        cp.wait_send()
        loc.wait()

    return pl.pallas_call(
        body, in_specs=[_ANY], out_specs=_ANY, out_shape=jax.ShapeDtypeStruct((2,) + s.shape, s.dtype),
        scratch_shapes=[pltpu.SemaphoreType.DMA(()), pltpu.SemaphoreType.DMA(()), pltpu.SemaphoreType.DMA(())],
        name=name)(s)


def _rows_tile(n, pref=512):
    t = min(n, pref)
    while n % t or (t % 8 and t != n):
        t -= 1
    return t


def _sum_parts(parts, *, name):
    P, rows, C = parts.shape
    tr = _rows_tile(rows)

    def body(p_ref, o_ref):
        acc = p_ref[0]
        for k in range(1, P):
            acc = acc + p_ref[k]
        o_ref[...] = acc

    return pl.pallas_call(
        body, grid=(rows // tr,), in_specs=[pl.BlockSpec((P, tr, C), lambda i: (0, i, 0))],
        out_specs=pl.BlockSpec((tr, C), lambda i: (i, 0)), out_shape=jax.ShapeDtypeStruct((rows, C), F32),
        compiler_params=_cparams(("parallel",)), name=name)(parts)


def _add2(a, b, *, name):
    rows, C = a.shape
    tr = _rows_tile(rows)

    def body(a_ref, b_ref, o_ref):
        o_ref[...] = a_ref[...] + b_ref[...]

    row = pl.BlockSpec((tr, C), lambda i: (i, 0))
    return pl.pallas_call(
        body, grid=(rows // tr,), in_specs=[row, row], out_specs=row, out_shape=jax.ShapeDtypeStruct((rows, C), F32),
        compiler_params=_cparams(("parallel",)), name=name)(a, b)


def _adamw(w, m, v, g, *, name):
    A, B, R, C = w.shape
    tr = _rows_tile(R, 256)
    c1 = 1.0 / (1.0 - ADAM_B1 ** ADAM_STEP)
    c2 = 1.0 / (1.0 - ADAM_B2 ** ADAM_STEP)

    def body(w_ref, m_ref, v_ref, g_ref, go_ref, d_ref, mo_ref, vo_ref):
        gv = g_ref[...]
        mn = ADAM_B1 * m_ref[...] + (1.0 - ADAM_B1) * gv
        vn = ADAM_B2 * v_ref[...] + (1.0 - ADAM_B2) * (gv * gv)
        go_ref[...] = gv
        mo_ref[...] = mn
        vo_ref[...] = vn
        d_ref[...] = -ADAM_LR * ((mn * c1) / (jnp.sqrt(vn * c2) + ADAM_EPS) + ADAM_WD * w_ref[...])

    ws = pl.BlockSpec((None, None, tr, C), lambda a, b, i: (a, b, i, 0))
    gs = pl.BlockSpec((None, None, tr, C), lambda a, b, i: (b, a, i, 0))
    shp = jax.ShapeDtypeStruct(w.shape, F32)
    return pl.pallas_call(
        body, grid=(A, B, R // tr), in_specs=[ws, ws, ws, gs], out_specs=[ws] * 4, out_shape=[shp] * 4,
        compiler_params=_cparams(("parallel", "parallel", "parallel")), name=name)(w, m, v, g)


def _reduce_grads(gs, tag):
    nl = len(gs)
    _, R, C = gs[0].shape
    mine, theirs = _rs_sib(gs, name=f"rs_sib_{tag}")
    rows = N_CHIPS * nl * (R // 2)
    chip_sum = _add2(mine.reshape(rows, C), theirs.reshape(rows, C), name=f"rs_add_{tag}").reshape(N_CHIPS, nl, R // 2, C)
    got = _a2a4(chip_sum, name=f"rs_a2a_{tag}")
    half = _sum_parts(got.reshape(N_CHIPS, nl * (R // 2), C), name=f"rs_sum_{tag}").reshape(nl, R // 2, C)
    return _sib_gather(half, name=f"rs_gather_{tag}")


WEIGHTS = ["rel_bias", "mem_norm_g", "norm_mix_g", "w_in", "s5_lam_re", "s5_lam_im", "s5_log_dt", "s5_b_re", "s5_b_im",
           "s5_c_re", "s5_c_im", "s5_d", "s5_w_glu", "pool_w", "pool_scale", "conv_w_dw", "conv_b_dw", "conv_ln_g",
           "conv_ln_b", "conv_w_pw", "grp_norm_g", "w_out", "norm_x_g", "w_xq", "w_xk", "w_xv", "w_xo", "norm_mlp_g",
           "w_up", "w_down", "norm_final_g"]
SHARDED = {"w_in": "col", "s5_w_glu": "row", "conv_w_dw": "col", "conv_w_pw": "row", "w_out": "row", "w_xq": "row",
           "w_xk": "row", "w_xv": "row", "w_xo": "col", "w_up": "col", "w_down": "row"}
SMALL = [n for n in WEIGHTS if n not in SHARDED]
SMALL_ALIGN = N_CHIPS * 2 * 8 * 128


def _mat(w, kind):
    return w.reshape(w.shape[0] * w.shape[1], w.shape[2]) if kind == "row" else w


def _att_bias(rel_bias):
    bucket, valid = _att_tables()
    b = jnp.transpose(rel_bias[jnp.asarray(bucket)], (0, 3, 1, 2))
    return jnp.where(jnp.asarray(valid)[:, None], b, NEG_INF)


def _rel_bias_grad(dbias):
    bucket, _ = _att_tables()
    nb, H = dbias.shape[0], dbias.shape[1]
    onehot = (jnp.asarray(bucket.reshape(-1))[:, None] == jnp.arange(REL_BUCKETS)[None, :]).astype(BF16)
    flat = jnp.transpose(dbias.reshape(nb, H, -1), (1, 0, 2)).reshape(H, -1)
    return _mm(flat, onehot, "nn", name="rel_bias_grad").T


def _layer_fwd(h, mem_n, bias, sp, bw, l):
    L, D = h.shape
    GW = D // N_MIXERS
    G = GW // S5_CH
    E = GW // ATT_HEADS
    sv = {"h": h}
    xn = _rms_fwd(h, sp["norm_mix_g"][l][None], name="norm_mix")
    proj = _mm(xn, bw["w_in"], "nn", name="proj_in")
    sv.update(xn=xn, proj=proj)
    disc, disc_vjp = jax.vjp(_s5_disc, sp["s5_lam_re"][l], sp["s5_lam_im"][l], sp["s5_log_dt"][l], sp["s5_b_re"][l], sp["s5_b_im"][l])
    ab_r, ab_i, bb_r, bb_i = disc
    s5 = dict(
        bdr=_bd(jnp.transpose(bb_r, (0, 2, 1))).astype(BF16), bdi=_bd(jnp.transpose(bb_i, (0, 2, 1))).astype(BF16),
        cdr=_bd(jnp.transpose(sp["s5_c_re"][l], (0, 2, 1))).astype(BF16), cdi=_bd(jnp.transpose(sp["s5_c_im"][l], (0, 2, 1))).astype(BF16),
        d=sp["s5_d"][l][None], wglu=bw["s5_w_glu"], ab=(ab_r.reshape(-1), ab_i.reshape(-1)), vjp=disc_vjp)
    pw, tr, ti = _cpow_tables(*s5["ab"])
    y_a, xr, xi = _s5_fwd(proj, s5["bdr"], s5["bdi"], pw, tr, ti, s5["cdr"], s5["cdi"], s5["d"], s5["wglu"], name="s5_fwd")
    sv.update(s5=s5, xr=xr, xi=xi, y_a=y_a)
    pool_s = sp["pool_scale"][l].reshape(len(POOL_WINDOWS), 1, -1)
    y_b = _pool_fwd(proj, sp["pool_w"][l], pool_s, name="pool_fwd")
    sv.update(y_b=y_b, pool_s=pool_s)
    hc = _conv1_fwd(proj, bw["conv_w_dw"], sp["conv_b_dw"][l][None], name="conv_dw_fwd")
    y_c = _conv2_fwd(hc, sp["conv_ln_g"][l][None], sp["conv_ln_b"][l][None], bw["conv_w_pw"], name="conv_pw_fwd")
    sv.update(hc=hc, y_c=y_c)
    scale = E ** -0.5
    q = (proj[:, 4 * GW:5 * GW] * scale).astype(BF16)
    k = proj[:, 5 * GW:6 * GW].astype(BF16)
    v = proj[:, 6 * GW:7 * GW].astype(BF16).reshape(L, ATT_HEADS, E)
    vx = jnp.concatenate([v, jnp.ones_like(v)], axis=-1).reshape(L, ATT_HEADS * 2 * E)
    qb, kb, vxb = _to_branches(q, ATT_HEADS), _to_branches(k, ATT_HEADS), _to_branches(vx, ATT_HEADS)
    ox = _att_fwd(qb, kb, vxb, bias, name="att_fwd")
    oxu = _from_branches(ox)
    mix = _mix_fwd(oxu, name="att_mix_fwd")
    y_d = mix.reshape(L, ATT_HEADS, 2 * E)[:, :, :E].reshape(L, GW)
    sv.update(qb=qb, kb=kb, vxb=vxb, ox=ox, oxu=oxu, y_d=y_d)
    yn = _grp_fwd([y_a, y_b, y_c, y_d], sp["grp_norm_g"][l][None], name="grp_fwd")
    h1 = _mm(yn, bw["w_out"], "nn", res=h, name="proj_out")
    sv.update(yn=yn, h1=h1)
    hx = _rms_fwd(h1, sp["norm_x_g"][l][None], name="norm_x")
    q_x = _mm(hx, bw["w_xq"], "nn", out_dtype=BF16, name="xq")
    k_x = _mm(mem_n, bw["w_xk"], "nn", out_dtype=BF16, name="xk")
    v_x = _mm(mem_n, bw["w_xv"], "nn", out_dtype=BF16, name="xv")
    o_x = _xatt_fwd(q_x, k_x, v_x, name="xatt_fwd")
    h2 = _mm(o_x, bw["w_xo"], "nn", res=h1, name="xo")
    sv.update(hx=hx, q_x=q_x, k_x=k_x, v_x=v_x, o_x=o_x, h2=h2)
    hm = _rms_fwd(h2, sp["norm_mlp_g"][l][None], name="norm_mlp")
    up, act = _mm(hm, bw["w_up"], "nn", epi="relu2", out_dtype=BF16, name="mlp_up")
    h3 = _mm(act, bw["w_down"], "nn", res=h2, name="mlp_down")
    sv.update(hm=hm, up=up, act=act)
    return h3, sv


def _stack_rows(g):
    return g.reshape(N_CHIPS, g.shape[0] // N_CHIPS, g.shape[1])


def _layer_bwd(dh3, d_memn, mem_n, bias, sp, bw, sv, l):
    L, D = dh3.shape
    GW = D // N_MIXERS
    G = GW // S5_CH
    E = GW // ATT_HEADS
    gs, gb = {}, {}
    d_up = _mm(dh3, bw["w_down"], "nt", epi="relu2_bwd", aux=sv["up"], out_dtype=BF16, name="mlp_down_dx")
    gb["w_down"] = _stack_rows(_mm(sv["act"], dh3, "tn", name="mlp_down_dw"))
    gb["w_up"] = _mm(sv["hm"], d_up, "tn", out_stack=N_CHIPS, name="mlp_up_dw")
    d_hm = _mm(d_up, bw["w_up"], "nt", name="mlp_up_dx")
    dh2, gs["norm_mlp_g"] = _rms_bwd(sv["h2"], sp["norm_mlp_g"][l][None], d_hm, dh3, name="norm_mlp_bwd")
    d_ox = _mm(dh2, bw["w_xo"], "nt", name="xo_dx")
    gb["w_xo"] = _mm(sv["o_x"], dh2, "tn", out_stack=N_CHIPS, name="xo_dw")
    dq_x, dk_x, dv_x = _xatt_bwd(sv["q_x"], sv["k_x"], sv["v_x"], d_ox, name="xatt_bwd")
    gb["w_xq"] = _stack_rows(_mm(sv["hx"], dq_x, "tn", name="xq_dw"))
    gb["w_xk"] = _stack_rows(_mm(mem_n, dk_x, "tn", name="xk_dw"))
    gb["w_xv"] = _stack_rows(_mm(mem_n, dv_x, "tn", name="xv_dw"))
    d_memn = _mm(dk_x, bw["w_xk"], "nt", res=d_memn, name="xk_dx")
    d_memn = _mm(dv_x, bw["w_xv"], "nt", res=d_memn, name="xv_dx")
    d_hx = _mm(dq_x, bw["w_xq"], "nt", name="xq_dx")
    dh1, gs["norm_x_g"] = _rms_bwd(sv["h1"], sp["norm_x_g"][l][None], d_hx, dh2, name="norm_x_bwd")
    d_yn = _mm(dh1, bw["w_out"], "nt", name="proj_out_dx")
    gb["w_out"] = _stack_rows(_mm(sv["yn"], dh1, "tn", name="proj_out_dw"))
    ys = [sv["y_a"], sv["y_b"], sv["y_c"], sv["y_d"]]
    dya, dyb, dyc, dyd, gs["grp_norm_g"] = _grp_bwd(ys, sp["grp_norm_g"][l][None], d_yn, name="grp_bwd")
    dy_ext = jnp.concatenate([dyd.reshape(L, ATT_HEADS, E), jnp.zeros((L, ATT_HEADS, E), F32)], axis=-1).reshape(L, ATT_HEADS * 2 * E)
    doxu = _mix_bwd(sv["oxu"], dy_ext, name="att_mix_bwd")
    dox = jnp.stack([_to_branches(doxu[b], ATT_HEADS)[b] for b in range(len(DILATED_PATTERNS))])
    dq_b, dk_b, dvx_b, dbias = _att_bwd(sv["qb"], sv["kb"], sv["vxb"], bias, sv["ox"], dox, name="att_bwd")
    dq = jnp.sum(_from_branches(dq_b), axis=0) * (E ** -0.5)
    dk = jnp.sum(_from_branches(dk_b), axis=0)
    dv = jnp.sum(_from_branches(dvx_b[..., :E]), axis=0)
    dhc, gs["conv_ln_g"], gs["conv_ln_b"], g_pw = _conv2_bwd(sv["hc"], sp["conv_ln_g"][l][None], sp["conv_ln_b"][l][None],
                                                               bw["conv_w_pw"], dyc, name="conv_pw_bwd")
    gb["conv_w_pw"] = _stack_rows(g_pw)
    dval, dgate, g_dw, gs["conv_b_dw"] = _conv1_bwd(sv["proj"], dhc, bw["conv_w_dw"], name="conv_dw_bwd")
    gb["conv_w_dw"] = jnp.transpose(g_dw.reshape(CONV_PAD, N_CHIPS, GW // N_CHIPS), (1, 0, 2))
    du_b, gs["pool_w"], g_ps = _pool_bwd(sv["proj"], dyb, sp["pool_w"][l], sv["pool_s"], name="pool_bwd")
    gs["pool_scale"] = g_ps.reshape(-1)
    s5 = sv["s5"]
    conj = (s5["ab"][0], -s5["ab"][1])
    pw, tr, ti = _cpow_tables(*conj)
    du_a, g_glu, g_d, dcdr, dcdi, dbdr, dbdi, dar, dai = _s5_bwd(
        sv["proj"], sv["xr"], sv["xi"], dya, s5["bdr"], s5["bdi"], pw, tr[::-1], ti[::-1], s5["cdr"], s5["cdi"], s5["d"], s5["wglu"],
        name="s5_bwd")
    gb["s5_w_glu"] = _stack_rows(g_glu)
    gs["s5_d"] = g_d.reshape(-1)
    gs["s5_c_re"] = jnp.transpose(_bd_extract(dcdr, G), (0, 2, 1))
    gs["s5_c_im"] = jnp.transpose(_bd_extract(dcdi, G), (0, 2, 1))
    d_bb_r = jnp.transpose(_bd_extract(dbdr, G), (0, 2, 1))
    d_bb_i = jnp.transpose(_bd_extract(dbdi, G), (0, 2, 1))
    d_ab_r = jnp.sum(dar, axis=0).reshape(G, S5_STATE)
    d_ab_i = jnp.sum(dai, axis=0).reshape(G, S5_STATE)
    gs["s5_lam_re"], gs["s5_lam_im"], gs["s5_log_dt"], gs["s5_b_re"], gs["s5_b_im"] = s5["vjp"]((d_ab_r, d_ab_i, d_bb_r, d_bb_i))
    d_proj = jnp.concatenate([du_a, du_b, dval, dgate, dq, dk, dv], axis=1)
    gb["w_in"] = _mm(sv["xn"], d_proj, "tn", out_stack=N_CHIPS, name="proj_in_dw")
    d_xn = _mm(d_proj, bw["w_in"], "nt", name="proj_in_dx")
    dh0, gs["norm_mix_g"] = _rms_bwd(sv["h"], sp["norm_mix_g"][l][None], d_xn, dh1, name="norm_mix_bwd")
    gs = {n: g.reshape(sp[n].shape[1:]) for n, g in gs.items()}
    return dh0, d_memn, dbias, gs, gb


def _device_step(x, mem, target, sp, big):
    nl = sp["norm_mix_g"].shape[0]
    mem_n = _rms_fwd(mem, sp["mem_norm_g"][None], name="norm_mem")
    bias = _att_bias(sp["rel_bias"])
    h, saved = x, []
    for l in range(nl):
        h, sv = _layer_fwd(h, mem_n, bias, sp, {n: big[n][l] for n in SHARDED}, l)
        saved.append(sv)
    loss, dh, g_final = _loss_head(h, sp["norm_final_g"][None], target, name="loss_head")
    d_memn = jnp.zeros(mem.shape, F32)
    dbias = jnp.zeros(bias.shape, F32)
    sg = {n: [None] * nl for n in SMALL}
    bg = {n: [None] * nl for n in SHARDED}
    for l in reversed(range(nl)):
        dh, d_memn, dbias_l, gs, gb = _layer_bwd(dh, d_memn, mem_n, bias, sp, {n: big[n][l] for n in SHARDED}, saved[l], l)
        dbias = dbias + dbias_l
        for n, g in gs.items():
            sg[n][l] = g
        for n, g in gb.items():
            bg[n][l] = g
    small = {n: jnp.stack(g) for n, g in sg.items() if g[0] is not None}
    small["norm_final_g"] = g_final.reshape(-1)
    _, g_mem = _rms_bwd(mem, sp["mem_norm_g"][None], d_memn, None, name="norm_mem_bwd")
    small["mem_norm_g"] = g_mem.reshape(-1)
    small["rel_bias"] = _rel_bias_grad(dbias)
    return loss, dh, small, bg


def _gather_big(shards):
    big = {}
    for n, kind in SHARDED.items():
        w = shards[n]
        per_layer = []
        for l in range(w.shape[0]):
            if n == "conv_w_dw":
                s = jnp.pad(w[l], ((0, CONV_PAD - CONV_WIDTH), (0, 0)))
                full = _ag_w(s, name=f"ag_{n}").reshape(N_CHIPS, CONV_PAD, -1)
                per_layer.append(jnp.transpose(full, (1, 0, 2)).reshape(CONV_PAD, -1))
            else:
                full = _ag_w(w[l].astype(BF16), name=f"ag_{n}")
                per_layer.append(_mat(full.reshape(N_CHIPS, w.shape[1], w.shape[2]), kind))
        big[n] = per_layer
    return big


def _pack_small(vals, extra=None):
    flat = [vals[n].reshape(-1).astype(F32) for n in SMALL]
    flat.append(jnp.zeros((1,), F32) if extra is None else extra.reshape(1))
    v = jnp.concatenate(flat)
    n_pad = -(-v.shape[0] // SMALL_ALIGN) * SMALL_ALIGN
    return jnp.pad(v, (0, n_pad - v.shape[0]))


def _unpack_small(flat, like):
    out, pos = {}, 0
    for n in SMALL:
        size = math.prod(like[n].shape)
        out[n] = flat[pos:pos + size].reshape(like[n].shape)
        pos += size
    return out, flat[pos]


def kernel(x, mem, rel_bias, mem_norm_g, norm_mix_g, w_in, s5_lam_re, s5_lam_im, s5_log_dt, s5_b_re, s5_b_im, s5_c_re, s5_c_im, s5_d, s5_w_glu, pool_w, pool_scale, conv_w_dw, conv_b_dw, conv_ln_g, conv_ln_b, conv_w_pw, grp_norm_g, w_out, norm_x_g, w_xq, w_xk, w_xv, w_xo, norm_mlp_g, w_up, w_down, norm_final_g, loss_target, m_rel_bias, m_mem_norm_g, m_norm_mix_g, m_w_in, m_s5_lam_re, m_s5_lam_im, m_s5_log_dt, m_s5_b_re, m_s5_b_im, m_s5_c_re, m_s5_c_im, m_s5_d, m_s5_w_glu, m_pool_w, m_pool_scale, m_conv_w_dw, m_conv_b_dw, m_conv_ln_g, m_conv_ln_b, m_conv_w_pw, m_grp_norm_g, m_w_out, m_norm_x_g, m_w_xq, m_w_xk, m_w_xv, m_w_xo, m_norm_mlp_g, m_w_up, m_w_down, m_norm_final_g, v_rel_bias, v_mem_norm_g, v_norm_mix_g, v_w_in, v_s5_lam_re, v_s5_lam_im, v_s5_log_dt, v_s5_b_re, v_s5_b_im, v_s5_c_re, v_s5_c_im, v_s5_d, v_s5_w_glu, v_pool_w, v_pool_scale, v_conv_w_dw, v_conv_b_dw, v_conv_ln_g, v_conv_ln_b, v_conv_w_pw, v_grp_norm_g, v_w_out, v_norm_x_g, v_w_xq, v_w_xk, v_w_xv, v_w_xo, v_norm_mlp_g, v_w_up, v_w_down, v_norm_final_g):
    env = dict(locals())
    w = {n: env[n] for n in WEIGHTS}
    m = {n: env["m_" + n] for n in WEIGHTS}
    v = {n: env["v_" + n] for n in WEIGHTS}
    sp = {n: w[n] for n in SMALL}
    big = _gather_big({n: w[n] for n in SHARDED})
    loss, grad_x, g_small, g_big = _device_step(x[0], mem[0], loss_target[0], sp, big)

    grad, delta, new_m, new_v = {}, {}, {}, {}
    for n in SHARDED:
        pad = ((0, 0), (0, CONV_PAD - CONV_WIDTH), (0, 0)) if n == "conv_w_dw" else None
        wmv = [jnp.pad(t[n], pad) if pad else t[n] for t in (w, m, v)]
        nl, R, C = wmv[0].shape
        g = _reduce_grads(g_big[n], n)
        outs = _adamw(*[t.reshape(nl, 2, R // 2, C) for t in wmv], g, name=f"adamw_{n}")
        outs = [o.reshape(nl, R, C)[:, :w[n].shape[1]] for o in outs]
        grad[n], delta[n], new_m[n], new_v[n] = outs
    packed = _pack_small(g_small, loss).reshape(N_CHIPS, -1, 128)
    quarter = _reduce_grads([packed], "small")
    total = _ag4(quarter.reshape(-1, 128), name="ag_small").reshape(1, 1, -1, 128)
    wmv = [_pack_small(t).reshape(1, 1, -1, 128) for t in (w, m, v)]
    outs = _adamw(*wmv, total, name="adamw_small")
    loss_sum = None
    for o, dst in zip(outs, (grad, delta, new_m, new_v)):
        vals, last = _unpack_small(o.reshape(-1), sp)
        dst.update(vals)
        loss_sum = last if loss_sum is None else loss_sum
    return (loss_sum, grad_x[None], *[grad[n] for n in WEIGHTS], *[delta[n] for n in WEIGHTS],
            *[new_m[n] for n in WEIGHTS], *[new_v[n] for n in WEIGHTS])
```

```python
import functools
import math

import jax
import jax.numpy as jnp
import numpy as np
from jax import lax
from jax.experimental import pallas as pl
from jax.experimental.pallas import tpu as pltpu

F32 = jnp.float32
BF16 = jnp.bfloat16
MESH = pl.DeviceIdType.MESH

N_MIXERS = 4
S5_CH = 16
S5_STATE = 64
POOL_WINDOWS = (2, 4, 8, 16)
CONV_WIDTH = 31
CONV_PAD = 32
ATT_HEADS = 8
ATT_BLOCK = 128
DILATED_PATTERNS = ((128, 1), (512, 4), (2048, 16))
REL_BUCKETS = 32
REL_MAX_DIST = 2048
X_HEADS = 4
X_HEAD_DIM = 128
NORM_EPS = 1e-6
NEG_INF = -1e30
ADAM_LR, ADAM_B1, ADAM_B2, ADAM_EPS, ADAM_WD, ADAM_STEP = 0.001, 0.9, 0.999, 1e-08, 0.01, 10
N_CHIPS = 4
VMEM_LIMIT = 56 * 1024 * 1024


def _cparams(sem=None, vmem=None):
    return pltpu.CompilerParams(dimension_semantics=sem, vmem_limit_bytes=vmem)


def _tile(n, pref):
    if n <= pref:
        return n
    t = pref
    while t >= 128:
        if n % t == 0:
            return t
        t -= 128
    return n


def _spec(arr, tr, tc, rc):
    if arr.ndim == 2:
        return pl.BlockSpec((tr, tc), lambda *g: rc(*g))
    per = arr.shape[2] // tc
    assert arr.shape[2] % tc == 0

    def imap(*g):
        r, c = rc(*g)
        return (c // per, r, c % per)

    return pl.BlockSpec((None, tr, tc), imap)


def _lshape(arr):
    return arr.shape if arr.ndim == 2 else (arr.shape[1], arr.shape[0] * arr.shape[2])


def _mm(a, b, mode, *, name, out_dtype=F32, res=None, epi=None, aux=None, out_stack=None, tm=1024, tn=1024, tk=512):
    la, lb = _lshape(a), _lshape(b)
    if mode == "nn":
        (M, K), (K2, N) = la, lb
    elif mode == "nt":
        (M, K), (N, K2) = la, lb
    else:
        (K, M), (K2, N) = la, lb
    assert K == K2, (la, lb, mode)
    lim = {"m": M, "n": N // out_stack if out_stack else N, "k": K}
    stacked = [(a, "m" if mode == "tn" else "k"), (b, "k" if mode == "nt" else "n"), (res, "n"), (aux, "n")]
    for arr, dim in stacked:
        if arr is not None and arr.ndim == 3:
            lim[dim] = math.gcd(lim[dim], arr.shape[2])
    tm, tn, tk = _tile(lim["m"], tm), _tile(lim["n"], tn), _tile(lim["k"], tk)
    nk = K // tk
    grid = (M // tm, N // tn, nk)
    a_spec = _spec(a, tk, tm, lambda i, j, k: (k, i)) if mode == "tn" else _spec(a, tm, tk, lambda i, j, k: (i, k))
    b_spec = _spec(b, tn, tk, lambda i, j, k: (j, k)) if mode == "nt" else _spec(b, tk, tn, lambda i, j, k: (k, j))
    dims = {"nn": ((1,), (0,)), "nt": ((1,), (1,)), "tn": ((0,), (0,))}[mode]
    ins, in_specs = [a, b], [a_spec, b_spec]
    for extra in (res, aux):
        if extra is not None:
            ins.append(extra)
            in_specs.append(_spec(extra, tm, tn, lambda i, j, k: (i, j)))
    if out_stack:
        o_shape = jax.ShapeDtypeStruct((out_stack, M, N // out_stack), out_dtype)
    else:
        o_shape = jax.ShapeDtypeStruct((M, N), out_dtype)
    o_spec = _spec(o_shape, tm, tn, lambda i, j, k: (i, j))
    n_out = 2 if epi == "relu2" else 1
    has_res, has_aux = res is not None, aux is not None

    def body(*refs):
        a_ref, b_ref = refs[0], refs[1]
        pos = 2
        res_ref = aux_ref = None
        if has_res:
            res_ref = refs[pos]
            pos += 1
        if has_aux:
            aux_ref = refs[pos]
            pos += 1
        outs = refs[pos:pos + n_out]
        acc_ref = refs[pos + n_out]
        k = pl.program_id(2)

        @pl.when(k == 0)
        def _():
            acc_ref[...] = jnp.zeros_like(acc_ref)

        acc_ref[...] += lax.dot_general(a_ref[...].astype(BF16), b_ref[...].astype(BF16), (dims, ((), ())),
                                        preferred_element_type=F32)

        @pl.when(k == nk - 1)
        def _():
            o = acc_ref[...]
            if has_res:
                o = o + res_ref[...].astype(F32)
            if epi == "relu2":
                outs[0][...] = o.astype(outs[0].dtype)
                r = jnp.maximum(o, 0.0)
                outs[1][...] = (r * r).astype(outs[1].dtype)
            elif epi == "relu2_bwd":
                outs[0][...] = (o * (2.0 * jnp.maximum(aux_ref[...].astype(F32), 0.0))).astype(outs[0].dtype)
            else:
                outs[0][...] = o.astype(outs[0].dtype)

    out = pl.pallas_call(
        body, grid=grid, in_specs=in_specs,
        out_specs=[o_spec] * n_out if n_out > 1 else o_spec,
        out_shape=[o_shape] * n_out if n_out > 1 else o_shape,
        scratch_shapes=[pltpu.VMEM((tm, tn), F32)],
        compiler_params=_cparams(("parallel", "parallel", "arbitrary"), VMEM_LIMIT), name=name,
    )(*ins)
    return out


def _rms_fwd(x, g, *, name, out_dtype=BF16):
    L, D = x.shape
    tr = _tile(L, 256)

    def body(x_ref, g_ref, o_ref):
        xv = x_ref[...]
        r = lax.rsqrt(jnp.mean(xv * xv, axis=-1, keepdims=True) + NORM_EPS)
        o_ref[...] = (xv * r * g_ref[...]).astype(o_ref.dtype)

    return pl.pallas_call(
        body, grid=(L // tr,),
        in_specs=[pl.BlockSpec((tr, D), lambda i: (i, 0)), pl.BlockSpec((1, D), lambda i: (0, 0))],
        out_specs=pl.BlockSpec((tr, D), lambda i: (i, 0)),
        out_shape=jax.ShapeDtypeStruct((L, D), out_dtype),
        compiler_params=_cparams(("parallel",)), name=name)(x, g)


def _rms_bwd(x, g, dy, dres, *, name):
    L, D = x.shape
    tr = _tile(L, 256)
    has_res = dres is not None

    def body(*refs):
        x_ref, g_ref, dy_ref = refs[:3]
        dres_ref = refs[3] if has_res else None
        dx_ref, dg_ref = refs[-2:]
        xv = x_ref[...]
        dyv = dy_ref[...].astype(F32)
        r = lax.rsqrt(jnp.mean(xv * xv, axis=-1, keepdims=True) + NORM_EPS)
        xh = xv * r
        dyg = dyv * g_ref[...]
        dx = r * (dyg - xh * jnp.mean(dyg * xh, axis=-1, keepdims=True))
        if has_res:
            dx = dx + dres_ref[...]
        dx_ref[...] = dx

        @pl.when(pl.program_id(0) == 0)
        def _():
            dg_ref[...] = jnp.zeros_like(dg_ref)

        dg_ref[...] += jnp.sum(dyv * xh, axis=0, keepdims=True)

    row = pl.BlockSpec((tr, D), lambda i: (i, 0))
    vec = pl.BlockSpec((1, D), lambda i: (0, 0))
    ins = [x, g, dy] + ([dres] if has_res else [])
    return pl.pallas_call(
        body, grid=(L // tr,), in_specs=[row, vec, row] + ([row] if has_res else []),
        out_specs=[row, vec],
        out_shape=[jax.ShapeDtypeStruct((L, D), F32), jax.ShapeDtypeStruct((1, D), F32)],
        compiler_params=_cparams(("arbitrary",)), name=name)(*ins)


def _loss_head(h, g, target, *, name):
    L, D = h.shape
    tr = _tile(L, 256)

    def body(x_ref, g_ref, t_ref, loss_ref, dx_ref, dg_ref):
        xv = x_ref[...]
        r = lax.rsqrt(jnp.mean(xv * xv, axis=-1, keepdims=True) + NORM_EPS)
        xh = xv * r
        err = xh * g_ref[...] - t_ref[...]
        dyv = err * (1.0 / D)
        dyg = dyv * g_ref[...]
        dx_ref[...] = r * (dyg - xh * jnp.mean(dyg * xh, axis=-1, keepdims=True))

        @pl.when(pl.program_id(0) == 0)
        def _():
            dg_ref[...] = jnp.zeros_like(dg_ref)
            loss_ref[...] = jnp.zeros_like(loss_ref)

        dg_ref[...] += jnp.sum(dyv * xh, axis=0, keepdims=True)
        loss_ref[...] += 0.5 * jnp.sum(jnp.sum(err * err, axis=1, keepdims=True), axis=0, keepdims=True) * (1.0 / D)

    row = pl.BlockSpec((tr, D), lambda i: (i, 0))
    vec = pl.BlockSpec((1, D), lambda i: (0, 0))
    return pl.pallas_call(
        body, grid=(L // tr,), in_specs=[row, vec, row],
        out_specs=[pl.BlockSpec((1, 1), lambda i: (0, 0)), row, vec],
        out_shape=[jax.ShapeDtypeStruct((1, 1), F32), jax.ShapeDtypeStruct((L, D), F32), jax.ShapeDtypeStruct((1, D), F32)],
        compiler_params=_cparams(("arbitrary",)), name=name)(h, g, target)


S5_TL = 256
S5_LW = 512


def _dot(a, b, dims):
    return lax.dot_general(a, b, (dims, ((), ())), preferred_element_type=F32)


_NN, _NT, _TN = ((1,), (0,)), ((1,), (1,)), ((0,), (0,))


def _gelu_and_grad(y):
    k = math.sqrt(2.0 / math.pi)
    y2 = y * y
    th = jnp.tanh(k * (y + 0.044715 * y * y2))
    g = 0.5 * y * (1.0 + th)
    gp = 0.5 * (1.0 + th) + 0.5 * y * (1.0 - th * th) * k * (1.0 + 3.0 * 0.044715 * y2)
    return g, gp


def _scan_tiles(re_s, im_s, ls, lw, pw_ref, tr_ref, ti_ref, carry_s, nt, reverse, extra=None):
    row = lax.broadcasted_iota(jnp.int32, (8, lw), 0)
    a = [pw_ref[k:k + 1, ls] for k in range(6)]
    tr_t, ti_t = tr_ref[:, ls], ti_ref[:, ls]
    edge = 0 if reverse else 7

    def tile(jj, carry):
        cr, ci, acc = carry
        j = (nt - 1 - jj) if reverse else jj
        off = pl.multiple_of(j * 8, 8)
        sr, si = re_s[pl.ds(off, 8), ls], im_s[pl.ds(off, 8), ls]
        for n, k in enumerate((1, 2, 4)):
            akr, aki = a[2 * n], a[2 * n + 1]
            if reverse:
                keep, sh = row < 8 - k, 8 - k
            else:
                keep, sh = row >= k, k
            shr = jnp.where(keep, pltpu.roll(sr, sh, 0), 0.0)
            shi = jnp.where(keep, pltpu.roll(si, sh, 0), 0.0)
            sr, si = sr + akr * shr - aki * shi, si + akr * shi + aki * shr
        xr = sr + tr_t * cr - ti_t * ci
        xi = si + tr_t * ci + ti_t * cr
        re_s[pl.ds(off, 8), ls] = xr
        im_s[pl.ds(off, 8), ls] = xi
        if extra is not None:
            acc = extra(off, xr, xi, acc, row)
        return xr[edge:edge + 1, :], xi[edge:edge + 1, :], acc

    acc0 = (jnp.zeros((8, lw), F32), jnp.zeros((8, lw), F32)) if extra is not None else 0
    cr, ci, acc = lax.fori_loop(0, nt, tile, (carry_s[0:1, ls], carry_s[1:2, ls], acc0))
    carry_s[0:1, ls] = cr
    carry_s[1:2, ls] = ci
    return acc


def _s5_fwd(proj, bdr, bdi, pw, tr, ti, cdr, cdi, dskip, wglu, *, name):
    L = proj.shape[0]
    GW, S = bdr.shape
    TL = _tile(L, S5_TL)
    lw = min(S, S5_LW)

    def body(u_ref, bdr_ref, bdi_ref, pw_ref, tr_ref, ti_ref, cdr_ref, cdi_ref, d_ref, w_ref,
             y_ref, xr_ref, xi_ref, re_s, im_s, carry_s):
        @pl.when(pl.program_id(0) == 0)
        def _():
            carry_s[...] = jnp.zeros_like(carry_s)

        u = u_ref[...]
        ub = u.astype(BF16)
        re_s[...] = _dot(ub, bdr_ref[...], _NN)
        im_s[...] = _dot(ub, bdi_ref[...], _NN)
        for lc in range(S // lw):
            _scan_tiles(re_s, im_s, slice(lc * lw, (lc + 1) * lw), lw, pw_ref, tr_ref, ti_ref, carry_s, TL // 8, False)
        xrb, xib = re_s[...].astype(BF16), im_s[...].astype(BF16)
        xr_ref[...] = xrb
        xi_ref[...] = xib
        y = _dot(xrb, cdr_ref[...], _NN) - _dot(xib, cdi_ref[...], _NN) + d_ref[...] * u
        g, _ = _gelu_and_grad(y)
        z = _dot(g.astype(BF16), w_ref[...], _NN)
        y_ref[...] = g * jax.nn.sigmoid(z)

    full = lambda arr: pl.BlockSpec(arr.shape, lambda i: (0,) * arr.ndim)
    return pl.pallas_call(
        body, grid=(L // TL,),
        in_specs=[pl.BlockSpec((TL, GW), lambda i: (i, 0))] + [full(t) for t in (bdr, bdi, pw, tr, ti, cdr, cdi, dskip, wglu)],
        out_specs=[pl.BlockSpec((TL, GW), lambda i: (i, 0)), pl.BlockSpec((TL, S), lambda i: (i, 0)), pl.BlockSpec((TL, S), lambda i: (i, 0))],
        out_shape=[jax.ShapeDtypeStruct((L, GW), F32), jax.ShapeDtypeStruct((L, S), BF16), jax.ShapeDtypeStruct((L, S), BF16)],
        scratch_shapes=[pltpu.VMEM((TL, S), F32), pltpu.VMEM((TL, S), F32), pltpu.VMEM((8, S), F32)],
        compiler_params=_cparams(("arbitrary",), VMEM_LIMIT), name=name,
    )(proj, bdr, bdi, pw, tr, ti, cdr, cdi, dskip, wglu)


def _s5_bwd(proj, xr, xi, dout, bdr, bdi, pwc, trc, tic, cdr, cdi, dskip, wglu, *, name):
    L = proj.shape[0]
    GW, S = bdr.shape
    TL = _tile(L, S5_TL)
    nc = L // TL
    lw = min(S, S5_LW)

    def body(u_ref, xr_ref, xi_ref, xrp_ref, xip_ref, do_ref, bdr_ref, bdi_ref, pw_ref, tr_ref, ti_ref, cdr_ref, cdi_ref,
             d_ref, w_ref, du_ref, dw_ref, dd_ref, dcdr_ref, dcdi_ref, dbdr_ref, dbdi_ref, dar_ref, dai_ref,
             gr_s, gi_s, xfr, xfi, carry_s):
        i = pl.program_id(0)

        @pl.when(i == 0)
        def _():
            carry_s[...] = jnp.zeros_like(carry_s)
            for r in (dw_ref, dd_ref, dcdr_ref, dcdi_ref, dbdr_ref, dbdi_ref, dar_ref, dai_ref):
                r[...] = jnp.zeros_like(r)

        u = u_ref[...]
        ub = u.astype(BF16)
        xrb, xib = xr_ref[...], xi_ref[...]
        y = _dot(xrb, cdr_ref[...], _NN) - _dot(xib, cdi_ref[...], _NN) + d_ref[...] * u
        g, gp = _gelu_and_grad(y)
        gb = g.astype(BF16)
        sg = jax.nn.sigmoid(_dot(gb, w_ref[...], _NN))
        dout = do_ref[...]
        dz = (dout * g * sg * (1.0 - sg)).astype(BF16)
        dg = dout * sg + _dot(dz, w_ref[...], _NT)
        dw_ref[...] += _dot(gb, dz, _TN)
        dy = dg * gp
        dyb = dy.astype(BF16)
        dd_ref[...] += jnp.sum(dy * u, axis=0, keepdims=True)
        gr_s[...] = _dot(dyb, cdr_ref[...], _NT)
        gi_s[...] = -_dot(dyb, cdi_ref[...], _NT)
        dcdr_ref[...] += _dot(xrb, dyb, _TN)
        dcdi_ref[...] -= _dot(xib, dyb, _TN)
        live = (i < nc - 1).astype(F32)
        xfr[0:8, :] = xrp_ref[...].astype(F32) * live
        xfi[0:8, :] = xip_ref[...].astype(F32) * live
        xfr[8:, :] = xrb.astype(F32)
        xfi[8:, :] = xib.astype(F32)
        for lc in range(S // lw):
            ls = slice(lc * lw, (lc + 1) * lw)

            def extra(off, lr, li, acc, row, ls=ls):
                cur_r, cur_i = xfr[pl.ds(off + 8, 8), ls], xfi[pl.ds(off + 8, 8), ls]
                prv_r, prv_i = xfr[pl.ds(off, 8), ls], xfi[pl.ds(off, 8), ls]
                xpr = jnp.where(row >= 1, pltpu.roll(cur_r, 1, 0), prv_r[7:8, :])
                xpi = jnp.where(row >= 1, pltpu.roll(cur_i, 1, 0), prv_i[7:8, :])
                return acc[0] + lr * xpr + li * xpi, acc[1] - lr * xpi + li * xpr

            acc = _scan_tiles(gr_s, gi_s, ls, lw, pw_ref, tr_ref, ti_ref, carry_s, TL // 8, True, extra)
            dar_ref[:, ls] += acc[0]
            dai_ref[:, ls] += acc[1]
        lrb, lib = gr_s[...].astype(BF16), gi_s[...].astype(BF16)
        du_ref[...] = dy * d_ref[...] + _dot(lrb, bdr_ref[...], _NT) + _dot(lib, bdi_ref[...], _NT)
        dbdr_ref[...] += _dot(ub, lrb, _TN)
        dbdi_ref[...] += _dot(ub, lib, _TN)

    rev = lambda i: nc - 1 - i
    full = lambda arr: pl.BlockSpec(arr.shape, lambda i: (0,) * arr.ndim)
    chunk = lambda w: pl.BlockSpec((TL, w), lambda i: (rev(i), 0))
    prev8 = pl.BlockSpec((8, S), lambda i: (jnp.maximum(rev(i) * (TL // 8) - 1, 0), 0))
    acc_shapes = [(GW, GW), (1, GW), (S, GW), (S, GW), (GW, S), (GW, S), (8, S), (8, S)]
    return pl.pallas_call(
        body, grid=(nc,),
        in_specs=[chunk(GW), chunk(S), chunk(S), prev8, prev8, chunk(GW)] + [full(t) for t in (bdr, bdi, pwc, trc, tic, cdr, cdi, dskip, wglu)],
        out_specs=[chunk(GW)] + [pl.BlockSpec(s, lambda i: (0, 0)) for s in acc_shapes],
        out_shape=[jax.ShapeDtypeStruct((L, GW), F32)] + [jax.ShapeDtypeStruct(s, F32) for s in acc_shapes],
        scratch_shapes=[pltpu.VMEM((TL, S), F32), pltpu.VMEM((TL, S), F32), pltpu.VMEM((TL + 8, S), F32),
                        pltpu.VMEM((TL + 8, S), F32), pltpu.VMEM((8, S), F32)],
        compiler_params=_cparams(("arbitrary",), VMEM_LIMIT), name=name,
    )(proj, xr, xi, xr, xi, dout, bdr, bdi, pwc, trc, tic, cdr, cdi, dskip, wglu)


def _cpow_tables(ar, ai):
    def cm(a, b):
        return a[0] * b[0] - a[1] * b[1], a[0] * b[1] + a[1] * b[0]
    p = [(ar, ai)]
    for _ in range(7):
        p.append(cm(p[-1], p[0]))
    z = jnp.zeros_like(ar)
    pw = jnp.stack([p[0][0], p[0][1], p[1][0], p[1][1], p[3][0], p[3][1], z, z])
    return pw, jnp.stack([q[0] for q in p]), jnp.stack([q[1] for q in p])


def _s5_disc(lam_re, lam_im, log_dt, b_re, b_im):
    dt = jnp.exp(log_dt)[:, None]
    mag = jnp.exp(lam_re * dt)
    ab_r, ab_i = mag * jnp.cos(lam_im * dt), mag * jnp.sin(lam_im * dt)
    den = lam_re * lam_re + lam_im * lam_im
    nr, ni = ab_r - 1.0, ab_i
    f_r = ((nr * lam_re + ni * lam_im) / den)[..., None]
    f_i = ((ni * lam_re - nr * lam_im) / den)[..., None]
    return ab_r, ab_i, f_r * b_re - f_i * b_im, f_r * b_im + f_i * b_re


def _bd(t):
    G, A, B = t.shape
    return (t[:, :, None, :] * jnp.eye(G, dtype=t.dtype)[:, None, :, None]).reshape(G * A, G * B)


def _bd_extract(m, G):
    A, B = m.shape[0] // G, m.shape[1] // G
    return jnp.sum(m.reshape(G, A, G, B) * jnp.eye(G, dtype=m.dtype)[:, None, :, None], axis=2)


POOL_TQ = 256


def _band(shape, row0, col0, win, transpose):
    r = lax.broadcasted_iota(jnp.int32, shape, 0) + row0
    c = lax.broadcasted_iota(jnp.int32, shape, 1) + col0
    d = (c - r) if transpose else (r - c)
    return ((d >= 0) & (d < win)).astype(BF16)


def _band_dot(band, v):
    hi = v.astype(BF16)
    lo = (v - hi.astype(F32)).astype(BF16)
    return _dot(band, hi, _NN) + _dot(band, lo, _NN)


def _pool_p(u_prev, u_cur, i, win, tq):
    t0 = i * tq
    cnt = jnp.minimum(lax.broadcasted_iota(jnp.int32, (tq, 1), 0) + t0 + 1, win).astype(F32)
    live = (i > 0).astype(F32)
    acc = _band_dot(_band((tq, tq), t0, t0, win, False), u_cur)
    acc += _band_dot(_band((tq, tq), t0, t0 - tq, win, False), u_prev * live)
    return acc / cnt - u_cur


def _pool_fwd(proj, pool_w, pool_scale3, *, name):
    L = proj.shape[0]
    nw, PC, _ = pool_w.shape
    tq = _tile(L, POOL_TQ)

    def body(up_ref, uc_ref, w_ref, s_ref, y_ref):
        g, i = pl.program_id(0), pl.program_id(1)
        win = jnp.left_shift(2, g)
        p = _pool_p(up_ref[...], uc_ref[...], i, win, tq)
        y_ref[...] = _dot(p.astype(BF16), w_ref[...].astype(BF16), _NN) * s_ref[...]

    return pl.pallas_call(
        body, grid=(nw, L // tq),
        in_specs=[pl.BlockSpec((tq, PC), lambda g, i: (jnp.maximum(i - 1, 0), nw + g)),
                  pl.BlockSpec((tq, PC), lambda g, i: (i, nw + g)),
                  pl.BlockSpec((None, PC, PC), lambda g, i: (g, 0, 0)),
                  pl.BlockSpec((None, 1, PC), lambda g, i: (g, 0, 0))],
        out_specs=pl.BlockSpec((tq, PC), lambda g, i: (i, g)),
        out_shape=jax.ShapeDtypeStruct((L, nw * PC), F32),
        compiler_params=_cparams(("parallel", "parallel")), name=name)(proj, proj, pool_w, pool_scale3)


def _pool_bwd(proj, dy, pool_w, pool_scale3, *, name):
    L = proj.shape[0]
    nw, PC, _ = pool_w.shape
    tq = _tile(L, POOL_TQ)
    nq = L // tq

    def body(up_ref, uc_ref, dyc_ref, dyn_ref, w_ref, s_ref, du_ref, dw_ref, ds_ref):
        g, i = pl.program_id(0), pl.program_id(1)
        win = jnp.left_shift(2, g)

        @pl.when(i == 0)
        def _():
            dw_ref[...] = jnp.zeros_like(dw_ref)
            ds_ref[...] = jnp.zeros_like(ds_ref)

        wb = w_ref[...].astype(BF16)
        p = _pool_p(up_ref[...], uc_ref[...], i, win, tq).astype(BF16)
        dyc = dyc_ref[...]
        ds_ref[...] += jnp.sum(dyc * _dot(p, wb, _NN), axis=0, keepdims=True)
        dys = (dyc * s_ref[...]).astype(BF16)
        dw_ref[...] += _dot(p, dys, _TN)
        t0 = i * tq
        rows = lax.broadcasted_iota(jnp.int32, (tq, 1), 0)
        dp_c = _dot(dys, wb, _NT)
        q_c = dp_c / jnp.minimum(rows + t0 + 1, win).astype(F32)
        live = (i < nq - 1).astype(F32)
        dp_n = _dot((dyn_ref[...] * s_ref[...] * live).astype(BF16), wb, _NT)
        q_n = dp_n / jnp.minimum(rows + t0 + tq + 1, win).astype(F32)
        du = _band_dot(_band((tq, tq), t0, t0, win, True), q_c) + _band_dot(_band((tq, tq), t0, t0 + tq, win, True), q_n)
        du_ref[...] = du - dp_c

    return pl.pallas_call(
        body, grid=(nw, nq),
        in_specs=[pl.BlockSpec((tq, PC), lambda g, i: (jnp.maximum(i - 1, 0), nw + g)),
                  pl.BlockSpec((tq, PC), lambda g, i: (i, nw + g)),
                  pl.BlockSpec((tq, PC), lambda g, i: (i, g)),
                  pl.BlockSpec((tq, PC), lambda g, i: (jnp.minimum(i + 1, nq - 1), g)),
                  pl.BlockSpec((None, PC, PC), lambda g, i: (g, 0, 0)),
                  pl.BlockSpec((None, 1, PC), lambda g, i: (g, 0, 0))],
        out_specs=[pl.BlockSpec((tq, PC), lambda g, i: (i, g)),
                   pl.BlockSpec((None, PC, PC), lambda g, i: (g, 0, 0)),
                   pl.BlockSpec((None, 1, PC), lambda g, i: (g, 0, 0))],
        out_shape=[jax.ShapeDtypeStruct((L, nw * PC), F32), jax.ShapeDtypeStruct((nw, PC, PC), F32),
                   jax.ShapeDtypeStruct((nw, 1, PC), F32)],
        compiler_params=_cparams(("parallel", "arbitrary")), name=name)(proj, proj, dy, dy, pool_w, pool_scale3)


CONV_RC = 256


def _conv1_fwd(proj, w_dw, b_dw, *, name):
    L = proj.shape[0]
    GW = w_dw.shape[1]
    CB = min(GW, 128)
    nb = GW // CB
    RC = _tile(L, CONV_RC)
    n = RC + CONV_PAD

    def body(val_ref, gate_ref, w_ref, b_ref, o_ref, hp):
        hp[0:CONV_PAD, :] = jnp.zeros((CONV_PAD, CB), F32)
        hp[CONV_PAD:, :] = val_ref[...] * jax.nn.sigmoid(gate_ref[...])
        wv = w_ref[...]

        def chunk(ci, carry):
            c0 = pl.multiple_of(ci * RC, 8)
            win = hp[pl.ds(c0, n), :]
            acc = jnp.zeros((RC, CB), F32) + b_ref[...]
            for k in range(CONV_WIDTH):
                acc = acc + wv[k:k + 1, :] * pltpu.roll(win, n - (k + 2), 0)[0:RC]
            o_ref[pl.ds(c0, RC), :] = acc
            return carry

        lax.fori_loop(0, L // RC, chunk, 0)

    col = lambda off: pl.BlockSpec((L, CB), lambda c: (0, off * nb + c))
    return pl.pallas_call(
        body, grid=(nb,),
        in_specs=[col(2), col(3), pl.BlockSpec((CONV_PAD, CB), lambda c: (0, c)), pl.BlockSpec((1, CB), lambda c: (0, c))],
        out_specs=pl.BlockSpec((L, CB), lambda c: (0, c)),
        out_shape=jax.ShapeDtypeStruct((L, GW), F32),
        scratch_shapes=[pltpu.VMEM((L + CONV_PAD, CB), F32)],
        compiler_params=_cparams(("parallel",)), name=name)(proj, proj, w_dw, b_dw)


def _conv1_bwd(proj, dhc, w_dw, *, name):
    L = proj.shape[0]
    GW = w_dw.shape[1]
    CB = min(GW, 128)
    nb = GW // CB
    RC = _tile(L, CONV_RC)
    n = RC + CONV_PAD

    def body(val_ref, gate_ref, d_ref, w_ref, dval_ref, dgate_ref, dw_ref, db_ref, hp, dp):
        val = val_ref[...]
        sg = jax.nn.sigmoid(gate_ref[...])
        hp[0:CONV_PAD, :] = jnp.zeros((CONV_PAD, CB), F32)
        hp[CONV_PAD:, :] = val * sg
        dp[0:L, :] = d_ref[...]
        dp[L:, :] = jnp.zeros((CONV_PAD, CB), F32)
        dw_ref[...] = jnp.zeros_like(dw_ref)
        db_ref[...] = jnp.sum(d_ref[...], axis=0, keepdims=True)
        wv = w_ref[...]

        def chunk(ci, carry):
            c0 = pl.multiple_of(ci * RC, 8)
            win = hp[pl.ds(c0, n), :]
            dwin = dp[pl.ds(c0, n), :]
            d = dwin[0:RC]
            dh = jnp.zeros((RC, CB), F32)
            for k in range(CONV_WIDTH):
                dw_ref[k:k + 1, :] += jnp.sum(d * pltpu.roll(win, n - (k + 2), 0)[0:RC], axis=0, keepdims=True)
                sh = CONV_WIDTH - 1 - k
                dh = dh + wv[k:k + 1, :] * (pltpu.roll(dwin, n - sh, 0)[0:RC] if sh else d)
            v, s = val_ref[pl.ds(c0, RC), :], jax.nn.sigmoid(gate_ref[pl.ds(c0, RC), :])
            dval_ref[pl.ds(c0, RC), :] = dh * s
            dgate_ref[pl.ds(c0, RC), :] = dh * v * s * (1.0 - s)
            return carry

        lax.fori_loop(0, L // RC, chunk, 0)

    col = lambda off: pl.BlockSpec((L, CB), lambda c: (0, off * nb + c))
    own = pl.BlockSpec((L, CB), lambda c: (0, c))
    return pl.pallas_call(
        body, grid=(nb,),
        in_specs=[col(2), col(3), own, pl.BlockSpec((CONV_PAD, CB), lambda c: (0, c))],
        out_specs=[own, own, pl.BlockSpec((CONV_PAD, CB), lambda c: (0, c)), pl.BlockSpec((1, CB), lambda c: (0, c))],
        out_shape=[jax.ShapeDtypeStruct((L, GW), F32), jax.ShapeDtypeStruct((L, GW), F32),
                   jax.ShapeDtypeStruct((CONV_PAD, GW), F32), jax.ShapeDtypeStruct((1, GW), F32)],
        scratch_shapes=[pltpu.VMEM((L + CONV_PAD, CB), F32), pltpu.VMEM((L + CONV_PAD, CB), F32)],
        compiler_params=_cparams(("parallel",)), name=name)(proj, proj, dhc, w_dw)


def _ln_silu(hc, g, b):
    mu = jnp.mean(hc, axis=-1, keepdims=True)
    xc = hc - mu
    r = lax.rsqrt(jnp.mean(xc * xc, axis=-1, keepdims=True) + NORM_EPS)
    xh = xc * r
    a = xh * g + b
    sg = jax.nn.sigmoid(a)
    return xh, r, a, sg


def _conv2_fwd(hc, ln_g, ln_b, w_pw, *, name):
    L, GW = hc.shape
    tr = _tile(L, 256)

    def body(h_ref, g_ref, b_ref, w_ref, y_ref):
        _, _, a, sg = _ln_silu(h_ref[...], g_ref[...], b_ref[...])
        y_ref[...] = _dot((a * sg).astype(BF16), w_ref[...], _NN)

    row = pl.BlockSpec((tr, GW), lambda i: (i, 0))
    vec = pl.BlockSpec((1, GW), lambda i: (0, 0))
    return pl.pallas_call(
        body, grid=(L // tr,), in_specs=[row, vec, vec, pl.BlockSpec((GW, GW), lambda i: (0, 0))],
        out_specs=row, out_shape=jax.ShapeDtypeStruct((L, GW), F32),
        compiler_params=_cparams(("parallel",)), name=name)(hc, ln_g, ln_b, w_pw)


def _conv2_bwd(hc, ln_g, ln_b, w_pw, dy, *, name):
    L, GW = hc.shape
    tr = _tile(L, 256)

    def body(h_ref, g_ref, b_ref, w_ref, dy_ref, dh_ref, dg_ref, db_ref, dw_ref):
        @pl.when(pl.program_id(0) == 0)
        def _():
            for r in (dg_ref, db_ref, dw_ref):
                r[...] = jnp.zeros_like(r)

        xh, r, a, sg = _ln_silu(h_ref[...], g_ref[...], b_ref[...])
        dyb = dy_ref[...].astype(BF16)
        dw_ref[...] += _dot((a * sg).astype(BF16), dyb, _TN)
        da = _dot(dyb, w_ref[...], _NT) * (sg + a * sg * (1.0 - sg))
        dg_ref[...] += jnp.sum(da * xh, axis=0, keepdims=True)
        db_ref[...] += jnp.sum(da, axis=0, keepdims=True)
        dxh = da * g_ref[...]
        dh_ref[...] = r * (dxh - jnp.mean(dxh, axis=-1, keepdims=True) - xh * jnp.mean(dxh * xh, axis=-1, keepdims=True))

    row = pl.BlockSpec((tr, GW), lambda i: (i, 0))
    vec = pl.BlockSpec((1, GW), lambda i: (0, 0))
    mat = pl.BlockSpec((GW, GW), lambda i: (0, 0))
    return pl.pallas_call(
        body, grid=(L // tr,), in_specs=[row, vec, vec, mat, row],
        out_specs=[row, vec, vec, mat],
        out_shape=[jax.ShapeDtypeStruct((L, GW), F32), jax.ShapeDtypeStruct((1, GW), F32), jax.ShapeDtypeStruct((1, GW), F32),
                   jax.ShapeDtypeStruct((GW, GW), F32)],
        compiler_params=_cparams(("arbitrary",)), name=name)(hc, ln_g, ln_b, w_pw, dy)


def _grp_fwd(ys, g, *, name):
    L, GW = ys[0].shape
    tr = _tile(L, 256)

    def body(*refs):
        g_ref, o_ref = refs[4], refs[5]
        for m in range(N_MIXERS):
            yv = refs[m][...]
            r = lax.rsqrt(jnp.mean(yv * yv, axis=-1, keepdims=True) + NORM_EPS)
            o_ref[:, m * GW:(m + 1) * GW] = (yv * r * g_ref[:, m * GW:(m + 1) * GW]).astype(o_ref.dtype)

    row = pl.BlockSpec((tr, GW), lambda i: (i, 0))
    return pl.pallas_call(
        body, grid=(L // tr,), in_specs=[row] * 4 + [pl.BlockSpec((1, N_MIXERS * GW), lambda i: (0, 0))],
        out_specs=pl.BlockSpec((tr, N_MIXERS * GW), lambda i: (i, 0)),
        out_shape=jax.ShapeDtypeStruct((L, N_MIXERS * GW), BF16),
        compiler_params=_cparams(("parallel",)), name=name)(*ys, g)


def _grp_bwd(ys, g, dy, *, name):
    L, GW = ys[0].shape
    tr = _tile(L, 256)

    def body(*refs):
        g_ref, dy_ref = refs[4], refs[5]
        outs, dg_ref = refs[6:10], refs[10]

        @pl.when(pl.program_id(0) == 0)
        def _():
            dg_ref[...] = jnp.zeros_like(dg_ref)

        for m in range(N_MIXERS):
            cs = slice(m * GW, (m + 1) * GW)
            yv = refs[m][...]
            r = lax.rsqrt(jnp.mean(yv * yv, axis=-1, keepdims=True) + NORM_EPS)
            xh = yv * r
            dyv = dy_ref[:, cs]
            dyg = dyv * g_ref[:, cs]
            outs[m][...] = r * (dyg - xh * jnp.mean(dyg * xh, axis=-1, keepdims=True))
            dg_ref[:, cs] += jnp.sum(dyv * xh, axis=0, keepdims=True)

    row = pl.BlockSpec((tr, GW), lambda i: (i, 0))
    wide = pl.BlockSpec((tr, N_MIXERS * GW), lambda i: (i, 0))
    vec = pl.BlockSpec((1, N_MIXERS * GW), lambda i: (0, 0))
    return pl.pallas_call(
        body, grid=(L // tr,), in_specs=[row] * 4 + [vec, wide],
        out_specs=[row] * 4 + [vec],
        out_shape=[jax.ShapeDtypeStruct((L, GW), F32)] * 4 + [jax.ShapeDtypeStruct((1, N_MIXERS * GW), F32)],
        compiler_params=_cparams(("arbitrary",)), name=name)(*ys, g, dy)


def _att_first(j, br, nblk):
    per = lax.shift_right_logical(jnp.int32(nblk), 2 * br)
    return jnp.bitwise_and(j, per - 1) == 0


def _att_fwd(q, k, vx, bias, *, name):
    nbr, H, L, E = q.shape
    B = ATT_BLOCK
    nblk = L // B

    def body(q_ref, k_ref, v_ref, b_ref, o_ref):
        br = pl.program_id(0)
        lane = lax.broadcasted_iota(jnp.int32, (B, 2 * E), 1)

        def blk(j, carry):
            off = pl.multiple_of(j * B, B)
            poff = pl.multiple_of(jnp.maximum(j - 1, 0) * B, B)
            qv = q_ref[pl.ds(off, B), :]
            s_p = _dot(qv, k_ref[pl.ds(poff, B), :], _NT) + b_ref[:, 0:B]
            s_c = _dot(qv, k_ref[pl.ds(off, B), :], _NT) + b_ref[:, B:2 * B]
            s_p = jnp.where(_att_first(j, br, nblk), NEG_INF, s_p)
            m = jnp.maximum(jnp.max(s_p, axis=-1, keepdims=True), jnp.max(s_c, axis=-1, keepdims=True))
            p_p, p_c = jnp.exp(s_p - m), jnp.exp(s_c - m)
            den = jnp.sum(p_p, axis=-1, keepdims=True) + jnp.sum(p_c, axis=-1, keepdims=True)
            acc = _dot(p_p.astype(BF16), v_ref[pl.ds(poff, B), :], _NN) + _dot(p_c.astype(BF16), v_ref[pl.ds(off, B), :], _NN)
            o_ref[pl.ds(off, B), :] = jnp.where(lane < E, acc / den, m + jnp.log(den))
            return carry

        lax.fori_loop(0, nblk, blk, 0)

    seq = lambda w: pl.BlockSpec((None, None, L, w), lambda b, h: (b, h, 0, 0))
    return pl.pallas_call(
        body, grid=(nbr, H),
        in_specs=[seq(E), seq(E), seq(2 * E), pl.BlockSpec((None, None, B, 2 * B), lambda b, h: (b, h, 0, 0))],
        out_specs=seq(2 * E), out_shape=jax.ShapeDtypeStruct((nbr, H, L, 2 * E), F32),
        compiler_params=_cparams(("parallel", "parallel")), name=name)(q, k, vx, bias)


def _att_bwd(q, k, vx, bias, ox, dox, *, name):
    nbr, H, L, E = q.shape
    B = ATT_BLOCK
    nblk = L // B

    def body(q_ref, k_ref, v_ref, b_ref, o_ref, do_ref, dq_ref, dk_ref, dv_ref, db_ref):
        br = pl.program_id(0)
        dk_ref[...] = jnp.zeros_like(dk_ref)
        dv_ref[...] = jnp.zeros_like(dv_ref)
        db_ref[...] = jnp.zeros_like(db_ref)
        lane = lax.broadcasted_iota(jnp.int32, (B, 2 * E), 1)

        def blk(j, carry):
            off = pl.multiple_of(j * B, B)
            poff = pl.multiple_of(jnp.maximum(j - 1, 0) * B, B)
            qv, kc, kp = q_ref[pl.ds(off, B), :], k_ref[pl.ds(off, B), :], k_ref[pl.ds(poff, B), :]
            vc, vp = v_ref[pl.ds(off, B), :], v_ref[pl.ds(poff, B), :]
            oe, de = o_ref[pl.ds(off, B), :], do_ref[pl.ds(off, B), :]
            lse, dlse = oe[:, E:E + 1], de[:, E:E + 1]
            do = jnp.where(lane < E, de, 0.0)
            dm = jnp.sum(do * oe, axis=-1, keepdims=True) - dlse
            dob = do.astype(BF16)
            s_p = _dot(qv, kp, _NT) + b_ref[:, 0:B]
            s_c = _dot(qv, kc, _NT) + b_ref[:, B:2 * B]
            s_p = jnp.where(_att_first(j, br, nblk), NEG_INF, s_p)
            p_p, p_c = jnp.exp(s_p - lse), jnp.exp(s_c - lse)
            ds_p = p_p * (_dot(dob, vp, _NT) - dm)
            ds_c = p_c * (_dot(dob, vc, _NT) - dm)
            db_ref[:, 0:B] += ds_p
            db_ref[:, B:2 * B] += ds_c
            dsb_p, dsb_c = ds_p.astype(BF16), ds_c.astype(BF16)
            dq_ref[pl.ds(off, B), :] = _dot(dsb_p, kp, _NN) + _dot(dsb_c, kc, _NN)
            dk_ref[pl.ds(poff, B), :] += _dot(dsb_p, qv, _TN)
            dk_ref[pl.ds(off, B), :] += _dot(dsb_c, qv, _TN)
            dv_ref[pl.ds(poff, B), :] += _dot(p_p.astype(BF16), dob, _TN)
            dv_ref[pl.ds(off, B), :] += _dot(p_c.astype(BF16), dob, _TN)
            return carry

        lax.fori_loop(0, nblk, blk, 0)

    seq = lambda w: pl.BlockSpec((None, None, L, w), lambda b, h: (b, h, 0, 0))
    bsp = pl.BlockSpec((None, None, B, 2 * B), lambda b, h: (b, h, 0, 0))
    return pl.pallas_call(
        body, grid=(nbr, H),
        in_specs=[seq(E), seq(E), seq(2 * E), bsp, seq(2 * E), seq(2 * E)],
        out_specs=[seq(E), seq(E), seq(2 * E), bsp],
        out_shape=[jax.ShapeDtypeStruct((nbr, H, L, E), F32), jax.ShapeDtypeStruct((nbr, H, L, E), F32),
                   jax.ShapeDtypeStruct((nbr, H, L, 2 * E), F32), jax.ShapeDtypeStruct((nbr, H, B, 2 * B), F32)],
        compiler_params=_cparams(("parallel", "parallel")), name=name)(q, k, vx, bias, ox, dox)


def _mix_weights(xs, lane, E, W):
    al = [jnp.where(lane < E, pltpu.roll(x, W - E, 1), x) for x in xs]
    m = functools.reduce(jnp.maximum, al)
    es = [jnp.exp(a - m) for a in al]
    tot = functools.reduce(lambda a, b: a + b, es)
    return [e / tot for e in es]


def _mix_fwd(ox, *, name):
    nbr, L, W = ox.shape
    E = W // ATT_HEADS // 2
    tr = _tile(L, 256)

    def body(x_ref, o_ref):
        lane = lax.broadcasted_iota(jnp.int32, (tr, W), 1) % (2 * E)
        xs = [x_ref[b] for b in range(nbr)]
        ws = _mix_weights(xs, lane, E, W)
        o_ref[...] = functools.reduce(lambda a, b: a + b, [w * x for w, x in zip(ws, xs)])

    return pl.pallas_call(
        body, grid=(L // tr,), in_specs=[pl.BlockSpec((nbr, tr, W), lambda i: (0, i, 0))],
        out_specs=pl.BlockSpec((tr, W), lambda i: (i, 0)), out_shape=jax.ShapeDtypeStruct((L, W), F32),
        compiler_params=_cparams(("parallel",)), name=name)(ox)


def _mix_bwd(ox, dy, *, name):
    nbr, L, W = ox.shape
    E = W // ATT_HEADS // 2
    tr = _tile(L, 256)

    def body(x_ref, dy_ref, o_ref):
        lane = lax.broadcasted_iota(jnp.int32, (tr, W), 1) % (2 * E)
        xs = [x_ref[b] for b in range(nbr)]
        ws = _mix_weights(xs, lane, E, W)
        mix = functools.reduce(lambda a, b: a + b, [w * x for w, x in zip(ws, xs)])
        dyv = jnp.where(lane < E, dy_ref[...], 0.0)
        r = lax.broadcasted_iota(jnp.int32, (W, W), 0)
        c = lax.broadcasted_iota(jnp.int32, (W, W), 1)
        seg = ((r // (2 * E) == c // (2 * E)) & (r % (2 * E) < E)).astype(F32)
        for b in range(nbr):
            t = dyv * (xs[b] - mix)
            dl = ws[b] * jnp.dot(t, seg, preferred_element_type=F32, precision=lax.Precision.HIGHEST)
            o_ref[b] = jnp.where(lane < E, ws[b] * dyv, dl)

    return pl.pallas_call(
        body, grid=(L // tr,),
        in_specs=[pl.BlockSpec((nbr, tr, W), lambda i: (0, i, 0)), pl.BlockSpec((tr, W), lambda i: (i, 0))],
        out_specs=pl.BlockSpec((nbr, tr, W), lambda i: (0, i, 0)), out_shape=jax.ShapeDtypeStruct((nbr, L, W), F32),
        compiler_params=_cparams(("parallel",), VMEM_LIMIT), name=name)(ox, dy)


def _t5_bucket(dist):
    n = np.maximum(dist, 0)
    max_exact = REL_BUCKETS // 2
    large = max_exact + (np.log(np.maximum(n, 1) / max_exact) / np.log(REL_MAX_DIST / max_exact)
                         * (REL_BUCKETS - max_exact)).astype(np.int64)
    large = np.minimum(large, REL_BUCKETS - 1)
    return np.where(n < max_exact, n, large).astype(np.int32)


def _att_tables():
    a = np.arange(ATT_BLOCK)[:, None]
    b = np.arange(2 * ATT_BLOCK)[None, :]
    sub = a + ATT_BLOCK - b
    buckets, valids = [], []
    for window, dil in DILATED_PATTERNS:
        buckets.append(_t5_bucket(sub * dil))
        valids.append((sub >= 0) & (sub <= window // dil))
    return np.stack(buckets), np.stack(valids)


def _to_branches(t, H):
    L = t.shape[0]
    E = t.shape[1] // H
    out = []
    for _, dil in DILATED_PATTERNS:
        out.append(t.reshape(L // dil, dil, H, E).transpose(2, 1, 0, 3).reshape(H, L, E))
    return jnp.stack(out)


def _from_branches(t):
    _, H, L, E = t.shape
    out = []
    for b, (_, dil) in enumerate(DILATED_PATTERNS):
        out.append(t[b].reshape(H, dil, L // dil, E).transpose(2, 1, 0, 3).reshape(L, H * E))
    return jnp.stack(out)


def _xatt_fwd(q, k, v, *, name):
    L, XW = q.shape
    M = k.shape[0]
    tr = _tile(L, 512)
    scale = X_HEAD_DIM ** -0.5

    def body(q_ref, k_ref, v_ref, o_ref):
        s = _dot(q_ref[...], k_ref[...], _NT) * scale
        p = jnp.exp(s - jnp.max(s, axis=-1, keepdims=True))
        den = jnp.sum(p, axis=-1, keepdims=True)
        o_ref[...] = (_dot(p.astype(BF16), v_ref[...], _NN) / den).astype(o_ref.dtype)

    qs = pl.BlockSpec((tr, X_HEAD_DIM), lambda h, i: (i, h))
    ks = pl.BlockSpec((M, X_HEAD_DIM), lambda h, i: (0, h))
    return pl.pallas_call(
        body, grid=(XW // X_HEAD_DIM, L // tr), in_specs=[qs, ks, ks], out_specs=qs,
        out_shape=jax.ShapeDtypeStruct((L, XW), BF16),
        compiler_params=_cparams(("parallel", "parallel")), name=name)(q, k, v)


def _xatt_bwd(q, k, v, do, *, name):
    L, XW = q.shape
    M = k.shape[0]
    tr = _tile(L, 512)
    scale = X_HEAD_DIM ** -0.5

    def body(q_ref, k_ref, v_ref, do_ref, dq_ref, dk_ref, dv_ref):
        @pl.when(pl.program_id(1) == 0)
        def _():
            dk_ref[...] = jnp.zeros_like(dk_ref)
            dv_ref[...] = jnp.zeros_like(dv_ref)

        qv, kv, vv = q_ref[...], k_ref[...], v_ref[...]
        s = _dot(qv, kv, _NT) * scale
        p = jnp.exp(s - jnp.max(s, axis=-1, keepdims=True))
        p = p / jnp.sum(p, axis=-1, keepdims=True)
        dob = do_ref[...].astype(BF16)
        pb = p.astype(BF16)
        dv_ref[...] += _dot(pb, dob, _TN)
        dp = _dot(dob, vv, _NT)
        ds = (p * (dp - jnp.sum(dp * p, axis=-1, keepdims=True)) * scale).astype(BF16)
        dq_ref[...] = _dot(ds, kv, _NN)
        dk_ref[...] += _dot(ds, qv, _TN)

    qs = pl.BlockSpec((tr, X_HEAD_DIM), lambda h, i: (i, h))
    ks = pl.BlockSpec((M, X_HEAD_DIM), lambda h, i: (0, h))
    return pl.pallas_call(
        body, grid=(XW // X_HEAD_DIM, L // tr), in_specs=[qs, ks, ks, qs], out_specs=[qs, ks, ks],
        out_shape=[jax.ShapeDtypeStruct((L, XW), F32), jax.ShapeDtypeStruct((M, XW), F32), jax.ShapeDtypeStruct((M, XW), F32)],
        compiler_params=_cparams(("parallel", "arbitrary")), name=name)(q, k, v, do)


_ANY = pl.BlockSpec(memory_space=pl.ANY)


def _place():
    mx, my, mc = lax.axis_index("x"), lax.axis_index("y"), lax.axis_index("c")
    chips = [(1 - mx, my), (mx, 1 - my), (1 - mx, 1 - my)]
    return mx, my, mc, chips


def _rcopy(src, dst, ssem, rsem, dev):
    return pltpu.make_async_remote_copy(src_ref=src, dst_ref=dst, send_sem=ssem, recv_sem=rsem, device_id=dev,
                                        device_id_type=MESH)


STAGE_BYTES = 2 * 1024 * 1024


def _staged_copy(pairs, buf, isem, osem):
    n = len(pairs)
    ins = [pltpu.make_async_copy(src, buf.at[i % 2], isem.at[i % 2]) for i, (src, _) in enumerate(pairs)]
    outs = [pltpu.make_async_copy(buf.at[i % 2], dst, osem.at[i % 2]) for i, (_, dst) in enumerate(pairs)]
    ins[0].start()
    for i in range(n):
        if i + 1 < n:
            if i >= 1:
                outs[i - 1].wait()
            ins[i + 1].start()
        ins[i].wait()
        outs[i].start()
    for i in range(max(n - 2, 0), n):
        outs[i].wait()


def _ag_w(x, *, name):
    R, C = x.shape
    R2 = R // 2

    ch = _rows_tile(R2, max(8, STAGE_BYTES // (C * x.dtype.itemsize)))

    def body(x_ref, o_ref, ssem, rsem, buf, isem, osem):
        mx, my, mc, chips = _place()
        me = 2 * mx + my
        sib = (mx, my, 1 - mc)
        mine = x_ref.at[pl.ds(mc * R2, R2)]
        sends = [_rcopy(mine, o_ref.at[me, mc], ssem.at[j], rsem.at[j], (px, py, mc)) for j, (px, py) in enumerate(chips)]
        for cp in sends:
            cp.start()
        _staged_copy([(x_ref.at[pl.ds(h * R2 + r, ch)], o_ref.at[me, h, pl.ds(r, ch)]) for h in range(2) for r in range(0, R2, ch)],
                     buf, isem, osem)
        passed = []
        for j, (px, py) in enumerate(chips):
            got = o_ref.at[2 * px + py, mc]
            _rcopy(mine, got, ssem.at[j], rsem.at[j], (px, py, mc)).wait_recv()
            cp = _rcopy(got, got, ssem.at[3 + j], rsem.at[3 + j], sib)
            cp.start()
            passed.append(cp)
        for j, (px, py) in enumerate(chips):
            theirs = o_ref.at[2 * px + py, 1 - mc]
            _rcopy(theirs, theirs, ssem.at[3 + j], rsem.at[3 + j], sib).wait_recv()
        for cp in sends + passed:
            cp.wait_send()

    return pl.pallas_call(
        body, in_specs=[_ANY], out_specs=_ANY, out_shape=jax.ShapeDtypeStruct((N_CHIPS, 2, R2, C), x.dtype),
        scratch_shapes=[pltpu.SemaphoreType.DMA((6,)), pltpu.SemaphoreType.DMA((6,)), pltpu.VMEM((2, ch, C), x.dtype),
                        pltpu.SemaphoreType.DMA((2,)), pltpu.SemaphoreType.DMA((2,))],
        name=name)(x)


def _ag4(x, *, name):
    def body(x_ref, o_ref, ssem, rsem, lsem):
        mx, my, mc, chips = _place()
        me = 2 * mx + my
        loc = pltpu.make_async_copy(x_ref, o_ref.at[me], lsem)
        loc.start()
        sends = [_rcopy(x_ref, o_ref.at[me], ssem.at[j], rsem.at[j], (px, py, mc)) for j, (px, py) in enumerate(chips)]
        for cp in sends:
            cp.start()
        for j, (px, py) in enumerate(chips):
            _rcopy(x_ref, o_ref.at[2 * px + py], ssem.at[j], rsem.at[j], (px, py, mc)).wait_recv()
        for cp in sends:
            cp.wait_send()
        loc.wait()

    return pl.pallas_call(
        body, in_specs=[_ANY], out_specs=_ANY, out_shape=jax.ShapeDtypeStruct((N_CHIPS,) + x.shape, x.dtype),
        scratch_shapes=[pltpu.SemaphoreType.DMA((3,)), pltpu.SemaphoreType.DMA((3,)), pltpu.SemaphoreType.DMA(())],
        name=name)(x)


def _rs_sib(gs, *, name):
    nl = len(gs)
    _, R, C = gs[0].shape
    R2 = R // 2
    n = nl * N_CHIPS

    def body(*refs):
        g_refs, (theirs_ref, ssem, rsem) = refs[:nl], refs[nl:]
        mx, my, mc, _ = _place()
        sib = (mx, my, 1 - mc)
        sends = [_rcopy(g_refs[l].at[k, pl.ds((1 - mc) * R2, R2)], theirs_ref.at[l, k], ssem.at[l * N_CHIPS + k],
                        rsem.at[l * N_CHIPS + k], sib) for l in range(nl) for k in range(N_CHIPS)]
        for cp in sends:
            cp.start()
        for l in range(nl):
            for k in range(N_CHIPS):
                t = theirs_ref.at[l, k]
                _rcopy(t, t, ssem.at[l * N_CHIPS + k], rsem.at[l * N_CHIPS + k], sib).wait_recv()
        for cp in sends:
            cp.wait_send()

    return pl.pallas_call(
        body, in_specs=[_ANY] * nl, out_specs=_ANY, out_shape=jax.ShapeDtypeStruct((nl, N_CHIPS, R2, C), gs[0].dtype),
        scratch_shapes=[pltpu.SemaphoreType.DMA((n,)), pltpu.SemaphoreType.DMA((n,))], name=name)(*gs)


def _a2a4(h, *, name):
    nl = h.shape[0]
    n = nl * (N_CHIPS - 1)

    def body(h_ref, o_ref, ssem, rsem):
        mx, my, mc, chips = _place()
        me = 2 * mx + my
        sends = [_rcopy(h_ref.at[l, 2 * px + py], o_ref.at[l, me], ssem.at[3 * l + j], rsem.at[3 * l + j], (px, py, mc))
                 for l in range(nl) for j, (px, py) in enumerate(chips)]
        for cp in sends:
            cp.start()
        for l in range(nl):
            for j, (px, py) in enumerate(chips):
                t = o_ref.at[l, 2 * px + py]
                _rcopy(t, t, ssem.at[3 * l + j], rsem.at[3 * l + j], (px, py, mc)).wait_recv()
        for cp in sends:
            cp.wait_send()

    return pl.pallas_call(
        body, in_specs=[_ANY], out_specs=_ANY, out_shape=jax.ShapeDtypeStruct(h.shape, h.dtype),
        scratch_shapes=[pltpu.SemaphoreType.DMA((n,)), pltpu.SemaphoreType.DMA((n,))], name=name)(h)


def _sib_fill(buf, *, name):
    nl = buf.shape[1]

    def body(b_ref, o_ref, ssem, rsem):
        mx, my, mc, _ = _place()
        sib = (mx, my, 1 - mc)
        sends = [_rcopy(o_ref.at[mc, l], o_ref.at[mc, l], ssem.at[l], rsem.at[l], sib) for l in range(nl)]
        for cp in sends:
            cp.start()
        for l in range(nl):
            t = o_ref.at[1 - mc, l]
            _rcopy(t, t, ssem.at[l], rsem.at[l], sib).wait_recv()
        for cp in sends:
            cp.wait_send()

    return pl.pallas_call(
        body, in_specs=[_ANY], out_specs=_ANY, out_shape=jax.ShapeDtypeStruct(buf.shape, buf.dtype),
        input_output_aliases={0: 0},
        scratch_shapes=[pltpu.SemaphoreType.DMA((nl,)), pltpu.SemaphoreType.DMA((nl,))], name=name)(buf)


def _rows_tile(n, pref=512):
    t = min(n, pref)
    while n % t or (t % 8 and t != n):
        t -= 1
    return t


def _rs_add(gs, theirs, c_arr, *, name):
    nl = len(gs)
    _, R, C = gs[0].shape
    R2 = R // 2
    tr = _rows_tile(R2, 256)
    nb = R2 // tr

    def body(c_ref, *refs):
        g_refs, t_ref, o_ref = refs[:nl], refs[nl], refs[nl + 1]
        l = pl.program_id(0)
        for ll in range(nl):
            @pl.when(l == ll)
            def _(ll=ll):
                o_ref[...] = (g_refs[ll][...] + t_ref[...]).astype(o_ref.dtype)

    def g_spec(ll):
        return pl.BlockSpec((None, tr, C), lambda l, k, i, c: (jnp.where(l == ll, k, 0), jnp.where(l == ll, c[0] * nb + i, 0), 0))

    blk = pl.BlockSpec((None, None, tr, C), lambda l, k, i, c: (l, k, i, 0))
    return pl.pallas_call(
        body,
        grid_spec=pltpu.PrefetchScalarGridSpec(
            num_scalar_prefetch=1, grid=(nl, N_CHIPS, nb), in_specs=[g_spec(ll) for ll in range(nl)] + [blk], out_specs=blk),
        out_shape=jax.ShapeDtypeStruct((nl, N_CHIPS, R2, C), BF16),
        compiler_params=_cparams(("arbitrary", "arbitrary", "arbitrary")), name=name)(c_arr, *gs, theirs)


def _rs_sum(chip_sum, got, mc_arr, *, name):
    nl, _, R2, C = chip_sum.shape
    tr = _rows_tile(R2, 256)

    def body(s_ref, own_ref, g0, g1, g2, g3, o_ref):
        me = s_ref[0]
        acc = jnp.zeros((tr, C), F32)
        for k, g_ref in enumerate((g0, g1, g2, g3)):
            acc = acc + jnp.where(me == k, own_ref[...], g_ref[...]).astype(F32)
        o_ref[...] = acc

    def got_spec(k):
        return pl.BlockSpec((None, None, tr, C), lambda l, i, s: (l, jnp.where(s[0] == k, (k + 1) % N_CHIPS, k), i, 0))

    return pl.pallas_call(
        body,
        grid_spec=pltpu.PrefetchScalarGridSpec(
            num_scalar_prefetch=1, grid=(nl, R2 // tr),
            in_specs=[pl.BlockSpec((None, None, tr, C), lambda l, i, s: (l, s[0], i, 0))] + [got_spec(k) for k in range(N_CHIPS)],
            out_specs=pl.BlockSpec((None, None, tr, C), lambda l, i, s: (s[1], l, i, 0))),
        out_shape=jax.ShapeDtypeStruct((2, nl, R2, C), F32),
        compiler_params=_cparams(("arbitrary", "arbitrary")), name=name)(mc_arr, chip_sum, got, got, got, got)


def _adamw(w, m, v, g, *, name):
    A, B, R, C = w.shape
    tr = _rows_tile(R, 256)
    c1 = 1.0 / (1.0 - ADAM_B1 ** ADAM_STEP)
    c2 = 1.0 / (1.0 - ADAM_B2 ** ADAM_STEP)

    def body(w_ref, m_ref, v_ref, g_ref, go_ref, d_ref, mo_ref, vo_ref):
        gv = g_ref[...]
        mn = ADAM_B1 * m_ref[...] + (1.0 - ADAM_B1) * gv
        vn = ADAM_B2 * v_ref[...] + (1.0 - ADAM_B2) * (gv * gv)
        go_ref[...] = gv
        mo_ref[...] = mn
        vo_ref[...] = vn
        d_ref[...] = -ADAM_LR * ((mn * c1) / (jnp.sqrt(vn * c2) + ADAM_EPS) + ADAM_WD * w_ref[...])

    ws = pl.BlockSpec((None, None, tr, C), lambda a, b, i: (a, b, i, 0))
    gs = pl.BlockSpec((None, None, tr, C), lambda a, b, i: (b, a, i, 0))
    shp = jax.ShapeDtypeStruct(w.shape, F32)
    return pl.pallas_call(
        body, grid=(A, B, R // tr), in_specs=[ws, ws, ws, gs], out_specs=[ws] * 4, out_shape=[shp] * 4,
        compiler_params=_cparams(("parallel", "parallel", "parallel")), name=name)(w, m, v, g)


def _reduce_grads(gs, tag):
    me = (2 * lax.axis_index("x") + lax.axis_index("y")).astype(jnp.int32)
    c = lax.axis_index("c").astype(jnp.int32)
    theirs = _rs_sib(gs, name=f"rs_sib_{tag}")
    chip_sum = _rs_add(gs, theirs, c.reshape(1), name=f"rs_add_{tag}")
    got = _a2a4(chip_sum, name=f"rs_a2a_{tag}")
    half = _rs_sum(chip_sum, got, jnp.stack([me, c]), name=f"rs_sum_{tag}")
    return _sib_fill(half, name=f"rs_fill_{tag}")


WEIGHTS = ["rel_bias", "mem_norm_g", "norm_mix_g", "w_in", "s5_lam_re", "s5_lam_im", "s5_log_dt", "s5_b_re", "s5_b_im",
           "s5_c_re", "s5_c_im", "s5_d", "s5_w_glu", "pool_w", "pool_scale", "conv_w_dw", "conv_b_dw", "conv_ln_g",
           "conv_ln_b", "conv_w_pw", "grp_norm_g", "w_out", "norm_x_g", "w_xq", "w_xk", "w_xv", "w_xo", "norm_mlp_g",
           "w_up", "w_down", "norm_final_g"]
SHARDED = {"w_in": "col", "s5_w_glu": "row", "conv_w_dw": "col", "conv_w_pw": "row", "w_out": "row", "w_xq": "row",
           "w_xk": "row", "w_xv": "row", "w_xo": "col", "w_up": "col", "w_down": "row"}
SMALL = [n for n in WEIGHTS if n not in SHARDED]
SMALL_ALIGN = N_CHIPS * 2 * 256 * 128


def _mat(w, kind):
    return w.reshape(w.shape[0] * w.shape[1], w.shape[2]) if kind == "row" else w


def _att_bias(rel_bias):
    bucket, valid = _att_tables()
    onehot = (jnp.asarray(bucket.reshape(-1))[:, None] == jnp.arange(REL_BUCKETS)[None, :]).astype(F32)
    b = jnp.dot(onehot, rel_bias, precision=lax.Precision.HIGHEST).reshape(bucket.shape + (rel_bias.shape[1],))
    return jnp.where(jnp.asarray(valid)[:, None], jnp.transpose(b, (0, 3, 1, 2)), NEG_INF)


def _rel_bias_grad(dbias):
    bucket, _ = _att_tables()
    nb, H = dbias.shape[0], dbias.shape[1]
    onehot = (jnp.asarray(bucket.reshape(-1))[:, None] == jnp.arange(REL_BUCKETS)[None, :]).astype(BF16)
    flat = jnp.transpose(dbias.reshape(nb, H, -1), (1, 0, 2)).reshape(H, -1)
    return _mm(flat, onehot, "nn", name="rel_bias_grad").T


def _layer_fwd(h, mem_n, bias, sp, bw, l):
    L, D = h.shape
    GW = D // N_MIXERS
    G = GW // S5_CH
    E = GW // ATT_HEADS
    sv = {"h": h}
    xn = _rms_fwd(h, sp["norm_mix_g"][l][None], name="norm_mix")
    proj = _mm(xn, bw["w_in"], "nn", name="proj_in")
    sv.update(xn=xn, proj=proj)
    disc, disc_vjp = jax.vjp(_s5_disc, sp["s5_lam_re"][l], sp["s5_lam_im"][l], sp["s5_log_dt"][l], sp["s5_b_re"][l], sp["s5_b_im"][l])
    ab_r, ab_i, bb_r, bb_i = disc
    s5 = dict(
        bdr=_bd(jnp.transpose(bb_r, (0, 2, 1))).astype(BF16), bdi=_bd(jnp.transpose(bb_i, (0, 2, 1))).astype(BF16),
        cdr=_bd(jnp.transpose(sp["s5_c_re"][l], (0, 2, 1))).astype(BF16), cdi=_bd(jnp.transpose(sp["s5_c_im"][l], (0, 2, 1))).astype(BF16),
        d=sp["s5_d"][l][None], wglu=bw["s5_w_glu"], ab=(ab_r.reshape(-1), ab_i.reshape(-1)), vjp=disc_vjp)
    pw, tr, ti = _cpow_tables(*s5["ab"])
    y_a, xr, xi = _s5_fwd(proj, s5["bdr"], s5["bdi"], pw, tr, ti, s5["cdr"], s5["cdi"], s5["d"], s5["wglu"], name="s5_fwd")
    sv.update(s5=s5, xr=xr, xi=xi, y_a=y_a)
    pool_s = sp["pool_scale"][l].reshape(len(POOL_WINDOWS), 1, -1)
    y_b = _pool_fwd(proj, sp["pool_w"][l], pool_s, name="pool_fwd")
    sv.update(y_b=y_b, pool_s=pool_s)
    hc = _conv1_fwd(proj, bw["conv_w_dw"], sp["conv_b_dw"][l][None], name="conv_dw_fwd")
    y_c = _conv2_fwd(hc, sp["conv_ln_g"][l][None], sp["conv_ln_b"][l][None], bw["conv_w_pw"], name="conv_pw_fwd")
    sv.update(hc=hc, y_c=y_c)
    scale = E ** -0.5
    q = (proj[:, 4 * GW:5 * GW] * scale).astype(BF16)
    k = proj[:, 5 * GW:6 * GW].astype(BF16)
    v = proj[:, 6 * GW:7 * GW].astype(BF16).reshape(L, ATT_HEADS, E)
    vx = jnp.concatenate([v, jnp.ones_like(v)], axis=-1).reshape(L, ATT_HEADS * 2 * E)
    qb, kb, vxb = _to_branches(q, ATT_HEADS), _to_branches(k, ATT_HEADS), _to_branches(vx, ATT_HEADS)
    ox = _att_fwd(qb, kb, vxb, bias, name="att_fwd")
    oxu = _from_branches(ox)
    mix = _mix_fwd(oxu, name="att_mix_fwd")
    y_d = mix.reshape(L, ATT_HEADS, 2 * E)[:, :, :E].reshape(L, GW)
    sv.update(qb=qb, kb=kb, vxb=vxb, ox=ox, oxu=oxu, y_d=y_d)
    yn = _grp_fwd([y_a, y_b, y_c, y_d], sp["grp_norm_g"][l][None], name="grp_fwd")
    h1 = _mm(yn, bw["w_out"], "nn", res=h, name="proj_out")
    sv.update(yn=yn, h1=h1)
    hx = _rms_fwd(h1, sp["norm_x_g"][l][None], name="norm_x")
    q_x = _mm(hx, bw["w_xq"], "nn", out_dtype=BF16, name="xq")
    k_x = _mm(mem_n, bw["w_xk"], "nn", out_dtype=BF16, name="xk")
    v_x = _mm(mem_n, bw["w_xv"], "nn", out_dtype=BF16, name="xv")
    o_x = _xatt_fwd(q_x, k_x, v_x, name="xatt_fwd")
    h2 = _mm(o_x, bw["w_xo"], "nn", res=h1, name="xo")
    sv.update(hx=hx, q_x=q_x, k_x=k_x, v_x=v_x, o_x=o_x, h2=h2)
    hm = _rms_fwd(h2, sp["norm_mlp_g"][l][None], name="norm_mlp")
    up, act = _mm(hm, bw["w_up"], "nn", epi="relu2", out_dtype=BF16, name="mlp_up")
    h3 = _mm(act, bw["w_down"], "nn", res=h2, name="mlp_down")
    sv.update(hm=hm, up=up, act=act)
    return h3, sv


def _stack_rows(g):
    return g.reshape(N_CHIPS, g.shape[0] // N_CHIPS, g.shape[1])


def _layer_bwd(dh3, d_memn, mem_n, bias, sp, bw, sv, l):
    L, D = dh3.shape
    GW = D // N_MIXERS
    G = GW // S5_CH
    E = GW // ATT_HEADS
    gs, gb = {}, {}
    d_up = _mm(dh3, bw["w_down"], "nt", epi="relu2_bwd", aux=sv["up"], out_dtype=BF16, name="mlp_down_dx")
    gb["w_down"] = _stack_rows(_mm(sv["act"], dh3, "tn", name="mlp_down_dw"))
    gb["w_up"] = _mm(sv["hm"], d_up, "tn", out_stack=N_CHIPS, name="mlp_up_dw")
    d_hm = _mm(d_up, bw["w_up"], "nt", name="mlp_up_dx")
    dh2, gs["norm_mlp_g"] = _rms_bwd(sv["h2"], sp["norm_mlp_g"][l][None], d_hm, dh3, name="norm_mlp_bwd")
    d_ox = _mm(dh2, bw["w_xo"], "nt", name="xo_dx")
    gb["w_xo"] = _mm(sv["o_x"], dh2, "tn", out_stack=N_CHIPS, name="xo_dw")
    dq_x, dk_x, dv_x = _xatt_bwd(sv["q_x"], sv["k_x"], sv["v_x"], d_ox, name="xatt_bwd")
    gb["w_xq"] = _stack_rows(_mm(sv["hx"], dq_x, "tn", name="xq_dw"))
    gb["w_xk"] = _stack_rows(_mm(mem_n, dk_x, "tn", name="xk_dw"))
    gb["w_xv"] = _stack_rows(_mm(mem_n, dv_x, "tn", name="xv_dw"))
    d_memn = _mm(dk_x, bw["w_xk"], "nt", res=d_memn, name="xk_dx")
    d_memn = _mm(dv_x, bw["w_xv"], "nt", res=d_memn, name="xv_dx")
    d_hx = _mm(dq_x, bw["w_xq"], "nt", name="xq_dx")
    dh1, gs["norm_x_g"] = _rms_bwd(sv["h1"], sp["norm_x_g"][l][None], d_hx, dh2, name="norm_x_bwd")
    d_yn = _mm(dh1, bw["w_out"], "nt", name="proj_out_dx")
    gb["w_out"] = _stack_rows(_mm(sv["yn"], dh1, "tn", name="proj_out_dw"))
    ys = [sv["y_a"], sv["y_b"], sv["y_c"], sv["y_d"]]
    dya, dyb, dyc, dyd, gs["grp_norm_g"] = _grp_bwd(ys, sp["grp_norm_g"][l][None], d_yn, name="grp_bwd")
    dy_ext = jnp.concatenate([dyd.reshape(L, ATT_HEADS, E), jnp.zeros((L, ATT_HEADS, E), F32)], axis=-1).reshape(L, ATT_HEADS * 2 * E)
    doxu = _mix_bwd(sv["oxu"], dy_ext, name="att_mix_bwd")
    dox = jnp.stack([_to_branches(doxu[b], ATT_HEADS)[b] for b in range(len(DILATED_PATTERNS))])
    dq_b, dk_b, dvx_b, dbias = _att_bwd(sv["qb"], sv["kb"], sv["vxb"], bias, sv["ox"], dox, name="att_bwd")
    dq = jnp.sum(_from_branches(dq_b), axis=0) * (E ** -0.5)
    dk = jnp.sum(_from_branches(dk_b), axis=0)
    dv = jnp.sum(_from_branches(dvx_b[..., :E]), axis=0)
    dhc, gs["conv_ln_g"], gs["conv_ln_b"], g_pw = _conv2_bwd(sv["hc"], sp["conv_ln_g"][l][None], sp["conv_ln_b"][l][None],
                                                               bw["conv_w_pw"], dyc, name="conv_pw_bwd")
    gb["conv_w_pw"] = _stack_rows(g_pw)
    dval, dgate, g_dw, gs["conv_b_dw"] = _conv1_bwd(sv["proj"], dhc, bw["conv_w_dw"], name="conv_dw_bwd")
    gb["conv_w_dw"] = jnp.transpose(g_dw.reshape(CONV_PAD, N_CHIPS, GW // N_CHIPS), (1, 0, 2))
    du_b, gs["pool_w"], g_ps = _pool_bwd(sv["proj"], dyb, sp["pool_w"][l], sv["pool_s"], name="pool_bwd")
    gs["pool_scale"] = g_ps.reshape(-1)
    s5 = sv["s5"]
    conj = (s5["ab"][0], -s5["ab"][1])
    pw, tr, ti = _cpow_tables(*conj)
    du_a, g_glu, g_d, dcdr, dcdi, dbdr, dbdi, dar, dai = _s5_bwd(
        sv["proj"], sv["xr"], sv["xi"], dya, s5["bdr"], s5["bdi"], pw, tr[::-1], ti[::-1], s5["cdr"], s5["cdi"], s5["d"], s5["wglu"],
        name="s5_bwd")
    gb["s5_w_glu"] = _stack_rows(g_glu)
    gs["s5_d"] = g_d.reshape(-1)
    gs["s5_c_re"] = jnp.transpose(_bd_extract(dcdr, G), (0, 2, 1))
    gs["s5_c_im"] = jnp.transpose(_bd_extract(dcdi, G), (0, 2, 1))
    d_bb_r = jnp.transpose(_bd_extract(dbdr, G), (0, 2, 1))
    d_bb_i = jnp.transpose(_bd_extract(dbdi, G), (0, 2, 1))
    d_ab_r = jnp.sum(dar, axis=0).reshape(G, S5_STATE)
    d_ab_i = jnp.sum(dai, axis=0).reshape(G, S5_STATE)
    gs["s5_lam_re"], gs["s5_lam_im"], gs["s5_log_dt"], gs["s5_b_re"], gs["s5_b_im"] = s5["vjp"]((d_ab_r, d_ab_i, d_bb_r, d_bb_i))
    d_proj = jnp.concatenate([du_a, du_b, dval, dgate, dq, dk, dv], axis=1)
    gb["w_in"] = _mm(sv["xn"], d_proj, "tn", out_stack=N_CHIPS, name="proj_in_dw")
    d_xn = _mm(d_proj, bw["w_in"], "nt", name="proj_in_dx")
    dh0, gs["norm_mix_g"] = _rms_bwd(sv["h"], sp["norm_mix_g"][l][None], d_xn, dh1, name="norm_mix_bwd")
    gs = {n: g.reshape(sp[n].shape[1:]) for n, g in gs.items()}
    return dh0, d_memn, dbias, gs, gb


def _device_step(x, mem, target, sp, big):
    nl = sp["norm_mix_g"].shape[0]
    mem_n = _rms_fwd(mem, sp["mem_norm_g"][None], name="norm_mem")
    bias = _att_bias(sp["rel_bias"])
    h, saved = x, []
    for l in range(nl):
        h, sv = _layer_fwd(h, mem_n, bias, sp, {n: big[n][l] for n in SHARDED}, l)
        saved.append(sv)
    loss, dh, g_final = _loss_head(h, sp["norm_final_g"][None], target, name="loss_head")
    d_memn = jnp.zeros(mem.shape, F32)
    dbias = jnp.zeros(bias.shape, F32)
    sg = {n: [None] * nl for n in SMALL}
    bg = {n: [None] * nl for n in SHARDED}
    for l in reversed(range(nl)):
        dh, d_memn, dbias_l, gs, gb = _layer_bwd(dh, d_memn, mem_n, bias, sp, {n: big[n][l] for n in SHARDED}, saved[l], l)
        dbias = dbias + dbias_l
        for n, g in gs.items():
            sg[n][l] = g
        for n, g in gb.items():
            bg[n][l] = g
    small = {n: jnp.stack(g) for n, g in sg.items() if g[0] is not None}
    small["norm_final_g"] = g_final.reshape(-1)
    _, g_mem = _rms_bwd(mem, sp["mem_norm_g"][None], d_memn, None, name="norm_mem_bwd")
    small["mem_norm_g"] = g_mem.reshape(-1)
    small["rel_bias"] = _rel_bias_grad(dbias)
    return loss, dh, small, bg


def _gather_big(shards):
    big = {}
    for n, kind in SHARDED.items():
        w = shards[n]
        per_layer = []
        for l in range(w.shape[0]):
            if n == "conv_w_dw":
                s = jnp.pad(w[l], ((0, CONV_PAD - CONV_WIDTH), (0, 0)))
                full = _ag_w(s, name=f"ag_{n}").reshape(N_CHIPS, CONV_PAD, -1)
                per_layer.append(jnp.transpose(full, (1, 0, 2)).reshape(CONV_PAD, -1))
            else:
                full = _ag_w(w[l].astype(BF16), name=f"ag_{n}")
                per_layer.append(_mat(full.reshape(N_CHIPS, w.shape[1], w.shape[2]), kind))
        big[n] = per_layer
    return big


def _pack_small(vals, extra=None):
    flat = [vals[n].reshape(-1).astype(F32) for n in SMALL]
    flat.append(jnp.zeros((1,), F32) if extra is None else extra.reshape(1))
    v = jnp.concatenate(flat)
    n_pad = -(-v.shape[0] // SMALL_ALIGN) * SMALL_ALIGN
    return jnp.pad(v, (0, n_pad - v.shape[0]))


def _unpack_small(flat, like):
    out, pos = {}, 0
    for n in SMALL:
        size = math.prod(like[n].shape)
        out[n] = flat[pos:pos + size].reshape(like[n].shape)
        pos += size
    return out, flat[pos]


def kernel(x, mem, rel_bias, mem_norm_g, norm_mix_g, w_in, s5_lam_re, s5_lam_im, s5_log_dt, s5_b_re, s5_b_im, s5_c_re, s5_c_im, s5_d, s5_w_glu, pool_w, pool_scale, conv_w_dw, conv_b_dw, conv_ln_g, conv_ln_b, conv_w_pw, grp_norm_g, w_out, norm_x_g, w_xq, w_xk, w_xv, w_xo, norm_mlp_g, w_up, w_down, norm_final_g, loss_target, m_rel_bias, m_mem_norm_g, m_norm_mix_g, m_w_in, m_s5_lam_re, m_s5_lam_im, m_s5_log_dt, m_s5_b_re, m_s5_b_im, m_s5_c_re, m_s5_c_im, m_s5_d, m_s5_w_glu, m_pool_w, m_pool_scale, m_conv_w_dw, m_conv_b_dw, m_conv_ln_g, m_conv_ln_b, m_conv_w_pw, m_grp_norm_g, m_w_out, m_norm_x_g, m_w_xq, m_w_xk, m_w_xv, m_w_xo, m_norm_mlp_g, m_w_up, m_w_down, m_norm_final_g, v_rel_bias, v_mem_norm_g, v_norm_mix_g, v_w_in, v_s5_lam_re, v_s5_lam_im, v_s5_log_dt, v_s5_b_re, v_s5_b_im, v_s5_c_re, v_s5_c_im, v_s5_d, v_s5_w_glu, v_pool_w, v_pool_scale, v_conv_w_dw, v_conv_b_dw, v_conv_ln_g, v_conv_ln_b, v_conv_w_pw, v_grp_norm_g, v_w_out, v_norm_x_g, v_w_xq, v_w_xk, v_w_xv, v_w_xo, v_norm_mlp_g, v_w_up, v_w_down, v_norm_final_g):
    env = dict(locals())
    w = {n: env[n] for n in WEIGHTS}
    m = {n: env["m_" + n] for n in WEIGHTS}
    v = {n: env["v_" + n] for n in WEIGHTS}
    sp = {n: w[n] for n in SMALL}
    big = _gather_big({n: w[n] for n in SHARDED})
    loss, grad_x, g_small, g_big = _device_step(x[0], mem[0], loss_target[0], sp, big)

    grad, delta, new_m, new_v = {}, {}, {}, {}
    for n in SHARDED:
        pad = ((0, 0), (0, CONV_PAD - CONV_WIDTH), (0, 0)) if n == "conv_w_dw" else None
        wmv = [jnp.pad(t[n], pad) if pad else t[n] for t in (w, m, v)]
        nl, R, C = wmv[0].shape
        g = _reduce_grads(g_big[n], n)
        outs = _adamw(*[t.reshape(nl, 2, R // 2, C) for t in wmv], g, name=f"adamw_{n}")
        outs = [o.reshape(nl, R, C)[:, :w[n].shape[1]] for o in outs]
        grad[n], delta[n], new_m[n], new_v[n] = outs
    packed = _pack_small(g_small, loss).reshape(N_CHIPS, -1, 128)
    quarter = _reduce_grads([packed], "small")
    total = _ag4(quarter.reshape(-1, 128), name="ag_small").reshape(1, 1, -1, 128)
    wmv = [_pack_small(t).reshape(1, 1, -1, 128) for t in (w, m, v)]
    outs = _adamw(*wmv, total, name="adamw_small")
    loss_sum = None
    for o, dst in zip(outs, (grad, delta, new_m, new_v)):
        vals, last = _unpack_small(o.reshape(-1), sp)
        dst.update(vals)
        loss_sum = last if loss_sum is None else loss_sum
    return (loss_sum, grad_x[None], *[grad[n] for n in WEIGHTS], *[delta[n] for n in WEIGHTS],
            *[new_m[n] for n in WEIGHTS], *[new_v[n] for n in WEIGHTS])
```

```python
import functools
import math

import jax
import jax.numpy as jnp
import numpy as np
from jax import lax
from jax.experimental import pallas as pl
from jax.experimental.pallas import tpu as pltpu

F32 = jnp.float32
BF16 = jnp.bfloat16
MESH = pl.DeviceIdType.MESH

N_MIXERS = 4
S5_CH = 16
S5_STATE = 64
POOL_WINDOWS = (2, 4, 8, 16)
CONV_WIDTH = 31
CONV_PAD = 32
ATT_HEADS = 8
ATT_BLOCK = 128
DILATED_PATTERNS = ((128, 1), (512, 4), (2048, 16))
REL_BUCKETS = 32
REL_MAX_DIST = 2048
X_HEADS = 4
X_HEAD_DIM = 128
NORM_EPS = 1e-6
NEG_INF = -1e30
ADAM_LR, ADAM_B1, ADAM_B2, ADAM_EPS, ADAM_WD, ADAM_STEP = 0.001, 0.9, 0.999, 1e-08, 0.01, 10
N_CHIPS = 4
VMEM_LIMIT = 56 * 1024 * 1024


def _cparams(sem=None, vmem=None):
    return pltpu.CompilerParams(dimension_semantics=sem, vmem_limit_bytes=vmem)


def _tile(n, pref):
    if n <= pref:
        return n
    t = pref
    while t >= 128:
        if n % t == 0:
            return t
        t -= 128
    return n


def _spec(arr, tr, tc, rc):
    if arr.ndim == 2:
        return pl.BlockSpec((tr, tc), lambda *g: rc(*g))
    per = arr.shape[2] // tc
    assert arr.shape[2] % tc == 0

    def imap(*g):
        r, c = rc(*g)
        return (c // per, r, c % per)

    return pl.BlockSpec((None, tr, tc), imap)


def _lshape(arr):
    return arr.shape if arr.ndim == 2 else (arr.shape[1], arr.shape[0] * arr.shape[2])


def _mm(a, b, mode, *, name, out_dtype=F32, res=None, epi=None, aux=None, out_stack=None, tm=1024, tn=1024, tk=2048):
    la, lb = _lshape(a), _lshape(b)
    if mode == "nn":
        (M, K), (K2, N) = la, lb
    elif mode == "nt":
        (M, K), (N, K2) = la, lb
    else:
        (K, M), (K2, N) = la, lb
    assert K == K2, (la, lb, mode)
    lim = {"m": M, "n": N // out_stack if out_stack else N, "k": K}
    stacked = [(a, "m" if mode == "tn" else "k"), (b, "k" if mode == "nt" else "n"), (res, "n"), (aux, "n")]
    for arr, dim in stacked:
        if arr is not None and arr.ndim == 3:
            lim[dim] = math.gcd(lim[dim], arr.shape[2])
    tm, tn, tk = _tile(lim["m"], tm), _tile(lim["n"], tn), _tile(lim["k"], tk)
    nk = K // tk
    grid = (M // tm, N // tn, nk)
    a_spec = _spec(a, tk, tm, lambda i, j, k: (k, i)) if mode == "tn" else _spec(a, tm, tk, lambda i, j, k: (i, k))
    b_spec = _spec(b, tn, tk, lambda i, j, k: (j, k)) if mode == "nt" else _spec(b, tk, tn, lambda i, j, k: (k, j))
    dims = {"nn": ((1,), (0,)), "nt": ((1,), (1,)), "tn": ((0,), (0,))}[mode]
    ins, in_specs = [a, b], [a_spec, b_spec]
    for extra in (res, aux):
        if extra is not None:
            ins.append(extra)
            in_specs.append(_spec(extra, tm, tn, lambda i, j, k: (i, j)))
    if out_stack:
        o_shape = jax.ShapeDtypeStruct((out_stack, M, N // out_stack), out_dtype)
    else:
        o_shape = jax.ShapeDtypeStruct((M, N), out_dtype)
    o_spec = _spec(o_shape, tm, tn, lambda i, j, k: (i, j))
    n_out = 2 if epi == "relu2" else 1
    has_res, has_aux = res is not None, aux is not None

    def body(*refs):
        a_ref, b_ref = refs[0], refs[1]
        pos = 2
        res_ref = aux_ref = None
        if has_res:
            res_ref = refs[pos]
            pos += 1
        if has_aux:
            aux_ref = refs[pos]
            pos += 1
        outs = refs[pos:pos + n_out]
        part = lax.dot_general(a_ref[...].astype(BF16), b_ref[...].astype(BF16), (dims, ((), ())), preferred_element_type=F32)
        if nk > 1:
            acc_ref = refs[pos + n_out]
            k = pl.program_id(2)

            @pl.when(k == 0)
            def _():
                acc_ref[...] = part

            @pl.when(k > 0)
            def _():
                acc_ref[...] += part

        @pl.when(pl.program_id(2) == nk - 1)
        def _():
            o = acc_ref[...] if nk > 1 else part
            if has_res:
                o = o + res_ref[...].astype(F32)
            if epi == "relu2":
                outs[0][...] = o.astype(outs[0].dtype)
                r = jnp.maximum(o, 0.0)
                outs[1][...] = (r * r).astype(outs[1].dtype)
            elif epi == "relu2_bwd":
                outs[0][...] = (o * (2.0 * jnp.maximum(aux_ref[...].astype(F32), 0.0))).astype(outs[0].dtype)
            else:
                outs[0][...] = o.astype(outs[0].dtype)

    out = pl.pallas_call(
        body, grid=grid, in_specs=in_specs,
        out_specs=[o_spec] * n_out if n_out > 1 else o_spec,
        out_shape=[o_shape] * n_out if n_out > 1 else o_shape,
        scratch_shapes=[pltpu.VMEM((tm, tn), F32)] if nk > 1 else [],
        compiler_params=_cparams(("parallel", "parallel", "arbitrary"), VMEM_LIMIT), name=name,
    )(*ins)
    return out


def _rms_fwd(x, g, *, name, out_dtype=BF16):
    L, D = x.shape
    tr = _tile(L, 256)

    def body(x_ref, g_ref, o_ref):
        xv = x_ref[...]
        r = lax.rsqrt(jnp.mean(xv * xv, axis=-1, keepdims=True) + NORM_EPS)
        o_ref[...] = (xv * r * g_ref[...]).astype(o_ref.dtype)

    return pl.pallas_call(
        body, grid=(L // tr,),
        in_specs=[pl.BlockSpec((tr, D), lambda i: (i, 0)), pl.BlockSpec((1, D), lambda i: (0, 0))],
        out_specs=pl.BlockSpec((tr, D), lambda i: (i, 0)),
        out_shape=jax.ShapeDtypeStruct((L, D), out_dtype),
        compiler_params=_cparams(("parallel",)), name=name)(x, g)


def _rms_bwd(x, g, dy, dres, *, name):
    L, D = x.shape
    tr = _tile(L, 256)
    has_res = dres is not None

    def body(*refs):
        x_ref, g_ref, dy_ref = refs[:3]
        dres_ref = refs[3] if has_res else None
        dx_ref, dg_ref = refs[-2:]
        xv = x_ref[...]
        dyv = dy_ref[...].astype(F32)
        r = lax.rsqrt(jnp.mean(xv * xv, axis=-1, keepdims=True) + NORM_EPS)
        xh = xv * r
        dyg = dyv * g_ref[...]
        dx = r * (dyg - xh * jnp.mean(dyg * xh, axis=-1, keepdims=True))
        if has_res:
            dx = dx + dres_ref[...]
        dx_ref[...] = dx

        @pl.when(pl.program_id(0) == 0)
        def _():
            dg_ref[...] = jnp.zeros_like(dg_ref)

        dg_ref[...] += jnp.sum(dyv * xh, axis=0, keepdims=True)

    row = pl.BlockSpec((tr, D), lambda i: (i, 0))
    vec = pl.BlockSpec((1, D), lambda i: (0, 0))
    ins = [x, g, dy] + ([dres] if has_res else [])
    return pl.pallas_call(
        body, grid=(L // tr,), in_specs=[row, vec, row] + ([row] if has_res else []),
        out_specs=[row, vec],
        out_shape=[jax.ShapeDtypeStruct((L, D), F32), jax.ShapeDtypeStruct((1, D), F32)],
        compiler_params=_cparams(("arbitrary",)), name=name)(*ins)


def _loss_head(h, g, target, *, name):
    L, D = h.shape
    tr = _tile(L, 256)

    def body(x_ref, g_ref, t_ref, loss_ref, dx_ref, dg_ref):
        xv = x_ref[...]
        r = lax.rsqrt(jnp.mean(xv * xv, axis=-1, keepdims=True) + NORM_EPS)
        xh = xv * r
        err = xh * g_ref[...] - t_ref[...]
        dyv = err * (1.0 / D)
        dyg = dyv * g_ref[...]
        dx_ref[...] = r * (dyg - xh * jnp.mean(dyg * xh, axis=-1, keepdims=True))

        @pl.when(pl.program_id(0) == 0)
        def _():
            dg_ref[...] = jnp.zeros_like(dg_ref)
            loss_ref[...] = jnp.zeros_like(loss_ref)

        dg_ref[...] += jnp.sum(dyv * xh, axis=0, keepdims=True)
        loss_ref[...] += 0.5 * jnp.sum(jnp.sum(err * err, axis=1, keepdims=True), axis=0, keepdims=True) * (1.0 / D)

    row = pl.BlockSpec((tr, D), lambda i: (i, 0))
    vec = pl.BlockSpec((1, D), lambda i: (0, 0))
    return pl.pallas_call(
        body, grid=(L // tr,), in_specs=[row, vec, row],
        out_specs=[pl.BlockSpec((1, 1), lambda i: (0, 0)), row, vec],
        out_shape=[jax.ShapeDtypeStruct((1, 1), F32), jax.ShapeDtypeStruct((L, D), F32), jax.ShapeDtypeStruct((1, D), F32)],
        compiler_params=_cparams(("arbitrary",)), name=name)(h, g, target)


S5_TL = 256
S5_LW = 512


def _dot(a, b, dims):
    return lax.dot_general(a, b, (dims, ((), ())), preferred_element_type=F32)


_NN, _NT, _TN = ((1,), (0,)), ((1,), (1,)), ((0,), (0,))


def _gelu_and_grad(y):
    k = math.sqrt(2.0 / math.pi)
    y2 = y * y
    th = jnp.tanh(k * (y + 0.044715 * y * y2))
    g = 0.5 * y * (1.0 + th)
    gp = 0.5 * (1.0 + th) + 0.5 * y * (1.0 - th * th) * k * (1.0 + 3.0 * 0.044715 * y2)
    return g, gp


def _scan_tiles(re_s, im_s, ls, lw, pw_ref, tr_ref, ti_ref, carry_s, nt, reverse, extra=None):
    row = lax.broadcasted_iota(jnp.int32, (8, lw), 0)
    a = [pw_ref[k:k + 1, ls] for k in range(6)]
    tr_t, ti_t = tr_ref[:, ls], ti_ref[:, ls]
    edge = 0 if reverse else 7

    def tile(jj, carry):
        cr, ci, acc = carry
        j = (nt - 1 - jj) if reverse else jj
        off = pl.multiple_of(j * 8, 8)
        sr, si = re_s[pl.ds(off, 8), ls], im_s[pl.ds(off, 8), ls]
        for n, k in enumerate((1, 2, 4)):
            akr, aki = a[2 * n], a[2 * n + 1]
            if reverse:
                keep, sh = row < 8 - k, 8 - k
            else:
                keep, sh = row >= k, k
            shr = jnp.where(keep, pltpu.roll(sr, sh, 0), 0.0)
            shi = jnp.where(keep, pltpu.roll(si, sh, 0), 0.0)
            sr, si = sr + akr * shr - aki * shi, si + akr * shi + aki * shr
        xr = sr + tr_t * cr - ti_t * ci
        xi = si + tr_t * ci + ti_t * cr
        re_s[pl.ds(off, 8), ls] = xr
        im_s[pl.ds(off, 8), ls] = xi
        if extra is not None:
            acc = extra(off, xr, xi, acc, row)
        return xr[edge:edge + 1, :], xi[edge:edge + 1, :], acc

    acc0 = (jnp.zeros((8, lw), F32), jnp.zeros((8, lw), F32)) if extra is not None else 0
    cr, ci, acc = lax.fori_loop(0, nt, tile, (carry_s[0:1, ls], carry_s[1:2, ls], acc0))
    carry_s[0:1, ls] = cr
    carry_s[1:2, ls] = ci
    return acc


def _s5_fwd(proj, bdr, bdi, pw, tr, ti, cdr, cdi, dskip, wglu, *, name):
    L = proj.shape[0]
    GW, S = bdr.shape
    TL = _tile(L, S5_TL)
    lw = min(S, S5_LW)

    def body(u_ref, bdr_ref, bdi_ref, pw_ref, tr_ref, ti_ref, cdr_ref, cdi_ref, d_ref, w_ref,
             y_ref, xr_ref, xi_ref, re_s, im_s, carry_s):
        @pl.when(pl.program_id(0) == 0)
        def _():
            carry_s[...] = jnp.zeros_like(carry_s)

        u = u_ref[...]
        ub = u.astype(BF16)
        re_s[...] = _dot(ub, bdr_ref[...], _NN)
        im_s[...] = _dot(ub, bdi_ref[...], _NN)
        for lc in range(S // lw):
            _scan_tiles(re_s, im_s, slice(lc * lw, (lc + 1) * lw), lw, pw_ref, tr_ref, ti_ref, carry_s, TL // 8, False)
        xrb, xib = re_s[...].astype(BF16), im_s[...].astype(BF16)
        xr_ref[...] = xrb
        xi_ref[...] = xib
        y = _dot(xrb, cdr_ref[...], _NN) - _dot(xib, cdi_ref[...], _NN) + d_ref[...] * u
        g, _ = _gelu_and_grad(y)
        z = _dot(g.astype(BF16), w_ref[...], _NN)
        y_ref[...] = g * jax.nn.sigmoid(z)

    full = lambda arr: pl.BlockSpec(arr.shape, lambda i: (0,) * arr.ndim)
    return pl.pallas_call(
        body, grid=(L // TL,),
        in_specs=[pl.BlockSpec((TL, GW), lambda i: (i, 0))] + [full(t) for t in (bdr, bdi, pw, tr, ti, cdr, cdi, dskip, wglu)],
        out_specs=[pl.BlockSpec((TL, GW), lambda i: (i, 0)), pl.BlockSpec((TL, S), lambda i: (i, 0)), pl.BlockSpec((TL, S), lambda i: (i, 0))],
        out_shape=[jax.ShapeDtypeStruct((L, GW), F32), jax.ShapeDtypeStruct((L, S), BF16), jax.ShapeDtypeStruct((L, S), BF16)],
        scratch_shapes=[pltpu.VMEM((TL, S), F32), pltpu.VMEM((TL, S), F32), pltpu.VMEM((8, S), F32)],
        compiler_params=_cparams(("arbitrary",), VMEM_LIMIT), name=name,
    )(proj, bdr, bdi, pw, tr, ti, cdr, cdi, dskip, wglu)


def _s5_bwd(proj, xr, xi, dout, bdr, bdi, pwc, trc, tic, cdr, cdi, dskip, wglu, *, name):
    L = proj.shape[0]
    GW, S = bdr.shape
    TL = _tile(L, S5_TL)
    nc = L // TL
    lw = min(S, S5_LW)

    def body(u_ref, xr_ref, xi_ref, xrp_ref, xip_ref, do_ref, bdr_ref, bdi_ref, pw_ref, tr_ref, ti_ref, cdr_ref, cdi_ref,
             d_ref, w_ref, du_ref, dw_ref, dd_ref, dcdr_ref, dcdi_ref, dbdr_ref, dbdi_ref, dar_ref, dai_ref,
             gr_s, gi_s, xfr, xfi, carry_s):
        i = pl.program_id(0)

        @pl.when(i == 0)
        def _():
            carry_s[...] = jnp.zeros_like(carry_s)
            for r in (dw_ref, dd_ref, dcdr_ref, dcdi_ref, dbdr_ref, dbdi_ref, dar_ref, dai_ref):
                r[...] = jnp.zeros_like(r)

        u = u_ref[...]
        ub = u.astype(BF16)
        xrb, xib = xr_ref[...], xi_ref[...]
        y = _dot(xrb, cdr_ref[...], _NN) - _dot(xib, cdi_ref[...], _NN) + d_ref[...] * u
        g, gp = _gelu_and_grad(y)
        gb = g.astype(BF16)
        sg = jax.nn.sigmoid(_dot(gb, w_ref[...], _NN))
        dout = do_ref[...]
        dz = (dout * g * sg * (1.0 - sg)).astype(BF16)
        dg = dout * sg + _dot(dz, w_ref[...], _NT)
        dw_ref[...] += _dot(gb, dz, _TN)
        dy = dg * gp
        dyb = dy.astype(BF16)
        dd_ref[...] += jnp.sum(dy * u, axis=0, keepdims=True)
        gr_s[...] = _dot(dyb, cdr_ref[...], _NT)
        gi_s[...] = -_dot(dyb, cdi_ref[...], _NT)
        dcdr_ref[...] += _dot(xrb, dyb, _TN)
        dcdi_ref[...] -= _dot(xib, dyb, _TN)
        live = (i < nc - 1).astype(F32)
        xfr[0:8, :] = xrp_ref[...].astype(F32) * live
        xfi[0:8, :] = xip_ref[...].astype(F32) * live
        xfr[8:, :] = xrb.astype(F32)
        xfi[8:, :] = xib.astype(F32)
        for lc in range(S // lw):
            ls = slice(lc * lw, (lc + 1) * lw)

            def extra(off, lr, li, acc, row, ls=ls):
                cur_r, cur_i = xfr[pl.ds(off + 8, 8), ls], xfi[pl.ds(off + 8, 8), ls]
                prv_r, prv_i = xfr[pl.ds(off, 8), ls], xfi[pl.ds(off, 8), ls]
                xpr = jnp.where(row >= 1, pltpu.roll(cur_r, 1, 0), prv_r[7:8, :])
                xpi = jnp.where(row >= 1, pltpu.roll(cur_i, 1, 0), prv_i[7:8, :])
                return acc[0] + lr * xpr + li * xpi, acc[1] - lr * xpi + li * xpr

            acc = _scan_tiles(gr_s, gi_s, ls, lw, pw_ref, tr_ref, ti_ref, carry_s, TL // 8, True, extra)
            dar_ref[:, ls] += acc[0]
            dai_ref[:, ls] += acc[1]
        lrb, lib = gr_s[...].astype(BF16), gi_s[...].astype(BF16)
        du_ref[...] = dy * d_ref[...] + _dot(lrb, bdr_ref[...], _NT) + _dot(lib, bdi_ref[...], _NT)
        dbdr_ref[...] += _dot(ub, lrb, _TN)
        dbdi_ref[...] += _dot(ub, lib, _TN)

    rev = lambda i: nc - 1 - i
    full = lambda arr: pl.BlockSpec(arr.shape, lambda i: (0,) * arr.ndim)
    chunk = lambda w: pl.BlockSpec((TL, w), lambda i: (rev(i), 0))
    prev8 = pl.BlockSpec((8, S), lambda i: (jnp.maximum(rev(i) * (TL // 8) - 1, 0), 0))
    acc_shapes = [(GW, GW), (1, GW), (S, GW), (S, GW), (GW, S), (GW, S), (8, S), (8, S)]
    return pl.pallas_call(
        body, grid=(nc,),
        in_specs=[chunk(GW), chunk(S), chunk(S), prev8, prev8, chunk(GW)] + [full(t) for t in (bdr, bdi, pwc, trc, tic, cdr, cdi, dskip, wglu)],
        out_specs=[chunk(GW)] + [pl.BlockSpec(s, lambda i: (0, 0)) for s in acc_shapes],
        out_shape=[jax.ShapeDtypeStruct((L, GW), F32)] + [jax.ShapeDtypeStruct(s, F32) for s in acc_shapes],
        scratch_shapes=[pltpu.VMEM((TL, S), F32), pltpu.VMEM((TL, S), F32), pltpu.VMEM((TL + 8, S), F32),
                        pltpu.VMEM((TL + 8, S), F32), pltpu.VMEM((8, S), F32)],
        compiler_params=_cparams(("arbitrary",), VMEM_LIMIT), name=name,
    )(proj, xr, xi, xr, xi, dout, bdr, bdi, pwc, trc, tic, cdr, cdi, dskip, wglu)


def _cpow_tables(ar, ai):
    def cm(a, b):
        return a[0] * b[0] - a[1] * b[1], a[0] * b[1] + a[1] * b[0]
    p = [(ar, ai)]
    for _ in range(7):
        p.append(cm(p[-1], p[0]))
    z = jnp.zeros_like(ar)
    pw = jnp.stack([p[0][0], p[0][1], p[1][0], p[1][1], p[3][0], p[3][1], z, z])
    return pw, jnp.stack([q[0] for q in p]), jnp.stack([q[1] for q in p])


def _s5_disc(lam_re, lam_im, log_dt, b_re, b_im):
    dt = jnp.exp(log_dt)[:, None]
    mag = jnp.exp(lam_re * dt)
    ab_r, ab_i = mag * jnp.cos(lam_im * dt), mag * jnp.sin(lam_im * dt)
    den = lam_re * lam_re + lam_im * lam_im
    nr, ni = ab_r - 1.0, ab_i
    f_r = ((nr * lam_re + ni * lam_im) / den)[..., None]
    f_i = ((ni * lam_re - nr * lam_im) / den)[..., None]
    return ab_r, ab_i, f_r * b_re - f_i * b_im, f_r * b_im + f_i * b_re


def _bd(t):
    G, A, B = t.shape
    return (t[:, :, None, :] * jnp.eye(G, dtype=t.dtype)[:, None, :, None]).reshape(G * A, G * B)


def _bd_extract(m, G):
    A, B = m.shape[0] // G, m.shape[1] // G
    return jnp.sum(m.reshape(G, A, G, B) * jnp.eye(G, dtype=m.dtype)[:, None, :, None], axis=2)


POOL_TQ = 256


def _band(shape, row0, col0, win, transpose):
    r = lax.broadcasted_iota(jnp.int32, shape, 0) + row0
    c = lax.broadcasted_iota(jnp.int32, shape, 1) + col0
    d = (c - r) if transpose else (r - c)
    return ((d >= 0) & (d < win)).astype(BF16)


def _band_dot(band, v):
    hi = v.astype(BF16)
    lo = (v - hi.astype(F32)).astype(BF16)
    return _dot(band, hi, _NN) + _dot(band, lo, _NN)


def _pool_p(u_prev, u_cur, i, win, tq):
    t0 = i * tq
    cnt = jnp.minimum(lax.broadcasted_iota(jnp.int32, (tq, 1), 0) + t0 + 1, win).astype(F32)
    live = (i > 0).astype(F32)
    acc = _band_dot(_band((tq, tq), t0, t0, win, False), u_cur)
    acc += _band_dot(_band((tq, tq), t0, t0 - tq, win, False), u_prev * live)
    return acc / cnt - u_cur


def _pool_fwd(proj, pool_w, pool_scale3, *, name):
    L = proj.shape[0]
    nw, PC, _ = pool_w.shape
    tq = _tile(L, POOL_TQ)

    def body(up_ref, uc_ref, w_ref, s_ref, y_ref):
        g, i = pl.program_id(0), pl.program_id(1)
        win = jnp.left_shift(2, g)
        p = _pool_p(up_ref[...], uc_ref[...], i, win, tq)
        y_ref[...] = _dot(p.astype(BF16), w_ref[...].astype(BF16), _NN) * s_ref[...]

    return pl.pallas_call(
        body, grid=(nw, L // tq),
        in_specs=[pl.BlockSpec((tq, PC), lambda g, i: (jnp.maximum(i - 1, 0), nw + g)),
                  pl.BlockSpec((tq, PC), lambda g, i: (i, nw + g)),
                  pl.BlockSpec((None, PC, PC), lambda g, i: (g, 0, 0)),
                  pl.BlockSpec((None, 1, PC), lambda g, i: (g, 0, 0))],
        out_specs=pl.BlockSpec((tq, PC), lambda g, i: (i, g)),
        out_shape=jax.ShapeDtypeStruct((L, nw * PC), F32),
        compiler_params=_cparams(("parallel", "parallel")), name=name)(proj, proj, pool_w, pool_scale3)


def _pool_bwd(proj, dy, pool_w, pool_scale3, *, name):
    L = proj.shape[0]
    nw, PC, _ = pool_w.shape
    tq = _tile(L, POOL_TQ)
    nq = L // tq

    def body(up_ref, uc_ref, dyc_ref, dyn_ref, w_ref, s_ref, du_ref, dw_ref, ds_ref):
        g, i = pl.program_id(0), pl.program_id(1)
        win = jnp.left_shift(2, g)

        @pl.when(i == 0)
        def _():
            dw_ref[...] = jnp.zeros_like(dw_ref)
            ds_ref[...] = jnp.zeros_like(ds_ref)

        wb = w_ref[...].astype(BF16)
        p = _pool_p(up_ref[...], uc_ref[...], i, win, tq).astype(BF16)
        dyc = dyc_ref[...]
        ds_ref[...] += jnp.sum(dyc * _dot(p, wb, _NN), axis=0, keepdims=True)
        dys = (dyc * s_ref[...]).astype(BF16)
        dw_ref[...] += _dot(p, dys, _TN)
        t0 = i * tq
        rows = lax.broadcasted_iota(jnp.int32, (tq, 1), 0)
        dp_c = _dot(dys, wb, _NT)
        q_c = dp_c / jnp.minimum(rows + t0 + 1, win).astype(F32)
        live = (i < nq - 1).astype(F32)
        dp_n = _dot((dyn_ref[...] * s_ref[...] * live).astype(BF16), wb, _NT)
        q_n = dp_n / jnp.minimum(rows + t0 + tq + 1, win).astype(F32)
        du = _band_dot(_band((tq, tq), t0, t0, win, True), q_c) + _band_dot(_band((tq, tq), t0, t0 + tq, win, True), q_n)
        du_ref[...] = du - dp_c

    return pl.pallas_call(
        body, grid=(nw, nq),
        in_specs=[pl.BlockSpec((tq, PC), lambda g, i: (jnp.maximum(i - 1, 0), nw + g)),
                  pl.BlockSpec((tq, PC), lambda g, i: (i, nw + g)),
                  pl.BlockSpec((tq, PC), lambda g, i: (i, g)),
                  pl.BlockSpec((tq, PC), lambda g, i: (jnp.minimum(i + 1, nq - 1), g)),
                  pl.BlockSpec((None, PC, PC), lambda g, i: (g, 0, 0)),
                  pl.BlockSpec((None, 1, PC), lambda g, i: (g, 0, 0))],
        out_specs=[pl.BlockSpec((tq, PC), lambda g, i: (i, g)),
                   pl.BlockSpec((None, PC, PC), lambda g, i: (g, 0, 0)),
                   pl.BlockSpec((None, 1, PC), lambda g, i: (g, 0, 0))],
        out_shape=[jax.ShapeDtypeStruct((L, nw * PC), F32), jax.ShapeDtypeStruct((nw, PC, PC), F32),
                   jax.ShapeDtypeStruct((nw, 1, PC), F32)],
        compiler_params=_cparams(("parallel", "arbitrary")), name=name)(proj, proj, dy, dy, pool_w, pool_scale3)


CONV_RC = 256


def _conv1_fwd(proj, w_dw, b_dw, *, name):
    L = proj.shape[0]
    GW = w_dw.shape[1]
    CB = min(GW, 128)
    nb = GW // CB
    RC = _tile(L, CONV_RC)
    n = RC + CONV_PAD

    def body(val_ref, gate_ref, w_ref, b_ref, o_ref, hp):
        hp[0:CONV_PAD, :] = jnp.zeros((CONV_PAD, CB), F32)
        hp[CONV_PAD:, :] = val_ref[...] * jax.nn.sigmoid(gate_ref[...])
        wv = w_ref[...]

        def chunk(ci, carry):
            c0 = pl.multiple_of(ci * RC, 8)
            win = hp[pl.ds(c0, n), :]
            acc = jnp.zeros((RC, CB), F32) + b_ref[...]
            for k in range(CONV_WIDTH):
                acc = acc + wv[k:k + 1, :] * pltpu.roll(win, n - (k + 2), 0)[0:RC]
            o_ref[pl.ds(c0, RC), :] = acc
            return carry

        lax.fori_loop(0, L // RC, chunk, 0)

    col = lambda off: pl.BlockSpec((L, CB), lambda c: (0, off * nb + c))
    return pl.pallas_call(
        body, grid=(nb,),
        in_specs=[col(2), col(3), pl.BlockSpec((CONV_PAD, CB), lambda c: (0, c)), pl.BlockSpec((1, CB), lambda c: (0, c))],
        out_specs=pl.BlockSpec((L, CB), lambda c: (0, c)),
        out_shape=jax.ShapeDtypeStruct((L, GW), F32),
        scratch_shapes=[pltpu.VMEM((L + CONV_PAD, CB), F32)],
        compiler_params=_cparams(("parallel",)), name=name)(proj, proj, w_dw, b_dw)


def _conv1_bwd(proj, dhc, w_dw, *, name):
    L = proj.shape[0]
    GW = w_dw.shape[1]
    CB = min(GW, 128)
    nb = GW // CB
    RC = _tile(L, CONV_RC)
    n = RC + CONV_PAD

    def body(val_ref, gate_ref, d_ref, w_ref, dval_ref, dgate_ref, dw_ref, db_ref, hp, dp):
        val = val_ref[...]
        sg = jax.nn.sigmoid(gate_ref[...])
        hp[0:CONV_PAD, :] = jnp.zeros((CONV_PAD, CB), F32)
        hp[CONV_PAD:, :] = val * sg
        dp[0:L, :] = d_ref[...]
        dp[L:, :] = jnp.zeros((CONV_PAD, CB), F32)
        dw_ref[...] = jnp.zeros_like(dw_ref)
        db_ref[...] = jnp.sum(d_ref[...], axis=0, keepdims=True)
        wv = w_ref[...]

        def chunk(ci, carry):
            c0 = pl.multiple_of(ci * RC, 8)
            win = hp[pl.ds(c0, n), :]
            dwin = dp[pl.ds(c0, n), :]
            d = dwin[0:RC]
            dh = jnp.zeros((RC, CB), F32)
            for k in range(CONV_WIDTH):
                dw_ref[k:k + 1, :] += jnp.sum(d * pltpu.roll(win, n - (k + 2), 0)[0:RC], axis=0, keepdims=True)
                sh = CONV_WIDTH - 1 - k
                dh = dh + wv[k:k + 1, :] * (pltpu.roll(dwin, n - sh, 0)[0:RC] if sh else d)
            v, s = val_ref[pl.ds(c0, RC), :], jax.nn.sigmoid(gate_ref[pl.ds(c0, RC), :])
            dval_ref[pl.ds(c0, RC), :] = dh * s
            dgate_ref[pl.ds(c0, RC), :] = dh * v * s * (1.0 - s)
            return carry

        lax.fori_loop(0, L // RC, chunk, 0)

    col = lambda off: pl.BlockSpec((L, CB), lambda c: (0, off * nb + c))
    own = pl.BlockSpec((L, CB), lambda c: (0, c))
    return pl.pallas_call(
        body, grid=(nb,),
        in_specs=[col(2), col(3), own, pl.BlockSpec((CONV_PAD, CB), lambda c: (0, c))],
        out_specs=[own, own, pl.BlockSpec((CONV_PAD, CB), lambda c: (0, c)), pl.BlockSpec((1, CB), lambda c: (0, c))],
        out_shape=[jax.ShapeDtypeStruct((L, GW), F32), jax.ShapeDtypeStruct((L, GW), F32),
                   jax.ShapeDtypeStruct((CONV_PAD, GW), F32), jax.ShapeDtypeStruct((1, GW), F32)],
        scratch_shapes=[pltpu.VMEM((L + CONV_PAD, CB), F32), pltpu.VMEM((L + CONV_PAD, CB), F32)],
        compiler_params=_cparams(("parallel",)), name=name)(proj, proj, dhc, w_dw)


def _ln_silu(hc, g, b):
    mu = jnp.mean(hc, axis=-1, keepdims=True)
    xc = hc - mu
    r = lax.rsqrt(jnp.mean(xc * xc, axis=-1, keepdims=True) + NORM_EPS)
    xh = xc * r
    a = xh * g + b
    sg = jax.nn.sigmoid(a)
    return xh, r, a, sg


def _conv2_fwd(hc, ln_g, ln_b, w_pw, *, name):
    L, GW = hc.shape
    tr = _tile(L, 256)

    def body(h_ref, g_ref, b_ref, w_ref, y_ref):
        _, _, a, sg = _ln_silu(h_ref[...], g_ref[...], b_ref[...])
        y_ref[...] = _dot((a * sg).astype(BF16), w_ref[...], _NN)

    row = pl.BlockSpec((tr, GW), lambda i: (i, 0))
    vec = pl.BlockSpec((1, GW), lambda i: (0, 0))
    return pl.pallas_call(
        body, grid=(L // tr,), in_specs=[row, vec, vec, pl.BlockSpec((GW, GW), lambda i: (0, 0))],
        out_specs=row, out_shape=jax.ShapeDtypeStruct((L, GW), F32),
        compiler_params=_cparams(("parallel",)), name=name)(hc, ln_g, ln_b, w_pw)


def _conv2_bwd(hc, ln_g, ln_b, w_pw, dy, *, name):
    L, GW = hc.shape
    tr = _tile(L, 256)

    def body(h_ref, g_ref, b_ref, w_ref, dy_ref, dh_ref, dg_ref, db_ref, dw_ref):
        @pl.when(pl.program_id(0) == 0)
        def _():
            for r in (dg_ref, db_ref, dw_ref):
                r[...] = jnp.zeros_like(r)

        xh, r, a, sg = _ln_silu(h_ref[...], g_ref[...], b_ref[...])
        dyb = dy_ref[...].astype(BF16)
        dw_ref[...] += _dot((a * sg).astype(BF16), dyb, _TN)
        da = _dot(dyb, w_ref[...], _NT) * (sg + a * sg * (1.0 - sg))
        dg_ref[...] += jnp.sum(da * xh, axis=0, keepdims=True)
        db_ref[...] += jnp.sum(da, axis=0, keepdims=True)
        dxh = da * g_ref[...]
        dh_ref[...] = r * (dxh - jnp.mean(dxh, axis=-1, keepdims=True) - xh * jnp.mean(dxh * xh, axis=-1, keepdims=True))

    row = pl.BlockSpec((tr, GW), lambda i: (i, 0))
    vec = pl.BlockSpec((1, GW), lambda i: (0, 0))
    mat = pl.BlockSpec((GW, GW), lambda i: (0, 0))
    return pl.pallas_call(
        body, grid=(L // tr,), in_specs=[row, vec, vec, mat, row],
        out_specs=[row, vec, vec, mat],
        out_shape=[jax.ShapeDtypeStruct((L, GW), F32), jax.ShapeDtypeStruct((1, GW), F32), jax.ShapeDtypeStruct((1, GW), F32),
                   jax.ShapeDtypeStruct((GW, GW), F32)],
        compiler_params=_cparams(("arbitrary",)), name=name)(hc, ln_g, ln_b, w_pw, dy)


def _grp_fwd(ys, g, *, name):
    L, GW = ys[0].shape
    tr = _tile(L, 256)

    def body(*refs):
        g_ref, o_ref = refs[4], refs[5]
        for m in range(N_MIXERS):
            yv = refs[m][...]
            r = lax.rsqrt(jnp.mean(yv * yv, axis=-1, keepdims=True) + NORM_EPS)
            o_ref[:, m * GW:(m + 1) * GW] = (yv * r * g_ref[:, m * GW:(m + 1) * GW]).astype(o_ref.dtype)

    row = pl.BlockSpec((tr, GW), lambda i: (i, 0))
    return pl.pallas_call(
        body, grid=(L // tr,), in_specs=[row] * 4 + [pl.BlockSpec((1, N_MIXERS * GW), lambda i: (0, 0))],
        out_specs=pl.BlockSpec((tr, N_MIXERS * GW), lambda i: (i, 0)),
        out_shape=jax.ShapeDtypeStruct((L, N_MIXERS * GW), BF16),
        compiler_params=_cparams(("parallel",)), name=name)(*ys, g)


def _grp_bwd(ys, g, dy, *, name):
    L, GW = ys[0].shape
    tr = _tile(L, 256)

    def body(*refs):
        g_ref, dy_ref = refs[4], refs[5]
        outs, dg_ref = refs[6:10], refs[10]

        @pl.when(pl.program_id(0) == 0)
        def _():
            dg_ref[...] = jnp.zeros_like(dg_ref)

        for m in range(N_MIXERS):
            cs = slice(m * GW, (m + 1) * GW)
            yv = refs[m][...]
            r = lax.rsqrt(jnp.mean(yv * yv, axis=-1, keepdims=True) + NORM_EPS)
            xh = yv * r
            dyv = dy_ref[:, cs]
            dyg = dyv * g_ref[:, cs]
            outs[m][...] = r * (dyg - xh * jnp.mean(dyg * xh, axis=-1, keepdims=True))
            dg_ref[:, cs] += jnp.sum(dyv * xh, axis=0, keepdims=True)

    row = pl.BlockSpec((tr, GW), lambda i: (i, 0))
    wide = pl.BlockSpec((tr, N_MIXERS * GW), lambda i: (i, 0))
    vec = pl.BlockSpec((1, N_MIXERS * GW), lambda i: (0, 0))
    return pl.pallas_call(
        body, grid=(L // tr,), in_specs=[row] * 4 + [vec, wide],
        out_specs=[row] * 4 + [vec],
        out_shape=[jax.ShapeDtypeStruct((L, GW), F32)] * 4 + [jax.ShapeDtypeStruct((1, N_MIXERS * GW), F32)],
        compiler_params=_cparams(("arbitrary",)), name=name)(*ys, g, dy)


def _att_first(j, br, nblk):
    per = lax.shift_right_logical(jnp.int32(nblk), 2 * br)
    return jnp.bitwise_and(j, per - 1) == 0


def _att_fwd(q, k, vx, bias, *, name):
    nbr, H, L, E = q.shape
    B = ATT_BLOCK
    nblk = L // B

    def body(q_ref, k_ref, v_ref, b_ref, o_ref):
        br = pl.program_id(0)
        lane = lax.broadcasted_iota(jnp.int32, (B, 2 * E), 1)

        def blk(j, carry):
            off = pl.multiple_of(j * B, B)
            poff = pl.multiple_of(jnp.maximum(j - 1, 0) * B, B)
            qv = q_ref[pl.ds(off, B), :]
            s_p = _dot(qv, k_ref[pl.ds(poff, B), :], _NT) + b_ref[:, 0:B]
            s_c = _dot(qv, k_ref[pl.ds(off, B), :], _NT) + b_ref[:, B:2 * B]
            s_p = jnp.where(_att_first(j, br, nblk), NEG_INF, s_p)
            m = jnp.maximum(jnp.max(s_p, axis=-1, keepdims=True), jnp.max(s_c, axis=-1, keepdims=True))
            p_p, p_c = jnp.exp(s_p - m), jnp.exp(s_c - m)
            den = jnp.sum(p_p, axis=-1, keepdims=True) + jnp.sum(p_c, axis=-1, keepdims=True)
            acc = _dot(p_p.astype(BF16), v_ref[pl.ds(poff, B), :], _NN) + _dot(p_c.astype(BF16), v_ref[pl.ds(off, B), :], _NN)
            o_ref[pl.ds(off, B), :] = jnp.where(lane < E, acc / den, m + jnp.log(den))
            return carry

        lax.fori_loop(0, nblk, blk, 0)

    seq = lambda w: pl.BlockSpec((None, None, L, w), lambda b, h: (b, h, 0, 0))
    return pl.pallas_call(
        body, grid=(nbr, H),
        in_specs=[seq(E), seq(E), seq(2 * E), pl.BlockSpec((None, None, B, 2 * B), lambda b, h: (b, h, 0, 0))],
        out_specs=seq(2 * E), out_shape=jax.ShapeDtypeStruct((nbr, H, L, 2 * E), F32),
        compiler_params=_cparams(("parallel", "parallel")), name=name)(q, k, vx, bias)


def _att_bwd(q, k, vx, bias, ox, dox, *, name):
    nbr, H, L, E = q.shape
    B = ATT_BLOCK
    nblk = L // B

    def body(q_ref, k_ref, v_ref, b_ref, o_ref, do_ref, dq_ref, dk_ref, dv_ref, db_ref):
        br = pl.program_id(0)
        dk_ref[...] = jnp.zeros_like(dk_ref)
        dv_ref[...] = jnp.zeros_like(dv_ref)
        db_ref[...] = jnp.zeros_like(db_ref)
        lane = lax.broadcasted_iota(jnp.int32, (B, 2 * E), 1)

        def blk(j, carry):
            off = pl.multiple_of(j * B, B)
            poff = pl.multiple_of(jnp.maximum(j - 1, 0) * B, B)
            qv, kc, kp = q_ref[pl.ds(off, B), :], k_ref[pl.ds(off, B), :], k_ref[pl.ds(poff, B), :]
            vc, vp = v_ref[pl.ds(off, B), :], v_ref[pl.ds(poff, B), :]
            oe, de = o_ref[pl.ds(off, B), :], do_ref[pl.ds(off, B), :]
            lse, dlse = oe[:, E:E + 1], de[:, E:E + 1]
            do = jnp.where(lane < E, de, 0.0)
            dm = jnp.sum(do * oe, axis=-1, keepdims=True) - dlse
            dob = do.astype(BF16)
            s_p = _dot(qv, kp, _NT) + b_ref[:, 0:B]
            s_c = _dot(qv, kc, _NT) + b_ref[:, B:2 * B]
            s_p = jnp.where(_att_first(j, br, nblk), NEG_INF, s_p)
            p_p, p_c = jnp.exp(s_p - lse), jnp.exp(s_c - lse)
            ds_p = p_p * (_dot(dob, vp, _NT) - dm)
            ds_c = p_c * (_dot(dob, vc, _NT) - dm)
            db_ref[:, 0:B] += ds_p
            db_ref[:, B:2 * B] += ds_c
            dsb_p, dsb_c = ds_p.astype(BF16), ds_c.astype(BF16)
            dq_ref[pl.ds(off, B), :] = _dot(dsb_p, kp, _NN) + _dot(dsb_c, kc, _NN)
            dk_ref[pl.ds(poff, B), :] += _dot(dsb_p, qv, _TN)
            dk_ref[pl.ds(off, B), :] += _dot(dsb_c, qv, _TN)
            dv_ref[pl.ds(poff, B), :] += _dot(p_p.astype(BF16), dob, _TN)
            dv_ref[pl.ds(off, B), :] += _dot(p_c.astype(BF16), dob, _TN)
            return carry

        lax.fori_loop(0, nblk, blk, 0)

    seq = lambda w: pl.BlockSpec((None, None, L, w), lambda b, h: (b, h, 0, 0))
    bsp = pl.BlockSpec((None, None, B, 2 * B), lambda b, h: (b, h, 0, 0))
    return pl.pallas_call(
        body, grid=(nbr, H),
        in_specs=[seq(E), seq(E), seq(2 * E), bsp, seq(2 * E), seq(2 * E)],
        out_specs=[seq(E), seq(E), seq(2 * E), bsp],
        out_shape=[jax.ShapeDtypeStruct((nbr, H, L, E), F32), jax.ShapeDtypeStruct((nbr, H, L, E), F32),
                   jax.ShapeDtypeStruct((nbr, H, L, 2 * E), F32), jax.ShapeDtypeStruct((nbr, H, B, 2 * B), F32)],
        compiler_params=_cparams(("parallel", "parallel")), name=name)(q, k, vx, bias, ox, dox)


def _mix_weights(xs, lane, E, W):
    al = [jnp.where(lane < E, pltpu.roll(x, W - E, 1), x) for x in xs]
    m = functools.reduce(jnp.maximum, al)
    es = [jnp.exp(a - m) for a in al]
    tot = functools.reduce(lambda a, b: a + b, es)
    return [e / tot for e in es]


def _mix_fwd(ox, *, name):
    nbr, L, W = ox.shape
    E = W // ATT_HEADS // 2
    tr = _tile(L, 256)

    def body(x_ref, o_ref):
        lane = lax.broadcasted_iota(jnp.int32, (tr, W), 1) % (2 * E)
        xs = [x_ref[b] for b in range(nbr)]
        ws = _mix_weights(xs, lane, E, W)
        o_ref[...] = functools.reduce(lambda a, b: a + b, [w * x for w, x in zip(ws, xs)])

    return pl.pallas_call(
        body, grid=(L // tr,), in_specs=[pl.BlockSpec((nbr, tr, W), lambda i: (0, i, 0))],
        out_specs=pl.BlockSpec((tr, W), lambda i: (i, 0)), out_shape=jax.ShapeDtypeStruct((L, W), F32),
        compiler_params=_cparams(("parallel",)), name=name)(ox)


def _mix_bwd(ox, dy, *, name):
    nbr, L, W = ox.shape
    E = W // ATT_HEADS // 2
    tr = _tile(L, 256)

    def body(x_ref, dy_ref, o_ref):
        lane = lax.broadcasted_iota(jnp.int32, (tr, W), 1) % (2 * E)
        xs = [x_ref[b] for b in range(nbr)]
        ws = _mix_weights(xs, lane, E, W)
        mix = functools.reduce(lambda a, b: a + b, [w * x for w, x in zip(ws, xs)])
        dyv = jnp.where(lane < E, dy_ref[...], 0.0)
        r = lax.broadcasted_iota(jnp.int32, (W, W), 0)
        c = lax.broadcasted_iota(jnp.int32, (W, W), 1)
        seg = ((r // (2 * E) == c // (2 * E)) & (r % (2 * E) < E)).astype(F32)
        for b in range(nbr):
            t = dyv * (xs[b] - mix)
            dl = ws[b] * jnp.dot(t, seg, preferred_element_type=F32, precision=lax.Precision.HIGHEST)
            o_ref[b] = jnp.where(lane < E, ws[b] * dyv, dl)

    return pl.pallas_call(
        body, grid=(L // tr,),
        in_specs=[pl.BlockSpec((nbr, tr, W), lambda i: (0, i, 0)), pl.BlockSpec((tr, W), lambda i: (i, 0))],
        out_specs=pl.BlockSpec((nbr, tr, W), lambda i: (0, i, 0)), out_shape=jax.ShapeDtypeStruct((nbr, L, W), F32),
        compiler_params=_cparams(("parallel",), VMEM_LIMIT), name=name)(ox, dy)


def _t5_bucket(dist):
    n = np.maximum(dist, 0)
    max_exact = REL_BUCKETS // 2
    large = max_exact + (np.log(np.maximum(n, 1) / max_exact) / np.log(REL_MAX_DIST / max_exact)
                         * (REL_BUCKETS - max_exact)).astype(np.int64)
    large = np.minimum(large, REL_BUCKETS - 1)
    return np.where(n < max_exact, n, large).astype(np.int32)


def _att_tables():
    a = np.arange(ATT_BLOCK)[:, None]
    b = np.arange(2 * ATT_BLOCK)[None, :]
    sub = a + ATT_BLOCK - b
    buckets, valids = [], []
    for window, dil in DILATED_PATTERNS:
        buckets.append(_t5_bucket(sub * dil))
        valids.append((sub >= 0) & (sub <= window // dil))
    return np.stack(buckets), np.stack(valids)


def _to_branches(t, H):
    L = t.shape[0]
    E = t.shape[1] // H
    out = []
    for _, dil in DILATED_PATTERNS:
        out.append(t.reshape(L // dil, dil, H, E).transpose(2, 1, 0, 3).reshape(H, L, E))
    return jnp.stack(out)


def _from_branches(t):
    _, H, L, E = t.shape
    out = []
    for b, (_, dil) in enumerate(DILATED_PATTERNS):
        out.append(t[b].reshape(H, dil, L // dil, E).transpose(2, 1, 0, 3).reshape(L, H * E))
    return jnp.stack(out)


def _xatt_fwd(q, k, v, *, name):
    L, XW = q.shape
    M = k.shape[0]
    tr = _tile(L, 512)
    scale = X_HEAD_DIM ** -0.5

    def body(q_ref, k_ref, v_ref, o_ref):
        s = _dot(q_ref[...], k_ref[...], _NT) * scale
        p = jnp.exp(s - jnp.max(s, axis=-1, keepdims=True))
        den = jnp.sum(p, axis=-1, keepdims=True)
        o_ref[...] = (_dot(p.astype(BF16), v_ref[...], _NN) / den).astype(o_ref.dtype)

    qs = pl.BlockSpec((tr, X_HEAD_DIM), lambda h, i: (i, h))
    ks = pl.BlockSpec((M, X_HEAD_DIM), lambda h, i: (0, h))
    return pl.pallas_call(
        body, grid=(XW // X_HEAD_DIM, L // tr), in_specs=[qs, ks, ks], out_specs=qs,
        out_shape=jax.ShapeDtypeStruct((L, XW), BF16),
        compiler_params=_cparams(("parallel", "parallel")), name=name)(q, k, v)


def _xatt_bwd(q, k, v, do, *, name):
    L, XW = q.shape
    M = k.shape[0]
    tr = _tile(L, 512)
    scale = X_HEAD_DIM ** -0.5

    def body(q_ref, k_ref, v_ref, do_ref, dq_ref, dk_ref, dv_ref):
        @pl.when(pl.program_id(1) == 0)
        def _():
            dk_ref[...] = jnp.zeros_like(dk_ref)
            dv_ref[...] = jnp.zeros_like(dv_ref)

        qv, kv, vv = q_ref[...], k_ref[...], v_ref[...]
        s = _dot(qv, kv, _NT) * scale
        p = jnp.exp(s - jnp.max(s, axis=-1, keepdims=True))
        p = p / jnp.sum(p, axis=-1, keepdims=True)
        dob = do_ref[...].astype(BF16)
        pb = p.astype(BF16)
        dv_ref[...] += _dot(pb, dob, _TN)
        dp = _dot(dob, vv, _NT)
        ds = (p * (dp - jnp.sum(dp * p, axis=-1, keepdims=True)) * scale).astype(BF16)
        dq_ref[...] = _dot(ds, kv, _NN)
        dk_ref[...] += _dot(ds, qv, _TN)

    qs = pl.BlockSpec((tr, X_HEAD_DIM), lambda h, i: (i, h))
    ks = pl.BlockSpec((M, X_HEAD_DIM), lambda h, i: (0, h))
    return pl.pallas_call(
        body, grid=(XW // X_HEAD_DIM, L // tr), in_specs=[qs, ks, ks, qs], out_specs=[qs, ks, ks],
        out_shape=[jax.ShapeDtypeStruct((L, XW), F32), jax.ShapeDtypeStruct((M, XW), F32), jax.ShapeDtypeStruct((M, XW), F32)],
        compiler_params=_cparams(("parallel", "arbitrary")), name=name)(q, k, v, do)


_ANY = pl.BlockSpec(memory_space=pl.ANY)


def _place():
    mx, my, mc = lax.axis_index("x"), lax.axis_index("y"), lax.axis_index("c")
    chips = [(1 - mx, my), (mx, 1 - my), (1 - mx, 1 - my)]
    return mx, my, mc, chips


def _rcopy(src, dst, ssem, rsem, dev):
    return pltpu.make_async_remote_copy(src_ref=src, dst_ref=dst, send_sem=ssem, recv_sem=rsem, device_id=dev,
                                        device_id_type=MESH)


STAGE_BYTES = 2 * 1024 * 1024


def _staged_copy(pairs, buf, isem, osem):
    n = len(pairs)
    ins = [pltpu.make_async_copy(src, buf.at[i % 2], isem.at[i % 2]) for i, (src, _) in enumerate(pairs)]
    outs = [pltpu.make_async_copy(buf.at[i % 2], dst, osem.at[i % 2]) for i, (_, dst) in enumerate(pairs)]
    ins[0].start()
    for i in range(n):
        if i + 1 < n:
            if i >= 1:
                outs[i - 1].wait()
            ins[i + 1].start()
        ins[i].wait()
        outs[i].start()
    for i in range(max(n - 2, 0), n):
        outs[i].wait()


def _ag_w(x, *, name):
    R, C = x.shape
    R2 = R // 2

    ch = _rows_tile(R2, max(8, STAGE_BYTES // (C * x.dtype.itemsize)))

    def body(x_ref, o_ref, ssem, rsem, buf, isem, osem):
        mx, my, mc, chips = _place()
        me = 2 * mx + my
        sib = (mx, my, 1 - mc)
        mine = x_ref.at[pl.ds(mc * R2, R2)]
        sends = [_rcopy(mine, o_ref.at[me, mc], ssem.at[j], rsem.at[j], (px, py, mc)) for j, (px, py) in enumerate(chips)]
        for cp in sends:
            cp.start()
        _staged_copy([(x_ref.at[pl.ds(h * R2 + r, ch)], o_ref.at[me, h, pl.ds(r, ch)]) for h in range(2) for r in range(0, R2, ch)],
                     buf, isem, osem)
        passed = []
        for j, (px, py) in enumerate(chips):
            got = o_ref.at[2 * px + py, mc]
            _rcopy(mine, got, ssem.at[j], rsem.at[j], (px, py, mc)).wait_recv()
            cp = _rcopy(got, got, ssem.at[3 + j], rsem.at[3 + j], sib)
            cp.start()
            passed.append(cp)
        for j, (px, py) in enumerate(chips):
            theirs = o_ref.at[2 * px + py, 1 - mc]
            _rcopy(theirs, theirs, ssem.at[3 + j], rsem.at[3 + j], sib).wait_recv()
        for cp in sends + passed:
            cp.wait_send()

    return pl.pallas_call(
        body, in_specs=[_ANY], out_specs=_ANY, out_shape=jax.ShapeDtypeStruct((N_CHIPS, 2, R2, C), x.dtype),
        scratch_shapes=[pltpu.SemaphoreType.DMA((6,)), pltpu.SemaphoreType.DMA((6,)), pltpu.VMEM((2, ch, C), x.dtype),
                        pltpu.SemaphoreType.DMA((2,)), pltpu.SemaphoreType.DMA((2,))],
        name=name)(x)


_HBM = pl.BlockSpec(memory_space=pltpu.HBM)
_SEMS = pl.BlockSpec(memory_space=pltpu.SEMAPHORE)
_VM = pl.BlockSpec(memory_space=pltpu.VMEM)
_DATAFLOW = pltpu.SideEffectType.DATAFLOW_SIDE_EFFECTING


def _ag_copies(x_refs, land_refs, ssem, rsem, inbound):
    mx, my, mc, chips = _place()
    me = 2 * mx + my
    cps = []
    for w, (x_ref, land_ref) in enumerate(zip(x_refs, land_refs)):
        R2 = x_ref.shape[0] // 2
        mine = x_ref.at[pl.ds(mc * R2, R2)]
        for j, (px, py) in enumerate(chips):
            dst = land_ref.at[2 * px + py, mc] if inbound else land_ref.at[me, mc]
            cps.append(_rcopy(mine, dst, ssem.at[3 * w + j], rsem.at[3 * w + j], (px, py, mc)))
    return cps


def _ag_start(xs, token, *, name):
    n = len(xs)
    lands = [pltpu.with_memory_space_constraint(lax.empty((N_CHIPS, 2, x.shape[0] // 2, x.shape[1]), x.dtype), pltpu.HBM) for x in xs]
    xs = [pltpu.with_memory_space_constraint(x, pltpu.HBM) for x in xs]

    def body(*refs):
        x_refs, land_refs, tok_ref = refs[:n], refs[n:2 * n], refs[2 * n]
        ssem, rsem, tok_out = refs[2 * n + 1], refs[2 * n + 2], refs[-1]
        for cp in _ag_copies(x_refs, land_refs, ssem, rsem, False):
            cp.start()
        tok_out[...] = tok_ref[...]

    outs = pl.pallas_call(
        body, name=name,
        out_shape=(pltpu.SemaphoreType.DMA((3 * n,)), pltpu.SemaphoreType.DMA((3 * n,)),
                   *[pltpu.HBM(x.shape, x.dtype) for x in xs], *[pltpu.HBM(t.shape, t.dtype) for t in lands],
                   jax.ShapeDtypeStruct(token.shape, token.dtype)),
        in_specs=[_HBM] * (2 * n) + [_VM], out_specs=(_SEMS, _SEMS, *[_HBM] * (2 * n), _VM),
        input_output_aliases={i: 2 + i for i in range(2 * n)},
        compiler_params=pltpu.CompilerParams(has_side_effects=_DATAFLOW),
    )(*xs, *lands, token)
    return outs[0], outs[1], list(outs[2:2 + n]), list(outs[2 + n:2 + 2 * n]), outs[-1]


def _ag_wait(ssem, rsem, xs, lands, after, *, name):
    n = len(xs)

    def body(*refs):
        x_refs, land_refs, ssem_ref, rsem_ref = refs[:n], refs[n:2 * n], refs[2 * n], refs[2 * n + 1]
        for cp in _ag_copies(x_refs, land_refs, ssem_ref, rsem_ref, True):
            cp.wait_send()
            cp.wait_recv()

    outs = pl.pallas_call(
        body, name=name,
        out_shape=(*[pltpu.HBM(x.shape, x.dtype) for x in xs], *[pltpu.HBM(t.shape, t.dtype) for t in lands]),
        in_specs=[_HBM] * (2 * n) + [_SEMS, _SEMS, _ANY], out_specs=tuple([_HBM] * (2 * n)),
        input_output_aliases={i: i for i in range(2 * n)},
        compiler_params=pltpu.CompilerParams(has_side_effects=_DATAFLOW),
    )(*xs, *lands, ssem, rsem, after)
    return list(outs[:n]), list(outs[n:])


def _ag_finish(land, x, *, name):
    R, C = x.shape
    R2 = R // 2
    ch = _rows_tile(R2, max(8, STAGE_BYTES // (C * x.dtype.itemsize)))

    def body(l_ref, x_ref, o_ref, ssem, rsem, buf, isem, osem):
        mx, my, mc, chips = _place()
        me = 2 * mx + my
        sib = (mx, my, 1 - mc)
        passed = []
        for j, (px, py) in enumerate(chips):
            got = o_ref.at[2 * px + py, mc]
            passed.append(_rcopy(got, got, ssem.at[j], rsem.at[j], sib))
        for cp in passed:
            cp.start()
        _staged_copy([(x_ref.at[pl.ds(h * R2 + r, ch)], o_ref.at[me, h, pl.ds(r, ch)]) for h in range(2) for r in range(0, R2, ch)],
                     buf, isem, osem)
        for j, (px, py) in enumerate(chips):
            theirs = o_ref.at[2 * px + py, 1 - mc]
            _rcopy(theirs, theirs, ssem.at[j], rsem.at[j], sib).wait_recv()
        for cp in passed:
            cp.wait_send()

    return pl.pallas_call(
        body, in_specs=[_ANY, _ANY], out_specs=_ANY, out_shape=jax.ShapeDtypeStruct(land.shape, land.dtype),
        input_output_aliases={0: 0},
        scratch_shapes=[pltpu.SemaphoreType.DMA((3,)), pltpu.SemaphoreType.DMA((3,)), pltpu.VMEM((2, ch, C), x.dtype),
                        pltpu.SemaphoreType.DMA((2,)), pltpu.SemaphoreType.DMA((2,))],
        name=name)(land, x)


def _ag4(x, *, name):
    def body(x_ref, o_ref, ssem, rsem, lsem):
        mx, my, mc, chips = _place()
        me = 2 * mx + my
        loc = pltpu.make_async_copy(x_ref, o_ref.at[me], lsem)
        loc.start()
        sends = [_rcopy(x_ref, o_ref.at[me], ssem.at[j], rsem.at[j], (px, py, mc)) for j, (px, py) in enumerate(chips)]
        for cp in sends:
            cp.start()
        for j, (px, py) in enumerate(chips):
            _rcopy(x_ref, o_ref.at[2 * px + py], ssem.at[j], rsem.at[j], (px, py, mc)).wait_recv()
        for cp in sends:
            cp.wait_send()
        loc.wait()

    return pl.pallas_call(
        body, in_specs=[_ANY], out_specs=_ANY, out_shape=jax.ShapeDtypeStruct((N_CHIPS,) + x.shape, x.dtype),
        scratch_shapes=[pltpu.SemaphoreType.DMA((3,)), pltpu.SemaphoreType.DMA((3,)), pltpu.SemaphoreType.DMA(())],
        name=name)(x)


def _rs_sib(gs, *, name):
    nl = len(gs)
    _, R, C = gs[0].shape
    R2 = R // 2
    n = nl * N_CHIPS

    def body(*refs):
        g_refs, (theirs_ref, ssem, rsem) = refs[:nl], refs[nl:]
        mx, my, mc, _ = _place()
        sib = (mx, my, 1 - mc)
        sends = [_rcopy(g_refs[l].at[k, pl.ds((1 - mc) * R2, R2)], theirs_ref.at[l, k], ssem.at[l * N_CHIPS + k],
                        rsem.at[l * N_CHIPS + k], sib) for l in range(nl) for k in range(N_CHIPS)]
        for cp in sends:
            cp.start()
        for l in range(nl):
            for k in range(N_CHIPS):
                t = theirs_ref.at[l, k]
                _rcopy(t, t, ssem.at[l * N_CHIPS + k], rsem.at[l * N_CHIPS + k], sib).wait_recv()
        for cp in sends:
            cp.wait_send()

    return pl.pallas_call(
        body, in_specs=[_ANY] * nl, out_specs=_ANY, out_shape=jax.ShapeDtypeStruct((nl, N_CHIPS, R2, C), gs[0].dtype),
        scratch_shapes=[pltpu.SemaphoreType.DMA((n,)), pltpu.SemaphoreType.DMA((n,))], name=name)(*gs)


def _a2a4(h, *, name):
    nl = h.shape[0]
    n = nl * (N_CHIPS - 1)

    def body(h_ref, o_ref, ssem, rsem):
        mx, my, mc, chips = _place()
        me = 2 * mx + my
        sends = [_rcopy(h_ref.at[l, 2 * px + py], o_ref.at[l, me], ssem.at[3 * l + j], rsem.at[3 * l + j], (px, py, mc))
                 for l in range(nl) for j, (px, py) in enumerate(chips)]
        for cp in sends:
            cp.start()
        for l in range(nl):
            for j, (px, py) in enumerate(chips):
                t = o_ref.at[l, 2 * px + py]
                _rcopy(t, t, ssem.at[3 * l + j], rsem.at[3 * l + j], (px, py, mc)).wait_recv()
        for cp in sends:
            cp.wait_send()

    return pl.pallas_call(
        body, in_specs=[_ANY], out_specs=_ANY, out_shape=jax.ShapeDtypeStruct(h.shape, h.dtype),
        scratch_shapes=[pltpu.SemaphoreType.DMA((n,)), pltpu.SemaphoreType.DMA((n,))], name=name)(h)


def _sib_fill(buf, *, name):
    nl = buf.shape[1]

    def body(b_ref, o_ref, ssem, rsem):
        mx, my, mc, _ = _place()
        sib = (mx, my, 1 - mc)
        sends = [_rcopy(o_ref.at[mc, l], o_ref.at[mc, l], ssem.at[l], rsem.at[l], sib) for l in range(nl)]
        for cp in sends:
            cp.start()
        for l in range(nl):
            t = o_ref.at[1 - mc, l]
            _rcopy(t, t, ssem.at[l], rsem.at[l], sib).wait_recv()
        for cp in sends:
            cp.wait_send()

    return pl.pallas_call(
        body, in_specs=[_ANY], out_specs=_ANY, out_shape=jax.ShapeDtypeStruct(buf.shape, buf.dtype),
        input_output_aliases={0: 0},
        scratch_shapes=[pltpu.SemaphoreType.DMA((nl,)), pltpu.SemaphoreType.DMA((nl,))], name=name)(buf)


def _rows_tile(n, pref=512):
    t = min(n, pref)
    while n % t or (t % 8 and t != n):
        t -= 1
    return t


def _rs_add(gs, theirs, c_arr, *, name):
    nl = len(gs)
    _, R, C = gs[0].shape
    R2 = R // 2
    tr = _rows_tile(R2, 256)
    nb = R2 // tr

    def body(c_ref, *refs):
        g_refs, t_ref, o_ref = refs[:nl], refs[nl], refs[nl + 1]
        l = pl.program_id(0)
        for ll in range(nl):
            @pl.when(l == ll)
            def _(ll=ll):
                o_ref[...] = (g_refs[ll][...] + t_ref[...]).astype(o_ref.dtype)

    def g_spec(ll):
        return pl.BlockSpec((None, tr, C), lambda l, k, i, c: (jnp.where(l == ll, k, 0), jnp.where(l == ll, c[0] * nb + i, 0), 0))

    blk = pl.BlockSpec((None, None, tr, C), lambda l, k, i, c: (l, k, i, 0))
    return pl.pallas_call(
        body,
        grid_spec=pltpu.PrefetchScalarGridSpec(
            num_scalar_prefetch=1, grid=(nl, N_CHIPS, nb), in_specs=[g_spec(ll) for ll in range(nl)] + [blk], out_specs=blk),
        out_shape=jax.ShapeDtypeStruct((nl, N_CHIPS, R2, C), BF16),
        compiler_params=_cparams(("arbitrary", "arbitrary", "arbitrary")), name=name)(c_arr, *gs, theirs)


def _rs_sum(chip_sum, got, mc_arr, *, name):
    nl, _, R2, C = chip_sum.shape
    tr = _rows_tile(R2, 256)

    def body(s_ref, own_ref, g0, g1, g2, g3, o_ref):
        me = s_ref[0]
        acc = jnp.zeros((tr, C), F32)
        for k, g_ref in enumerate((g0, g1, g2, g3)):
            acc = acc + jnp.where(me == k, own_ref[...], g_ref[...]).astype(F32)
        o_ref[...] = acc

    def got_spec(k):
        return pl.BlockSpec((None, None, tr, C), lambda l, i, s: (l, jnp.where(s[0] == k, (k + 1) % N_CHIPS, k), i, 0))

    return pl.pallas_call(
        body,
        grid_spec=pltpu.PrefetchScalarGridSpec(
            num_scalar_prefetch=1, grid=(nl, R2 // tr),
            in_specs=[pl.BlockSpec((None, None, tr, C), lambda l, i, s: (l, s[0], i, 0))] + [got_spec(k) for k in range(N_CHIPS)],
            out_specs=pl.BlockSpec((None, None, tr, C), lambda l, i, s: (s[1], l, i, 0))),
        out_shape=jax.ShapeDtypeStruct((2, nl, R2, C), F32),
        compiler_params=_cparams(("arbitrary", "arbitrary")), name=name)(mc_arr, chip_sum, got, got, got, got)


def _adamw(w, m, v, g, *, name):
    A, B, R, C = w.shape
    tr = _rows_tile(R, 256)
    c1 = 1.0 / (1.0 - ADAM_B1 ** ADAM_STEP)
    c2 = 1.0 / (1.0 - ADAM_B2 ** ADAM_STEP)

    def body(w_ref, m_ref, v_ref, g_ref, go_ref, d_ref, mo_ref, vo_ref):
        gv = g_ref[...]
        mn = ADAM_B1 * m_ref[...] + (1.0 - ADAM_B1) * gv
        vn = ADAM_B2 * v_ref[...] + (1.0 - ADAM_B2) * (gv * gv)
        go_ref[...] = gv
        mo_ref[...] = mn
        vo_ref[...] = vn
        d_ref[...] = -ADAM_LR * ((mn * c1) / (jnp.sqrt(vn * c2) + ADAM_EPS) + ADAM_WD * w_ref[...])

    ws = pl.BlockSpec((None, None, tr, C), lambda a, b, i: (a, b, i, 0))
    gs = pl.BlockSpec((None, None, tr, C), lambda a, b, i: (b, a, i, 0))
    shp = jax.ShapeDtypeStruct(w.shape, F32)
    return pl.pallas_call(
        body, grid=(A, B, R // tr), in_specs=[ws, ws, ws, gs], out_specs=[ws] * 4, out_shape=[shp] * 4,
        compiler_params=_cparams(("parallel", "parallel", "parallel")), name=name)(w, m, v, g)


def _reduce_grads(gs, tag):
    me = (2 * lax.axis_index("x") + lax.axis_index("y")).astype(jnp.int32)
    c = lax.axis_index("c").astype(jnp.int32)
    theirs = _rs_sib(gs, name=f"rs_sib_{tag}")
    chip_sum = _rs_add(gs, theirs, c.reshape(1), name=f"rs_add_{tag}")
    got = _a2a4(chip_sum, name=f"rs_a2a_{tag}")
    half = _rs_sum(chip_sum, got, jnp.stack([me, c]), name=f"rs_sum_{tag}")
    return _sib_fill(half, name=f"rs_fill_{tag}")


WEIGHTS = ["rel_bias", "mem_norm_g", "norm_mix_g", "w_in", "s5_lam_re", "s5_lam_im", "s5_log_dt", "s5_b_re", "s5_b_im",
           "s5_c_re", "s5_c_im", "s5_d", "s5_w_glu", "pool_w", "pool_scale", "conv_w_dw", "conv_b_dw", "conv_ln_g",
           "conv_ln_b", "conv_w_pw", "grp_norm_g", "w_out", "norm_x_g", "w_xq", "w_xk", "w_xv", "w_xo", "norm_mlp_g",
           "w_up", "w_down", "norm_final_g"]
SHARDED = {"w_in": "col", "s5_w_glu": "row", "conv_w_dw": "col", "conv_w_pw": "row", "w_out": "row", "w_xq": "row",
           "w_xk": "row", "w_xv": "row", "w_xo": "col", "w_up": "col", "w_down": "row"}
SMALL = [n for n in WEIGHTS if n not in SHARDED]
SMALL_ALIGN = N_CHIPS * 2 * 256 * 128


def _mat(w, kind):
    return w.reshape(w.shape[0] * w.shape[1], w.shape[2]) if kind == "row" else w


def _att_bias(rel_bias):
    bucket, valid = _att_tables()
    onehot = (jnp.asarray(bucket.reshape(-1))[:, None] == jnp.arange(REL_BUCKETS)[None, :]).astype(F32)
    b = jnp.dot(onehot, rel_bias, precision=lax.Precision.HIGHEST).reshape(bucket.shape + (rel_bias.shape[1],))
    return jnp.where(jnp.asarray(valid)[:, None], jnp.transpose(b, (0, 3, 1, 2)), NEG_INF)


def _rel_bias_grad(dbias):
    bucket, _ = _att_tables()
    nb, H = dbias.shape[0], dbias.shape[1]
    onehot = (jnp.asarray(bucket.reshape(-1))[:, None] == jnp.arange(REL_BUCKETS)[None, :]).astype(BF16)
    flat = jnp.transpose(dbias.reshape(nb, H, -1), (1, 0, 2)).reshape(H, -1)
    return _mm(flat, onehot, "nn", name="rel_bias_grad").T


def _layer_fwd(h, mem_n, bias, sp, bw, l):
    L, D = h.shape
    GW = D // N_MIXERS
    G = GW // S5_CH
    E = GW // ATT_HEADS
    sv = {"h": h}
    xn = _rms_fwd(h, sp["norm_mix_g"][l][None], name="norm_mix")
    proj = _mm(xn, bw["w_in"], "nn", name="proj_in")
    sv.update(xn=xn, proj=proj)
    disc, disc_vjp = jax.vjp(_s5_disc, sp["s5_lam_re"][l], sp["s5_lam_im"][l], sp["s5_log_dt"][l], sp["s5_b_re"][l], sp["s5_b_im"][l])
    ab_r, ab_i, bb_r, bb_i = disc
    s5 = dict(
        bdr=_bd(jnp.transpose(bb_r, (0, 2, 1))).astype(BF16), bdi=_bd(jnp.transpose(bb_i, (0, 2, 1))).astype(BF16),
        cdr=_bd(jnp.transpose(sp["s5_c_re"][l], (0, 2, 1))).astype(BF16), cdi=_bd(jnp.transpose(sp["s5_c_im"][l], (0, 2, 1))).astype(BF16),
        d=sp["s5_d"][l][None], wglu=bw["s5_w_glu"], ab=(ab_r.reshape(-1), ab_i.reshape(-1)), vjp=disc_vjp)
    pw, tr, ti = _cpow_tables(*s5["ab"])
    y_a, xr, xi = _s5_fwd(proj, s5["bdr"], s5["bdi"], pw, tr, ti, s5["cdr"], s5["cdi"], s5["d"], s5["wglu"], name="s5_fwd")
    sv.update(s5=s5, xr=xr, xi=xi, y_a=y_a)
    pool_s = sp["pool_scale"][l].reshape(len(POOL_WINDOWS), 1, -1)
    y_b = _pool_fwd(proj, sp["pool_w"][l], pool_s, name="pool_fwd")
    sv.update(y_b=y_b, pool_s=pool_s)
    hc = _conv1_fwd(proj, bw["conv_w_dw"], sp["conv_b_dw"][l][None], name="conv_dw_fwd")
    y_c = _conv2_fwd(hc, sp["conv_ln_g"][l][None], sp["conv_ln_b"][l][None], bw["conv_w_pw"], name="conv_pw_fwd")
    sv.update(hc=hc, y_c=y_c)
    scale = E ** -0.5
    q = (proj[:, 4 * GW:5 * GW] * scale).astype(BF16)
    k = proj[:, 5 * GW:6 * GW].astype(BF16)
    v = proj[:, 6 * GW:7 * GW].astype(BF16).reshape(L, ATT_HEADS, E)
    vx = jnp.concatenate([v, jnp.ones_like(v)], axis=-1).reshape(L, ATT_HEADS * 2 * E)
    qb, kb, vxb = _to_branches(q, ATT_HEADS), _to_branches(k, ATT_HEADS), _to_branches(vx, ATT_HEADS)
    ox = _att_fwd(qb, kb, vxb, bias, name="att_fwd")
    oxu = _from_branches(ox)
    mix = _mix_fwd(oxu, name="att_mix_fwd")
    y_d = mix.reshape(L, ATT_HEADS, 2 * E)[:, :, :E].reshape(L, GW)
    sv.update(qb=qb, kb=kb, vxb=vxb, ox=ox, oxu=oxu, y_d=y_d)
    yn = _grp_fwd([y_a, y_b, y_c, y_d], sp["grp_norm_g"][l][None], name="grp_fwd")
    h1 = _mm(yn, bw["w_out"], "nn", res=h, name="proj_out")
    sv.update(yn=yn, h1=h1)
    hx = _rms_fwd(h1, sp["norm_x_g"][l][None], name="norm_x")
    q_x = _mm(hx, bw["w_xq"], "nn", out_dtype=BF16, name="xq")
    k_x = _mm(mem_n, bw["w_xk"], "nn", out_dtype=BF16, name="xk")
    v_x = _mm(mem_n, bw["w_xv"], "nn", out_dtype=BF16, name="xv")
    o_x = _xatt_fwd(q_x, k_x, v_x, name="xatt_fwd")
    h2 = _mm(o_x, bw["w_xo"], "nn", res=h1, name="xo")
    sv.update(hx=hx, q_x=q_x, k_x=k_x, v_x=v_x, o_x=o_x, h2=h2)
    hm = _rms_fwd(h2, sp["norm_mlp_g"][l][None], name="norm_mlp")
    up, act = _mm(hm, bw["w_up"], "nn", epi="relu2", out_dtype=BF16, name="mlp_up")
    h3 = _mm(act, bw["w_down"], "nn", res=h2, name="mlp_down")
    sv.update(hm=hm, up=up, act=act)
    return h3, sv


def _stack_rows(g):
    return g.reshape(N_CHIPS, g.shape[0] // N_CHIPS, g.shape[1])


def _layer_bwd(dh3, d_memn, mem_n, bias, sp, bw, sv, l):
    L, D = dh3.shape
    GW = D // N_MIXERS
    G = GW // S5_CH
    E = GW // ATT_HEADS
    gs, gb = {}, {}
    d_up = _mm(dh3, bw["w_down"], "nt", epi="relu2_bwd", aux=sv["up"], out_dtype=BF16, name="mlp_down_dx")
    gb["w_down"] = _stack_rows(_mm(sv["act"], dh3, "tn", name="mlp_down_dw"))
    gb["w_up"] = _mm(sv["hm"], d_up, "tn", out_stack=N_CHIPS, name="mlp_up_dw")
    d_hm = _mm(d_up, bw["w_up"], "nt", name="mlp_up_dx")
    dh2, gs["norm_mlp_g"] = _rms_bwd(sv["h2"], sp["norm_mlp_g"][l][None], d_hm, dh3, name="norm_mlp_bwd")
    d_ox = _mm(dh2, bw["w_xo"], "nt", name="xo_dx")
    gb["w_xo"] = _mm(sv["o_x"], dh2, "tn", out_stack=N_CHIPS, name="xo_dw")
    dq_x, dk_x, dv_x = _xatt_bwd(sv["q_x"], sv["k_x"], sv["v_x"], d_ox, name="xatt_bwd")
    gb["w_xq"] = _stack_rows(_mm(sv["hx"], dq_x, "tn", name="xq_dw"))
    gb["w_xk"] = _stack_rows(_mm(mem_n, dk_x, "tn", name="xk_dw"))
    gb["w_xv"] = _stack_rows(_mm(mem_n, dv_x, "tn", name="xv_dw"))
    d_memn = _mm(dk_x, bw["w_xk"], "nt", res=d_memn, name="xk_dx")
    d_memn = _mm(dv_x, bw["w_xv"], "nt", res=d_memn, name="xv_dx")
    d_hx = _mm(dq_x, bw["w_xq"], "nt", name="xq_dx")
    dh1, gs["norm_x_g"] = _rms_bwd(sv["h1"], sp["norm_x_g"][l][None], d_hx, dh2, name="norm_x_bwd")
    d_yn = _mm(dh1, bw["w_out"], "nt", name="proj_out_dx")
    gb["w_out"] = _stack_rows(_mm(sv["yn"], dh1, "tn", name="proj_out_dw"))
    ys = [sv["y_a"], sv["y_b"], sv["y_c"], sv["y_d"]]
    dya, dyb, dyc, dyd, gs["grp_norm_g"] = _grp_bwd(ys, sp["grp_norm_g"][l][None], d_yn, name="grp_bwd")
    dy_ext = jnp.concatenate([dyd.reshape(L, ATT_HEADS, E), jnp.zeros((L, ATT_HEADS, E), F32)], axis=-1).reshape(L, ATT_HEADS * 2 * E)
    doxu = _mix_bwd(sv["oxu"], dy_ext, name="att_mix_bwd")
    dox = jnp.stack([_to_branches(doxu[b], ATT_HEADS)[b] for b in range(len(DILATED_PATTERNS))])
    dq_b, dk_b, dvx_b, dbias = _att_bwd(sv["qb"], sv["kb"], sv["vxb"], bias, sv["ox"], dox, name="att_bwd")
    dq = jnp.sum(_from_branches(dq_b), axis=0) * (E ** -0.5)
    dk = jnp.sum(_from_branches(dk_b), axis=0)
    dv = jnp.sum(_from_branches(dvx_b[..., :E]), axis=0)
    dhc, gs["conv_ln_g"], gs["conv_ln_b"], g_pw = _conv2_bwd(sv["hc"], sp["conv_ln_g"][l][None], sp["conv_ln_b"][l][None],
                                                               bw["conv_w_pw"], dyc, name="conv_pw_bwd")
    gb["conv_w_pw"] = _stack_rows(g_pw)
    dval, dgate, g_dw, gs["conv_b_dw"] = _conv1_bwd(sv["proj"], dhc, bw["conv_w_dw"], name="conv_dw_bwd")
    gb["conv_w_dw"] = jnp.transpose(g_dw.reshape(CONV_PAD, N_CHIPS, GW // N_CHIPS), (1, 0, 2))
    du_b, gs["pool_w"], g_ps = _pool_bwd(sv["proj"], dyb, sp["pool_w"][l], sv["pool_s"], name="pool_bwd")
    gs["pool_scale"] = g_ps.reshape(-1)
    s5 = sv["s5"]
    conj = (s5["ab"][0], -s5["ab"][1])
    pw, tr, ti = _cpow_tables(*conj)
    du_a, g_glu, g_d, dcdr, dcdi, dbdr, dbdi, dar, dai = _s5_bwd(
        sv["proj"], sv["xr"], sv["xi"], dya, s5["bdr"], s5["bdi"], pw, tr[::-1], ti[::-1], s5["cdr"], s5["cdi"], s5["d"], s5["wglu"],
        name="s5_bwd")
    gb["s5_w_glu"] = _stack_rows(g_glu)
    gs["s5_d"] = g_d.reshape(-1)
    gs["s5_c_re"] = jnp.transpose(_bd_extract(dcdr, G), (0, 2, 1))
    gs["s5_c_im"] = jnp.transpose(_bd_extract(dcdi, G), (0, 2, 1))
    d_bb_r = jnp.transpose(_bd_extract(dbdr, G), (0, 2, 1))
    d_bb_i = jnp.transpose(_bd_extract(dbdi, G), (0, 2, 1))
    d_ab_r = jnp.sum(dar, axis=0).reshape(G, S5_STATE)
    d_ab_i = jnp.sum(dai, axis=0).reshape(G, S5_STATE)
    gs["s5_lam_re"], gs["s5_lam_im"], gs["s5_log_dt"], gs["s5_b_re"], gs["s5_b_im"] = s5["vjp"]((d_ab_r, d_ab_i, d_bb_r, d_bb_i))
    d_proj = jnp.concatenate([du_a, du_b, dval, dgate, dq, dk, dv], axis=1)
    gb["w_in"] = _mm(sv["xn"], d_proj, "tn", out_stack=N_CHIPS, name="proj_in_dw")
    d_xn = _mm(d_proj, bw["w_in"], "nt", name="proj_in_dx")
    dh0, gs["norm_mix_g"] = _rms_bwd(sv["h"], sp["norm_mix_g"][l][None], d_xn, dh1, name="norm_mix_bwd")
    gs = {n: g.reshape(sp[n].shape[1:]) for n, g in gs.items()}
    return dh0, d_memn, dbias, gs, gb


def _device_step(x, mem, target, sp, get_big):
    nl = sp["norm_mix_g"].shape[0]
    mem_n = _rms_fwd(mem, sp["mem_norm_g"][None], name="norm_mem")
    bias = _att_bias(sp["rel_bias"])
    h, saved, big = x, [], {n: [] for n in SHARDED}
    for l in range(nl):
        bw = get_big(l, h)
        for n in SHARDED:
            big[n].append(bw[n])
        h, sv = _layer_fwd(h, mem_n, bias, sp, bw, l)
        saved.append(sv)
    loss, dh, g_final = _loss_head(h, sp["norm_final_g"][None], target, name="loss_head")
    d_memn = jnp.zeros(mem.shape, F32)
    dbias = jnp.zeros(bias.shape, F32)
    sg = {n: [None] * nl for n in SMALL}
    bg = {n: [None] * nl for n in SHARDED}
    for l in reversed(range(nl)):
        dh, d_memn, dbias_l, gs, gb = _layer_bwd(dh, d_memn, mem_n, bias, sp, {n: big[n][l] for n in SHARDED}, saved[l], l)
        dbias = dbias + dbias_l
        for n, g in gs.items():
            sg[n][l] = g
        for n, g in gb.items():
            bg[n][l] = g
    small = {n: jnp.stack(g) for n, g in sg.items() if g[0] is not None}
    small["norm_final_g"] = g_final.reshape(-1)
    _, g_mem = _rms_bwd(mem, sp["mem_norm_g"][None], d_memn, None, name="norm_mem_bwd")
    small["mem_norm_g"] = g_mem.reshape(-1)
    small["rel_bias"] = _rel_bias_grad(dbias)
    return loss, dh, small, bg


def _gather_big(shards):
    nl = shards["w_in"].shape[0]
    token = jnp.zeros((8, 128), F32)
    pending = []
    for l in range(nl):
        xs = [jnp.pad(shards[n][l], ((0, CONV_PAD - CONV_WIDTH), (0, 0))) if n == "conv_w_dw" else shards[n][l].astype(BF16)
              for n in SHARDED]
        ssem, rsem, xs, lands, token = _ag_start(xs, token, name=f"ag_start_l{l}")
        pending.append((ssem, rsem, xs, lands))

    def get_big(l, after):
        ssem, rsem, xs, lands = pending[l]
        xs, lands = _ag_wait(ssem, rsem, xs, lands, token if l == 0 else after, name=f"ag_wait_l{l}")
        bw = {}
        for (n, kind), x, land in zip(SHARDED.items(), xs, lands):
            full = _ag_finish(land, x, name=f"ag_finish_{n}").reshape(N_CHIPS, x.shape[0], x.shape[1])
            bw[n] = jnp.transpose(full, (1, 0, 2)).reshape(CONV_PAD, -1) if n == "conv_w_dw" else _mat(full, kind)
        return bw

    return get_big


def _pack_small(vals, extra=None):
    flat = [vals[n].reshape(-1).astype(F32) for n in SMALL]
    flat.append(jnp.zeros((1,), F32) if extra is None else extra.reshape(1))
    v = jnp.concatenate(flat)
    n_pad = -(-v.shape[0] // SMALL_ALIGN) * SMALL_ALIGN
    return jnp.pad(v, (0, n_pad - v.shape[0]))


def _unpack_small(flat, like):
    out, pos = {}, 0
    for n in SMALL:
        size = math.prod(like[n].shape)
        out[n] = flat[pos:pos + size].reshape(like[n].shape)
        pos += size
    return out, flat[pos]


def kernel(x, mem, rel_bias, mem_norm_g, norm_mix_g, w_in, s5_lam_re, s5_lam_im, s5_log_dt, s5_b_re, s5_b_im, s5_c_re, s5_c_im, s5_d, s5_w_glu, pool_w, pool_scale, conv_w_dw, conv_b_dw, conv_ln_g, conv_ln_b, conv_w_pw, grp_norm_g, w_out, norm_x_g, w_xq, w_xk, w_xv, w_xo, norm_mlp_g, w_up, w_down, norm_final_g, loss_target, m_rel_bias, m_mem_norm_g, m_norm_mix_g, m_w_in, m_s5_lam_re, m_s5_lam_im, m_s5_log_dt, m_s5_b_re, m_s5_b_im, m_s5_c_re, m_s5_c_im, m_s5_d, m_s5_w_glu, m_pool_w, m_pool_scale, m_conv_w_dw, m_conv_b_dw, m_conv_ln_g, m_conv_ln_b, m_conv_w_pw, m_grp_norm_g, m_w_out, m_norm_x_g, m_w_xq, m_w_xk, m_w_xv, m_w_xo, m_norm_mlp_g, m_w_up, m_w_down, m_norm_final_g, v_rel_bias, v_mem_norm_g, v_norm_mix_g, v_w_in, v_s5_lam_re, v_s5_lam_im, v_s5_log_dt, v_s5_b_re, v_s5_b_im, v_s5_c_re, v_s5_c_im, v_s5_d, v_s5_w_glu, v_pool_w, v_pool_scale, v_conv_w_dw, v_conv_b_dw, v_conv_ln_g, v_conv_ln_b, v_conv_w_pw, v_grp_norm_g, v_w_out, v_norm_x_g, v_w_xq, v_w_xk, v_w_xv, v_w_xo, v_norm_mlp_g, v_w_up, v_w_down, v_norm_final_g):
    env = dict(locals())
    w = {n: env[n] for n in WEIGHTS}
    m = {n: env["m_" + n] for n in WEIGHTS}
    v = {n: env["v_" + n] for n in WEIGHTS}
    sp = {n: w[n] for n in SMALL}
    get_big = _gather_big({n: w[n] for n in SHARDED})
    loss, grad_x, g_small, g_big = _device_step(x[0], mem[0], loss_target[0], sp, get_big)

    grad, delta, new_m, new_v = {}, {}, {}, {}
    for n in SHARDED:
        pad = ((0, 0), (0, CONV_PAD - CONV_WIDTH), (0, 0)) if n == "conv_w_dw" else None
        wmv = [jnp.pad(t[n], pad) if pad else t[n] for t in (w, m, v)]
        nl, R, C = wmv[0].shape
        g = _reduce_grads(g_big[n], n)
        outs = _adamw(*[t.reshape(nl, 2, R // 2, C) for t in wmv], g, name=f"adamw_{n}")
        outs = [o.reshape(nl, R, C)[:, :w[n].shape[1]] for o in outs]
        grad[n], delta[n], new_m[n], new_v[n] = outs
    packed = _pack_small(g_small, loss).reshape(N_CHIPS, -1, 128)
    quarter = _reduce_grads([packed], "small")
    total = _ag4(quarter.reshape(-1, 128), name="ag_small").reshape(1, 1, -1, 128)
    wmv = [_pack_small(t).reshape(1, 1, -1, 128) for t in (w, m, v)]
    outs = _adamw(*wmv, total, name="adamw_small")
    loss_sum = None
    for o, dst in zip(outs, (grad, delta, new_m, new_v)):
        vals, last = _unpack_small(o.reshape(-1), sp)
        dst.update(vals)
        loss_sum = last if loss_sum is None else loss_sum
    return (loss_sum, grad_x[None], *[grad[n] for n in WEIGHTS], *[delta[n] for n in WEIGHTS],
            *[new_m[n] for n in WEIGHTS], *[new_v[n] for n in WEIGHTS])
```

```python
import functools
import math

import jax
import jax.numpy as jnp
import numpy as np
from jax import lax
from jax.experimental import pallas as pl
from jax.experimental.pallas import tpu as pltpu

F32 = jnp.float32
BF16 = jnp.bfloat16
MESH = pl.DeviceIdType.MESH

N_MIXERS = 4
S5_CH = 16
S5_STATE = 64
POOL_WINDOWS = (2, 4, 8, 16)
CONV_WIDTH = 31
CONV_PAD = 32
ATT_HEADS = 8
ATT_BLOCK = 128
DILATED_PATTERNS = ((128, 1), (512, 4), (2048, 16))
REL_BUCKETS = 32
REL_MAX_DIST = 2048
X_HEADS = 4
X_HEAD_DIM = 128
NORM_EPS = 1e-6
NEG_INF = -1e30
ADAM_LR, ADAM_B1, ADAM_B2, ADAM_EPS, ADAM_WD, ADAM_STEP = 0.001, 0.9, 0.999, 1e-08, 0.01, 10
N_CHIPS = 4
VMEM_LIMIT = 56 * 1024 * 1024


def _cparams(sem=None, vmem=None):
    return pltpu.CompilerParams(dimension_semantics=sem, vmem_limit_bytes=vmem)


def _tile(n, pref):
    if n <= pref:
        return n
    t = pref
    while t >= 128:
        if n % t == 0:
            return t
        t -= 128
    return n


def _spec(arr, tr, tc, rc):
    if arr.ndim == 2:
        return pl.BlockSpec((tr, tc), lambda *g: rc(*g))
    per = arr.shape[2] // tc
    assert arr.shape[2] % tc == 0

    def imap(*g):
        r, c = rc(*g)
        return (c // per, r, c % per)

    return pl.BlockSpec((None, tr, tc), imap)


def _lshape(arr):
    return arr.shape if arr.ndim == 2 else (arr.shape[1], arr.shape[0] * arr.shape[2])


def _mm(a, b, mode, *, name, out_dtype=F32, res=None, epi=None, aux=None, out_stack=None, tm=1024, tn=1024, tk=2048):
    la, lb = _lshape(a), _lshape(b)
    if mode == "nn":
        (M, K), (K2, N) = la, lb
    elif mode == "nt":
        (M, K), (N, K2) = la, lb
    else:
        (K, M), (K2, N) = la, lb
    assert K == K2, (la, lb, mode)
    lim = {"m": M, "n": N // out_stack if out_stack else N, "k": K}
    stacked = [(a, "m" if mode == "tn" else "k"), (b, "k" if mode == "nt" else "n"), (res, "n"), (aux, "n")]
    for arr, dim in stacked:
        if arr is not None and arr.ndim == 3:
            lim[dim] = math.gcd(lim[dim], arr.shape[2])
    tm, tn, tk = _tile(lim["m"], tm), _tile(lim["n"], tn), _tile(lim["k"], tk)
    nk = K // tk
    grid = (M // tm, N // tn, nk)
    a_spec = _spec(a, tk, tm, lambda i, j, k: (k, i)) if mode == "tn" else _spec(a, tm, tk, lambda i, j, k: (i, k))
    b_spec = _spec(b, tn, tk, lambda i, j, k: (j, k)) if mode == "nt" else _spec(b, tk, tn, lambda i, j, k: (k, j))
    dims = {"nn": ((1,), (0,)), "nt": ((1,), (1,)), "tn": ((0,), (0,))}[mode]
    ins, in_specs = [a, b], [a_spec, b_spec]
    for extra in (res, aux):
        if extra is not None:
            ins.append(extra)
            in_specs.append(_spec(extra, tm, tn, lambda i, j, k: (i, j)))
    if out_stack:
        o_shape = jax.ShapeDtypeStruct((out_stack, M, N // out_stack), out_dtype)
    else:
        o_shape = jax.ShapeDtypeStruct((M, N), out_dtype)
    o_spec = _spec(o_shape, tm, tn, lambda i, j, k: (i, j))
    n_out = 2 if epi == "relu2" else 1
    has_res, has_aux = res is not None, aux is not None

    def body(*refs):
        a_ref, b_ref = refs[0], refs[1]
        pos = 2
        res_ref = aux_ref = None
        if has_res:
            res_ref = refs[pos]
            pos += 1
        if has_aux:
            aux_ref = refs[pos]
            pos += 1
        outs = refs[pos:pos + n_out]
        part = lax.dot_general(a_ref[...].astype(BF16), b_ref[...].astype(BF16), (dims, ((), ())), preferred_element_type=F32)
        if nk > 1:
            acc_ref = refs[pos + n_out]
            k = pl.program_id(2)

            @pl.when(k == 0)
            def _():
                acc_ref[...] = part

            @pl.when(k > 0)
            def _():
                acc_ref[...] += part

        @pl.when(pl.program_id(2) == nk - 1)
        def _():
            o = acc_ref[...] if nk > 1 else part
            if has_res:
                o = o + res_ref[...].astype(F32)
            if epi == "relu2":
                outs[0][...] = o.astype(outs[0].dtype)
                r = jnp.maximum(o, 0.0)
                outs[1][...] = (r * r).astype(outs[1].dtype)
            elif epi == "relu2_bwd":
                outs[0][...] = (o * (2.0 * jnp.maximum(aux_ref[...].astype(F32), 0.0))).astype(outs[0].dtype)
            else:
                outs[0][...] = o.astype(outs[0].dtype)

    out = pl.pallas_call(
        body, grid=grid, in_specs=in_specs,
        out_specs=[o_spec] * n_out if n_out > 1 else o_spec,
        out_shape=[o_shape] * n_out if n_out > 1 else o_shape,
        scratch_shapes=[pltpu.VMEM((tm, tn), F32)] if nk > 1 else [],
        compiler_params=_cparams(("parallel", "parallel", "arbitrary"), VMEM_LIMIT), name=name,
    )(*ins)
    return out


def _rms_fwd(x, g, *, name, out_dtype=BF16):
    L, D = x.shape
    tr = _tile(L, 256)

    def body(x_ref, g_ref, o_ref):
        xv = x_ref[...]
        r = lax.rsqrt(jnp.mean(xv * xv, axis=-1, keepdims=True) + NORM_EPS)
        o_ref[...] = (xv * r * g_ref[...]).astype(o_ref.dtype)

    return pl.pallas_call(
        body, grid=(L // tr,),
        in_specs=[pl.BlockSpec((tr, D), lambda i: (i, 0)), pl.BlockSpec((1, D), lambda i: (0, 0))],
        out_specs=pl.BlockSpec((tr, D), lambda i: (i, 0)),
        out_shape=jax.ShapeDtypeStruct((L, D), out_dtype),
        compiler_params=_cparams(("parallel",)), name=name)(x, g)


def _rms_bwd(x, g, dy, dres, *, name):
    L, D = x.shape
    tr = _tile(L, 256)
    has_res = dres is not None

    def body(*refs):
        x_ref, g_ref, dy_ref = refs[:3]
        dres_ref = refs[3] if has_res else None
        dx_ref, dg_ref = refs[-2:]
        xv = x_ref[...]
        dyv = dy_ref[...].astype(F32)
        r = lax.rsqrt(jnp.mean(xv * xv, axis=-1, keepdims=True) + NORM_EPS)
        xh = xv * r
        dyg = dyv * g_ref[...]
        dx = r * (dyg - xh * jnp.mean(dyg * xh, axis=-1, keepdims=True))
        if has_res:
            dx = dx + dres_ref[...]
        dx_ref[...] = dx

        @pl.when(pl.program_id(0) == 0)
        def _():
            dg_ref[...] = jnp.zeros_like(dg_ref)

        dg_ref[...] += jnp.sum(dyv * xh, axis=0, keepdims=True)

    row = pl.BlockSpec((tr, D), lambda i: (i, 0))
    vec = pl.BlockSpec((1, D), lambda i: (0, 0))
    ins = [x, g, dy] + ([dres] if has_res else [])
    return pl.pallas_call(
        body, grid=(L // tr,), in_specs=[row, vec, row] + ([row] if has_res else []),
        out_specs=[row, vec],
        out_shape=[jax.ShapeDtypeStruct((L, D), F32), jax.ShapeDtypeStruct((1, D), F32)],
        compiler_params=_cparams(("arbitrary",)), name=name)(*ins)


def _loss_head(h, g, target, *, name):
    L, D = h.shape
    tr = _tile(L, 256)

    def body(x_ref, g_ref, t_ref, loss_ref, dx_ref, dg_ref):
        xv = x_ref[...]
        r = lax.rsqrt(jnp.mean(xv * xv, axis=-1, keepdims=True) + NORM_EPS)
        xh = xv * r
        err = xh * g_ref[...] - t_ref[...]
        dyv = err * (1.0 / D)
        dyg = dyv * g_ref[...]
        dx_ref[...] = r * (dyg - xh * jnp.mean(dyg * xh, axis=-1, keepdims=True))

        @pl.when(pl.program_id(0) == 0)
        def _():
            dg_ref[...] = jnp.zeros_like(dg_ref)
            loss_ref[...] = jnp.zeros_like(loss_ref)

        dg_ref[...] += jnp.sum(dyv * xh, axis=0, keepdims=True)
        loss_ref[...] += 0.5 * jnp.sum(jnp.sum(err * err, axis=1, keepdims=True), axis=0, keepdims=True) * (1.0 / D)

    row = pl.BlockSpec((tr, D), lambda i: (i, 0))
    vec = pl.BlockSpec((1, D), lambda i: (0, 0))
    return pl.pallas_call(
        body, grid=(L // tr,), in_specs=[row, vec, row],
        out_specs=[pl.BlockSpec((1, 1), lambda i: (0, 0)), row, vec],
        out_shape=[jax.ShapeDtypeStruct((1, 1), F32), jax.ShapeDtypeStruct((L, D), F32), jax.ShapeDtypeStruct((1, D), F32)],
        compiler_params=_cparams(("arbitrary",)), name=name)(h, g, target)


S5_TL = 256
S5_LW = 512


def _dot(a, b, dims):
    return lax.dot_general(a, b, (dims, ((), ())), preferred_element_type=F32)


_NN, _NT, _TN = ((1,), (0,)), ((1,), (1,)), ((0,), (0,))


def _gelu_and_grad(y):
    k = math.sqrt(2.0 / math.pi)
    y2 = y * y
    th = jnp.tanh(k * (y + 0.044715 * y * y2))
    g = 0.5 * y * (1.0 + th)
    gp = 0.5 * (1.0 + th) + 0.5 * y * (1.0 - th * th) * k * (1.0 + 3.0 * 0.044715 * y2)
    return g, gp


def _scan_tiles(re_s, im_s, ls, lw, pw_ref, tr_ref, ti_ref, carry_s, nt, reverse, extra=None):
    row = lax.broadcasted_iota(jnp.int32, (8, lw), 0)
    a = [pw_ref[k:k + 1, ls] for k in range(6)]
    tr_t, ti_t = tr_ref[:, ls], ti_ref[:, ls]
    edge = 0 if reverse else 7

    def tile(jj, carry):
        cr, ci, acc = carry
        j = (nt - 1 - jj) if reverse else jj
        off = pl.multiple_of(j * 8, 8)
        sr, si = re_s[pl.ds(off, 8), ls], im_s[pl.ds(off, 8), ls]
        for n, k in enumerate((1, 2, 4)):
            akr, aki = a[2 * n], a[2 * n + 1]
            if reverse:
                keep, sh = row < 8 - k, 8 - k
            else:
                keep, sh = row >= k, k
            shr = jnp.where(keep, pltpu.roll(sr, sh, 0), 0.0)
            shi = jnp.where(keep, pltpu.roll(si, sh, 0), 0.0)
            sr, si = sr + akr * shr - aki * shi, si + akr * shi + aki * shr
        xr = sr + tr_t * cr - ti_t * ci
        xi = si + tr_t * ci + ti_t * cr
        re_s[pl.ds(off, 8), ls] = xr
        im_s[pl.ds(off, 8), ls] = xi
        if extra is not None:
            acc = extra(off, xr, xi, acc, row)
        return xr[edge:edge + 1, :], xi[edge:edge + 1, :], acc

    acc0 = (jnp.zeros((8, lw), F32), jnp.zeros((8, lw), F32)) if extra is not None else 0
    cr, ci, acc = lax.fori_loop(0, nt, tile, (carry_s[0:1, ls], carry_s[1:2, ls], acc0))
    carry_s[0:1, ls] = cr
    carry_s[1:2, ls] = ci
    return acc


def _s5_fwd(proj, bdr, bdi, pw, tr, ti, cdr, cdi, dskip, wglu, *, name):
    L = proj.shape[0]
    GW, S = bdr.shape
    TL = _tile(L, S5_TL)
    lw = min(S, S5_LW)

    def body(u_ref, bdr_ref, bdi_ref, pw_ref, tr_ref, ti_ref, cdr_ref, cdi_ref, d_ref, w_ref,
             y_ref, xr_ref, xi_ref, re_s, im_s, carry_s):
        @pl.when(pl.program_id(0) == 0)
        def _():
            carry_s[...] = jnp.zeros_like(carry_s)

        u = u_ref[...]
        ub = u.astype(BF16)
        re_s[...] = _dot(ub, bdr_ref[...], _NN)
        im_s[...] = _dot(ub, bdi_ref[...], _NN)
        for lc in range(S // lw):
            _scan_tiles(re_s, im_s, slice(lc * lw, (lc + 1) * lw), lw, pw_ref, tr_ref, ti_ref, carry_s, TL // 8, False)
        xrb, xib = re_s[...].astype(BF16), im_s[...].astype(BF16)
        xr_ref[...] = xrb
        xi_ref[...] = xib
        y = _dot(xrb, cdr_ref[...], _NN) - _dot(xib, cdi_ref[...], _NN) + d_ref[...] * u
        g, _ = _gelu_and_grad(y)
        z = _dot(g.astype(BF16), w_ref[...], _NN)
        y_ref[...] = g * jax.nn.sigmoid(z)

    full = lambda arr: pl.BlockSpec(arr.shape, lambda i: (0,) * arr.ndim)
    return pl.pallas_call(
        body, grid=(L // TL,),
        in_specs=[pl.BlockSpec((TL, GW), lambda i: (i, 0))] + [full(t) for t in (bdr, bdi, pw, tr, ti, cdr, cdi, dskip, wglu)],
        out_specs=[pl.BlockSpec((TL, GW), lambda i: (i, 0)), pl.BlockSpec((TL, S), lambda i: (i, 0)), pl.BlockSpec((TL, S), lambda i: (i, 0))],
        out_shape=[jax.ShapeDtypeStruct((L, GW), F32), jax.ShapeDtypeStruct((L, S), BF16), jax.ShapeDtypeStruct((L, S), BF16)],
        scratch_shapes=[pltpu.VMEM((TL, S), F32), pltpu.VMEM((TL, S), F32), pltpu.VMEM((8, S), F32)],
        compiler_params=_cparams(("arbitrary",), VMEM_LIMIT), name=name,
    )(proj, bdr, bdi, pw, tr, ti, cdr, cdi, dskip, wglu)


def _s5_bwd(proj, xr, xi, dout, bdr, bdi, pwc, trc, tic, cdr, cdi, dskip, wglu, *, name):
    L = proj.shape[0]
    GW, S = bdr.shape
    TL = _tile(L, S5_TL)
    nc = L // TL
    lw = min(S, S5_LW)

    def body(u_ref, xr_ref, xi_ref, xrp_ref, xip_ref, do_ref, bdr_ref, bdi_ref, pw_ref, tr_ref, ti_ref, cdr_ref, cdi_ref,
             d_ref, w_ref, du_ref, dw_ref, dd_ref, dcdr_ref, dcdi_ref, dbdr_ref, dbdi_ref, dar_ref, dai_ref,
             gr_s, gi_s, xfr, xfi, carry_s):
        i = pl.program_id(0)

        @pl.when(i == 0)
        def _():
            carry_s[...] = jnp.zeros_like(carry_s)
            for r in (dw_ref, dd_ref, dcdr_ref, dcdi_ref, dbdr_ref, dbdi_ref, dar_ref, dai_ref):
                r[...] = jnp.zeros_like(r)

        u = u_ref[...]
        ub = u.astype(BF16)
        xrb, xib = xr_ref[...], xi_ref[...]
        y = _dot(xrb, cdr_ref[...], _NN) - _dot(xib, cdi_ref[...], _NN) + d_ref[...] * u
        g, gp = _gelu_and_grad(y)
        gb = g.astype(BF16)
        sg = jax.nn.sigmoid(_dot(gb, w_ref[...], _NN))
        dout = do_ref[...]
        dz = (dout * g * sg * (1.0 - sg)).astype(BF16)
        dg = dout * sg + _dot(dz, w_ref[...], _NT)
        dw_ref[...] += _dot(gb, dz, _TN)
        dy = dg * gp
        dyb = dy.astype(BF16)
        dd_ref[...] += jnp.sum(dy * u, axis=0, keepdims=True)
        gr_s[...] = _dot(dyb, cdr_ref[...], _NT)
        gi_s[...] = -_dot(dyb, cdi_ref[...], _NT)
        dcdr_ref[...] += _dot(xrb, dyb, _TN)
        dcdi_ref[...] -= _dot(xib, dyb, _TN)
        live = (i < nc - 1).astype(F32)
        xfr[0:8, :] = xrp_ref[...].astype(F32) * live
        xfi[0:8, :] = xip_ref[...].astype(F32) * live
        xfr[8:, :] = xrb.astype(F32)
        xfi[8:, :] = xib.astype(F32)
        for lc in range(S // lw):
            ls = slice(lc * lw, (lc + 1) * lw)

            def extra(off, lr, li, acc, row, ls=ls):
                cur_r, cur_i = xfr[pl.ds(off + 8, 8), ls], xfi[pl.ds(off + 8, 8), ls]
                prv_r, prv_i = xfr[pl.ds(off, 8), ls], xfi[pl.ds(off, 8), ls]
                xpr = jnp.where(row >= 1, pltpu.roll(cur_r, 1, 0), prv_r[7:8, :])
                xpi = jnp.where(row >= 1, pltpu.roll(cur_i, 1, 0), prv_i[7:8, :])
                return acc[0] + lr * xpr + li * xpi, acc[1] - lr * xpi + li * xpr

            acc = _scan_tiles(gr_s, gi_s, ls, lw, pw_ref, tr_ref, ti_ref, carry_s, TL // 8, True, extra)
            dar_ref[:, ls] += acc[0]
            dai_ref[:, ls] += acc[1]
        lrb, lib = gr_s[...].astype(BF16), gi_s[...].astype(BF16)
        du_ref[...] = dy * d_ref[...] + _dot(lrb, bdr_ref[...], _NT) + _dot(lib, bdi_ref[...], _NT)
        dbdr_ref[...] += _dot(ub, lrb, _TN)
        dbdi_ref[...] += _dot(ub, lib, _TN)

    rev = lambda i: nc - 1 - i
    full = lambda arr: pl.BlockSpec(arr.shape, lambda i: (0,) * arr.ndim)
    chunk = lambda w: pl.BlockSpec((TL, w), lambda i: (rev(i), 0))
    prev8 = pl.BlockSpec((8, S), lambda i: (jnp.maximum(rev(i) * (TL // 8) - 1, 0), 0))
    acc_shapes = [(GW, GW), (1, GW), (S, GW), (S, GW), (GW, S), (GW, S), (8, S), (8, S)]
    return pl.pallas_call(
        body, grid=(nc,),
        in_specs=[chunk(GW), chunk(S), chunk(S), prev8, prev8, chunk(GW)] + [full(t) for t in (bdr, bdi, pwc, trc, tic, cdr, cdi, dskip, wglu)],
        out_specs=[chunk(GW)] + [pl.BlockSpec(s, lambda i: (0, 0)) for s in acc_shapes],
        out_shape=[jax.ShapeDtypeStruct((L, GW), F32)] + [jax.ShapeDtypeStruct(s, F32) for s in acc_shapes],
        scratch_shapes=[pltpu.VMEM((TL, S), F32), pltpu.VMEM((TL, S), F32), pltpu.VMEM((TL + 8, S), F32),
                        pltpu.VMEM((TL + 8, S), F32), pltpu.VMEM((8, S), F32)],
        compiler_params=_cparams(("arbitrary",), VMEM_LIMIT), name=name,
    )(proj, xr, xi, xr, xi, dout, bdr, bdi, pwc, trc, tic, cdr, cdi, dskip, wglu)


def _cpow_tables(ar, ai):
    def cm(a, b):
        return a[0] * b[0] - a[1] * b[1], a[0] * b[1] + a[1] * b[0]
    p = [(ar, ai)]
    for _ in range(7):
        p.append(cm(p[-1], p[0]))
    z = jnp.zeros_like(ar)
    pw = jnp.stack([p[0][0], p[0][1], p[1][0], p[1][1], p[3][0], p[3][1], z, z])
    return pw, jnp.stack([q[0] for q in p]), jnp.stack([q[1] for q in p])


def _s5_disc(lam_re, lam_im, log_dt, b_re, b_im):
    dt = jnp.exp(log_dt)[:, None]
    mag = jnp.exp(lam_re * dt)
    ab_r, ab_i = mag * jnp.cos(lam_im * dt), mag * jnp.sin(lam_im * dt)
    den = lam_re * lam_re + lam_im * lam_im
    nr, ni = ab_r - 1.0, ab_i
    f_r = ((nr * lam_re + ni * lam_im) / den)[..., None]
    f_i = ((ni * lam_re - nr * lam_im) / den)[..., None]
    return ab_r, ab_i, f_r * b_re - f_i * b_im, f_r * b_im + f_i * b_re


def _bd(t):
    G, A, B = t.shape
    return (t[:, :, None, :] * jnp.eye(G, dtype=t.dtype)[:, None, :, None]).reshape(G * A, G * B)


def _bd_extract(m, G):
    A, B = m.shape[0] // G, m.shape[1] // G
    return jnp.sum(m.reshape(G, A, G, B) * jnp.eye(G, dtype=m.dtype)[:, None, :, None], axis=2)


POOL_TQ = 256


def _band(shape, row0, col0, win, transpose):
    r = lax.broadcasted_iota(jnp.int32, shape, 0) + row0
    c = lax.broadcasted_iota(jnp.int32, shape, 1) + col0
    d = (c - r) if transpose else (r - c)
    return ((d >= 0) & (d < win)).astype(BF16)


def _band_dot(band, v):
    hi = v.astype(BF16)
    lo = (v - hi.astype(F32)).astype(BF16)
    return _dot(band, hi, _NN) + _dot(band, lo, _NN)


def _pool_p(u_prev, u_cur, i, win, tq):
    t0 = i * tq
    cnt = jnp.minimum(lax.broadcasted_iota(jnp.int32, (tq, 1), 0) + t0 + 1, win).astype(F32)
    live = (i > 0).astype(F32)
    acc = _band_dot(_band((tq, tq), t0, t0, win, False), u_cur)
    acc += _band_dot(_band((tq, tq), t0, t0 - tq, win, False), u_prev * live)
    return acc / cnt - u_cur


def _pool_fwd(proj, pool_w, pool_scale3, *, name):
    L = proj.shape[0]
    nw, PC, _ = pool_w.shape
    tq = _tile(L, POOL_TQ)

    def body(up_ref, uc_ref, w_ref, s_ref, y_ref):
        g, i = pl.program_id(0), pl.program_id(1)
        win = jnp.left_shift(2, g)
        p = _pool_p(up_ref[...], uc_ref[...], i, win, tq)
        y_ref[...] = _dot(p.astype(BF16), w_ref[...].astype(BF16), _NN) * s_ref[...]

    return pl.pallas_call(
        body, grid=(nw, L // tq),
        in_specs=[pl.BlockSpec((tq, PC), lambda g, i: (jnp.maximum(i - 1, 0), nw + g)),
                  pl.BlockSpec((tq, PC), lambda g, i: (i, nw + g)),
                  pl.BlockSpec((None, PC, PC), lambda g, i: (g, 0, 0)),
                  pl.BlockSpec((None, 1, PC), lambda g, i: (g, 0, 0))],
        out_specs=pl.BlockSpec((tq, PC), lambda g, i: (i, g)),
        out_shape=jax.ShapeDtypeStruct((L, nw * PC), F32),
        compiler_params=_cparams(("parallel", "parallel")), name=name)(proj, proj, pool_w, pool_scale3)


def _pool_bwd(proj, dy, pool_w, pool_scale3, *, name):
    L = proj.shape[0]
    nw, PC, _ = pool_w.shape
    tq = _tile(L, POOL_TQ)
    nq = L // tq

    def body(up_ref, uc_ref, dyc_ref, dyn_ref, w_ref, s_ref, du_ref, dw_ref, ds_ref):
        g, i = pl.program_id(0), pl.program_id(1)
        win = jnp.left_shift(2, g)

        @pl.when(i == 0)
        def _():
            dw_ref[...] = jnp.zeros_like(dw_ref)
            ds_ref[...] = jnp.zeros_like(ds_ref)

        wb = w_ref[...].astype(BF16)
        p = _pool_p(up_ref[...], uc_ref[...], i, win, tq).astype(BF16)
        dyc = dyc_ref[...]
        ds_ref[...] += jnp.sum(dyc * _dot(p, wb, _NN), axis=0, keepdims=True)
        dys = (dyc * s_ref[...]).astype(BF16)
        dw_ref[...] += _dot(p, dys, _TN)
        t0 = i * tq
        rows = lax.broadcasted_iota(jnp.int32, (tq, 1), 0)
        dp_c = _dot(dys, wb, _NT)
        q_c = dp_c / jnp.minimum(rows + t0 + 1, win).astype(F32)
        live = (i < nq - 1).astype(F32)
        dp_n = _dot((dyn_ref[...] * s_ref[...] * live).astype(BF16), wb, _NT)
        q_n = dp_n / jnp.minimum(rows + t0 + tq + 1, win).astype(F32)
        du = _band_dot(_band((tq, tq), t0, t0, win, True), q_c) + _band_dot(_band((tq, tq), t0, t0 + tq, win, True), q_n)
        du_ref[...] = du - dp_c

    return pl.pallas_call(
        body, grid=(nw, nq),
        in_specs=[pl.BlockSpec((tq, PC), lambda g, i: (jnp.maximum(i - 1, 0), nw + g)),
                  pl.BlockSpec((tq, PC), lambda g, i: (i, nw + g)),
                  pl.BlockSpec((tq, PC), lambda g, i: (i, g)),
                  pl.BlockSpec((tq, PC), lambda g, i: (jnp.minimum(i + 1, nq - 1), g)),
                  pl.BlockSpec((None, PC, PC), lambda g, i: (g, 0, 0)),
                  pl.BlockSpec((None, 1, PC), lambda g, i: (g, 0, 0))],
        out_specs=[pl.BlockSpec((tq, PC), lambda g, i: (i, g)),
                   pl.BlockSpec((None, PC, PC), lambda g, i: (g, 0, 0)),
                   pl.BlockSpec((None, 1, PC), lambda g, i: (g, 0, 0))],
        out_shape=[jax.ShapeDtypeStruct((L, nw * PC), F32), jax.ShapeDtypeStruct((nw, PC, PC), F32),
                   jax.ShapeDtypeStruct((nw, 1, PC), F32)],
        compiler_params=_cparams(("parallel", "arbitrary")), name=name)(proj, proj, dy, dy, pool_w, pool_scale3)


CONV_RC = 256


def _conv1_fwd(proj, w_dw, b_dw, *, name):
    L = proj.shape[0]
    GW = w_dw.shape[1]
    CB = min(GW, 128)
    nb = GW // CB
    RC = _tile(L, CONV_RC)
    n = RC + CONV_PAD

    def body(val_ref, gate_ref, w_ref, b_ref, o_ref, hp):
        hp[0:CONV_PAD, :] = jnp.zeros((CONV_PAD, CB), F32)
        hp[CONV_PAD:, :] = val_ref[...] * jax.nn.sigmoid(gate_ref[...])
        wv = w_ref[...]

        def chunk(ci, carry):
            c0 = pl.multiple_of(ci * RC, 8)
            win = hp[pl.ds(c0, n), :]
            acc = jnp.zeros((RC, CB), F32) + b_ref[...]
            for k in range(CONV_WIDTH):
                acc = acc + wv[k:k + 1, :] * pltpu.roll(win, n - (k + 2), 0)[0:RC]
            o_ref[pl.ds(c0, RC), :] = acc
            return carry

        lax.fori_loop(0, L // RC, chunk, 0)

    col = lambda off: pl.BlockSpec((L, CB), lambda c: (0, off * nb + c))
    return pl.pallas_call(
        body, grid=(nb,),
        in_specs=[col(2), col(3), pl.BlockSpec((CONV_PAD, CB), lambda c: (0, c)), pl.BlockSpec((1, CB), lambda c: (0, c))],
        out_specs=pl.BlockSpec((L, CB), lambda c: (0, c)),
        out_shape=jax.ShapeDtypeStruct((L, GW), F32),
        scratch_shapes=[pltpu.VMEM((L + CONV_PAD, CB), F32)],
        compiler_params=_cparams(("parallel",)), name=name)(proj, proj, w_dw, b_dw)


def _conv1_bwd(proj, dhc, w_dw, *, name):
    L = proj.shape[0]
    GW = w_dw.shape[1]
    CB = min(GW, 128)
    nb = GW // CB
    RC = _tile(L, CONV_RC)
    n = RC + CONV_PAD

    def body(val_ref, gate_ref, d_ref, w_ref, dval_ref, dgate_ref, dw_ref, db_ref, hp, dp):
        val = val_ref[...]
        sg = jax.nn.sigmoid(gate_ref[...])
        hp[0:CONV_PAD, :] = jnp.zeros((CONV_PAD, CB), F32)
        hp[CONV_PAD:, :] = val * sg
        dp[0:L, :] = d_ref[...]
        dp[L:, :] = jnp.zeros((CONV_PAD, CB), F32)
        dw_ref[...] = jnp.zeros_like(dw_ref)
        db_ref[...] = jnp.sum(d_ref[...], axis=0, keepdims=True)
        wv = w_ref[...]

        def chunk(ci, carry):
            c0 = pl.multiple_of(ci * RC, 8)
            win = hp[pl.ds(c0, n), :]
            dwin = dp[pl.ds(c0, n), :]
            d = dwin[0:RC]
            dh = jnp.zeros((RC, CB), F32)
            for k in range(CONV_WIDTH):
                dw_ref[k:k + 1, :] += jnp.sum(d * pltpu.roll(win, n - (k + 2), 0)[0:RC], axis=0, keepdims=True)
                sh = CONV_WIDTH - 1 - k
                dh = dh + wv[k:k + 1, :] * (pltpu.roll(dwin, n - sh, 0)[0:RC] if sh else d)
            v, s = val_ref[pl.ds(c0, RC), :], jax.nn.sigmoid(gate_ref[pl.ds(c0, RC), :])
            dval_ref[pl.ds(c0, RC), :] = dh * s
            dgate_ref[pl.ds(c0, RC), :] = dh * v * s * (1.0 - s)
            return carry

        lax.fori_loop(0, L // RC, chunk, 0)

    col = lambda off: pl.BlockSpec((L, CB), lambda c: (0, off * nb + c))
    own = pl.BlockSpec((L, CB), lambda c: (0, c))
    return pl.pallas_call(
        body, grid=(nb,),
        in_specs=[col(2), col(3), own, pl.BlockSpec((CONV_PAD, CB), lambda c: (0, c))],
        out_specs=[own, own, pl.BlockSpec((CONV_PAD, CB), lambda c: (0, c)), pl.BlockSpec((1, CB), lambda c: (0, c))],
        out_shape=[jax.ShapeDtypeStruct((L, GW), F32), jax.ShapeDtypeStruct((L, GW), F32),
                   jax.ShapeDtypeStruct((CONV_PAD, GW), F32), jax.ShapeDtypeStruct((1, GW), F32)],
        scratch_shapes=[pltpu.VMEM((L + CONV_PAD, CB), F32), pltpu.VMEM((L + CONV_PAD, CB), F32)],
        compiler_params=_cparams(("parallel",)), name=name)(proj, proj, dhc, w_dw)


def _ln_silu(hc, g, b):
    mu = jnp.mean(hc, axis=-1, keepdims=True)
    xc = hc - mu
    r = lax.rsqrt(jnp.mean(xc * xc, axis=-1, keepdims=True) + NORM_EPS)
    xh = xc * r
    a = xh * g + b
    sg = jax.nn.sigmoid(a)
    return xh, r, a, sg


def _conv2_fwd(hc, ln_g, ln_b, w_pw, *, name):
    L, GW = hc.shape
    tr = _tile(L, 256)

    def body(h_ref, g_ref, b_ref, w_ref, y_ref):
        _, _, a, sg = _ln_silu(h_ref[...], g_ref[...], b_ref[...])
        y_ref[...] = _dot((a * sg).astype(BF16), w_ref[...], _NN)

    row = pl.BlockSpec((tr, GW), lambda i: (i, 0))
    vec = pl.BlockSpec((1, GW), lambda i: (0, 0))
    return pl.pallas_call(
        body, grid=(L // tr,), in_specs=[row, vec, vec, pl.BlockSpec((GW, GW), lambda i: (0, 0))],
        out_specs=row, out_shape=jax.ShapeDtypeStruct((L, GW), F32),
        compiler_params=_cparams(("parallel",)), name=name)(hc, ln_g, ln_b, w_pw)


def _conv2_bwd(hc, ln_g, ln_b, w_pw, dy, *, name):
    L, GW = hc.shape
    tr = _tile(L, 256)

    def body(h_ref, g_ref, b_ref, w_ref, dy_ref, dh_ref, dg_ref, db_ref, dw_ref):
        @pl.when(pl.program_id(0) == 0)
        def _():
            for r in (dg_ref, db_ref, dw_ref):
                r[...] = jnp.zeros_like(r)

        xh, r, a, sg = _ln_silu(h_ref[...], g_ref[...], b_ref[...])
        dyb = dy_ref[...].astype(BF16)
        dw_ref[...] += _dot((a * sg).astype(BF16), dyb, _TN)
        da = _dot(dyb, w_ref[...], _NT) * (sg + a * sg * (1.0 - sg))
        dg_ref[...] += jnp.sum(da * xh, axis=0, keepdims=True)
        db_ref[...] += jnp.sum(da, axis=0, keepdims=True)
        dxh = da * g_ref[...]
        dh_ref[...] = r * (dxh - jnp.mean(dxh, axis=-1, keepdims=True) - xh * jnp.mean(dxh * xh, axis=-1, keepdims=True))

    row = pl.BlockSpec((tr, GW), lambda i: (i, 0))
    vec = pl.BlockSpec((1, GW), lambda i: (0, 0))
    mat = pl.BlockSpec((GW, GW), lambda i: (0, 0))
    return pl.pallas_call(
        body, grid=(L // tr,), in_specs=[row, vec, vec, mat, row],
        out_specs=[row, vec, vec, mat],
        out_shape=[jax.ShapeDtypeStruct((L, GW), F32), jax.ShapeDtypeStruct((1, GW), F32), jax.ShapeDtypeStruct((1, GW), F32),
                   jax.ShapeDtypeStruct((GW, GW), F32)],
        compiler_params=_cparams(("arbitrary",)), name=name)(hc, ln_g, ln_b, w_pw, dy)


def _grp_fwd(ys, g, *, name):
    L, GW = ys[0].shape
    tr = _tile(L, 256)

    def body(*refs):
        g_ref, o_ref = refs[4], refs[5]
        for m in range(N_MIXERS):
            yv = refs[m][...]
            r = lax.rsqrt(jnp.mean(yv * yv, axis=-1, keepdims=True) + NORM_EPS)
            o_ref[:, m * GW:(m + 1) * GW] = (yv * r * g_ref[:, m * GW:(m + 1) * GW]).astype(o_ref.dtype)

    row = pl.BlockSpec((tr, GW), lambda i: (i, 0))
    return pl.pallas_call(
        body, grid=(L // tr,), in_specs=[row] * 4 + [pl.BlockSpec((1, N_MIXERS * GW), lambda i: (0, 0))],
        out_specs=pl.BlockSpec((tr, N_MIXERS * GW), lambda i: (i, 0)),
        out_shape=jax.ShapeDtypeStruct((L, N_MIXERS * GW), BF16),
        compiler_params=_cparams(("parallel",)), name=name)(*ys, g)


def _grp_bwd(ys, g, dy, *, name):
    L, GW = ys[0].shape
    tr = _tile(L, 256)

    def body(*refs):
        g_ref, dy_ref = refs[4], refs[5]
        outs, dg_ref = refs[6:10], refs[10]

        @pl.when(pl.program_id(0) == 0)
        def _():
            dg_ref[...] = jnp.zeros_like(dg_ref)

        for m in range(N_MIXERS):
            cs = slice(m * GW, (m + 1) * GW)
            yv = refs[m][...]
            r = lax.rsqrt(jnp.mean(yv * yv, axis=-1, keepdims=True) + NORM_EPS)
            xh = yv * r
            dyv = dy_ref[:, cs]
            dyg = dyv * g_ref[:, cs]
            outs[m][...] = r * (dyg - xh * jnp.mean(dyg * xh, axis=-1, keepdims=True))
            dg_ref[:, cs] += jnp.sum(dyv * xh, axis=0, keepdims=True)

    row = pl.BlockSpec((tr, GW), lambda i: (i, 0))
    wide = pl.BlockSpec((tr, N_MIXERS * GW), lambda i: (i, 0))
    vec = pl.BlockSpec((1, N_MIXERS * GW), lambda i: (0, 0))
    return pl.pallas_call(
        body, grid=(L // tr,), in_specs=[row] * 4 + [vec, wide],
        out_specs=[row] * 4 + [vec],
        out_shape=[jax.ShapeDtypeStruct((L, GW), F32)] * 4 + [jax.ShapeDtypeStruct((1, N_MIXERS * GW), F32)],
        compiler_params=_cparams(("arbitrary",)), name=name)(*ys, g, dy)


def _att_first(j, br, nblk):
    per = lax.shift_right_logical(jnp.int32(nblk), 2 * br)
    return jnp.bitwise_and(j, per - 1) == 0


def _att_fwd(q, k, vx, bias, *, name):
    nbr, H, L, E = q.shape
    B = ATT_BLOCK
    nblk = L // B

    def body(q_ref, k_ref, v_ref, b_ref, o_ref):
        br = pl.program_id(0)
        lane = lax.broadcasted_iota(jnp.int32, (B, 2 * E), 1)

        def blk(j, carry):
            off = pl.multiple_of(j * B, B)
            poff = pl.multiple_of(jnp.maximum(j - 1, 0) * B, B)
            qv = q_ref[pl.ds(off, B), :]
            s_p = _dot(qv, k_ref[pl.ds(poff, B), :], _NT) + b_ref[:, 0:B]
            s_c = _dot(qv, k_ref[pl.ds(off, B), :], _NT) + b_ref[:, B:2 * B]
            s_p = jnp.where(_att_first(j, br, nblk), NEG_INF, s_p)
            m = jnp.maximum(jnp.max(s_p, axis=-1, keepdims=True), jnp.max(s_c, axis=-1, keepdims=True))
            p_p, p_c = jnp.exp(s_p - m), jnp.exp(s_c - m)
            den = jnp.sum(p_p, axis=-1, keepdims=True) + jnp.sum(p_c, axis=-1, keepdims=True)
            acc = _dot(p_p.astype(BF16), v_ref[pl.ds(poff, B), :], _NN) + _dot(p_c.astype(BF16), v_ref[pl.ds(off, B), :], _NN)
            o_ref[pl.ds(off, B), :] = jnp.where(lane < E, acc / den, m + jnp.log(den))
            return carry

        lax.fori_loop(0, nblk, blk, 0)

    seq = lambda w: pl.BlockSpec((None, None, L, w), lambda b, h: (b, h, 0, 0))
    return pl.pallas_call(
        body, grid=(nbr, H),
        in_specs=[seq(E), seq(E), seq(2 * E), pl.BlockSpec((None, None, B, 2 * B), lambda b, h: (b, h, 0, 0))],
        out_specs=seq(2 * E), out_shape=jax.ShapeDtypeStruct((nbr, H, L, 2 * E), F32),
        compiler_params=_cparams(("parallel", "parallel")), name=name)(q, k, vx, bias)


def _att_bwd(q, k, vx, bias, ox, dox, *, name):
    nbr, H, L, E = q.shape
    B = ATT_BLOCK
    nblk = L // B

    def body(q_ref, k_ref, v_ref, b_ref, o_ref, do_ref, dq_ref, dk_ref, dv_ref, db_ref):
        br = pl.program_id(0)
        dk_ref[...] = jnp.zeros_like(dk_ref)
        dv_ref[...] = jnp.zeros_like(dv_ref)
        db_ref[...] = jnp.zeros_like(db_ref)
        lane = lax.broadcasted_iota(jnp.int32, (B, 2 * E), 1)

        def blk(j, carry):
            off = pl.multiple_of(j * B, B)
            poff = pl.multiple_of(jnp.maximum(j - 1, 0) * B, B)
            qv, kc, kp = q_ref[pl.ds(off, B), :], k_ref[pl.ds(off, B), :], k_ref[pl.ds(poff, B), :]
            vc, vp = v_ref[pl.ds(off, B), :], v_ref[pl.ds(poff, B), :]
            oe, de = o_ref[pl.ds(off, B), :], do_ref[pl.ds(off, B), :]
            lse, dlse = oe[:, E:E + 1], de[:, E:E + 1]
            do = jnp.where(lane < E, de, 0.0)
            dm = jnp.sum(do * oe, axis=-1, keepdims=True) - dlse
            dob = do.astype(BF16)
            s_p = _dot(qv, kp, _NT) + b_ref[:, 0:B]
            s_c = _dot(qv, kc, _NT) + b_ref[:, B:2 * B]
            s_p = jnp.where(_att_first(j, br, nblk), NEG_INF, s_p)
            p_p, p_c = jnp.exp(s_p - lse), jnp.exp(s_c - lse)
            ds_p = p_p * (_dot(dob, vp, _NT) - dm)
            ds_c = p_c * (_dot(dob, vc, _NT) - dm)
            db_ref[:, 0:B] += ds_p
            db_ref[:, B:2 * B] += ds_c
            dsb_p, dsb_c = ds_p.astype(BF16), ds_c.astype(BF16)
            dq_ref[pl.ds(off, B), :] = _dot(dsb_p, kp, _NN) + _dot(dsb_c, kc, _NN)
            dk_ref[pl.ds(poff, B), :] += _dot(dsb_p, qv, _TN)
            dk_ref[pl.ds(off, B), :] += _dot(dsb_c, qv, _TN)
            dv_ref[pl.ds(poff, B), :] += _dot(p_p.astype(BF16), dob, _TN)
            dv_ref[pl.ds(off, B), :] += _dot(p_c.astype(BF16), dob, _TN)
            return carry

        lax.fori_loop(0, nblk, blk, 0)

    seq = lambda w: pl.BlockSpec((None, None, L, w), lambda b, h: (b, h, 0, 0))
    bsp = pl.BlockSpec((None, None, B, 2 * B), lambda b, h: (b, h, 0, 0))
    return pl.pallas_call(
        body, grid=(nbr, H),
        in_specs=[seq(E), seq(E), seq(2 * E), bsp, seq(2 * E), seq(2 * E)],
        out_specs=[seq(E), seq(E), seq(2 * E), bsp],
        out_shape=[jax.ShapeDtypeStruct((nbr, H, L, E), F32), jax.ShapeDtypeStruct((nbr, H, L, E), F32),
                   jax.ShapeDtypeStruct((nbr, H, L, 2 * E), F32), jax.ShapeDtypeStruct((nbr, H, B, 2 * B), F32)],
        compiler_params=_cparams(("parallel", "parallel")), name=name)(q, k, vx, bias, ox, dox)


def _mix_weights(xs, lane, E, W):
    al = [jnp.where(lane < E, pltpu.roll(x, W - E, 1), x) for x in xs]
    m = functools.reduce(jnp.maximum, al)
    es = [jnp.exp(a - m) for a in al]
    tot = functools.reduce(lambda a, b: a + b, es)
    return [e / tot for e in es]


def _mix_fwd(ox, *, name):
    nbr, L, W = ox.shape
    E = W // ATT_HEADS // 2
    tr = _tile(L, 256)

    def body(x_ref, o_ref):
        lane = lax.broadcasted_iota(jnp.int32, (tr, W), 1) % (2 * E)
        xs = [x_ref[b] for b in range(nbr)]
        ws = _mix_weights(xs, lane, E, W)
        o_ref[...] = functools.reduce(lambda a, b: a + b, [w * x for w, x in zip(ws, xs)])

    return pl.pallas_call(
        body, grid=(L // tr,), in_specs=[pl.BlockSpec((nbr, tr, W), lambda i: (0, i, 0))],
        out_specs=pl.BlockSpec((tr, W), lambda i: (i, 0)), out_shape=jax.ShapeDtypeStruct((L, W), F32),
        compiler_params=_cparams(("parallel",)), name=name)(ox)


def _mix_bwd(ox, dy, *, name):
    nbr, L, W = ox.shape
    E = W // ATT_HEADS // 2
    tr = _tile(L, 256)

    def body(x_ref, dy_ref, o_ref):
        lane = lax.broadcasted_iota(jnp.int32, (tr, W), 1) % (2 * E)
        xs = [x_ref[b] for b in range(nbr)]
        ws = _mix_weights(xs, lane, E, W)
        mix = functools.reduce(lambda a, b: a + b, [w * x for w, x in zip(ws, xs)])
        dyv = jnp.where(lane < E, dy_ref[...], 0.0)
        r = lax.broadcasted_iota(jnp.int32, (W, W), 0)
        c = lax.broadcasted_iota(jnp.int32, (W, W), 1)
        seg = ((r // (2 * E) == c // (2 * E)) & (r % (2 * E) < E)).astype(F32)
        for b in range(nbr):
            t = dyv * (xs[b] - mix)
            dl = ws[b] * jnp.dot(t, seg, preferred_element_type=F32, precision=lax.Precision.HIGHEST)
            o_ref[b] = jnp.where(lane < E, ws[b] * dyv, dl)

    return pl.pallas_call(
        body, grid=(L // tr,),
        in_specs=[pl.BlockSpec((nbr, tr, W), lambda i: (0, i, 0)), pl.BlockSpec((tr, W), lambda i: (i, 0))],
        out_specs=pl.BlockSpec((nbr, tr, W), lambda i: (0, i, 0)), out_shape=jax.ShapeDtypeStruct((nbr, L, W), F32),
        compiler_params=_cparams(("parallel",), VMEM_LIMIT), name=name)(ox, dy)


def _t5_bucket(dist):
    n = np.maximum(dist, 0)
    max_exact = REL_BUCKETS // 2
    large = max_exact + (np.log(np.maximum(n, 1) / max_exact) / np.log(REL_MAX_DIST / max_exact)
                         * (REL_BUCKETS - max_exact)).astype(np.int64)
    large = np.minimum(large, REL_BUCKETS - 1)
    return np.where(n < max_exact, n, large).astype(np.int32)


def _att_tables():
    a = np.arange(ATT_BLOCK)[:, None]
    b = np.arange(2 * ATT_BLOCK)[None, :]
    sub = a + ATT_BLOCK - b
    buckets, valids = [], []
    for window, dil in DILATED_PATTERNS:
        buckets.append(_t5_bucket(sub * dil))
        valids.append((sub >= 0) & (sub <= window // dil))
    return np.stack(buckets), np.stack(valids)


def _to_branches(t, H):
    L = t.shape[0]
    E = t.shape[1] // H
    out = []
    for _, dil in DILATED_PATTERNS:
        out.append(t.reshape(L // dil, dil, H, E).transpose(2, 1, 0, 3).reshape(H, L, E))
    return jnp.stack(out)


def _from_branches(t):
    _, H, L, E = t.shape
    out = []
    for b, (_, dil) in enumerate(DILATED_PATTERNS):
        out.append(t[b].reshape(H, dil, L // dil, E).transpose(2, 1, 0, 3).reshape(L, H * E))
    return jnp.stack(out)


def _xatt_fwd(q, k, v, *, name):
    L, XW = q.shape
    M = k.shape[0]
    tr = _tile(L, 512)
    scale = X_HEAD_DIM ** -0.5

    def body(q_ref, k_ref, v_ref, o_ref):
        s = _dot(q_ref[...], k_ref[...], _NT) * scale
        p = jnp.exp(s - jnp.max(s, axis=-1, keepdims=True))
        den = jnp.sum(p, axis=-1, keepdims=True)
        o_ref[...] = (_dot(p.astype(BF16), v_ref[...], _NN) / den).astype(o_ref.dtype)

    qs = pl.BlockSpec((tr, X_HEAD_DIM), lambda h, i: (i, h))
    ks = pl.BlockSpec((M, X_HEAD_DIM), lambda h, i: (0, h))
    return pl.pallas_call(
        body, grid=(XW // X_HEAD_DIM, L // tr), in_specs=[qs, ks, ks], out_specs=qs,
        out_shape=jax.ShapeDtypeStruct((L, XW), BF16),
        compiler_params=_cparams(("parallel", "parallel")), name=name)(q, k, v)


def _xatt_bwd(q, k, v, do, *, name):
    L, XW = q.shape
    M = k.shape[0]
    tr = _tile(L, 512)
    scale = X_HEAD_DIM ** -0.5

    def body(q_ref, k_ref, v_ref, do_ref, dq_ref, dk_ref, dv_ref):
        @pl.when(pl.program_id(1) == 0)
        def _():
            dk_ref[...] = jnp.zeros_like(dk_ref)
            dv_ref[...] = jnp.zeros_like(dv_ref)

        qv, kv, vv = q_ref[...], k_ref[...], v_ref[...]
        s = _dot(qv, kv, _NT) * scale
        p = jnp.exp(s - jnp.max(s, axis=-1, keepdims=True))
        p = p / jnp.sum(p, axis=-1, keepdims=True)
        dob = do_ref[...].astype(BF16)
        pb = p.astype(BF16)
        dv_ref[...] += _dot(pb, dob, _TN)
        dp = _dot(dob, vv, _NT)
        ds = (p * (dp - jnp.sum(dp * p, axis=-1, keepdims=True)) * scale).astype(BF16)
        dq_ref[...] = _dot(ds, kv, _NN)
        dk_ref[...] += _dot(ds, qv, _TN)

    qs = pl.BlockSpec((tr, X_HEAD_DIM), lambda h, i: (i, h))
    ks = pl.BlockSpec((M, X_HEAD_DIM), lambda h, i: (0, h))
    return pl.pallas_call(
        body, grid=(XW // X_HEAD_DIM, L // tr), in_specs=[qs, ks, ks, qs], out_specs=[qs, ks, ks],
        out_shape=[jax.ShapeDtypeStruct((L, XW), F32), jax.ShapeDtypeStruct((M, XW), F32), jax.ShapeDtypeStruct((M, XW), F32)],
        compiler_params=_cparams(("parallel", "arbitrary")), name=name)(q, k, v, do)


_ANY = pl.BlockSpec(memory_space=pl.ANY)


def _place():
    mx, my, mc = lax.axis_index("x"), lax.axis_index("y"), lax.axis_index("c")
    chips = [(1 - mx, my), (mx, 1 - my), (1 - mx, 1 - my)]
    return mx, my, mc, chips


def _rcopy(src, dst, ssem, rsem, dev):
    return pltpu.make_async_remote_copy(src_ref=src, dst_ref=dst, send_sem=ssem, recv_sem=rsem, device_id=dev,
                                        device_id_type=MESH)


STAGE_BYTES = 2 * 1024 * 1024


def _staged_copy(pairs, buf, isem, osem):
    n = len(pairs)
    ins = [pltpu.make_async_copy(src, buf.at[i % 2], isem.at[i % 2]) for i, (src, _) in enumerate(pairs)]
    outs = [pltpu.make_async_copy(buf.at[i % 2], dst, osem.at[i % 2]) for i, (_, dst) in enumerate(pairs)]
    ins[0].start()
    for i in range(n):
        if i + 1 < n:
            if i >= 1:
                outs[i - 1].wait()
            ins[i + 1].start()
        ins[i].wait()
        outs[i].start()
    for i in range(max(n - 2, 0), n):
        outs[i].wait()


_HBM = pl.BlockSpec(memory_space=pltpu.HBM)
_SEMS = pl.BlockSpec(memory_space=pltpu.SEMAPHORE)
_VM = pl.BlockSpec(memory_space=pltpu.VMEM)
_DATAFLOW = pltpu.SideEffectType.DATAFLOW_SIDE_EFFECTING


def _ag_copies(x_refs, land_refs, ssem, rsem, inbound):
    mx, my, mc, chips = _place()
    me = 2 * mx + my
    cps = []
    for w, (x_ref, land_ref) in enumerate(zip(x_refs, land_refs)):
        R2 = x_ref.shape[0] // 2
        mine = x_ref.at[pl.ds(mc * R2, R2)]
        for j, (px, py) in enumerate(chips):
            dst = land_ref.at[2 * px + py, mc] if inbound else land_ref.at[me, mc]
            cps.append(_rcopy(mine, dst, ssem.at[3 * w + j], rsem.at[3 * w + j], (px, py, mc)))
    return cps


def _ag_start(xs, token, *, name):
    n = len(xs)
    lands = [pltpu.with_memory_space_constraint(lax.empty((N_CHIPS, 2, x.shape[0] // 2, x.shape[1]), x.dtype), pltpu.HBM) for x in xs]
    xs = [pltpu.with_memory_space_constraint(x, pltpu.HBM) for x in xs]

    def body(*refs):
        x_refs, land_refs, tok_ref = refs[:n], refs[n:2 * n], refs[2 * n]
        ssem, rsem, tok_out = refs[2 * n + 1], refs[2 * n + 2], refs[-1]
        for cp in _ag_copies(x_refs, land_refs, ssem, rsem, False):
            cp.start()
        tok_out[...] = tok_ref[...]

    outs = pl.pallas_call(
        body, name=name,
        out_shape=(pltpu.SemaphoreType.DMA((3 * n,)), pltpu.SemaphoreType.DMA((3 * n,)),
                   *[pltpu.HBM(x.shape, x.dtype) for x in xs], *[pltpu.HBM(t.shape, t.dtype) for t in lands],
                   jax.ShapeDtypeStruct(token.shape, token.dtype)),
        in_specs=[_HBM] * (2 * n) + [_VM], out_specs=(_SEMS, _SEMS, *[_HBM] * (2 * n), _VM),
        input_output_aliases={i: 2 + i for i in range(2 * n)},
        compiler_params=pltpu.CompilerParams(has_side_effects=_DATAFLOW),
    )(*xs, *lands, token)
    return outs[0], outs[1], list(outs[2:2 + n]), list(outs[2 + n:2 + 2 * n]), outs[-1]


def _ag_wait(ssem, rsem, xs, lands, after, *, name):
    n = len(xs)

    def body(*refs):
        x_refs, land_refs, ssem_ref, rsem_ref = refs[:n], refs[n:2 * n], refs[2 * n], refs[2 * n + 1]
        for cp in _ag_copies(x_refs, land_refs, ssem_ref, rsem_ref, True):
            cp.wait_send()
            cp.wait_recv()

    outs = pl.pallas_call(
        body, name=name,
        out_shape=(*[pltpu.HBM(x.shape, x.dtype) for x in xs], *[pltpu.HBM(t.shape, t.dtype) for t in lands]),
        in_specs=[_HBM] * (2 * n) + [_SEMS, _SEMS, _ANY], out_specs=tuple([_HBM] * (2 * n)),
        input_output_aliases={i: i for i in range(2 * n)},
        compiler_params=pltpu.CompilerParams(has_side_effects=_DATAFLOW),
    )(*xs, *lands, ssem, rsem, after)
    return list(outs[:n]), list(outs[n:])


def _ag_finish(land, x, *, name):
    R, C = x.shape
    R2 = R // 2
    ch = _rows_tile(R2, max(8, STAGE_BYTES // (C * x.dtype.itemsize)))

    def body(l_ref, x_ref, o_ref, ssem, rsem, buf, isem, osem):
        mx, my, mc, chips = _place()
        me = 2 * mx + my
        sib = (mx, my, 1 - mc)
        passed = []
        for j, (px, py) in enumerate(chips):
            got = o_ref.at[2 * px + py, mc]
            passed.append(_rcopy(got, got, ssem.at[j], rsem.at[j], sib))
        for cp in passed:
            cp.start()
        _staged_copy([(x_ref.at[pl.ds(h * R2 + r, ch)], o_ref.at[me, h, pl.ds(r, ch)]) for h in range(2) for r in range(0, R2, ch)],
                     buf, isem, osem)
        for j, (px, py) in enumerate(chips):
            theirs = o_ref.at[2 * px + py, 1 - mc]
            _rcopy(theirs, theirs, ssem.at[j], rsem.at[j], sib).wait_recv()
        for cp in passed:
            cp.wait_send()

    return pl.pallas_call(
        body, in_specs=[_ANY, _ANY], out_specs=_ANY, out_shape=jax.ShapeDtypeStruct(land.shape, land.dtype),
        input_output_aliases={0: 0},
        scratch_shapes=[pltpu.SemaphoreType.DMA((3,)), pltpu.SemaphoreType.DMA((3,)), pltpu.VMEM((2, ch, C), x.dtype),
                        pltpu.SemaphoreType.DMA((2,)), pltpu.SemaphoreType.DMA((2,))],
        name=name)(land, x)


def _ag4(x, *, name):
    def body(x_ref, o_ref, ssem, rsem, lsem):
        mx, my, mc, chips = _place()
        me = 2 * mx + my
        loc = pltpu.make_async_copy(x_ref, o_ref.at[me], lsem)
        loc.start()
        sends = [_rcopy(x_ref, o_ref.at[me], ssem.at[j], rsem.at[j], (px, py, mc)) for j, (px, py) in enumerate(chips)]
        for cp in sends:
            cp.start()
        for j, (px, py) in enumerate(chips):
            _rcopy(x_ref, o_ref.at[2 * px + py], ssem.at[j], rsem.at[j], (px, py, mc)).wait_recv()
        for cp in sends:
            cp.wait_send()
        loc.wait()

    return pl.pallas_call(
        body, in_specs=[_ANY], out_specs=_ANY, out_shape=jax.ShapeDtypeStruct((N_CHIPS,) + x.shape, x.dtype),
        scratch_shapes=[pltpu.SemaphoreType.DMA((3,)), pltpu.SemaphoreType.DMA((3,)), pltpu.SemaphoreType.DMA(())],
        name=name)(x)


def _rs_sib(gs, *, name):
    n = len(gs)

    def body(*refs):
        g_refs, t_refs, (ssem, rsem) = refs[:n], refs[n:2 * n], refs[2 * n:]
        mx, my, mc, _ = _place()
        sib = (mx, my, 1 - mc)
        sends = []
        for w in range(n):
            R2 = g_refs[w].shape[1] // 2
            for k in range(N_CHIPS):
                sends.append(_rcopy(g_refs[w].at[k, pl.ds((1 - mc) * R2, R2)], t_refs[w].at[k], ssem.at[N_CHIPS * w + k],
                                    rsem.at[N_CHIPS * w + k], sib))
        for cp in sends:
            cp.start()
        for w in range(n):
            for k in range(N_CHIPS):
                t = t_refs[w].at[k]
                _rcopy(t, t, ssem.at[N_CHIPS * w + k], rsem.at[N_CHIPS * w + k], sib).wait_recv()
        for cp in sends:
            cp.wait_send()

    return pl.pallas_call(
        body, in_specs=[_ANY] * n, out_specs=[_ANY] * n,
        out_shape=[jax.ShapeDtypeStruct((N_CHIPS, g.shape[1] // 2, g.shape[2]), g.dtype) for g in gs],
        scratch_shapes=[pltpu.SemaphoreType.DMA((N_CHIPS * n,)), pltpu.SemaphoreType.DMA((N_CHIPS * n,))], name=name)(*gs)


def _a2a_copies(h_refs, got_refs, ssem, rsem, inbound):
    mx, my, mc, chips = _place()
    me = 2 * mx + my
    cps = []
    for w, (h_ref, got_ref) in enumerate(zip(h_refs, got_refs)):
        for j, (px, py) in enumerate(chips):
            dst = got_ref.at[2 * px + py] if inbound else got_ref.at[me]
            cps.append(_rcopy(h_ref.at[2 * px + py], dst, ssem.at[3 * w + j], rsem.at[3 * w + j], (px, py, mc)))
    return cps


def _a2a_start(hs, *, name):
    n = len(hs)
    gots = [pltpu.with_memory_space_constraint(lax.empty(h.shape, h.dtype), pltpu.HBM) for h in hs]
    hs = [pltpu.with_memory_space_constraint(h, pltpu.HBM) for h in hs]

    def body(*refs):
        h_refs, got_refs = refs[:n], refs[n:2 * n]
        ssem, rsem, tok_out = refs[2 * n], refs[2 * n + 1], refs[-1]
        for cp in _a2a_copies(h_refs, got_refs, ssem, rsem, False):
            cp.start()
        tok_out[...] = jnp.zeros_like(tok_out)

    outs = pl.pallas_call(
        body, name=name,
        out_shape=(pltpu.SemaphoreType.DMA((3 * n,)), pltpu.SemaphoreType.DMA((3 * n,)),
                   *[pltpu.HBM(h.shape, h.dtype) for h in hs], *[pltpu.HBM(h.shape, h.dtype) for h in hs],
                   jax.ShapeDtypeStruct((8, 128), F32)),
        in_specs=[_HBM] * (2 * n), out_specs=(_SEMS, _SEMS, *[_HBM] * (2 * n), _VM),
        input_output_aliases={i: 2 + i for i in range(2 * n)},
        compiler_params=pltpu.CompilerParams(has_side_effects=_DATAFLOW),
    )(*hs, *gots)
    return outs[0], outs[1], list(outs[2:2 + n]), list(outs[2 + n:2 + 2 * n]), outs[-1]


def _a2a_wait(ssem, rsem, hs, gots, after, *, name):
    n = len(hs)

    def body(*refs):
        h_refs, got_refs, ssem_ref, rsem_ref = refs[:n], refs[n:2 * n], refs[2 * n], refs[2 * n + 1]
        for cp in _a2a_copies(h_refs, got_refs, ssem_ref, rsem_ref, True):
            cp.wait_send()
            cp.wait_recv()

    outs = pl.pallas_call(
        body, name=name,
        out_shape=tuple(pltpu.HBM(h.shape, h.dtype) for h in hs + gots),
        in_specs=[_HBM] * (2 * n) + [_SEMS, _SEMS, _ANY], out_specs=tuple([_HBM] * (2 * n)),
        input_output_aliases={i: i for i in range(2 * n)},
        compiler_params=pltpu.CompilerParams(has_side_effects=_DATAFLOW),
    )(*hs, *gots, ssem, rsem, after)
    return list(outs[:n]), list(outs[n:])


def _sib_fill(bufs, *, name):
    n = len(bufs)

    def body(*refs):
        o_refs, (ssem, rsem) = refs[n:2 * n], refs[2 * n:]
        mx, my, mc, _ = _place()
        sib = (mx, my, 1 - mc)
        sends = [_rcopy(o_refs[w].at[mc], o_refs[w].at[mc], ssem.at[w], rsem.at[w], sib) for w in range(n)]
        for cp in sends:
            cp.start()
        for w in range(n):
            t = o_refs[w].at[1 - mc]
            _rcopy(t, t, ssem.at[w], rsem.at[w], sib).wait_recv()
        for cp in sends:
            cp.wait_send()

    return pl.pallas_call(
        body, in_specs=[_ANY] * n, out_specs=[_ANY] * n, out_shape=[jax.ShapeDtypeStruct(b.shape, b.dtype) for b in bufs],
        input_output_aliases={i: i for i in range(n)},
        scratch_shapes=[pltpu.SemaphoreType.DMA((n,)), pltpu.SemaphoreType.DMA((n,))], name=name)(*bufs)


def _rows_tile(n, pref=512):
    t = min(n, pref)
    while n % t or (t % 8 and t != n):
        t -= 1
    return t


def _rs_add(g, theirs, c_arr, payload, *, name):
    _, R, C = g.shape
    R2 = R // 2
    tr = _rows_tile(R2, 256)
    nb = R2 // tr

    def body(c_ref, g_ref, t_ref, o_ref):
        o_ref[...] = (g_ref[...] + t_ref[...]).astype(o_ref.dtype)

    blk = pl.BlockSpec((None, tr, C), lambda k, i, c: (k, i, 0))
    return pl.pallas_call(
        body,
        grid_spec=pltpu.PrefetchScalarGridSpec(
            num_scalar_prefetch=1, grid=(N_CHIPS, nb),
            in_specs=[pl.BlockSpec((None, tr, C), lambda k, i, c: (k, c[0] * nb + i, 0)), blk], out_specs=blk),
        out_shape=jax.ShapeDtypeStruct((N_CHIPS, R2, C), payload),
        compiler_params=_cparams(("arbitrary", "arbitrary")), name=name)(c_arr, g, theirs)


def _rs_sum(chip_sum, got, mc_arr, *, name):
    _, R2, C = chip_sum.shape
    tr = _rows_tile(R2, 256)

    def body(s_ref, own_ref, g0, g1, g2, g3, o_ref):
        me = s_ref[0]
        acc = jnp.zeros((tr, C), F32)
        for k, g_ref in enumerate((g0, g1, g2, g3)):
            acc = acc + jnp.where(me == k, own_ref[...], g_ref[...]).astype(F32)
        o_ref[...] = acc

    def got_spec(k):
        return pl.BlockSpec((None, tr, C), lambda i, s: (jnp.where(s[0] == k, (k + 1) % N_CHIPS, k), i, 0))

    return pl.pallas_call(
        body,
        grid_spec=pltpu.PrefetchScalarGridSpec(
            num_scalar_prefetch=1, grid=(R2 // tr,),
            in_specs=[pl.BlockSpec((None, tr, C), lambda i, s: (s[0], i, 0))] + [got_spec(k) for k in range(N_CHIPS)],
            out_specs=pl.BlockSpec((None, tr, C), lambda i, s: (s[1], i, 0))),
        out_shape=jax.ShapeDtypeStruct((2, R2, C), F32),
        compiler_params=_cparams(("arbitrary",)), name=name)(mc_arr, chip_sum, got, got, got, got)


def _adamw(w, m, v, gs, *, name):
    nl, _, R2, C = w.shape
    tr = _rows_tile(R2, 256)
    c1 = 1.0 / (1.0 - ADAM_B1 ** ADAM_STEP)
    c2 = 1.0 / (1.0 - ADAM_B2 ** ADAM_STEP)

    def body(w_ref, m_ref, v_ref, *refs):
        g_refs, (go_ref, d_ref, mo_ref, vo_ref) = refs[:nl], refs[nl:]
        l = pl.program_id(0)
        for ll in range(nl):
            @pl.when(l == ll)
            def _(ll=ll):
                gv = g_refs[ll][...]
                mn = ADAM_B1 * m_ref[...] + (1.0 - ADAM_B1) * gv
                vn = ADAM_B2 * v_ref[...] + (1.0 - ADAM_B2) * (gv * gv)
                go_ref[...] = gv
                mo_ref[...] = mn
                vo_ref[...] = vn
                d_ref[...] = -ADAM_LR * ((mn * c1) / (jnp.sqrt(vn * c2) + ADAM_EPS) + ADAM_WD * w_ref[...])

    def g_spec(ll):
        return pl.BlockSpec((None, tr, C), lambda l, h, i: (jnp.where(l == ll, h, 0), jnp.where(l == ll, i, 0), 0))

    ws = pl.BlockSpec((None, None, tr, C), lambda l, h, i: (l, h, i, 0))
    shp = jax.ShapeDtypeStruct(w.shape, F32)
    return pl.pallas_call(
        body, grid=(nl, 2, R2 // tr), in_specs=[ws, ws, ws] + [g_spec(ll) for ll in range(nl)], out_specs=[ws] * 4,
        out_shape=[shp] * 4, compiler_params=_cparams(("arbitrary", "arbitrary", "arbitrary")), name=name)(w, m, v, *gs)


class _GradReducer:
    def __init__(self, tag, payload):
        self.tag, self.payload = tag, payload
        self.me = (2 * lax.axis_index("x") + lax.axis_index("y")).astype(jnp.int32)
        self.c = lax.axis_index("c").astype(jnp.int32)

    def start(self, names, gs):
        theirs = _rs_sib(gs, name=f"rs_sib_{self.tag}")
        hs = [_rs_add(g, t, self.c.reshape(1), self.payload, name=f"rs_add_{n}") for n, g, t in zip(names, gs, theirs)]
        self.names = names
        self.ssem, self.rsem, self.hs, self.gots, token = _a2a_start(hs, name=f"rs_a2a_start_{self.tag}")
        return token

    def finish(self, after):
        hs, gots = _a2a_wait(self.ssem, self.rsem, self.hs, self.gots, after, name=f"rs_a2a_wait_{self.tag}")
        mc = jnp.stack([self.me, self.c])
        return {n: _rs_sum(h, g, mc, name=f"rs_sum_{n}") for n, h, g in zip(self.names, hs, gots)}


WEIGHTS = ["rel_bias", "mem_norm_g", "norm_mix_g", "w_in", "s5_lam_re", "s5_lam_im", "s5_log_dt", "s5_b_re", "s5_b_im",
           "s5_c_re", "s5_c_im", "s5_d", "s5_w_glu", "pool_w", "pool_scale", "conv_w_dw", "conv_b_dw", "conv_ln_g",
           "conv_ln_b", "conv_w_pw", "grp_norm_g", "w_out", "norm_x_g", "w_xq", "w_xk", "w_xv", "w_xo", "norm_mlp_g",
           "w_up", "w_down", "norm_final_g"]
SHARDED = {"w_in": "col", "s5_w_glu": "row", "conv_w_dw": "col", "conv_w_pw": "row", "w_out": "row", "w_xq": "row",
           "w_xk": "row", "w_xv": "row", "w_xo": "col", "w_up": "col", "w_down": "row"}
SMALL = [n for n in WEIGHTS if n not in SHARDED]
SMALL_ALIGN = N_CHIPS * 2 * 256 * 128


def _mat(w, kind):
    return w.reshape(w.shape[0] * w.shape[1], w.shape[2]) if kind == "row" else w


def _att_bias(rel_bias):
    bucket, valid = _att_tables()
    onehot = (jnp.asarray(bucket.reshape(-1))[:, None] == jnp.arange(REL_BUCKETS)[None, :]).astype(F32)
    b = jnp.dot(onehot, rel_bias, precision=lax.Precision.HIGHEST).reshape(bucket.shape + (rel_bias.shape[1],))
    return jnp.where(jnp.asarray(valid)[:, None], jnp.transpose(b, (0, 3, 1, 2)), NEG_INF)


def _rel_bias_grad(dbias):
    bucket, _ = _att_tables()
    nb, H = dbias.shape[0], dbias.shape[1]
    onehot = (jnp.asarray(bucket.reshape(-1))[:, None] == jnp.arange(REL_BUCKETS)[None, :]).astype(BF16)
    flat = jnp.transpose(dbias.reshape(nb, H, -1), (1, 0, 2)).reshape(H, -1)
    return _mm(flat, onehot, "nn", name="rel_bias_grad").T


def _layer_fwd(h, mem_n, bias, sp, bw, l):
    L, D = h.shape
    GW = D // N_MIXERS
    G = GW // S5_CH
    E = GW // ATT_HEADS
    sv = {"h": h}
    xn = _rms_fwd(h, sp["norm_mix_g"][l][None], name="norm_mix")
    proj = _mm(xn, bw["w_in"], "nn", name="proj_in")
    sv.update(xn=xn, proj=proj)
    disc, disc_vjp = jax.vjp(_s5_disc, sp["s5_lam_re"][l], sp["s5_lam_im"][l], sp["s5_log_dt"][l], sp["s5_b_re"][l], sp["s5_b_im"][l])
    ab_r, ab_i, bb_r, bb_i = disc
    s5 = dict(
        bdr=_bd(jnp.transpose(bb_r, (0, 2, 1))).astype(BF16), bdi=_bd(jnp.transpose(bb_i, (0, 2, 1))).astype(BF16),
        cdr=_bd(jnp.transpose(sp["s5_c_re"][l], (0, 2, 1))).astype(BF16), cdi=_bd(jnp.transpose(sp["s5_c_im"][l], (0, 2, 1))).astype(BF16),
        d=sp["s5_d"][l][None], wglu=bw["s5_w_glu"], ab=(ab_r.reshape(-1), ab_i.reshape(-1)), vjp=disc_vjp)
    pw, tr, ti = _cpow_tables(*s5["ab"])
    y_a, xr, xi = _s5_fwd(proj, s5["bdr"], s5["bdi"], pw, tr, ti, s5["cdr"], s5["cdi"], s5["d"], s5["wglu"], name="s5_fwd")
    sv.update(s5=s5, xr=xr, xi=xi, y_a=y_a)
    pool_s = sp["pool_scale"][l].reshape(len(POOL_WINDOWS), 1, -1)
    y_b = _pool_fwd(proj, sp["pool_w"][l], pool_s, name="pool_fwd")
    sv.update(y_b=y_b, pool_s=pool_s)
    hc = _conv1_fwd(proj, bw["conv_w_dw"], sp["conv_b_dw"][l][None], name="conv_dw_fwd")
    y_c = _conv2_fwd(hc, sp["conv_ln_g"][l][None], sp["conv_ln_b"][l][None], bw["conv_w_pw"], name="conv_pw_fwd")
    sv.update(hc=hc, y_c=y_c)
    scale = E ** -0.5
    q = (proj[:, 4 * GW:5 * GW] * scale).astype(BF16)
    k = proj[:, 5 * GW:6 * GW].astype(BF16)
    v = proj[:, 6 * GW:7 * GW].astype(BF16).reshape(L, ATT_HEADS, E)
    vx = jnp.concatenate([v, jnp.ones_like(v)], axis=-1).reshape(L, ATT_HEADS * 2 * E)
    qb, kb, vxb = _to_branches(q, ATT_HEADS), _to_branches(k, ATT_HEADS), _to_branches(vx, ATT_HEADS)
    ox = _att_fwd(qb, kb, vxb, bias, name="att_fwd")
    oxu = _from_branches(ox)
    mix = _mix_fwd(oxu, name="att_mix_fwd")
    y_d = mix.reshape(L, ATT_HEADS, 2 * E)[:, :, :E].reshape(L, GW)
    sv.update(qb=qb, kb=kb, vxb=vxb, ox=ox, oxu=oxu, y_d=y_d)
    yn = _grp_fwd([y_a, y_b, y_c, y_d], sp["grp_norm_g"][l][None], name="grp_fwd")
    h1 = _mm(yn, bw["w_out"], "nn", res=h, name="proj_out")
    sv.update(yn=yn, h1=h1)
    hx = _rms_fwd(h1, sp["norm_x_g"][l][None], name="norm_x")
    q_x = _mm(hx, bw["w_xq"], "nn", out_dtype=BF16, name="xq")
    k_x = _mm(mem_n, bw["w_xk"], "nn", out_dtype=BF16, name="xk")
    v_x = _mm(mem_n, bw["w_xv"], "nn", out_dtype=BF16, name="xv")
    o_x = _xatt_fwd(q_x, k_x, v_x, name="xatt_fwd")
    h2 = _mm(o_x, bw["w_xo"], "nn", res=h1, name="xo")
    sv.update(hx=hx, q_x=q_x, k_x=k_x, v_x=v_x, o_x=o_x, h2=h2)
    hm = _rms_fwd(h2, sp["norm_mlp_g"][l][None], name="norm_mlp")
    up, act = _mm(hm, bw["w_up"], "nn", epi="relu2", out_dtype=BF16, name="mlp_up")
    h3 = _mm(act, bw["w_down"], "nn", res=h2, name="mlp_down")
    sv.update(hm=hm, up=up, act=act)
    return h3, sv


def _stack_rows(g):
    return g.reshape(N_CHIPS, g.shape[0] // N_CHIPS, g.shape[1])


def _layer_bwd(dh3, d_memn, mem_n, bias, sp, bw, sv, l):
    L, D = dh3.shape
    GW = D // N_MIXERS
    G = GW // S5_CH
    E = GW // ATT_HEADS
    gs, gb = {}, {}
    d_up = _mm(dh3, bw["w_down"], "nt", epi="relu2_bwd", aux=sv["up"], out_dtype=BF16, name="mlp_down_dx")
    gb["w_down"] = _stack_rows(_mm(sv["act"], dh3, "tn", name="mlp_down_dw"))
    gb["w_up"] = _mm(sv["hm"], d_up, "tn", out_stack=N_CHIPS, name="mlp_up_dw")
    d_hm = _mm(d_up, bw["w_up"], "nt", name="mlp_up_dx")
    dh2, gs["norm_mlp_g"] = _rms_bwd(sv["h2"], sp["norm_mlp_g"][l][None], d_hm, dh3, name="norm_mlp_bwd")
    d_ox = _mm(dh2, bw["w_xo"], "nt", name="xo_dx")
    gb["w_xo"] = _mm(sv["o_x"], dh2, "tn", out_stack=N_CHIPS, name="xo_dw")
    dq_x, dk_x, dv_x = _xatt_bwd(sv["q_x"], sv["k_x"], sv["v_x"], d_ox, name="xatt_bwd")
    gb["w_xq"] = _stack_rows(_mm(sv["hx"], dq_x, "tn", name="xq_dw"))
    gb["w_xk"] = _stack_rows(_mm(mem_n, dk_x, "tn", name="xk_dw"))
    gb["w_xv"] = _stack_rows(_mm(mem_n, dv_x, "tn", name="xv_dw"))
    d_memn = _mm(dk_x, bw["w_xk"], "nt", res=d_memn, name="xk_dx")
    d_memn = _mm(dv_x, bw["w_xv"], "nt", res=d_memn, name="xv_dx")
    d_hx = _mm(dq_x, bw["w_xq"], "nt", name="xq_dx")
    dh1, gs["norm_x_g"] = _rms_bwd(sv["h1"], sp["norm_x_g"][l][None], d_hx, dh2, name="norm_x_bwd")
    d_yn = _mm(dh1, bw["w_out"], "nt", name="proj_out_dx")
    gb["w_out"] = _stack_rows(_mm(sv["yn"], dh1, "tn", name="proj_out_dw"))
    ys = [sv["y_a"], sv["y_b"], sv["y_c"], sv["y_d"]]
    dya, dyb, dyc, dyd, gs["grp_norm_g"] = _grp_bwd(ys, sp["grp_norm_g"][l][None], d_yn, name="grp_bwd")
    dy_ext = jnp.concatenate([dyd.reshape(L, ATT_HEADS, E), jnp.zeros((L, ATT_HEADS, E), F32)], axis=-1).reshape(L, ATT_HEADS * 2 * E)
    doxu = _mix_bwd(sv["oxu"], dy_ext, name="att_mix_bwd")
    dox = jnp.stack([_to_branches(doxu[b], ATT_HEADS)[b] for b in range(len(DILATED_PATTERNS))])
    dq_b, dk_b, dvx_b, dbias = _att_bwd(sv["qb"], sv["kb"], sv["vxb"], bias, sv["ox"], dox, name="att_bwd")
    dq = jnp.sum(_from_branches(dq_b), axis=0) * (E ** -0.5)
    dk = jnp.sum(_from_branches(dk_b), axis=0)
    dv = jnp.sum(_from_branches(dvx_b[..., :E]), axis=0)
    dhc, gs["conv_ln_g"], gs["conv_ln_b"], g_pw = _conv2_bwd(sv["hc"], sp["conv_ln_g"][l][None], sp["conv_ln_b"][l][None],
                                                               bw["conv_w_pw"], dyc, name="conv_pw_bwd")
    gb["conv_w_pw"] = _stack_rows(g_pw)
    dval, dgate, g_dw, gs["conv_b_dw"] = _conv1_bwd(sv["proj"], dhc, bw["conv_w_dw"], name="conv_dw_bwd")
    gb["conv_w_dw"] = jnp.transpose(g_dw.reshape(CONV_PAD, N_CHIPS, GW // N_CHIPS), (1, 0, 2))
    du_b, gs["pool_w"], g_ps = _pool_bwd(sv["proj"], dyb, sp["pool_w"][l], sv["pool_s"], name="pool_bwd")
    gs["pool_scale"] = g_ps.reshape(-1)
    s5 = sv["s5"]
    conj = (s5["ab"][0], -s5["ab"][1])
    pw, tr, ti = _cpow_tables(*conj)
    du_a, g_glu, g_d, dcdr, dcdi, dbdr, dbdi, dar, dai = _s5_bwd(
        sv["proj"], sv["xr"], sv["xi"], dya, s5["bdr"], s5["bdi"], pw, tr[::-1], ti[::-1], s5["cdr"], s5["cdi"], s5["d"], s5["wglu"],
        name="s5_bwd")
    gb["s5_w_glu"] = _stack_rows(g_glu)
    gs["s5_d"] = g_d.reshape(-1)
    gs["s5_c_re"] = jnp.transpose(_bd_extract(dcdr, G), (0, 2, 1))
    gs["s5_c_im"] = jnp.transpose(_bd_extract(dcdi, G), (0, 2, 1))
    d_bb_r = jnp.transpose(_bd_extract(dbdr, G), (0, 2, 1))
    d_bb_i = jnp.transpose(_bd_extract(dbdi, G), (0, 2, 1))
    d_ab_r = jnp.sum(dar, axis=0).reshape(G, S5_STATE)
    d_ab_i = jnp.sum(dai, axis=0).reshape(G, S5_STATE)
    gs["s5_lam_re"], gs["s5_lam_im"], gs["s5_log_dt"], gs["s5_b_re"], gs["s5_b_im"] = s5["vjp"]((d_ab_r, d_ab_i, d_bb_r, d_bb_i))
    d_proj = jnp.concatenate([du_a, du_b, dval, dgate, dq, dk, dv], axis=1)
    gb["w_in"] = _mm(sv["xn"], d_proj, "tn", out_stack=N_CHIPS, name="proj_in_dw")
    d_xn = _mm(d_proj, bw["w_in"], "nt", name="proj_in_dx")
    dh0, gs["norm_mix_g"] = _rms_bwd(sv["h"], sp["norm_mix_g"][l][None], d_xn, dh1, name="norm_mix_bwd")
    gs = {n: g.reshape(sp[n].shape[1:]) for n, g in gs.items()}
    return dh0, d_memn, dbias, gs, gb


def _device_step(x, mem, target, sp, get_big, on_grads):
    nl = sp["norm_mix_g"].shape[0]
    mem_n = _rms_fwd(mem, sp["mem_norm_g"][None], name="norm_mem")
    bias = _att_bias(sp["rel_bias"])
    h, saved, big = x, [], {n: [] for n in SHARDED}
    for l in range(nl):
        bw = get_big(l, h)
        for n in SHARDED:
            big[n].append(bw[n])
        h, sv = _layer_fwd(h, mem_n, bias, sp, bw, l)
        saved.append(sv)
    loss, dh, g_final = _loss_head(h, sp["norm_final_g"][None], target, name="loss_head")
    d_memn = jnp.zeros(mem.shape, F32)
    dbias = jnp.zeros(bias.shape, F32)
    sg = {n: [None] * nl for n in SMALL}
    for l in reversed(range(nl)):
        dh, d_memn, dbias_l, gs, gb = _layer_bwd(dh, d_memn, mem_n, bias, sp, {n: big[n][l] for n in SHARDED}, saved[l], l)
        dbias = dbias + dbias_l
        for n, g in gs.items():
            sg[n][l] = g
        dh = on_grads(l, gb, dh)
    small = {n: jnp.stack(g) for n, g in sg.items() if g[0] is not None}
    small["norm_final_g"] = g_final.reshape(-1)
    _, g_mem = _rms_bwd(mem, sp["mem_norm_g"][None], d_memn, None, name="norm_mem_bwd")
    small["mem_norm_g"] = g_mem.reshape(-1)
    small["rel_bias"] = _rel_bias_grad(dbias)
    return loss, dh, small


def _gather_big(shards):
    nl = shards["w_in"].shape[0]
    token = jnp.zeros((8, 128), F32)
    pending = []
    for l in range(nl):
        xs = [jnp.pad(shards[n][l], ((0, CONV_PAD - CONV_WIDTH), (0, 0))) if n == "conv_w_dw" else shards[n][l].astype(BF16)
              for n in SHARDED]
        ssem, rsem, xs, lands, token = _ag_start(xs, token, name=f"ag_start_l{l}")
        pending.append((ssem, rsem, xs, lands))

    def get_big(l, after):
        ssem, rsem, xs, lands = pending[l]
        xs, lands = _ag_wait(ssem, rsem, xs, lands, token if l == 0 else after, name=f"ag_wait_l{l}")
        bw = {}
        for (n, kind), x, land in zip(SHARDED.items(), xs, lands):
            full = _ag_finish(land, x, name=f"ag_finish_{n}").reshape(N_CHIPS, x.shape[0], x.shape[1])
            bw[n] = jnp.transpose(full, (1, 0, 2)).reshape(CONV_PAD, -1) if n == "conv_w_dw" else _mat(full, kind)
        return bw

    return get_big


def _pack_small(vals, extra=None):
    flat = [vals[n].reshape(-1).astype(F32) for n in SMALL]
    flat.append(jnp.zeros((1,), F32) if extra is None else extra.reshape(1))
    v = jnp.concatenate(flat)
    n_pad = -(-v.shape[0] // SMALL_ALIGN) * SMALL_ALIGN
    return jnp.pad(v, (0, n_pad - v.shape[0]))


def _unpack_small(flat, like):
    out, pos = {}, 0
    for n in SMALL:
        size = math.prod(like[n].shape)
        out[n] = flat[pos:pos + size].reshape(like[n].shape)
        pos += size
    return out, flat[pos]


def kernel(x, mem, rel_bias, mem_norm_g, norm_mix_g, w_in, s5_lam_re, s5_lam_im, s5_log_dt, s5_b_re, s5_b_im, s5_c_re, s5_c_im, s5_d, s5_w_glu, pool_w, pool_scale, conv_w_dw, conv_b_dw, conv_ln_g, conv_ln_b, conv_w_pw, grp_norm_g, w_out, norm_x_g, w_xq, w_xk, w_xv, w_xo, norm_mlp_g, w_up, w_down, norm_final_g, loss_target, m_rel_bias, m_mem_norm_g, m_norm_mix_g, m_w_in, m_s5_lam_re, m_s5_lam_im, m_s5_log_dt, m_s5_b_re, m_s5_b_im, m_s5_c_re, m_s5_c_im, m_s5_d, m_s5_w_glu, m_pool_w, m_pool_scale, m_conv_w_dw, m_conv_b_dw, m_conv_ln_g, m_conv_ln_b, m_conv_w_pw, m_grp_norm_g, m_w_out, m_norm_x_g, m_w_xq, m_w_xk, m_w_xv, m_w_xo, m_norm_mlp_g, m_w_up, m_w_down, m_norm_final_g, v_rel_bias, v_mem_norm_g, v_norm_mix_g, v_w_in, v_s5_lam_re, v_s5_lam_im, v_s5_log_dt, v_s5_b_re, v_s5_b_im, v_s5_c_re, v_s5_c_im, v_s5_d, v_s5_w_glu, v_pool_w, v_pool_scale, v_conv_w_dw, v_conv_b_dw, v_conv_ln_g, v_conv_ln_b, v_conv_w_pw, v_grp_norm_g, v_w_out, v_norm_x_g, v_w_xq, v_w_xk, v_w_xv, v_w_xo, v_norm_mlp_g, v_w_up, v_w_down, v_norm_final_g):
    env = dict(locals())
    w = {n: env[n] for n in WEIGHTS}
    m = {n: env["m_" + n] for n in WEIGHTS}
    v = {n: env["v_" + n] for n in WEIGHTS}
    sp = {n: w[n] for n in SMALL}
    get_big = _gather_big({n: w[n] for n in SHARDED})
    nl = w["w_in"].shape[0]
    names = list(SHARDED)
    reducers, reduced = {}, {}

    def on_grads(l, gb, dh):
        reducers[l] = _GradReducer(f"l{l}", BF16)
        token = reducers[l].start(names, [gb[n] for n in names])
        if l + 1 in reducers:
            reduced[l + 1] = reducers[l + 1].finish(dh)
        if l == 0:
            reduced[0] = reducers[0].finish(token)
        return dh + token[0, 0]

    loss, grad_x, g_small = _device_step(x[0], mem[0], loss_target[0], sp, get_big, on_grads)

    grad, delta, new_m, new_v = {}, {}, {}, {}
    filled = _sib_fill([reduced[l][n] for n in names for l in range(nl)], name="rs_fill")
    for i, n in enumerate(names):
        pad = ((0, 0), (0, CONV_PAD - CONV_WIDTH), (0, 0)) if n == "conv_w_dw" else None
        wmv = [jnp.pad(t[n], pad) if pad else t[n] for t in (w, m, v)]
        _, R, C = wmv[0].shape
        outs = _adamw(*[t.reshape(nl, 2, R // 2, C) for t in wmv], filled[i * nl:(i + 1) * nl], name=f"adamw_{n}")
        outs = [o.reshape(nl, R, C)[:, :w[n].shape[1]] for o in outs]
        grad[n], delta[n], new_m[n], new_v[n] = outs
    packed = _pack_small(g_small, loss).reshape(N_CHIPS, -1, 128)
    small = _GradReducer("small", F32)
    token = small.start(["small"], [packed])
    quarter = _sib_fill([small.finish(token)["small"]], name="rs_fill_small")[0]
    total = _ag4(quarter.reshape(-1, 128), name="ag_small").reshape(2, -1, 128)
    wmv = [_pack_small(t).reshape(1, 2, -1, 128) for t in (w, m, v)]
    outs = _adamw(*wmv, [total], name="adamw_small")
    loss_sum = None
    for o, dst in zip(outs, (grad, delta, new_m, new_v)):
        vals, last = _unpack_small(o.reshape(-1), sp)
        dst.update(vals)
        loss_sum = last if loss_sum is None else loss_sum
    return (loss_sum, grad_x[None], *[grad[n] for n in WEIGHTS], *[delta[n] for n in WEIGHTS],
            *[new_m[n] for n in WEIGHTS], *[new_v[n] for n in WEIGHTS])
```

```python
import functools
import math

import jax
import jax.numpy as jnp
import numpy as np
from jax import lax
from jax.experimental import pallas as pl
from jax.experimental.pallas import tpu as pltpu

F32 = jnp.float32
BF16 = jnp.bfloat16
MESH = pl.DeviceIdType.MESH

N_MIXERS = 4
S5_CH = 16
S5_STATE = 64
POOL_WINDOWS = (2, 4, 8, 16)
CONV_WIDTH = 31
CONV_PAD = 32
ATT_HEADS = 8
ATT_BLOCK = 128
DILATED_PATTERNS = ((128, 1), (512, 4), (2048, 16))
REL_BUCKETS = 32
REL_MAX_DIST = 2048
X_HEADS = 4
X_HEAD_DIM = 128
NORM_EPS = 1e-6
NEG_INF = -1e30
ADAM_LR, ADAM_B1, ADAM_B2, ADAM_EPS, ADAM_WD, ADAM_STEP = 0.001, 0.9, 0.999, 1e-08, 0.01, 10
N_CHIPS = 4
VMEM_LIMIT = 56 * 1024 * 1024


def _cparams(sem=None, vmem=None):
    return pltpu.CompilerParams(dimension_semantics=sem, vmem_limit_bytes=vmem)


def _tile(n, pref):
    if n <= pref:
        return n
    t = pref
    while t >= 128:
        if n % t == 0:
            return t
        t -= 128
    return n


def _spec(arr, tr, tc, rc):
    if arr.ndim == 2:
        return pl.BlockSpec((tr, tc), lambda *g: rc(*g))
    per = arr.shape[2] // tc
    assert arr.shape[2] % tc == 0

    def imap(*g):
        r, c = rc(*g)
        return (c // per, r, c % per)

    return pl.BlockSpec((None, tr, tc), imap)


def _lshape(arr):
    return arr.shape if arr.ndim == 2 else (arr.shape[1], arr.shape[0] * arr.shape[2])


def _mm(a, b, mode, *, name, out_dtype=F32, res=None, epi=None, aux=None, out_stack=None, tm=1024, tn=1024, tk=2048):
    la, lb = _lshape(a), _lshape(b)
    if mode == "nn":
        (M, K), (K2, N) = la, lb
    elif mode == "nt":
        (M, K), (N, K2) = la, lb
    else:
        (K, M), (K2, N) = la, lb
    assert K == K2, (la, lb, mode)
    lim = {"m": M, "n": N // out_stack if out_stack else N, "k": K}
    stacked = [(a, "m" if mode == "tn" else "k"), (b, "k" if mode == "nt" else "n"), (res, "n"), (aux, "n")]
    for arr, dim in stacked:
        if arr is not None and arr.ndim == 3:
            lim[dim] = math.gcd(lim[dim], arr.shape[2])
    tm, tn, tk = _tile(lim["m"], tm), _tile(lim["n"], tn), _tile(lim["k"], tk)
    nk = K // tk
    grid = (M // tm, N // tn, nk)
    a_spec = _spec(a, tk, tm, lambda i, j, k: (k, i)) if mode == "tn" else _spec(a, tm, tk, lambda i, j, k: (i, k))
    b_spec = _spec(b, tn, tk, lambda i, j, k: (j, k)) if mode == "nt" else _spec(b, tk, tn, lambda i, j, k: (k, j))
    dims = {"nn": ((1,), (0,)), "nt": ((1,), (1,)), "tn": ((0,), (0,))}[mode]
    ins, in_specs = [a, b], [a_spec, b_spec]
    for extra in (res, aux):
        if extra is not None:
            ins.append(extra)
            in_specs.append(_spec(extra, tm, tn, lambda i, j, k: (i, j)))
    if out_stack:
        o_shape = jax.ShapeDtypeStruct((out_stack, M, N // out_stack), out_dtype)
    else:
        o_shape = jax.ShapeDtypeStruct((M, N), out_dtype)
    o_spec = _spec(o_shape, tm, tn, lambda i, j, k: (i, j))
    n_out = 2 if epi == "relu2" else 1
    has_res, has_aux = res is not None, aux is not None

    def body(*refs):
        a_ref, b_ref = refs[0], refs[1]
        pos = 2
        res_ref = aux_ref = None
        if has_res:
            res_ref = refs[pos]
            pos += 1
        if has_aux:
            aux_ref = refs[pos]
            pos += 1
        outs = refs[pos:pos + n_out]
        part = lax.dot_general(a_ref[...].astype(BF16), b_ref[...].astype(BF16), (dims, ((), ())), preferred_element_type=F32)
        if nk > 1:
            acc_ref = refs[pos + n_out]
            k = pl.program_id(2)

            @pl.when(k == 0)
            def _():
                acc_ref[...] = part

            @pl.when(k > 0)
            def _():
                acc_ref[...] += part

        @pl.when(pl.program_id(2) == nk - 1)
        def _():
            o = acc_ref[...] if nk > 1 else part
            if has_res:
                o = o + res_ref[...].astype(F32)
            if epi == "relu2":
                outs[0][...] = o.astype(outs[0].dtype)
                r = jnp.maximum(o, 0.0)
                outs[1][...] = (r * r).astype(outs[1].dtype)
            elif epi == "relu2_bwd":
                outs[0][...] = (o * (2.0 * jnp.maximum(aux_ref[...].astype(F32), 0.0))).astype(outs[0].dtype)
            else:
                outs[0][...] = o.astype(outs[0].dtype)

    out = pl.pallas_call(
        body, grid=grid, in_specs=in_specs,
        out_specs=[o_spec] * n_out if n_out > 1 else o_spec,
        out_shape=[o_shape] * n_out if n_out > 1 else o_shape,
        scratch_shapes=[pltpu.VMEM((tm, tn), F32)] if nk > 1 else [],
        compiler_params=_cparams(("parallel", "parallel", "arbitrary"), VMEM_LIMIT), name=name,
    )(*ins)
    return out


def _rms_fwd(x, g, *, name, out_dtype=BF16):
    L, D = x.shape
    tr = _tile(L, 256)

    def body(x_ref, g_ref, o_ref):
        xv = x_ref[...]
        r = lax.rsqrt(jnp.mean(xv * xv, axis=-1, keepdims=True) + NORM_EPS)
        o_ref[...] = (xv * r * g_ref[...]).astype(o_ref.dtype)

    return pl.pallas_call(
        body, grid=(L // tr,),
        in_specs=[pl.BlockSpec((tr, D), lambda i: (i, 0)), pl.BlockSpec((1, D), lambda i: (0, 0))],
        out_specs=pl.BlockSpec((tr, D), lambda i: (i, 0)),
        out_shape=jax.ShapeDtypeStruct((L, D), out_dtype),
        compiler_params=_cparams(("parallel",)), name=name)(x, g)


def _rms_bwd(x, g, dy, dres, *, name):
    L, D = x.shape
    tr = _tile(L, 256)
    has_res = dres is not None

    def body(*refs):
        x_ref, g_ref, dy_ref = refs[:3]
        dres_ref = refs[3] if has_res else None
        dx_ref, dg_ref = refs[-2:]
        xv = x_ref[...]
        dyv = dy_ref[...].astype(F32)
        r = lax.rsqrt(jnp.mean(xv * xv, axis=-1, keepdims=True) + NORM_EPS)
        xh = xv * r
        dyg = dyv * g_ref[...]
        dx = r * (dyg - xh * jnp.mean(dyg * xh, axis=-1, keepdims=True))
        if has_res:
            dx = dx + dres_ref[...]
        dx_ref[...] = dx

        @pl.when(pl.program_id(0) == 0)
        def _():
            dg_ref[...] = jnp.zeros_like(dg_ref)

        dg_ref[...] += jnp.sum(dyv * xh, axis=0, keepdims=True)

    row = pl.BlockSpec((tr, D), lambda i: (i, 0))
    vec = pl.BlockSpec((1, D), lambda i: (0, 0))
    ins = [x, g, dy] + ([dres] if has_res else [])
    return pl.pallas_call(
        body, grid=(L // tr,), in_specs=[row, vec, row] + ([row] if has_res else []),
        out_specs=[row, vec],
        out_shape=[jax.ShapeDtypeStruct((L, D), F32), jax.ShapeDtypeStruct((1, D), F32)],
        compiler_params=_cparams(("arbitrary",)), name=name)(*ins)


def _loss_head(h, g, target, *, name):
    L, D = h.shape
    tr = _tile(L, 256)

    def body(x_ref, g_ref, t_ref, loss_ref, dx_ref, dg_ref):
        xv = x_ref[...]
        r = lax.rsqrt(jnp.mean(xv * xv, axis=-1, keepdims=True) + NORM_EPS)
        xh = xv * r
        err = xh * g_ref[...] - t_ref[...]
        dyv = err * (1.0 / D)
        dyg = dyv * g_ref[...]
        dx_ref[...] = r * (dyg - xh * jnp.mean(dyg * xh, axis=-1, keepdims=True))

        @pl.when(pl.program_id(0) == 0)
        def _():
            dg_ref[...] = jnp.zeros_like(dg_ref)
            loss_ref[...] = jnp.zeros_like(loss_ref)

        dg_ref[...] += jnp.sum(dyv * xh, axis=0, keepdims=True)
        loss_ref[...] += 0.5 * jnp.sum(jnp.sum(err * err, axis=1, keepdims=True), axis=0, keepdims=True) * (1.0 / D)

    row = pl.BlockSpec((tr, D), lambda i: (i, 0))
    vec = pl.BlockSpec((1, D), lambda i: (0, 0))
    return pl.pallas_call(
        body, grid=(L // tr,), in_specs=[row, vec, row],
        out_specs=[pl.BlockSpec((1, 1), lambda i: (0, 0)), row, vec],
        out_shape=[jax.ShapeDtypeStruct((1, 1), F32), jax.ShapeDtypeStruct((L, D), F32), jax.ShapeDtypeStruct((1, D), F32)],
        compiler_params=_cparams(("arbitrary",)), name=name)(h, g, target)


S5_TL = 256
S5_LW = 512


def _dot(a, b, dims):
    return lax.dot_general(a, b, (dims, ((), ())), preferred_element_type=F32)


_NN, _NT, _TN = ((1,), (0,)), ((1,), (1,)), ((0,), (0,))


def _gelu_and_grad(y):
    k = math.sqrt(2.0 / math.pi)
    y2 = y * y
    th = jnp.tanh(k * (y + 0.044715 * y * y2))
    g = 0.5 * y * (1.0 + th)
    gp = 0.5 * (1.0 + th) + 0.5 * y * (1.0 - th * th) * k * (1.0 + 3.0 * 0.044715 * y2)
    return g, gp


def _scan_tiles(re_s, im_s, ls, lw, pw_ref, tr_ref, ti_ref, carry_s, nt, reverse, extra=None):
    row = lax.broadcasted_iota(jnp.int32, (8, lw), 0)
    a = [pw_ref[k:k + 1, ls] for k in range(6)]
    tr_t, ti_t = tr_ref[:, ls], ti_ref[:, ls]
    edge = 0 if reverse else 7

    def tile(jj, carry):
        cr, ci, acc = carry
        j = (nt - 1 - jj) if reverse else jj
        off = pl.multiple_of(j * 8, 8)
        sr, si = re_s[pl.ds(off, 8), ls], im_s[pl.ds(off, 8), ls]
        for n, k in enumerate((1, 2, 4)):
            akr, aki = a[2 * n], a[2 * n + 1]
            if reverse:
                keep, sh = row < 8 - k, 8 - k
            else:
                keep, sh = row >= k, k
            shr = jnp.where(keep, pltpu.roll(sr, sh, 0), 0.0)
            shi = jnp.where(keep, pltpu.roll(si, sh, 0), 0.0)
            sr, si = sr + akr * shr - aki * shi, si + akr * shi + aki * shr
        xr = sr + tr_t * cr - ti_t * ci
        xi = si + tr_t * ci + ti_t * cr
        re_s[pl.ds(off, 8), ls] = xr
        im_s[pl.ds(off, 8), ls] = xi
        if extra is not None:
            acc = extra(off, xr, xi, acc, row)
        return xr[edge:edge + 1, :], xi[edge:edge + 1, :], acc

    acc0 = (jnp.zeros((8, lw), F32), jnp.zeros((8, lw), F32)) if extra is not None else 0
    cr, ci, acc = lax.fori_loop(0, nt, tile, (carry_s[0:1, ls], carry_s[1:2, ls], acc0))
    carry_s[0:1, ls] = cr
    carry_s[1:2, ls] = ci
    return acc


def _s5_fwd(proj, bdr, bdi, pw, tr, ti, cdr, cdi, dskip, wglu, *, name):
    L = proj.shape[0]
    GW, S = bdr.shape
    TL = _tile(L, S5_TL)
    lw = min(S, S5_LW)

    def body(u_ref, bdr_ref, bdi_ref, pw_ref, tr_ref, ti_ref, cdr_ref, cdi_ref, d_ref, w_ref,
             y_ref, xr_ref, xi_ref, re_s, im_s, carry_s):
        @pl.when(pl.program_id(0) == 0)
        def _():
            carry_s[...] = jnp.zeros_like(carry_s)

        u = u_ref[...]
        ub = u.astype(BF16)
        re_s[...] = _dot(ub, bdr_ref[...], _NN)
        im_s[...] = _dot(ub, bdi_ref[...], _NN)
        for lc in range(S // lw):
            _scan_tiles(re_s, im_s, slice(lc * lw, (lc + 1) * lw), lw, pw_ref, tr_ref, ti_ref, carry_s, TL // 8, False)
        xrb, xib = re_s[...].astype(BF16), im_s[...].astype(BF16)
        xr_ref[...] = xrb
        xi_ref[...] = xib
        y = _dot(xrb, cdr_ref[...], _NN) - _dot(xib, cdi_ref[...], _NN) + d_ref[...] * u
        g, _ = _gelu_and_grad(y)
        z = _dot(g.astype(BF16), w_ref[...], _NN)
        y_ref[...] = g * jax.nn.sigmoid(z)

    full = lambda arr: pl.BlockSpec(arr.shape, lambda i: (0,) * arr.ndim)
    return pl.pallas_call(
        body, grid=(L // TL,),
        in_specs=[pl.BlockSpec((TL, GW), lambda i: (i, 0))] + [full(t) for t in (bdr, bdi, pw, tr, ti, cdr, cdi, dskip, wglu)],
        out_specs=[pl.BlockSpec((TL, GW), lambda i: (i, 0)), pl.BlockSpec((TL, S), lambda i: (i, 0)), pl.BlockSpec((TL, S), lambda i: (i, 0))],
        out_shape=[jax.ShapeDtypeStruct((L, GW), F32), jax.ShapeDtypeStruct((L, S), BF16), jax.ShapeDtypeStruct((L, S), BF16)],
        scratch_shapes=[pltpu.VMEM((TL, S), F32), pltpu.VMEM((TL, S), F32), pltpu.VMEM((8, S), F32)],
        compiler_params=_cparams(("arbitrary",), VMEM_LIMIT), name=name,
    )(proj, bdr, bdi, pw, tr, ti, cdr, cdi, dskip, wglu)


def _s5_bwd(proj, xr, xi, dout, bdr, bdi, pwc, trc, tic, cdr, cdi, dskip, wglu, *, name):
    L = proj.shape[0]
    GW, S = bdr.shape
    TL = _tile(L, S5_TL)
    nc = L // TL
    lw = min(S, S5_LW)

    def body(u_ref, xr_ref, xi_ref, xrp_ref, xip_ref, do_ref, bdr_ref, bdi_ref, pw_ref, tr_ref, ti_ref, cdr_ref, cdi_ref,
             d_ref, w_ref, du_ref, dw_ref, dd_ref, dcdr_ref, dcdi_ref, dbdr_ref, dbdi_ref, dar_ref, dai_ref,
             gr_s, gi_s, xfr, xfi, carry_s):
        i = pl.program_id(0)

        @pl.when(i == 0)
        def _():
            carry_s[...] = jnp.zeros_like(carry_s)
            for r in (dw_ref, dd_ref, dcdr_ref, dcdi_ref, dbdr_ref, dbdi_ref, dar_ref, dai_ref):
                r[...] = jnp.zeros_like(r)

        u = u_ref[...]
        ub = u.astype(BF16)
        xrb, xib = xr_ref[...], xi_ref[...]
        y = _dot(xrb, cdr_ref[...], _NN) - _dot(xib, cdi_ref[...], _NN) + d_ref[...] * u
        g, gp = _gelu_and_grad(y)
        gb = g.astype(BF16)
        sg = jax.nn.sigmoid(_dot(gb, w_ref[...], _NN))
        dout = do_ref[...]
        dz = (dout * g * sg * (1.0 - sg)).astype(BF16)
        dg = dout * sg + _dot(dz, w_ref[...], _NT)
        dw_ref[...] += _dot(gb, dz, _TN)
        dy = dg * gp
        dyb = dy.astype(BF16)
        dd_ref[...] += jnp.sum(dy * u, axis=0, keepdims=True)
        gr_s[...] = _dot(dyb, cdr_ref[...], _NT)
        gi_s[...] = -_dot(dyb, cdi_ref[...], _NT)
        dcdr_ref[...] += _dot(xrb, dyb, _TN)
        dcdi_ref[...] -= _dot(xib, dyb, _TN)
        live = (i < nc - 1).astype(F32)
        xfr[0:8, :] = xrp_ref[...].astype(F32) * live
        xfi[0:8, :] = xip_ref[...].astype(F32) * live
        xfr[8:, :] = xrb.astype(F32)
        xfi[8:, :] = xib.astype(F32)
        for lc in range(S // lw):
            ls = slice(lc * lw, (lc + 1) * lw)

            def extra(off, lr, li, acc, row, ls=ls):
                cur_r, cur_i = xfr[pl.ds(off + 8, 8), ls], xfi[pl.ds(off + 8, 8), ls]
                prv_r, prv_i = xfr[pl.ds(off, 8), ls], xfi[pl.ds(off, 8), ls]
                xpr = jnp.where(row >= 1, pltpu.roll(cur_r, 1, 0), prv_r[7:8, :])
                xpi = jnp.where(row >= 1, pltpu.roll(cur_i, 1, 0), prv_i[7:8, :])
                return acc[0] + lr * xpr + li * xpi, acc[1] - lr * xpi + li * xpr

            acc = _scan_tiles(gr_s, gi_s, ls, lw, pw_ref, tr_ref, ti_ref, carry_s, TL // 8, True, extra)
            dar_ref[:, ls] += acc[0]
            dai_ref[:, ls] += acc[1]
        lrb, lib = gr_s[...].astype(BF16), gi_s[...].astype(BF16)
        du_ref[...] = dy * d_ref[...] + _dot(lrb, bdr_ref[...], _NT) + _dot(lib, bdi_ref[...], _NT)
        dbdr_ref[...] += _dot(ub, lrb, _TN)
        dbdi_ref[...] += _dot(ub, lib, _TN)

    rev = lambda i: nc - 1 - i
    full = lambda arr: pl.BlockSpec(arr.shape, lambda i: (0,) * arr.ndim)
    chunk = lambda w: pl.BlockSpec((TL, w), lambda i: (rev(i), 0))
    prev8 = pl.BlockSpec((8, S), lambda i: (jnp.maximum(rev(i) * (TL // 8) - 1, 0), 0))
    acc_shapes = [(GW, GW), (1, GW), (S, GW), (S, GW), (GW, S), (GW, S), (8, S), (8, S)]
    return pl.pallas_call(
        body, grid=(nc,),
        in_specs=[chunk(GW), chunk(S), chunk(S), prev8, prev8, chunk(GW)] + [full(t) for t in (bdr, bdi, pwc, trc, tic, cdr, cdi, dskip, wglu)],
        out_specs=[chunk(GW)] + [pl.BlockSpec(s, lambda i: (0, 0)) for s in acc_shapes],
        out_shape=[jax.ShapeDtypeStruct((L, GW), F32)] + [jax.ShapeDtypeStruct(s, F32) for s in acc_shapes],
        scratch_shapes=[pltpu.VMEM((TL, S), F32), pltpu.VMEM((TL, S), F32), pltpu.VMEM((TL + 8, S), F32),
                        pltpu.VMEM((TL + 8, S), F32), pltpu.VMEM((8, S), F32)],
        compiler_params=_cparams(("arbitrary",), VMEM_LIMIT), name=name,
    )(proj, xr, xi, xr, xi, dout, bdr, bdi, pwc, trc, tic, cdr, cdi, dskip, wglu)


def _cpow_tables(ar, ai):
    def cm(a, b):
        return a[0] * b[0] - a[1] * b[1], a[0] * b[1] + a[1] * b[0]
    p = [(ar, ai)]
    for _ in range(7):
        p.append(cm(p[-1], p[0]))
    z = jnp.zeros_like(ar)
    pw = jnp.stack([p[0][0], p[0][1], p[1][0], p[1][1], p[3][0], p[3][1], z, z])
    return pw, jnp.stack([q[0] for q in p]), jnp.stack([q[1] for q in p])


def _s5_disc(lam_re, lam_im, log_dt, b_re, b_im):
    dt = jnp.exp(log_dt)[:, None]
    mag = jnp.exp(lam_re * dt)
    ab_r, ab_i = mag * jnp.cos(lam_im * dt), mag * jnp.sin(lam_im * dt)
    den = lam_re * lam_re + lam_im * lam_im
    nr, ni = ab_r - 1.0, ab_i
    f_r = ((nr * lam_re + ni * lam_im) / den)[..., None]
    f_i = ((ni * lam_re - nr * lam_im) / den)[..., None]
    return ab_r, ab_i, f_r * b_re - f_i * b_im, f_r * b_im + f_i * b_re


def _bd(t):
    G, A, B = t.shape
    return (t[:, :, None, :] * jnp.eye(G, dtype=t.dtype)[:, None, :, None]).reshape(G * A, G * B)


def _bd_extract(m, G):
    A, B = m.shape[0] // G, m.shape[1] // G
    return jnp.sum(m.reshape(G, A, G, B) * jnp.eye(G, dtype=m.dtype)[:, None, :, None], axis=2)


POOL_TQ = 256


def _band(shape, row0, col0, win, transpose):
    r = lax.broadcasted_iota(jnp.int32, shape, 0) + row0
    c = lax.broadcasted_iota(jnp.int32, shape, 1) + col0
    d = (c - r) if transpose else (r - c)
    return ((d >= 0) & (d < win)).astype(BF16)


def _band_dot(band, v):
    hi = v.astype(BF16)
    lo = (v - hi.astype(F32)).astype(BF16)
    return _dot(band, hi, _NN) + _dot(band, lo, _NN)


def _pool_p(u_prev, u_cur, i, win, tq):
    t0 = i * tq
    cnt = jnp.minimum(lax.broadcasted_iota(jnp.int32, (tq, 1), 0) + t0 + 1, win).astype(F32)
    live = (i > 0).astype(F32)
    acc = _band_dot(_band((tq, tq), t0, t0, win, False), u_cur)
    acc += _band_dot(_band((tq, tq), t0, t0 - tq, win, False), u_prev * live)
    return acc / cnt - u_cur


def _pool_fwd(proj, pool_w, pool_scale3, *, name):
    L = proj.shape[0]
    nw, PC, _ = pool_w.shape
    tq = _tile(L, POOL_TQ)

    def body(up_ref, uc_ref, w_ref, s_ref, y_ref):
        g, i = pl.program_id(0), pl.program_id(1)
        win = jnp.left_shift(2, g)
        p = _pool_p(up_ref[...], uc_ref[...], i, win, tq)
        y_ref[...] = _dot(p.astype(BF16), w_ref[...].astype(BF16), _NN) * s_ref[...]

    return pl.pallas_call(
        body, grid=(nw, L // tq),
        in_specs=[pl.BlockSpec((tq, PC), lambda g, i: (jnp.maximum(i - 1, 0), nw + g)),
                  pl.BlockSpec((tq, PC), lambda g, i: (i, nw + g)),
                  pl.BlockSpec((None, PC, PC), lambda g, i: (g, 0, 0)),
                  pl.BlockSpec((None, 1, PC), lambda g, i: (g, 0, 0))],
        out_specs=pl.BlockSpec((tq, PC), lambda g, i: (i, g)),
        out_shape=jax.ShapeDtypeStruct((L, nw * PC), F32),
        compiler_params=_cparams(("parallel", "parallel")), name=name)(proj, proj, pool_w, pool_scale3)


def _pool_bwd(proj, dy, pool_w, pool_scale3, *, name):
    L = proj.shape[0]
    nw, PC, _ = pool_w.shape
    tq = _tile(L, POOL_TQ)
    nq = L // tq

    def body(up_ref, uc_ref, dyc_ref, dyn_ref, w_ref, s_ref, du_ref, dw_ref, ds_ref):
        g, i = pl.program_id(0), pl.program_id(1)
        win = jnp.left_shift(2, g)

        @pl.when(i == 0)
        def _():
            dw_ref[...] = jnp.zeros_like(dw_ref)
            ds_ref[...] = jnp.zeros_like(ds_ref)

        wb = w_ref[...].astype(BF16)
        p = _pool_p(up_ref[...], uc_ref[...], i, win, tq).astype(BF16)
        dyc = dyc_ref[...]
        ds_ref[...] += jnp.sum(dyc * _dot(p, wb, _NN), axis=0, keepdims=True)
        dys = (dyc * s_ref[...]).astype(BF16)
        dw_ref[...] += _dot(p, dys, _TN)
        t0 = i * tq
        rows = lax.broadcasted_iota(jnp.int32, (tq, 1), 0)
        dp_c = _dot(dys, wb, _NT)
        q_c = dp_c / jnp.minimum(rows + t0 + 1, win).astype(F32)
        live = (i < nq - 1).astype(F32)
        dp_n = _dot((dyn_ref[...] * s_ref[...] * live).astype(BF16), wb, _NT)
        q_n = dp_n / jnp.minimum(rows + t0 + tq + 1, win).astype(F32)
        du = _band_dot(_band((tq, tq), t0, t0, win, True), q_c) + _band_dot(_band((tq, tq), t0, t0 + tq, win, True), q_n)
        du_ref[...] = du - dp_c

    return pl.pallas_call(
        body, grid=(nw, nq),
        in_specs=[pl.BlockSpec((tq, PC), lambda g, i: (jnp.maximum(i - 1, 0), nw + g)),
                  pl.BlockSpec((tq, PC), lambda g, i: (i, nw + g)),
                  pl.BlockSpec((tq, PC), lambda g, i: (i, g)),
                  pl.BlockSpec((tq, PC), lambda g, i: (jnp.minimum(i + 1, nq - 1), g)),
                  pl.BlockSpec((None, PC, PC), lambda g, i: (g, 0, 0)),
                  pl.BlockSpec((None, 1, PC), lambda g, i: (g, 0, 0))],
        out_specs=[pl.BlockSpec((tq, PC), lambda g, i: (i, g)),
                   pl.BlockSpec((None, PC, PC), lambda g, i: (g, 0, 0)),
                   pl.BlockSpec((None, 1, PC), lambda g, i: (g, 0, 0))],
        out_shape=[jax.ShapeDtypeStruct((L, nw * PC), F32), jax.ShapeDtypeStruct((nw, PC, PC), F32),
                   jax.ShapeDtypeStruct((nw, 1, PC), F32)],
        compiler_params=_cparams(("parallel", "arbitrary")), name=name)(proj, proj, dy, dy, pool_w, pool_scale3)


CONV_RC = 256


def _conv1_fwd(proj, w_dw, b_dw, *, name):
    L = proj.shape[0]
    GW = w_dw.shape[1]
    CB = min(GW, 128)
    nb = GW // CB
    RC = _tile(L, CONV_RC)
    n = RC + CONV_PAD

    def body(val_ref, gate_ref, w_ref, b_ref, o_ref, hp):
        hp[0:CONV_PAD, :] = jnp.zeros((CONV_PAD, CB), F32)
        hp[CONV_PAD:, :] = val_ref[...] * jax.nn.sigmoid(gate_ref[...])
        wv = w_ref[...]

        def chunk(ci, carry):
            c0 = pl.multiple_of(ci * RC, 8)
            win = hp[pl.ds(c0, n), :]
            acc = jnp.zeros((RC, CB), F32) + b_ref[...]
            for k in range(CONV_WIDTH):
                acc = acc + wv[k:k + 1, :] * pltpu.roll(win, n - (k + 2), 0)[0:RC]
            o_ref[pl.ds(c0, RC), :] = acc
            return carry

        lax.fori_loop(0, L // RC, chunk, 0)

    col = lambda off: pl.BlockSpec((L, CB), lambda c: (0, off * nb + c))
    return pl.pallas_call(
        body, grid=(nb,),
        in_specs=[col(2), col(3), pl.BlockSpec((CONV_PAD, CB), lambda c: (0, c)), pl.BlockSpec((1, CB), lambda c: (0, c))],
        out_specs=pl.BlockSpec((L, CB), lambda c: (0, c)),
        out_shape=jax.ShapeDtypeStruct((L, GW), F32),
        scratch_shapes=[pltpu.VMEM((L + CONV_PAD, CB), F32)],
        compiler_params=_cparams(("parallel",)), name=name)(proj, proj, w_dw, b_dw)


def _conv1_bwd(proj, dhc, w_dw, *, name):
    L = proj.shape[0]
    GW = w_dw.shape[1]
    CB = min(GW, 128)
    nb = GW // CB
    RC = _tile(L, CONV_RC)
    n = RC + CONV_PAD

    def body(val_ref, gate_ref, d_ref, w_ref, dval_ref, dgate_ref, dw_ref, db_ref, hp, dp):
        val = val_ref[...]
        sg = jax.nn.sigmoid(gate_ref[...])
        hp[0:CONV_PAD, :] = jnp.zeros((CONV_PAD, CB), F32)
        hp[CONV_PAD:, :] = val * sg
        dp[0:L, :] = d_ref[...]
        dp[L:, :] = jnp.zeros((CONV_PAD, CB), F32)
        dw_ref[...] = jnp.zeros_like(dw_ref)
        db_ref[...] = jnp.sum(d_ref[...], axis=0, keepdims=True)
        wv = w_ref[...]

        def chunk(ci, carry):
            c0 = pl.multiple_of(ci * RC, 8)
            win = hp[pl.ds(c0, n), :]
            dwin = dp[pl.ds(c0, n), :]
            d = dwin[0:RC]
            dh = jnp.zeros((RC, CB), F32)
            for k in range(CONV_WIDTH):
                dw_ref[k:k + 1, :] += jnp.sum(d * pltpu.roll(win, n - (k + 2), 0)[0:RC], axis=0, keepdims=True)
                sh = CONV_WIDTH - 1 - k
                dh = dh + wv[k:k + 1, :] * (pltpu.roll(dwin, n - sh, 0)[0:RC] if sh else d)
            v, s = val_ref[pl.ds(c0, RC), :], jax.nn.sigmoid(gate_ref[pl.ds(c0, RC), :])
            dval_ref[pl.ds(c0, RC), :] = dh * s
            dgate_ref[pl.ds(c0, RC), :] = dh * v * s * (1.0 - s)
            return carry

        lax.fori_loop(0, L // RC, chunk, 0)

    col = lambda off: pl.BlockSpec((L, CB), lambda c: (0, off * nb + c))
    own = pl.BlockSpec((L, CB), lambda c: (0, c))
    return pl.pallas_call(
        body, grid=(nb,),
        in_specs=[col(2), col(3), own, pl.BlockSpec((CONV_PAD, CB), lambda c: (0, c))],
        out_specs=[own, own, pl.BlockSpec((CONV_PAD, CB), lambda c: (0, c)), pl.BlockSpec((1, CB), lambda c: (0, c))],
        out_shape=[jax.ShapeDtypeStruct((L, GW), F32), jax.ShapeDtypeStruct((L, GW), F32),
                   jax.ShapeDtypeStruct((CONV_PAD, GW), F32), jax.ShapeDtypeStruct((1, GW), F32)],
        scratch_shapes=[pltpu.VMEM((L + CONV_PAD, CB), F32), pltpu.VMEM((L + CONV_PAD, CB), F32)],
        compiler_params=_cparams(("parallel",)), name=name)(proj, proj, dhc, w_dw)


def _ln_silu(hc, g, b):
    mu = jnp.mean(hc, axis=-1, keepdims=True)
    xc = hc - mu
    r = lax.rsqrt(jnp.mean(xc * xc, axis=-1, keepdims=True) + NORM_EPS)
    xh = xc * r
    a = xh * g + b
    sg = jax.nn.sigmoid(a)
    return xh, r, a, sg


def _conv2_fwd(hc, ln_g, ln_b, w_pw, *, name):
    L, GW = hc.shape
    tr = _tile(L, 256)

    def body(h_ref, g_ref, b_ref, w_ref, y_ref):
        _, _, a, sg = _ln_silu(h_ref[...], g_ref[...], b_ref[...])
        y_ref[...] = _dot((a * sg).astype(BF16), w_ref[...], _NN)

    row = pl.BlockSpec((tr, GW), lambda i: (i, 0))
    vec = pl.BlockSpec((1, GW), lambda i: (0, 0))
    return pl.pallas_call(
        body, grid=(L // tr,), in_specs=[row, vec, vec, pl.BlockSpec((GW, GW), lambda i: (0, 0))],
        out_specs=row, out_shape=jax.ShapeDtypeStruct((L, GW), F32),
        compiler_params=_cparams(("parallel",)), name=name)(hc, ln_g, ln_b, w_pw)


def _conv2_bwd(hc, ln_g, ln_b, w_pw, dy, *, name):
    L, GW = hc.shape
    tr = _tile(L, 256)

    def body(h_ref, g_ref, b_ref, w_ref, dy_ref, dh_ref, dg_ref, db_ref, dw_ref):
        @pl.when(pl.program_id(0) == 0)
        def _():
            for r in (dg_ref, db_ref, dw_ref):
                r[...] = jnp.zeros_like(r)

        xh, r, a, sg = _ln_silu(h_ref[...], g_ref[...], b_ref[...])
        dyb = dy_ref[...].astype(BF16)
        dw_ref[...] += _dot((a * sg).astype(BF16), dyb, _TN)
        da = _dot(dyb, w_ref[...], _NT) * (sg + a * sg * (1.0 - sg))
        dg_ref[...] += jnp.sum(da * xh, axis=0, keepdims=True)
        db_ref[...] += jnp.sum(da, axis=0, keepdims=True)
        dxh = da * g_ref[...]
        dh_ref[...] = r * (dxh - jnp.mean(dxh, axis=-1, keepdims=True) - xh * jnp.mean(dxh * xh, axis=-1, keepdims=True))

    row = pl.BlockSpec((tr, GW), lambda i: (i, 0))
    vec = pl.BlockSpec((1, GW), lambda i: (0, 0))
    mat = pl.BlockSpec((GW, GW), lambda i: (0, 0))
    return pl.pallas_call(
        body, grid=(L // tr,), in_specs=[row, vec, vec, mat, row],
        out_specs=[row, vec, vec, mat],
        out_shape=[jax.ShapeDtypeStruct((L, GW), F32), jax.ShapeDtypeStruct((1, GW), F32), jax.ShapeDtypeStruct((1, GW), F32),
                   jax.ShapeDtypeStruct((GW, GW), F32)],
        compiler_params=_cparams(("arbitrary",)), name=name)(hc, ln_g, ln_b, w_pw, dy)


def _grp_fwd(ys, g, *, name):
    L, GW = ys[0].shape
    tr = _tile(L, 256)

    def body(*refs):
        g_ref, o_ref = refs[4], refs[5]
        for m in range(N_MIXERS):
            yv = refs[m][...]
            r = lax.rsqrt(jnp.mean(yv * yv, axis=-1, keepdims=True) + NORM_EPS)
            o_ref[:, m * GW:(m + 1) * GW] = (yv * r * g_ref[:, m * GW:(m + 1) * GW]).astype(o_ref.dtype)

    row = pl.BlockSpec((tr, GW), lambda i: (i, 0))
    return pl.pallas_call(
        body, grid=(L // tr,), in_specs=[row] * 4 + [pl.BlockSpec((1, N_MIXERS * GW), lambda i: (0, 0))],
        out_specs=pl.BlockSpec((tr, N_MIXERS * GW), lambda i: (i, 0)),
        out_shape=jax.ShapeDtypeStruct((L, N_MIXERS * GW), BF16),
        compiler_params=_cparams(("parallel",)), name=name)(*ys, g)


def _grp_bwd(ys, g, dy, *, name):
    L, GW = ys[0].shape
    tr = _tile(L, 256)

    def body(*refs):
        g_ref, dy_ref = refs[4], refs[5]
        outs, dg_ref = refs[6:10], refs[10]

        @pl.when(pl.program_id(0) == 0)
        def _():
            dg_ref[...] = jnp.zeros_like(dg_ref)

        for m in range(N_MIXERS):
            cs = slice(m * GW, (m + 1) * GW)
            yv = refs[m][...]
            r = lax.rsqrt(jnp.mean(yv * yv, axis=-1, keepdims=True) + NORM_EPS)
            xh = yv * r
            dyv = dy_ref[:, cs]
            dyg = dyv * g_ref[:, cs]
            outs[m][...] = r * (dyg - xh * jnp.mean(dyg * xh, axis=-1, keepdims=True))
            dg_ref[:, cs] += jnp.sum(dyv * xh, axis=0, keepdims=True)

    row = pl.BlockSpec((tr, GW), lambda i: (i, 0))
    wide = pl.BlockSpec((tr, N_MIXERS * GW), lambda i: (i, 0))
    vec = pl.BlockSpec((1, N_MIXERS * GW), lambda i: (0, 0))
    return pl.pallas_call(
        body, grid=(L // tr,), in_specs=[row] * 4 + [vec, wide],
        out_specs=[row] * 4 + [vec],
        out_shape=[jax.ShapeDtypeStruct((L, GW), F32)] * 4 + [jax.ShapeDtypeStruct((1, N_MIXERS * GW), F32)],
        compiler_params=_cparams(("arbitrary",)), name=name)(*ys, g, dy)


def _att_first(j, br, nblk):
    per = lax.shift_right_logical(jnp.int32(nblk), 2 * br)
    return jnp.bitwise_and(j, per - 1) == 0


def _att_fwd(q, k, vx, bias, *, name):
    nbr, H, L, E = q.shape
    B = ATT_BLOCK
    nblk = L // B

    def body(q_ref, k_ref, v_ref, b_ref, o_ref):
        br = pl.program_id(0)
        lane = lax.broadcasted_iota(jnp.int32, (B, 2 * E), 1)

        def blk(j, carry):
            off = pl.multiple_of(j * B, B)
            poff = pl.multiple_of(jnp.maximum(j - 1, 0) * B, B)
            qv = q_ref[pl.ds(off, B), :]
            s_p = _dot(qv, k_ref[pl.ds(poff, B), :], _NT) + b_ref[:, 0:B]
            s_c = _dot(qv, k_ref[pl.ds(off, B), :], _NT) + b_ref[:, B:2 * B]
            s_p = jnp.where(_att_first(j, br, nblk), NEG_INF, s_p)
            m = jnp.maximum(jnp.max(s_p, axis=-1, keepdims=True), jnp.max(s_c, axis=-1, keepdims=True))
            p_p, p_c = jnp.exp(s_p - m), jnp.exp(s_c - m)
            den = jnp.sum(p_p, axis=-1, keepdims=True) + jnp.sum(p_c, axis=-1, keepdims=True)
            acc = _dot(p_p.astype(BF16), v_ref[pl.ds(poff, B), :], _NN) + _dot(p_c.astype(BF16), v_ref[pl.ds(off, B), :], _NN)
            o_ref[pl.ds(off, B), :] = jnp.where(lane < E, acc / den, m + jnp.log(den))
            return carry

        lax.fori_loop(0, nblk, blk, 0)

    seq = lambda w: pl.BlockSpec((None, None, L, w), lambda b, h: (b, h, 0, 0))
    return pl.pallas_call(
        body, grid=(nbr, H),
        in_specs=[seq(E), seq(E), seq(2 * E), pl.BlockSpec((None, None, B, 2 * B), lambda b, h: (b, h, 0, 0))],
        out_specs=seq(2 * E), out_shape=jax.ShapeDtypeStruct((nbr, H, L, 2 * E), F32),
        compiler_params=_cparams(("parallel", "parallel")), name=name)(q, k, vx, bias)


def _att_bwd(q, k, vx, bias, ox, dox, *, name):
    nbr, H, L, E = q.shape
    B = ATT_BLOCK
    nblk = L // B

    def body(q_ref, k_ref, v_ref, b_ref, o_ref, do_ref, dq_ref, dk_ref, dv_ref, db_ref):
        br = pl.program_id(0)
        dk_ref[...] = jnp.zeros_like(dk_ref)
        dv_ref[...] = jnp.zeros_like(dv_ref)
        db_ref[...] = jnp.zeros_like(db_ref)
        lane = lax.broadcasted_iota(jnp.int32, (B, 2 * E), 1)

        def blk(j, carry):
            off = pl.multiple_of(j * B, B)
            poff = pl.multiple_of(jnp.maximum(j - 1, 0) * B, B)
            qv, kc, kp = q_ref[pl.ds(off, B), :], k_ref[pl.ds(off, B), :], k_ref[pl.ds(poff, B), :]
            vc, vp = v_ref[pl.ds(off, B), :], v_ref[pl.ds(poff, B), :]
            oe, de = o_ref[pl.ds(off, B), :], do_ref[pl.ds(off, B), :]
            lse, dlse = oe[:, E:E + 1], de[:, E:E + 1]
            do = jnp.where(lane < E, de, 0.0)
            dm = jnp.sum(do * oe, axis=-1, keepdims=True) - dlse
            dob = do.astype(BF16)
            s_p = _dot(qv, kp, _NT) + b_ref[:, 0:B]
            s_c = _dot(qv, kc, _NT) + b_ref[:, B:2 * B]
            s_p = jnp.where(_att_first(j, br, nblk), NEG_INF, s_p)
            p_p, p_c = jnp.exp(s_p - lse), jnp.exp(s_c - lse)
            ds_p = p_p * (_dot(dob, vp, _NT) - dm)
            ds_c = p_c * (_dot(dob, vc, _NT) - dm)
            db_ref[:, 0:B] += ds_p
            db_ref[:, B:2 * B] += ds_c
            dsb_p, dsb_c = ds_p.astype(BF16), ds_c.astype(BF16)
            dq_ref[pl.ds(off, B), :] = _dot(dsb_p, kp, _NN) + _dot(dsb_c, kc, _NN)
            dk_ref[pl.ds(poff, B), :] += _dot(dsb_p, qv, _TN)
            dk_ref[pl.ds(off, B), :] += _dot(dsb_c, qv, _TN)
            dv_ref[pl.ds(poff, B), :] += _dot(p_p.astype(BF16), dob, _TN)
            dv_ref[pl.ds(off, B), :] += _dot(p_c.astype(BF16), dob, _TN)
            return carry

        lax.fori_loop(0, nblk, blk, 0)

    seq = lambda w: pl.BlockSpec((None, None, L, w), lambda b, h: (b, h, 0, 0))
    bsp = pl.BlockSpec((None, None, B, 2 * B), lambda b, h: (b, h, 0, 0))
    return pl.pallas_call(
        body, grid=(nbr, H),
        in_specs=[seq(E), seq(E), seq(2 * E), bsp, seq(2 * E), seq(2 * E)],
        out_specs=[seq(E), seq(E), seq(2 * E), bsp],
        out_shape=[jax.ShapeDtypeStruct((nbr, H, L, E), F32), jax.ShapeDtypeStruct((nbr, H, L, E), F32),
                   jax.ShapeDtypeStruct((nbr, H, L, 2 * E), F32), jax.ShapeDtypeStruct((nbr, H, B, 2 * B), F32)],
        compiler_params=_cparams(("parallel", "parallel")), name=name)(q, k, vx, bias, ox, dox)


def _mix_weights(xs, lane, E, W):
    al = [jnp.where(lane < E, pltpu.roll(x, W - E, 1), x) for x in xs]
    m = functools.reduce(jnp.maximum, al)
    es = [jnp.exp(a - m) for a in al]
    tot = functools.reduce(lambda a, b: a + b, es)
    return [e / tot for e in es]


def _mix_fwd(ox, *, name):
    nbr, L, W = ox.shape
    E = W // ATT_HEADS // 2
    tr = _tile(L, 256)

    def body(x_ref, o_ref):
        lane = lax.broadcasted_iota(jnp.int32, (tr, W), 1) % (2 * E)
        xs = [x_ref[b] for b in range(nbr)]
        ws = _mix_weights(xs, lane, E, W)
        o_ref[...] = functools.reduce(lambda a, b: a + b, [w * x for w, x in zip(ws, xs)])

    return pl.pallas_call(
        body, grid=(L // tr,), in_specs=[pl.BlockSpec((nbr, tr, W), lambda i: (0, i, 0))],
        out_specs=pl.BlockSpec((tr, W), lambda i: (i, 0)), out_shape=jax.ShapeDtypeStruct((L, W), F32),
        compiler_params=_cparams(("parallel",)), name=name)(ox)


def _mix_bwd(ox, dy, *, name):
    nbr, L, W = ox.shape
    E = W // ATT_HEADS // 2
    tr = _tile(L, 256)

    def body(x_ref, dy_ref, o_ref):
        lane = lax.broadcasted_iota(jnp.int32, (tr, W), 1) % (2 * E)
        xs = [x_ref[b] for b in range(nbr)]
        ws = _mix_weights(xs, lane, E, W)
        mix = functools.reduce(lambda a, b: a + b, [w * x for w, x in zip(ws, xs)])
        dyv = jnp.where(lane < E, dy_ref[...], 0.0)
        r = lax.broadcasted_iota(jnp.int32, (W, W), 0)
        c = lax.broadcasted_iota(jnp.int32, (W, W), 1)
        seg = ((r // (2 * E) == c // (2 * E)) & (r % (2 * E) < E)).astype(F32)
        for b in range(nbr):
            t = dyv * (xs[b] - mix)
            dl = ws[b] * jnp.dot(t, seg, preferred_element_type=F32, precision=lax.Precision.HIGHEST)
            o_ref[b] = jnp.where(lane < E, ws[b] * dyv, dl)

    return pl.pallas_call(
        body, grid=(L // tr,),
        in_specs=[pl.BlockSpec((nbr, tr, W), lambda i: (0, i, 0)), pl.BlockSpec((tr, W), lambda i: (i, 0))],
        out_specs=pl.BlockSpec((nbr, tr, W), lambda i: (0, i, 0)), out_shape=jax.ShapeDtypeStruct((nbr, L, W), F32),
        compiler_params=_cparams(("parallel",), VMEM_LIMIT), name=name)(ox, dy)


def _datt_lanes(GW):
    E = GW // ATT_HEADS
    lb = min(GW, 128)
    return lb, lb // E, E


def _datt_rows(c, br, nblk):
    dil = 4 ** br
    nbs = nblk // dil
    r, jb = c // nbs, c % nbs
    return r + dil * ATT_BLOCK * jb, r + dil * ATT_BLOCK * jnp.maximum(jb - 1, 0), jb == 0


def _datt_fwd(proj, bias, *, name):
    L = proj.shape[0]
    GW = proj.shape[1] // 7
    lb, hps, E = _datt_lanes(GW)
    B = ATT_BLOCK
    nblk = L // B
    scale = E ** -0.5

    def body(q_ref, k_ref, v_ref, b_ref, y_ref, lse_ref, m_s, l_s):
        m_s[...] = jnp.full(m_s.shape, NEG_INF, F32)
        l_s[...] = jnp.zeros(l_s.shape, F32)
        y_ref[...] = jnp.zeros(y_ref.shape, F32)
        lane = lax.broadcasted_iota(jnp.int32, (B, lb), 1) // E

        for br, (_, dil) in enumerate(DILATED_PATTERNS):
            def combo(c, carry, br=br, dil=dil):
                start, pstart, first = _datt_rows(c, br, nblk)
                rows, prows = pl.ds(start, B, stride=dil), pl.ds(pstart, B, stride=dil)
                q = q_ref[rows, :] * scale
                kc, kp = k_ref[rows, :].astype(BF16), k_ref[prows, :].astype(BF16)
                vc, vp = v_ref[rows, :].astype(BF16), v_ref[prows, :].astype(BF16)
                m_old, l_old, a_old = m_s[rows, :], l_s[rows, :], y_ref[rows, :]
                m_new, l_new, a_new = m_old, l_old, a_old
                for a in range(hps):
                    sel = lane == a
                    qm = jnp.where(sel, q, 0.0).astype(BF16)
                    s_p = _dot(qm, kp, _NT) + b_ref[br, a, :, 0:B]
                    s_c = _dot(qm, kc, _NT) + b_ref[br, a, :, B:2 * B]
                    s_p = jnp.where(first, NEG_INF, s_p)
                    mo, lo = m_old[:, a * E:a * E + 1], l_old[:, a * E:a * E + 1]
                    mn = jnp.maximum(mo, jnp.maximum(jnp.max(s_p, axis=-1, keepdims=True), jnp.max(s_c, axis=-1, keepdims=True)))
                    alpha = jnp.exp(mo - mn)
                    p_p, p_c = jnp.exp(s_p - mn), jnp.exp(s_c - mn)
                    ln = alpha * lo + jnp.sum(p_p, axis=-1, keepdims=True) + jnp.sum(p_c, axis=-1, keepdims=True)
                    pv = _dot(p_p.astype(BF16), vp, _NN) + _dot(p_c.astype(BF16), vc, _NN)
                    m_new = jnp.where(sel, mn, m_new)
                    l_new = jnp.where(sel, ln, l_new)
                    a_new = jnp.where(sel, alpha * a_old + pv, a_new)
                m_s[rows, :] = m_new
                l_s[rows, :] = l_new
                y_ref[rows, :] = a_new
                return carry

            lax.fori_loop(0, nblk, combo, 0)
        y_ref[...] = y_ref[...] / l_s[...]
        lse_ref[...] = m_s[...] + jnp.log(l_s[...])

    col = lambda off: pl.BlockSpec((L, lb), lambda h: (0, off * (GW // lb) + h))
    own = pl.BlockSpec((L, lb), lambda h: (0, h))
    return pl.pallas_call(
        body, grid=(GW // lb,),
        in_specs=[col(4), col(5), col(6), pl.BlockSpec((len(DILATED_PATTERNS), hps, B, 2 * B), lambda h: (0, h, 0, 0))],
        out_specs=[own, own], out_shape=[jax.ShapeDtypeStruct((L, GW), F32)] * 2,
        scratch_shapes=[pltpu.VMEM((L, lb), F32), pltpu.VMEM((L, lb), F32)],
        compiler_params=_cparams(("parallel",), VMEM_LIMIT), name=name)(proj, proj, proj, bias)


def _datt_bwd(proj, bias, y, lse, dy, *, name):
    L = proj.shape[0]
    GW = proj.shape[1] // 7
    lb, hps, E = _datt_lanes(GW)
    B = ATT_BLOCK
    nblk = L // B
    scale = E ** -0.5

    def body(q_ref, k_ref, v_ref, b_ref, y_ref, lse_ref, dy_ref, dq_ref, dk_ref, dv_ref, db_ref):
        for r in (dq_ref, dk_ref, dv_ref, db_ref):
            r[...] = jnp.zeros(r.shape, F32)
        lane = lax.broadcasted_iota(jnp.int32, (B, lb), 1) // E

        for br, (_, dil) in enumerate(DILATED_PATTERNS):
            def combo(c, carry, br=br, dil=dil):
                start, pstart, first = _datt_rows(c, br, nblk)
                rows, prows = pl.ds(start, B, stride=dil), pl.ds(pstart, B, stride=dil)
                q = q_ref[rows, :] * scale
                kc, kp = k_ref[rows, :].astype(BF16), k_ref[prows, :].astype(BF16)
                vc, vp = v_ref[rows, :].astype(BF16), v_ref[prows, :].astype(BF16)
                dyr, lser = dy_ref[rows, :], lse_ref[rows, :]
                dyy = dyr * y_ref[rows, :]
                dq = jnp.zeros((B, lb), F32)
                dkc, dkp, dvc, dvp = dq, dq, dq, dq
                for a in range(hps):
                    sel = lane == a
                    qm = jnp.where(sel, q, 0.0).astype(BF16)
                    dym = jnp.where(sel, dyr, 0.0).astype(BF16)
                    dm = jnp.sum(jnp.where(sel, dyy, 0.0), axis=-1, keepdims=True)
                    lse_a = lser[:, a * E:a * E + 1]
                    s_p = _dot(qm, kp, _NT) + b_ref[br, a, :, 0:B]
                    s_c = _dot(qm, kc, _NT) + b_ref[br, a, :, B:2 * B]
                    s_p = jnp.where(first, NEG_INF, s_p)
                    p_p, p_c = jnp.exp(s_p - lse_a), jnp.exp(s_c - lse_a)
                    ds_p = p_p * (_dot(dym, vp, _NT) - dm)
                    ds_c = p_c * (_dot(dym, vc, _NT) - dm)
                    db_ref[br, a, :, 0:B] += ds_p
                    db_ref[br, a, :, B:2 * B] += ds_c
                    dsb_p, dsb_c = ds_p.astype(BF16), ds_c.astype(BF16)
                    dq = dq + jnp.where(sel, _dot(dsb_p, kp, _NN) + _dot(dsb_c, kc, _NN), 0.0)
                    dkp = dkp + _dot(dsb_p, qm, _TN)
                    dkc = dkc + _dot(dsb_c, qm, _TN)
                    dvp = dvp + _dot(p_p.astype(BF16), dym, _TN)
                    dvc = dvc + _dot(p_c.astype(BF16), dym, _TN)
                dq_ref[rows, :] += dq * scale
                dk_ref[prows, :] += dkp
                dk_ref[rows, :] += dkc
                dv_ref[prows, :] += dvp
                dv_ref[rows, :] += dvc
                return carry

            lax.fori_loop(0, nblk, combo, 0)

    col = lambda off: pl.BlockSpec((L, lb), lambda h: (0, off * (GW // lb) + h))
    own = pl.BlockSpec((L, lb), lambda h: (0, h))
    bsp = pl.BlockSpec((len(DILATED_PATTERNS), hps, B, 2 * B), lambda h: (0, h, 0, 0))
    return pl.pallas_call(
        body, grid=(GW // lb,),
        in_specs=[col(4), col(5), col(6), bsp, own, own, own], out_specs=[own, own, own, bsp],
        out_shape=[jax.ShapeDtypeStruct((L, GW), F32)] * 3 + [jax.ShapeDtypeStruct((len(DILATED_PATTERNS), ATT_HEADS, B, 2 * B), F32)],
        compiler_params=_cparams(("parallel",), VMEM_LIMIT), name=name)(proj, proj, proj, bias, y, lse, dy)


def _t5_bucket(dist):
    n = np.maximum(dist, 0)
    max_exact = REL_BUCKETS // 2
    large = max_exact + (np.log(np.maximum(n, 1) / max_exact) / np.log(REL_MAX_DIST / max_exact)
                         * (REL_BUCKETS - max_exact)).astype(np.int64)
    large = np.minimum(large, REL_BUCKETS - 1)
    return np.where(n < max_exact, n, large).astype(np.int32)


def _att_tables():
    a = np.arange(ATT_BLOCK)[:, None]
    b = np.arange(2 * ATT_BLOCK)[None, :]
    sub = a + ATT_BLOCK - b
    buckets, valids = [], []
    for window, dil in DILATED_PATTERNS:
        buckets.append(_t5_bucket(sub * dil))
        valids.append((sub >= 0) & (sub <= window // dil))
    return np.stack(buckets), np.stack(valids)


def _to_branches(t, H):
    L = t.shape[0]
    E = t.shape[1] // H
    out = []
    for _, dil in DILATED_PATTERNS:
        out.append(t.reshape(L // dil, dil, H, E).transpose(2, 1, 0, 3).reshape(H, L, E))
    return jnp.stack(out)


def _from_branches(t):
    _, H, L, E = t.shape
    out = []
    for b, (_, dil) in enumerate(DILATED_PATTERNS):
        out.append(t[b].reshape(H, dil, L // dil, E).transpose(2, 1, 0, 3).reshape(L, H * E))
    return jnp.stack(out)


def _xatt_fwd(q, k, v, *, name):
    L, XW = q.shape
    M = k.shape[0]
    tr = _tile(L, 512)
    scale = X_HEAD_DIM ** -0.5

    def body(q_ref, k_ref, v_ref, o_ref):
        s = _dot(q_ref[...], k_ref[...], _NT) * scale
        p = jnp.exp(s - jnp.max(s, axis=-1, keepdims=True))
        den = jnp.sum(p, axis=-1, keepdims=True)
        o_ref[...] = (_dot(p.astype(BF16), v_ref[...], _NN) / den).astype(o_ref.dtype)

    qs = pl.BlockSpec((tr, X_HEAD_DIM), lambda h, i: (i, h))
    ks = pl.BlockSpec((M, X_HEAD_DIM), lambda h, i: (0, h))
    return pl.pallas_call(
        body, grid=(XW // X_HEAD_DIM, L // tr), in_specs=[qs, ks, ks], out_specs=qs,
        out_shape=jax.ShapeDtypeStruct((L, XW), BF16),
        compiler_params=_cparams(("parallel", "parallel")), name=name)(q, k, v)


def _xatt_bwd(q, k, v, do, *, name):
    L, XW = q.shape
    M = k.shape[0]
    tr = _tile(L, 512)
    scale = X_HEAD_DIM ** -0.5

    def body(q_ref, k_ref, v_ref, do_ref, dq_ref, dk_ref, dv_ref):
        @pl.when(pl.program_id(1) == 0)
        def _():
            dk_ref[...] = jnp.zeros_like(dk_ref)
            dv_ref[...] = jnp.zeros_like(dv_ref)

        qv, kv, vv = q_ref[...], k_ref[...], v_ref[...]
        s = _dot(qv, kv, _NT) * scale
        p = jnp.exp(s - jnp.max(s, axis=-1, keepdims=True))
        p = p / jnp.sum(p, axis=-1, keepdims=True)
        dob = do_ref[...].astype(BF16)
        pb = p.astype(BF16)
        dv_ref[...] += _dot(pb, dob, _TN)
        dp = _dot(dob, vv, _NT)
        ds = (p * (dp - jnp.sum(dp * p, axis=-1, keepdims=True)) * scale).astype(BF16)
        dq_ref[...] = _dot(ds, kv, _NN)
        dk_ref[...] += _dot(ds, qv, _TN)

    qs = pl.BlockSpec((tr, X_HEAD_DIM), lambda h, i: (i, h))
    ks = pl.BlockSpec((M, X_HEAD_DIM), lambda h, i: (0, h))
    return pl.pallas_call(
        body, grid=(XW // X_HEAD_DIM, L // tr), in_specs=[qs, ks, ks, qs], out_specs=[qs, ks, ks],
        out_shape=[jax.ShapeDtypeStruct((L, XW), F32), jax.ShapeDtypeStruct((M, XW), F32), jax.ShapeDtypeStruct((M, XW), F32)],
        compiler_params=_cparams(("parallel", "arbitrary")), name=name)(q, k, v, do)


_ANY = pl.BlockSpec(memory_space=pl.ANY)


def _place():
    mx, my, mc = lax.axis_index("x"), lax.axis_index("y"), lax.axis_index("c")
    chips = [(1 - mx, my), (mx, 1 - my), (1 - mx, 1 - my)]
    return mx, my, mc, chips


def _rcopy(src, dst, ssem, rsem, dev):
    return pltpu.make_async_remote_copy(src_ref=src, dst_ref=dst, send_sem=ssem, recv_sem=rsem, device_id=dev,
                                        device_id_type=MESH)


STAGE_BYTES = 2 * 1024 * 1024


def _staged_copy(pairs, buf, isem, osem):
    n = len(pairs)
    ins = [pltpu.make_async_copy(src, buf.at[i % 2], isem.at[i % 2]) for i, (src, _) in enumerate(pairs)]
    outs = [pltpu.make_async_copy(buf.at[i % 2], dst, osem.at[i % 2]) for i, (_, dst) in enumerate(pairs)]
    ins[0].start()
    for i in range(n):
        if i + 1 < n:
            if i >= 1:
                outs[i - 1].wait()
            ins[i + 1].start()
        ins[i].wait()
        outs[i].start()
    for i in range(max(n - 2, 0), n):
        outs[i].wait()


_HBM = pl.BlockSpec(memory_space=pltpu.HBM)
_SEMS = pl.BlockSpec(memory_space=pltpu.SEMAPHORE)
_VM = pl.BlockSpec(memory_space=pltpu.VMEM)
_DATAFLOW = pltpu.SideEffectType.DATAFLOW_SIDE_EFFECTING


def _ag_copies(x_refs, land_refs, ssem, rsem, inbound):
    mx, my, mc, chips = _place()
    me = 2 * mx + my
    cps = []
    for w, (x_ref, land_ref) in enumerate(zip(x_refs, land_refs)):
        R2 = x_ref.shape[0] // 2
        mine = x_ref.at[pl.ds(mc * R2, R2)]
        for j, (px, py) in enumerate(chips):
            dst = land_ref.at[2 * px + py, mc] if inbound else land_ref.at[me, mc]
            cps.append(_rcopy(mine, dst, ssem.at[3 * w + j], rsem.at[3 * w + j], (px, py, mc)))
    return cps


def _ag_start(xs, token, *, name):
    n = len(xs)
    lands = [pltpu.with_memory_space_constraint(lax.empty((N_CHIPS, 2, x.shape[0] // 2, x.shape[1]), x.dtype), pltpu.HBM) for x in xs]
    xs = [pltpu.with_memory_space_constraint(x, pltpu.HBM) for x in xs]

    def body(*refs):
        x_refs, land_refs, tok_ref = refs[:n], refs[n:2 * n], refs[2 * n]
        ssem, rsem, tok_out = refs[2 * n + 1], refs[2 * n + 2], refs[-1]
        for cp in _ag_copies(x_refs, land_refs, ssem, rsem, False):
            cp.start()
        tok_out[...] = tok_ref[...]

    outs = pl.pallas_call(
        body, name=name,
        out_shape=(pltpu.SemaphoreType.DMA((3 * n,)), pltpu.SemaphoreType.DMA((3 * n,)),
                   *[pltpu.HBM(x.shape, x.dtype) for x in xs], *[pltpu.HBM(t.shape, t.dtype) for t in lands],
                   jax.ShapeDtypeStruct(token.shape, token.dtype)),
        in_specs=[_HBM] * (2 * n) + [_VM], out_specs=(_SEMS, _SEMS, *[_HBM] * (2 * n), _VM),
        input_output_aliases={i: 2 + i for i in range(2 * n)},
        compiler_params=pltpu.CompilerParams(has_side_effects=_DATAFLOW),
    )(*xs, *lands, token)
    return outs[0], outs[1], list(outs[2:2 + n]), list(outs[2 + n:2 + 2 * n]), outs[-1]


def _ag_wait(ssem, rsem, xs, lands, after, *, name):
    n = len(xs)

    def body(*refs):
        x_refs, land_refs, ssem_ref, rsem_ref = refs[:n], refs[n:2 * n], refs[2 * n], refs[2 * n + 1]
        for cp in _ag_copies(x_refs, land_refs, ssem_ref, rsem_ref, True):
            cp.wait_send()
            cp.wait_recv()

    outs = pl.pallas_call(
        body, name=name,
        out_shape=(*[pltpu.HBM(x.shape, x.dtype) for x in xs], *[pltpu.HBM(t.shape, t.dtype) for t in lands]),
        in_specs=[_HBM] * (2 * n) + [_SEMS, _SEMS, _ANY], out_specs=tuple([_HBM] * (2 * n)),
        input_output_aliases={i: i for i in range(2 * n)},
        compiler_params=pltpu.CompilerParams(has_side_effects=_DATAFLOW),
    )(*xs, *lands, ssem, rsem, after)
    return list(outs[:n]), list(outs[n:])


def _ag_finish(land, x, *, name):
    R, C = x.shape
    R2 = R // 2
    ch = _rows_tile(R2, max(8, STAGE_BYTES // (C * x.dtype.itemsize)))

    def body(l_ref, x_ref, o_ref, ssem, rsem, buf, isem, osem):
        mx, my, mc, chips = _place()
        me = 2 * mx + my
        sib = (mx, my, 1 - mc)
        passed = []
        for j, (px, py) in enumerate(chips):
            got = o_ref.at[2 * px + py, mc]
            passed.append(_rcopy(got, got, ssem.at[j], rsem.at[j], sib))
        for cp in passed:
            cp.start()
        _staged_copy([(x_ref.at[pl.ds(h * R2 + r, ch)], o_ref.at[me, h, pl.ds(r, ch)]) for h in range(2) for r in range(0, R2, ch)],
                     buf, isem, osem)
        for j, (px, py) in enumerate(chips):
            theirs = o_ref.at[2 * px + py, 1 - mc]
            _rcopy(theirs, theirs, ssem.at[j], rsem.at[j], sib).wait_recv()
        for cp in passed:
            cp.wait_send()

    return pl.pallas_call(
        body, in_specs=[_ANY, _ANY], out_specs=_ANY, out_shape=jax.ShapeDtypeStruct(land.shape, land.dtype),
        input_output_aliases={0: 0},
        scratch_shapes=[pltpu.SemaphoreType.DMA((3,)), pltpu.SemaphoreType.DMA((3,)), pltpu.VMEM((2, ch, C), x.dtype),
                        pltpu.SemaphoreType.DMA((2,)), pltpu.SemaphoreType.DMA((2,))],
        name=name)(land, x)


def _ag4(x, *, name):
    def body(x_ref, o_ref, ssem, rsem, lsem):
        mx, my, mc, chips = _place()
        me = 2 * mx + my
        loc = pltpu.make_async_copy(x_ref, o_ref.at[me], lsem)
        loc.start()
        sends = [_rcopy(x_ref, o_ref.at[me], ssem.at[j], rsem.at[j], (px, py, mc)) for j, (px, py) in enumerate(chips)]
        for cp in sends:
            cp.start()
        for j, (px, py) in enumerate(chips):
            _rcopy(x_ref, o_ref.at[2 * px + py], ssem.at[j], rsem.at[j], (px, py, mc)).wait_recv()
        for cp in sends:
            cp.wait_send()
        loc.wait()

    return pl.pallas_call(
        body, in_specs=[_ANY], out_specs=_ANY, out_shape=jax.ShapeDtypeStruct((N_CHIPS,) + x.shape, x.dtype),
        scratch_shapes=[pltpu.SemaphoreType.DMA((3,)), pltpu.SemaphoreType.DMA((3,)), pltpu.SemaphoreType.DMA(())],
        name=name)(x)


def _rs_sib(gs, *, name):
    n = len(gs)

    def body(*refs):
        g_refs, t_refs, (ssem, rsem) = refs[:n], refs[n:2 * n], refs[2 * n:]
        mx, my, mc, _ = _place()
        sib = (mx, my, 1 - mc)
        sends = []
        for w in range(n):
            R2 = g_refs[w].shape[1] // 2
            for k in range(N_CHIPS):
                sends.append(_rcopy(g_refs[w].at[k, pl.ds((1 - mc) * R2, R2)], t_refs[w].at[k], ssem.at[N_CHIPS * w + k],
                                    rsem.at[N_CHIPS * w + k], sib))
        for cp in sends:
            cp.start()
        for w in range(n):
            for k in range(N_CHIPS):
                t = t_refs[w].at[k]
                _rcopy(t, t, ssem.at[N_CHIPS * w + k], rsem.at[N_CHIPS * w + k], sib).wait_recv()
        for cp in sends:
            cp.wait_send()

    return pl.pallas_call(
        body, in_specs=[_ANY] * n, out_specs=[_ANY] * n,
        out_shape=[jax.ShapeDtypeStruct((N_CHIPS, g.shape[1] // 2, g.shape[2]), g.dtype) for g in gs],
        scratch_shapes=[pltpu.SemaphoreType.DMA((N_CHIPS * n,)), pltpu.SemaphoreType.DMA((N_CHIPS * n,))], name=name)(*gs)


def _a2a_copies(h_refs, got_refs, ssem, rsem, inbound):
    mx, my, mc, chips = _place()
    me = 2 * mx + my
    cps = []
    for w, (h_ref, got_ref) in enumerate(zip(h_refs, got_refs)):
        for j, (px, py) in enumerate(chips):
            dst = got_ref.at[2 * px + py] if inbound else got_ref.at[me]
            cps.append(_rcopy(h_ref.at[2 * px + py], dst, ssem.at[3 * w + j], rsem.at[3 * w + j], (px, py, mc)))
    return cps


def _a2a_start(hs, *, name):
    n = len(hs)
    gots = [pltpu.with_memory_space_constraint(lax.empty(h.shape, h.dtype), pltpu.HBM) for h in hs]
    hs = [pltpu.with_memory_space_constraint(h, pltpu.HBM) for h in hs]

    def body(*refs):
        h_refs, got_refs = refs[:n], refs[n:2 * n]
        ssem, rsem, tok_out = refs[2 * n], refs[2 * n + 1], refs[-1]
        for cp in _a2a_copies(h_refs, got_refs, ssem, rsem, False):
            cp.start()
        tok_out[...] = jnp.zeros_like(tok_out)

    outs = pl.pallas_call(
        body, name=name,
        out_shape=(pltpu.SemaphoreType.DMA((3 * n,)), pltpu.SemaphoreType.DMA((3 * n,)),
                   *[pltpu.HBM(h.shape, h.dtype) for h in hs], *[pltpu.HBM(h.shape, h.dtype) for h in hs],
                   jax.ShapeDtypeStruct((8, 128), F32)),
        in_specs=[_HBM] * (2 * n), out_specs=(_SEMS, _SEMS, *[_HBM] * (2 * n), _VM),
        input_output_aliases={i: 2 + i for i in range(2 * n)},
        compiler_params=pltpu.CompilerParams(has_side_effects=_DATAFLOW),
    )(*hs, *gots)
    return outs[0], outs[1], list(outs[2:2 + n]), list(outs[2 + n:2 + 2 * n]), outs[-1]


def _a2a_wait(ssem, rsem, hs, gots, after, *, name):
    n = len(hs)

    def body(*refs):
        h_refs, got_refs, ssem_ref, rsem_ref = refs[:n], refs[n:2 * n], refs[2 * n], refs[2 * n + 1]
        for cp in _a2a_copies(h_refs, got_refs, ssem_ref, rsem_ref, True):
            cp.wait_send()
            cp.wait_recv()

    outs = pl.pallas_call(
        body, name=name,
        out_shape=tuple(pltpu.HBM(h.shape, h.dtype) for h in hs + gots),
        in_specs=[_HBM] * (2 * n) + [_SEMS, _SEMS, _ANY], out_specs=tuple([_HBM] * (2 * n)),
        input_output_aliases={i: i for i in range(2 * n)},
        compiler_params=pltpu.CompilerParams(has_side_effects=_DATAFLOW),
    )(*hs, *gots, ssem, rsem, after)
    return list(outs[:n]), list(outs[n:])


def _sib_fill(bufs, *, name):
    n = len(bufs)

    def body(*refs):
        o_refs, (ssem, rsem) = refs[n:2 * n], refs[2 * n:]
        mx, my, mc, _ = _place()
        sib = (mx, my, 1 - mc)
        sends = [_rcopy(o_refs[w].at[mc], o_refs[w].at[mc], ssem.at[w], rsem.at[w], sib) for w in range(n)]
        for cp in sends:
            cp.start()
        for w in range(n):
            t = o_refs[w].at[1 - mc]
            _rcopy(t, t, ssem.at[w], rsem.at[w], sib).wait_recv()
        for cp in sends:
            cp.wait_send()

    return pl.pallas_call(
        body, in_specs=[_ANY] * n, out_specs=[_ANY] * n, out_shape=[jax.ShapeDtypeStruct(b.shape, b.dtype) for b in bufs],
        input_output_aliases={i: i for i in range(n)},
        scratch_shapes=[pltpu.SemaphoreType.DMA((n,)), pltpu.SemaphoreType.DMA((n,))], name=name)(*bufs)


def _rows_tile(n, pref=512):
    t = min(n, pref)
    while n % t or (t % 8 and t != n):
        t -= 1
    return t


def _rs_add(g, theirs, c_arr, payload, *, name):
    _, R, C = g.shape
    R2 = R // 2
    tr = _rows_tile(R2, 256)
    nb = R2 // tr

    def body(c_ref, g_ref, t_ref, o_ref):
        o_ref[...] = (g_ref[...].astype(F32) + t_ref[...].astype(F32)).astype(o_ref.dtype)

    blk = pl.BlockSpec((None, tr, C), lambda k, i, c: (k, i, 0))
    return pl.pallas_call(
        body,
        grid_spec=pltpu.PrefetchScalarGridSpec(
            num_scalar_prefetch=1, grid=(N_CHIPS, nb),
            in_specs=[pl.BlockSpec((None, tr, C), lambda k, i, c: (k, c[0] * nb + i, 0)), blk], out_specs=blk),
        out_shape=jax.ShapeDtypeStruct((N_CHIPS, R2, C), payload),
        compiler_params=_cparams(("arbitrary", "arbitrary")), name=name)(c_arr, g, theirs)


def _rs_sum(chip_sum, got, mc_arr, *, name):
    _, R2, C = chip_sum.shape
    tr = _rows_tile(R2, 256)

    def body(s_ref, own_ref, g0, g1, g2, g3, o_ref):
        me = s_ref[0]
        acc = jnp.zeros((tr, C), F32)
        for k, g_ref in enumerate((g0, g1, g2, g3)):
            acc = acc + jnp.where(me == k, own_ref[...], g_ref[...]).astype(F32)
        o_ref[...] = acc

    def got_spec(k):
        return pl.BlockSpec((None, tr, C), lambda i, s: (jnp.where(s[0] == k, (k + 1) % N_CHIPS, k), i, 0))

    return pl.pallas_call(
        body,
        grid_spec=pltpu.PrefetchScalarGridSpec(
            num_scalar_prefetch=1, grid=(R2 // tr,),
            in_specs=[pl.BlockSpec((None, tr, C), lambda i, s: (s[0], i, 0))] + [got_spec(k) for k in range(N_CHIPS)],
            out_specs=pl.BlockSpec((None, tr, C), lambda i, s: (s[1], i, 0))),
        out_shape=jax.ShapeDtypeStruct((2, R2, C), F32),
        compiler_params=_cparams(("arbitrary",)), name=name)(mc_arr, chip_sum, got, got, got, got)


def _adamw(w, m, v, gs, *, name):
    nl, _, R2, C = w.shape
    tr = _rows_tile(R2, 256)
    c1 = 1.0 / (1.0 - ADAM_B1 ** ADAM_STEP)
    c2 = 1.0 / (1.0 - ADAM_B2 ** ADAM_STEP)

    def body(w_ref, m_ref, v_ref, *refs):
        g_refs, (go_ref, d_ref, mo_ref, vo_ref) = refs[:nl], refs[nl:]
        l = pl.program_id(0)
        for ll in range(nl):
            @pl.when(l == ll)
            def _(ll=ll):
                gv = g_refs[ll][...]
                mn = ADAM_B1 * m_ref[...] + (1.0 - ADAM_B1) * gv
                vn = ADAM_B2 * v_ref[...] + (1.0 - ADAM_B2) * (gv * gv)
                go_ref[...] = gv
                mo_ref[...] = mn
                vo_ref[...] = vn
                d_ref[...] = -ADAM_LR * ((mn * c1) / (jnp.sqrt(vn * c2) + ADAM_EPS) + ADAM_WD * w_ref[...])

    def g_spec(ll):
        return pl.BlockSpec((None, tr, C), lambda l, h, i: (jnp.where(l == ll, h, 0), jnp.where(l == ll, i, 0), 0))

    ws = pl.BlockSpec((None, None, tr, C), lambda l, h, i: (l, h, i, 0))
    shp = jax.ShapeDtypeStruct(w.shape, F32)
    return pl.pallas_call(
        body, grid=(nl, 2, R2 // tr), in_specs=[ws, ws, ws] + [g_spec(ll) for ll in range(nl)], out_specs=[ws] * 4,
        out_shape=[shp] * 4, compiler_params=_cparams(("arbitrary", "arbitrary", "arbitrary")), name=name)(w, m, v, *gs)


class _GradReducer:
    def __init__(self, tag, payload):
        self.tag, self.payload = tag, payload
        self.me = (2 * lax.axis_index("x") + lax.axis_index("y")).astype(jnp.int32)
        self.c = lax.axis_index("c").astype(jnp.int32)

    def start(self, names, gs):
        theirs = _rs_sib(gs, name=f"rs_sib_{self.tag}")
        hs = [_rs_add(g, t, self.c.reshape(1), self.payload, name=f"rs_add_{n}") for n, g, t in zip(names, gs, theirs)]
        self.names = names
        self.ssem, self.rsem, self.hs, self.gots, token = _a2a_start(hs, name=f"rs_a2a_start_{self.tag}")
        return token

    def finish(self, after):
        hs, gots = _a2a_wait(self.ssem, self.rsem, self.hs, self.gots, after, name=f"rs_a2a_wait_{self.tag}")
        mc = jnp.stack([self.me, self.c])
        return {n: _rs_sum(h, g, mc, name=f"rs_sum_{n}") for n, h, g in zip(self.names, hs, gots)}


WEIGHTS = ["rel_bias", "mem_norm_g", "norm_mix_g", "w_in", "s5_lam_re", "s5_lam_im", "s5_log_dt", "s5_b_re", "s5_b_im",
           "s5_c_re", "s5_c_im", "s5_d", "s5_w_glu", "pool_w", "pool_scale", "conv_w_dw", "conv_b_dw", "conv_ln_g",
           "conv_ln_b", "conv_w_pw", "grp_norm_g", "w_out", "norm_x_g", "w_xq", "w_xk", "w_xv", "w_xo", "norm_mlp_g",
           "w_up", "w_down", "norm_final_g"]
SHARDED = {"w_in": "col", "s5_w_glu": "row", "conv_w_dw": "col", "conv_w_pw": "row", "w_out": "row", "w_xq": "row",
           "w_xk": "row", "w_xv": "row", "w_xo": "col", "w_up": "col", "w_down": "row"}
SMALL = [n for n in WEIGHTS if n not in SHARDED]
SMALL_ALIGN = N_CHIPS * 2 * 256 * 128


def _mat(w, kind):
    return w.reshape(w.shape[0] * w.shape[1], w.shape[2]) if kind == "row" else w


def _att_bias(rel_bias):
    bucket, valid = _att_tables()
    onehot = (jnp.asarray(bucket.reshape(-1))[:, None] == jnp.arange(REL_BUCKETS)[None, :]).astype(F32)
    b = jnp.dot(onehot, rel_bias, precision=lax.Precision.HIGHEST).reshape(bucket.shape + (rel_bias.shape[1],))
    return jnp.where(jnp.asarray(valid)[:, None], jnp.transpose(b, (0, 3, 1, 2)), NEG_INF)


def _rel_bias_grad(dbias):
    bucket, _ = _att_tables()
    nb, H = dbias.shape[0], dbias.shape[1]
    onehot = (jnp.asarray(bucket.reshape(-1))[:, None] == jnp.arange(REL_BUCKETS)[None, :]).astype(BF16)
    flat = jnp.transpose(dbias.reshape(nb, H, -1), (1, 0, 2)).reshape(H, -1)
    return _mm(flat, onehot, "nn", name="rel_bias_grad").T


def _layer_fwd(h, mem_n, bias, sp, bw, l):
    L, D = h.shape
    GW = D // N_MIXERS
    G = GW // S5_CH
    E = GW // ATT_HEADS
    sv = {"h": h}
    xn = _rms_fwd(h, sp["norm_mix_g"][l][None], name="norm_mix")
    proj = _mm(xn, bw["w_in"], "nn", name="proj_in")
    sv.update(xn=xn, proj=proj)
    disc, disc_vjp = jax.vjp(_s5_disc, sp["s5_lam_re"][l], sp["s5_lam_im"][l], sp["s5_log_dt"][l], sp["s5_b_re"][l], sp["s5_b_im"][l])
    ab_r, ab_i, bb_r, bb_i = disc
    s5 = dict(
        bdr=_bd(jnp.transpose(bb_r, (0, 2, 1))).astype(BF16), bdi=_bd(jnp.transpose(bb_i, (0, 2, 1))).astype(BF16),
        cdr=_bd(jnp.transpose(sp["s5_c_re"][l], (0, 2, 1))).astype(BF16), cdi=_bd(jnp.transpose(sp["s5_c_im"][l], (0, 2, 1))).astype(BF16),
        d=sp["s5_d"][l][None], wglu=bw["s5_w_glu"], ab=(ab_r.reshape(-1), ab_i.reshape(-1)), vjp=disc_vjp)
    pw, tr, ti = _cpow_tables(*s5["ab"])
    y_a, xr, xi = _s5_fwd(proj, s5["bdr"], s5["bdi"], pw, tr, ti, s5["cdr"], s5["cdi"], s5["d"], s5["wglu"], name="s5_fwd")
    sv.update(s5=s5, xr=xr, xi=xi, y_a=y_a)
    pool_s = sp["pool_scale"][l].reshape(len(POOL_WINDOWS), 1, -1)
    y_b = _pool_fwd(proj, sp["pool_w"][l], pool_s, name="pool_fwd")
    sv.update(y_b=y_b, pool_s=pool_s)
    hc = _conv1_fwd(proj, bw["conv_w_dw"], sp["conv_b_dw"][l][None], name="conv_dw_fwd")
    y_c = _conv2_fwd(hc, sp["conv_ln_g"][l][None], sp["conv_ln_b"][l][None], bw["conv_w_pw"], name="conv_pw_fwd")
    sv.update(hc=hc, y_c=y_c)
    y_d, lse_d = _datt_fwd(proj, bias, name="att_fwd")
    sv.update(y_d=y_d, lse_d=lse_d)
    yn = _grp_fwd([y_a, y_b, y_c, y_d], sp["grp_norm_g"][l][None], name="grp_fwd")
    h1 = _mm(yn, bw["w_out"], "nn", res=h, name="proj_out")
    sv.update(yn=yn, h1=h1)
    hx = _rms_fwd(h1, sp["norm_x_g"][l][None], name="norm_x")
    q_x = _mm(hx, bw["w_xq"], "nn", out_dtype=BF16, name="xq")
    k_x = _mm(mem_n, bw["w_xk"], "nn", out_dtype=BF16, name="xk")
    v_x = _mm(mem_n, bw["w_xv"], "nn", out_dtype=BF16, name="xv")
    o_x = _xatt_fwd(q_x, k_x, v_x, name="xatt_fwd")
    h2 = _mm(o_x, bw["w_xo"], "nn", res=h1, name="xo")
    sv.update(hx=hx, q_x=q_x, k_x=k_x, v_x=v_x, o_x=o_x, h2=h2)
    hm = _rms_fwd(h2, sp["norm_mlp_g"][l][None], name="norm_mlp")
    up, act = _mm(hm, bw["w_up"], "nn", epi="relu2", out_dtype=BF16, name="mlp_up")
    h3 = _mm(act, bw["w_down"], "nn", res=h2, name="mlp_down")
    sv.update(hm=hm, up=up, act=act)
    return h3, sv


def _stack_rows(g):
    return g.reshape(N_CHIPS, g.shape[0] // N_CHIPS, g.shape[1])


def _layer_bwd(dh3, d_memn, mem_n, bias, sp, bw, sv, l):
    L, D = dh3.shape
    GW = D // N_MIXERS
    G = GW // S5_CH
    E = GW // ATT_HEADS
    gs, gb = {}, {}
    d_up = _mm(dh3, bw["w_down"], "nt", epi="relu2_bwd", aux=sv["up"], out_dtype=BF16, name="mlp_down_dx")
    gb["w_down"] = _stack_rows(_mm(sv["act"], dh3, "tn", out_dtype=BF16, name="mlp_down_dw"))
    gb["w_up"] = _mm(sv["hm"], d_up, "tn", out_dtype=BF16, out_stack=N_CHIPS, name="mlp_up_dw")
    d_hm = _mm(d_up, bw["w_up"], "nt", name="mlp_up_dx")
    dh2, gs["norm_mlp_g"] = _rms_bwd(sv["h2"], sp["norm_mlp_g"][l][None], d_hm, dh3, name="norm_mlp_bwd")
    d_ox = _mm(dh2, bw["w_xo"], "nt", name="xo_dx")
    gb["w_xo"] = _mm(sv["o_x"], dh2, "tn", out_dtype=BF16, out_stack=N_CHIPS, name="xo_dw")
    dq_x, dk_x, dv_x = _xatt_bwd(sv["q_x"], sv["k_x"], sv["v_x"], d_ox, name="xatt_bwd")
    gb["w_xq"] = _stack_rows(_mm(sv["hx"], dq_x, "tn", out_dtype=BF16, name="xq_dw"))
    gb["w_xk"] = _stack_rows(_mm(mem_n, dk_x, "tn", out_dtype=BF16, name="xk_dw"))
    gb["w_xv"] = _stack_rows(_mm(mem_n, dv_x, "tn", out_dtype=BF16, name="xv_dw"))
    d_memn = _mm(dk_x, bw["w_xk"], "nt", res=d_memn, name="xk_dx")
    d_memn = _mm(dv_x, bw["w_xv"], "nt", res=d_memn, name="xv_dx")
    d_hx = _mm(dq_x, bw["w_xq"], "nt", name="xq_dx")
    dh1, gs["norm_x_g"] = _rms_bwd(sv["h1"], sp["norm_x_g"][l][None], d_hx, dh2, name="norm_x_bwd")
    d_yn = _mm(dh1, bw["w_out"], "nt", name="proj_out_dx")
    gb["w_out"] = _stack_rows(_mm(sv["yn"], dh1, "tn", out_dtype=BF16, name="proj_out_dw"))
    ys = [sv["y_a"], sv["y_b"], sv["y_c"], sv["y_d"]]
    dya, dyb, dyc, dyd, gs["grp_norm_g"] = _grp_bwd(ys, sp["grp_norm_g"][l][None], d_yn, name="grp_bwd")
    dq, dk, dv, dbias = _datt_bwd(sv["proj"], bias, sv["y_d"], sv["lse_d"], dyd, name="att_bwd")
    dhc, gs["conv_ln_g"], gs["conv_ln_b"], g_pw = _conv2_bwd(sv["hc"], sp["conv_ln_g"][l][None], sp["conv_ln_b"][l][None],
                                                               bw["conv_w_pw"], dyc, name="conv_pw_bwd")
    gb["conv_w_pw"] = _stack_rows(g_pw)
    dval, dgate, g_dw, gs["conv_b_dw"] = _conv1_bwd(sv["proj"], dhc, bw["conv_w_dw"], name="conv_dw_bwd")
    gb["conv_w_dw"] = jnp.transpose(g_dw.reshape(CONV_PAD, N_CHIPS, GW // N_CHIPS), (1, 0, 2))
    du_b, gs["pool_w"], g_ps = _pool_bwd(sv["proj"], dyb, sp["pool_w"][l], sv["pool_s"], name="pool_bwd")
    gs["pool_scale"] = g_ps.reshape(-1)
    s5 = sv["s5"]
    conj = (s5["ab"][0], -s5["ab"][1])
    pw, tr, ti = _cpow_tables(*conj)
    du_a, g_glu, g_d, dcdr, dcdi, dbdr, dbdi, dar, dai = _s5_bwd(
        sv["proj"], sv["xr"], sv["xi"], dya, s5["bdr"], s5["bdi"], pw, tr[::-1], ti[::-1], s5["cdr"], s5["cdi"], s5["d"], s5["wglu"],
        name="s5_bwd")
    gb["s5_w_glu"] = _stack_rows(g_glu)
    gs["s5_d"] = g_d.reshape(-1)
    gs["s5_c_re"] = jnp.transpose(_bd_extract(dcdr, G), (0, 2, 1))
    gs["s5_c_im"] = jnp.transpose(_bd_extract(dcdi, G), (0, 2, 1))
    d_bb_r = jnp.transpose(_bd_extract(dbdr, G), (0, 2, 1))
    d_bb_i = jnp.transpose(_bd_extract(dbdi, G), (0, 2, 1))
    d_ab_r = jnp.sum(dar, axis=0).reshape(G, S5_STATE)
    d_ab_i = jnp.sum(dai, axis=0).reshape(G, S5_STATE)
    gs["s5_lam_re"], gs["s5_lam_im"], gs["s5_log_dt"], gs["s5_b_re"], gs["s5_b_im"] = s5["vjp"]((d_ab_r, d_ab_i, d_bb_r, d_bb_i))
    d_proj = jnp.concatenate([du_a, du_b, dval, dgate, dq, dk, dv], axis=1)
    gb["w_in"] = _mm(sv["xn"], d_proj, "tn", out_dtype=BF16, out_stack=N_CHIPS, name="proj_in_dw")
    d_xn = _mm(d_proj, bw["w_in"], "nt", name="proj_in_dx")
    dh0, gs["norm_mix_g"] = _rms_bwd(sv["h"], sp["norm_mix_g"][l][None], d_xn, dh1, name="norm_mix_bwd")
    gs = {n: g.reshape(sp[n].shape[1:]) for n, g in gs.items()}
    return dh0, d_memn, dbias, gs, gb


def _device_step(x, mem, target, sp, get_big, on_grads):
    nl = sp["norm_mix_g"].shape[0]
    mem_n = _rms_fwd(mem, sp["mem_norm_g"][None], name="norm_mem")
    bias = _att_bias(sp["rel_bias"])
    h, saved, big = x, [], {n: [] for n in SHARDED}
    for l in range(nl):
        bw = get_big(l, h)
        for n in SHARDED:
            big[n].append(bw[n])
        h, sv = _layer_fwd(h, mem_n, bias, sp, bw, l)
        saved.append(sv)
    loss, dh, g_final = _loss_head(h, sp["norm_final_g"][None], target, name="loss_head")
    d_memn = jnp.zeros(mem.shape, F32)
    dbias = jnp.zeros(bias.shape, F32)
    sg = {n: [None] * nl for n in SMALL}
    for l in reversed(range(nl)):
        dh, d_memn, dbias_l, gs, gb = _layer_bwd(dh, d_memn, mem_n, bias, sp, {n: big[n][l] for n in SHARDED}, saved[l], l)
        dbias = dbias + dbias_l
        for n, g in gs.items():
            sg[n][l] = g
        dh = on_grads(l, gb, dh)
    small = {n: jnp.stack(g) for n, g in sg.items() if g[0] is not None}
    small["norm_final_g"] = g_final.reshape(-1)
    _, g_mem = _rms_bwd(mem, sp["mem_norm_g"][None], d_memn, None, name="norm_mem_bwd")
    small["mem_norm_g"] = g_mem.reshape(-1)
    small["rel_bias"] = _rel_bias_grad(dbias)
    return loss, dh, small


def _gather_big(shards):
    nl = shards["w_in"].shape[0]
    token = jnp.zeros((8, 128), F32)
    pending = []
    for l in range(nl):
        xs = [jnp.pad(shards[n][l], ((0, CONV_PAD - CONV_WIDTH), (0, 0))) if n == "conv_w_dw" else shards[n][l].astype(BF16)
              for n in SHARDED]
        ssem, rsem, xs, lands, token = _ag_start(xs, token, name=f"ag_start_l{l}")
        pending.append((ssem, rsem, xs, lands))

    def get_big(l, after):
        ssem, rsem, xs, lands = pending[l]
        xs, lands = _ag_wait(ssem, rsem, xs, lands, token if l == 0 else after, name=f"ag_wait_l{l}")
        bw = {}
        for (n, kind), x, land in zip(SHARDED.items(), xs, lands):
            full = _ag_finish(land, x, name=f"ag_finish_{n}").reshape(N_CHIPS, x.shape[0], x.shape[1])
            bw[n] = jnp.transpose(full, (1, 0, 2)).reshape(CONV_PAD, -1) if n == "conv_w_dw" else _mat(full, kind)
        return bw

    return get_big


def _pack_small(vals, extra=None):
    flat = [vals[n].reshape(-1).astype(F32) for n in SMALL]
    flat.append(jnp.zeros((1,), F32) if extra is None else extra.reshape(1))
    v = jnp.concatenate(flat)
    n_pad = -(-v.shape[0] // SMALL_ALIGN) * SMALL_ALIGN
    return jnp.pad(v, (0, n_pad - v.shape[0]))


def _unpack_small(flat, like):
    out, pos = {}, 0
    for n in SMALL:
        size = math.prod(like[n].shape)
        out[n] = flat[pos:pos + size].reshape(like[n].shape)
        pos += size
    return out, flat[pos]


def kernel(x, mem, rel_bias, mem_norm_g, norm_mix_g, w_in, s5_lam_re, s5_lam_im, s5_log_dt, s5_b_re, s5_b_im, s5_c_re, s5_c_im, s5_d, s5_w_glu, pool_w, pool_scale, conv_w_dw, conv_b_dw, conv_ln_g, conv_ln_b, conv_w_pw, grp_norm_g, w_out, norm_x_g, w_xq, w_xk, w_xv, w_xo, norm_mlp_g, w_up, w_down, norm_final_g, loss_target, m_rel_bias, m_mem_norm_g, m_norm_mix_g, m_w_in, m_s5_lam_re, m_s5_lam_im, m_s5_log_dt, m_s5_b_re, m_s5_b_im, m_s5_c_re, m_s5_c_im, m_s5_d, m_s5_w_glu, m_pool_w, m_pool_scale, m_conv_w_dw, m_conv_b_dw, m_conv_ln_g, m_conv_ln_b, m_conv_w_pw, m_grp_norm_g, m_w_out, m_norm_x_g, m_w_xq, m_w_xk, m_w_xv, m_w_xo, m_norm_mlp_g, m_w_up, m_w_down, m_norm_final_g, v_rel_bias, v_mem_norm_g, v_norm_mix_g, v_w_in, v_s5_lam_re, v_s5_lam_im, v_s5_log_dt, v_s5_b_re, v_s5_b_im, v_s5_c_re, v_s5_c_im, v_s5_d, v_s5_w_glu, v_pool_w, v_pool_scale, v_conv_w_dw, v_conv_b_dw, v_conv_ln_g, v_conv_ln_b, v_conv_w_pw, v_grp_norm_g, v_w_out, v_norm_x_g, v_w_xq, v_w_xk, v_w_xv, v_w_xo, v_norm_mlp_g, v_w_up, v_w_down, v_norm_final_g):
    env = dict(locals())
    w = {n: env[n] for n in WEIGHTS}
    m = {n: env["m_" + n] for n in WEIGHTS}
    v = {n: env["v_" + n] for n in WEIGHTS}
    sp = {n: w[n] for n in SMALL}
    get_big = _gather_big({n: w[n] for n in SHARDED})
    nl = w["w_in"].shape[0]
    names = list(SHARDED)
    reducers, reduced = {}, {}

    def on_grads(l, gb, dh):
        reducers[l] = _GradReducer(f"l{l}", BF16)
        token = reducers[l].start(names, [gb[n] for n in names])
        if l + 1 in reducers:
            reduced[l + 1] = reducers[l + 1].finish(dh)
        if l == 0:
            reduced[0] = reducers[0].finish(token)
        return dh + token[0, 0]

    loss, grad_x, g_small = _device_step(x[0], mem[0], loss_target[0], sp, get_big, on_grads)

    grad, delta, new_m, new_v = {}, {}, {}, {}
    filled = _sib_fill([reduced[l][n] for n in names for l in range(nl)], name="rs_fill")
    for i, n in enumerate(names):
        pad = ((0, 0), (0, CONV_PAD - CONV_WIDTH), (0, 0)) if n == "conv_w_dw" else None
        wmv = [jnp.pad(t[n], pad) if pad else t[n] for t in (w, m, v)]
        _, R, C = wmv[0].shape
        outs = _adamw(*[t.reshape(nl, 2, R // 2, C) for t in wmv], filled[i * nl:(i + 1) * nl], name=f"adamw_{n}")
        outs = [o.reshape(nl, R, C)[:, :w[n].shape[1]] for o in outs]
        grad[n], delta[n], new_m[n], new_v[n] = outs
    packed = _pack_small(g_small, loss).reshape(N_CHIPS, -1, 128)
    small = _GradReducer("small", F32)
    token = small.start(["small"], [packed])
    quarter = _sib_fill([small.finish(token)["small"]], name="rs_fill_small")[0]
    total = _ag4(quarter.reshape(-1, 128), name="ag_small").reshape(2, -1, 128)
    wmv = [_pack_small(t).reshape(1, 2, -1, 128) for t in (w, m, v)]
    outs = _adamw(*wmv, [total], name="adamw_small")
    loss_sum = None
    for o, dst in zip(outs, (grad, delta, new_m, new_v)):
        vals, last = _unpack_small(o.reshape(-1), sp)
        dst.update(vals)
        loss_sum = last if loss_sum is None else loss_sum
    return (loss_sum, grad_x[None], *[grad[n] for n in WEIGHTS], *[delta[n] for n in WEIGHTS],
            *[new_m[n] for n in WEIGHTS], *[new_v[n] for n in WEIGHTS])
```

```python
import functools
import math

import jax
import jax.numpy as jnp
import numpy as np
from jax import lax
from jax.experimental import pallas as pl
from jax.experimental.pallas import tpu as pltpu

F32 = jnp.float32
BF16 = jnp.bfloat16
MESH = pl.DeviceIdType.MESH

N_MIXERS = 4
S5_CH = 16
S5_STATE = 64
POOL_WINDOWS = (2, 4, 8, 16)
CONV_WIDTH = 31
CONV_PAD = 32
ATT_HEADS = 8
ATT_BLOCK = 128
DILATED_PATTERNS = ((128, 1), (512, 4), (2048, 16))
REL_BUCKETS = 32
REL_MAX_DIST = 2048
X_HEADS = 4
X_HEAD_DIM = 128
NORM_EPS = 1e-6
NEG_INF = -1e30
ADAM_LR, ADAM_B1, ADAM_B2, ADAM_EPS, ADAM_WD, ADAM_STEP = 0.001, 0.9, 0.999, 1e-08, 0.01, 10
N_CHIPS = 4
VMEM_LIMIT = 56 * 1024 * 1024


def _cparams(sem=None, vmem=None):
    return pltpu.CompilerParams(dimension_semantics=sem, vmem_limit_bytes=vmem)


def _tile(n, pref):
    if n <= pref:
        return n
    t = pref
    while t >= 128:
        if n % t == 0:
            return t
        t -= 128
    return n


def _spec(arr, tr, tc, rc):
    if arr.ndim == 2:
        return pl.BlockSpec((tr, tc), lambda *g: rc(*g))
    per = arr.shape[2] // tc
    assert arr.shape[2] % tc == 0

    def imap(*g):
        r, c = rc(*g)
        return (c // per, r, c % per)

    return pl.BlockSpec((None, tr, tc), imap)


def _lshape(arr):
    return arr.shape if arr.ndim == 2 else (arr.shape[1], arr.shape[0] * arr.shape[2])


def _mm(a, b, mode, *, name, out_dtype=F32, res=None, epi=None, aux=None, out_stack=None, tm=1024, tn=1024, tk=2048):
    la, lb = _lshape(a), _lshape(b)
    if mode == "nn":
        (M, K), (K2, N) = la, lb
    elif mode == "nt":
        (M, K), (N, K2) = la, lb
    else:
        (K, M), (K2, N) = la, lb
    assert K == K2, (la, lb, mode)
    lim = {"m": M, "n": N // out_stack if out_stack else N, "k": K}
    stacked = [(a, "m" if mode == "tn" else "k"), (b, "k" if mode == "nt" else "n"), (res, "n"), (aux, "n")]
    for arr, dim in stacked:
        if arr is not None and arr.ndim == 3:
            lim[dim] = math.gcd(lim[dim], arr.shape[2])
    tm, tn, tk = _tile(lim["m"], tm), _tile(lim["n"], tn), _tile(lim["k"], tk)
    nk = K // tk
    grid = (M // tm, N // tn, nk)
    a_spec = _spec(a, tk, tm, lambda i, j, k: (k, i)) if mode == "tn" else _spec(a, tm, tk, lambda i, j, k: (i, k))
    b_spec = _spec(b, tn, tk, lambda i, j, k: (j, k)) if mode == "nt" else _spec(b, tk, tn, lambda i, j, k: (k, j))
    dims = {"nn": ((1,), (0,)), "nt": ((1,), (1,)), "tn": ((0,), (0,))}[mode]
    ins, in_specs = [a, b], [a_spec, b_spec]
    for extra in (res, aux):
        if extra is not None:
            ins.append(extra)
            in_specs.append(_spec(extra, tm, tn, lambda i, j, k: (i, j)))
    if out_stack:
        o_shape = jax.ShapeDtypeStruct((out_stack, M, N // out_stack), out_dtype)
    else:
        o_shape = jax.ShapeDtypeStruct((M, N), out_dtype)
    o_spec = _spec(o_shape, tm, tn, lambda i, j, k: (i, j))
    n_out = 2 if epi == "relu2" else 1
    has_res, has_aux = res is not None, aux is not None

    def body(*refs):
        a_ref, b_ref = refs[0], refs[1]
        pos = 2
        res_ref = aux_ref = None
        if has_res:
            res_ref = refs[pos]
            pos += 1
        if has_aux:
            aux_ref = refs[pos]
            pos += 1
        outs = refs[pos:pos + n_out]
        part = lax.dot_general(a_ref[...].astype(BF16), b_ref[...].astype(BF16), (dims, ((), ())), preferred_element_type=F32)
        if nk > 1:
            acc_ref = refs[pos + n_out]
            k = pl.program_id(2)

            @pl.when(k == 0)
            def _():
                acc_ref[...] = part

            @pl.when(k > 0)
            def _():
                acc_ref[...] += part

        @pl.when(pl.program_id(2) == nk - 1)
        def _():
            o = acc_ref[...] if nk > 1 else part
            if has_res:
                o = o + res_ref[...].astype(F32)
            if epi == "relu2":
                outs[0][...] = o.astype(outs[0].dtype)
                r = jnp.maximum(o, 0.0)
                outs[1][...] = (r * r).astype(outs[1].dtype)
            elif epi == "relu2_bwd":
                outs[0][...] = (o * (2.0 * jnp.maximum(aux_ref[...].astype(F32), 0.0))).astype(outs[0].dtype)
            else:
                outs[0][...] = o.astype(outs[0].dtype)

    out = pl.pallas_call(
        body, grid=grid, in_specs=in_specs,
        out_specs=[o_spec] * n_out if n_out > 1 else o_spec,
        out_shape=[o_shape] * n_out if n_out > 1 else o_shape,
        scratch_shapes=[pltpu.VMEM((tm, tn), F32)] if nk > 1 else [],
        compiler_params=_cparams(("parallel", "parallel", "arbitrary"), VMEM_LIMIT), name=name,
    )(*ins)
    return out


def _rms_fwd(x, g, *, name, out_dtype=BF16):
    L, D = x.shape
    tr = _tile(L, 256)

    def body(x_ref, g_ref, o_ref):
        xv = x_ref[...]
        r = lax.rsqrt(jnp.mean(xv * xv, axis=-1, keepdims=True) + NORM_EPS)
        o_ref[...] = (xv * r * g_ref[...]).astype(o_ref.dtype)

    return pl.pallas_call(
        body, grid=(L // tr,),
        in_specs=[pl.BlockSpec((tr, D), lambda i: (i, 0)), pl.BlockSpec((1, D), lambda i: (0, 0))],
        out_specs=pl.BlockSpec((tr, D), lambda i: (i, 0)),
        out_shape=jax.ShapeDtypeStruct((L, D), out_dtype),
        compiler_params=_cparams(("parallel",)), name=name)(x, g)


def _rms_bwd(x, g, dy, dres, *, name):
    L, D = x.shape
    tr = _tile(L, 256)
    has_res = dres is not None

    def body(*refs):
        x_ref, g_ref, dy_ref = refs[:3]
        dres_ref = refs[3] if has_res else None
        dx_ref, dg_ref = refs[-2:]
        xv = x_ref[...]
        dyv = dy_ref[...].astype(F32)
        r = lax.rsqrt(jnp.mean(xv * xv, axis=-1, keepdims=True) + NORM_EPS)
        xh = xv * r
        dyg = dyv * g_ref[...]
        dx = r * (dyg - xh * jnp.mean(dyg * xh, axis=-1, keepdims=True))
        if has_res:
            dx = dx + dres_ref[...]
        dx_ref[...] = dx

        @pl.when(pl.program_id(0) == 0)
        def _():
            dg_ref[...] = jnp.zeros_like(dg_ref)

        dg_ref[...] += jnp.sum(dyv * xh, axis=0, keepdims=True)

    row = pl.BlockSpec((tr, D), lambda i: (i, 0))
    vec = pl.BlockSpec((1, D), lambda i: (0, 0))
    ins = [x, g, dy] + ([dres] if has_res else [])
    return pl.pallas_call(
        body, grid=(L // tr,), in_specs=[row, vec, row] + ([row] if has_res else []),
        out_specs=[row, vec],
        out_shape=[jax.ShapeDtypeStruct((L, D), F32), jax.ShapeDtypeStruct((1, D), F32)],
        compiler_params=_cparams(("arbitrary",)), name=name)(*ins)


def _loss_head(h, g, target, *, name):
    L, D = h.shape
    tr = _tile(L, 256)

    def body(x_ref, g_ref, t_ref, loss_ref, dx_ref, dg_ref):
        xv = x_ref[...]
        r = lax.rsqrt(jnp.mean(xv * xv, axis=-1, keepdims=True) + NORM_EPS)
        xh = xv * r
        err = xh * g_ref[...] - t_ref[...]
        dyv = err * (1.0 / D)
        dyg = dyv * g_ref[...]
        dx_ref[...] = r * (dyg - xh * jnp.mean(dyg * xh, axis=-1, keepdims=True))

        @pl.when(pl.program_id(0) == 0)
        def _():
            dg_ref[...] = jnp.zeros_like(dg_ref)
            loss_ref[...] = jnp.zeros_like(loss_ref)

        dg_ref[...] += jnp.sum(dyv * xh, axis=0, keepdims=True)
        loss_ref[...] += 0.5 * jnp.sum(jnp.sum(err * err, axis=1, keepdims=True), axis=0, keepdims=True) * (1.0 / D)

    row = pl.BlockSpec((tr, D), lambda i: (i, 0))
    vec = pl.BlockSpec((1, D), lambda i: (0, 0))
    return pl.pallas_call(
        body, grid=(L // tr,), in_specs=[row, vec, row],
        out_specs=[pl.BlockSpec((1, 1), lambda i: (0, 0)), row, vec],
        out_shape=[jax.ShapeDtypeStruct((1, 1), F32), jax.ShapeDtypeStruct((L, D), F32), jax.ShapeDtypeStruct((1, D), F32)],
        compiler_params=_cparams(("arbitrary",)), name=name)(h, g, target)


S5_TL = 256
S5_LW = 512


def _dot(a, b, dims):
    return lax.dot_general(a, b, (dims, ((), ())), preferred_element_type=F32)


_NN, _NT, _TN = ((1,), (0,)), ((1,), (1,)), ((0,), (0,))


def _gelu_and_grad(y):
    k = math.sqrt(2.0 / math.pi)
    y2 = y * y
    th = jnp.tanh(k * (y + 0.044715 * y * y2))
    g = 0.5 * y * (1.0 + th)
    gp = 0.5 * (1.0 + th) + 0.5 * y * (1.0 - th * th) * k * (1.0 + 3.0 * 0.044715 * y2)
    return g, gp


def _scan_tiles(re_s, im_s, ls, lw, pw_ref, tr_ref, ti_ref, carry_s, nt, reverse, extra=None):
    row = lax.broadcasted_iota(jnp.int32, (8, lw), 0)
    a = [pw_ref[k:k + 1, ls] for k in range(6)]
    tr_t, ti_t = tr_ref[:, ls], ti_ref[:, ls]
    edge = 0 if reverse else 7

    def tile(jj, carry):
        cr, ci, acc = carry
        j = (nt - 1 - jj) if reverse else jj
        off = pl.multiple_of(j * 8, 8)
        sr, si = re_s[pl.ds(off, 8), ls], im_s[pl.ds(off, 8), ls]
        for n, k in enumerate((1, 2, 4)):
            akr, aki = a[2 * n], a[2 * n + 1]
            if reverse:
                keep, sh = row < 8 - k, 8 - k
            else:
                keep, sh = row >= k, k
            shr = jnp.where(keep, pltpu.roll(sr, sh, 0), 0.0)
            shi = jnp.where(keep, pltpu.roll(si, sh, 0), 0.0)
            sr, si = sr + akr * shr - aki * shi, si + akr * shi + aki * shr
        xr = sr + tr_t * cr - ti_t * ci
        xi = si + tr_t * ci + ti_t * cr
        re_s[pl.ds(off, 8), ls] = xr
        im_s[pl.ds(off, 8), ls] = xi
        if extra is not None:
            acc = extra(off, xr, xi, acc, row)
        return xr[edge:edge + 1, :], xi[edge:edge + 1, :], acc

    acc0 = (jnp.zeros((8, lw), F32), jnp.zeros((8, lw), F32)) if extra is not None else 0
    cr, ci, acc = lax.fori_loop(0, nt, tile, (carry_s[0:1, ls], carry_s[1:2, ls], acc0))
    carry_s[0:1, ls] = cr
    carry_s[1:2, ls] = ci
    return acc


def _s5_fwd(proj, bdr, bdi, pw, tr, ti, cdr, cdi, dskip, wglu, *, name):
    L = proj.shape[0]
    GW, S = bdr.shape
    TL = _tile(L, S5_TL)
    lw = min(S, S5_LW)

    def body(u_ref, bdr_ref, bdi_ref, pw_ref, tr_ref, ti_ref, cdr_ref, cdi_ref, d_ref, w_ref,
             y_ref, xr_ref, xi_ref, re_s, im_s, carry_s):
        @pl.when(pl.program_id(0) == 0)
        def _():
            carry_s[...] = jnp.zeros_like(carry_s)

        u = u_ref[...]
        ub = u.astype(BF16)
        re_s[...] = _dot(ub, bdr_ref[...], _NN)
        im_s[...] = _dot(ub, bdi_ref[...], _NN)
        for lc in range(S // lw):
            _scan_tiles(re_s, im_s, slice(lc * lw, (lc + 1) * lw), lw, pw_ref, tr_ref, ti_ref, carry_s, TL // 8, False)
        xrb, xib = re_s[...].astype(BF16), im_s[...].astype(BF16)
        xr_ref[...] = xrb
        xi_ref[...] = xib
        y = _dot(xrb, cdr_ref[...], _NN) - _dot(xib, cdi_ref[...], _NN) + d_ref[...] * u
        g, _ = _gelu_and_grad(y)
        z = _dot(g.astype(BF16), w_ref[...], _NN)
        y_ref[...] = g * jax.nn.sigmoid(z)

    full = lambda arr: pl.BlockSpec(arr.shape, lambda i: (0,) * arr.ndim)
    return pl.pallas_call(
        body, grid=(L // TL,),
        in_specs=[pl.BlockSpec((TL, GW), lambda i: (i, 0))] + [full(t) for t in (bdr, bdi, pw, tr, ti, cdr, cdi, dskip, wglu)],
        out_specs=[pl.BlockSpec((TL, GW), lambda i: (i, 0)), pl.BlockSpec((TL, S), lambda i: (i, 0)), pl.BlockSpec((TL, S), lambda i: (i, 0))],
        out_shape=[jax.ShapeDtypeStruct((L, GW), F32), jax.ShapeDtypeStruct((L, S), BF16), jax.ShapeDtypeStruct((L, S), BF16)],
        scratch_shapes=[pltpu.VMEM((TL, S), F32), pltpu.VMEM((TL, S), F32), pltpu.VMEM((8, S), F32)],
        compiler_params=_cparams(("arbitrary",), VMEM_LIMIT), name=name,
    )(proj, bdr, bdi, pw, tr, ti, cdr, cdi, dskip, wglu)


def _s5_bwd(proj, xr, xi, dout, bdr, bdi, pwc, trc, tic, cdr, cdi, dskip, wglu, *, name):
    L = proj.shape[0]
    GW, S = bdr.shape
    TL = _tile(L, S5_TL)
    nc = L // TL
    lw = min(S, S5_LW)

    def body(u_ref, xr_ref, xi_ref, xrp_ref, xip_ref, do_ref, bdr_ref, bdi_ref, pw_ref, tr_ref, ti_ref, cdr_ref, cdi_ref,
             d_ref, w_ref, du_ref, dw_ref, dd_ref, dcdr_ref, dcdi_ref, dbdr_ref, dbdi_ref, dar_ref, dai_ref,
             gr_s, gi_s, xfr, xfi, carry_s):
        i = pl.program_id(0)

        @pl.when(i == 0)
        def _():
            carry_s[...] = jnp.zeros_like(carry_s)
            for r in (dw_ref, dd_ref, dcdr_ref, dcdi_ref, dbdr_ref, dbdi_ref, dar_ref, dai_ref):
                r[...] = jnp.zeros_like(r)

        u = u_ref[...]
        ub = u.astype(BF16)
        xrb, xib = xr_ref[...], xi_ref[...]
        y = _dot(xrb, cdr_ref[...], _NN) - _dot(xib, cdi_ref[...], _NN) + d_ref[...] * u
        g, gp = _gelu_and_grad(y)
        gb = g.astype(BF16)
        sg = jax.nn.sigmoid(_dot(gb, w_ref[...], _NN))
        dout = do_ref[...]
        dz = (dout * g * sg * (1.0 - sg)).astype(BF16)
        dg = dout * sg + _dot(dz, w_ref[...], _NT)
        dw_ref[...] += _dot(gb, dz, _TN)
        dy = dg * gp
        dyb = dy.astype(BF16)
        dd_ref[...] += jnp.sum(dy * u, axis=0, keepdims=True)
        gr_s[...] = _dot(dyb, cdr_ref[...], _NT)
        gi_s[...] = -_dot(dyb, cdi_ref[...], _NT)
        dcdr_ref[...] += _dot(xrb, dyb, _TN)
        dcdi_ref[...] -= _dot(xib, dyb, _TN)
        live = (i < nc - 1).astype(F32)
        xfr[0:8, :] = xrp_ref[...].astype(F32) * live
        xfi[0:8, :] = xip_ref[...].astype(F32) * live
        xfr[8:, :] = xrb.astype(F32)
        xfi[8:, :] = xib.astype(F32)
        for lc in range(S // lw):
            ls = slice(lc * lw, (lc + 1) * lw)

            def extra(off, lr, li, acc, row, ls=ls):
                cur_r, cur_i = xfr[pl.ds(off + 8, 8), ls], xfi[pl.ds(off + 8, 8), ls]
                prv_r, prv_i = xfr[pl.ds(off, 8), ls], xfi[pl.ds(off, 8), ls]
                xpr = jnp.where(row >= 1, pltpu.roll(cur_r, 1, 0), prv_r[7:8, :])
                xpi = jnp.where(row >= 1, pltpu.roll(cur_i, 1, 0), prv_i[7:8, :])
                return acc[0] + lr * xpr + li * xpi, acc[1] - lr * xpi + li * xpr

            acc = _scan_tiles(gr_s, gi_s, ls, lw, pw_ref, tr_ref, ti_ref, carry_s, TL // 8, True, extra)
            dar_ref[:, ls] += acc[0]
            dai_ref[:, ls] += acc[1]
        lrb, lib = gr_s[...].astype(BF16), gi_s[...].astype(BF16)
        du_ref[...] = dy * d_ref[...] + _dot(lrb, bdr_ref[...], _NT) + _dot(lib, bdi_ref[...], _NT)
        dbdr_ref[...] += _dot(ub, lrb, _TN)
        dbdi_ref[...] += _dot(ub, lib, _TN)

    rev = lambda i: nc - 1 - i
    full = lambda arr: pl.BlockSpec(arr.shape, lambda i: (0,) * arr.ndim)
    chunk = lambda w: pl.BlockSpec((TL, w), lambda i: (rev(i), 0))
    prev8 = pl.BlockSpec((8, S), lambda i: (jnp.maximum(rev(i) * (TL // 8) - 1, 0), 0))
    acc_shapes = [(GW, GW), (1, GW), (S, GW), (S, GW), (GW, S), (GW, S), (8, S), (8, S)]
    return pl.pallas_call(
        body, grid=(nc,),
        in_specs=[chunk(GW), chunk(S), chunk(S), prev8, prev8, chunk(GW)] + [full(t) for t in (bdr, bdi, pwc, trc, tic, cdr, cdi, dskip, wglu)],
        out_specs=[chunk(GW)] + [pl.BlockSpec(s, lambda i: (0, 0)) for s in acc_shapes],
        out_shape=[jax.ShapeDtypeStruct((L, GW), F32)] + [jax.ShapeDtypeStruct(s, F32) for s in acc_shapes],
        scratch_shapes=[pltpu.VMEM((TL, S), F32), pltpu.VMEM((TL, S), F32), pltpu.VMEM((TL + 8, S), F32),
                        pltpu.VMEM((TL + 8, S), F32), pltpu.VMEM((8, S), F32)],
        compiler_params=_cparams(("arbitrary",), VMEM_LIMIT), name=name,
    )(proj, xr, xi, xr, xi, dout, bdr, bdi, pwc, trc, tic, cdr, cdi, dskip, wglu)


def _cpow_tables(ar, ai):
    def cm(a, b):
        return a[0] * b[0] - a[1] * b[1], a[0] * b[1] + a[1] * b[0]
    p = [(ar, ai)]
    for _ in range(7):
        p.append(cm(p[-1], p[0]))
    z = jnp.zeros_like(ar)
    pw = jnp.stack([p[0][0], p[0][1], p[1][0], p[1][1], p[3][0], p[3][1], z, z])
    return pw, jnp.stack([q[0] for q in p]), jnp.stack([q[1] for q in p])


def _s5_disc(lam_re, lam_im, log_dt, b_re, b_im):
    dt = jnp.exp(log_dt)[:, None]
    mag = jnp.exp(lam_re * dt)
    ab_r, ab_i = mag * jnp.cos(lam_im * dt), mag * jnp.sin(lam_im * dt)
    den = lam_re * lam_re + lam_im * lam_im
    nr, ni = ab_r - 1.0, ab_i
    f_r = ((nr * lam_re + ni * lam_im) / den)[..., None]
    f_i = ((ni * lam_re - nr * lam_im) / den)[..., None]
    return ab_r, ab_i, f_r * b_re - f_i * b_im, f_r * b_im + f_i * b_re


def _bd(t):
    G, A, B = t.shape
    return (t[:, :, None, :] * jnp.eye(G, dtype=t.dtype)[:, None, :, None]).reshape(G * A, G * B)


def _bd_extract(m, G):
    A, B = m.shape[0] // G, m.shape[1] // G
    return jnp.sum(m.reshape(G, A, G, B) * jnp.eye(G, dtype=m.dtype)[:, None, :, None], axis=2)


POOL_TQ = 256


def _band(shape, row0, col0, win, transpose):
    r = lax.broadcasted_iota(jnp.int32, shape, 0) + row0
    c = lax.broadcasted_iota(jnp.int32, shape, 1) + col0
    d = (c - r) if transpose else (r - c)
    return ((d >= 0) & (d < win)).astype(BF16)


def _band_dot(band, v):
    hi = v.astype(BF16)
    lo = (v - hi.astype(F32)).astype(BF16)
    return _dot(band, hi, _NN) + _dot(band, lo, _NN)


def _pool_p(u_prev, u_cur, i, win, tq):
    t0 = i * tq
    cnt = jnp.minimum(lax.broadcasted_iota(jnp.int32, (tq, 1), 0) + t0 + 1, win).astype(F32)
    live = (i > 0).astype(F32)
    acc = _band_dot(_band((tq, tq), t0, t0, win, False), u_cur)
    acc += _band_dot(_band((tq, tq), t0, t0 - tq, win, False), u_prev * live)
    return acc / cnt - u_cur


def _pool_fwd(proj, pool_w, pool_scale3, *, name):
    L = proj.shape[0]
    nw, PC, _ = pool_w.shape
    tq = _tile(L, POOL_TQ)

    def body(up_ref, uc_ref, w_ref, s_ref, y_ref):
        g, i = pl.program_id(0), pl.program_id(1)
        win = jnp.left_shift(2, g)
        p = _pool_p(up_ref[...], uc_ref[...], i, win, tq)
        y_ref[...] = _dot(p.astype(BF16), w_ref[...].astype(BF16), _NN) * s_ref[...]

    return pl.pallas_call(
        body, grid=(nw, L // tq),
        in_specs=[pl.BlockSpec((tq, PC), lambda g, i: (jnp.maximum(i - 1, 0), nw + g)),
                  pl.BlockSpec((tq, PC), lambda g, i: (i, nw + g)),
                  pl.BlockSpec((None, PC, PC), lambda g, i: (g, 0, 0)),
                  pl.BlockSpec((None, 1, PC), lambda g, i: (g, 0, 0))],
        out_specs=pl.BlockSpec((tq, PC), lambda g, i: (i, g)),
        out_shape=jax.ShapeDtypeStruct((L, nw * PC), F32),
        compiler_params=_cparams(("parallel", "parallel")), name=name)(proj, proj, pool_w, pool_scale3)


def _pool_bwd(proj, dy, pool_w, pool_scale3, *, name):
    L = proj.shape[0]
    nw, PC, _ = pool_w.shape
    tq = _tile(L, POOL_TQ)
    nq = L // tq

    def body(up_ref, uc_ref, dyc_ref, dyn_ref, w_ref, s_ref, du_ref, dw_ref, ds_ref):
        g, i = pl.program_id(0), pl.program_id(1)
        win = jnp.left_shift(2, g)

        @pl.when(i == 0)
        def _():
            dw_ref[...] = jnp.zeros_like(dw_ref)
            ds_ref[...] = jnp.zeros_like(ds_ref)

        wb = w_ref[...].astype(BF16)
        p = _pool_p(up_ref[...], uc_ref[...], i, win, tq).astype(BF16)
        dyc = dyc_ref[...]
        ds_ref[...] += jnp.sum(dyc * _dot(p, wb, _NN), axis=0, keepdims=True)
        dys = (dyc * s_ref[...]).astype(BF16)
        dw_ref[...] += _dot(p, dys, _TN)
        t0 = i * tq
        rows = lax.broadcasted_iota(jnp.int32, (tq, 1), 0)
        dp_c = _dot(dys, wb, _NT)
        q_c = dp_c / jnp.minimum(rows + t0 + 1, win).astype(F32)
        live = (i < nq - 1).astype(F32)
        dp_n = _dot((dyn_ref[...] * s_ref[...] * live).astype(BF16), wb, _NT)
        q_n = dp_n / jnp.minimum(rows + t0 + tq + 1, win).astype(F32)
        du = _band_dot(_band((tq, tq), t0, t0, win, True), q_c) + _band_dot(_band((tq, tq), t0, t0 + tq, win, True), q_n)
        du_ref[...] = du - dp_c

    return pl.pallas_call(
        body, grid=(nw, nq),
        in_specs=[pl.BlockSpec((tq, PC), lambda g, i: (jnp.maximum(i - 1, 0), nw + g)),
                  pl.BlockSpec((tq, PC), lambda g, i: (i, nw + g)),
                  pl.BlockSpec((tq, PC), lambda g, i: (i, g)),
                  pl.BlockSpec((tq, PC), lambda g, i: (jnp.minimum(i + 1, nq - 1), g)),
                  pl.BlockSpec((None, PC, PC), lambda g, i: (g, 0, 0)),
                  pl.BlockSpec((None, 1, PC), lambda g, i: (g, 0, 0))],
        out_specs=[pl.BlockSpec((tq, PC), lambda g, i: (i, g)),
                   pl.BlockSpec((None, PC, PC), lambda g, i: (g, 0, 0)),
                   pl.BlockSpec((None, 1, PC), lambda g, i: (g, 0, 0))],
        out_shape=[jax.ShapeDtypeStruct((L, nw * PC), F32), jax.ShapeDtypeStruct((nw, PC, PC), F32),
                   jax.ShapeDtypeStruct((nw, 1, PC), F32)],
        compiler_params=_cparams(("parallel", "arbitrary")), name=name)(proj, proj, dy, dy, pool_w, pool_scale3)


CONV_RC = 256


def _conv1_fwd(proj, w_dw, b_dw, *, name):
    L = proj.shape[0]
    GW = w_dw.shape[1]
    CB = min(GW, 128)
    nb = GW // CB
    RC = _tile(L, CONV_RC)
    n = RC + CONV_PAD

    def body(val_ref, gate_ref, w_ref, b_ref, o_ref, hp):
        hp[0:CONV_PAD, :] = jnp.zeros((CONV_PAD, CB), F32)
        hp[CONV_PAD:, :] = val_ref[...] * jax.nn.sigmoid(gate_ref[...])
        wv = w_ref[...]

        def chunk(ci, carry):
            c0 = pl.multiple_of(ci * RC, 8)
            win = hp[pl.ds(c0, n), :]
            acc = jnp.zeros((RC, CB), F32) + b_ref[...]
            for k in range(CONV_WIDTH):
                acc = acc + wv[k:k + 1, :] * pltpu.roll(win, n - (k + 2), 0)[0:RC]
            o_ref[pl.ds(c0, RC), :] = acc
            return carry

        lax.fori_loop(0, L // RC, chunk, 0)

    col = lambda off: pl.BlockSpec((L, CB), lambda c: (0, off * nb + c))
    return pl.pallas_call(
        body, grid=(nb,),
        in_specs=[col(2), col(3), pl.BlockSpec((CONV_PAD, CB), lambda c: (0, c)), pl.BlockSpec((1, CB), lambda c: (0, c))],
        out_specs=pl.BlockSpec((L, CB), lambda c: (0, c)),
        out_shape=jax.ShapeDtypeStruct((L, GW), F32),
        scratch_shapes=[pltpu.VMEM((L + CONV_PAD, CB), F32)],
        compiler_params=_cparams(("parallel",)), name=name)(proj, proj, w_dw, b_dw)


def _conv1_bwd(proj, dhc, w_dw, *, name):
    L = proj.shape[0]
    GW = w_dw.shape[1]
    CB = min(GW, 128)
    nb = GW // CB
    RC = _tile(L, CONV_RC)
    n = RC + CONV_PAD

    def body(val_ref, gate_ref, d_ref, w_ref, dval_ref, dgate_ref, dw_ref, db_ref, hp, dp):
        val = val_ref[...]
        sg = jax.nn.sigmoid(gate_ref[...])
        hp[0:CONV_PAD, :] = jnp.zeros((CONV_PAD, CB), F32)
        hp[CONV_PAD:, :] = val * sg
        dp[0:L, :] = d_ref[...]
        dp[L:, :] = jnp.zeros((CONV_PAD, CB), F32)
        dw_ref[...] = jnp.zeros_like(dw_ref)
        db_ref[...] = jnp.sum(d_ref[...], axis=0, keepdims=True)
        wv = w_ref[...]

        def chunk(ci, carry):
            c0 = pl.multiple_of(ci * RC, 8)
            win = hp[pl.ds(c0, n), :]
            dwin = dp[pl.ds(c0, n), :]
            d = dwin[0:RC]
            dh = jnp.zeros((RC, CB), F32)
            for k in range(CONV_WIDTH):
                dw_ref[k:k + 1, :] += jnp.sum(d * pltpu.roll(win, n - (k + 2), 0)[0:RC], axis=0, keepdims=True)
                sh = CONV_WIDTH - 1 - k
                dh = dh + wv[k:k + 1, :] * (pltpu.roll(dwin, n - sh, 0)[0:RC] if sh else d)
            v, s = val_ref[pl.ds(c0, RC), :], jax.nn.sigmoid(gate_ref[pl.ds(c0, RC), :])
            dval_ref[pl.ds(c0, RC), :] = dh * s
            dgate_ref[pl.ds(c0, RC), :] = dh * v * s * (1.0 - s)
            return carry

        lax.fori_loop(0, L // RC, chunk, 0)

    col = lambda off: pl.BlockSpec((L, CB), lambda c: (0, off * nb + c))
    own = pl.BlockSpec((L, CB), lambda c: (0, c))
    return pl.pallas_call(
        body, grid=(nb,),
        in_specs=[col(2), col(3), own, pl.BlockSpec((CONV_PAD, CB), lambda c: (0, c))],
        out_specs=[own, own, pl.BlockSpec((CONV_PAD, CB), lambda c: (0, c)), pl.BlockSpec((1, CB), lambda c: (0, c))],
        out_shape=[jax.ShapeDtypeStruct((L, GW), F32), jax.ShapeDtypeStruct((L, GW), F32),
                   jax.ShapeDtypeStruct((CONV_PAD, GW), F32), jax.ShapeDtypeStruct((1, GW), F32)],
        scratch_shapes=[pltpu.VMEM((L + CONV_PAD, CB), F32), pltpu.VMEM((L + CONV_PAD, CB), F32)],
        compiler_params=_cparams(("parallel",)), name=name)(proj, proj, dhc, w_dw)


def _ln_silu(hc, g, b):
    mu = jnp.mean(hc, axis=-1, keepdims=True)
    xc = hc - mu
    r = lax.rsqrt(jnp.mean(xc * xc, axis=-1, keepdims=True) + NORM_EPS)
    xh = xc * r
    a = xh * g + b
    sg = jax.nn.sigmoid(a)
    return xh, r, a, sg


def _conv2_fwd(hc, ln_g, ln_b, w_pw, *, name):
    L, GW = hc.shape
    tr = _tile(L, 256)

    def body(h_ref, g_ref, b_ref, w_ref, y_ref):
        _, _, a, sg = _ln_silu(h_ref[...], g_ref[...], b_ref[...])
        y_ref[...] = _dot((a * sg).astype(BF16), w_ref[...], _NN)

    row = pl.BlockSpec((tr, GW), lambda i: (i, 0))
    vec = pl.BlockSpec((1, GW), lambda i: (0, 0))
    return pl.pallas_call(
        body, grid=(L // tr,), in_specs=[row, vec, vec, pl.BlockSpec((GW, GW), lambda i: (0, 0))],
        out_specs=row, out_shape=jax.ShapeDtypeStruct((L, GW), F32),
        compiler_params=_cparams(("parallel",)), name=name)(hc, ln_g, ln_b, w_pw)


def _conv2_bwd(hc, ln_g, ln_b, w_pw, dy, *, name):
    L, GW = hc.shape
    tr = _tile(L, 256)

    def body(h_ref, g_ref, b_ref, w_ref, dy_ref, dh_ref, dg_ref, db_ref, dw_ref):
        @pl.when(pl.program_id(0) == 0)
        def _():
            for r in (dg_ref, db_ref, dw_ref):
                r[...] = jnp.zeros_like(r)

        xh, r, a, sg = _ln_silu(h_ref[...], g_ref[...], b_ref[...])
        dyb = dy_ref[...].astype(BF16)
        dw_ref[...] += _dot((a * sg).astype(BF16), dyb, _TN)
        da = _dot(dyb, w_ref[...], _NT) * (sg + a * sg * (1.0 - sg))
        dg_ref[...] += jnp.sum(da * xh, axis=0, keepdims=True)
        db_ref[...] += jnp.sum(da, axis=0, keepdims=True)
        dxh = da * g_ref[...]
        dh_ref[...] = r * (dxh - jnp.mean(dxh, axis=-1, keepdims=True) - xh * jnp.mean(dxh * xh, axis=-1, keepdims=True))

    row = pl.BlockSpec((tr, GW), lambda i: (i, 0))
    vec = pl.BlockSpec((1, GW), lambda i: (0, 0))
    mat = pl.BlockSpec((GW, GW), lambda i: (0, 0))
    return pl.pallas_call(
        body, grid=(L // tr,), in_specs=[row, vec, vec, mat, row],
        out_specs=[row, vec, vec, mat],
        out_shape=[jax.ShapeDtypeStruct((L, GW), F32), jax.ShapeDtypeStruct((1, GW), F32), jax.ShapeDtypeStruct((1, GW), F32),
                   jax.ShapeDtypeStruct((GW, GW), F32)],
        compiler_params=_cparams(("arbitrary",)), name=name)(hc, ln_g, ln_b, w_pw, dy)


def _grp_fwd(ys, g, *, name):
    L, GW = ys[0].shape
    tr = _tile(L, 256)

    def body(*refs):
        g_ref, o_ref = refs[4], refs[5]
        for m in range(N_MIXERS):
            yv = refs[m][...]
            r = lax.rsqrt(jnp.mean(yv * yv, axis=-1, keepdims=True) + NORM_EPS)
            o_ref[:, m * GW:(m + 1) * GW] = (yv * r * g_ref[:, m * GW:(m + 1) * GW]).astype(o_ref.dtype)

    row = pl.BlockSpec((tr, GW), lambda i: (i, 0))
    return pl.pallas_call(
        body, grid=(L // tr,), in_specs=[row] * 4 + [pl.BlockSpec((1, N_MIXERS * GW), lambda i: (0, 0))],
        out_specs=pl.BlockSpec((tr, N_MIXERS * GW), lambda i: (i, 0)),
        out_shape=jax.ShapeDtypeStruct((L, N_MIXERS * GW), BF16),
        compiler_params=_cparams(("parallel",)), name=name)(*ys, g)


def _grp_bwd(ys, g, dy, *, name):
    L, GW = ys[0].shape
    tr = _tile(L, 256)

    def body(*refs):
        g_ref, dy_ref = refs[4], refs[5]
        outs, dg_ref = refs[6:10], refs[10]

        @pl.when(pl.program_id(0) == 0)
        def _():
            dg_ref[...] = jnp.zeros_like(dg_ref)

        for m in range(N_MIXERS):
            cs = slice(m * GW, (m + 1) * GW)
            yv = refs[m][...]
            r = lax.rsqrt(jnp.mean(yv * yv, axis=-1, keepdims=True) + NORM_EPS)
            xh = yv * r
            dyv = dy_ref[:, cs]
            dyg = dyv * g_ref[:, cs]
            outs[m][...] = r * (dyg - xh * jnp.mean(dyg * xh, axis=-1, keepdims=True))
            dg_ref[:, cs] += jnp.sum(dyv * xh, axis=0, keepdims=True)

    row = pl.BlockSpec((tr, GW), lambda i: (i, 0))
    wide = pl.BlockSpec((tr, N_MIXERS * GW), lambda i: (i, 0))
    vec = pl.BlockSpec((1, N_MIXERS * GW), lambda i: (0, 0))
    return pl.pallas_call(
        body, grid=(L // tr,), in_specs=[row] * 4 + [vec, wide],
        out_specs=[row] * 4 + [vec],
        out_shape=[jax.ShapeDtypeStruct((L, GW), F32)] * 4 + [jax.ShapeDtypeStruct((1, N_MIXERS * GW), F32)],
        compiler_params=_cparams(("arbitrary",)), name=name)(*ys, g, dy)


def _att_first(j, br, nblk):
    per = lax.shift_right_logical(jnp.int32(nblk), 2 * br)
    return jnp.bitwise_and(j, per - 1) == 0


def _att_fwd(q, k, vx, bias, *, name):
    nbr, H, L, E = q.shape
    B = ATT_BLOCK
    nblk = L // B

    def body(q_ref, k_ref, v_ref, b_ref, o_ref):
        br = pl.program_id(0)
        lane = lax.broadcasted_iota(jnp.int32, (B, 2 * E), 1)

        def blk(j, carry):
            off = pl.multiple_of(j * B, B)
            poff = pl.multiple_of(jnp.maximum(j - 1, 0) * B, B)
            qv = q_ref[pl.ds(off, B), :]
            s_p = _dot(qv, k_ref[pl.ds(poff, B), :], _NT) + b_ref[:, 0:B]
            s_c = _dot(qv, k_ref[pl.ds(off, B), :], _NT) + b_ref[:, B:2 * B]
            s_p = jnp.where(_att_first(j, br, nblk), NEG_INF, s_p)
            m = jnp.maximum(jnp.max(s_p, axis=-1, keepdims=True), jnp.max(s_c, axis=-1, keepdims=True))
            p_p, p_c = jnp.exp(s_p - m), jnp.exp(s_c - m)
            den = jnp.sum(p_p, axis=-1, keepdims=True) + jnp.sum(p_c, axis=-1, keepdims=True)
            acc = _dot(p_p.astype(BF16), v_ref[pl.ds(poff, B), :], _NN) + _dot(p_c.astype(BF16), v_ref[pl.ds(off, B), :], _NN)
            o_ref[pl.ds(off, B), :] = jnp.where(lane < E, acc / den, m + jnp.log(den))
            return carry

        lax.fori_loop(0, nblk, blk, 0)

    seq = lambda w: pl.BlockSpec((None, None, L, w), lambda b, h: (b, h, 0, 0))
    return pl.pallas_call(
        body, grid=(nbr, H),
        in_specs=[seq(E), seq(E), seq(2 * E), pl.BlockSpec((None, None, B, 2 * B), lambda b, h: (b, h, 0, 0))],
        out_specs=seq(2 * E), out_shape=jax.ShapeDtypeStruct((nbr, H, L, 2 * E), F32),
        compiler_params=_cparams(("parallel", "parallel")), name=name)(q, k, vx, bias)


def _att_bwd(q, k, vx, bias, ox, dox, *, name):
    nbr, H, L, E = q.shape
    B = ATT_BLOCK
    nblk = L // B

    def body(q_ref, k_ref, v_ref, b_ref, o_ref, do_ref, dq_ref, dk_ref, dv_ref, db_ref):
        br = pl.program_id(0)
        dk_ref[...] = jnp.zeros_like(dk_ref)
        dv_ref[...] = jnp.zeros_like(dv_ref)
        db_ref[...] = jnp.zeros_like(db_ref)
        lane = lax.broadcasted_iota(jnp.int32, (B, 2 * E), 1)

        def blk(j, carry):
            off = pl.multiple_of(j * B, B)
            poff = pl.multiple_of(jnp.maximum(j - 1, 0) * B, B)
            qv, kc, kp = q_ref[pl.ds(off, B), :], k_ref[pl.ds(off, B), :], k_ref[pl.ds(poff, B), :]
            vc, vp = v_ref[pl.ds(off, B), :], v_ref[pl.ds(poff, B), :]
            oe, de = o_ref[pl.ds(off, B), :], do_ref[pl.ds(off, B), :]
            lse, dlse = oe[:, E:E + 1], de[:, E:E + 1]
            do = jnp.where(lane < E, de, 0.0)
            dm = jnp.sum(do * oe, axis=-1, keepdims=True) - dlse
            dob = do.astype(BF16)
            s_p = _dot(qv, kp, _NT) + b_ref[:, 0:B]
            s_c = _dot(qv, kc, _NT) + b_ref[:, B:2 * B]
            s_p = jnp.where(_att_first(j, br, nblk), NEG_INF, s_p)
            p_p, p_c = jnp.exp(s_p - lse), jnp.exp(s_c - lse)
            ds_p = p_p * (_dot(dob, vp, _NT) - dm)
            ds_c = p_c * (_dot(dob, vc, _NT) - dm)
            db_ref[:, 0:B] += ds_p
            db_ref[:, B:2 * B] += ds_c
            dsb_p, dsb_c = ds_p.astype(BF16), ds_c.astype(BF16)
            dq_ref[pl.ds(off, B), :] = _dot(dsb_p, kp, _NN) + _dot(dsb_c, kc, _NN)
            dk_ref[pl.ds(poff, B), :] += _dot(dsb_p, qv, _TN)
            dk_ref[pl.ds(off, B), :] += _dot(dsb_c, qv, _TN)
            dv_ref[pl.ds(poff, B), :] += _dot(p_p.astype(BF16), dob, _TN)
            dv_ref[pl.ds(off, B), :] += _dot(p_c.astype(BF16), dob, _TN)
            return carry

        lax.fori_loop(0, nblk, blk, 0)

    seq = lambda w: pl.BlockSpec((None, None, L, w), lambda b, h: (b, h, 0, 0))
    bsp = pl.BlockSpec((None, None, B, 2 * B), lambda b, h: (b, h, 0, 0))
    return pl.pallas_call(
        body, grid=(nbr, H),
        in_specs=[seq(E), seq(E), seq(2 * E), bsp, seq(2 * E), seq(2 * E)],
        out_specs=[seq(E), seq(E), seq(2 * E), bsp],
        out_shape=[jax.ShapeDtypeStruct((nbr, H, L, E), F32), jax.ShapeDtypeStruct((nbr, H, L, E), F32),
                   jax.ShapeDtypeStruct((nbr, H, L, 2 * E), F32), jax.ShapeDtypeStruct((nbr, H, B, 2 * B), F32)],
        compiler_params=_cparams(("parallel", "parallel")), name=name)(q, k, vx, bias, ox, dox)


def _mix_weights(xs, lane, E, W):
    al = [jnp.where(lane < E, pltpu.roll(x, W - E, 1), x) for x in xs]
    m = functools.reduce(jnp.maximum, al)
    es = [jnp.exp(a - m) for a in al]
    tot = functools.reduce(lambda a, b: a + b, es)
    return [e / tot for e in es]


def _mix_fwd(ox, *, name):
    nbr, L, W = ox.shape
    E = W // ATT_HEADS // 2
    tr = _tile(L, 256)

    def body(x_ref, o_ref):
        lane = lax.broadcasted_iota(jnp.int32, (tr, W), 1) % (2 * E)
        xs = [x_ref[b] for b in range(nbr)]
        ws = _mix_weights(xs, lane, E, W)
        o_ref[...] = functools.reduce(lambda a, b: a + b, [w * x for w, x in zip(ws, xs)])

    return pl.pallas_call(
        body, grid=(L // tr,), in_specs=[pl.BlockSpec((nbr, tr, W), lambda i: (0, i, 0))],
        out_specs=pl.BlockSpec((tr, W), lambda i: (i, 0)), out_shape=jax.ShapeDtypeStruct((L, W), F32),
        compiler_params=_cparams(("parallel",)), name=name)(ox)


def _mix_bwd(ox, dy, *, name):
    nbr, L, W = ox.shape
    E = W // ATT_HEADS // 2
    tr = _tile(L, 256)

    def body(x_ref, dy_ref, o_ref):
        lane = lax.broadcasted_iota(jnp.int32, (tr, W), 1) % (2 * E)
        xs = [x_ref[b] for b in range(nbr)]
        ws = _mix_weights(xs, lane, E, W)
        mix = functools.reduce(lambda a, b: a + b, [w * x for w, x in zip(ws, xs)])
        dyv = jnp.where(lane < E, dy_ref[...], 0.0)
        r = lax.broadcasted_iota(jnp.int32, (W, W), 0)
        c = lax.broadcasted_iota(jnp.int32, (W, W), 1)
        seg = ((r // (2 * E) == c // (2 * E)) & (r % (2 * E) < E)).astype(F32)
        for b in range(nbr):
            t = dyv * (xs[b] - mix)
            dl = ws[b] * jnp.dot(t, seg, preferred_element_type=F32, precision=lax.Precision.HIGHEST)
            o_ref[b] = jnp.where(lane < E, ws[b] * dyv, dl)

    return pl.pallas_call(
        body, grid=(L // tr,),
        in_specs=[pl.BlockSpec((nbr, tr, W), lambda i: (0, i, 0)), pl.BlockSpec((tr, W), lambda i: (i, 0))],
        out_specs=pl.BlockSpec((nbr, tr, W), lambda i: (0, i, 0)), out_shape=jax.ShapeDtypeStruct((nbr, L, W), F32),
        compiler_params=_cparams(("parallel",), VMEM_LIMIT), name=name)(ox, dy)


def _datt_lanes(GW):
    E = GW // ATT_HEADS
    lb = min(GW, 128)
    return lb, lb // E, E


def _datt_rows(c, br, nblk):
    dil = 4 ** br
    nbs = nblk // dil
    r, jb = c // nbs, c % nbs
    return r + dil * ATT_BLOCK * jb, r + dil * ATT_BLOCK * jnp.maximum(jb - 1, 0), jb == 0


def _datt_fwd(proj, bias, *, name):
    L = proj.shape[0]
    GW = proj.shape[1] // 7
    lb, hps, E = _datt_lanes(GW)
    B = ATT_BLOCK
    nblk = L // B
    scale = E ** -0.5

    def body(q_ref, k_ref, v_ref, b_ref, y_ref, lse_ref, m_s, l_s):
        m_s[...] = jnp.full(m_s.shape, NEG_INF, F32)
        l_s[...] = jnp.zeros(l_s.shape, F32)
        y_ref[...] = jnp.zeros(y_ref.shape, F32)
        lane = lax.broadcasted_iota(jnp.int32, (B, lb), 1) // E

        for br, (_, dil) in enumerate(DILATED_PATTERNS):
            def combo(c, carry, br=br, dil=dil):
                start, pstart, first = _datt_rows(c, br, nblk)
                rows, prows = pl.ds(start, B, stride=dil), pl.ds(pstart, B, stride=dil)
                q = q_ref[rows, :] * scale
                kc, kp = k_ref[rows, :].astype(BF16), k_ref[prows, :].astype(BF16)
                vc, vp = v_ref[rows, :].astype(BF16), v_ref[prows, :].astype(BF16)
                m_old, l_old, a_old = m_s[rows, :], l_s[rows, :], y_ref[rows, :]
                m_new, l_new, a_new = m_old, l_old, a_old
                for a in range(hps):
                    sel = lane == a
                    qm = jnp.where(sel, q, 0.0).astype(BF16)
                    s_p = _dot(qm, kp, _NT) + b_ref[br, a, :, 0:B]
                    s_c = _dot(qm, kc, _NT) + b_ref[br, a, :, B:2 * B]
                    s_p = jnp.where(first, NEG_INF, s_p)
                    mo, lo = m_old[:, a * E:a * E + 1], l_old[:, a * E:a * E + 1]
                    mn = jnp.maximum(mo, jnp.maximum(jnp.max(s_p, axis=-1, keepdims=True), jnp.max(s_c, axis=-1, keepdims=True)))
                    alpha = jnp.exp(mo - mn)
                    p_p, p_c = jnp.exp(s_p - mn), jnp.exp(s_c - mn)
                    ln = alpha * lo + jnp.sum(p_p, axis=-1, keepdims=True) + jnp.sum(p_c, axis=-1, keepdims=True)
                    pv = _dot(p_p.astype(BF16), vp, _NN) + _dot(p_c.astype(BF16), vc, _NN)
                    m_new = jnp.where(sel, mn, m_new)
                    l_new = jnp.where(sel, ln, l_new)
                    a_new = jnp.where(sel, alpha * a_old + pv, a_new)
                m_s[rows, :] = m_new
                l_s[rows, :] = l_new
                y_ref[rows, :] = a_new
                return carry

            lax.fori_loop(0, nblk, combo, 0)
        y_ref[...] = y_ref[...] / l_s[...]
        lse_ref[...] = m_s[...] + jnp.log(l_s[...])

    col = lambda off: pl.BlockSpec((L, lb), lambda h: (0, off * (GW // lb) + h))
    own = pl.BlockSpec((L, lb), lambda h: (0, h))
    return pl.pallas_call(
        body, grid=(GW // lb,),
        in_specs=[col(4), col(5), col(6), pl.BlockSpec((len(DILATED_PATTERNS), hps, B, 2 * B), lambda h: (0, h, 0, 0))],
        out_specs=[own, own], out_shape=[jax.ShapeDtypeStruct((L, GW), F32)] * 2,
        scratch_shapes=[pltpu.VMEM((L, lb), F32), pltpu.VMEM((L, lb), F32)],
        compiler_params=_cparams(("parallel",), VMEM_LIMIT), name=name)(proj, proj, proj, bias)


def _datt_bwd(proj, bias, y, lse, dy, *, name):
    L = proj.shape[0]
    GW = proj.shape[1] // 7
    lb, hps, E = _datt_lanes(GW)
    B = ATT_BLOCK
    nblk = L // B
    scale = E ** -0.5

    def body(q_ref, k_ref, v_ref, b_ref, y_ref, lse_ref, dy_ref, dq_ref, dk_ref, dv_ref, db_ref):
        for r in (dq_ref, dk_ref, dv_ref, db_ref):
            r[...] = jnp.zeros(r.shape, F32)
        lane = lax.broadcasted_iota(jnp.int32, (B, lb), 1) // E

        for br, (_, dil) in enumerate(DILATED_PATTERNS):
            def combo(c, carry, br=br, dil=dil):
                start, pstart, first = _datt_rows(c, br, nblk)
                rows, prows = pl.ds(start, B, stride=dil), pl.ds(pstart, B, stride=dil)
                q = q_ref[rows, :] * scale
                kc, kp = k_ref[rows, :].astype(BF16), k_ref[prows, :].astype(BF16)
                vc, vp = v_ref[rows, :].astype(BF16), v_ref[prows, :].astype(BF16)
                dyr, lser = dy_ref[rows, :], lse_ref[rows, :]
                dyy = dyr * y_ref[rows, :]
                dq = jnp.zeros((B, lb), F32)
                dkc, dkp, dvc, dvp = dq, dq, dq, dq
                for a in range(hps):
                    sel = lane == a
                    qm = jnp.where(sel, q, 0.0).astype(BF16)
                    dym = jnp.where(sel, dyr, 0.0).astype(BF16)
                    dm = jnp.sum(jnp.where(sel, dyy, 0.0), axis=-1, keepdims=True)
                    lse_a = lser[:, a * E:a * E + 1]
                    s_p = _dot(qm, kp, _NT) + b_ref[br, a, :, 0:B]
                    s_c = _dot(qm, kc, _NT) + b_ref[br, a, :, B:2 * B]
                    s_p = jnp.where(first, NEG_INF, s_p)
                    p_p, p_c = jnp.exp(s_p - lse_a), jnp.exp(s_c - lse_a)
                    ds_p = p_p * (_dot(dym, vp, _NT) - dm)
                    ds_c = p_c * (_dot(dym, vc, _NT) - dm)
                    db_ref[br, a, :, 0:B] += ds_p
                    db_ref[br, a, :, B:2 * B] += ds_c
                    dsb_p, dsb_c = ds_p.astype(BF16), ds_c.astype(BF16)
                    dq = dq + jnp.where(sel, _dot(dsb_p, kp, _NN) + _dot(dsb_c, kc, _NN), 0.0)
                    dkp = dkp + _dot(dsb_p, qm, _TN)
                    dkc = dkc + _dot(dsb_c, qm, _TN)
                    dvp = dvp + _dot(p_p.astype(BF16), dym, _TN)
                    dvc = dvc + _dot(p_c.astype(BF16), dym, _TN)
                dq_ref[rows, :] += dq * scale
                dk_ref[prows, :] += dkp
                dk_ref[rows, :] += dkc
                dv_ref[prows, :] += dvp
                dv_ref[rows, :] += dvc
                return carry

            lax.fori_loop(0, nblk, combo, 0)

    col = lambda off: pl.BlockSpec((L, lb), lambda h: (0, off * (GW // lb) + h))
    own = pl.BlockSpec((L, lb), lambda h: (0, h))
    bsp = pl.BlockSpec((len(DILATED_PATTERNS), hps, B, 2 * B), lambda h: (0, h, 0, 0))
    return pl.pallas_call(
        body, grid=(GW // lb,),
        in_specs=[col(4), col(5), col(6), bsp, own, own, own], out_specs=[own, own, own, bsp],
        out_shape=[jax.ShapeDtypeStruct((L, GW), F32)] * 3 + [jax.ShapeDtypeStruct((len(DILATED_PATTERNS), ATT_HEADS, B, 2 * B), F32)],
        compiler_params=_cparams(("parallel",), VMEM_LIMIT), name=name)(proj, proj, proj, bias, y, lse, dy)


def _t5_bucket(dist):
    n = np.maximum(dist, 0)
    max_exact = REL_BUCKETS // 2
    large = max_exact + (np.log(np.maximum(n, 1) / max_exact) / np.log(REL_MAX_DIST / max_exact)
                         * (REL_BUCKETS - max_exact)).astype(np.int64)
    large = np.minimum(large, REL_BUCKETS - 1)
    return np.where(n < max_exact, n, large).astype(np.int32)


def _att_tables():
    a = np.arange(ATT_BLOCK)[:, None]
    b = np.arange(2 * ATT_BLOCK)[None, :]
    sub = a + ATT_BLOCK - b
    buckets, valids = [], []
    for window, dil in DILATED_PATTERNS:
        buckets.append(_t5_bucket(sub * dil))
        valids.append((sub >= 0) & (sub <= window // dil))
    return np.stack(buckets), np.stack(valids)


def _to_branches(t, H):
    L = t.shape[0]
    E = t.shape[1] // H
    out = []
    for _, dil in DILATED_PATTERNS:
        out.append(t.reshape(L // dil, dil, H, E).transpose(2, 1, 0, 3).reshape(H, L, E))
    return jnp.stack(out)


def _from_branches(t):
    _, H, L, E = t.shape
    out = []
    for b, (_, dil) in enumerate(DILATED_PATTERNS):
        out.append(t[b].reshape(H, dil, L // dil, E).transpose(2, 1, 0, 3).reshape(L, H * E))
    return jnp.stack(out)


def _xatt_fwd(q, k, v, *, name):
    L, XW = q.shape
    M = k.shape[0]
    tr = _tile(L, 512)
    scale = X_HEAD_DIM ** -0.5

    def body(q_ref, k_ref, v_ref, o_ref):
        s = _dot(q_ref[...], k_ref[...], _NT) * scale
        p = jnp.exp(s - jnp.max(s, axis=-1, keepdims=True))
        den = jnp.sum(p, axis=-1, keepdims=True)
        o_ref[...] = (_dot(p.astype(BF16), v_ref[...], _NN) / den).astype(o_ref.dtype)

    qs = pl.BlockSpec((tr, X_HEAD_DIM), lambda h, i: (i, h))
    ks = pl.BlockSpec((M, X_HEAD_DIM), lambda h, i: (0, h))
    return pl.pallas_call(
        body, grid=(XW // X_HEAD_DIM, L // tr), in_specs=[qs, ks, ks], out_specs=qs,
        out_shape=jax.ShapeDtypeStruct((L, XW), BF16),
        compiler_params=_cparams(("parallel", "parallel")), name=name)(q, k, v)


def _xatt_bwd(q, k, v, do, *, name):
    L, XW = q.shape
    M = k.shape[0]
    tr = _tile(L, 512)
    scale = X_HEAD_DIM ** -0.5

    def body(q_ref, k_ref, v_ref, do_ref, dq_ref, dk_ref, dv_ref):
        @pl.when(pl.program_id(1) == 0)
        def _():
            dk_ref[...] = jnp.zeros_like(dk_ref)
            dv_ref[...] = jnp.zeros_like(dv_ref)

        qv, kv, vv = q_ref[...], k_ref[...], v_ref[...]
        s = _dot(qv, kv, _NT) * scale
        p = jnp.exp(s - jnp.max(s, axis=-1, keepdims=True))
        p = p / jnp.sum(p, axis=-1, keepdims=True)
        dob = do_ref[...].astype(BF16)
        pb = p.astype(BF16)
        dv_ref[...] += _dot(pb, dob, _TN)
        dp = _dot(dob, vv, _NT)
        ds = (p * (dp - jnp.sum(dp * p, axis=-1, keepdims=True)) * scale).astype(BF16)
        dq_ref[...] = _dot(ds, kv, _NN)
        dk_ref[...] += _dot(ds, qv, _TN)

    qs = pl.BlockSpec((tr, X_HEAD_DIM), lambda h, i: (i, h))
    ks = pl.BlockSpec((M, X_HEAD_DIM), lambda h, i: (0, h))
    return pl.pallas_call(
        body, grid=(XW // X_HEAD_DIM, L // tr), in_specs=[qs, ks, ks, qs], out_specs=[qs, ks, ks],
        out_shape=[jax.ShapeDtypeStruct((L, XW), F32), jax.ShapeDtypeStruct((M, XW), F32), jax.ShapeDtypeStruct((M, XW), F32)],
        compiler_params=_cparams(("parallel", "arbitrary")), name=name)(q, k, v, do)


_ANY = pl.BlockSpec(memory_space=pl.ANY)


def _place():
    mx, my, mc = lax.axis_index("x"), lax.axis_index("y"), lax.axis_index("c")
    chips = [(1 - mx, my), (mx, 1 - my), (1 - mx, 1 - my)]
    return mx, my, mc, chips


def _rcopy(src, dst, ssem, rsem, dev):
    return pltpu.make_async_remote_copy(src_ref=src, dst_ref=dst, send_sem=ssem, recv_sem=rsem, device_id=dev,
                                        device_id_type=MESH)


STAGE_BYTES = 2 * 1024 * 1024


def _staged_copy(pairs, buf, isem, osem):
    n = len(pairs)
    ins = [pltpu.make_async_copy(src, buf.at[i % 2], isem.at[i % 2]) for i, (src, _) in enumerate(pairs)]
    outs = [pltpu.make_async_copy(buf.at[i % 2], dst, osem.at[i % 2]) for i, (_, dst) in enumerate(pairs)]
    ins[0].start()
    for i in range(n):
        if i + 1 < n:
            if i >= 1:
                outs[i - 1].wait()
            ins[i + 1].start()
        ins[i].wait()
        outs[i].start()
    for i in range(max(n - 2, 0), n):
        outs[i].wait()


_HBM = pl.BlockSpec(memory_space=pltpu.HBM)
_SEMS = pl.BlockSpec(memory_space=pltpu.SEMAPHORE)
_VM = pl.BlockSpec(memory_space=pltpu.VMEM)
_DATAFLOW = pltpu.SideEffectType.DATAFLOW_SIDE_EFFECTING


def _ag_copies(x_refs, land_refs, ssem, rsem, inbound):
    mx, my, mc, chips = _place()
    me = 2 * mx + my
    cps = []
    for w, (x_ref, land_ref) in enumerate(zip(x_refs, land_refs)):
        R2 = x_ref.shape[0] // 2
        mine = x_ref.at[pl.ds(mc * R2, R2)]
        for j, (px, py) in enumerate(chips):
            dst = land_ref.at[2 * px + py, mc] if inbound else land_ref.at[me, mc]
            cps.append(_rcopy(mine, dst, ssem.at[3 * w + j], rsem.at[3 * w + j], (px, py, mc)))
    return cps


def _ag_start(xs, token, *, name):
    n = len(xs)
    lands = [pltpu.with_memory_space_constraint(lax.empty((N_CHIPS, 2, x.shape[0] // 2, x.shape[1]), x.dtype), pltpu.HBM) for x in xs]
    xs = [pltpu.with_memory_space_constraint(x, pltpu.HBM) for x in xs]

    def body(*refs):
        x_refs, land_refs, tok_ref = refs[:n], refs[n:2 * n], refs[2 * n]
        ssem, rsem, tok_out = refs[2 * n + 1], refs[2 * n + 2], refs[-1]
        for cp in _ag_copies(x_refs, land_refs, ssem, rsem, False):
            cp.start()
        tok_out[...] = tok_ref[...]

    outs = pl.pallas_call(
        body, name=name,
        out_shape=(pltpu.SemaphoreType.DMA((3 * n,)), pltpu.SemaphoreType.DMA((3 * n,)),
                   *[pltpu.HBM(x.shape, x.dtype) for x in xs], *[pltpu.HBM(t.shape, t.dtype) for t in lands],
                   jax.ShapeDtypeStruct(token.shape, token.dtype)),
        in_specs=[_HBM] * (2 * n) + [_VM], out_specs=(_SEMS, _SEMS, *[_HBM] * (2 * n), _VM),
        input_output_aliases={i: 2 + i for i in range(2 * n)},
        compiler_params=pltpu.CompilerParams(has_side_effects=_DATAFLOW),
    )(*xs, *lands, token)
    return outs[0], outs[1], list(outs[2:2 + n]), list(outs[2 + n:2 + 2 * n]), outs[-1]


def _ag_wait(ssem, rsem, xs, lands, after, *, name):
    n = len(xs)

    def body(*refs):
        x_refs, land_refs, ssem_ref, rsem_ref = refs[:n], refs[n:2 * n], refs[2 * n], refs[2 * n + 1]
        for cp in _ag_copies(x_refs, land_refs, ssem_ref, rsem_ref, True):
            cp.wait_send()
            cp.wait_recv()

    after = list(after) if isinstance(after, (list, tuple)) else [after]
    outs = pl.pallas_call(
        body, name=name,
        out_shape=(*[pltpu.HBM(x.shape, x.dtype) for x in xs], *[pltpu.HBM(t.shape, t.dtype) for t in lands]),
        in_specs=[_HBM] * (2 * n) + [_SEMS, _SEMS] + [_ANY] * len(after), out_specs=tuple([_HBM] * (2 * n)),
        input_output_aliases={i: i for i in range(2 * n)},
        compiler_params=pltpu.CompilerParams(has_side_effects=_DATAFLOW),
    )(*xs, *lands, ssem, rsem, *after)
    return list(outs[:n]), list(outs[n:])


def _ag_finish(lands, xs, *, name):
    n = len(xs)

    def body(*refs):
        x_refs, o_refs, (ssem, rsem) = refs[n:2 * n], refs[2 * n:3 * n], refs[3 * n:]
        mx, my, mc, chips = _place()
        me = 2 * mx + my
        sib = (mx, my, 1 - mc)
        passed = []
        for w in range(n):
            for j, (px, py) in enumerate(chips):
                got = o_refs[w].at[2 * px + py, mc]
                passed.append(_rcopy(got, got, ssem.at[3 * w + j], rsem.at[3 * w + j], sib))
        for cp in passed:
            cp.start()
        for w in range(n):
            R, C = x_refs[w].shape
            R2 = R // 2
            ch = _rows_tile(R2, max(8, STAGE_BYTES // (C * x_refs[w].dtype.itemsize)))
            pairs = [(x_refs[w].at[pl.ds(h * R2 + r, ch)], o_refs[w].at[me, h, pl.ds(r, ch)]) for h in range(2) for r in range(0, R2, ch)]
            pl.run_scoped(functools.partial(_staged_copy, pairs), pltpu.VMEM((2, ch, C), x_refs[w].dtype),
                          pltpu.SemaphoreType.DMA((2,)), pltpu.SemaphoreType.DMA((2,)))
        for w in range(n):
            for j, (px, py) in enumerate(chips):
                theirs = o_refs[w].at[2 * px + py, 1 - mc]
                _rcopy(theirs, theirs, ssem.at[3 * w + j], rsem.at[3 * w + j], sib).wait_recv()
        for cp in passed:
            cp.wait_send()

    return pl.pallas_call(
        body, in_specs=[_ANY] * (2 * n), out_specs=[_ANY] * n,
        out_shape=[jax.ShapeDtypeStruct(t.shape, t.dtype) for t in lands],
        input_output_aliases={i: i for i in range(n)},
        scratch_shapes=[pltpu.SemaphoreType.DMA((3 * n,)), pltpu.SemaphoreType.DMA((3 * n,))],
        name=name)(*lands, *xs)


def _ag4(x, *, name):
    def body(x_ref, o_ref, ssem, rsem, lsem):
        mx, my, mc, chips = _place()
        me = 2 * mx + my
        loc = pltpu.make_async_copy(x_ref, o_ref.at[me], lsem)
        loc.start()
        sends = [_rcopy(x_ref, o_ref.at[me], ssem.at[j], rsem.at[j], (px, py, mc)) for j, (px, py) in enumerate(chips)]
        for cp in sends:
            cp.start()
        for j, (px, py) in enumerate(chips):
            _rcopy(x_ref, o_ref.at[2 * px + py], ssem.at[j], rsem.at[j], (px, py, mc)).wait_recv()
        for cp in sends:
            cp.wait_send()
        loc.wait()

    return pl.pallas_call(
        body, in_specs=[_ANY], out_specs=_ANY, out_shape=jax.ShapeDtypeStruct((N_CHIPS,) + x.shape, x.dtype),
        scratch_shapes=[pltpu.SemaphoreType.DMA((3,)), pltpu.SemaphoreType.DMA((3,)), pltpu.SemaphoreType.DMA(())],
        name=name)(x)


def _rs_sib(gs, *, name):
    n = len(gs)

    def body(*refs):
        g_refs, t_refs, (ssem, rsem) = refs[:n], refs[n:2 * n], refs[2 * n:]
        mx, my, mc, _ = _place()
        sib = (mx, my, 1 - mc)
        sends = []
        for w in range(n):
            R2 = g_refs[w].shape[1] // 2
            for k in range(N_CHIPS):
                sends.append(_rcopy(g_refs[w].at[k, pl.ds((1 - mc) * R2, R2)], t_refs[w].at[k], ssem.at[N_CHIPS * w + k],
                                    rsem.at[N_CHIPS * w + k], sib))
        for cp in sends:
            cp.start()
        for w in range(n):
            for k in range(N_CHIPS):
                t = t_refs[w].at[k]
                _rcopy(t, t, ssem.at[N_CHIPS * w + k], rsem.at[N_CHIPS * w + k], sib).wait_recv()
        for cp in sends:
            cp.wait_send()

    return pl.pallas_call(
        body, in_specs=[_ANY] * n, out_specs=[_ANY] * n,
        out_shape=[jax.ShapeDtypeStruct((N_CHIPS, g.shape[1] // 2, g.shape[2]), g.dtype) for g in gs],
        scratch_shapes=[pltpu.SemaphoreType.DMA((N_CHIPS * n,)), pltpu.SemaphoreType.DMA((N_CHIPS * n,))], name=name)(*gs)


def _a2a_copies(h_refs, got_refs, ssem, rsem, inbound):
    mx, my, mc, chips = _place()
    me = 2 * mx + my
    cps = []
    for w, (h_ref, got_ref) in enumerate(zip(h_refs, got_refs)):
        for j, (px, py) in enumerate(chips):
            dst = got_ref.at[2 * px + py] if inbound else got_ref.at[me]
            cps.append(_rcopy(h_ref.at[2 * px + py], dst, ssem.at[3 * w + j], rsem.at[3 * w + j], (px, py, mc)))
    return cps


def _a2a_start(hs, *, name):
    n = len(hs)
    gots = [pltpu.with_memory_space_constraint(lax.empty(h.shape, h.dtype), pltpu.HBM) for h in hs]
    hs = [pltpu.with_memory_space_constraint(h, pltpu.HBM) for h in hs]

    def body(*refs):
        h_refs, got_refs = refs[:n], refs[n:2 * n]
        ssem, rsem, tok_out = refs[2 * n], refs[2 * n + 1], refs[-1]
        for cp in _a2a_copies(h_refs, got_refs, ssem, rsem, False):
            cp.start()
        tok_out[...] = jnp.zeros_like(tok_out)

    outs = pl.pallas_call(
        body, name=name,
        out_shape=(pltpu.SemaphoreType.DMA((3 * n,)), pltpu.SemaphoreType.DMA((3 * n,)),
                   *[pltpu.HBM(h.shape, h.dtype) for h in hs], *[pltpu.HBM(h.shape, h.dtype) for h in hs],
                   jax.ShapeDtypeStruct((8, 128), F32)),
        in_specs=[_HBM] * (2 * n), out_specs=(_SEMS, _SEMS, *[_HBM] * (2 * n), _VM),
        input_output_aliases={i: 2 + i for i in range(2 * n)},
        compiler_params=pltpu.CompilerParams(has_side_effects=_DATAFLOW),
    )(*hs, *gots)
    return outs[0], outs[1], list(outs[2:2 + n]), list(outs[2 + n:2 + 2 * n]), outs[-1]


def _a2a_wait(ssem, rsem, hs, gots, after, *, name):
    n = len(hs)

    def body(*refs):
        h_refs, got_refs, ssem_ref, rsem_ref = refs[:n], refs[n:2 * n], refs[2 * n], refs[2 * n + 1]
        for cp in _a2a_copies(h_refs, got_refs, ssem_ref, rsem_ref, True):
            cp.wait_send()
            cp.wait_recv()

    after = list(after) if isinstance(after, (list, tuple)) else [after]
    outs = pl.pallas_call(
        body, name=name,
        out_shape=tuple(pltpu.HBM(h.shape, h.dtype) for h in hs + gots),
        in_specs=[_HBM] * (2 * n) + [_SEMS, _SEMS] + [_ANY] * len(after), out_specs=tuple([_HBM] * (2 * n)),
        input_output_aliases={i: i for i in range(2 * n)},
        compiler_params=pltpu.CompilerParams(has_side_effects=_DATAFLOW),
    )(*hs, *gots, ssem, rsem, *after)
    return list(outs[:n]), list(outs[n:])


def _sib_fill(bufs, *, name):
    n = len(bufs)

    def body(*refs):
        o_refs, (ssem, rsem) = refs[n:2 * n], refs[2 * n:]
        mx, my, mc, _ = _place()
        sib = (mx, my, 1 - mc)
        sends = [_rcopy(o_refs[w].at[mc], o_refs[w].at[mc], ssem.at[w], rsem.at[w], sib) for w in range(n)]
        for cp in sends:
            cp.start()
        for w in range(n):
            t = o_refs[w].at[1 - mc]
            _rcopy(t, t, ssem.at[w], rsem.at[w], sib).wait_recv()
        for cp in sends:
            cp.wait_send()

    return pl.pallas_call(
        body, in_specs=[_ANY] * n, out_specs=[_ANY] * n, out_shape=[jax.ShapeDtypeStruct(b.shape, b.dtype) for b in bufs],
        input_output_aliases={i: i for i in range(n)},
        scratch_shapes=[pltpu.SemaphoreType.DMA((n,)), pltpu.SemaphoreType.DMA((n,))], name=name)(*bufs)


def _rows_tile(n, pref=512):
    t = min(n, pref)
    while n % t or (t % 8 and t != n):
        t -= 1
    return t


def _rs_add(g, theirs, c_arr, payload, *, name):
    _, R, C = g.shape
    R2 = R // 2
    tr = _rows_tile(R2, 256)
    nb = R2 // tr

    def body(c_ref, g_ref, t_ref, o_ref):
        o_ref[...] = (g_ref[...].astype(F32) + t_ref[...].astype(F32)).astype(o_ref.dtype)

    blk = pl.BlockSpec((None, tr, C), lambda k, i, c: (k, i, 0))
    return pl.pallas_call(
        body,
        grid_spec=pltpu.PrefetchScalarGridSpec(
            num_scalar_prefetch=1, grid=(N_CHIPS, nb),
            in_specs=[pl.BlockSpec((None, tr, C), lambda k, i, c: (k, c[0] * nb + i, 0)), blk], out_specs=blk),
        out_shape=jax.ShapeDtypeStruct((N_CHIPS, R2, C), payload),
        compiler_params=_cparams(("arbitrary", "arbitrary")), name=name)(c_arr, g, theirs)


def _rs_sum(chip_sum, got, mc_arr, *, name):
    _, R2, C = chip_sum.shape
    tr = _rows_tile(R2, 256)

    def body(s_ref, own_ref, g0, g1, g2, g3, o_ref):
        me = s_ref[0]
        acc = jnp.zeros((tr, C), F32)
        for k, g_ref in enumerate((g0, g1, g2, g3)):
            acc = acc + jnp.where(me == k, own_ref[...], g_ref[...]).astype(F32)
        o_ref[...] = acc

    def got_spec(k):
        return pl.BlockSpec((None, tr, C), lambda i, s: (jnp.where(s[0] == k, (k + 1) % N_CHIPS, k), i, 0))

    return pl.pallas_call(
        body,
        grid_spec=pltpu.PrefetchScalarGridSpec(
            num_scalar_prefetch=1, grid=(R2 // tr,),
            in_specs=[pl.BlockSpec((None, tr, C), lambda i, s: (s[0], i, 0))] + [got_spec(k) for k in range(N_CHIPS)],
            out_specs=pl.BlockSpec((None, tr, C), lambda i, s: (s[1], i, 0))),
        out_shape=jax.ShapeDtypeStruct((2, R2, C), F32),
        compiler_params=_cparams(("arbitrary",)), name=name)(mc_arr, chip_sum, got, got, got, got)


def _adamw(w, m, v, gs, *, name, first=0, prev=None):
    _, _, R2, C = w.shape
    nl = len(gs)
    tr = _rows_tile(R2, 256)
    c1 = 1.0 / (1.0 - ADAM_B1 ** ADAM_STEP)
    c2 = 1.0 / (1.0 - ADAM_B2 ** ADAM_STEP)

    def body(w_ref, m_ref, v_ref, *refs):
        g_refs, (go_ref, d_ref, mo_ref, vo_ref) = refs[:nl], refs[-4:]
        l = pl.program_id(0)
        for ll in range(nl):
            @pl.when(l == ll)
            def _(ll=ll):
                gv = g_refs[ll][...]
                mn = ADAM_B1 * m_ref[...] + (1.0 - ADAM_B1) * gv
                vn = ADAM_B2 * v_ref[...] + (1.0 - ADAM_B2) * (gv * gv)
                go_ref[...] = gv
                mo_ref[...] = mn
                vo_ref[...] = vn
                d_ref[...] = -ADAM_LR * ((mn * c1) / (jnp.sqrt(vn * c2) + ADAM_EPS) + ADAM_WD * w_ref[...])

    def g_spec(ll):
        return pl.BlockSpec((None, tr, C), lambda l, h, i: (jnp.where(l == ll, h, 0), jnp.where(l == ll, i, 0), 0))

    ws = pl.BlockSpec((None, None, tr, C), lambda l, h, i: (l + first, h, i, 0))
    shp = jax.ShapeDtypeStruct(w.shape, F32)
    prev = list(prev) if prev is not None else []
    return pl.pallas_call(
        body, grid=(nl, 2, R2 // tr), in_specs=[ws, ws, ws] + [g_spec(ll) for ll in range(nl)] + [_ANY] * len(prev),
        out_specs=[ws] * 4, out_shape=[shp] * 4,
        input_output_aliases={3 + nl + k: k for k in range(len(prev))},
        compiler_params=_cparams(("arbitrary", "arbitrary", "arbitrary")), name=name)(w, m, v, *gs, *prev)


class _GradReducer:
    def __init__(self, tag, payload):
        self.tag, self.payload = tag, payload
        self.me = (2 * lax.axis_index("x") + lax.axis_index("y")).astype(jnp.int32)
        self.c = lax.axis_index("c").astype(jnp.int32)

    def start(self, names, gs):
        theirs = _rs_sib(gs, name=f"rs_sib_{self.tag}")
        hs = [_rs_add(g, t, self.c.reshape(1), self.payload, name=f"rs_add_{n}") for n, g, t in zip(names, gs, theirs)]
        self.names = names
        self.ssem, self.rsem, self.hs, self.gots, token = _a2a_start(hs, name=f"rs_a2a_start_{self.tag}")
        return token

    def finish(self, after):
        hs, gots = _a2a_wait(self.ssem, self.rsem, self.hs, self.gots, after, name=f"rs_a2a_wait_{self.tag}")
        mc = jnp.stack([self.me, self.c])
        return {n: _rs_sum(h, g, mc, name=f"rs_sum_{n}") for n, h, g in zip(self.names, hs, gots)}


WEIGHTS = ["rel_bias", "mem_norm_g", "norm_mix_g", "w_in", "s5_lam_re", "s5_lam_im", "s5_log_dt", "s5_b_re", "s5_b_im",
           "s5_c_re", "s5_c_im", "s5_d", "s5_w_glu", "pool_w", "pool_scale", "conv_w_dw", "conv_b_dw", "conv_ln_g",
           "conv_ln_b", "conv_w_pw", "grp_norm_g", "w_out", "norm_x_g", "w_xq", "w_xk", "w_xv", "w_xo", "norm_mlp_g",
           "w_up", "w_down", "norm_final_g"]
SHARDED = {"w_in": "col", "s5_w_glu": "row", "conv_w_dw": "col", "conv_w_pw": "row", "w_out": "row", "w_xq": "row",
           "w_xk": "row", "w_xv": "row", "w_xo": "col", "w_up": "col", "w_down": "row"}
SMALL = [n for n in WEIGHTS if n not in SHARDED]
SMALL_ALIGN = N_CHIPS * 2 * 256 * 128


def _mat(w, kind):
    return w.reshape(w.shape[0] * w.shape[1], w.shape[2]) if kind == "row" else w


def _att_bias(rel_bias):
    bucket, valid = _att_tables()
    onehot = (jnp.asarray(bucket.reshape(-1))[:, None] == jnp.arange(REL_BUCKETS)[None, :]).astype(F32)
    b = jnp.dot(onehot, rel_bias, precision=lax.Precision.HIGHEST).reshape(bucket.shape + (rel_bias.shape[1],))
    return jnp.where(jnp.asarray(valid)[:, None], jnp.transpose(b, (0, 3, 1, 2)), NEG_INF)


def _rel_bias_grad(dbias):
    bucket, _ = _att_tables()
    nb, H = dbias.shape[0], dbias.shape[1]
    onehot = (jnp.asarray(bucket.reshape(-1))[:, None] == jnp.arange(REL_BUCKETS)[None, :]).astype(BF16)
    flat = jnp.transpose(dbias.reshape(nb, H, -1), (1, 0, 2)).reshape(H, -1)
    return _mm(flat, onehot, "nn", name="rel_bias_grad").T


def _layer_fwd(h, mem_n, bias, sp, bw, l):
    L, D = h.shape
    GW = D // N_MIXERS
    G = GW // S5_CH
    E = GW // ATT_HEADS
    sv = {"h": h}
    xn = _rms_fwd(h, sp["norm_mix_g"][l][None], name="norm_mix")
    proj = _mm(xn, bw["w_in"], "nn", name="proj_in")
    sv.update(xn=xn, proj=proj)
    disc, disc_vjp = jax.vjp(_s5_disc, sp["s5_lam_re"][l], sp["s5_lam_im"][l], sp["s5_log_dt"][l], sp["s5_b_re"][l], sp["s5_b_im"][l])
    ab_r, ab_i, bb_r, bb_i = disc
    s5 = dict(
        bdr=_bd(jnp.transpose(bb_r, (0, 2, 1))).astype(BF16), bdi=_bd(jnp.transpose(bb_i, (0, 2, 1))).astype(BF16),
        cdr=_bd(jnp.transpose(sp["s5_c_re"][l], (0, 2, 1))).astype(BF16), cdi=_bd(jnp.transpose(sp["s5_c_im"][l], (0, 2, 1))).astype(BF16),
        d=sp["s5_d"][l][None], wglu=bw["s5_w_glu"], ab=(ab_r.reshape(-1), ab_i.reshape(-1)), vjp=disc_vjp)
    pw, tr, ti = _cpow_tables(*s5["ab"])
    y_a, xr, xi = _s5_fwd(proj, s5["bdr"], s5["bdi"], pw, tr, ti, s5["cdr"], s5["cdi"], s5["d"], s5["wglu"], name="s5_fwd")
    sv.update(s5=s5, xr=xr, xi=xi, y_a=y_a)
    pool_s = sp["pool_scale"][l].reshape(len(POOL_WINDOWS), 1, -1)
    y_b = _pool_fwd(proj, sp["pool_w"][l], pool_s, name="pool_fwd")
    sv.update(y_b=y_b, pool_s=pool_s)
    hc = _conv1_fwd(proj, bw["conv_w_dw"], sp["conv_b_dw"][l][None], name="conv_dw_fwd")
    y_c = _conv2_fwd(hc, sp["conv_ln_g"][l][None], sp["conv_ln_b"][l][None], bw["conv_w_pw"], name="conv_pw_fwd")
    sv.update(hc=hc, y_c=y_c)
    y_d, lse_d = _datt_fwd(proj, bias, name="att_fwd")
    sv.update(y_d=y_d, lse_d=lse_d)
    yn = _grp_fwd([y_a, y_b, y_c, y_d], sp["grp_norm_g"][l][None], name="grp_fwd")
    h1 = _mm(yn, bw["w_out"], "nn", res=h, name="proj_out")
    sv.update(yn=yn, h1=h1)
    hx = _rms_fwd(h1, sp["norm_x_g"][l][None], name="norm_x")
    q_x = _mm(hx, bw["w_xq"], "nn", out_dtype=BF16, name="xq")
    k_x = _mm(mem_n, bw["w_xk"], "nn", out_dtype=BF16, name="xk")
    v_x = _mm(mem_n, bw["w_xv"], "nn", out_dtype=BF16, name="xv")
    o_x = _xatt_fwd(q_x, k_x, v_x, name="xatt_fwd")
    h2 = _mm(o_x, bw["w_xo"], "nn", res=h1, name="xo")
    sv.update(hx=hx, q_x=q_x, k_x=k_x, v_x=v_x, o_x=o_x, h2=h2)
    hm = _rms_fwd(h2, sp["norm_mlp_g"][l][None], name="norm_mlp")
    up, act = _mm(hm, bw["w_up"], "nn", epi="relu2", out_dtype=BF16, name="mlp_up")
    h3 = _mm(act, bw["w_down"], "nn", res=h2, name="mlp_down")
    sv.update(hm=hm, up=up, act=act)
    return h3, sv


def _stack_rows(g):
    return g.reshape(N_CHIPS, g.shape[0] // N_CHIPS, g.shape[1])


def _layer_bwd(dh3, d_memn, mem_n, bias, sp, bw, sv, l):
    L, D = dh3.shape
    GW = D // N_MIXERS
    G = GW // S5_CH
    E = GW // ATT_HEADS
    gs, gb = {}, {}
    d_up = _mm(dh3, bw["w_down"], "nt", epi="relu2_bwd", aux=sv["up"], out_dtype=BF16, name="mlp_down_dx")
    gb["w_down"] = _stack_rows(_mm(sv["act"], dh3, "tn", out_dtype=BF16, name="mlp_down_dw"))
    gb["w_up"] = _mm(sv["hm"], d_up, "tn", out_dtype=BF16, out_stack=N_CHIPS, name="mlp_up_dw")
    d_hm = _mm(d_up, bw["w_up"], "nt", name="mlp_up_dx")
    dh2, gs["norm_mlp_g"] = _rms_bwd(sv["h2"], sp["norm_mlp_g"][l][None], d_hm, dh3, name="norm_mlp_bwd")
    d_ox = _mm(dh2, bw["w_xo"], "nt", name="xo_dx")
    gb["w_xo"] = _mm(sv["o_x"], dh2, "tn", out_dtype=BF16, out_stack=N_CHIPS, name="xo_dw")
    dq_x, dk_x, dv_x = _xatt_bwd(sv["q_x"], sv["k_x"], sv["v_x"], d_ox, name="xatt_bwd")
    gb["w_xq"] = _stack_rows(_mm(sv["hx"], dq_x, "tn", out_dtype=BF16, name="xq_dw"))
    gb["w_xk"] = _stack_rows(_mm(mem_n, dk_x, "tn", out_dtype=BF16, name="xk_dw"))
    gb["w_xv"] = _stack_rows(_mm(mem_n, dv_x, "tn", out_dtype=BF16, name="xv_dw"))
    d_memn = _mm(dk_x, bw["w_xk"], "nt", res=d_memn, name="xk_dx")
    d_memn = _mm(dv_x, bw["w_xv"], "nt", res=d_memn, name="xv_dx")
    d_hx = _mm(dq_x, bw["w_xq"], "nt", name="xq_dx")
    dh1, gs["norm_x_g"] = _rms_bwd(sv["h1"], sp["norm_x_g"][l][None], d_hx, dh2, name="norm_x_bwd")
    d_yn = _mm(dh1, bw["w_out"], "nt", name="proj_out_dx")
    gb["w_out"] = _stack_rows(_mm(sv["yn"], dh1, "tn", out_dtype=BF16, name="proj_out_dw"))
    ys = [sv["y_a"], sv["y_b"], sv["y_c"], sv["y_d"]]
    dya, dyb, dyc, dyd, gs["grp_norm_g"] = _grp_bwd(ys, sp["grp_norm_g"][l][None], d_yn, name="grp_bwd")
    dq, dk, dv, dbias = _datt_bwd(sv["proj"], bias, sv["y_d"], sv["lse_d"], dyd, name="att_bwd")
    dhc, gs["conv_ln_g"], gs["conv_ln_b"], g_pw = _conv2_bwd(sv["hc"], sp["conv_ln_g"][l][None], sp["conv_ln_b"][l][None],
                                                               bw["conv_w_pw"], dyc, name="conv_pw_bwd")
    gb["conv_w_pw"] = _stack_rows(g_pw)
    dval, dgate, g_dw, gs["conv_b_dw"] = _conv1_bwd(sv["proj"], dhc, bw["conv_w_dw"], name="conv_dw_bwd")
    gb["conv_w_dw"] = jnp.transpose(g_dw.reshape(CONV_PAD, N_CHIPS, GW // N_CHIPS), (1, 0, 2))
    du_b, gs["pool_w"], g_ps = _pool_bwd(sv["proj"], dyb, sp["pool_w"][l], sv["pool_s"], name="pool_bwd")
    gs["pool_scale"] = g_ps.reshape(-1)
    s5 = sv["s5"]
    conj = (s5["ab"][0], -s5["ab"][1])
    pw, tr, ti = _cpow_tables(*conj)
    du_a, g_glu, g_d, dcdr, dcdi, dbdr, dbdi, dar, dai = _s5_bwd(
        sv["proj"], sv["xr"], sv["xi"], dya, s5["bdr"], s5["bdi"], pw, tr[::-1], ti[::-1], s5["cdr"], s5["cdi"], s5["d"], s5["wglu"],
        name="s5_bwd")
    gb["s5_w_glu"] = _stack_rows(g_glu)
    gs["s5_d"] = g_d.reshape(-1)
    gs["s5_c_re"] = jnp.transpose(_bd_extract(dcdr, G), (0, 2, 1))
    gs["s5_c_im"] = jnp.transpose(_bd_extract(dcdi, G), (0, 2, 1))
    d_bb_r = jnp.transpose(_bd_extract(dbdr, G), (0, 2, 1))
    d_bb_i = jnp.transpose(_bd_extract(dbdi, G), (0, 2, 1))
    d_ab_r = jnp.sum(dar, axis=0).reshape(G, S5_STATE)
    d_ab_i = jnp.sum(dai, axis=0).reshape(G, S5_STATE)
    gs["s5_lam_re"], gs["s5_lam_im"], gs["s5_log_dt"], gs["s5_b_re"], gs["s5_b_im"] = s5["vjp"]((d_ab_r, d_ab_i, d_bb_r, d_bb_i))
    d_proj = jnp.concatenate([du_a, du_b, dval, dgate, dq, dk, dv], axis=1)
    gb["w_in"] = _mm(sv["xn"], d_proj, "tn", out_dtype=BF16, out_stack=N_CHIPS, name="proj_in_dw")
    d_xn = _mm(d_proj, bw["w_in"], "nt", name="proj_in_dx")
    dh0, gs["norm_mix_g"] = _rms_bwd(sv["h"], sp["norm_mix_g"][l][None], d_xn, dh1, name="norm_mix_bwd")
    gs = {n: g.reshape(sp[n].shape[1:]) for n, g in gs.items()}
    return dh0, d_memn, dbias, gs, gb


def _device_step(x, mem, target, sp, get_big, on_grads):
    nl = sp["norm_mix_g"].shape[0]
    mem_n = _rms_fwd(mem, sp["mem_norm_g"][None], name="norm_mem")
    bias = _att_bias(sp["rel_bias"])
    h, saved, big = x, [], {n: [] for n in SHARDED}
    for l in range(nl):
        bw = get_big(l, h)
        for n in SHARDED:
            big[n].append(bw[n])
        h, sv = _layer_fwd(h, mem_n, bias, sp, bw, l)
        saved.append(sv)
    loss, dh, g_final = _loss_head(h, sp["norm_final_g"][None], target, name="loss_head")
    d_memn = jnp.zeros(mem.shape, F32)
    dbias = jnp.zeros(bias.shape, F32)
    sg = {n: [None] * nl for n in SMALL}
    for l in reversed(range(nl)):
        dh, d_memn, dbias_l, gs, gb = _layer_bwd(dh, d_memn, mem_n, bias, sp, {n: big[n][l] for n in SHARDED}, saved[l], l)
        dbias = dbias + dbias_l
        for n, g in gs.items():
            sg[n][l] = g
        dh = on_grads(l, gb, dh)
    small = {n: jnp.stack(g) for n, g in sg.items() if g[0] is not None}
    small["norm_final_g"] = g_final.reshape(-1)
    _, g_mem = _rms_bwd(mem, sp["mem_norm_g"][None], d_memn, None, name="norm_mem_bwd")
    small["mem_norm_g"] = g_mem.reshape(-1)
    small["rel_bias"] = _rel_bias_grad(dbias)
    return loss, dh, small


def _gather_big(shards, prep):
    nl = shards["w_in"].shape[0]
    token = jnp.zeros((8, 128), F32)
    pending = []
    for l in range(nl):
        xs = [jnp.pad(shards[n][l], ((0, CONV_PAD - CONV_WIDTH), (0, 0))) if n == "conv_w_dw" else shards[n][l].astype(BF16)
              for n in SHARDED]
        ssem, rsem, xs, lands, token = _ag_start(xs, token, name=f"ag_start_l{l}")
        pending.append((ssem, rsem, xs, lands))

    def get_big(l, after):
        ssem, rsem, xs, lands = pending[l]
        xs, lands = _ag_wait(ssem, rsem, xs, lands, [token] + list(prep) if l == 0 else after, name=f"ag_wait_l{l}")
        bw = {}
        for (n, kind), x, full in zip(SHARDED.items(), xs, _ag_finish(lands, xs, name="ag_finish")):
            full = full.reshape(N_CHIPS, x.shape[0], x.shape[1])
            bw[n] = jnp.transpose(full, (1, 0, 2)).reshape(CONV_PAD, -1) if n == "conv_w_dw" else _mat(full, kind)
        return bw

    return get_big


def _pack_small(vals, extra=None):
    flat = [vals[n].reshape(-1).astype(F32) for n in SMALL]
    flat.append(jnp.zeros((1,), F32) if extra is None else extra.reshape(1))
    v = jnp.concatenate(flat)
    n_pad = -(-v.shape[0] // SMALL_ALIGN) * SMALL_ALIGN
    return jnp.pad(v, (0, n_pad - v.shape[0]))


def _unpack_small(flat, like):
    out, pos = {}, 0
    for n in SMALL:
        size = math.prod(like[n].shape)
        out[n] = flat[pos:pos + size].reshape(like[n].shape)
        pos += size
    return out, flat[pos]


def kernel(x, mem, rel_bias, mem_norm_g, norm_mix_g, w_in, s5_lam_re, s5_lam_im, s5_log_dt, s5_b_re, s5_b_im, s5_c_re, s5_c_im, s5_d, s5_w_glu, pool_w, pool_scale, conv_w_dw, conv_b_dw, conv_ln_g, conv_ln_b, conv_w_pw, grp_norm_g, w_out, norm_x_g, w_xq, w_xk, w_xv, w_xo, norm_mlp_g, w_up, w_down, norm_final_g, loss_target, m_rel_bias, m_mem_norm_g, m_norm_mix_g, m_w_in, m_s5_lam_re, m_s5_lam_im, m_s5_log_dt, m_s5_b_re, m_s5_b_im, m_s5_c_re, m_s5_c_im, m_s5_d, m_s5_w_glu, m_pool_w, m_pool_scale, m_conv_w_dw, m_conv_b_dw, m_conv_ln_g, m_conv_ln_b, m_conv_w_pw, m_grp_norm_g, m_w_out, m_norm_x_g, m_w_xq, m_w_xk, m_w_xv, m_w_xo, m_norm_mlp_g, m_w_up, m_w_down, m_norm_final_g, v_rel_bias, v_mem_norm_g, v_norm_mix_g, v_w_in, v_s5_lam_re, v_s5_lam_im, v_s5_log_dt, v_s5_b_re, v_s5_b_im, v_s5_c_re, v_s5_c_im, v_s5_d, v_s5_w_glu, v_pool_w, v_pool_scale, v_conv_w_dw, v_conv_b_dw, v_conv_ln_g, v_conv_ln_b, v_conv_w_pw, v_grp_norm_g, v_w_out, v_norm_x_g, v_w_xq, v_w_xk, v_w_xv, v_w_xo, v_norm_mlp_g, v_w_up, v_w_down, v_norm_final_g):
    env = dict(locals())
    w = {n: env[n] for n in WEIGHTS}
    m = {n: env["m_" + n] for n in WEIGHTS}
    v = {n: env["v_" + n] for n in WEIGHTS}
    sp = {n: w[n] for n in SMALL}
    small_wmv = [_pack_small(t).reshape(1, 2, -1, 128) for t in (w, m, v)]
    get_big = _gather_big({n: w[n] for n in SHARDED}, small_wmv)
    nl = w["w_in"].shape[0]
    names = list(SHARDED)
    reducers, reduced = {}, {}

    def on_grads(l, gb, dh):
        reducers[l] = _GradReducer(f"l{l}", BF16)
        token = reducers[l].start(names, [gb[n] for n in names])
        if l + 1 in reducers:
            reduced[l + 1] = reducers[l + 1].finish(dh)
        return dh + token[0, 0]

    loss, grad_x, g_small = _device_step(x[0], mem[0], loss_target[0], sp, get_big, on_grads)

    def shard_views(n):
        pad = ((0, 0), (0, CONV_PAD - CONV_WIDTH), (0, 0)) if n == "conv_w_dw" else None
        wmv = [jnp.pad(t[n], pad) if pad else t[n] for t in (w, m, v)]
        _, R, C = wmv[0].shape
        return [t.reshape(nl, 2, R // 2, C) for t in wmv]

    grad, delta, new_m, new_v = {}, {}, {}, {}
    packed = _pack_small(g_small, loss).reshape(N_CHIPS, -1, 128)
    small = _GradReducer("small", F32)
    token = small.start(["small"], [packed])
    quarter = _sib_fill([small.finish(token)["small"]], name="rs_fill_small")[0]
    total = _ag4(quarter.reshape(-1, 128), name="ag_small").reshape(2, -1, 128)
    outs = _adamw(*small_wmv, [total], name="adamw_small")
    busy, upper = [outs[0]], {}
    if nl > 1:
        filled = _sib_fill([reduced[l][n] for n in names for l in range(1, nl)], name="rs_fill_upper")
        for i, n in enumerate(names):
            upper[n] = _adamw(*shard_views(n), filled[i * (nl - 1):(i + 1) * (nl - 1)], first=1, name=f"adamw_upper_{n}")
            busy.append(upper[n][0])
    reduced[0] = reducers[0].finish(busy)
    filled = _sib_fill([reduced[0][n] for n in names], name="rs_fill_first")
    for i, n in enumerate(names):
        res = _adamw(*shard_views(n), [filled[i]], prev=upper.get(n), name=f"adamw_first_{n}")
        R = w[n].shape[1]
        grad[n], delta[n], new_m[n], new_v[n] = [o.reshape(nl, -1, o.shape[-1])[:, :R] for o in res]
    loss_sum = None
    for o, dst in zip(outs, (grad, delta, new_m, new_v)):
        vals, last = _unpack_small(o.reshape(-1), sp)
        dst.update(vals)
        loss_sum = last if loss_sum is None else loss_sum
    return (loss_sum, grad_x[None], *[grad[n] for n in WEIGHTS], *[delta[n] for n in WEIGHTS],
            *[new_m[n] for n in WEIGHTS], *[new_v[n] for n in WEIGHTS])
```

```python
import functools
import math

import jax
import jax.numpy as jnp
import numpy as np
from jax import lax
from jax.experimental import pallas as pl
from jax.experimental.pallas import tpu as pltpu

F32 = jnp.float32
BF16 = jnp.bfloat16
MESH = pl.DeviceIdType.MESH

N_MIXERS = 4
S5_CH = 16
S5_STATE = 64
POOL_WINDOWS = (2, 4, 8, 16)
CONV_WIDTH = 31
CONV_PAD = 32
ATT_HEADS = 8
ATT_BLOCK = 128
DILATED_PATTERNS = ((128, 1), (512, 4), (2048, 16))
REL_BUCKETS = 32
REL_MAX_DIST = 2048
X_HEADS = 4
X_HEAD_DIM = 128
NORM_EPS = 1e-6
NEG_INF = -1e30
ADAM_LR, ADAM_B1, ADAM_B2, ADAM_EPS, ADAM_WD, ADAM_STEP = 0.001, 0.9, 0.999, 1e-08, 0.01, 10
N_CHIPS = 4
VMEM_LIMIT = 56 * 1024 * 1024


def _cparams(sem=None, vmem=None):
    return pltpu.CompilerParams(dimension_semantics=sem, vmem_limit_bytes=vmem)


def _tile(n, pref):
    if n <= pref:
        return n
    t = pref
    while t >= 128:
        if n % t == 0:
            return t
        t -= 128
    return n


def _spec(arr, tr, tc, rc):
    if arr.ndim == 2:
        return pl.BlockSpec((tr, tc), lambda *g: rc(*g))
    per = arr.shape[2] // tc
    assert arr.shape[2] % tc == 0

    def imap(*g):
        r, c = rc(*g)
        return (c // per, r, c % per)

    return pl.BlockSpec((None, tr, tc), imap)


def _lshape(arr):
    return arr.shape if arr.ndim == 2 else (arr.shape[1], arr.shape[0] * arr.shape[2])


def _mm(a, b, mode, *, name, out_dtype=F32, res=None, epi=None, aux=None, out_stack=None, tm=1024, tn=1024, tk=2048):
    la, lb = _lshape(a), _lshape(b)
    if mode == "nn":
        (M, K), (K2, N) = la, lb
    elif mode == "nt":
        (M, K), (N, K2) = la, lb
    else:
        (K, M), (K2, N) = la, lb
    assert K == K2, (la, lb, mode)
    lim = {"m": M, "n": N // out_stack if out_stack else N, "k": K}
    stacked = [(a, "m" if mode == "tn" else "k"), (b, "k" if mode == "nt" else "n"), (res, "n"), (aux, "n")]
    for arr, dim in stacked:
        if arr is not None and arr.ndim == 3:
            lim[dim] = math.gcd(lim[dim], arr.shape[2])
    tm, tn, tk = _tile(lim["m"], tm), _tile(lim["n"], tn), _tile(lim["k"], tk)
    nk = K // tk
    grid = (M // tm, N // tn, nk)
    a_spec = _spec(a, tk, tm, lambda i, j, k: (k, i)) if mode == "tn" else _spec(a, tm, tk, lambda i, j, k: (i, k))
    b_spec = _spec(b, tn, tk, lambda i, j, k: (j, k)) if mode == "nt" else _spec(b, tk, tn, lambda i, j, k: (k, j))
    dims = {"nn": ((1,), (0,)), "nt": ((1,), (1,)), "tn": ((0,), (0,))}[mode]
    ins, in_specs = [a, b], [a_spec, b_spec]
    for extra in (res, aux):
        if extra is not None:
            ins.append(extra)
            in_specs.append(_spec(extra, tm, tn, lambda i, j, k: (i, j)))
    if out_stack:
        o_shape = jax.ShapeDtypeStruct((out_stack, M, N // out_stack), out_dtype)
    else:
        o_shape = jax.ShapeDtypeStruct((M, N), out_dtype)
    o_spec = _spec(o_shape, tm, tn, lambda i, j, k: (i, j))
    n_out = 2 if epi == "relu2" else 1
    has_res, has_aux = res is not None, aux is not None

    def body(*refs):
        a_ref, b_ref = refs[0], refs[1]
        pos = 2
        res_ref = aux_ref = None
        if has_res:
            res_ref = refs[pos]
            pos += 1
        if has_aux:
            aux_ref = refs[pos]
            pos += 1
        outs = refs[pos:pos + n_out]
        part = lax.dot_general(a_ref[...].astype(BF16), b_ref[...].astype(BF16), (dims, ((), ())), preferred_element_type=F32)
        if nk > 1:
            acc_ref = refs[pos + n_out]
            k = pl.program_id(2)

            @pl.when(k == 0)
            def _():
                acc_ref[...] = part

            @pl.when(k > 0)
            def _():
                acc_ref[...] += part

        @pl.when(pl.program_id(2) == nk - 1)
        def _():
            o = acc_ref[...] if nk > 1 else part
            if has_res:
                o = o + res_ref[...].astype(F32)
            if epi == "relu2":
                outs[0][...] = o.astype(outs[0].dtype)
                r = jnp.maximum(o, 0.0)
                outs[1][...] = (r * r).astype(outs[1].dtype)
            elif epi == "relu2_bwd":
                outs[0][...] = (o * (2.0 * jnp.maximum(aux_ref[...].astype(F32), 0.0))).astype(outs[0].dtype)
            else:
                outs[0][...] = o.astype(outs[0].dtype)

    out = pl.pallas_call(
        body, grid=grid, in_specs=in_specs,
        out_specs=[o_spec] * n_out if n_out > 1 else o_spec,
        out_shape=[o_shape] * n_out if n_out > 1 else o_shape,
        scratch_shapes=[pltpu.VMEM((tm, tn), F32)] if nk > 1 else [],
        compiler_params=_cparams(("parallel", "parallel", "arbitrary"), VMEM_LIMIT), name=name,
    )(*ins)
    return out


def _rms_fwd(x, g, *, name, out_dtype=BF16):
    L, D = x.shape
    tr = _tile(L, 256)

    def body(x_ref, g_ref, o_ref):
        xv = x_ref[...]
        r = lax.rsqrt(jnp.mean(xv * xv, axis=-1, keepdims=True) + NORM_EPS)
        o_ref[...] = (xv * r * g_ref[...]).astype(o_ref.dtype)

    return pl.pallas_call(
        body, grid=(L // tr,),
        in_specs=[pl.BlockSpec((tr, D), lambda i: (i, 0)), pl.BlockSpec((1, D), lambda i: (0, 0))],
        out_specs=pl.BlockSpec((tr, D), lambda i: (i, 0)),
        out_shape=jax.ShapeDtypeStruct((L, D), out_dtype),
        compiler_params=_cparams(("parallel",)), name=name)(x, g)


def _rms_bwd(x, g, dy, dres, *, name):
    L, D = x.shape
    tr = _tile(L, 256)
    has_res = dres is not None

    def body(*refs):
        x_ref, g_ref, dy_ref = refs[:3]
        dres_ref = refs[3] if has_res else None
        dx_ref, dg_ref = refs[-2:]
        xv = x_ref[...]
        dyv = dy_ref[...].astype(F32)
        r = lax.rsqrt(jnp.mean(xv * xv, axis=-1, keepdims=True) + NORM_EPS)
        xh = xv * r
        dyg = dyv * g_ref[...]
        dx = r * (dyg - xh * jnp.mean(dyg * xh, axis=-1, keepdims=True))
        if has_res:
            dx = dx + dres_ref[...]
        dx_ref[...] = dx

        @pl.when(pl.program_id(0) == 0)
        def _():
            dg_ref[...] = jnp.zeros_like(dg_ref)

        dg_ref[...] += jnp.sum(dyv * xh, axis=0, keepdims=True)

    row = pl.BlockSpec((tr, D), lambda i: (i, 0))
    vec = pl.BlockSpec((1, D), lambda i: (0, 0))
    ins = [x, g, dy] + ([dres] if has_res else [])
    return pl.pallas_call(
        body, grid=(L // tr,), in_specs=[row, vec, row] + ([row] if has_res else []),
        out_specs=[row, vec],
        out_shape=[jax.ShapeDtypeStruct((L, D), F32), jax.ShapeDtypeStruct((1, D), F32)],
        compiler_params=_cparams(("arbitrary",)), name=name)(*ins)


def _loss_head(h, g, target, *, name):
    L, D = h.shape
    tr = _tile(L, 256)

    def body(x_ref, g_ref, t_ref, loss_ref, dx_ref, dg_ref):
        xv = x_ref[...]
        r = lax.rsqrt(jnp.mean(xv * xv, axis=-1, keepdims=True) + NORM_EPS)
        xh = xv * r
        err = xh * g_ref[...] - t_ref[...]
        dyv = err * (1.0 / D)
        dyg = dyv * g_ref[...]
        dx_ref[...] = r * (dyg - xh * jnp.mean(dyg * xh, axis=-1, keepdims=True))

        @pl.when(pl.program_id(0) == 0)
        def _():
            dg_ref[...] = jnp.zeros_like(dg_ref)
            loss_ref[...] = jnp.zeros_like(loss_ref)

        dg_ref[...] += jnp.sum(dyv * xh, axis=0, keepdims=True)
        loss_ref[...] += 0.5 * jnp.sum(jnp.sum(err * err, axis=1, keepdims=True), axis=0, keepdims=True) * (1.0 / D)

    row = pl.BlockSpec((tr, D), lambda i: (i, 0))
    vec = pl.BlockSpec((1, D), lambda i: (0, 0))
    return pl.pallas_call(
        body, grid=(L // tr,), in_specs=[row, vec, row],
        out_specs=[pl.BlockSpec((1, 1), lambda i: (0, 0)), row, vec],
        out_shape=[jax.ShapeDtypeStruct((1, 1), F32), jax.ShapeDtypeStruct((L, D), F32), jax.ShapeDtypeStruct((1, D), F32)],
        compiler_params=_cparams(("arbitrary",)), name=name)(h, g, target)


S5_TL = 256
S5_LW = 512


def _dot(a, b, dims):
    return lax.dot_general(a, b, (dims, ((), ())), preferred_element_type=F32)


_NN, _NT, _TN = ((1,), (0,)), ((1,), (1,)), ((0,), (0,))


def _gelu_and_grad(y):
    k = math.sqrt(2.0 / math.pi)
    y2 = y * y
    th = jnp.tanh(k * (y + 0.044715 * y * y2))
    g = 0.5 * y * (1.0 + th)
    gp = 0.5 * (1.0 + th) + 0.5 * y * (1.0 - th * th) * k * (1.0 + 3.0 * 0.044715 * y2)
    return g, gp


def _scan_tiles(re_s, im_s, ls, lw, pw_ref, tr_ref, ti_ref, carry_s, nt, reverse, extra=None):
    row = lax.broadcasted_iota(jnp.int32, (8, lw), 0)
    a = [pw_ref[k:k + 1, ls] for k in range(6)]
    tr_t, ti_t = tr_ref[:, ls], ti_ref[:, ls]
    edge = 0 if reverse else 7

    def tile(jj, carry):
        cr, ci, acc = carry
        j = (nt - 1 - jj) if reverse else jj
        off = pl.multiple_of(j * 8, 8)
        sr, si = re_s[pl.ds(off, 8), ls], im_s[pl.ds(off, 8), ls]
        for n, k in enumerate((1, 2, 4)):
            akr, aki = a[2 * n], a[2 * n + 1]
            if reverse:
                keep, sh = row < 8 - k, 8 - k
            else:
                keep, sh = row >= k, k
            shr = jnp.where(keep, pltpu.roll(sr, sh, 0), 0.0)
            shi = jnp.where(keep, pltpu.roll(si, sh, 0), 0.0)
            sr, si = sr + akr * shr - aki * shi, si + akr * shi + aki * shr
        xr = sr + tr_t * cr - ti_t * ci
        xi = si + tr_t * ci + ti_t * cr
        re_s[pl.ds(off, 8), ls] = xr
        im_s[pl.ds(off, 8), ls] = xi
        if extra is not None:
            acc = extra(off, xr, xi, acc, row)
        return xr[edge:edge + 1, :], xi[edge:edge + 1, :], acc

    acc0 = (jnp.zeros((8, lw), F32), jnp.zeros((8, lw), F32)) if extra is not None else 0
    cr, ci, acc = lax.fori_loop(0, nt, tile, (carry_s[0:1, ls], carry_s[1:2, ls], acc0))
    carry_s[0:1, ls] = cr
    carry_s[1:2, ls] = ci
    return acc


def _s5_fwd(proj, bdr, bdi, pw, tr, ti, cdr, cdi, dskip, wglu, *, name):
    L = proj.shape[0]
    GW, S = bdr.shape
    TL = _tile(L, S5_TL)
    lw = min(S, S5_LW)

    def body(u_ref, bdr_ref, bdi_ref, pw_ref, tr_ref, ti_ref, cdr_ref, cdi_ref, d_ref, w_ref,
             y_ref, xr_ref, xi_ref, re_s, im_s, carry_s):
        @pl.when(pl.program_id(0) == 0)
        def _():
            carry_s[...] = jnp.zeros_like(carry_s)

        u = u_ref[...]
        ub = u.astype(BF16)
        re_s[...] = _dot(ub, bdr_ref[...], _NN)
        im_s[...] = _dot(ub, bdi_ref[...], _NN)
        for lc in range(S // lw):
            _scan_tiles(re_s, im_s, slice(lc * lw, (lc + 1) * lw), lw, pw_ref, tr_ref, ti_ref, carry_s, TL // 8, False)
        xrb, xib = re_s[...].astype(BF16), im_s[...].astype(BF16)
        xr_ref[...] = xrb
        xi_ref[...] = xib
        y = _dot(xrb, cdr_ref[...], _NN) - _dot(xib, cdi_ref[...], _NN) + d_ref[...] * u
        g, _ = _gelu_and_grad(y)
        z = _dot(g.astype(BF16), w_ref[...], _NN)
        y_ref[...] = g * jax.nn.sigmoid(z)

    full = lambda arr: pl.BlockSpec(arr.shape, lambda i: (0,) * arr.ndim)
    return pl.pallas_call(
        body, grid=(L // TL,),
        in_specs=[pl.BlockSpec((TL, GW), lambda i: (i, 0))] + [full(t) for t in (bdr, bdi, pw, tr, ti, cdr, cdi, dskip, wglu)],
        out_specs=[pl.BlockSpec((TL, GW), lambda i: (i, 0)), pl.BlockSpec((TL, S), lambda i: (i, 0)), pl.BlockSpec((TL, S), lambda i: (i, 0))],
        out_shape=[jax.ShapeDtypeStruct((L, GW), F32), jax.ShapeDtypeStruct((L, S), BF16), jax.ShapeDtypeStruct((L, S), BF16)],
        scratch_shapes=[pltpu.VMEM((TL, S), F32), pltpu.VMEM((TL, S), F32), pltpu.VMEM((8, S), F32)],
        compiler_params=_cparams(("arbitrary",), VMEM_LIMIT), name=name,
    )(proj, bdr, bdi, pw, tr, ti, cdr, cdi, dskip, wglu)


def _s5_bwd(proj, xr, xi, dout, bdr, bdi, pwc, trc, tic, cdr, cdi, dskip, wglu, *, name):
    L = proj.shape[0]
    GW, S = bdr.shape
    TL = _tile(L, S5_TL)
    nc = L // TL
    lw = min(S, S5_LW)

    def body(u_ref, xr_ref, xi_ref, xrp_ref, xip_ref, do_ref, bdr_ref, bdi_ref, pw_ref, tr_ref, ti_ref, cdr_ref, cdi_ref,
             d_ref, w_ref, du_ref, dw_ref, dd_ref, dcdr_ref, dcdi_ref, dbdr_ref, dbdi_ref, dar_ref, dai_ref,
             gr_s, gi_s, xfr, xfi, carry_s):
        i = pl.program_id(0)

        @pl.when(i == 0)
        def _():
            carry_s[...] = jnp.zeros_like(carry_s)
            for r in (dw_ref, dd_ref, dcdr_ref, dcdi_ref, dbdr_ref, dbdi_ref, dar_ref, dai_ref):
                r[...] = jnp.zeros_like(r)

        u = u_ref[...]
        ub = u.astype(BF16)
        xrb, xib = xr_ref[...], xi_ref[...]
        y = _dot(xrb, cdr_ref[...], _NN) - _dot(xib, cdi_ref[...], _NN) + d_ref[...] * u
        g, gp = _gelu_and_grad(y)
        gb = g.astype(BF16)
        sg = jax.nn.sigmoid(_dot(gb, w_ref[...], _NN))
        dout = do_ref[...]
        dz = (dout * g * sg * (1.0 - sg)).astype(BF16)
        dg = dout * sg + _dot(dz, w_ref[...], _NT)
        dw_ref[...] += _dot(gb, dz, _TN)
        dy = dg * gp
        dyb = dy.astype(BF16)
        dd_ref[...] += jnp.sum(dy * u, axis=0, keepdims=True)
        gr_s[...] = _dot(dyb, cdr_ref[...], _NT)
        gi_s[...] = -_dot(dyb, cdi_ref[...], _NT)
        dcdr_ref[...] += _dot(xrb, dyb, _TN)
        dcdi_ref[...] -= _dot(xib, dyb, _TN)
        live = (i < nc - 1).astype(F32)
        xfr[0:8, :] = xrp_ref[...].astype(F32) * live
        xfi[0:8, :] = xip_ref[...].astype(F32) * live
        xfr[8:, :] = xrb.astype(F32)
        xfi[8:, :] = xib.astype(F32)
        for lc in range(S // lw):
            ls = slice(lc * lw, (lc + 1) * lw)

            def extra(off, lr, li, acc, row, ls=ls):
                cur_r, cur_i = xfr[pl.ds(off + 8, 8), ls], xfi[pl.ds(off + 8, 8), ls]
                prv_r, prv_i = xfr[pl.ds(off, 8), ls], xfi[pl.ds(off, 8), ls]
                xpr = jnp.where(row >= 1, pltpu.roll(cur_r, 1, 0), prv_r[7:8, :])
                xpi = jnp.where(row >= 1, pltpu.roll(cur_i, 1, 0), prv_i[7:8, :])
                return acc[0] + lr * xpr + li * xpi, acc[1] - lr * xpi + li * xpr

            acc = _scan_tiles(gr_s, gi_s, ls, lw, pw_ref, tr_ref, ti_ref, carry_s, TL // 8, True, extra)
            dar_ref[:, ls] += acc[0]
            dai_ref[:, ls] += acc[1]
        lrb, lib = gr_s[...].astype(BF16), gi_s[...].astype(BF16)
        du_ref[...] = dy * d_ref[...] + _dot(lrb, bdr_ref[...], _NT) + _dot(lib, bdi_ref[...], _NT)
        dbdr_ref[...] += _dot(ub, lrb, _TN)
        dbdi_ref[...] += _dot(ub, lib, _TN)

    rev = lambda i: nc - 1 - i
    full = lambda arr: pl.BlockSpec(arr.shape, lambda i: (0,) * arr.ndim)
    chunk = lambda w: pl.BlockSpec((TL, w), lambda i: (rev(i), 0))
    prev8 = pl.BlockSpec((8, S), lambda i: (jnp.maximum(rev(i) * (TL // 8) - 1, 0), 0))
    acc_shapes = [(GW, GW), (1, GW), (S, GW), (S, GW), (GW, S), (GW, S), (8, S), (8, S)]
    return pl.pallas_call(
        body, grid=(nc,),
        in_specs=[chunk(GW), chunk(S), chunk(S), prev8, prev8, chunk(GW)] + [full(t) for t in (bdr, bdi, pwc, trc, tic, cdr, cdi, dskip, wglu)],
        out_specs=[chunk(GW)] + [pl.BlockSpec(s, lambda i: (0, 0)) for s in acc_shapes],
        out_shape=[jax.ShapeDtypeStruct((L, GW), F32)] + [jax.ShapeDtypeStruct(s, F32) for s in acc_shapes],
        scratch_shapes=[pltpu.VMEM((TL, S), F32), pltpu.VMEM((TL, S), F32), pltpu.VMEM((TL + 8, S), F32),
                        pltpu.VMEM((TL + 8, S), F32), pltpu.VMEM((8, S), F32)],
        compiler_params=_cparams(("arbitrary",), VMEM_LIMIT), name=name,
    )(proj, xr, xi, xr, xi, dout, bdr, bdi, pwc, trc, tic, cdr, cdi, dskip, wglu)


def _cpow_tables(ar, ai):
    def cm(a, b):
        return a[0] * b[0] - a[1] * b[1], a[0] * b[1] + a[1] * b[0]
    p = [(ar, ai)]
    for _ in range(7):
        p.append(cm(p[-1], p[0]))
    z = jnp.zeros_like(ar)
    pw = jnp.stack([p[0][0], p[0][1], p[1][0], p[1][1], p[3][0], p[3][1], z, z])
    return pw, jnp.stack([q[0] for q in p]), jnp.stack([q[1] for q in p])


def _s5_disc(lam_re, lam_im, log_dt, b_re, b_im):
    dt = jnp.exp(log_dt)[:, None]
    mag = jnp.exp(lam_re * dt)
    ab_r, ab_i = mag * jnp.cos(lam_im * dt), mag * jnp.sin(lam_im * dt)
    den = lam_re * lam_re + lam_im * lam_im
    nr, ni = ab_r - 1.0, ab_i
    f_r = ((nr * lam_re + ni * lam_im) / den)[..., None]
    f_i = ((ni * lam_re - nr * lam_im) / den)[..., None]
    return ab_r, ab_i, f_r * b_re - f_i * b_im, f_r * b_im + f_i * b_re


def _bd(t):
    G, A, B = t.shape
    return (t[:, :, None, :] * jnp.eye(G, dtype=t.dtype)[:, None, :, None]).reshape(G * A, G * B)


def _bd_extract(m, G):
    A, B = m.shape[0] // G, m.shape[1] // G
    return jnp.sum(m.reshape(G, A, G, B) * jnp.eye(G, dtype=m.dtype)[:, None, :, None], axis=2)


POOL_TQ = 256


def _band(shape, row0, col0, win, transpose):
    r = lax.broadcasted_iota(jnp.int32, shape, 0) + row0
    c = lax.broadcasted_iota(jnp.int32, shape, 1) + col0
    d = (c - r) if transpose else (r - c)
    return ((d >= 0) & (d < win)).astype(BF16)


def _band_dot(band, v):
    hi = v.astype(BF16)
    lo = (v - hi.astype(F32)).astype(BF16)
    return _dot(band, hi, _NN) + _dot(band, lo, _NN)


def _pool_p(u_prev, u_cur, i, win, tq):
    t0 = i * tq
    cnt = jnp.minimum(lax.broadcasted_iota(jnp.int32, (tq, 1), 0) + t0 + 1, win).astype(F32)
    live = (i > 0).astype(F32)
    acc = _band_dot(_band((tq, tq), t0, t0, win, False), u_cur)
    acc += _band_dot(_band((tq, tq), t0, t0 - tq, win, False), u_prev * live)
    return acc / cnt - u_cur


def _pool_fwd(proj, pool_w, pool_scale3, *, name):
    L = proj.shape[0]
    nw, PC, _ = pool_w.shape
    tq = _tile(L, POOL_TQ)

    def body(up_ref, uc_ref, w_ref, s_ref, y_ref):
        g, i = pl.program_id(0), pl.program_id(1)
        win = jnp.left_shift(2, g)
        p = _pool_p(up_ref[...], uc_ref[...], i, win, tq)
        y_ref[...] = _dot(p.astype(BF16), w_ref[...].astype(BF16), _NN) * s_ref[...]

    return pl.pallas_call(
        body, grid=(nw, L // tq),
        in_specs=[pl.BlockSpec((tq, PC), lambda g, i: (jnp.maximum(i - 1, 0), nw + g)),
                  pl.BlockSpec((tq, PC), lambda g, i: (i, nw + g)),
                  pl.BlockSpec((None, PC, PC), lambda g, i: (g, 0, 0)),
                  pl.BlockSpec((None, 1, PC), lambda g, i: (g, 0, 0))],
        out_specs=pl.BlockSpec((tq, PC), lambda g, i: (i, g)),
        out_shape=jax.ShapeDtypeStruct((L, nw * PC), F32),
        compiler_params=_cparams(("parallel", "parallel")), name=name)(proj, proj, pool_w, pool_scale3)


def _pool_bwd(proj, dy, pool_w, pool_scale3, *, name):
    L = proj.shape[0]
    nw, PC, _ = pool_w.shape
    tq = _tile(L, POOL_TQ)
    nq = L // tq

    def body(up_ref, uc_ref, dyc_ref, dyn_ref, w_ref, s_ref, du_ref, dw_ref, ds_ref):
        g, i = pl.program_id(0), pl.program_id(1)
        win = jnp.left_shift(2, g)

        @pl.when(i == 0)
        def _():
            dw_ref[...] = jnp.zeros_like(dw_ref)
            ds_ref[...] = jnp.zeros_like(ds_ref)

        wb = w_ref[...].astype(BF16)
        p = _pool_p(up_ref[...], uc_ref[...], i, win, tq).astype(BF16)
        dyc = dyc_ref[...]
        ds_ref[...] += jnp.sum(dyc * _dot(p, wb, _NN), axis=0, keepdims=True)
        dys = (dyc * s_ref[...]).astype(BF16)
        dw_ref[...] += _dot(p, dys, _TN)
        t0 = i * tq
        rows = lax.broadcasted_iota(jnp.int32, (tq, 1), 0)
        dp_c = _dot(dys, wb, _NT)
        q_c = dp_c / jnp.minimum(rows + t0 + 1, win).astype(F32)
        live = (i < nq - 1).astype(F32)
        dp_n = _dot((dyn_ref[...] * s_ref[...] * live).astype(BF16), wb, _NT)
        q_n = dp_n / jnp.minimum(rows + t0 + tq + 1, win).astype(F32)
        du = _band_dot(_band((tq, tq), t0, t0, win, True), q_c) + _band_dot(_band((tq, tq), t0, t0 + tq, win, True), q_n)
        du_ref[...] = du - dp_c

    return pl.pallas_call(
        body, grid=(nw, nq),
        in_specs=[pl.BlockSpec((tq, PC), lambda g, i: (jnp.maximum(i - 1, 0), nw + g)),
                  pl.BlockSpec((tq, PC), lambda g, i: (i, nw + g)),
                  pl.BlockSpec((tq, PC), lambda g, i: (i, g)),
                  pl.BlockSpec((tq, PC), lambda g, i: (jnp.minimum(i + 1, nq - 1), g)),
                  pl.BlockSpec((None, PC, PC), lambda g, i: (g, 0, 0)),
                  pl.BlockSpec((None, 1, PC), lambda g, i: (g, 0, 0))],
        out_specs=[pl.BlockSpec((tq, PC), lambda g, i: (i, g)),
                   pl.BlockSpec((None, PC, PC), lambda g, i: (g, 0, 0)),
                   pl.BlockSpec((None, 1, PC), lambda g, i: (g, 0, 0))],
        out_shape=[jax.ShapeDtypeStruct((L, nw * PC), F32), jax.ShapeDtypeStruct((nw, PC, PC), F32),
                   jax.ShapeDtypeStruct((nw, 1, PC), F32)],
        compiler_params=_cparams(("parallel", "arbitrary")), name=name)(proj, proj, dy, dy, pool_w, pool_scale3)


CONV_RC = 256


def _conv1_fwd(proj, w_dw, b_dw, *, name):
    L = proj.shape[0]
    GW = w_dw.shape[1]
    CB = min(GW, 128)
    nb = GW // CB
    RC = _tile(L, CONV_RC)
    n = RC + CONV_PAD

    def body(val_ref, gate_ref, w_ref, b_ref, o_ref, hp):
        hp[0:CONV_PAD, :] = jnp.zeros((CONV_PAD, CB), F32)
        hp[CONV_PAD:, :] = val_ref[...] * jax.nn.sigmoid(gate_ref[...])
        wv = w_ref[...]

        def chunk(ci, carry):
            c0 = pl.multiple_of(ci * RC, 8)
            win = hp[pl.ds(c0, n), :]
            acc = jnp.zeros((RC, CB), F32) + b_ref[...]
            for k in range(CONV_WIDTH):
                acc = acc + wv[k:k + 1, :] * pltpu.roll(win, n - (k + 2), 0)[0:RC]
            o_ref[pl.ds(c0, RC), :] = acc
            return carry

        lax.fori_loop(0, L // RC, chunk, 0)

    col = lambda off: pl.BlockSpec((L, CB), lambda c: (0, off * nb + c))
    return pl.pallas_call(
        body, grid=(nb,),
        in_specs=[col(2), col(3), pl.BlockSpec((CONV_PAD, CB), lambda c: (0, c)), pl.BlockSpec((1, CB), lambda c: (0, c))],
        out_specs=pl.BlockSpec((L, CB), lambda c: (0, c)),
        out_shape=jax.ShapeDtypeStruct((L, GW), F32),
        scratch_shapes=[pltpu.VMEM((L + CONV_PAD, CB), F32)],
        compiler_params=_cparams(("parallel",)), name=name)(proj, proj, w_dw, b_dw)


def _conv1_bwd(proj, dhc, w_dw, *, name):
    L = proj.shape[0]
    GW = w_dw.shape[1]
    CB = min(GW, 128)
    nb = GW // CB
    RC = _tile(L, CONV_RC)
    n = RC + CONV_PAD

    def body(val_ref, gate_ref, d_ref, w_ref, dval_ref, dgate_ref, dw_ref, db_ref, hp, dp):
        val = val_ref[...]
        sg = jax.nn.sigmoid(gate_ref[...])
        hp[0:CONV_PAD, :] = jnp.zeros((CONV_PAD, CB), F32)
        hp[CONV_PAD:, :] = val * sg
        dp[0:L, :] = d_ref[...]
        dp[L:, :] = jnp.zeros((CONV_PAD, CB), F32)
        dw_ref[...] = jnp.zeros_like(dw_ref)
        db_ref[...] = jnp.sum(d_ref[...], axis=0, keepdims=True)
        wv = w_ref[...]

        def chunk(ci, carry):
            c0 = pl.multiple_of(ci * RC, 8)
            win = hp[pl.ds(c0, n), :]
            dwin = dp[pl.ds(c0, n), :]
            d = dwin[0:RC]
            dh = jnp.zeros((RC, CB), F32)
            for k in range(CONV_WIDTH):
                dw_ref[k:k + 1, :] += jnp.sum(d * pltpu.roll(win, n - (k + 2), 0)[0:RC], axis=0, keepdims=True)
                sh = CONV_WIDTH - 1 - k
                dh = dh + wv[k:k + 1, :] * (pltpu.roll(dwin, n - sh, 0)[0:RC] if sh else d)
            v, s = val_ref[pl.ds(c0, RC), :], jax.nn.sigmoid(gate_ref[pl.ds(c0, RC), :])
            dval_ref[pl.ds(c0, RC), :] = dh * s
            dgate_ref[pl.ds(c0, RC), :] = dh * v * s * (1.0 - s)
            return carry

        lax.fori_loop(0, L // RC, chunk, 0)

    col = lambda off: pl.BlockSpec((L, CB), lambda c: (0, off * nb + c))
    own = pl.BlockSpec((L, CB), lambda c: (0, c))
    return pl.pallas_call(
        body, grid=(nb,),
        in_specs=[col(2), col(3), own, pl.BlockSpec((CONV_PAD, CB), lambda c: (0, c))],
        out_specs=[own, own, pl.BlockSpec((CONV_PAD, CB), lambda c: (0, c)), pl.BlockSpec((1, CB), lambda c: (0, c))],
        out_shape=[jax.ShapeDtypeStruct((L, GW), F32), jax.ShapeDtypeStruct((L, GW), F32),
                   jax.ShapeDtypeStruct((CONV_PAD, GW), F32), jax.ShapeDtypeStruct((1, GW), F32)],
        scratch_shapes=[pltpu.VMEM((L + CONV_PAD, CB), F32), pltpu.VMEM((L + CONV_PAD, CB), F32)],
        compiler_params=_cparams(("parallel",)), name=name)(proj, proj, dhc, w_dw)


def _ln_silu(hc, g, b):
    mu = jnp.mean(hc, axis=-1, keepdims=True)
    xc = hc - mu
    r = lax.rsqrt(jnp.mean(xc * xc, axis=-1, keepdims=True) + NORM_EPS)
    xh = xc * r
    a = xh * g + b
    sg = jax.nn.sigmoid(a)
    return xh, r, a, sg


def _conv2_fwd(hc, ln_g, ln_b, w_pw, *, name):
    L, GW = hc.shape
    tr = _tile(L, 256)

    def body(h_ref, g_ref, b_ref, w_ref, y_ref):
        _, _, a, sg = _ln_silu(h_ref[...], g_ref[...], b_ref[...])
        y_ref[...] = _dot((a * sg).astype(BF16), w_ref[...], _NN)

    row = pl.BlockSpec((tr, GW), lambda i: (i, 0))
    vec = pl.BlockSpec((1, GW), lambda i: (0, 0))
    return pl.pallas_call(
        body, grid=(L // tr,), in_specs=[row, vec, vec, pl.BlockSpec((GW, GW), lambda i: (0, 0))],
        out_specs=row, out_shape=jax.ShapeDtypeStruct((L, GW), F32),
        compiler_params=_cparams(("parallel",)), name=name)(hc, ln_g, ln_b, w_pw)


def _conv2_bwd(hc, ln_g, ln_b, w_pw, dy, *, name):
    L, GW = hc.shape
    tr = _tile(L, 256)

    def body(h_ref, g_ref, b_ref, w_ref, dy_ref, dh_ref, dg_ref, db_ref, dw_ref):
        @pl.when(pl.program_id(0) == 0)
        def _():
            for r in (dg_ref, db_ref, dw_ref):
                r[...] = jnp.zeros_like(r)

        xh, r, a, sg = _ln_silu(h_ref[...], g_ref[...], b_ref[...])
        dyb = dy_ref[...].astype(BF16)
        dw_ref[...] += _dot((a * sg).astype(BF16), dyb, _TN)
        da = _dot(dyb, w_ref[...], _NT) * (sg + a * sg * (1.0 - sg))
        dg_ref[...] += jnp.sum(da * xh, axis=0, keepdims=True)
        db_ref[...] += jnp.sum(da, axis=0, keepdims=True)
        dxh = da * g_ref[...]
        dh_ref[...] = r * (dxh - jnp.mean(dxh, axis=-1, keepdims=True) - xh * jnp.mean(dxh * xh, axis=-1, keepdims=True))

    row = pl.BlockSpec((tr, GW), lambda i: (i, 0))
    vec = pl.BlockSpec((1, GW), lambda i: (0, 0))
    mat = pl.BlockSpec((GW, GW), lambda i: (0, 0))
    return pl.pallas_call(
        body, grid=(L // tr,), in_specs=[row, vec, vec, mat, row],
        out_specs=[row, vec, vec, mat],
        out_shape=[jax.ShapeDtypeStruct((L, GW), F32), jax.ShapeDtypeStruct((1, GW), F32), jax.ShapeDtypeStruct((1, GW), F32),
                   jax.ShapeDtypeStruct((GW, GW), F32)],
        compiler_params=_cparams(("arbitrary",)), name=name)(hc, ln_g, ln_b, w_pw, dy)


def _grp_fwd(ys, g, *, name):
    L, GW = ys[0].shape
    tr = _tile(L, 256)

    def body(*refs):
        g_ref, o_ref = refs[4], refs[5]
        for m in range(N_MIXERS):
            yv = refs[m][...]
            r = lax.rsqrt(jnp.mean(yv * yv, axis=-1, keepdims=True) + NORM_EPS)
            o_ref[:, m * GW:(m + 1) * GW] = (yv * r * g_ref[:, m * GW:(m + 1) * GW]).astype(o_ref.dtype)

    row = pl.BlockSpec((tr, GW), lambda i: (i, 0))
    return pl.pallas_call(
        body, grid=(L // tr,), in_specs=[row] * 4 + [pl.BlockSpec((1, N_MIXERS * GW), lambda i: (0, 0))],
        out_specs=pl.BlockSpec((tr, N_MIXERS * GW), lambda i: (i, 0)),
        out_shape=jax.ShapeDtypeStruct((L, N_MIXERS * GW), BF16),
        compiler_params=_cparams(("parallel",)), name=name)(*ys, g)


def _grp_bwd(ys, g, dy, *, name):
    L, GW = ys[0].shape
    tr = _tile(L, 256)

    def body(*refs):
        g_ref, dy_ref = refs[4], refs[5]
        outs, dg_ref = refs[6:10], refs[10]

        @pl.when(pl.program_id(0) == 0)
        def _():
            dg_ref[...] = jnp.zeros_like(dg_ref)

        for m in range(N_MIXERS):
            cs = slice(m * GW, (m + 1) * GW)
            yv = refs[m][...]
            r = lax.rsqrt(jnp.mean(yv * yv, axis=-1, keepdims=True) + NORM_EPS)
            xh = yv * r
            dyv = dy_ref[:, cs]
            dyg = dyv * g_ref[:, cs]
            outs[m][...] = r * (dyg - xh * jnp.mean(dyg * xh, axis=-1, keepdims=True))
            dg_ref[:, cs] += jnp.sum(dyv * xh, axis=0, keepdims=True)

    row = pl.BlockSpec((tr, GW), lambda i: (i, 0))
    wide = pl.BlockSpec((tr, N_MIXERS * GW), lambda i: (i, 0))
    vec = pl.BlockSpec((1, N_MIXERS * GW), lambda i: (0, 0))
    return pl.pallas_call(
        body, grid=(L // tr,), in_specs=[row] * 4 + [vec, wide],
        out_specs=[row] * 4 + [vec],
        out_shape=[jax.ShapeDtypeStruct((L, GW), F32)] * 4 + [jax.ShapeDtypeStruct((1, N_MIXERS * GW), F32)],
        compiler_params=_cparams(("arbitrary",)), name=name)(*ys, g, dy)


def _att_first(j, br, nblk):
    per = lax.shift_right_logical(jnp.int32(nblk), 2 * br)
    return jnp.bitwise_and(j, per - 1) == 0


def _att_fwd(q, k, vx, bias, *, name):
    nbr, H, L, E = q.shape
    B = ATT_BLOCK
    nblk = L // B

    def body(q_ref, k_ref, v_ref, b_ref, o_ref):
        br = pl.program_id(0)
        lane = lax.broadcasted_iota(jnp.int32, (B, 2 * E), 1)

        def blk(j, carry):
            off = pl.multiple_of(j * B, B)
            poff = pl.multiple_of(jnp.maximum(j - 1, 0) * B, B)
            qv = q_ref[pl.ds(off, B), :]
            s_p = _dot(qv, k_ref[pl.ds(poff, B), :], _NT) + b_ref[:, 0:B]
            s_c = _dot(qv, k_ref[pl.ds(off, B), :], _NT) + b_ref[:, B:2 * B]
            s_p = jnp.where(_att_first(j, br, nblk), NEG_INF, s_p)
            m = jnp.maximum(jnp.max(s_p, axis=-1, keepdims=True), jnp.max(s_c, axis=-1, keepdims=True))
            p_p, p_c = jnp.exp(s_p - m), jnp.exp(s_c - m)
            den = jnp.sum(p_p, axis=-1, keepdims=True) + jnp.sum(p_c, axis=-1, keepdims=True)
            acc = _dot(p_p.astype(BF16), v_ref[pl.ds(poff, B), :], _NN) + _dot(p_c.astype(BF16), v_ref[pl.ds(off, B), :], _NN)
            o_ref[pl.ds(off, B), :] = jnp.where(lane < E, acc / den, m + jnp.log(den))
            return carry

        lax.fori_loop(0, nblk, blk, 0)

    seq = lambda w: pl.BlockSpec((None, None, L, w), lambda b, h: (b, h, 0, 0))
    return pl.pallas_call(
        body, grid=(nbr, H),
        in_specs=[seq(E), seq(E), seq(2 * E), pl.BlockSpec((None, None, B, 2 * B), lambda b, h: (b, h, 0, 0))],
        out_specs=seq(2 * E), out_shape=jax.ShapeDtypeStruct((nbr, H, L, 2 * E), F32),
        compiler_params=_cparams(("parallel", "parallel")), name=name)(q, k, vx, bias)


def _att_bwd(q, k, vx, bias, ox, dox, *, name):
    nbr, H, L, E = q.shape
    B = ATT_BLOCK
    nblk = L // B

    def body(q_ref, k_ref, v_ref, b_ref, o_ref, do_ref, dq_ref, dk_ref, dv_ref, db_ref):
        br = pl.program_id(0)
        dk_ref[...] = jnp.zeros_like(dk_ref)
        dv_ref[...] = jnp.zeros_like(dv_ref)
        db_ref[...] = jnp.zeros_like(db_ref)
        lane = lax.broadcasted_iota(jnp.int32, (B, 2 * E), 1)

        def blk(j, carry):
            off = pl.multiple_of(j * B, B)
            poff = pl.multiple_of(jnp.maximum(j - 1, 0) * B, B)
            qv, kc, kp = q_ref[pl.ds(off, B), :], k_ref[pl.ds(off, B), :], k_ref[pl.ds(poff, B), :]
            vc, vp = v_ref[pl.ds(off, B), :], v_ref[pl.ds(poff, B), :]
            oe, de = o_ref[pl.ds(off, B), :], do_ref[pl.ds(off, B), :]
            lse, dlse = oe[:, E:E + 1], de[:, E:E + 1]
            do = jnp.where(lane < E, de, 0.0)
            dm = jnp.sum(do * oe, axis=-1, keepdims=True) - dlse
            dob = do.astype(BF16)
            s_p = _dot(qv, kp, _NT) + b_ref[:, 0:B]
            s_c = _dot(qv, kc, _NT) + b_ref[:, B:2 * B]
            s_p = jnp.where(_att_first(j, br, nblk), NEG_INF, s_p)
            p_p, p_c = jnp.exp(s_p - lse), jnp.exp(s_c - lse)
            ds_p = p_p * (_dot(dob, vp, _NT) - dm)
            ds_c = p_c * (_dot(dob, vc, _NT) - dm)
            db_ref[:, 0:B] += ds_p
            db_ref[:, B:2 * B] += ds_c
            dsb_p, dsb_c = ds_p.astype(BF16), ds_c.astype(BF16)
            dq_ref[pl.ds(off, B), :] = _dot(dsb_p, kp, _NN) + _dot(dsb_c, kc, _NN)
            dk_ref[pl.ds(poff, B), :] += _dot(dsb_p, qv, _TN)
            dk_ref[pl.ds(off, B), :] += _dot(dsb_c, qv, _TN)
            dv_ref[pl.ds(poff, B), :] += _dot(p_p.astype(BF16), dob, _TN)
            dv_ref[pl.ds(off, B), :] += _dot(p_c.astype(BF16), dob, _TN)
            return carry

        lax.fori_loop(0, nblk, blk, 0)

    seq = lambda w: pl.BlockSpec((None, None, L, w), lambda b, h: (b, h, 0, 0))
    bsp = pl.BlockSpec((None, None, B, 2 * B), lambda b, h: (b, h, 0, 0))
    return pl.pallas_call(
        body, grid=(nbr, H),
        in_specs=[seq(E), seq(E), seq(2 * E), bsp, seq(2 * E), seq(2 * E)],
        out_specs=[seq(E), seq(E), seq(2 * E), bsp],
        out_shape=[jax.ShapeDtypeStruct((nbr, H, L, E), F32), jax.ShapeDtypeStruct((nbr, H, L, E), F32),
                   jax.ShapeDtypeStruct((nbr, H, L, 2 * E), F32), jax.ShapeDtypeStruct((nbr, H, B, 2 * B), F32)],
        compiler_params=_cparams(("parallel", "parallel")), name=name)(q, k, vx, bias, ox, dox)


def _mix_weights(xs, lane, E, W):
    al = [jnp.where(lane < E, pltpu.roll(x, W - E, 1), x) for x in xs]
    m = functools.reduce(jnp.maximum, al)
    es = [jnp.exp(a - m) for a in al]
    tot = functools.reduce(lambda a, b: a + b, es)
    return [e / tot for e in es]


def _mix_fwd(ox, *, name):
    nbr, L, W = ox.shape
    E = W // ATT_HEADS // 2
    tr = _tile(L, 256)

    def body(x_ref, o_ref):
        lane = lax.broadcasted_iota(jnp.int32, (tr, W), 1) % (2 * E)
        xs = [x_ref[b] for b in range(nbr)]
        ws = _mix_weights(xs, lane, E, W)
        o_ref[...] = functools.reduce(lambda a, b: a + b, [w * x for w, x in zip(ws, xs)])

    return pl.pallas_call(
        body, grid=(L // tr,), in_specs=[pl.BlockSpec((nbr, tr, W), lambda i: (0, i, 0))],
        out_specs=pl.BlockSpec((tr, W), lambda i: (i, 0)), out_shape=jax.ShapeDtypeStruct((L, W), F32),
        compiler_params=_cparams(("parallel",)), name=name)(ox)


def _mix_bwd(ox, dy, *, name):
    nbr, L, W = ox.shape
    E = W // ATT_HEADS // 2
    tr = _tile(L, 256)

    def body(x_ref, dy_ref, o_ref):
        lane = lax.broadcasted_iota(jnp.int32, (tr, W), 1) % (2 * E)
        xs = [x_ref[b] for b in range(nbr)]
        ws = _mix_weights(xs, lane, E, W)
        mix = functools.reduce(lambda a, b: a + b, [w * x for w, x in zip(ws, xs)])
        dyv = jnp.where(lane < E, dy_ref[...], 0.0)
        r = lax.broadcasted_iota(jnp.int32, (W, W), 0)
        c = lax.broadcasted_iota(jnp.int32, (W, W), 1)
        seg = ((r // (2 * E) == c // (2 * E)) & (r % (2 * E) < E)).astype(F32)
        for b in range(nbr):
            t = dyv * (xs[b] - mix)
            dl = ws[b] * jnp.dot(t, seg, preferred_element_type=F32, precision=lax.Precision.HIGHEST)
            o_ref[b] = jnp.where(lane < E, ws[b] * dyv, dl)

    return pl.pallas_call(
        body, grid=(L // tr,),
        in_specs=[pl.BlockSpec((nbr, tr, W), lambda i: (0, i, 0)), pl.BlockSpec((tr, W), lambda i: (i, 0))],
        out_specs=pl.BlockSpec((nbr, tr, W), lambda i: (0, i, 0)), out_shape=jax.ShapeDtypeStruct((nbr, L, W), F32),
        compiler_params=_cparams(("parallel",), VMEM_LIMIT), name=name)(ox, dy)


def _datt_lanes(GW):
    E = GW // ATT_HEADS
    lb = min(GW, 128)
    return lb, lb // E, E


def _datt_rows(c, br, nblk):
    dil = 4 ** br
    nbs = nblk // dil
    r, jb = c // nbs, c % nbs
    return r + dil * ATT_BLOCK * jb, r + dil * ATT_BLOCK * jnp.maximum(jb - 1, 0), jb == 0


def _datt_fwd(proj, bias, *, name):
    L = proj.shape[0]
    GW = proj.shape[1] // 7
    lb, hps, E = _datt_lanes(GW)
    B = ATT_BLOCK
    nblk = L // B
    scale = E ** -0.5

    def body(q_ref, k_ref, v_ref, b_ref, y_ref, lse_ref, m_s, l_s):
        m_s[...] = jnp.full(m_s.shape, NEG_INF, F32)
        l_s[...] = jnp.zeros(l_s.shape, F32)
        y_ref[...] = jnp.zeros(y_ref.shape, F32)
        lane = lax.broadcasted_iota(jnp.int32, (B, lb), 1) // E

        for br, (_, dil) in enumerate(DILATED_PATTERNS):
            def combo(c, carry, br=br, dil=dil):
                start, pstart, first = _datt_rows(c, br, nblk)
                rows, prows = pl.ds(start, B, stride=dil), pl.ds(pstart, B, stride=dil)
                q = q_ref[rows, :] * scale
                kc, kp = k_ref[rows, :].astype(BF16), k_ref[prows, :].astype(BF16)
                vc, vp = v_ref[rows, :].astype(BF16), v_ref[prows, :].astype(BF16)
                m_old, l_old, a_old = m_s[rows, :], l_s[rows, :], y_ref[rows, :]
                m_new, l_new, a_new = m_old, l_old, a_old
                for a in range(hps):
                    sel = lane == a
                    qm = jnp.where(sel, q, 0.0).astype(BF16)
                    s_p = _dot(qm, kp, _NT) + b_ref[br, a, :, 0:B]
                    s_c = _dot(qm, kc, _NT) + b_ref[br, a, :, B:2 * B]
                    s_p = jnp.where(first, NEG_INF, s_p)
                    mo, lo = m_old[:, a * E:a * E + 1], l_old[:, a * E:a * E + 1]
                    mn = jnp.maximum(mo, jnp.maximum(jnp.max(s_p, axis=-1, keepdims=True), jnp.max(s_c, axis=-1, keepdims=True)))
                    alpha = jnp.exp(mo - mn)
                    p_p, p_c = jnp.exp(s_p - mn), jnp.exp(s_c - mn)
                    ln = alpha * lo + jnp.sum(p_p, axis=-1, keepdims=True) + jnp.sum(p_c, axis=-1, keepdims=True)
                    pv = _dot(p_p.astype(BF16), vp, _NN) + _dot(p_c.astype(BF16), vc, _NN)
                    m_new = jnp.where(sel, mn, m_new)
                    l_new = jnp.where(sel, ln, l_new)
                    a_new = jnp.where(sel, alpha * a_old + pv, a_new)
                m_s[rows, :] = m_new
                l_s[rows, :] = l_new
                y_ref[rows, :] = a_new
                return carry

            lax.fori_loop(0, nblk, combo, 0)
        y_ref[...] = y_ref[...] / l_s[...]
        lse_ref[...] = m_s[...] + jnp.log(l_s[...])

    col = lambda off: pl.BlockSpec((L, lb), lambda h: (0, off * (GW // lb) + h))
    own = pl.BlockSpec((L, lb), lambda h: (0, h))
    return pl.pallas_call(
        body, grid=(GW // lb,),
        in_specs=[col(4), col(5), col(6), pl.BlockSpec((len(DILATED_PATTERNS), hps, B, 2 * B), lambda h: (0, h, 0, 0))],
        out_specs=[own, own], out_shape=[jax.ShapeDtypeStruct((L, GW), F32)] * 2,
        scratch_shapes=[pltpu.VMEM((L, lb), F32), pltpu.VMEM((L, lb), F32)],
        compiler_params=_cparams(("parallel",), VMEM_LIMIT), name=name)(proj, proj, proj, bias)


def _datt_bwd(proj, bias, y, lse, dy, *, name):
    L = proj.shape[0]
    GW = proj.shape[1] // 7
    lb, hps, E = _datt_lanes(GW)
    B = ATT_BLOCK
    nblk = L // B
    scale = E ** -0.5

    def body(q_ref, k_ref, v_ref, b_ref, y_ref, lse_ref, dy_ref, dq_ref, dk_ref, dv_ref, db_ref):
        for r in (dq_ref, dk_ref, dv_ref, db_ref):
            r[...] = jnp.zeros(r.shape, F32)
        lane = lax.broadcasted_iota(jnp.int32, (B, lb), 1) // E

        for br, (_, dil) in enumerate(DILATED_PATTERNS):
            def combo(c, carry, br=br, dil=dil):
                start, pstart, first = _datt_rows(c, br, nblk)
                rows, prows = pl.ds(start, B, stride=dil), pl.ds(pstart, B, stride=dil)
                q = q_ref[rows, :] * scale
                kc, kp = k_ref[rows, :].astype(BF16), k_ref[prows, :].astype(BF16)
                vc, vp = v_ref[rows, :].astype(BF16), v_ref[prows, :].astype(BF16)
                dyr, lser = dy_ref[rows, :], lse_ref[rows, :]
                dyy = dyr * y_ref[rows, :]
                dq = jnp.zeros((B, lb), F32)
                dkc, dkp, dvc, dvp = dq, dq, dq, dq
                for a in range(hps):
                    sel = lane == a
                    qm = jnp.where(sel, q, 0.0).astype(BF16)
                    dym = jnp.where(sel, dyr, 0.0).astype(BF16)
                    dm = jnp.sum(jnp.where(sel, dyy, 0.0), axis=-1, keepdims=True)
                    lse_a = lser[:, a * E:a * E + 1]
                    s_p = _dot(qm, kp, _NT) + b_ref[br, a, :, 0:B]
                    s_c = _dot(qm, kc, _NT) + b_ref[br, a, :, B:2 * B]
                    s_p = jnp.where(first, NEG_INF, s_p)
                    p_p, p_c = jnp.exp(s_p - lse_a), jnp.exp(s_c - lse_a)
                    ds_p = p_p * (_dot(dym, vp, _NT) - dm)
                    ds_c = p_c * (_dot(dym, vc, _NT) - dm)
                    db_ref[br, a, :, 0:B] += ds_p
                    db_ref[br, a, :, B:2 * B] += ds_c
                    dsb_p, dsb_c = ds_p.astype(BF16), ds_c.astype(BF16)
                    dq = dq + jnp.where(sel, _dot(dsb_p, kp, _NN) + _dot(dsb_c, kc, _NN), 0.0)
                    dkp = dkp + _dot(dsb_p, qm, _TN)
                    dkc = dkc + _dot(dsb_c, qm, _TN)
                    dvp = dvp + _dot(p_p.astype(BF16), dym, _TN)
                    dvc = dvc + _dot(p_c.astype(BF16), dym, _TN)
                dq_ref[rows, :] += dq * scale
                dk_ref[prows, :] += dkp
                dk_ref[rows, :] += dkc
                dv_ref[prows, :] += dvp
                dv_ref[rows, :] += dvc
                return carry

            lax.fori_loop(0, nblk, combo, 0)

    col = lambda off: pl.BlockSpec((L, lb), lambda h: (0, off * (GW // lb) + h))
    own = pl.BlockSpec((L, lb), lambda h: (0, h))
    bsp = pl.BlockSpec((len(DILATED_PATTERNS), hps, B, 2 * B), lambda h: (0, h, 0, 0))
    return pl.pallas_call(
        body, grid=(GW // lb,),
        in_specs=[col(4), col(5), col(6), bsp, own, own, own], out_specs=[own, own, own, bsp],
        out_shape=[jax.ShapeDtypeStruct((L, GW), F32)] * 3 + [jax.ShapeDtypeStruct((len(DILATED_PATTERNS), ATT_HEADS, B, 2 * B), F32)],
        compiler_params=_cparams(("parallel",), VMEM_LIMIT), name=name)(proj, proj, proj, bias, y, lse, dy)


def _t5_bucket(dist):
    n = np.maximum(dist, 0)
    max_exact = REL_BUCKETS // 2
    large = max_exact + (np.log(np.maximum(n, 1) / max_exact) / np.log(REL_MAX_DIST / max_exact)
                         * (REL_BUCKETS - max_exact)).astype(np.int64)
    large = np.minimum(large, REL_BUCKETS - 1)
    return np.where(n < max_exact, n, large).astype(np.int32)


def _att_tables():
    a = np.arange(ATT_BLOCK)[:, None]
    b = np.arange(2 * ATT_BLOCK)[None, :]
    sub = a + ATT_BLOCK - b
    buckets, valids = [], []
    for window, dil in DILATED_PATTERNS:
        buckets.append(_t5_bucket(sub * dil))
        valids.append((sub >= 0) & (sub <= window // dil))
    return np.stack(buckets), np.stack(valids)


def _to_branches(t, H):
    L = t.shape[0]
    E = t.shape[1] // H
    out = []
    for _, dil in DILATED_PATTERNS:
        out.append(t.reshape(L // dil, dil, H, E).transpose(2, 1, 0, 3).reshape(H, L, E))
    return jnp.stack(out)


def _from_branches(t):
    _, H, L, E = t.shape
    out = []
    for b, (_, dil) in enumerate(DILATED_PATTERNS):
        out.append(t[b].reshape(H, dil, L // dil, E).transpose(2, 1, 0, 3).reshape(L, H * E))
    return jnp.stack(out)


def _xatt_fwd(q, k, v, *, name):
    L, XW = q.shape
    M = k.shape[0]
    tr = _tile(L, 512)
    scale = X_HEAD_DIM ** -0.5

    def body(q_ref, k_ref, v_ref, o_ref):
        s = _dot(q_ref[...], k_ref[...], _NT) * scale
        p = jnp.exp(s - jnp.max(s, axis=-1, keepdims=True))
        den = jnp.sum(p, axis=-1, keepdims=True)
        o_ref[...] = (_dot(p.astype(BF16), v_ref[...], _NN) / den).astype(o_ref.dtype)

    qs = pl.BlockSpec((tr, X_HEAD_DIM), lambda h, i: (i, h))
    ks = pl.BlockSpec((M, X_HEAD_DIM), lambda h, i: (0, h))
    return pl.pallas_call(
        body, grid=(XW // X_HEAD_DIM, L // tr), in_specs=[qs, ks, ks], out_specs=qs,
        out_shape=jax.ShapeDtypeStruct((L, XW), BF16),
        compiler_params=_cparams(("parallel", "parallel")), name=name)(q, k, v)


def _xatt_bwd(q, k, v, do, *, name):
    L, XW = q.shape
    M = k.shape[0]
    tr = _tile(L, 512)
    scale = X_HEAD_DIM ** -0.5

    def body(q_ref, k_ref, v_ref, do_ref, dq_ref, dk_ref, dv_ref):
        @pl.when(pl.program_id(1) == 0)
        def _():
            dk_ref[...] = jnp.zeros_like(dk_ref)
            dv_ref[...] = jnp.zeros_like(dv_ref)

        qv, kv, vv = q_ref[...], k_ref[...], v_ref[...]
        s = _dot(qv, kv, _NT) * scale
        p = jnp.exp(s - jnp.max(s, axis=-1, keepdims=True))
        p = p / jnp.sum(p, axis=-1, keepdims=True)
        dob = do_ref[...].astype(BF16)
        pb = p.astype(BF16)
        dv_ref[...] += _dot(pb, dob, _TN)
        dp = _dot(dob, vv, _NT)
        ds = (p * (dp - jnp.sum(dp * p, axis=-1, keepdims=True)) * scale).astype(BF16)
        dq_ref[...] = _dot(ds, kv, _NN)
        dk_ref[...] += _dot(ds, qv, _TN)

    qs = pl.BlockSpec((tr, X_HEAD_DIM), lambda h, i: (i, h))
    ks = pl.BlockSpec((M, X_HEAD_DIM), lambda h, i: (0, h))
    return pl.pallas_call(
        body, grid=(XW // X_HEAD_DIM, L // tr), in_specs=[qs, ks, ks, qs], out_specs=[qs, ks, ks],
        out_shape=[jax.ShapeDtypeStruct((L, XW), F32), jax.ShapeDtypeStruct((M, XW), F32), jax.ShapeDtypeStruct((M, XW), F32)],
        compiler_params=_cparams(("parallel", "arbitrary")), name=name)(q, k, v, do)


_ANY = pl.BlockSpec(memory_space=pl.ANY)


def _place():
    mx, my, mc = lax.axis_index("x"), lax.axis_index("y"), lax.axis_index("c")
    chips = [(1 - mx, my), (mx, 1 - my), (1 - mx, 1 - my)]
    return mx, my, mc, chips


def _rcopy(src, dst, ssem, rsem, dev):
    return pltpu.make_async_remote_copy(src_ref=src, dst_ref=dst, send_sem=ssem, recv_sem=rsem, device_id=dev,
                                        device_id_type=MESH)


STAGE_BYTES = 2 * 1024 * 1024


def _staged_copy(pairs, buf, isem, osem):
    n = len(pairs)
    ins = [pltpu.make_async_copy(src, buf.at[i % 2], isem.at[i % 2]) for i, (src, _) in enumerate(pairs)]
    outs = [pltpu.make_async_copy(buf.at[i % 2], dst, osem.at[i % 2]) for i, (_, dst) in enumerate(pairs)]
    ins[0].start()
    for i in range(n):
        if i + 1 < n:
            if i >= 1:
                outs[i - 1].wait()
            ins[i + 1].start()
        ins[i].wait()
        outs[i].start()
    for i in range(max(n - 2, 0), n):
        outs[i].wait()


_HBM = pl.BlockSpec(memory_space=pltpu.HBM)
_SEMS = pl.BlockSpec(memory_space=pltpu.SEMAPHORE)
_VM = pl.BlockSpec(memory_space=pltpu.VMEM)
_DATAFLOW = pltpu.SideEffectType.DATAFLOW_SIDE_EFFECTING


def _ag_copies(x_refs, land_refs, ssem, rsem, inbound, which=None):
    mx, my, mc, chips = _place()
    me = 2 * mx + my
    cps = []
    for i, (x_ref, land_ref) in enumerate(zip(x_refs, land_refs)):
        w = i if which is None else which[i]
        R2 = x_ref.shape[0] // 2
        mine = x_ref.at[pl.ds(mc * R2, R2)]
        for j, (px, py) in enumerate(chips):
            dst = land_ref.at[2 * px + py, mc] if inbound else land_ref.at[me, mc]
            cps.append(_rcopy(mine, dst, ssem.at[3 * w + j], rsem.at[3 * w + j], (px, py, mc)))
    return cps


def _ag_start(xs, token, *, name):
    n = len(xs)
    lands = [pltpu.with_memory_space_constraint(lax.empty((N_CHIPS, 2, x.shape[0] // 2, x.shape[1]), x.dtype), pltpu.HBM) for x in xs]
    xs = [pltpu.with_memory_space_constraint(x, pltpu.HBM) for x in xs]

    def body(*refs):
        x_refs, land_refs, tok_ref = refs[:n], refs[n:2 * n], refs[2 * n]
        ssem, rsem, tok_out = refs[2 * n + 1], refs[2 * n + 2], refs[-1]
        for cp in _ag_copies(x_refs, land_refs, ssem, rsem, False):
            cp.start()
        tok_out[...] = tok_ref[...]

    outs = pl.pallas_call(
        body, name=name,
        out_shape=(pltpu.SemaphoreType.DMA((3 * n,)), pltpu.SemaphoreType.DMA((3 * n,)),
                   *[pltpu.HBM(x.shape, x.dtype) for x in xs], *[pltpu.HBM(t.shape, t.dtype) for t in lands],
                   jax.ShapeDtypeStruct(token.shape, token.dtype)),
        in_specs=[_HBM] * (2 * n) + [_VM], out_specs=(_SEMS, _SEMS, *[_HBM] * (2 * n), _VM),
        input_output_aliases={i: 2 + i for i in range(2 * n)},
        compiler_params=pltpu.CompilerParams(has_side_effects=_DATAFLOW),
    )(*xs, *lands, token)
    return outs[0], outs[1], list(outs[2:2 + n]), list(outs[2 + n:2 + 2 * n]), outs[-1]


def _ag_wait(ssem, rsem, xs, lands, after, which, *, name):
    n = len(xs)

    def body(*refs):
        x_refs, land_refs, ssem_ref, rsem_ref = refs[:n], refs[n:2 * n], refs[2 * n], refs[2 * n + 1]
        for cp in _ag_copies(x_refs, land_refs, ssem_ref, rsem_ref, True, which):
            cp.wait_send()
            cp.wait_recv()

    after = list(after) if isinstance(after, (list, tuple)) else [after]
    outs = pl.pallas_call(
        body, name=name,
        out_shape=(*[pltpu.HBM(x.shape, x.dtype) for x in xs], *[pltpu.HBM(t.shape, t.dtype) for t in lands]),
        in_specs=[_HBM] * (2 * n) + [_SEMS, _SEMS] + [_ANY] * len(after), out_specs=tuple([_HBM] * (2 * n)),
        input_output_aliases={i: i for i in range(2 * n)},
        compiler_params=pltpu.CompilerParams(has_side_effects=_DATAFLOW),
    )(*xs, *lands, ssem, rsem, *after)
    return list(outs[:n]), list(outs[n:])


def _ag_finish(lands, xs, *, name):
    n = len(xs)

    def body(*refs):
        x_refs, o_refs, (ssem, rsem) = refs[n:2 * n], refs[2 * n:3 * n], refs[3 * n:]
        mx, my, mc, chips = _place()
        me = 2 * mx + my
        sib = (mx, my, 1 - mc)
        passed = []
        for w in range(n):
            for j, (px, py) in enumerate(chips):
                got = o_refs[w].at[2 * px + py, mc]
                passed.append(_rcopy(got, got, ssem.at[3 * w + j], rsem.at[3 * w + j], sib))
        for cp in passed:
            cp.start()
        for w in range(n):
            R, C = x_refs[w].shape
            R2 = R // 2
            ch = _rows_tile(R2, max(8, STAGE_BYTES // (C * x_refs[w].dtype.itemsize)))
            pairs = [(x_refs[w].at[pl.ds(h * R2 + r, ch)], o_refs[w].at[me, h, pl.ds(r, ch)]) for h in range(2) for r in range(0, R2, ch)]
            pl.run_scoped(functools.partial(_staged_copy, pairs), pltpu.VMEM((2, ch, C), x_refs[w].dtype),
                          pltpu.SemaphoreType.DMA((2,)), pltpu.SemaphoreType.DMA((2,)))
        for w in range(n):
            for j, (px, py) in enumerate(chips):
                theirs = o_refs[w].at[2 * px + py, 1 - mc]
                _rcopy(theirs, theirs, ssem.at[3 * w + j], rsem.at[3 * w + j], sib).wait_recv()
        for cp in passed:
            cp.wait_send()

    return pl.pallas_call(
        body, in_specs=[_ANY] * (2 * n), out_specs=[_ANY] * n,
        out_shape=[jax.ShapeDtypeStruct(t.shape, t.dtype) for t in lands],
        input_output_aliases={i: i for i in range(n)},
        scratch_shapes=[pltpu.SemaphoreType.DMA((3 * n,)), pltpu.SemaphoreType.DMA((3 * n,))],
        name=name)(*lands, *xs)


def _ag4(x, *, name):
    def body(x_ref, o_ref, ssem, rsem, lsem):
        mx, my, mc, chips = _place()
        me = 2 * mx + my
        loc = pltpu.make_async_copy(x_ref, o_ref.at[me], lsem)
        loc.start()
        sends = [_rcopy(x_ref, o_ref.at[me], ssem.at[j], rsem.at[j], (px, py, mc)) for j, (px, py) in enumerate(chips)]
        for cp in sends:
            cp.start()
        for j, (px, py) in enumerate(chips):
            _rcopy(x_ref, o_ref.at[2 * px + py], ssem.at[j], rsem.at[j], (px, py, mc)).wait_recv()
        for cp in sends:
            cp.wait_send()
        loc.wait()

    return pl.pallas_call(
        body, in_specs=[_ANY], out_specs=_ANY, out_shape=jax.ShapeDtypeStruct((N_CHIPS,) + x.shape, x.dtype),
        scratch_shapes=[pltpu.SemaphoreType.DMA((3,)), pltpu.SemaphoreType.DMA((3,)), pltpu.SemaphoreType.DMA(())],
        name=name)(x)


def _rs_sib(gs, after=(), *, name):
    n, na = len(gs), len(after)

    def body(*refs):
        g_refs, t_refs, (ssem, rsem) = refs[:n], refs[n + na:2 * n + na], refs[2 * n + na:]
        mx, my, mc, _ = _place()
        sib = (mx, my, 1 - mc)
        sends = []
        for w in range(n):
            R2 = g_refs[w].shape[1] // 2
            for k in range(N_CHIPS):
                sends.append(_rcopy(g_refs[w].at[k, pl.ds((1 - mc) * R2, R2)], t_refs[w].at[k], ssem.at[N_CHIPS * w + k],
                                    rsem.at[N_CHIPS * w + k], sib))
        for cp in sends:
            cp.start()
        for w in range(n):
            for k in range(N_CHIPS):
                t = t_refs[w].at[k]
                _rcopy(t, t, ssem.at[N_CHIPS * w + k], rsem.at[N_CHIPS * w + k], sib).wait_recv()
        for cp in sends:
            cp.wait_send()

    return pl.pallas_call(
        body, in_specs=[_ANY] * (n + na), out_specs=[_ANY] * n,
        out_shape=[jax.ShapeDtypeStruct((N_CHIPS, g.shape[1] // 2, g.shape[2]), g.dtype) for g in gs],
        scratch_shapes=[pltpu.SemaphoreType.DMA((N_CHIPS * n,)), pltpu.SemaphoreType.DMA((N_CHIPS * n,))],
        name=name)(*gs, *after)


def _a2a_copies(h_refs, got_refs, ssem, rsem, inbound):
    mx, my, mc, chips = _place()
    me = 2 * mx + my
    cps = []
    for w, (h_ref, got_ref) in enumerate(zip(h_refs, got_refs)):
        for j, (px, py) in enumerate(chips):
            dst = got_ref.at[2 * px + py] if inbound else got_ref.at[me]
            cps.append(_rcopy(h_ref.at[2 * px + py], dst, ssem.at[3 * w + j], rsem.at[3 * w + j], (px, py, mc)))
    return cps


def _a2a_start(hs, *, name):
    n = len(hs)
    gots = [pltpu.with_memory_space_constraint(lax.empty(h.shape, h.dtype), pltpu.HBM) for h in hs]
    hs = [pltpu.with_memory_space_constraint(h, pltpu.HBM) for h in hs]

    def body(*refs):
        h_refs, got_refs = refs[:n], refs[n:2 * n]
        ssem, rsem, tok_out = refs[2 * n], refs[2 * n + 1], refs[-1]
        for cp in _a2a_copies(h_refs, got_refs, ssem, rsem, False):
            cp.start()
        tok_out[...] = jnp.zeros_like(tok_out)

    outs = pl.pallas_call(
        body, name=name,
        out_shape=(pltpu.SemaphoreType.DMA((3 * n,)), pltpu.SemaphoreType.DMA((3 * n,)),
                   *[pltpu.HBM(h.shape, h.dtype) for h in hs], *[pltpu.HBM(h.shape, h.dtype) for h in hs],
                   jax.ShapeDtypeStruct((8, 128), F32)),
        in_specs=[_HBM] * (2 * n), out_specs=(_SEMS, _SEMS, *[_HBM] * (2 * n), _VM),
        input_output_aliases={i: 2 + i for i in range(2 * n)},
        compiler_params=pltpu.CompilerParams(has_side_effects=_DATAFLOW),
    )(*hs, *gots)
    return outs[0], outs[1], list(outs[2:2 + n]), list(outs[2 + n:2 + 2 * n]), outs[-1]


def _a2a_wait(ssem, rsem, hs, gots, after, *, name):
    n = len(hs)

    def body(*refs):
        h_refs, got_refs, ssem_ref, rsem_ref = refs[:n], refs[n:2 * n], refs[2 * n], refs[2 * n + 1]
        for cp in _a2a_copies(h_refs, got_refs, ssem_ref, rsem_ref, True):
            cp.wait_send()
            cp.wait_recv()

    after = list(after) if isinstance(after, (list, tuple)) else [after]
    outs = pl.pallas_call(
        body, name=name,
        out_shape=tuple(pltpu.HBM(h.shape, h.dtype) for h in hs + gots),
        in_specs=[_HBM] * (2 * n) + [_SEMS, _SEMS] + [_ANY] * len(after), out_specs=tuple([_HBM] * (2 * n)),
        input_output_aliases={i: i for i in range(2 * n)},
        compiler_params=pltpu.CompilerParams(has_side_effects=_DATAFLOW),
    )(*hs, *gots, ssem, rsem, *after)
    return list(outs[:n]), list(outs[n:])


def _sib_fill(bufs, *, name):
    n = len(bufs)

    def body(*refs):
        o_refs, (ssem, rsem) = refs[n:2 * n], refs[2 * n:]
        mx, my, mc, _ = _place()
        sib = (mx, my, 1 - mc)
        sends = [_rcopy(o_refs[w].at[mc], o_refs[w].at[mc], ssem.at[w], rsem.at[w], sib) for w in range(n)]
        for cp in sends:
            cp.start()
        for w in range(n):
            t = o_refs[w].at[1 - mc]
            _rcopy(t, t, ssem.at[w], rsem.at[w], sib).wait_recv()
        for cp in sends:
            cp.wait_send()

    return pl.pallas_call(
        body, in_specs=[_ANY] * n, out_specs=[_ANY] * n, out_shape=[jax.ShapeDtypeStruct(b.shape, b.dtype) for b in bufs],
        input_output_aliases={i: i for i in range(n)},
        scratch_shapes=[pltpu.SemaphoreType.DMA((n,)), pltpu.SemaphoreType.DMA((n,))], name=name)(*bufs)


def _rows_tile(n, pref=512):
    t = min(n, pref)
    while n % t or (t % 8 and t != n):
        t -= 1
    return t


def _rs_add(g, theirs, c_arr, payload, *, name):
    _, R, C = g.shape
    R2 = R // 2
    tr = _rows_tile(R2, 256)
    nb = R2 // tr

    def body(c_ref, g_ref, t_ref, o_ref):
        o_ref[...] = (g_ref[...].astype(F32) + t_ref[...].astype(F32)).astype(o_ref.dtype)

    blk = pl.BlockSpec((None, tr, C), lambda k, i, c: (k, i, 0))
    return pl.pallas_call(
        body,
        grid_spec=pltpu.PrefetchScalarGridSpec(
            num_scalar_prefetch=1, grid=(N_CHIPS, nb),
            in_specs=[pl.BlockSpec((None, tr, C), lambda k, i, c: (k, c[0] * nb + i, 0)), blk], out_specs=blk),
        out_shape=jax.ShapeDtypeStruct((N_CHIPS, R2, C), payload),
        compiler_params=_cparams(("arbitrary", "arbitrary")), name=name)(c_arr, g, theirs)


def _rs_sum(chip_sum, got, mc_arr, *, name):
    _, R2, C = chip_sum.shape
    tr = _rows_tile(R2, 256)

    def body(s_ref, own_ref, g0, g1, g2, g3, o_ref):
        me = s_ref[0]
        acc = jnp.zeros((tr, C), F32)
        for k, g_ref in enumerate((g0, g1, g2, g3)):
            acc = acc + jnp.where(me == k, own_ref[...], g_ref[...]).astype(F32)
        o_ref[...] = acc

    def got_spec(k):
        return pl.BlockSpec((None, tr, C), lambda i, s: (jnp.where(s[0] == k, (k + 1) % N_CHIPS, k), i, 0))

    return pl.pallas_call(
        body,
        grid_spec=pltpu.PrefetchScalarGridSpec(
            num_scalar_prefetch=1, grid=(R2 // tr,),
            in_specs=[pl.BlockSpec((None, tr, C), lambda i, s: (s[0], i, 0))] + [got_spec(k) for k in range(N_CHIPS)],
            out_specs=pl.BlockSpec((None, tr, C), lambda i, s: (s[1], i, 0))),
        out_shape=jax.ShapeDtypeStruct((2, R2, C), F32),
        compiler_params=_cparams(("arbitrary",)), name=name)(mc_arr, chip_sum, got, got, got, got)


def _adamw(w, m, v, gs, *, name, first=0, prev=None):
    _, _, R2, C = w.shape
    nl = len(gs)
    tr = _rows_tile(R2, 256)
    c1 = 1.0 / (1.0 - ADAM_B1 ** ADAM_STEP)
    c2 = 1.0 / (1.0 - ADAM_B2 ** ADAM_STEP)

    def body(w_ref, m_ref, v_ref, *refs):
        g_refs, (go_ref, d_ref, mo_ref, vo_ref) = refs[:nl], refs[-4:]
        l = pl.program_id(0)
        for ll in range(nl):
            @pl.when(l == ll)
            def _(ll=ll):
                gv = g_refs[ll][...]
                mn = ADAM_B1 * m_ref[...] + (1.0 - ADAM_B1) * gv
                vn = ADAM_B2 * v_ref[...] + (1.0 - ADAM_B2) * (gv * gv)
                go_ref[...] = gv
                mo_ref[...] = mn
                vo_ref[...] = vn
                d_ref[...] = -ADAM_LR * ((mn * c1) / (jnp.sqrt(vn * c2) + ADAM_EPS) + ADAM_WD * w_ref[...])

    def g_spec(ll):
        return pl.BlockSpec((None, tr, C), lambda l, h, i: (jnp.where(l == ll, h, 0), jnp.where(l == ll, i, 0), 0))

    ws = pl.BlockSpec((None, None, tr, C), lambda l, h, i: (l + first, h, i, 0))
    shp = jax.ShapeDtypeStruct(w.shape, F32)
    prev = list(prev) if prev is not None else []
    return pl.pallas_call(
        body, grid=(nl, 2, R2 // tr), in_specs=[ws, ws, ws] + [g_spec(ll) for ll in range(nl)] + [_ANY] * len(prev),
        out_specs=[ws] * 4, out_shape=[shp] * 4,
        input_output_aliases={3 + nl + k: k for k in range(len(prev))},
        compiler_params=_cparams(("arbitrary", "arbitrary", "arbitrary")), name=name)(w, m, v, *gs, *prev)


class _GradReducer:
    def __init__(self, tag, payload):
        self.tag, self.payload = tag, payload
        self.me = (2 * lax.axis_index("x") + lax.axis_index("y")).astype(jnp.int32)
        self.c = lax.axis_index("c").astype(jnp.int32)

    def start(self, names, gs, after=()):
        theirs = _rs_sib(gs, after, name=f"rs_sib_{self.tag}")
        hs = [_rs_add(g, t, self.c.reshape(1), self.payload, name=f"rs_add_{n}") for n, g, t in zip(names, gs, theirs)]
        self.names = names
        self.ssem, self.rsem, self.hs, self.gots, token = _a2a_start(hs, name=f"rs_a2a_start_{self.tag}")
        return token

    def finish(self, after):
        hs, gots = _a2a_wait(self.ssem, self.rsem, self.hs, self.gots, after, name=f"rs_a2a_wait_{self.tag}")
        mc = jnp.stack([self.me, self.c])
        return {n: _rs_sum(h, g, mc, name=f"rs_sum_{n}") for n, h, g in zip(self.names, hs, gots)}


WEIGHTS = ["rel_bias", "mem_norm_g", "norm_mix_g", "w_in", "s5_lam_re", "s5_lam_im", "s5_log_dt", "s5_b_re", "s5_b_im",
           "s5_c_re", "s5_c_im", "s5_d", "s5_w_glu", "pool_w", "pool_scale", "conv_w_dw", "conv_b_dw", "conv_ln_g",
           "conv_ln_b", "conv_w_pw", "grp_norm_g", "w_out", "norm_x_g", "w_xq", "w_xk", "w_xv", "w_xo", "norm_mlp_g",
           "w_up", "w_down", "norm_final_g"]
SHARDED = {"w_in": "col", "s5_w_glu": "row", "conv_w_dw": "col", "conv_w_pw": "row", "w_out": "row", "w_xq": "row",
           "w_xk": "row", "w_xv": "row", "w_xo": "col", "w_up": "col", "w_down": "row"}
AG_FIRST = ("w_in", "s5_w_glu", "conv_w_dw", "conv_w_pw")
SMALL = [n for n in WEIGHTS if n not in SHARDED]
SMALL_ALIGN = N_CHIPS * 2 * 256 * 128


def _mat(w, kind):
    return w.reshape(w.shape[0] * w.shape[1], w.shape[2]) if kind == "row" else w


def _att_bias(rel_bias):
    bucket, valid = _att_tables()
    onehot = (jnp.asarray(bucket.reshape(-1))[:, None] == jnp.arange(REL_BUCKETS)[None, :]).astype(F32)
    b = jnp.dot(onehot, rel_bias, precision=lax.Precision.HIGHEST).reshape(bucket.shape + (rel_bias.shape[1],))
    return jnp.where(jnp.asarray(valid)[:, None], jnp.transpose(b, (0, 3, 1, 2)), NEG_INF)


def _rel_bias_grad(dbias):
    bucket, _ = _att_tables()
    nb, H = dbias.shape[0], dbias.shape[1]
    onehot = (jnp.asarray(bucket.reshape(-1))[:, None] == jnp.arange(REL_BUCKETS)[None, :]).astype(BF16)
    flat = jnp.transpose(dbias.reshape(nb, H, -1), (1, 0, 2)).reshape(H, -1)
    return _mm(flat, onehot, "nn", name="rel_bias_grad").T


def _layer_fwd(h, mem_n, bias, sp, bw, l):
    L, D = h.shape
    GW = D // N_MIXERS
    G = GW // S5_CH
    E = GW // ATT_HEADS
    sv = {"h": h}
    xn = _rms_fwd(h, sp["norm_mix_g"][l][None], name="norm_mix")
    proj = _mm(xn, bw["w_in"], "nn", name="proj_in")
    sv.update(xn=xn, proj=proj)
    disc, disc_vjp = jax.vjp(_s5_disc, sp["s5_lam_re"][l], sp["s5_lam_im"][l], sp["s5_log_dt"][l], sp["s5_b_re"][l], sp["s5_b_im"][l])
    ab_r, ab_i, bb_r, bb_i = disc
    s5 = dict(
        bdr=_bd(jnp.transpose(bb_r, (0, 2, 1))).astype(BF16), bdi=_bd(jnp.transpose(bb_i, (0, 2, 1))).astype(BF16),
        cdr=_bd(jnp.transpose(sp["s5_c_re"][l], (0, 2, 1))).astype(BF16), cdi=_bd(jnp.transpose(sp["s5_c_im"][l], (0, 2, 1))).astype(BF16),
        d=sp["s5_d"][l][None], wglu=bw["s5_w_glu"], ab=(ab_r.reshape(-1), ab_i.reshape(-1)), vjp=disc_vjp)
    pw, tr, ti = _cpow_tables(*s5["ab"])
    y_a, xr, xi = _s5_fwd(proj, s5["bdr"], s5["bdi"], pw, tr, ti, s5["cdr"], s5["cdi"], s5["d"], s5["wglu"], name="s5_fwd")
    sv.update(s5=s5, xr=xr, xi=xi, y_a=y_a)
    pool_s = sp["pool_scale"][l].reshape(len(POOL_WINDOWS), 1, -1)
    y_b = _pool_fwd(proj, sp["pool_w"][l], pool_s, name="pool_fwd")
    sv.update(y_b=y_b, pool_s=pool_s)
    hc = _conv1_fwd(proj, bw["conv_w_dw"], sp["conv_b_dw"][l][None], name="conv_dw_fwd")
    y_c = _conv2_fwd(hc, sp["conv_ln_g"][l][None], sp["conv_ln_b"][l][None], bw["conv_w_pw"], name="conv_pw_fwd")
    sv.update(hc=hc, y_c=y_c)
    y_d, lse_d = _datt_fwd(proj, bias, name="att_fwd")
    sv.update(y_d=y_d, lse_d=lse_d)
    yn = _grp_fwd([y_a, y_b, y_c, y_d], sp["grp_norm_g"][l][None], name="grp_fwd")
    bw.rest(yn)
    h1 = _mm(yn, bw["w_out"], "nn", res=h, name="proj_out")
    sv.update(yn=yn, h1=h1)
    hx = _rms_fwd(h1, sp["norm_x_g"][l][None], name="norm_x")
    q_x = _mm(hx, bw["w_xq"], "nn", out_dtype=BF16, name="xq")
    k_x = _mm(mem_n, bw["w_xk"], "nn", out_dtype=BF16, name="xk")
    v_x = _mm(mem_n, bw["w_xv"], "nn", out_dtype=BF16, name="xv")
    o_x = _xatt_fwd(q_x, k_x, v_x, name="xatt_fwd")
    h2 = _mm(o_x, bw["w_xo"], "nn", res=h1, name="xo")
    sv.update(hx=hx, q_x=q_x, k_x=k_x, v_x=v_x, o_x=o_x, h2=h2)
    hm = _rms_fwd(h2, sp["norm_mlp_g"][l][None], name="norm_mlp")
    up, act = _mm(hm, bw["w_up"], "nn", epi="relu2", out_dtype=BF16, name="mlp_up")
    h3 = _mm(act, bw["w_down"], "nn", res=h2, name="mlp_down")
    sv.update(hm=hm, up=up, act=act)
    return h3, sv


def _stack_rows(g):
    return g.reshape(N_CHIPS, g.shape[0] // N_CHIPS, g.shape[1])


def _layer_bwd(dh3, d_memn, mem_n, bias, sp, bw, sv, l):
    L, D = dh3.shape
    GW = D // N_MIXERS
    G = GW // S5_CH
    E = GW // ATT_HEADS
    gs, gb = {}, {}
    d_up = _mm(dh3, bw["w_down"], "nt", epi="relu2_bwd", aux=sv["up"], out_dtype=BF16, name="mlp_down_dx")
    gb["w_down"] = _stack_rows(_mm(sv["act"], dh3, "tn", out_dtype=BF16, name="mlp_down_dw"))
    gb["w_up"] = _mm(sv["hm"], d_up, "tn", out_dtype=BF16, out_stack=N_CHIPS, name="mlp_up_dw")
    d_hm = _mm(d_up, bw["w_up"], "nt", name="mlp_up_dx")
    dh2, gs["norm_mlp_g"] = _rms_bwd(sv["h2"], sp["norm_mlp_g"][l][None], d_hm, dh3, name="norm_mlp_bwd")
    d_ox = _mm(dh2, bw["w_xo"], "nt", name="xo_dx")
    gb["w_xo"] = _mm(sv["o_x"], dh2, "tn", out_dtype=BF16, out_stack=N_CHIPS, name="xo_dw")
    dq_x, dk_x, dv_x = _xatt_bwd(sv["q_x"], sv["k_x"], sv["v_x"], d_ox, name="xatt_bwd")
    gb["w_xq"] = _stack_rows(_mm(sv["hx"], dq_x, "tn", out_dtype=BF16, name="xq_dw"))
    gb["w_xk"] = _stack_rows(_mm(mem_n, dk_x, "tn", out_dtype=BF16, name="xk_dw"))
    gb["w_xv"] = _stack_rows(_mm(mem_n, dv_x, "tn", out_dtype=BF16, name="xv_dw"))
    d_memn = _mm(dk_x, bw["w_xk"], "nt", res=d_memn, name="xk_dx")
    d_memn = _mm(dv_x, bw["w_xv"], "nt", res=d_memn, name="xv_dx")
    d_hx = _mm(dq_x, bw["w_xq"], "nt", name="xq_dx")
    dh1, gs["norm_x_g"] = _rms_bwd(sv["h1"], sp["norm_x_g"][l][None], d_hx, dh2, name="norm_x_bwd")
    d_yn = _mm(dh1, bw["w_out"], "nt", name="proj_out_dx")
    gb["w_out"] = _stack_rows(_mm(sv["yn"], dh1, "tn", out_dtype=BF16, name="proj_out_dw"))
    ys = [sv["y_a"], sv["y_b"], sv["y_c"], sv["y_d"]]
    dya, dyb, dyc, dyd, gs["grp_norm_g"] = _grp_bwd(ys, sp["grp_norm_g"][l][None], d_yn, name="grp_bwd")
    dq, dk, dv, dbias = _datt_bwd(sv["proj"], bias, sv["y_d"], sv["lse_d"], dyd, name="att_bwd")
    dhc, gs["conv_ln_g"], gs["conv_ln_b"], g_pw = _conv2_bwd(sv["hc"], sp["conv_ln_g"][l][None], sp["conv_ln_b"][l][None],
                                                               bw["conv_w_pw"], dyc, name="conv_pw_bwd")
    gb["conv_w_pw"] = _stack_rows(g_pw)
    dval, dgate, g_dw, gs["conv_b_dw"] = _conv1_bwd(sv["proj"], dhc, bw["conv_w_dw"], name="conv_dw_bwd")
    gb["conv_w_dw"] = jnp.transpose(g_dw.reshape(CONV_PAD, N_CHIPS, GW // N_CHIPS), (1, 0, 2))
    du_b, gs["pool_w"], g_ps = _pool_bwd(sv["proj"], dyb, sp["pool_w"][l], sv["pool_s"], name="pool_bwd")
    gs["pool_scale"] = g_ps.reshape(-1)
    s5 = sv["s5"]
    conj = (s5["ab"][0], -s5["ab"][1])
    pw, tr, ti = _cpow_tables(*conj)
    du_a, g_glu, g_d, dcdr, dcdi, dbdr, dbdi, dar, dai = _s5_bwd(
        sv["proj"], sv["xr"], sv["xi"], dya, s5["bdr"], s5["bdi"], pw, tr[::-1], ti[::-1], s5["cdr"], s5["cdi"], s5["d"], s5["wglu"],
        name="s5_bwd")
    gb["s5_w_glu"] = _stack_rows(g_glu)
    gs["s5_d"] = g_d.reshape(-1)
    gs["s5_c_re"] = jnp.transpose(_bd_extract(dcdr, G), (0, 2, 1))
    gs["s5_c_im"] = jnp.transpose(_bd_extract(dcdi, G), (0, 2, 1))
    d_bb_r = jnp.transpose(_bd_extract(dbdr, G), (0, 2, 1))
    d_bb_i = jnp.transpose(_bd_extract(dbdi, G), (0, 2, 1))
    d_ab_r = jnp.sum(dar, axis=0).reshape(G, S5_STATE)
    d_ab_i = jnp.sum(dai, axis=0).reshape(G, S5_STATE)
    gs["s5_lam_re"], gs["s5_lam_im"], gs["s5_log_dt"], gs["s5_b_re"], gs["s5_b_im"] = s5["vjp"]((d_ab_r, d_ab_i, d_bb_r, d_bb_i))
    d_proj = jnp.concatenate([du_a, du_b, dval, dgate, dq, dk, dv], axis=1)
    gb["w_in"] = _mm(sv["xn"], d_proj, "tn", out_dtype=BF16, out_stack=N_CHIPS, name="proj_in_dw")
    d_xn = _mm(d_proj, bw["w_in"], "nt", name="proj_in_dx")
    dh0, gs["norm_mix_g"] = _rms_bwd(sv["h"], sp["norm_mix_g"][l][None], d_xn, dh1, name="norm_mix_bwd")
    gs = {n: g.reshape(sp[n].shape[1:]) for n, g in gs.items()}
    return dh0, d_memn, dbias, gs, gb


def _device_step(x, mem, target, sp, get_big, on_grads):
    nl = sp["norm_mix_g"].shape[0]
    mem_n = _rms_fwd(mem, sp["mem_norm_g"][None], name="norm_mem")
    bias = _att_bias(sp["rel_bias"])
    h, saved, big = x, [], {n: [] for n in SHARDED}
    for l in range(nl):
        bw = get_big(l, h)
        h, sv = _layer_fwd(h, mem_n, bias, sp, bw, l)
        saved.append(sv)
        for n in SHARDED:
            big[n].append(bw[n])
    loss, dh, g_final = _loss_head(h, sp["norm_final_g"][None], target, name="loss_head")
    d_memn = jnp.zeros(mem.shape, F32)
    dbias = jnp.zeros(bias.shape, F32)
    sg = {n: [None] * nl for n in SMALL}
    for l in reversed(range(nl)):
        dh, d_memn, dbias_l, gs, gb = _layer_bwd(dh, d_memn, mem_n, bias, sp, {n: big[n][l] for n in SHARDED}, saved[l], l)
        dbias = dbias + dbias_l
        for n, g in gs.items():
            sg[n][l] = g
        dh = on_grads(l, gb, dh)
    small = {n: jnp.stack(g) for n, g in sg.items() if g[0] is not None}
    small["norm_final_g"] = g_final.reshape(-1)
    _, g_mem = _rms_bwd(mem, sp["mem_norm_g"][None], d_memn, None, name="norm_mem_bwd")
    small["mem_norm_g"] = g_mem.reshape(-1)
    small["rel_bias"] = _rel_bias_grad(dbias)
    return loss, dh, small


def _gather_big(shards, prep):
    nl = shards["w_in"].shape[0]
    token = jnp.zeros((8, 128), F32)
    pending = []
    for l in range(nl):
        xs = [jnp.pad(shards[n][l], ((0, CONV_PAD - CONV_WIDTH), (0, 0))) if n == "conv_w_dw" else shards[n][l].astype(BF16)
              for n in SHARDED]
        ssem, rsem, xs, lands, token = _ag_start(xs, token, name=f"ag_start_l{l}")
        pending.append((ssem, rsem, xs, lands))

    names = list(SHARDED)

    class LayerWeights(dict):
        def __init__(self, l, after):
            super().__init__()
            self.l, self.state = l, pending[l]
            first = [i for i, n in enumerate(names) if n in AG_FIRST] if l == 0 else list(range(len(names)))
            self.left = [i for i in range(len(names)) if i not in first]
            self._complete(first, [token] + list(prep) if l == 0 else [after], "a")

        def _complete(self, which, after, tag):
            ssem, rsem, xs, lands = self.state
            xs, lands = _ag_wait(ssem, rsem, [xs[i] for i in which], [lands[i] for i in which], after, which,
                                 name=f"ag_wait_l{self.l}{tag}")
            fulls = _ag_finish(lands, xs, name=f"ag_finish_{tag if self.l == 0 else 'all'}")
            for i, x, full in zip(which, xs, fulls):
                n = names[i]
                full = full.reshape(N_CHIPS, x.shape[0], x.shape[1])
                self[n] = jnp.transpose(full, (1, 0, 2)).reshape(CONV_PAD, -1) if n == "conv_w_dw" else _mat(full, SHARDED[n])

        def rest(self, after):
            if self.left:
                self._complete(self.left, [after], "b")
                self.left = []

    return LayerWeights


def _pack_small(vals, extra=None):
    flat = [vals[n].reshape(-1).astype(F32) for n in SMALL]
    flat.append(jnp.zeros((1,), F32) if extra is None else extra.reshape(1))
    v = jnp.concatenate(flat)
    n_pad = -(-v.shape[0] // SMALL_ALIGN) * SMALL_ALIGN
    return jnp.pad(v, (0, n_pad - v.shape[0]))


def _unpack_small(flat, like):
    out, pos = {}, 0
    for n in SMALL:
        size = math.prod(like[n].shape)
        out[n] = flat[pos:pos + size].reshape(like[n].shape)
        pos += size
    return out, flat[pos]


def kernel(x, mem, rel_bias, mem_norm_g, norm_mix_g, w_in, s5_lam_re, s5_lam_im, s5_log_dt, s5_b_re, s5_b_im, s5_c_re, s5_c_im, s5_d, s5_w_glu, pool_w, pool_scale, conv_w_dw, conv_b_dw, conv_ln_g, conv_ln_b, conv_w_pw, grp_norm_g, w_out, norm_x_g, w_xq, w_xk, w_xv, w_xo, norm_mlp_g, w_up, w_down, norm_final_g, loss_target, m_rel_bias, m_mem_norm_g, m_norm_mix_g, m_w_in, m_s5_lam_re, m_s5_lam_im, m_s5_log_dt, m_s5_b_re, m_s5_b_im, m_s5_c_re, m_s5_c_im, m_s5_d, m_s5_w_glu, m_pool_w, m_pool_scale, m_conv_w_dw, m_conv_b_dw, m_conv_ln_g, m_conv_ln_b, m_conv_w_pw, m_grp_norm_g, m_w_out, m_norm_x_g, m_w_xq, m_w_xk, m_w_xv, m_w_xo, m_norm_mlp_g, m_w_up, m_w_down, m_norm_final_g, v_rel_bias, v_mem_norm_g, v_norm_mix_g, v_w_in, v_s5_lam_re, v_s5_lam_im, v_s5_log_dt, v_s5_b_re, v_s5_b_im, v_s5_c_re, v_s5_c_im, v_s5_d, v_s5_w_glu, v_pool_w, v_pool_scale, v_conv_w_dw, v_conv_b_dw, v_conv_ln_g, v_conv_ln_b, v_conv_w_pw, v_grp_norm_g, v_w_out, v_norm_x_g, v_w_xq, v_w_xk, v_w_xv, v_w_xo, v_norm_mlp_g, v_w_up, v_w_down, v_norm_final_g):
    env = dict(locals())
    w = {n: env[n] for n in WEIGHTS}
    m = {n: env["m_" + n] for n in WEIGHTS}
    v = {n: env["v_" + n] for n in WEIGHTS}
    sp = {n: w[n] for n in SMALL}
    small_wmv = [_pack_small(t).reshape(1, 2, -1, 128) for t in (w, m, v)]
    get_big = _gather_big({n: w[n] for n in SHARDED}, small_wmv)
    nl = w["w_in"].shape[0]
    names = list(SHARDED)
    reducers, reduced = {}, {}

    def on_grads(l, gb, dh):
        reducers[l] = _GradReducer(f"l{l}", BF16)
        token = reducers[l].start(names, [gb[n] for n in names])
        if l + 1 in reducers:
            reduced[l + 1] = reducers[l + 1].finish(dh)
        return dh + token[0, 0]

    loss, grad_x, g_small = _device_step(x[0], mem[0], loss_target[0], sp, get_big, on_grads)

    def shard_views(n):
        pad = ((0, 0), (0, CONV_PAD - CONV_WIDTH), (0, 0)) if n == "conv_w_dw" else None
        wmv = [jnp.pad(t[n], pad) if pad else t[n] for t in (w, m, v)]
        _, R, C = wmv[0].shape
        return [t.reshape(nl, 2, R // 2, C) for t in wmv]

    grad, delta, new_m, new_v = {}, {}, {}, {}
    busy, upper = [grad_x], {}
    if nl > 1:
        filled = _sib_fill([reduced[l][n] for n in names for l in range(1, nl)], name="rs_fill_upper")
        for i, n in enumerate(names):
            upper[n] = _adamw(*shard_views(n), filled[i * (nl - 1):(i + 1) * (nl - 1)], first=1, name=f"adamw_upper_{n}")
            busy.append(upper[n][0])
    reduced[0] = reducers[0].finish(busy)
    packed = _pack_small(g_small, loss).reshape(N_CHIPS, -1, 128)
    small = _GradReducer("small", F32)
    token = small.start(["small"], [packed], after=[reduced[0][names[0]]])
    quarter = _sib_fill([small.finish(token)["small"]], name="rs_fill_small")[0]
    total = _ag4(quarter.reshape(-1, 128), name="ag_small").reshape(2, -1, 128)
    outs = _adamw(*small_wmv, [total], name="adamw_small")
    filled = _sib_fill([reduced[0][n] for n in names], name="rs_fill_first")
    for i, n in enumerate(names):
        res = _adamw(*shard_views(n), [filled[i]], prev=upper.get(n), name=f"adamw_first_{n}")
        R = w[n].shape[1]
        grad[n], delta[n], new_m[n], new_v[n] = [o.reshape(nl, -1, o.shape[-1])[:, :R] for o in res]
    loss_sum = None
    for o, dst in zip(outs, (grad, delta, new_m, new_v)):
        vals, last = _unpack_small(o.reshape(-1), sp)
        dst.update(vals)
        loss_sum = last if loss_sum is None else loss_sum
    return (loss_sum, grad_x[None], *[grad[n] for n in WEIGHTS], *[delta[n] for n in WEIGHTS],
            *[new_m[n] for n in WEIGHTS], *[new_v[n] for n in WEIGHTS])
```

```python
import functools
import math

import jax
import jax.numpy as jnp
import numpy as np
from jax import lax
from jax.experimental import pallas as pl
from jax.experimental.pallas import tpu as pltpu

F32 = jnp.float32
BF16 = jnp.bfloat16
MESH = pl.DeviceIdType.MESH

N_MIXERS = 4
S5_CH = 16
S5_STATE = 64
POOL_WINDOWS = (2, 4, 8, 16)
CONV_WIDTH = 31
CONV_PAD = 32
ATT_HEADS = 8
ATT_BLOCK = 128
DILATED_PATTERNS = ((128, 1), (512, 4), (2048, 16))
REL_BUCKETS = 32
REL_MAX_DIST = 2048
X_HEADS = 4
X_HEAD_DIM = 128
NORM_EPS = 1e-6
NEG_INF = -1e30
ADAM_LR, ADAM_B1, ADAM_B2, ADAM_EPS, ADAM_WD, ADAM_STEP = 0.001, 0.9, 0.999, 1e-08, 0.01, 10
N_CHIPS = 4
VMEM_LIMIT = 56 * 1024 * 1024


def _cparams(sem=None, vmem=None):
    return pltpu.CompilerParams(dimension_semantics=sem, vmem_limit_bytes=vmem)


def _tile(n, pref):
    if n <= pref:
        return n
    t = pref
    while t >= 128:
        if n % t == 0:
            return t
        t -= 128
    return n


def _spec(arr, tr, tc, rc):
    if arr.ndim == 2:
        return pl.BlockSpec((tr, tc), lambda *g: rc(*g))
    per = arr.shape[2] // tc
    assert arr.shape[2] % tc == 0

    def imap(*g):
        r, c = rc(*g)
        return (c // per, r, c % per)

    return pl.BlockSpec((None, tr, tc), imap)


def _lshape(arr):
    return arr.shape if arr.ndim == 2 else (arr.shape[1], arr.shape[0] * arr.shape[2])


def _mm(a, b, mode, *, name, out_dtype=F32, res=None, epi=None, aux=None, out_stack=None, tm=1024, tn=1024, tk=2048):
    la, lb = _lshape(a), _lshape(b)
    if mode == "nn":
        (M, K), (K2, N) = la, lb
    elif mode == "nt":
        (M, K), (N, K2) = la, lb
    else:
        (K, M), (K2, N) = la, lb
    assert K == K2, (la, lb, mode)
    lim = {"m": M, "n": N // out_stack if out_stack else N, "k": K}
    stacked = [(a, "m" if mode == "tn" else "k"), (b, "k" if mode == "nt" else "n"), (res, "n"), (aux, "n")]
    for arr, dim in stacked:
        if arr is not None and arr.ndim == 3:
            lim[dim] = math.gcd(lim[dim], arr.shape[2])
    tm, tn, tk = _tile(lim["m"], tm), _tile(lim["n"], tn), _tile(lim["k"], tk)
    nk = K // tk
    grid = (M // tm, N // tn, nk)
    a_spec = _spec(a, tk, tm, lambda i, j, k: (k, i)) if mode == "tn" else _spec(a, tm, tk, lambda i, j, k: (i, k))
    b_spec = _spec(b, tn, tk, lambda i, j, k: (j, k)) if mode == "nt" else _spec(b, tk, tn, lambda i, j, k: (k, j))
    dims = {"nn": ((1,), (0,)), "nt": ((1,), (1,)), "tn": ((0,), (0,))}[mode]
    ins, in_specs = [a, b], [a_spec, b_spec]
    for extra in (res, aux):
        if extra is not None:
            ins.append(extra)
            in_specs.append(_spec(extra, tm, tn, lambda i, j, k: (i, j)))
    if out_stack:
        o_shape = jax.ShapeDtypeStruct((out_stack, M, N // out_stack), out_dtype)
    else:
        o_shape = jax.ShapeDtypeStruct((M, N), out_dtype)
    o_spec = _spec(o_shape, tm, tn, lambda i, j, k: (i, j))
    n_out = 2 if epi == "relu2" else 1
    has_res, has_aux = res is not None, aux is not None

    def body(*refs):
        a_ref, b_ref = refs[0], refs[1]
        pos = 2
        res_ref = aux_ref = None
        if has_res:
            res_ref = refs[pos]
            pos += 1
        if has_aux:
            aux_ref = refs[pos]
            pos += 1
        outs = refs[pos:pos + n_out]
        part = lax.dot_general(a_ref[...].astype(BF16), b_ref[...].astype(BF16), (dims, ((), ())), preferred_element_type=F32)
        if nk > 1:
            acc_ref = refs[pos + n_out]
            k = pl.program_id(2)

            @pl.when(k == 0)
            def _():
                acc_ref[...] = part

            @pl.when(k > 0)
            def _():
                acc_ref[...] += part

        @pl.when(pl.program_id(2) == nk - 1)
        def _():
            o = acc_ref[...] if nk > 1 else part
            if has_res:
                o = o + res_ref[...].astype(F32)
            if epi == "relu2":
                outs[0][...] = o.astype(outs[0].dtype)
                r = jnp.maximum(o, 0.0)
                outs[1][...] = (r * r).astype(outs[1].dtype)
            elif epi == "relu2_bwd":
                outs[0][...] = (o * (2.0 * jnp.maximum(aux_ref[...].astype(F32), 0.0))).astype(outs[0].dtype)
            else:
                outs[0][...] = o.astype(outs[0].dtype)

    out = pl.pallas_call(
        body, grid=grid, in_specs=in_specs,
        out_specs=[o_spec] * n_out if n_out > 1 else o_spec,
        out_shape=[o_shape] * n_out if n_out > 1 else o_shape,
        scratch_shapes=[pltpu.VMEM((tm, tn), F32)] if nk > 1 else [],
        compiler_params=_cparams(("parallel", "parallel", "arbitrary"), VMEM_LIMIT), name=name,
    )(*ins)
    return out


def _rms_fwd(x, g, *, name, out_dtype=BF16):
    L, D = x.shape
    tr = _tile(L, 256)

    def body(x_ref, g_ref, o_ref):
        xv = x_ref[...]
        r = lax.rsqrt(jnp.mean(xv * xv, axis=-1, keepdims=True) + NORM_EPS)
        o_ref[...] = (xv * r * g_ref[...]).astype(o_ref.dtype)

    return pl.pallas_call(
        body, grid=(L // tr,),
        in_specs=[pl.BlockSpec((tr, D), lambda i: (i, 0)), pl.BlockSpec((1, D), lambda i: (0, 0))],
        out_specs=pl.BlockSpec((tr, D), lambda i: (i, 0)),
        out_shape=jax.ShapeDtypeStruct((L, D), out_dtype),
        compiler_params=_cparams(("parallel",)), name=name)(x, g)


def _rms_bwd(x, g, dy, dres, *, name):
    L, D = x.shape
    tr = _tile(L, 256)
    has_res = dres is not None

    def body(*refs):
        x_ref, g_ref, dy_ref = refs[:3]
        dres_ref = refs[3] if has_res else None
        dx_ref, dg_ref = refs[-2:]
        xv = x_ref[...]
        dyv = dy_ref[...].astype(F32)
        r = lax.rsqrt(jnp.mean(xv * xv, axis=-1, keepdims=True) + NORM_EPS)
        xh = xv * r
        dyg = dyv * g_ref[...]
        dx = r * (dyg - xh * jnp.mean(dyg * xh, axis=-1, keepdims=True))
        if has_res:
            dx = dx + dres_ref[...]
        dx_ref[...] = dx

        @pl.when(pl.program_id(0) == 0)
        def _():
            dg_ref[...] = jnp.zeros_like(dg_ref)

        dg_ref[...] += jnp.sum(dyv * xh, axis=0, keepdims=True)

    row = pl.BlockSpec((tr, D), lambda i: (i, 0))
    vec = pl.BlockSpec((1, D), lambda i: (0, 0))
    ins = [x, g, dy] + ([dres] if has_res else [])
    return pl.pallas_call(
        body, grid=(L // tr,), in_specs=[row, vec, row] + ([row] if has_res else []),
        out_specs=[row, vec],
        out_shape=[jax.ShapeDtypeStruct((L, D), F32), jax.ShapeDtypeStruct((1, D), F32)],
        compiler_params=_cparams(("arbitrary",)), name=name)(*ins)


def _loss_head(h, g, target, *, name):
    L, D = h.shape
    tr = _tile(L, 256)

    def body(x_ref, g_ref, t_ref, loss_ref, dx_ref, dg_ref):
        xv = x_ref[...]
        r = lax.rsqrt(jnp.mean(xv * xv, axis=-1, keepdims=True) + NORM_EPS)
        xh = xv * r
        err = xh * g_ref[...] - t_ref[...]
        dyv = err * (1.0 / D)
        dyg = dyv * g_ref[...]
        dx_ref[...] = r * (dyg - xh * jnp.mean(dyg * xh, axis=-1, keepdims=True))

        @pl.when(pl.program_id(0) == 0)
        def _():
            dg_ref[...] = jnp.zeros_like(dg_ref)
            loss_ref[...] = jnp.zeros_like(loss_ref)

        dg_ref[...] += jnp.sum(dyv * xh, axis=0, keepdims=True)
        loss_ref[...] += 0.5 * jnp.sum(jnp.sum(err * err, axis=1, keepdims=True), axis=0, keepdims=True) * (1.0 / D)

    row = pl.BlockSpec((tr, D), lambda i: (i, 0))
    vec = pl.BlockSpec((1, D), lambda i: (0, 0))
    return pl.pallas_call(
        body, grid=(L // tr,), in_specs=[row, vec, row],
        out_specs=[pl.BlockSpec((1, 1), lambda i: (0, 0)), row, vec],
        out_shape=[jax.ShapeDtypeStruct((1, 1), F32), jax.ShapeDtypeStruct((L, D), F32), jax.ShapeDtypeStruct((1, D), F32)],
        compiler_params=_cparams(("arbitrary",)), name=name)(h, g, target)


S5_TL = 256
S5_LW = 512


def _dot(a, b, dims):
    return lax.dot_general(a, b, (dims, ((), ())), preferred_element_type=F32)


_NN, _NT, _TN = ((1,), (0,)), ((1,), (1,)), ((0,), (0,))


def _gelu_and_grad(y):
    k = math.sqrt(2.0 / math.pi)
    y2 = y * y
    th = jnp.tanh(k * (y + 0.044715 * y * y2))
    g = 0.5 * y * (1.0 + th)
    gp = 0.5 * (1.0 + th) + 0.5 * y * (1.0 - th * th) * k * (1.0 + 3.0 * 0.044715 * y2)
    return g, gp


def _scan_tiles(re_s, im_s, ls, lw, pw_ref, tr_ref, ti_ref, carry_s, nt, reverse, extra=None):
    row = lax.broadcasted_iota(jnp.int32, (8, lw), 0)
    a = [pw_ref[k:k + 1, ls] for k in range(6)]
    tr_t, ti_t = tr_ref[:, ls], ti_ref[:, ls]
    edge = 0 if reverse else 7

    def tile(jj, carry):
        cr, ci, acc = carry
        j = (nt - 1 - jj) if reverse else jj
        off = pl.multiple_of(j * 8, 8)
        sr, si = re_s[pl.ds(off, 8), ls], im_s[pl.ds(off, 8), ls]
        for n, k in enumerate((1, 2, 4)):
            akr, aki = a[2 * n], a[2 * n + 1]
            if reverse:
                keep, sh = row < 8 - k, 8 - k
            else:
                keep, sh = row >= k, k
            shr = jnp.where(keep, pltpu.roll(sr, sh, 0), 0.0)
            shi = jnp.where(keep, pltpu.roll(si, sh, 0), 0.0)
            sr, si = sr + akr * shr - aki * shi, si + akr * shi + aki * shr
        xr = sr + tr_t * cr - ti_t * ci
        xi = si + tr_t * ci + ti_t * cr
        re_s[pl.ds(off, 8), ls] = xr
        im_s[pl.ds(off, 8), ls] = xi
        if extra is not None:
            acc = extra(off, xr, xi, acc, row)
        return xr[edge:edge + 1, :], xi[edge:edge + 1, :], acc

    acc0 = (jnp.zeros((8, lw), F32), jnp.zeros((8, lw), F32)) if extra is not None else 0
    cr, ci, acc = lax.fori_loop(0, nt, tile, (carry_s[0:1, ls], carry_s[1:2, ls], acc0))
    carry_s[0:1, ls] = cr
    carry_s[1:2, ls] = ci
    return acc


def _s5_fwd(proj, bdr, bdi, pw, tr, ti, cdr, cdi, dskip, wglu, *, name):
    L = proj.shape[0]
    GW, S = bdr.shape
    TL = _tile(L, S5_TL)
    lw = min(S, S5_LW)

    def body(u_ref, bdr_ref, bdi_ref, pw_ref, tr_ref, ti_ref, cdr_ref, cdi_ref, d_ref, w_ref,
             y_ref, xr_ref, xi_ref, re_s, im_s, carry_s):
        @pl.when(pl.program_id(0) == 0)
        def _():
            carry_s[...] = jnp.zeros_like(carry_s)

        u = u_ref[...]
        ub = u.astype(BF16)
        re_s[...] = _dot(ub, bdr_ref[...], _NN)
        im_s[...] = _dot(ub, bdi_ref[...], _NN)
        for lc in range(S // lw):
            _scan_tiles(re_s, im_s, slice(lc * lw, (lc + 1) * lw), lw, pw_ref, tr_ref, ti_ref, carry_s, TL // 8, False)
        xrb, xib = re_s[...].astype(BF16), im_s[...].astype(BF16)
        xr_ref[...] = xrb
        xi_ref[...] = xib
        y = _dot(xrb, cdr_ref[...], _NN) - _dot(xib, cdi_ref[...], _NN) + d_ref[...] * u
        g, _ = _gelu_and_grad(y)
        z = _dot(g.astype(BF16), w_ref[...], _NN)
        y_ref[...] = g * jax.nn.sigmoid(z)

    full = lambda arr: pl.BlockSpec(arr.shape, lambda i: (0,) * arr.ndim)
    return pl.pallas_call(
        body, grid=(L // TL,),
        in_specs=[pl.BlockSpec((TL, GW), lambda i: (i, 0))] + [full(t) for t in (bdr, bdi, pw, tr, ti, cdr, cdi, dskip, wglu)],
        out_specs=[pl.BlockSpec((TL, GW), lambda i: (i, 0)), pl.BlockSpec((TL, S), lambda i: (i, 0)), pl.BlockSpec((TL, S), lambda i: (i, 0))],
        out_shape=[jax.ShapeDtypeStruct((L, GW), F32), jax.ShapeDtypeStruct((L, S), BF16), jax.ShapeDtypeStruct((L, S), BF16)],
        scratch_shapes=[pltpu.VMEM((TL, S), F32), pltpu.VMEM((TL, S), F32), pltpu.VMEM((8, S), F32)],
        compiler_params=_cparams(("arbitrary",), VMEM_LIMIT), name=name,
    )(proj, bdr, bdi, pw, tr, ti, cdr, cdi, dskip, wglu)


def _s5_bwd(proj, xr, xi, dout, bdr, bdi, pwc, trc, tic, cdr, cdi, dskip, wglu, *, name):
    L = proj.shape[0]
    GW, S = bdr.shape
    TL = _tile(L, S5_TL)
    nc = L // TL
    lw = min(S, S5_LW)

    def body(u_ref, xr_ref, xi_ref, xrp_ref, xip_ref, do_ref, bdr_ref, bdi_ref, pw_ref, tr_ref, ti_ref, cdr_ref, cdi_ref,
             d_ref, w_ref, du_ref, dw_ref, dd_ref, dcdr_ref, dcdi_ref, dbdr_ref, dbdi_ref, dar_ref, dai_ref,
             gr_s, gi_s, xfr, xfi, carry_s):
        i = pl.program_id(0)

        @pl.when(i == 0)
        def _():
            carry_s[...] = jnp.zeros_like(carry_s)
            for r in (dw_ref, dd_ref, dcdr_ref, dcdi_ref, dbdr_ref, dbdi_ref, dar_ref, dai_ref):
                r[...] = jnp.zeros_like(r)

        u = u_ref[...]
        ub = u.astype(BF16)
        xrb, xib = xr_ref[...], xi_ref[...]
        y = _dot(xrb, cdr_ref[...], _NN) - _dot(xib, cdi_ref[...], _NN) + d_ref[...] * u
        g, gp = _gelu_and_grad(y)
        gb = g.astype(BF16)
        sg = jax.nn.sigmoid(_dot(gb, w_ref[...], _NN))
        dout = do_ref[...]
        dz = (dout * g * sg * (1.0 - sg)).astype(BF16)
        dg = dout * sg + _dot(dz, w_ref[...], _NT)
        dw_ref[...] += _dot(gb, dz, _TN)
        dy = dg * gp
        dyb = dy.astype(BF16)
        dd_ref[...] += jnp.sum(dy * u, axis=0, keepdims=True)
        gr_s[...] = _dot(dyb, cdr_ref[...], _NT)
        gi_s[...] = -_dot(dyb, cdi_ref[...], _NT)
        dcdr_ref[...] += _dot(xrb, dyb, _TN)
        dcdi_ref[...] -= _dot(xib, dyb, _TN)
        live = (i < nc - 1).astype(F32)
        xfr[0:8, :] = xrp_ref[...].astype(F32) * live
        xfi[0:8, :] = xip_ref[...].astype(F32) * live
        xfr[8:, :] = xrb.astype(F32)
        xfi[8:, :] = xib.astype(F32)
        for lc in range(S // lw):
            ls = slice(lc * lw, (lc + 1) * lw)

            def extra(off, lr, li, acc, row, ls=ls):
                cur_r, cur_i = xfr[pl.ds(off + 8, 8), ls], xfi[pl.ds(off + 8, 8), ls]
                prv_r, prv_i = xfr[pl.ds(off, 8), ls], xfi[pl.ds(off, 8), ls]
                xpr = jnp.where(row >= 1, pltpu.roll(cur_r, 1, 0), prv_r[7:8, :])
                xpi = jnp.where(row >= 1, pltpu.roll(cur_i, 1, 0), prv_i[7:8, :])
                return acc[0] + lr * xpr + li * xpi, acc[1] - lr * xpi + li * xpr

            acc = _scan_tiles(gr_s, gi_s, ls, lw, pw_ref, tr_ref, ti_ref, carry_s, TL // 8, True, extra)
            dar_ref[:, ls] += acc[0]
            dai_ref[:, ls] += acc[1]
        lrb, lib = gr_s[...].astype(BF16), gi_s[...].astype(BF16)
        du_ref[...] = dy * d_ref[...] + _dot(lrb, bdr_ref[...], _NT) + _dot(lib, bdi_ref[...], _NT)
        dbdr_ref[...] += _dot(ub, lrb, _TN)
        dbdi_ref[...] += _dot(ub, lib, _TN)

    rev = lambda i: nc - 1 - i
    full = lambda arr: pl.BlockSpec(arr.shape, lambda i: (0,) * arr.ndim)
    chunk = lambda w: pl.BlockSpec((TL, w), lambda i: (rev(i), 0))
    prev8 = pl.BlockSpec((8, S), lambda i: (jnp.maximum(rev(i) * (TL // 8) - 1, 0), 0))
    acc_shapes = [(GW, GW), (1, GW), (S, GW), (S, GW), (GW, S), (GW, S), (8, S), (8, S)]
    return pl.pallas_call(
        body, grid=(nc,),
        in_specs=[chunk(GW), chunk(S), chunk(S), prev8, prev8, chunk(GW)] + [full(t) for t in (bdr, bdi, pwc, trc, tic, cdr, cdi, dskip, wglu)],
        out_specs=[chunk(GW)] + [pl.BlockSpec(s, lambda i: (0, 0)) for s in acc_shapes],
        out_shape=[jax.ShapeDtypeStruct((L, GW), F32)] + [jax.ShapeDtypeStruct(s, F32) for s in acc_shapes],
        scratch_shapes=[pltpu.VMEM((TL, S), F32), pltpu.VMEM((TL, S), F32), pltpu.VMEM((TL + 8, S), F32),
                        pltpu.VMEM((TL + 8, S), F32), pltpu.VMEM((8, S), F32)],
        compiler_params=_cparams(("arbitrary",), VMEM_LIMIT), name=name,
    )(proj, xr, xi, xr, xi, dout, bdr, bdi, pwc, trc, tic, cdr, cdi, dskip, wglu)


def _cpow_tables(ar, ai):
    def cm(a, b):
        return a[0] * b[0] - a[1] * b[1], a[0] * b[1] + a[1] * b[0]
    p = [(ar, ai)]
    for _ in range(7):
        p.append(cm(p[-1], p[0]))
    z = jnp.zeros_like(ar)
    pw = jnp.stack([p[0][0], p[0][1], p[1][0], p[1][1], p[3][0], p[3][1], z, z])
    return pw, jnp.stack([q[0] for q in p]), jnp.stack([q[1] for q in p])


def _s5_disc(lam_re, lam_im, log_dt, b_re, b_im):
    dt = jnp.exp(log_dt)[:, None]
    mag = jnp.exp(lam_re * dt)
    ab_r, ab_i = mag * jnp.cos(lam_im * dt), mag * jnp.sin(lam_im * dt)
    den = lam_re * lam_re + lam_im * lam_im
    nr, ni = ab_r - 1.0, ab_i
    f_r = ((nr * lam_re + ni * lam_im) / den)[..., None]
    f_i = ((ni * lam_re - nr * lam_im) / den)[..., None]
    return ab_r, ab_i, f_r * b_re - f_i * b_im, f_r * b_im + f_i * b_re


def _bd(t):
    G, A, B = t.shape
    return (t[:, :, None, :] * jnp.eye(G, dtype=t.dtype)[:, None, :, None]).reshape(G * A, G * B)


def _bd_extract(m, G):
    A, B = m.shape[0] // G, m.shape[1] // G
    return jnp.sum(m.reshape(G, A, G, B) * jnp.eye(G, dtype=m.dtype)[:, None, :, None], axis=2)


POOL_TQ = 256


def _band(shape, row0, col0, win, transpose):
    r = lax.broadcasted_iota(jnp.int32, shape, 0) + row0
    c = lax.broadcasted_iota(jnp.int32, shape, 1) + col0
    d = (c - r) if transpose else (r - c)
    return ((d >= 0) & (d < win)).astype(BF16)


def _band_dot(band, v):
    hi = v.astype(BF16)
    lo = (v - hi.astype(F32)).astype(BF16)
    return _dot(band, hi, _NN) + _dot(band, lo, _NN)


def _pool_p(u_prev, u_cur, i, win, tq):
    t0 = i * tq
    cnt = jnp.minimum(lax.broadcasted_iota(jnp.int32, (tq, 1), 0) + t0 + 1, win).astype(F32)
    live = (i > 0).astype(F32)
    acc = _band_dot(_band((tq, tq), t0, t0, win, False), u_cur)
    acc += _band_dot(_band((tq, tq), t0, t0 - tq, win, False), u_prev * live)
    return acc / cnt - u_cur


def _pool_fwd(proj, pool_w, pool_scale3, *, name):
    L = proj.shape[0]
    nw, PC, _ = pool_w.shape
    tq = _tile(L, POOL_TQ)

    def body(up_ref, uc_ref, w_ref, s_ref, y_ref):
        g, i = pl.program_id(0), pl.program_id(1)
        win = jnp.left_shift(2, g)
        p = _pool_p(up_ref[...], uc_ref[...], i, win, tq)
        y_ref[...] = _dot(p.astype(BF16), w_ref[...].astype(BF16), _NN) * s_ref[...]

    return pl.pallas_call(
        body, grid=(nw, L // tq),
        in_specs=[pl.BlockSpec((tq, PC), lambda g, i: (jnp.maximum(i - 1, 0), nw + g)),
                  pl.BlockSpec((tq, PC), lambda g, i: (i, nw + g)),
                  pl.BlockSpec((None, PC, PC), lambda g, i: (g, 0, 0)),
                  pl.BlockSpec((None, 1, PC), lambda g, i: (g, 0, 0))],
        out_specs=pl.BlockSpec((tq, PC), lambda g, i: (i, g)),
        out_shape=jax.ShapeDtypeStruct((L, nw * PC), F32),
        compiler_params=_cparams(("parallel", "parallel")), name=name)(proj, proj, pool_w, pool_scale3)


def _pool_bwd(proj, dy, pool_w, pool_scale3, *, name):
    L = proj.shape[0]
    nw, PC, _ = pool_w.shape
    tq = _tile(L, POOL_TQ)
    nq = L // tq

    def body(up_ref, uc_ref, dyc_ref, dyn_ref, w_ref, s_ref, du_ref, dw_ref, ds_ref):
        g, i = pl.program_id(0), pl.program_id(1)
        win = jnp.left_shift(2, g)

        @pl.when(i == 0)
        def _():
            dw_ref[...] = jnp.zeros_like(dw_ref)
            ds_ref[...] = jnp.zeros_like(ds_ref)

        wb = w_ref[...].astype(BF16)
        p = _pool_p(up_ref[...], uc_ref[...], i, win, tq).astype(BF16)
        dyc = dyc_ref[...]
        ds_ref[...] += jnp.sum(dyc * _dot(p, wb, _NN), axis=0, keepdims=True)
        dys = (dyc * s_ref[...]).astype(BF16)
        dw_ref[...] += _dot(p, dys, _TN)
        t0 = i * tq
        rows = lax.broadcasted_iota(jnp.int32, (tq, 1), 0)
        dp_c = _dot(dys, wb, _NT)
        q_c = dp_c / jnp.minimum(rows + t0 + 1, win).astype(F32)
        live = (i < nq - 1).astype(F32)
        dp_n = _dot((dyn_ref[...] * s_ref[...] * live).astype(BF16), wb, _NT)
        q_n = dp_n / jnp.minimum(rows + t0 + tq + 1, win).astype(F32)
        du = _band_dot(_band((tq, tq), t0, t0, win, True), q_c) + _band_dot(_band((tq, tq), t0, t0 + tq, win, True), q_n)
        du_ref[...] = du - dp_c

    return pl.pallas_call(
        body, grid=(nw, nq),
        in_specs=[pl.BlockSpec((tq, PC), lambda g, i: (jnp.maximum(i - 1, 0), nw + g)),
                  pl.BlockSpec((tq, PC), lambda g, i: (i, nw + g)),
                  pl.BlockSpec((tq, PC), lambda g, i: (i, g)),
                  pl.BlockSpec((tq, PC), lambda g, i: (jnp.minimum(i + 1, nq - 1), g)),
                  pl.BlockSpec((None, PC, PC), lambda g, i: (g, 0, 0)),
                  pl.BlockSpec((None, 1, PC), lambda g, i: (g, 0, 0))],
        out_specs=[pl.BlockSpec((tq, PC), lambda g, i: (i, g)),
                   pl.BlockSpec((None, PC, PC), lambda g, i: (g, 0, 0)),
                   pl.BlockSpec((None, 1, PC), lambda g, i: (g, 0, 0))],
        out_shape=[jax.ShapeDtypeStruct((L, nw * PC), F32), jax.ShapeDtypeStruct((nw, PC, PC), F32),
                   jax.ShapeDtypeStruct((nw, 1, PC), F32)],
        compiler_params=_cparams(("parallel", "arbitrary")), name=name)(proj, proj, dy, dy, pool_w, pool_scale3)


CONV_RC = 256


def _conv1_fwd(proj, w_dw, b_dw, *, name):
    L = proj.shape[0]
    GW = w_dw.shape[1]
    CB = min(GW, 128)
    nb = GW // CB
    RC = _tile(L, CONV_RC)
    n = RC + CONV_PAD

    def body(val_ref, gate_ref, w_ref, b_ref, o_ref, hp):
        hp[0:CONV_PAD, :] = jnp.zeros((CONV_PAD, CB), F32)
        hp[CONV_PAD:, :] = val_ref[...] * jax.nn.sigmoid(gate_ref[...])
        wv = w_ref[...]

        def chunk(ci, carry):
            c0 = pl.multiple_of(ci * RC, 8)
            win = hp[pl.ds(c0, n), :]
            acc = jnp.zeros((RC, CB), F32) + b_ref[...]
            for k in range(CONV_WIDTH):
                acc = acc + wv[k:k + 1, :] * pltpu.roll(win, n - (k + 2), 0)[0:RC]
            o_ref[pl.ds(c0, RC), :] = acc
            return carry

        lax.fori_loop(0, L // RC, chunk, 0)

    col = lambda off: pl.BlockSpec((L, CB), lambda c: (0, off * nb + c))
    return pl.pallas_call(
        body, grid=(nb,),
        in_specs=[col(2), col(3), pl.BlockSpec((CONV_PAD, CB), lambda c: (0, c)), pl.BlockSpec((1, CB), lambda c: (0, c))],
        out_specs=pl.BlockSpec((L, CB), lambda c: (0, c)),
        out_shape=jax.ShapeDtypeStruct((L, GW), F32),
        scratch_shapes=[pltpu.VMEM((L + CONV_PAD, CB), F32)],
        compiler_params=_cparams(("parallel",)), name=name)(proj, proj, w_dw, b_dw)


def _conv1_bwd(proj, dhc, w_dw, *, name):
    L = proj.shape[0]
    GW = w_dw.shape[1]
    CB = min(GW, 128)
    nb = GW // CB
    RC = _tile(L, CONV_RC)
    n = RC + CONV_PAD

    def body(val_ref, gate_ref, d_ref, w_ref, dval_ref, dgate_ref, dw_ref, db_ref, hp, dp):
        val = val_ref[...]
        sg = jax.nn.sigmoid(gate_ref[...])
        hp[0:CONV_PAD, :] = jnp.zeros((CONV_PAD, CB), F32)
        hp[CONV_PAD:, :] = val * sg
        dp[0:L, :] = d_ref[...]
        dp[L:, :] = jnp.zeros((CONV_PAD, CB), F32)
        dw_ref[...] = jnp.zeros_like(dw_ref)
        db_ref[...] = jnp.sum(d_ref[...], axis=0, keepdims=True)
        wv = w_ref[...]

        def chunk(ci, carry):
            c0 = pl.multiple_of(ci * RC, 8)
            win = hp[pl.ds(c0, n), :]
            dwin = dp[pl.ds(c0, n), :]
            d = dwin[0:RC]
            dh = jnp.zeros((RC, CB), F32)
            for k in range(CONV_WIDTH):
                dw_ref[k:k + 1, :] += jnp.sum(d * pltpu.roll(win, n - (k + 2), 0)[0:RC], axis=0, keepdims=True)
                sh = CONV_WIDTH - 1 - k
                dh = dh + wv[k:k + 1, :] * (pltpu.roll(dwin, n - sh, 0)[0:RC] if sh else d)
            v, s = val_ref[pl.ds(c0, RC), :], jax.nn.sigmoid(gate_ref[pl.ds(c0, RC), :])
            dval_ref[pl.ds(c0, RC), :] = dh * s
            dgate_ref[pl.ds(c0, RC), :] = dh * v * s * (1.0 - s)
            return carry

        lax.fori_loop(0, L // RC, chunk, 0)

    col = lambda off: pl.BlockSpec((L, CB), lambda c: (0, off * nb + c))
    own = pl.BlockSpec((L, CB), lambda c: (0, c))
    return pl.pallas_call(
        body, grid=(nb,),
        in_specs=[col(2), col(3), own, pl.BlockSpec((CONV_PAD, CB), lambda c: (0, c))],
        out_specs=[own, own, pl.BlockSpec((CONV_PAD, CB), lambda c: (0, c)), pl.BlockSpec((1, CB), lambda c: (0, c))],
        out_shape=[jax.ShapeDtypeStruct((L, GW), F32), jax.ShapeDtypeStruct((L, GW), F32),
                   jax.ShapeDtypeStruct((CONV_PAD, GW), F32), jax.ShapeDtypeStruct((1, GW), F32)],
        scratch_shapes=[pltpu.VMEM((L + CONV_PAD, CB), F32), pltpu.VMEM((L + CONV_PAD, CB), F32)],
        compiler_params=_cparams(("parallel",)), name=name)(proj, proj, dhc, w_dw)


def _ln_silu(hc, g, b):
    mu = jnp.mean(hc, axis=-1, keepdims=True)
    xc = hc - mu
    r = lax.rsqrt(jnp.mean(xc * xc, axis=-1, keepdims=True) + NORM_EPS)
    xh = xc * r
    a = xh * g + b
    sg = jax.nn.sigmoid(a)
    return xh, r, a, sg


def _conv2_fwd(hc, ln_g, ln_b, w_pw, *, name):
    L, GW = hc.shape
    tr = _tile(L, 256)

    def body(h_ref, g_ref, b_ref, w_ref, y_ref):
        _, _, a, sg = _ln_silu(h_ref[...], g_ref[...], b_ref[...])
        y_ref[...] = _dot((a * sg).astype(BF16), w_ref[...], _NN)

    row = pl.BlockSpec((tr, GW), lambda i: (i, 0))
    vec = pl.BlockSpec((1, GW), lambda i: (0, 0))
    return pl.pallas_call(
        body, grid=(L // tr,), in_specs=[row, vec, vec, pl.BlockSpec((GW, GW), lambda i: (0, 0))],
        out_specs=row, out_shape=jax.ShapeDtypeStruct((L, GW), F32),
        compiler_params=_cparams(("parallel",)), name=name)(hc, ln_g, ln_b, w_pw)


def _conv2_bwd(hc, ln_g, ln_b, w_pw, dy, *, name):
    L, GW = hc.shape
    tr = _tile(L, 256)

    def body(h_ref, g_ref, b_ref, w_ref, dy_ref, dh_ref, dg_ref, db_ref, dw_ref):
        @pl.when(pl.program_id(0) == 0)
        def _():
            for r in (dg_ref, db_ref, dw_ref):
                r[...] = jnp.zeros_like(r)

        xh, r, a, sg = _ln_silu(h_ref[...], g_ref[...], b_ref[...])
        dyb = dy_ref[...].astype(BF16)
        dw_ref[...] += _dot((a * sg).astype(BF16), dyb, _TN)
        da = _dot(dyb, w_ref[...], _NT) * (sg + a * sg * (1.0 - sg))
        dg_ref[...] += jnp.sum(da * xh, axis=0, keepdims=True)
        db_ref[...] += jnp.sum(da, axis=0, keepdims=True)
        dxh = da * g_ref[...]
        dh_ref[...] = r * (dxh - jnp.mean(dxh, axis=-1, keepdims=True) - xh * jnp.mean(dxh * xh, axis=-1, keepdims=True))

    row = pl.BlockSpec((tr, GW), lambda i: (i, 0))
    vec = pl.BlockSpec((1, GW), lambda i: (0, 0))
    mat = pl.BlockSpec((GW, GW), lambda i: (0, 0))
    return pl.pallas_call(
        body, grid=(L // tr,), in_specs=[row, vec, vec, mat, row],
        out_specs=[row, vec, vec, mat],
        out_shape=[jax.ShapeDtypeStruct((L, GW), F32), jax.ShapeDtypeStruct((1, GW), F32), jax.ShapeDtypeStruct((1, GW), F32),
                   jax.ShapeDtypeStruct((GW, GW), F32)],
        compiler_params=_cparams(("arbitrary",)), name=name)(hc, ln_g, ln_b, w_pw, dy)


def _grp_fwd(ys, g, *, name):
    L, GW = ys[0].shape
    tr = _tile(L, 256)

    def body(*refs):
        g_ref, o_ref = refs[4], refs[5]
        for m in range(N_MIXERS):
            yv = refs[m][...]
            r = lax.rsqrt(jnp.mean(yv * yv, axis=-1, keepdims=True) + NORM_EPS)
            o_ref[:, m * GW:(m + 1) * GW] = (yv * r * g_ref[:, m * GW:(m + 1) * GW]).astype(o_ref.dtype)

    row = pl.BlockSpec((tr, GW), lambda i: (i, 0))
    return pl.pallas_call(
        body, grid=(L // tr,), in_specs=[row] * 4 + [pl.BlockSpec((1, N_MIXERS * GW), lambda i: (0, 0))],
        out_specs=pl.BlockSpec((tr, N_MIXERS * GW), lambda i: (i, 0)),
        out_shape=jax.ShapeDtypeStruct((L, N_MIXERS * GW), BF16),
        compiler_params=_cparams(("parallel",)), name=name)(*ys, g)


def _grp_bwd(ys, g, dy, *, name):
    L, GW = ys[0].shape
    tr = _tile(L, 256)

    def body(*refs):
        g_ref, dy_ref = refs[4], refs[5]
        outs, dg_ref = refs[6:10], refs[10]

        @pl.when(pl.program_id(0) == 0)
        def _():
            dg_ref[...] = jnp.zeros_like(dg_ref)

        for m in range(N_MIXERS):
            cs = slice(m * GW, (m + 1) * GW)
            yv = refs[m][...]
            r = lax.rsqrt(jnp.mean(yv * yv, axis=-1, keepdims=True) + NORM_EPS)
            xh = yv * r
            dyv = dy_ref[:, cs]
            dyg = dyv * g_ref[:, cs]
            outs[m][...] = r * (dyg - xh * jnp.mean(dyg * xh, axis=-1, keepdims=True))
            dg_ref[:, cs] += jnp.sum(dyv * xh, axis=0, keepdims=True)

    row = pl.BlockSpec((tr, GW), lambda i: (i, 0))
    wide = pl.BlockSpec((tr, N_MIXERS * GW), lambda i: (i, 0))
    vec = pl.BlockSpec((1, N_MIXERS * GW), lambda i: (0, 0))
    return pl.pallas_call(
        body, grid=(L // tr,), in_specs=[row] * 4 + [vec, wide],
        out_specs=[row] * 4 + [vec],
        out_shape=[jax.ShapeDtypeStruct((L, GW), F32)] * 4 + [jax.ShapeDtypeStruct((1, N_MIXERS * GW), F32)],
        compiler_params=_cparams(("arbitrary",)), name=name)(*ys, g, dy)


def _att_first(j, br, nblk):
    per = lax.shift_right_logical(jnp.int32(nblk), 2 * br)
    return jnp.bitwise_and(j, per - 1) == 0


def _att_fwd(q, k, vx, bias, *, name):
    nbr, H, L, E = q.shape
    B = ATT_BLOCK
    nblk = L // B

    def body(q_ref, k_ref, v_ref, b_ref, o_ref):
        br = pl.program_id(0)
        lane = lax.broadcasted_iota(jnp.int32, (B, 2 * E), 1)

        def blk(j, carry):
            off = pl.multiple_of(j * B, B)
            poff = pl.multiple_of(jnp.maximum(j - 1, 0) * B, B)
            qv = q_ref[pl.ds(off, B), :]
            s_p = _dot(qv, k_ref[pl.ds(poff, B), :], _NT) + b_ref[:, 0:B]
            s_c = _dot(qv, k_ref[pl.ds(off, B), :], _NT) + b_ref[:, B:2 * B]
            s_p = jnp.where(_att_first(j, br, nblk), NEG_INF, s_p)
            m = jnp.maximum(jnp.max(s_p, axis=-1, keepdims=True), jnp.max(s_c, axis=-1, keepdims=True))
            p_p, p_c = jnp.exp(s_p - m), jnp.exp(s_c - m)
            den = jnp.sum(p_p, axis=-1, keepdims=True) + jnp.sum(p_c, axis=-1, keepdims=True)
            acc = _dot(p_p.astype(BF16), v_ref[pl.ds(poff, B), :], _NN) + _dot(p_c.astype(BF16), v_ref[pl.ds(off, B), :], _NN)
            o_ref[pl.ds(off, B), :] = jnp.where(lane < E, acc / den, m + jnp.log(den))
            return carry

        lax.fori_loop(0, nblk, blk, 0)

    seq = lambda w: pl.BlockSpec((None, None, L, w), lambda b, h: (b, h, 0, 0))
    return pl.pallas_call(
        body, grid=(nbr, H),
        in_specs=[seq(E), seq(E), seq(2 * E), pl.BlockSpec((None, None, B, 2 * B), lambda b, h: (b, h, 0, 0))],
        out_specs=seq(2 * E), out_shape=jax.ShapeDtypeStruct((nbr, H, L, 2 * E), F32),
        compiler_params=_cparams(("parallel", "parallel")), name=name)(q, k, vx, bias)


def _att_bwd(q, k, vx, bias, ox, dox, *, name):
    nbr, H, L, E = q.shape
    B = ATT_BLOCK
    nblk = L // B

    def body(q_ref, k_ref, v_ref, b_ref, o_ref, do_ref, dq_ref, dk_ref, dv_ref, db_ref):
        br = pl.program_id(0)
        dk_ref[...] = jnp.zeros_like(dk_ref)
        dv_ref[...] = jnp.zeros_like(dv_ref)
        db_ref[...] = jnp.zeros_like(db_ref)
        lane = lax.broadcasted_iota(jnp.int32, (B, 2 * E), 1)

        def blk(j, carry):
            off = pl.multiple_of(j * B, B)
            poff = pl.multiple_of(jnp.maximum(j - 1, 0) * B, B)
            qv, kc, kp = q_ref[pl.ds(off, B), :], k_ref[pl.ds(off, B), :], k_ref[pl.ds(poff, B), :]
            vc, vp = v_ref[pl.ds(off, B), :], v_ref[pl.ds(poff, B), :]
            oe, de = o_ref[pl.ds(off, B), :], do_ref[pl.ds(off, B), :]
            lse, dlse = oe[:, E:E + 1], de[:, E:E + 1]
            do = jnp.where(lane < E, de, 0.0)
            dm = jnp.sum(do * oe, axis=-1, keepdims=True) - dlse
            dob = do.astype(BF16)
            s_p = _dot(qv, kp, _NT) + b_ref[:, 0:B]
            s_c = _dot(qv, kc, _NT) + b_ref[:, B:2 * B]
            s_p = jnp.where(_att_first(j, br, nblk), NEG_INF, s_p)
            p_p, p_c = jnp.exp(s_p - lse), jnp.exp(s_c - lse)
            ds_p = p_p * (_dot(dob, vp, _NT) - dm)
            ds_c = p_c * (_dot(dob, vc, _NT) - dm)
            db_ref[:, 0:B] += ds_p
            db_ref[:, B:2 * B] += ds_c
            dsb_p, dsb_c = ds_p.astype(BF16), ds_c.astype(BF16)
            dq_ref[pl.ds(off, B), :] = _dot(dsb_p, kp, _NN) + _dot(dsb_c, kc, _NN)
            dk_ref[pl.ds(poff, B), :] += _dot(dsb_p, qv, _TN)
            dk_ref[pl.ds(off, B), :] += _dot(dsb_c, qv, _TN)
            dv_ref[pl.ds(poff, B), :] += _dot(p_p.astype(BF16), dob, _TN)
            dv_ref[pl.ds(off, B), :] += _dot(p_c.astype(BF16), dob, _TN)
            return carry

        lax.fori_loop(0, nblk, blk, 0)

    seq = lambda w: pl.BlockSpec((None, None, L, w), lambda b, h: (b, h, 0, 0))
    bsp = pl.BlockSpec((None, None, B, 2 * B), lambda b, h: (b, h, 0, 0))
    return pl.pallas_call(
        body, grid=(nbr, H),
        in_specs=[seq(E), seq(E), seq(2 * E), bsp, seq(2 * E), seq(2 * E)],
        out_specs=[seq(E), seq(E), seq(2 * E), bsp],
        out_shape=[jax.ShapeDtypeStruct((nbr, H, L, E), F32), jax.ShapeDtypeStruct((nbr, H, L, E), F32),
                   jax.ShapeDtypeStruct((nbr, H, L, 2 * E), F32), jax.ShapeDtypeStruct((nbr, H, B, 2 * B), F32)],
        compiler_params=_cparams(("parallel", "parallel")), name=name)(q, k, vx, bias, ox, dox)


def _mix_weights(xs, lane, E, W):
    al = [jnp.where(lane < E, pltpu.roll(x, W - E, 1), x) for x in xs]
    m = functools.reduce(jnp.maximum, al)
    es = [jnp.exp(a - m) for a in al]
    tot = functools.reduce(lambda a, b: a + b, es)
    return [e / tot for e in es]


def _mix_fwd(ox, *, name):
    nbr, L, W = ox.shape
    E = W // ATT_HEADS // 2
    tr = _tile(L, 256)

    def body(x_ref, o_ref):
        lane = lax.broadcasted_iota(jnp.int32, (tr, W), 1) % (2 * E)
        xs = [x_ref[b] for b in range(nbr)]
        ws = _mix_weights(xs, lane, E, W)
        o_ref[...] = functools.reduce(lambda a, b: a + b, [w * x for w, x in zip(ws, xs)])

    return pl.pallas_call(
        body, grid=(L // tr,), in_specs=[pl.BlockSpec((nbr, tr, W), lambda i: (0, i, 0))],
        out_specs=pl.BlockSpec((tr, W), lambda i: (i, 0)), out_shape=jax.ShapeDtypeStruct((L, W), F32),
        compiler_params=_cparams(("parallel",)), name=name)(ox)


def _mix_bwd(ox, dy, *, name):
    nbr, L, W = ox.shape
    E = W // ATT_HEADS // 2
    tr = _tile(L, 256)

    def body(x_ref, dy_ref, o_ref):
        lane = lax.broadcasted_iota(jnp.int32, (tr, W), 1) % (2 * E)
        xs = [x_ref[b] for b in range(nbr)]
        ws = _mix_weights(xs, lane, E, W)
        mix = functools.reduce(lambda a, b: a + b, [w * x for w, x in zip(ws, xs)])
        dyv = jnp.where(lane < E, dy_ref[...], 0.0)
        r = lax.broadcasted_iota(jnp.int32, (W, W), 0)
        c = lax.broadcasted_iota(jnp.int32, (W, W), 1)
        seg = ((r // (2 * E) == c // (2 * E)) & (r % (2 * E) < E)).astype(F32)
        for b in range(nbr):
            t = dyv * (xs[b] - mix)
            dl = ws[b] * jnp.dot(t, seg, preferred_element_type=F32, precision=lax.Precision.HIGHEST)
            o_ref[b] = jnp.where(lane < E, ws[b] * dyv, dl)

    return pl.pallas_call(
        body, grid=(L // tr,),
        in_specs=[pl.BlockSpec((nbr, tr, W), lambda i: (0, i, 0)), pl.BlockSpec((tr, W), lambda i: (i, 0))],
        out_specs=pl.BlockSpec((nbr, tr, W), lambda i: (0, i, 0)), out_shape=jax.ShapeDtypeStruct((nbr, L, W), F32),
        compiler_params=_cparams(("parallel",), VMEM_LIMIT), name=name)(ox, dy)


def _datt_lanes(GW):
    E = GW // ATT_HEADS
    lb = min(GW, 128)
    return lb, lb // E, E


def _datt_rows(c, br, nblk):
    dil = 4 ** br
    nbs = nblk // dil
    r, jb = c // nbs, c % nbs
    return r + dil * ATT_BLOCK * jb, r + dil * ATT_BLOCK * jnp.maximum(jb - 1, 0), jb == 0


def _datt_fwd(proj, bias, *, name):
    L = proj.shape[0]
    GW = proj.shape[1] // 7
    lb, hps, E = _datt_lanes(GW)
    B = ATT_BLOCK
    nblk = L // B
    scale = E ** -0.5

    def body(q_ref, k_ref, v_ref, b_ref, y_ref, lse_ref, m_s, l_s):
        m_s[...] = jnp.full(m_s.shape, NEG_INF, F32)
        l_s[...] = jnp.zeros(l_s.shape, F32)
        y_ref[...] = jnp.zeros(y_ref.shape, F32)
        lane = lax.broadcasted_iota(jnp.int32, (B, lb), 1) // E

        for br, (_, dil) in enumerate(DILATED_PATTERNS):
            def combo(c, carry, br=br, dil=dil):
                start, pstart, first = _datt_rows(c, br, nblk)
                rows, prows = pl.ds(start, B, stride=dil), pl.ds(pstart, B, stride=dil)
                q = q_ref[rows, :] * scale
                kc, kp = k_ref[rows, :].astype(BF16), k_ref[prows, :].astype(BF16)
                vc, vp = v_ref[rows, :].astype(BF16), v_ref[prows, :].astype(BF16)
                m_old, l_old, a_old = m_s[rows, :], l_s[rows, :], y_ref[rows, :]
                m_new, l_new, a_new = m_old, l_old, a_old
                for a in range(hps):
                    sel = lane == a
                    qm = jnp.where(sel, q, 0.0).astype(BF16)
                    s_p = _dot(qm, kp, _NT) + b_ref[br, a, :, 0:B]
                    s_c = _dot(qm, kc, _NT) + b_ref[br, a, :, B:2 * B]
                    s_p = jnp.where(first, NEG_INF, s_p)
                    mo, lo = m_old[:, a * E:a * E + 1], l_old[:, a * E:a * E + 1]
                    mn = jnp.maximum(mo, jnp.maximum(jnp.max(s_p, axis=-1, keepdims=True), jnp.max(s_c, axis=-1, keepdims=True)))
                    alpha = jnp.exp(mo - mn)
                    p_p, p_c = jnp.exp(s_p - mn), jnp.exp(s_c - mn)
                    ln = alpha * lo + jnp.sum(p_p, axis=-1, keepdims=True) + jnp.sum(p_c, axis=-1, keepdims=True)
                    pv = _dot(p_p.astype(BF16), vp, _NN) + _dot(p_c.astype(BF16), vc, _NN)
                    m_new = jnp.where(sel, mn, m_new)
                    l_new = jnp.where(sel, ln, l_new)
                    a_new = jnp.where(sel, alpha * a_old + pv, a_new)
                m_s[rows, :] = m_new
                l_s[rows, :] = l_new
                y_ref[rows, :] = a_new
                return carry

            lax.fori_loop(0, nblk, combo, 0)
        y_ref[...] = y_ref[...] / l_s[...]
        lse_ref[...] = m_s[...] + jnp.log(l_s[...])

    col = lambda off: pl.BlockSpec((L, lb), lambda h: (0, off * (GW // lb) + h))
    own = pl.BlockSpec((L, lb), lambda h: (0, h))
    return pl.pallas_call(
        body, grid=(GW // lb,),
        in_specs=[col(4), col(5), col(6), pl.BlockSpec((len(DILATED_PATTERNS), hps, B, 2 * B), lambda h: (0, h, 0, 0))],
        out_specs=[own, own], out_shape=[jax.ShapeDtypeStruct((L, GW), F32)] * 2,
        scratch_shapes=[pltpu.VMEM((L, lb), F32), pltpu.VMEM((L, lb), F32)],
        compiler_params=_cparams(("parallel",), VMEM_LIMIT), name=name)(proj, proj, proj, bias)


def _datt_bwd(proj, bias, y, lse, dy, *, name):
    L = proj.shape[0]
    GW = proj.shape[1] // 7
    lb, hps, E = _datt_lanes(GW)
    B = ATT_BLOCK
    nblk = L // B
    scale = E ** -0.5

    def body(q_ref, k_ref, v_ref, b_ref, y_ref, lse_ref, dy_ref, dq_ref, dk_ref, dv_ref, db_ref):
        for r in (dq_ref, dk_ref, dv_ref, db_ref):
            r[...] = jnp.zeros(r.shape, F32)
        lane = lax.broadcasted_iota(jnp.int32, (B, lb), 1) // E

        for br, (_, dil) in enumerate(DILATED_PATTERNS):
            def combo(c, carry, br=br, dil=dil):
                start, pstart, first = _datt_rows(c, br, nblk)
                rows, prows = pl.ds(start, B, stride=dil), pl.ds(pstart, B, stride=dil)
                q = q_ref[rows, :] * scale
                kc, kp = k_ref[rows, :].astype(BF16), k_ref[prows, :].astype(BF16)
                vc, vp = v_ref[rows, :].astype(BF16), v_ref[prows, :].astype(BF16)
                dyr, lser = dy_ref[rows, :], lse_ref[rows, :]
                dyy = dyr * y_ref[rows, :]
                dq = jnp.zeros((B, lb), F32)
                dkc, dkp, dvc, dvp = dq, dq, dq, dq
                for a in range(hps):
                    sel = lane == a
                    qm = jnp.where(sel, q, 0.0).astype(BF16)
                    dym = jnp.where(sel, dyr, 0.0).astype(BF16)
                    dm = jnp.sum(jnp.where(sel, dyy, 0.0), axis=-1, keepdims=True)
                    lse_a = lser[:, a * E:a * E + 1]
                    s_p = _dot(qm, kp, _NT) + b_ref[br, a, :, 0:B]
                    s_c = _dot(qm, kc, _NT) + b_ref[br, a, :, B:2 * B]
                    s_p = jnp.where(first, NEG_INF, s_p)
                    p_p, p_c = jnp.exp(s_p - lse_a), jnp.exp(s_c - lse_a)
                    ds_p = p_p * (_dot(dym, vp, _NT) - dm)
                    ds_c = p_c * (_dot(dym, vc, _NT) - dm)
                    db_ref[br, a, :, 0:B] += ds_p
                    db_ref[br, a, :, B:2 * B] += ds_c
                    dsb_p, dsb_c = ds_p.astype(BF16), ds_c.astype(BF16)
                    dq = dq + jnp.where(sel, _dot(dsb_p, kp, _NN) + _dot(dsb_c, kc, _NN), 0.0)
                    dkp = dkp + _dot(dsb_p, qm, _TN)
                    dkc = dkc + _dot(dsb_c, qm, _TN)
                    dvp = dvp + _dot(p_p.astype(BF16), dym, _TN)
                    dvc = dvc + _dot(p_c.astype(BF16), dym, _TN)
                dq_ref[rows, :] += dq * scale
                dk_ref[prows, :] += dkp
                dk_ref[rows, :] += dkc
                dv_ref[prows, :] += dvp
                dv_ref[rows, :] += dvc
                return carry

            lax.fori_loop(0, nblk, combo, 0)

    col = lambda off: pl.BlockSpec((L, lb), lambda h: (0, off * (GW // lb) + h))
    own = pl.BlockSpec((L, lb), lambda h: (0, h))
    bsp = pl.BlockSpec((len(DILATED_PATTERNS), hps, B, 2 * B), lambda h: (0, h, 0, 0))
    return pl.pallas_call(
        body, grid=(GW // lb,),
        in_specs=[col(4), col(5), col(6), bsp, own, own, own], out_specs=[own, own, own, bsp],
        out_shape=[jax.ShapeDtypeStruct((L, GW), F32)] * 3 + [jax.ShapeDtypeStruct((len(DILATED_PATTERNS), ATT_HEADS, B, 2 * B), F32)],
        compiler_params=_cparams(("parallel",), VMEM_LIMIT), name=name)(proj, proj, proj, bias, y, lse, dy)


def _t5_bucket(dist):
    n = np.maximum(dist, 0)
    max_exact = REL_BUCKETS // 2
    large = max_exact + (np.log(np.maximum(n, 1) / max_exact) / np.log(REL_MAX_DIST / max_exact)
                         * (REL_BUCKETS - max_exact)).astype(np.int64)
    large = np.minimum(large, REL_BUCKETS - 1)
    return np.where(n < max_exact, n, large).astype(np.int32)


def _att_tables():
    a = np.arange(ATT_BLOCK)[:, None]
    b = np.arange(2 * ATT_BLOCK)[None, :]
    sub = a + ATT_BLOCK - b
    buckets, valids = [], []
    for window, dil in DILATED_PATTERNS:
        buckets.append(_t5_bucket(sub * dil))
        valids.append((sub >= 0) & (sub <= window // dil))
    return np.stack(buckets), np.stack(valids)


def _to_branches(t, H):
    L = t.shape[0]
    E = t.shape[1] // H
    out = []
    for _, dil in DILATED_PATTERNS:
        out.append(t.reshape(L // dil, dil, H, E).transpose(2, 1, 0, 3).reshape(H, L, E))
    return jnp.stack(out)


def _from_branches(t):
    _, H, L, E = t.shape
    out = []
    for b, (_, dil) in enumerate(DILATED_PATTERNS):
        out.append(t[b].reshape(H, dil, L // dil, E).transpose(2, 1, 0, 3).reshape(L, H * E))
    return jnp.stack(out)


def _xatt_fwd(q, k, v, *, name):
    L, XW = q.shape
    M = k.shape[0]
    tr = _tile(L, 512)
    scale = X_HEAD_DIM ** -0.5

    def body(q_ref, k_ref, v_ref, o_ref):
        s = _dot(q_ref[...], k_ref[...], _NT) * scale
        p = jnp.exp(s - jnp.max(s, axis=-1, keepdims=True))
        den = jnp.sum(p, axis=-1, keepdims=True)
        o_ref[...] = (_dot(p.astype(BF16), v_ref[...], _NN) / den).astype(o_ref.dtype)

    qs = pl.BlockSpec((tr, X_HEAD_DIM), lambda h, i: (i, h))
    ks = pl.BlockSpec((M, X_HEAD_DIM), lambda h, i: (0, h))
    return pl.pallas_call(
        body, grid=(XW // X_HEAD_DIM, L // tr), in_specs=[qs, ks, ks], out_specs=qs,
        out_shape=jax.ShapeDtypeStruct((L, XW), BF16),
        compiler_params=_cparams(("parallel", "parallel")), name=name)(q, k, v)


def _xatt_bwd(q, k, v, do, *, name):
    L, XW = q.shape
    M = k.shape[0]
    tr = _tile(L, 512)
    scale = X_HEAD_DIM ** -0.5

    def body(q_ref, k_ref, v_ref, do_ref, dq_ref, dk_ref, dv_ref):
        @pl.when(pl.program_id(1) == 0)
        def _():
            dk_ref[...] = jnp.zeros_like(dk_ref)
            dv_ref[...] = jnp.zeros_like(dv_ref)

        qv, kv, vv = q_ref[...], k_ref[...], v_ref[...]
        s = _dot(qv, kv, _NT) * scale
        p = jnp.exp(s - jnp.max(s, axis=-1, keepdims=True))
        p = p / jnp.sum(p, axis=-1, keepdims=True)
        dob = do_ref[...].astype(BF16)
        pb = p.astype(BF16)
        dv_ref[...] += _dot(pb, dob, _TN)
        dp = _dot(dob, vv, _NT)
        ds = (p * (dp - jnp.sum(dp * p, axis=-1, keepdims=True)) * scale).astype(BF16)
        dq_ref[...] = _dot(ds, kv, _NN)
        dk_ref[...] += _dot(ds, qv, _TN)

    qs = pl.BlockSpec((tr, X_HEAD_DIM), lambda h, i: (i, h))
    ks = pl.BlockSpec((M, X_HEAD_DIM), lambda h, i: (0, h))
    return pl.pallas_call(
        body, grid=(XW // X_HEAD_DIM, L // tr), in_specs=[qs, ks, ks, qs], out_specs=[qs, ks, ks],
        out_shape=[jax.ShapeDtypeStruct((L, XW), F32), jax.ShapeDtypeStruct((M, XW), F32), jax.ShapeDtypeStruct((M, XW), F32)],
        compiler_params=_cparams(("parallel", "arbitrary")), name=name)(q, k, v, do)


_ANY = pl.BlockSpec(memory_space=pl.ANY)


def _place():
    mx, my, mc = lax.axis_index("x"), lax.axis_index("y"), lax.axis_index("c")
    chips = [(1 - mx, my), (mx, 1 - my), (1 - mx, 1 - my)]
    return mx, my, mc, chips


def _rcopy(src, dst, ssem, rsem, dev):
    return pltpu.make_async_remote_copy(src_ref=src, dst_ref=dst, send_sem=ssem, recv_sem=rsem, device_id=dev,
                                        device_id_type=MESH)


STAGE_BYTES = 2 * 1024 * 1024


def _staged_copy(pairs, buf, isem, osem):
    n = len(pairs)
    ins = [pltpu.make_async_copy(src, buf.at[i % 2], isem.at[i % 2]) for i, (src, _) in enumerate(pairs)]
    outs = [pltpu.make_async_copy(buf.at[i % 2], dst, osem.at[i % 2]) for i, (_, dst) in enumerate(pairs)]
    ins[0].start()
    for i in range(n):
        if i + 1 < n:
            if i >= 1:
                outs[i - 1].wait()
            ins[i + 1].start()
        ins[i].wait()
        outs[i].start()
    for i in range(max(n - 2, 0), n):
        outs[i].wait()


_HBM = pl.BlockSpec(memory_space=pltpu.HBM)
_SEMS = pl.BlockSpec(memory_space=pltpu.SEMAPHORE)
_VM = pl.BlockSpec(memory_space=pltpu.VMEM)
_DATAFLOW = pltpu.SideEffectType.DATAFLOW_SIDE_EFFECTING


def _ag_copies(x_refs, land_refs, ssem, rsem, inbound, which=None):
    mx, my, mc, chips = _place()
    me = 2 * mx + my
    cps = []
    for i, (x_ref, land_ref) in enumerate(zip(x_refs, land_refs)):
        w = i if which is None else which[i]
        R2 = x_ref.shape[0] // 2
        mine = x_ref.at[pl.ds(mc * R2, R2)]
        for j, (px, py) in enumerate(chips):
            dst = land_ref.at[2 * px + py, mc] if inbound else land_ref.at[me, mc]
            cps.append(_rcopy(mine, dst, ssem.at[3 * w + j], rsem.at[3 * w + j], (px, py, mc)))
    return cps


def _ag_start(xs, token, *, name):
    n = len(xs)
    lands = [pltpu.with_memory_space_constraint(lax.empty((N_CHIPS, 2, x.shape[0] // 2, x.shape[1]), x.dtype), pltpu.HBM) for x in xs]
    xs = [pltpu.with_memory_space_constraint(x, pltpu.HBM) for x in xs]

    def body(*refs):
        x_refs, land_refs, tok_ref = refs[:n], refs[n:2 * n], refs[2 * n]
        ssem, rsem, tok_out = refs[2 * n + 1], refs[2 * n + 2], refs[-1]
        for cp in _ag_copies(x_refs, land_refs, ssem, rsem, False):
            cp.start()
        tok_out[...] = tok_ref[...]

    outs = pl.pallas_call(
        body, name=name,
        out_shape=(pltpu.SemaphoreType.DMA((3 * n,)), pltpu.SemaphoreType.DMA((3 * n,)),
                   *[pltpu.HBM(x.shape, x.dtype) for x in xs], *[pltpu.HBM(t.shape, t.dtype) for t in lands],
                   jax.ShapeDtypeStruct(token.shape, token.dtype)),
        in_specs=[_HBM] * (2 * n) + [_VM], out_specs=(_SEMS, _SEMS, *[_HBM] * (2 * n), _VM),
        input_output_aliases={i: 2 + i for i in range(2 * n)},
        compiler_params=pltpu.CompilerParams(has_side_effects=_DATAFLOW),
    )(*xs, *lands, token)
    return outs[0], outs[1], list(outs[2:2 + n]), list(outs[2 + n:2 + 2 * n]), outs[-1]


def _ag_wait(ssem, rsem, xs, lands, after, which, *, name):
    n = len(xs)

    def body(*refs):
        x_refs, land_refs, ssem_ref, rsem_ref = refs[:n], refs[n:2 * n], refs[2 * n], refs[2 * n + 1]
        for cp in _ag_copies(x_refs, land_refs, ssem_ref, rsem_ref, True, which):
            cp.wait_send()
            cp.wait_recv()

    after = list(after) if isinstance(after, (list, tuple)) else [after]
    outs = pl.pallas_call(
        body, name=name,
        out_shape=(*[pltpu.HBM(x.shape, x.dtype) for x in xs], *[pltpu.HBM(t.shape, t.dtype) for t in lands]),
        in_specs=[_HBM] * (2 * n) + [_SEMS, _SEMS] + [_ANY] * len(after), out_specs=tuple([_HBM] * (2 * n)),
        input_output_aliases={i: i for i in range(2 * n)},
        compiler_params=pltpu.CompilerParams(has_side_effects=_DATAFLOW),
    )(*xs, *lands, ssem, rsem, *after)
    return list(outs[:n]), list(outs[n:])


def _ag_finish(lands, xs, *, name):
    n = len(xs)

    def body(*refs):
        x_refs, o_refs, (ssem, rsem) = refs[n:2 * n], refs[2 * n:3 * n], refs[3 * n:]
        mx, my, mc, chips = _place()
        me = 2 * mx + my
        sib = (mx, my, 1 - mc)
        passed = []
        for w in range(n):
            for j, (px, py) in enumerate(chips):
                got = o_refs[w].at[2 * px + py, mc]
                passed.append(_rcopy(got, got, ssem.at[3 * w + j], rsem.at[3 * w + j], sib))
        for cp in passed:
            cp.start()
        for w in range(n):
            R, C = x_refs[w].shape
            R2 = R // 2
            ch = _rows_tile(R2, max(8, STAGE_BYTES // (C * x_refs[w].dtype.itemsize)))
            pairs = [(x_refs[w].at[pl.ds(h * R2 + r, ch)], o_refs[w].at[me, h, pl.ds(r, ch)]) for h in range(2) for r in range(0, R2, ch)]
            pl.run_scoped(functools.partial(_staged_copy, pairs), pltpu.VMEM((2, ch, C), x_refs[w].dtype),
                          pltpu.SemaphoreType.DMA((2,)), pltpu.SemaphoreType.DMA((2,)))
        for w in range(n):
            for j, (px, py) in enumerate(chips):
                theirs = o_refs[w].at[2 * px + py, 1 - mc]
                _rcopy(theirs, theirs, ssem.at[3 * w + j], rsem.at[3 * w + j], sib).wait_recv()
        for cp in passed:
            cp.wait_send()

    return pl.pallas_call(
        body, in_specs=[_ANY] * (2 * n), out_specs=[_ANY] * n,
        out_shape=[jax.ShapeDtypeStruct(t.shape, t.dtype) for t in lands],
        input_output_aliases={i: i for i in range(n)},
        scratch_shapes=[pltpu.SemaphoreType.DMA((3 * n,)), pltpu.SemaphoreType.DMA((3 * n,))],
        name=name)(*lands, *xs)


def _ag4(x, *, name):
    def body(x_ref, o_ref, ssem, rsem, lsem):
        mx, my, mc, chips = _place()
        me = 2 * mx + my
        loc = pltpu.make_async_copy(x_ref, o_ref.at[me], lsem)
        loc.start()
        sends = [_rcopy(x_ref, o_ref.at[me], ssem.at[j], rsem.at[j], (px, py, mc)) for j, (px, py) in enumerate(chips)]
        for cp in sends:
            cp.start()
        for j, (px, py) in enumerate(chips):
            _rcopy(x_ref, o_ref.at[2 * px + py], ssem.at[j], rsem.at[j], (px, py, mc)).wait_recv()
        for cp in sends:
            cp.wait_send()
        loc.wait()

    return pl.pallas_call(
        body, in_specs=[_ANY], out_specs=_ANY, out_shape=jax.ShapeDtypeStruct((N_CHIPS,) + x.shape, x.dtype),
        scratch_shapes=[pltpu.SemaphoreType.DMA((3,)), pltpu.SemaphoreType.DMA((3,)), pltpu.SemaphoreType.DMA(())],
        name=name)(x)


def _rs_sib(gs, after=(), *, name):
    n, na = len(gs), len(after)

    def body(*refs):
        g_refs, t_refs, (ssem, rsem) = refs[:n], refs[n + na:2 * n + na], refs[2 * n + na:]
        mx, my, mc, _ = _place()
        sib = (mx, my, 1 - mc)
        sends = []
        for w in range(n):
            R2 = g_refs[w].shape[1] // 2
            for k in range(N_CHIPS):
                sends.append(_rcopy(g_refs[w].at[k, pl.ds((1 - mc) * R2, R2)], t_refs[w].at[k], ssem.at[N_CHIPS * w + k],
                                    rsem.at[N_CHIPS * w + k], sib))
        for cp in sends:
            cp.start()
        for w in range(n):
            for k in range(N_CHIPS):
                t = t_refs[w].at[k]
                _rcopy(t, t, ssem.at[N_CHIPS * w + k], rsem.at[N_CHIPS * w + k], sib).wait_recv()
        for cp in sends:
            cp.wait_send()

    return pl.pallas_call(
        body, in_specs=[_ANY] * (n + na), out_specs=[_ANY] * n,
        out_shape=[jax.ShapeDtypeStruct((N_CHIPS, g.shape[1] // 2, g.shape[2]), g.dtype) for g in gs],
        scratch_shapes=[pltpu.SemaphoreType.DMA((N_CHIPS * n,)), pltpu.SemaphoreType.DMA((N_CHIPS * n,))],
        name=name)(*gs, *after)


def _a2a_copies(h_refs, got_refs, ssem, rsem, inbound):
    mx, my, mc, chips = _place()
    me = 2 * mx + my
    cps = []
    for w, (h_ref, got_ref) in enumerate(zip(h_refs, got_refs)):
        for j, (px, py) in enumerate(chips):
            dst = got_ref.at[2 * px + py] if inbound else got_ref.at[me]
            cps.append(_rcopy(h_ref.at[2 * px + py], dst, ssem.at[3 * w + j], rsem.at[3 * w + j], (px, py, mc)))
    return cps


def _a2a_start(hs, *, name):
    n = len(hs)
    gots = [pltpu.with_memory_space_constraint(lax.empty(h.shape, h.dtype), pltpu.HBM) for h in hs]
    hs = [pltpu.with_memory_space_constraint(h, pltpu.HBM) for h in hs]

    def body(*refs):
        h_refs, got_refs = refs[:n], refs[n:2 * n]
        ssem, rsem, tok_out = refs[2 * n], refs[2 * n + 1], refs[-1]
        for cp in _a2a_copies(h_refs, got_refs, ssem, rsem, False):
            cp.start()
        tok_out[...] = jnp.zeros_like(tok_out)

    outs = pl.pallas_call(
        body, name=name,
        out_shape=(pltpu.SemaphoreType.DMA((3 * n,)), pltpu.SemaphoreType.DMA((3 * n,)),
                   *[pltpu.HBM(h.shape, h.dtype) for h in hs], *[pltpu.HBM(h.shape, h.dtype) for h in hs],
                   jax.ShapeDtypeStruct((8, 128), F32)),
        in_specs=[_HBM] * (2 * n), out_specs=(_SEMS, _SEMS, *[_HBM] * (2 * n), _VM),
        input_output_aliases={i: 2 + i for i in range(2 * n)},
        compiler_params=pltpu.CompilerParams(has_side_effects=_DATAFLOW),
    )(*hs, *gots)
    return outs[0], outs[1], list(outs[2:2 + n]), list(outs[2 + n:2 + 2 * n]), outs[-1]


def _a2a_wait(ssem, rsem, hs, gots, after, *, name):
    n = len(hs)

    def body(*refs):
        h_refs, got_refs, ssem_ref, rsem_ref = refs[:n], refs[n:2 * n], refs[2 * n], refs[2 * n + 1]
        for cp in _a2a_copies(h_refs, got_refs, ssem_ref, rsem_ref, True):
            cp.wait_send()
            cp.wait_recv()

    after = list(after) if isinstance(after, (list, tuple)) else [after]
    outs = pl.pallas_call(
        body, name=name,
        out_shape=tuple(pltpu.HBM(h.shape, h.dtype) for h in hs + gots),
        in_specs=[_HBM] * (2 * n) + [_SEMS, _SEMS] + [_ANY] * len(after), out_specs=tuple([_HBM] * (2 * n)),
        input_output_aliases={i: i for i in range(2 * n)},
        compiler_params=pltpu.CompilerParams(has_side_effects=_DATAFLOW),
    )(*hs, *gots, ssem, rsem, *after)
    return list(outs[:n]), list(outs[n:])


def _sib_fill(bufs, after=(), *, name):
    n, na = len(bufs), len(after)

    def body(*refs):
        o_refs, (ssem, rsem) = refs[n + na:2 * n + na], refs[2 * n + na:]
        mx, my, mc, _ = _place()
        sib = (mx, my, 1 - mc)
        sends = [_rcopy(o_refs[w].at[mc], o_refs[w].at[mc], ssem.at[w], rsem.at[w], sib) for w in range(n)]
        for cp in sends:
            cp.start()
        for w in range(n):
            t = o_refs[w].at[1 - mc]
            _rcopy(t, t, ssem.at[w], rsem.at[w], sib).wait_recv()
        for cp in sends:
            cp.wait_send()

    return pl.pallas_call(
        body, in_specs=[_ANY] * (n + na), out_specs=[_ANY] * n, out_shape=[jax.ShapeDtypeStruct(b.shape, b.dtype) for b in bufs],
        input_output_aliases={i: i for i in range(n)},
        scratch_shapes=[pltpu.SemaphoreType.DMA((n,)), pltpu.SemaphoreType.DMA((n,))], name=name)(*bufs, *after)


def _rows_tile(n, pref=512):
    t = min(n, pref)
    while n % t or (t % 8 and t != n):
        t -= 1
    return t


def _rs_add(g, theirs, c_arr, payload, *, name):
    _, R, C = g.shape
    R2 = R // 2
    tr = _rows_tile(R2, 256)
    nb = R2 // tr

    def body(c_ref, g_ref, t_ref, o_ref):
        o_ref[...] = (g_ref[...].astype(F32) + t_ref[...].astype(F32)).astype(o_ref.dtype)

    blk = pl.BlockSpec((None, tr, C), lambda k, i, c: (k, i, 0))
    return pl.pallas_call(
        body,
        grid_spec=pltpu.PrefetchScalarGridSpec(
            num_scalar_prefetch=1, grid=(N_CHIPS, nb),
            in_specs=[pl.BlockSpec((None, tr, C), lambda k, i, c: (k, c[0] * nb + i, 0)), blk], out_specs=blk),
        out_shape=jax.ShapeDtypeStruct((N_CHIPS, R2, C), payload),
        compiler_params=_cparams(("arbitrary", "arbitrary")), name=name)(c_arr, g, theirs)


def _rs_sum(chip_sum, got, mc_arr, *, name):
    _, R2, C = chip_sum.shape
    tr = _rows_tile(R2, 256)

    def body(s_ref, own_ref, g0, g1, g2, g3, o_ref):
        me = s_ref[0]
        acc = jnp.zeros((tr, C), F32)
        for k, g_ref in enumerate((g0, g1, g2, g3)):
            acc = acc + jnp.where(me == k, own_ref[...], g_ref[...]).astype(F32)
        o_ref[...] = acc

    def got_spec(k):
        return pl.BlockSpec((None, tr, C), lambda i, s: (jnp.where(s[0] == k, (k + 1) % N_CHIPS, k), i, 0))

    return pl.pallas_call(
        body,
        grid_spec=pltpu.PrefetchScalarGridSpec(
            num_scalar_prefetch=1, grid=(R2 // tr,),
            in_specs=[pl.BlockSpec((None, tr, C), lambda i, s: (s[0], i, 0))] + [got_spec(k) for k in range(N_CHIPS)],
            out_specs=pl.BlockSpec((None, tr, C), lambda i, s: (s[1], i, 0))),
        out_shape=jax.ShapeDtypeStruct((2, R2, C), F32),
        compiler_params=_cparams(("arbitrary",)), name=name)(mc_arr, chip_sum, got, got, got, got)


def _adamw(w, m, v, gs, *, name, first=0, prev=None):
    _, _, R2, C = w.shape
    nl = len(gs)
    tr = _rows_tile(R2, 256)
    c1 = 1.0 / (1.0 - ADAM_B1 ** ADAM_STEP)
    c2 = 1.0 / (1.0 - ADAM_B2 ** ADAM_STEP)

    def body(w_ref, m_ref, v_ref, *refs):
        g_refs, (go_ref, d_ref, mo_ref, vo_ref) = refs[:nl], refs[-4:]
        l = pl.program_id(0)
        for ll in range(nl):
            @pl.when(l == ll)
            def _(ll=ll):
                gv = g_refs[ll][...]
                mn = ADAM_B1 * m_ref[...] + (1.0 - ADAM_B1) * gv
                vn = ADAM_B2 * v_ref[...] + (1.0 - ADAM_B2) * (gv * gv)
                go_ref[...] = gv
                mo_ref[...] = mn
                vo_ref[...] = vn
                d_ref[...] = -ADAM_LR * ((mn * c1) / (jnp.sqrt(vn * c2) + ADAM_EPS) + ADAM_WD * w_ref[...])

    def g_spec(ll):
        return pl.BlockSpec((None, tr, C), lambda l, h, i: (jnp.where(l == ll, h, 0), jnp.where(l == ll, i, 0), 0))

    ws = pl.BlockSpec((None, None, tr, C), lambda l, h, i: (l + first, h, i, 0))
    shp = jax.ShapeDtypeStruct(w.shape, F32)
    prev = list(prev) if prev is not None else []
    return pl.pallas_call(
        body, grid=(nl, 2, R2 // tr), in_specs=[ws, ws, ws] + [g_spec(ll) for ll in range(nl)] + [_ANY] * len(prev),
        out_specs=[ws] * 4, out_shape=[shp] * 4,
        input_output_aliases={3 + nl + k: k for k in range(len(prev))},
        compiler_params=_cparams(("arbitrary", "arbitrary", "arbitrary")), name=name)(w, m, v, *gs, *prev)


class _GradReducer:
    def __init__(self, tag, payload):
        self.tag, self.payload = tag, payload
        self.me = (2 * lax.axis_index("x") + lax.axis_index("y")).astype(jnp.int32)
        self.c = lax.axis_index("c").astype(jnp.int32)

    def start(self, names, gs, after=()):
        theirs = _rs_sib(gs, after, name=f"rs_sib_{self.tag}")
        hs = [_rs_add(g, t, self.c.reshape(1), self.payload, name=f"rs_add_{n}") for n, g, t in zip(names, gs, theirs)]
        self.names = names
        self.ssem, self.rsem, self.hs, self.gots, token = _a2a_start(hs, name=f"rs_a2a_start_{self.tag}")
        return token

    def finish(self, after):
        hs, gots = _a2a_wait(self.ssem, self.rsem, self.hs, self.gots, after, name=f"rs_a2a_wait_{self.tag}")
        mc = jnp.stack([self.me, self.c])
        return {n: _rs_sum(h, g, mc, name=f"rs_sum_{n}") for n, h, g in zip(self.names, hs, gots)}


WEIGHTS = ["rel_bias", "mem_norm_g", "norm_mix_g", "w_in", "s5_lam_re", "s5_lam_im", "s5_log_dt", "s5_b_re", "s5_b_im",
           "s5_c_re", "s5_c_im", "s5_d", "s5_w_glu", "pool_w", "pool_scale", "conv_w_dw", "conv_b_dw", "conv_ln_g",
           "conv_ln_b", "conv_w_pw", "grp_norm_g", "w_out", "norm_x_g", "w_xq", "w_xk", "w_xv", "w_xo", "norm_mlp_g",
           "w_up", "w_down", "norm_final_g"]
SHARDED = {"w_in": "col", "s5_w_glu": "row", "conv_w_dw": "col", "conv_w_pw": "row", "w_out": "row", "w_xq": "row",
           "w_xk": "row", "w_xv": "row", "w_xo": "col", "w_up": "col", "w_down": "row"}
AG_FIRST = ("w_in", "s5_w_glu", "conv_w_dw", "conv_w_pw")
SMALL = [n for n in WEIGHTS if n not in SHARDED]
SMALL_ALIGN = N_CHIPS * 2 * 256 * 128


def _mat(w, kind):
    return w.reshape(w.shape[0] * w.shape[1], w.shape[2]) if kind == "row" else w


def _att_bias(rel_bias):
    bucket, valid = _att_tables()
    onehot = (jnp.asarray(bucket.reshape(-1))[:, None] == jnp.arange(REL_BUCKETS)[None, :]).astype(F32)
    b = jnp.dot(onehot, rel_bias, precision=lax.Precision.HIGHEST).reshape(bucket.shape + (rel_bias.shape[1],))
    return jnp.where(jnp.asarray(valid)[:, None], jnp.transpose(b, (0, 3, 1, 2)), NEG_INF)


def _rel_bias_grad(dbias):
    bucket, _ = _att_tables()
    nb, H = dbias.shape[0], dbias.shape[1]
    onehot = (jnp.asarray(bucket.reshape(-1))[:, None] == jnp.arange(REL_BUCKETS)[None, :]).astype(BF16)
    flat = jnp.transpose(dbias.reshape(nb, H, -1), (1, 0, 2)).reshape(H, -1)
    return _mm(flat, onehot, "nn", name="rel_bias_grad").T


def _layer_fwd(h, mem_n, bias, sp, bw, l):
    L, D = h.shape
    GW = D // N_MIXERS
    G = GW // S5_CH
    E = GW // ATT_HEADS
    sv = {"h": h}
    xn = _rms_fwd(h, sp["norm_mix_g"][l][None], name="norm_mix")
    proj = _mm(xn, bw["w_in"], "nn", name="proj_in")
    sv.update(xn=xn, proj=proj)
    disc, disc_vjp = jax.vjp(_s5_disc, sp["s5_lam_re"][l], sp["s5_lam_im"][l], sp["s5_log_dt"][l], sp["s5_b_re"][l], sp["s5_b_im"][l])
    ab_r, ab_i, bb_r, bb_i = disc
    s5 = dict(
        bdr=_bd(jnp.transpose(bb_r, (0, 2, 1))).astype(BF16), bdi=_bd(jnp.transpose(bb_i, (0, 2, 1))).astype(BF16),
        cdr=_bd(jnp.transpose(sp["s5_c_re"][l], (0, 2, 1))).astype(BF16), cdi=_bd(jnp.transpose(sp["s5_c_im"][l], (0, 2, 1))).astype(BF16),
        d=sp["s5_d"][l][None], wglu=bw["s5_w_glu"], ab=(ab_r.reshape(-1), ab_i.reshape(-1)), vjp=disc_vjp)
    pw, tr, ti = _cpow_tables(*s5["ab"])
    y_a, xr, xi = _s5_fwd(proj, s5["bdr"], s5["bdi"], pw, tr, ti, s5["cdr"], s5["cdi"], s5["d"], s5["wglu"], name="s5_fwd")
    sv.update(s5=s5, xr=xr, xi=xi, y_a=y_a)
    pool_s = sp["pool_scale"][l].reshape(len(POOL_WINDOWS), 1, -1)
    y_b = _pool_fwd(proj, sp["pool_w"][l], pool_s, name="pool_fwd")
    sv.update(y_b=y_b, pool_s=pool_s)
    hc = _conv1_fwd(proj, bw["conv_w_dw"], sp["conv_b_dw"][l][None], name="conv_dw_fwd")
    y_c = _conv2_fwd(hc, sp["conv_ln_g"][l][None], sp["conv_ln_b"][l][None], bw["conv_w_pw"], name="conv_pw_fwd")
    sv.update(hc=hc, y_c=y_c)
    y_d, lse_d = _datt_fwd(proj, bias, name="att_fwd")
    sv.update(y_d=y_d, lse_d=lse_d)
    yn = _grp_fwd([y_a, y_b, y_c, y_d], sp["grp_norm_g"][l][None], name="grp_fwd")
    bw.rest(yn)
    h1 = _mm(yn, bw["w_out"], "nn", res=h, name="proj_out")
    sv.update(yn=yn, h1=h1)
    hx = _rms_fwd(h1, sp["norm_x_g"][l][None], name="norm_x")
    q_x = _mm(hx, bw["w_xq"], "nn", out_dtype=BF16, name="xq")
    k_x = _mm(mem_n, bw["w_xk"], "nn", out_dtype=BF16, name="xk")
    v_x = _mm(mem_n, bw["w_xv"], "nn", out_dtype=BF16, name="xv")
    o_x = _xatt_fwd(q_x, k_x, v_x, name="xatt_fwd")
    h2 = _mm(o_x, bw["w_xo"], "nn", res=h1, name="xo")
    sv.update(hx=hx, q_x=q_x, k_x=k_x, v_x=v_x, o_x=o_x, h2=h2)
    hm = _rms_fwd(h2, sp["norm_mlp_g"][l][None], name="norm_mlp")
    up, act = _mm(hm, bw["w_up"], "nn", epi="relu2", out_dtype=BF16, name="mlp_up")
    h3 = _mm(act, bw["w_down"], "nn", res=h2, name="mlp_down")
    sv.update(hm=hm, up=up, act=act)
    return h3, sv


def _stack_rows(g):
    return g.reshape(N_CHIPS, g.shape[0] // N_CHIPS, g.shape[1])


def _layer_bwd(dh3, d_memn, mem_n, bias, sp, bw, sv, l):
    L, D = dh3.shape
    GW = D // N_MIXERS
    G = GW // S5_CH
    E = GW // ATT_HEADS
    gs, gb = {}, {}
    d_up = _mm(dh3, bw["w_down"], "nt", epi="relu2_bwd", aux=sv["up"], out_dtype=BF16, name="mlp_down_dx")
    gb["w_down"] = _stack_rows(_mm(sv["act"], dh3, "tn", out_dtype=BF16, name="mlp_down_dw"))
    gb["w_up"] = _mm(sv["hm"], d_up, "tn", out_dtype=BF16, out_stack=N_CHIPS, name="mlp_up_dw")
    d_hm = _mm(d_up, bw["w_up"], "nt", name="mlp_up_dx")
    dh2, gs["norm_mlp_g"] = _rms_bwd(sv["h2"], sp["norm_mlp_g"][l][None], d_hm, dh3, name="norm_mlp_bwd")
    d_ox = _mm(dh2, bw["w_xo"], "nt", name="xo_dx")
    gb["w_xo"] = _mm(sv["o_x"], dh2, "tn", out_dtype=BF16, out_stack=N_CHIPS, name="xo_dw")
    dq_x, dk_x, dv_x = _xatt_bwd(sv["q_x"], sv["k_x"], sv["v_x"], d_ox, name="xatt_bwd")
    gb["w_xq"] = _stack_rows(_mm(sv["hx"], dq_x, "tn", out_dtype=BF16, name="xq_dw"))
    gb["w_xk"] = _stack_rows(_mm(mem_n, dk_x, "tn", out_dtype=BF16, name="xk_dw"))
    gb["w_xv"] = _stack_rows(_mm(mem_n, dv_x, "tn", out_dtype=BF16, name="xv_dw"))
    d_memn = _mm(dk_x, bw["w_xk"], "nt", res=d_memn, name="xk_dx")
    d_memn = _mm(dv_x, bw["w_xv"], "nt", res=d_memn, name="xv_dx")
    d_hx = _mm(dq_x, bw["w_xq"], "nt", name="xq_dx")
    dh1, gs["norm_x_g"] = _rms_bwd(sv["h1"], sp["norm_x_g"][l][None], d_hx, dh2, name="norm_x_bwd")
    d_yn = _mm(dh1, bw["w_out"], "nt", name="proj_out_dx")
    gb["w_out"] = _stack_rows(_mm(sv["yn"], dh1, "tn", out_dtype=BF16, name="proj_out_dw"))
    ys = [sv["y_a"], sv["y_b"], sv["y_c"], sv["y_d"]]
    dya, dyb, dyc, dyd, gs["grp_norm_g"] = _grp_bwd(ys, sp["grp_norm_g"][l][None], d_yn, name="grp_bwd")
    dq, dk, dv, dbias = _datt_bwd(sv["proj"], bias, sv["y_d"], sv["lse_d"], dyd, name="att_bwd")
    dhc, gs["conv_ln_g"], gs["conv_ln_b"], g_pw = _conv2_bwd(sv["hc"], sp["conv_ln_g"][l][None], sp["conv_ln_b"][l][None],
                                                               bw["conv_w_pw"], dyc, name="conv_pw_bwd")
    gb["conv_w_pw"] = _stack_rows(g_pw)
    dval, dgate, g_dw, gs["conv_b_dw"] = _conv1_bwd(sv["proj"], dhc, bw["conv_w_dw"], name="conv_dw_bwd")
    gb["conv_w_dw"] = jnp.transpose(g_dw.reshape(CONV_PAD, N_CHIPS, GW // N_CHIPS), (1, 0, 2))
    du_b, gs["pool_w"], g_ps = _pool_bwd(sv["proj"], dyb, sp["pool_w"][l], sv["pool_s"], name="pool_bwd")
    gs["pool_scale"] = g_ps.reshape(-1)
    s5 = sv["s5"]
    conj = (s5["ab"][0], -s5["ab"][1])
    pw, tr, ti = _cpow_tables(*conj)
    du_a, g_glu, g_d, dcdr, dcdi, dbdr, dbdi, dar, dai = _s5_bwd(
        sv["proj"], sv["xr"], sv["xi"], dya, s5["bdr"], s5["bdi"], pw, tr[::-1], ti[::-1], s5["cdr"], s5["cdi"], s5["d"], s5["wglu"],
        name="s5_bwd")
    gb["s5_w_glu"] = _stack_rows(g_glu)
    gs["s5_d"] = g_d.reshape(-1)
    gs["s5_c_re"] = jnp.transpose(_bd_extract(dcdr, G), (0, 2, 1))
    gs["s5_c_im"] = jnp.transpose(_bd_extract(dcdi, G), (0, 2, 1))
    d_bb_r = jnp.transpose(_bd_extract(dbdr, G), (0, 2, 1))
    d_bb_i = jnp.transpose(_bd_extract(dbdi, G), (0, 2, 1))
    d_ab_r = jnp.sum(dar, axis=0).reshape(G, S5_STATE)
    d_ab_i = jnp.sum(dai, axis=0).reshape(G, S5_STATE)
    gs["s5_lam_re"], gs["s5_lam_im"], gs["s5_log_dt"], gs["s5_b_re"], gs["s5_b_im"] = s5["vjp"]((d_ab_r, d_ab_i, d_bb_r, d_bb_i))
    d_proj = jnp.concatenate([du_a, du_b, dval, dgate, dq, dk, dv], axis=1)
    gb["w_in"] = _mm(sv["xn"], d_proj, "tn", out_dtype=BF16, out_stack=N_CHIPS, name="proj_in_dw")
    d_xn = _mm(d_proj, bw["w_in"], "nt", name="proj_in_dx")
    dh0, gs["norm_mix_g"] = _rms_bwd(sv["h"], sp["norm_mix_g"][l][None], d_xn, dh1, name="norm_mix_bwd")
    gs = {n: g.reshape(sp[n].shape[1:]) for n, g in gs.items()}
    return dh0, d_memn, dbias, gs, gb


def _device_step(x, mem, target, sp, get_big, on_grads):
    nl = sp["norm_mix_g"].shape[0]
    mem_n = _rms_fwd(mem, sp["mem_norm_g"][None], name="norm_mem")
    bias = _att_bias(sp["rel_bias"])
    h, saved, big = x, [], {n: [] for n in SHARDED}
    for l in range(nl):
        bw = get_big(l, h)
        h, sv = _layer_fwd(h, mem_n, bias, sp, bw, l)
        saved.append(sv)
        for n in SHARDED:
            big[n].append(bw[n])
    loss, dh, g_final = _loss_head(h, sp["norm_final_g"][None], target, name="loss_head")
    d_memn = jnp.zeros(mem.shape, F32)
    dbias = jnp.zeros(bias.shape, F32)
    sg = {n: [None] * nl for n in SMALL}
    for l in reversed(range(nl)):
        dh, d_memn, dbias_l, gs, gb = _layer_bwd(dh, d_memn, mem_n, bias, sp, {n: big[n][l] for n in SHARDED}, saved[l], l)
        dbias = dbias + dbias_l
        for n, g in gs.items():
            sg[n][l] = g
        dh = on_grads(l, gb, dh)
    small = {n: jnp.stack(g) for n, g in sg.items() if g[0] is not None}
    small["norm_final_g"] = g_final.reshape(-1)
    _, g_mem = _rms_bwd(mem, sp["mem_norm_g"][None], d_memn, None, name="norm_mem_bwd")
    small["mem_norm_g"] = g_mem.reshape(-1)
    small["rel_bias"] = _rel_bias_grad(dbias)
    return loss, dh, small


def _gather_big(shards, prep):
    nl = shards["w_in"].shape[0]
    token = jnp.zeros((8, 128), F32)
    pending = []
    for l in range(nl):
        xs = [jnp.pad(shards[n][l], ((0, CONV_PAD - CONV_WIDTH), (0, 0))) if n == "conv_w_dw" else shards[n][l].astype(BF16)
              for n in SHARDED]
        ssem, rsem, xs, lands, token = _ag_start(xs, token, name=f"ag_start_l{l}")
        pending.append((ssem, rsem, xs, lands))

    names = list(SHARDED)

    class LayerWeights(dict):
        def __init__(self, l, after):
            super().__init__()
            self.l, self.state = l, pending[l]
            first = [i for i, n in enumerate(names) if n in AG_FIRST] if l == 0 else list(range(len(names)))
            self.left = [i for i in range(len(names)) if i not in first]
            self._complete(first, [token] + list(prep) if l == 0 else [after], "a")

        def _complete(self, which, after, tag):
            ssem, rsem, xs, lands = self.state
            xs, lands = _ag_wait(ssem, rsem, [xs[i] for i in which], [lands[i] for i in which], after, which,
                                 name=f"ag_wait_l{self.l}{tag}")
            fulls = _ag_finish(lands, xs, name=f"ag_finish_{tag if self.l == 0 else 'all'}")
            for i, x, full in zip(which, xs, fulls):
                n = names[i]
                full = full.reshape(N_CHIPS, x.shape[0], x.shape[1])
                self[n] = jnp.transpose(full, (1, 0, 2)).reshape(CONV_PAD, -1) if n == "conv_w_dw" else _mat(full, SHARDED[n])

        def rest(self, after):
            if self.left:
                self._complete(self.left, [after], "b")
                self.left = []

    return LayerWeights


def _pack_small(vals, extra=None):
    flat = [vals[n].reshape(-1).astype(F32) for n in SMALL]
    flat.append(jnp.zeros((1,), F32) if extra is None else extra.reshape(1))
    v = jnp.concatenate(flat)
    n_pad = -(-v.shape[0] // SMALL_ALIGN) * SMALL_ALIGN
    return jnp.pad(v, (0, n_pad - v.shape[0]))


def _unpack_small(flat, like):
    out, pos = {}, 0
    for n in SMALL:
        size = math.prod(like[n].shape)
        out[n] = flat[pos:pos + size].reshape(like[n].shape)
        pos += size
    return out, flat[pos]


def kernel(x, mem, rel_bias, mem_norm_g, norm_mix_g, w_in, s5_lam_re, s5_lam_im, s5_log_dt, s5_b_re, s5_b_im, s5_c_re, s5_c_im, s5_d, s5_w_glu, pool_w, pool_scale, conv_w_dw, conv_b_dw, conv_ln_g, conv_ln_b, conv_w_pw, grp_norm_g, w_out, norm_x_g, w_xq, w_xk, w_xv, w_xo, norm_mlp_g, w_up, w_down, norm_final_g, loss_target, m_rel_bias, m_mem_norm_g, m_norm_mix_g, m_w_in, m_s5_lam_re, m_s5_lam_im, m_s5_log_dt, m_s5_b_re, m_s5_b_im, m_s5_c_re, m_s5_c_im, m_s5_d, m_s5_w_glu, m_pool_w, m_pool_scale, m_conv_w_dw, m_conv_b_dw, m_conv_ln_g, m_conv_ln_b, m_conv_w_pw, m_grp_norm_g, m_w_out, m_norm_x_g, m_w_xq, m_w_xk, m_w_xv, m_w_xo, m_norm_mlp_g, m_w_up, m_w_down, m_norm_final_g, v_rel_bias, v_mem_norm_g, v_norm_mix_g, v_w_in, v_s5_lam_re, v_s5_lam_im, v_s5_log_dt, v_s5_b_re, v_s5_b_im, v_s5_c_re, v_s5_c_im, v_s5_d, v_s5_w_glu, v_pool_w, v_pool_scale, v_conv_w_dw, v_conv_b_dw, v_conv_ln_g, v_conv_ln_b, v_conv_w_pw, v_grp_norm_g, v_w_out, v_norm_x_g, v_w_xq, v_w_xk, v_w_xv, v_w_xo, v_norm_mlp_g, v_w_up, v_w_down, v_norm_final_g):
    env = dict(locals())
    w = {n: env[n] for n in WEIGHTS}
    m = {n: env["m_" + n] for n in WEIGHTS}
    v = {n: env["v_" + n] for n in WEIGHTS}
    sp = {n: w[n] for n in SMALL}
    small_wmv = [_pack_small(t).reshape(1, 2, -1, 128) for t in (w, m, v)]
    get_big = _gather_big({n: w[n] for n in SHARDED}, small_wmv)
    nl = w["w_in"].shape[0]
    names = list(SHARDED)
    reducers, reduced, tokens = {}, {}, {}

    def on_grads(l, gb, dh):
        reducers[l] = _GradReducer(f"l{l}", BF16)
        token = tokens[l] = reducers[l].start(names, [gb[n] for n in names])
        if l + 1 in reducers:
            reduced[l + 1] = reducers[l + 1].finish(dh)
        return dh + token[0, 0]

    loss, grad_x, g_small = _device_step(x[0], mem[0], loss_target[0], sp, get_big, on_grads)

    def shard_views(n):
        pad = ((0, 0), (0, CONV_PAD - CONV_WIDTH), (0, 0)) if n == "conv_w_dw" else None
        wmv = [jnp.pad(t[n], pad) if pad else t[n] for t in (w, m, v)]
        _, R, C = wmv[0].shape
        return [t.reshape(nl, 2, R // 2, C) for t in wmv]

    grad, delta, new_m, new_v = {}, {}, {}, {}
    busy, upper = [grad_x], {}
    if nl > 1:
        filled = _sib_fill([reduced[l][n] for n in names for l in range(1, nl)], [tokens[0]], name="rs_fill_upper")
        for i, n in enumerate(names):
            upper[n] = _adamw(*shard_views(n), filled[i * (nl - 1):(i + 1) * (nl - 1)], first=1, name=f"adamw_upper_{n}")
            busy.append(upper[n][0])
    reduced[0] = reducers[0].finish(busy)
    packed = _pack_small(g_small, loss).reshape(N_CHIPS, -1, 128)
    small = _GradReducer("small", F32)
    token = small.start(["small"], [packed], after=[reduced[0][names[0]]])
    quarter = _sib_fill([small.finish(token)["small"]], name="rs_fill_small")[0]
    total = _ag4(quarter.reshape(-1, 128), name="ag_small").reshape(2, -1, 128)
    outs = _adamw(*small_wmv, [total], name="adamw_small")
    filled = _sib_fill([reduced[0][n] for n in names], name="rs_fill_first")
    for i, n in enumerate(names):
        res = _adamw(*shard_views(n), [filled[i]], prev=upper.get(n), name=f"adamw_first_{n}")
        R = w[n].shape[1]
        grad[n], delta[n], new_m[n], new_v[n] = [o.reshape(nl, -1, o.shape[-1])[:, :R] for o in res]
    loss_sum = None
    for o, dst in zip(outs, (grad, delta, new_m, new_v)):
        vals, last = _unpack_small(o.reshape(-1), sp)
        dst.update(vals)
        loss_sum = last if loss_sum is None else loss_sum
    return (loss_sum, grad_x[None], *[grad[n] for n in WEIGHTS], *[delta[n] for n in WEIGHTS],
            *[new_m[n] for n in WEIGHTS], *[new_v[n] for n in WEIGHTS])
```

```python
import functools
import math

import jax
import jax.numpy as jnp
import numpy as np
from jax import lax
from jax.experimental import pallas as pl
from jax.experimental.pallas import tpu as pltpu

F32 = jnp.float32
BF16 = jnp.bfloat16
MESH = pl.DeviceIdType.MESH

N_MIXERS = 4
S5_CH = 16
S5_STATE = 64
POOL_WINDOWS = (2, 4, 8, 16)
CONV_WIDTH = 31
CONV_PAD = 32
ATT_HEADS = 8
ATT_BLOCK = 128
DILATED_PATTERNS = ((128, 1), (512, 4), (2048, 16))
REL_BUCKETS = 32
REL_MAX_DIST = 2048
X_HEADS = 4
X_HEAD_DIM = 128
NORM_EPS = 1e-6
NEG_INF = -1e30
ADAM_LR, ADAM_B1, ADAM_B2, ADAM_EPS, ADAM_WD, ADAM_STEP = 0.001, 0.9, 0.999, 1e-08, 0.01, 10
N_CHIPS = 4
VMEM_LIMIT = 56 * 1024 * 1024


def _cparams(sem=None, vmem=None):
    return pltpu.CompilerParams(dimension_semantics=sem, vmem_limit_bytes=vmem)


def _tile(n, pref):
    if n <= pref:
        return n
    t = pref
    while t >= 128:
        if n % t == 0:
            return t
        t -= 128
    return n


def _spec(arr, tr, tc, rc):
    if arr.ndim == 2:
        return pl.BlockSpec((tr, tc), lambda *g: rc(*g))
    per = arr.shape[2] // tc
    assert arr.shape[2] % tc == 0

    def imap(*g):
        r, c = rc(*g)
        return (c // per, r, c % per)

    return pl.BlockSpec((None, tr, tc), imap)


def _lshape(arr):
    return arr.shape if arr.ndim == 2 else (arr.shape[1], arr.shape[0] * arr.shape[2])


def _mm(a, b, mode, *, name, out_dtype=F32, res=None, epi=None, aux=None, out_stack=None, tm=1024, tn=1024, tk=2048):
    la, lb = _lshape(a), _lshape(b)
    if mode == "nn":
        (M, K), (K2, N) = la, lb
    elif mode == "nt":
        (M, K), (N, K2) = la, lb
    else:
        (K, M), (K2, N) = la, lb
    assert K == K2, (la, lb, mode)
    lim = {"m": M, "n": N // out_stack if out_stack else N, "k": K}
    stacked = [(a, "m" if mode == "tn" else "k"), (b, "k" if mode == "nt" else "n"), (res, "n"), (aux, "n")]
    for arr, dim in stacked:
        if arr is not None and arr.ndim == 3:
            lim[dim] = math.gcd(lim[dim], arr.shape[2])
    tm, tn, tk = _tile(lim["m"], tm), _tile(lim["n"], tn), _tile(lim["k"], tk)
    nk = K // tk
    grid = (M // tm, N // tn, nk)
    a_spec = _spec(a, tk, tm, lambda i, j, k: (k, i)) if mode == "tn" else _spec(a, tm, tk, lambda i, j, k: (i, k))
    b_spec = _spec(b, tn, tk, lambda i, j, k: (j, k)) if mode == "nt" else _spec(b, tk, tn, lambda i, j, k: (k, j))
    dims = {"nn": ((1,), (0,)), "nt": ((1,), (1,)), "tn": ((0,), (0,))}[mode]
    ins, in_specs = [a, b], [a_spec, b_spec]
    for extra in (res, aux):
        if extra is not None:
            ins.append(extra)
            in_specs.append(_spec(extra, tm, tn, lambda i, j, k: (i, j)))
    if out_stack:
        o_shape = jax.ShapeDtypeStruct((out_stack, M, N // out_stack), out_dtype)
    else:
        o_shape = jax.ShapeDtypeStruct((M, N), out_dtype)
    o_spec = _spec(o_shape, tm, tn, lambda i, j, k: (i, j))
    n_out = 2 if epi == "relu2" else 1
    has_res, has_aux = res is not None, aux is not None

    def body(*refs):
        a_ref, b_ref = refs[0], refs[1]
        pos = 2
        res_ref = aux_ref = None
        if has_res:
            res_ref = refs[pos]
            pos += 1
        if has_aux:
            aux_ref = refs[pos]
            pos += 1
        outs = refs[pos:pos + n_out]
        part = lax.dot_general(a_ref[...].astype(BF16), b_ref[...].astype(BF16), (dims, ((), ())), preferred_element_type=F32)
        if nk > 1:
            acc_ref = refs[pos + n_out]
            k = pl.program_id(2)

            @pl.when(k == 0)
            def _():
                acc_ref[...] = part

            @pl.when(k > 0)
            def _():
                acc_ref[...] += part

        @pl.when(pl.program_id(2) == nk - 1)
        def _():
            o = acc_ref[...] if nk > 1 else part
            if has_res:
                o = o + res_ref[...].astype(F32)
            if epi == "relu2":
                outs[0][...] = o.astype(outs[0].dtype)
                r = jnp.maximum(o, 0.0)
                outs[1][...] = (r * r).astype(outs[1].dtype)
            elif epi == "relu2_bwd":
                outs[0][...] = (o * (2.0 * jnp.maximum(aux_ref[...].astype(F32), 0.0))).astype(outs[0].dtype)
            else:
                outs[0][...] = o.astype(outs[0].dtype)

    out = pl.pallas_call(
        body, grid=grid, in_specs=in_specs,
        out_specs=[o_spec] * n_out if n_out > 1 else o_spec,
        out_shape=[o_shape] * n_out if n_out > 1 else o_shape,
        scratch_shapes=[pltpu.VMEM((tm, tn), F32)] if nk > 1 else [],
        compiler_params=_cparams(("parallel", "parallel", "arbitrary"), VMEM_LIMIT), name=name,
    )(*ins)
    return out


def _rms_fwd(x, g, *, name, out_dtype=BF16):
    L, D = x.shape
    tr = _tile(L, 256)

    def body(x_ref, g_ref, o_ref):
        xv = x_ref[...]
        r = lax.rsqrt(jnp.mean(xv * xv, axis=-1, keepdims=True) + NORM_EPS)
        o_ref[...] = (xv * r * g_ref[...]).astype(o_ref.dtype)

    return pl.pallas_call(
        body, grid=(L // tr,),
        in_specs=[pl.BlockSpec((tr, D), lambda i: (i, 0)), pl.BlockSpec((1, D), lambda i: (0, 0))],
        out_specs=pl.BlockSpec((tr, D), lambda i: (i, 0)),
        out_shape=jax.ShapeDtypeStruct((L, D), out_dtype),
        compiler_params=_cparams(("parallel",)), name=name)(x, g)


def _rms_bwd(x, g, dy, dres, *, name):
    L, D = x.shape
    tr = _tile(L, 256)
    has_res = dres is not None

    def body(*refs):
        x_ref, g_ref, dy_ref = refs[:3]
        dres_ref = refs[3] if has_res else None
        dx_ref, dg_ref = refs[-2:]
        xv = x_ref[...]
        dyv = dy_ref[...].astype(F32)
        r = lax.rsqrt(jnp.mean(xv * xv, axis=-1, keepdims=True) + NORM_EPS)
        xh = xv * r
        dyg = dyv * g_ref[...]
        dx = r * (dyg - xh * jnp.mean(dyg * xh, axis=-1, keepdims=True))
        if has_res:
            dx = dx + dres_ref[...]
        dx_ref[...] = dx

        @pl.when(pl.program_id(0) == 0)
        def _():
            dg_ref[...] = jnp.zeros_like(dg_ref)

        dg_ref[...] += jnp.sum(dyv * xh, axis=0, keepdims=True)

    row = pl.BlockSpec((tr, D), lambda i: (i, 0))
    vec = pl.BlockSpec((1, D), lambda i: (0, 0))
    ins = [x, g, dy] + ([dres] if has_res else [])
    return pl.pallas_call(
        body, grid=(L // tr,), in_specs=[row, vec, row] + ([row] if has_res else []),
        out_specs=[row, vec],
        out_shape=[jax.ShapeDtypeStruct((L, D), F32), jax.ShapeDtypeStruct((1, D), F32)],
        compiler_params=_cparams(("arbitrary",)), name=name)(*ins)


def _loss_head(h, g, target, *, name):
    L, D = h.shape
    tr = _tile(L, 256)

    def body(x_ref, g_ref, t_ref, loss_ref, dx_ref, dg_ref):
        xv = x_ref[...]
        r = lax.rsqrt(jnp.mean(xv * xv, axis=-1, keepdims=True) + NORM_EPS)
        xh = xv * r
        err = xh * g_ref[...] - t_ref[...]
        dyv = err * (1.0 / D)
        dyg = dyv * g_ref[...]
        dx_ref[...] = r * (dyg - xh * jnp.mean(dyg * xh, axis=-1, keepdims=True))

        @pl.when(pl.program_id(0) == 0)
        def _():
            dg_ref[...] = jnp.zeros_like(dg_ref)
            loss_ref[...] = jnp.zeros_like(loss_ref)

        dg_ref[...] += jnp.sum(dyv * xh, axis=0, keepdims=True)
        loss_ref[...] += 0.5 * jnp.sum(jnp.sum(err * err, axis=1, keepdims=True), axis=0, keepdims=True) * (1.0 / D)

    row = pl.BlockSpec((tr, D), lambda i: (i, 0))
    vec = pl.BlockSpec((1, D), lambda i: (0, 0))
    return pl.pallas_call(
        body, grid=(L // tr,), in_specs=[row, vec, row],
        out_specs=[pl.BlockSpec((1, 1), lambda i: (0, 0)), row, vec],
        out_shape=[jax.ShapeDtypeStruct((1, 1), F32), jax.ShapeDtypeStruct((L, D), F32), jax.ShapeDtypeStruct((1, D), F32)],
        compiler_params=_cparams(("arbitrary",)), name=name)(h, g, target)


S5_TL = 256
S5_LW = 512


def _dot(a, b, dims):
    return lax.dot_general(a, b, (dims, ((), ())), preferred_element_type=F32)


_NN, _NT, _TN = ((1,), (0,)), ((1,), (1,)), ((0,), (0,))


def _gelu_and_grad(y):
    k = math.sqrt(2.0 / math.pi)
    y2 = y * y
    th = jnp.tanh(k * (y + 0.044715 * y * y2))
    g = 0.5 * y * (1.0 + th)
    gp = 0.5 * (1.0 + th) + 0.5 * y * (1.0 - th * th) * k * (1.0 + 3.0 * 0.044715 * y2)
    return g, gp


def _scan_tiles(re_s, im_s, ls, lw, pw_ref, tr_ref, ti_ref, carry_s, nt, reverse, extra=None):
    row = lax.broadcasted_iota(jnp.int32, (8, lw), 0)
    a = [pw_ref[k:k + 1, ls] for k in range(6)]
    tr_t, ti_t = tr_ref[:, ls], ti_ref[:, ls]
    edge = 0 if reverse else 7

    def tile(jj, carry):
        cr, ci, acc = carry
        j = (nt - 1 - jj) if reverse else jj
        off = pl.multiple_of(j * 8, 8)
        sr, si = re_s[pl.ds(off, 8), ls], im_s[pl.ds(off, 8), ls]
        for n, k in enumerate((1, 2, 4)):
            akr, aki = a[2 * n], a[2 * n + 1]
            if reverse:
                keep, sh = row < 8 - k, 8 - k
            else:
                keep, sh = row >= k, k
            shr = jnp.where(keep, pltpu.roll(sr, sh, 0), 0.0)
            shi = jnp.where(keep, pltpu.roll(si, sh, 0), 0.0)
            sr, si = sr + akr * shr - aki * shi, si + akr * shi + aki * shr
        xr = sr + tr_t * cr - ti_t * ci
        xi = si + tr_t * ci + ti_t * cr
        re_s[pl.ds(off, 8), ls] = xr
        im_s[pl.ds(off, 8), ls] = xi
        if extra is not None:
            acc = extra(off, xr, xi, acc, row)
        return xr[edge:edge + 1, :], xi[edge:edge + 1, :], acc

    acc0 = (jnp.zeros((8, lw), F32), jnp.zeros((8, lw), F32)) if extra is not None else 0
    cr, ci, acc = lax.fori_loop(0, nt, tile, (carry_s[0:1, ls], carry_s[1:2, ls], acc0))
    carry_s[0:1, ls] = cr
    carry_s[1:2, ls] = ci
    return acc


def _s5_fwd(proj, bdr, bdi, pw, tr, ti, cdr, cdi, dskip, wglu, *, name):
    L = proj.shape[0]
    GW, S = bdr.shape
    TL = _tile(L, S5_TL)
    lw = min(S, S5_LW)

    def body(u_ref, bdr_ref, bdi_ref, pw_ref, tr_ref, ti_ref, cdr_ref, cdi_ref, d_ref, w_ref,
             y_ref, xr_ref, xi_ref, re_s, im_s, carry_s):
        @pl.when(pl.program_id(0) == 0)
        def _():
            carry_s[...] = jnp.zeros_like(carry_s)

        u = u_ref[...]
        ub = u.astype(BF16)
        re_s[...] = _dot(ub, bdr_ref[...], _NN)
        im_s[...] = _dot(ub, bdi_ref[...], _NN)
        for lc in range(S // lw):
            _scan_tiles(re_s, im_s, slice(lc * lw, (lc + 1) * lw), lw, pw_ref, tr_ref, ti_ref, carry_s, TL // 8, False)
        xrb, xib = re_s[...].astype(BF16), im_s[...].astype(BF16)
        xr_ref[...] = xrb
        xi_ref[...] = xib
        y = _dot(xrb, cdr_ref[...], _NN) - _dot(xib, cdi_ref[...], _NN) + d_ref[...] * u
        g, _ = _gelu_and_grad(y)
        z = _dot(g.astype(BF16), w_ref[...], _NN)
        y_ref[...] = g * jax.nn.sigmoid(z)

    full = lambda arr: pl.BlockSpec(arr.shape, lambda i: (0,) * arr.ndim)
    return pl.pallas_call(
        body, grid=(L // TL,),
        in_specs=[pl.BlockSpec((TL, GW), lambda i: (i, 0))] + [full(t) for t in (bdr, bdi, pw, tr, ti, cdr, cdi, dskip, wglu)],
        out_specs=[pl.BlockSpec((TL, GW), lambda i: (i, 0)), pl.BlockSpec((TL, S), lambda i: (i, 0)), pl.BlockSpec((TL, S), lambda i: (i, 0))],
        out_shape=[jax.ShapeDtypeStruct((L, GW), F32), jax.ShapeDtypeStruct((L, S), BF16), jax.ShapeDtypeStruct((L, S), BF16)],
        scratch_shapes=[pltpu.VMEM((TL, S), F32), pltpu.VMEM((TL, S), F32), pltpu.VMEM((8, S), F32)],
        compiler_params=_cparams(("arbitrary",), VMEM_LIMIT), name=name,
    )(proj, bdr, bdi, pw, tr, ti, cdr, cdi, dskip, wglu)


def _s5_bwd(proj, xr, xi, dout, bdr, bdi, pwc, trc, tic, cdr, cdi, dskip, wglu, *, name):
    L = proj.shape[0]
    GW, S = bdr.shape
    TL = _tile(L, S5_TL)
    nc = L // TL
    lw = min(S, S5_LW)

    def body(u_ref, xr_ref, xi_ref, xrp_ref, xip_ref, do_ref, bdr_ref, bdi_ref, pw_ref, tr_ref, ti_ref, cdr_ref, cdi_ref,
             d_ref, w_ref, du_ref, dw_ref, dd_ref, dcdr_ref, dcdi_ref, dbdr_ref, dbdi_ref, dar_ref, dai_ref,
             gr_s, gi_s, xfr, xfi, carry_s):
        i = pl.program_id(0)

        @pl.when(i == 0)
        def _():
            carry_s[...] = jnp.zeros_like(carry_s)
            for r in (dw_ref, dd_ref, dcdr_ref, dcdi_ref, dbdr_ref, dbdi_ref, dar_ref, dai_ref):
                r[...] = jnp.zeros_like(r)

        u = u_ref[...]
        ub = u.astype(BF16)
        xrb, xib = xr_ref[...], xi_ref[...]
        y = _dot(xrb, cdr_ref[...], _NN) - _dot(xib, cdi_ref[...], _NN) + d_ref[...] * u
        g, gp = _gelu_and_grad(y)
        gb = g.astype(BF16)
        sg = jax.nn.sigmoid(_dot(gb, w_ref[...], _NN))
        dout = do_ref[...]
        dz = (dout * g * sg * (1.0 - sg)).astype(BF16)
        dg = dout * sg + _dot(dz, w_ref[...], _NT)
        dw_ref[...] += _dot(gb, dz, _TN)
        dy = dg * gp
        dyb = dy.astype(BF16)
        dd_ref[...] += jnp.sum(dy * u, axis=0, keepdims=True)
        gr_s[...] = _dot(dyb, cdr_ref[...], _NT)
        gi_s[...] = -_dot(dyb, cdi_ref[...], _NT)
        dcdr_ref[...] += _dot(xrb, dyb, _TN)
        dcdi_ref[...] -= _dot(xib, dyb, _TN)
        live = (i < nc - 1).astype(F32)
        xfr[0:8, :] = xrp_ref[...].astype(F32) * live
        xfi[0:8, :] = xip_ref[...].astype(F32) * live
        xfr[8:, :] = xrb.astype(F32)
        xfi[8:, :] = xib.astype(F32)
        for lc in range(S // lw):
            ls = slice(lc * lw, (lc + 1) * lw)

            def extra(off, lr, li, acc, row, ls=ls):
                cur_r, cur_i = xfr[pl.ds(off + 8, 8), ls], xfi[pl.ds(off + 8, 8), ls]
                prv_r, prv_i = xfr[pl.ds(off, 8), ls], xfi[pl.ds(off, 8), ls]
                xpr = jnp.where(row >= 1, pltpu.roll(cur_r, 1, 0), prv_r[7:8, :])
                xpi = jnp.where(row >= 1, pltpu.roll(cur_i, 1, 0), prv_i[7:8, :])
                return acc[0] + lr * xpr + li * xpi, acc[1] - lr * xpi + li * xpr

            acc = _scan_tiles(gr_s, gi_s, ls, lw, pw_ref, tr_ref, ti_ref, carry_s, TL // 8, True, extra)
            dar_ref[:, ls] += acc[0]
            dai_ref[:, ls] += acc[1]
        lrb, lib = gr_s[...].astype(BF16), gi_s[...].astype(BF16)
        du_ref[...] = dy * d_ref[...] + _dot(lrb, bdr_ref[...], _NT) + _dot(lib, bdi_ref[...], _NT)
        dbdr_ref[...] += _dot(ub, lrb, _TN)
        dbdi_ref[...] += _dot(ub, lib, _TN)

    rev = lambda i: nc - 1 - i
    full = lambda arr: pl.BlockSpec(arr.shape, lambda i: (0,) * arr.ndim)
    chunk = lambda w: pl.BlockSpec((TL, w), lambda i: (rev(i), 0))
    prev8 = pl.BlockSpec((8, S), lambda i: (jnp.maximum(rev(i) * (TL // 8) - 1, 0), 0))
    acc_shapes = [(GW, GW), (1, GW), (S, GW), (S, GW), (GW, S), (GW, S), (8, S), (8, S)]
    return pl.pallas_call(
        body, grid=(nc,),
        in_specs=[chunk(GW), chunk(S), chunk(S), prev8, prev8, chunk(GW)] + [full(t) for t in (bdr, bdi, pwc, trc, tic, cdr, cdi, dskip, wglu)],
        out_specs=[chunk(GW)] + [pl.BlockSpec(s, lambda i: (0, 0)) for s in acc_shapes],
        out_shape=[jax.ShapeDtypeStruct((L, GW), F32)] + [jax.ShapeDtypeStruct(s, F32) for s in acc_shapes],
        scratch_shapes=[pltpu.VMEM((TL, S), F32), pltpu.VMEM((TL, S), F32), pltpu.VMEM((TL + 8, S), F32),
                        pltpu.VMEM((TL + 8, S), F32), pltpu.VMEM((8, S), F32)],
        compiler_params=_cparams(("arbitrary",), VMEM_LIMIT), name=name,
    )(proj, xr, xi, xr, xi, dout, bdr, bdi, pwc, trc, tic, cdr, cdi, dskip, wglu)


def _cpow_tables(ar, ai):
    def cm(a, b):
        return a[0] * b[0] - a[1] * b[1], a[0] * b[1] + a[1] * b[0]
    p = [(ar, ai)]
    for _ in range(7):
        p.append(cm(p[-1], p[0]))
    z = jnp.zeros_like(ar)
    pw = jnp.stack([p[0][0], p[0][1], p[1][0], p[1][1], p[3][0], p[3][1], z, z])
    return pw, jnp.stack([q[0] for q in p]), jnp.stack([q[1] for q in p])


def _s5_disc(lam_re, lam_im, log_dt, b_re, b_im):
    dt = jnp.exp(log_dt)[:, None]
    mag = jnp.exp(lam_re * dt)
    ab_r, ab_i = mag * jnp.cos(lam_im * dt), mag * jnp.sin(lam_im * dt)
    den = lam_re * lam_re + lam_im * lam_im
    nr, ni = ab_r - 1.0, ab_i
    f_r = ((nr * lam_re + ni * lam_im) / den)[..., None]
    f_i = ((ni * lam_re - nr * lam_im) / den)[..., None]
    return ab_r, ab_i, f_r * b_re - f_i * b_im, f_r * b_im + f_i * b_re


def _bd(t):
    G, A, B = t.shape
    return (t[:, :, None, :] * jnp.eye(G, dtype=t.dtype)[:, None, :, None]).reshape(G * A, G * B)


def _bd_extract(m, G):
    A, B = m.shape[0] // G, m.shape[1] // G
    return jnp.sum(m.reshape(G, A, G, B) * jnp.eye(G, dtype=m.dtype)[:, None, :, None], axis=2)


POOL_TQ = 256


def _band(shape, row0, col0, win, transpose):
    r = lax.broadcasted_iota(jnp.int32, shape, 0) + row0
    c = lax.broadcasted_iota(jnp.int32, shape, 1) + col0
    d = (c - r) if transpose else (r - c)
    return ((d >= 0) & (d < win)).astype(BF16)


def _band_dot(band, v):
    hi = v.astype(BF16)
    lo = (v - hi.astype(F32)).astype(BF16)
    return _dot(band, hi, _NN) + _dot(band, lo, _NN)


def _pool_p(u_prev, u_cur, i, win, tq):
    t0 = i * tq
    cnt = jnp.minimum(lax.broadcasted_iota(jnp.int32, (tq, 1), 0) + t0 + 1, win).astype(F32)
    live = (i > 0).astype(F32)
    acc = _band_dot(_band((tq, tq), t0, t0, win, False), u_cur)
    acc += _band_dot(_band((tq, tq), t0, t0 - tq, win, False), u_prev * live)
    return acc / cnt - u_cur


def _pool_fwd(proj, pool_w, pool_scale3, *, name):
    L = proj.shape[0]
    nw, PC, _ = pool_w.shape
    tq = _tile(L, POOL_TQ)

    def body(up_ref, uc_ref, w_ref, s_ref, y_ref):
        g, i = pl.program_id(0), pl.program_id(1)
        win = jnp.left_shift(2, g)
        p = _pool_p(up_ref[...], uc_ref[...], i, win, tq)
        y_ref[...] = _dot(p.astype(BF16), w_ref[...].astype(BF16), _NN) * s_ref[...]

    return pl.pallas_call(
        body, grid=(nw, L // tq),
        in_specs=[pl.BlockSpec((tq, PC), lambda g, i: (jnp.maximum(i - 1, 0), nw + g)),
                  pl.BlockSpec((tq, PC), lambda g, i: (i, nw + g)),
                  pl.BlockSpec((None, PC, PC), lambda g, i: (g, 0, 0)),
                  pl.BlockSpec((None, 1, PC), lambda g, i: (g, 0, 0))],
        out_specs=pl.BlockSpec((tq, PC), lambda g, i: (i, g)),
        out_shape=jax.ShapeDtypeStruct((L, nw * PC), F32),
        compiler_params=_cparams(("parallel", "parallel")), name=name)(proj, proj, pool_w, pool_scale3)


def _pool_bwd(proj, dy, pool_w, pool_scale3, *, name):
    L = proj.shape[0]
    nw, PC, _ = pool_w.shape
    tq = _tile(L, POOL_TQ)
    nq = L // tq

    def body(up_ref, uc_ref, dyc_ref, dyn_ref, w_ref, s_ref, du_ref, dw_ref, ds_ref):
        g, i = pl.program_id(0), pl.program_id(1)
        win = jnp.left_shift(2, g)

        @pl.when(i == 0)
        def _():
            dw_ref[...] = jnp.zeros_like(dw_ref)
            ds_ref[...] = jnp.zeros_like(ds_ref)

        wb = w_ref[...].astype(BF16)
        p = _pool_p(up_ref[...], uc_ref[...], i, win, tq).astype(BF16)
        dyc = dyc_ref[...]
        ds_ref[...] += jnp.sum(dyc * _dot(p, wb, _NN), axis=0, keepdims=True)
        dys = (dyc * s_ref[...]).astype(BF16)
        dw_ref[...] += _dot(p, dys, _TN)
        t0 = i * tq
        rows = lax.broadcasted_iota(jnp.int32, (tq, 1), 0)
        dp_c = _dot(dys, wb, _NT)
        q_c = dp_c / jnp.minimum(rows + t0 + 1, win).astype(F32)
        live = (i < nq - 1).astype(F32)
        dp_n = _dot((dyn_ref[...] * s_ref[...] * live).astype(BF16), wb, _NT)
        q_n = dp_n / jnp.minimum(rows + t0 + tq + 1, win).astype(F32)
        du = _band_dot(_band((tq, tq), t0, t0, win, True), q_c) + _band_dot(_band((tq, tq), t0, t0 + tq, win, True), q_n)
        du_ref[...] = du - dp_c

    return pl.pallas_call(
        body, grid=(nw, nq),
        in_specs=[pl.BlockSpec((tq, PC), lambda g, i: (jnp.maximum(i - 1, 0), nw + g)),
                  pl.BlockSpec((tq, PC), lambda g, i: (i, nw + g)),
                  pl.BlockSpec((tq, PC), lambda g, i: (i, g)),
                  pl.BlockSpec((tq, PC), lambda g, i: (jnp.minimum(i + 1, nq - 1), g)),
                  pl.BlockSpec((None, PC, PC), lambda g, i: (g, 0, 0)),
                  pl.BlockSpec((None, 1, PC), lambda g, i: (g, 0, 0))],
        out_specs=[pl.BlockSpec((tq, PC), lambda g, i: (i, g)),
                   pl.BlockSpec((None, PC, PC), lambda g, i: (g, 0, 0)),
                   pl.BlockSpec((None, 1, PC), lambda g, i: (g, 0, 0))],
        out_shape=[jax.ShapeDtypeStruct((L, nw * PC), F32), jax.ShapeDtypeStruct((nw, PC, PC), F32),
                   jax.ShapeDtypeStruct((nw, 1, PC), F32)],
        compiler_params=_cparams(("parallel", "arbitrary")), name=name)(proj, proj, dy, dy, pool_w, pool_scale3)


CONV_RC = 256


def _conv1_fwd(proj, w_dw, b_dw, *, name):
    L = proj.shape[0]
    GW = w_dw.shape[1]
    CB = min(GW, 128)
    nb = GW // CB
    RC = _tile(L, CONV_RC)
    n = RC + CONV_PAD

    def body(val_ref, gate_ref, w_ref, b_ref, o_ref, hp):
        hp[0:CONV_PAD, :] = jnp.zeros((CONV_PAD, CB), F32)
        hp[CONV_PAD:, :] = val_ref[...] * jax.nn.sigmoid(gate_ref[...])
        wv = w_ref[...]

        def chunk(ci, carry):
            c0 = pl.multiple_of(ci * RC, 8)
            win = hp[pl.ds(c0, n), :]
            acc = jnp.zeros((RC, CB), F32) + b_ref[...]
            for k in range(CONV_WIDTH):
                acc = acc + wv[k:k + 1, :] * pltpu.roll(win, n - (k + 2), 0)[0:RC]
            o_ref[pl.ds(c0, RC), :] = acc
            return carry

        lax.fori_loop(0, L // RC, chunk, 0)

    col = lambda off: pl.BlockSpec((L, CB), lambda c: (0, off * nb + c))
    return pl.pallas_call(
        body, grid=(nb,),
        in_specs=[col(2), col(3), pl.BlockSpec((CONV_PAD, CB), lambda c: (0, c)), pl.BlockSpec((1, CB), lambda c: (0, c))],
        out_specs=pl.BlockSpec((L, CB), lambda c: (0, c)),
        out_shape=jax.ShapeDtypeStruct((L, GW), F32),
        scratch_shapes=[pltpu.VMEM((L + CONV_PAD, CB), F32)],
        compiler_params=_cparams(("parallel",)), name=name)(proj, proj, w_dw, b_dw)


def _conv1_bwd(proj, dhc, w_dw, *, name):
    L = proj.shape[0]
    GW = w_dw.shape[1]
    CB = min(GW, 128)
    nb = GW // CB
    RC = _tile(L, CONV_RC)
    n = RC + CONV_PAD

    def body(val_ref, gate_ref, d_ref, w_ref, dval_ref, dgate_ref, dw_ref, db_ref, hp, dp):
        val = val_ref[...]
        sg = jax.nn.sigmoid(gate_ref[...])
        hp[0:CONV_PAD, :] = jnp.zeros((CONV_PAD, CB), F32)
        hp[CONV_PAD:, :] = val * sg
        dp[0:L, :] = d_ref[...]
        dp[L:, :] = jnp.zeros((CONV_PAD, CB), F32)
        dw_ref[...] = jnp.zeros_like(dw_ref)
        db_ref[...] = jnp.sum(d_ref[...], axis=0, keepdims=True)
        wv = w_ref[...]

        def chunk(ci, carry):
            c0 = pl.multiple_of(ci * RC, 8)
            win = hp[pl.ds(c0, n), :]
            dwin = dp[pl.ds(c0, n), :]
            d = dwin[0:RC]
            dh = jnp.zeros((RC, CB), F32)
            for k in range(CONV_WIDTH):
                dw_ref[k:k + 1, :] += jnp.sum(d * pltpu.roll(win, n - (k + 2), 0)[0:RC], axis=0, keepdims=True)
                sh = CONV_WIDTH - 1 - k
                dh = dh + wv[k:k + 1, :] * (pltpu.roll(dwin, n - sh, 0)[0:RC] if sh else d)
            v, s = val_ref[pl.ds(c0, RC), :], jax.nn.sigmoid(gate_ref[pl.ds(c0, RC), :])
            dval_ref[pl.ds(c0, RC), :] = dh * s
            dgate_ref[pl.ds(c0, RC), :] = dh * v * s * (1.0 - s)
            return carry

        lax.fori_loop(0, L // RC, chunk, 0)

    col = lambda off: pl.BlockSpec((L, CB), lambda c: (0, off * nb + c))
    own = pl.BlockSpec((L, CB), lambda c: (0, c))
    return pl.pallas_call(
        body, grid=(nb,),
        in_specs=[col(2), col(3), own, pl.BlockSpec((CONV_PAD, CB), lambda c: (0, c))],
        out_specs=[own, own, pl.BlockSpec((CONV_PAD, CB), lambda c: (0, c)), pl.BlockSpec((1, CB), lambda c: (0, c))],
        out_shape=[jax.ShapeDtypeStruct((L, GW), F32), jax.ShapeDtypeStruct((L, GW), F32),
                   jax.ShapeDtypeStruct((CONV_PAD, GW), F32), jax.ShapeDtypeStruct((1, GW), F32)],
        scratch_shapes=[pltpu.VMEM((L + CONV_PAD, CB), F32), pltpu.VMEM((L + CONV_PAD, CB), F32)],
        compiler_params=_cparams(("parallel",)), name=name)(proj, proj, dhc, w_dw)


def _ln_silu(hc, g, b):
    mu = jnp.mean(hc, axis=-1, keepdims=True)
    xc = hc - mu
    r = lax.rsqrt(jnp.mean(xc * xc, axis=-1, keepdims=True) + NORM_EPS)
    xh = xc * r
    a = xh * g + b
    sg = jax.nn.sigmoid(a)
    return xh, r, a, sg


def _conv2_fwd(hc, ln_g, ln_b, w_pw, *, name):
    L, GW = hc.shape
    tr = _tile(L, 256)

    def body(h_ref, g_ref, b_ref, w_ref, y_ref):
        _, _, a, sg = _ln_silu(h_ref[...], g_ref[...], b_ref[...])
        y_ref[...] = _dot((a * sg).astype(BF16), w_ref[...], _NN)

    row = pl.BlockSpec((tr, GW), lambda i: (i, 0))
    vec = pl.BlockSpec((1, GW), lambda i: (0, 0))
    return pl.pallas_call(
        body, grid=(L // tr,), in_specs=[row, vec, vec, pl.BlockSpec((GW, GW), lambda i: (0, 0))],
        out_specs=row, out_shape=jax.ShapeDtypeStruct((L, GW), F32),
        compiler_params=_cparams(("parallel",)), name=name)(hc, ln_g, ln_b, w_pw)


def _conv2_bwd(hc, ln_g, ln_b, w_pw, dy, *, name):
    L, GW = hc.shape
    tr = _tile(L, 256)

    def body(h_ref, g_ref, b_ref, w_ref, dy_ref, dh_ref, dg_ref, db_ref, dw_ref):
        @pl.when(pl.program_id(0) == 0)
        def _():
            for r in (dg_ref, db_ref, dw_ref):
                r[...] = jnp.zeros_like(r)

        xh, r, a, sg = _ln_silu(h_ref[...], g_ref[...], b_ref[...])
        dyb = dy_ref[...].astype(BF16)
        dw_ref[...] += _dot((a * sg).astype(BF16), dyb, _TN)
        da = _dot(dyb, w_ref[...], _NT) * (sg + a * sg * (1.0 - sg))
        dg_ref[...] += jnp.sum(da * xh, axis=0, keepdims=True)
        db_ref[...] += jnp.sum(da, axis=0, keepdims=True)
        dxh = da * g_ref[...]
        dh_ref[...] = r * (dxh - jnp.mean(dxh, axis=-1, keepdims=True) - xh * jnp.mean(dxh * xh, axis=-1, keepdims=True))

    row = pl.BlockSpec((tr, GW), lambda i: (i, 0))
    vec = pl.BlockSpec((1, GW), lambda i: (0, 0))
    mat = pl.BlockSpec((GW, GW), lambda i: (0, 0))
    return pl.pallas_call(
        body, grid=(L // tr,), in_specs=[row, vec, vec, mat, row],
        out_specs=[row, vec, vec, mat],
        out_shape=[jax.ShapeDtypeStruct((L, GW), F32), jax.ShapeDtypeStruct((1, GW), F32), jax.ShapeDtypeStruct((1, GW), F32),
                   jax.ShapeDtypeStruct((GW, GW), F32)],
        compiler_params=_cparams(("arbitrary",)), name=name)(hc, ln_g, ln_b, w_pw, dy)


def _grp_fwd(ys, g, *, name):
    L, GW = ys[0].shape
    tr = _tile(L, 256)

    def body(*refs):
        g_ref, o_ref = refs[4], refs[5]
        for m in range(N_MIXERS):
            yv = refs[m][...]
            r = lax.rsqrt(jnp.mean(yv * yv, axis=-1, keepdims=True) + NORM_EPS)
            o_ref[:, m * GW:(m + 1) * GW] = (yv * r * g_ref[:, m * GW:(m + 1) * GW]).astype(o_ref.dtype)

    row = pl.BlockSpec((tr, GW), lambda i: (i, 0))
    return pl.pallas_call(
        body, grid=(L // tr,), in_specs=[row] * 4 + [pl.BlockSpec((1, N_MIXERS * GW), lambda i: (0, 0))],
        out_specs=pl.BlockSpec((tr, N_MIXERS * GW), lambda i: (i, 0)),
        out_shape=jax.ShapeDtypeStruct((L, N_MIXERS * GW), BF16),
        compiler_params=_cparams(("parallel",)), name=name)(*ys, g)


def _grp_bwd(ys, g, dy, *, name):
    L, GW = ys[0].shape
    tr = _tile(L, 256)

    def body(*refs):
        g_ref, dy_ref = refs[4], refs[5]
        outs, dg_ref = refs[6:10], refs[10]

        @pl.when(pl.program_id(0) == 0)
        def _():
            dg_ref[...] = jnp.zeros_like(dg_ref)

        for m in range(N_MIXERS):
            cs = slice(m * GW, (m + 1) * GW)
            yv = refs[m][...]
            r = lax.rsqrt(jnp.mean(yv * yv, axis=-1, keepdims=True) + NORM_EPS)
            xh = yv * r
            dyv = dy_ref[:, cs]
            dyg = dyv * g_ref[:, cs]
            outs[m][...] = r * (dyg - xh * jnp.mean(dyg * xh, axis=-1, keepdims=True))
            dg_ref[:, cs] += jnp.sum(dyv * xh, axis=0, keepdims=True)

    row = pl.BlockSpec((tr, GW), lambda i: (i, 0))
    wide = pl.BlockSpec((tr, N_MIXERS * GW), lambda i: (i, 0))
    vec = pl.BlockSpec((1, N_MIXERS * GW), lambda i: (0, 0))
    return pl.pallas_call(
        body, grid=(L // tr,), in_specs=[row] * 4 + [vec, wide],
        out_specs=[row] * 4 + [vec],
        out_shape=[jax.ShapeDtypeStruct((L, GW), F32)] * 4 + [jax.ShapeDtypeStruct((1, N_MIXERS * GW), F32)],
        compiler_params=_cparams(("arbitrary",)), name=name)(*ys, g, dy)


ATT_PAR = 2


def _datt_lanes(GW):
    E = GW // ATT_HEADS
    lb = min(GW, 128)
    return lb, lb // E, E


def _datt_rows(c, br, nblk):
    dil = 4 ** br
    nbs = nblk // dil
    r, jb = c // nbs, c % nbs
    return r + dil * ATT_BLOCK * jb, r + dil * ATT_BLOCK * jnp.maximum(jb - 1, 0), jb == 0


def _datt_fwd(proj, bias, *, name):
    L = proj.shape[0]
    GW = proj.shape[1] // 7
    lb, hps, E = _datt_lanes(GW)
    B = ATT_BLOCK
    nblk = L // B
    scale = E ** -0.5

    def body(q_ref, k_ref, v_ref, b_ref, y_ref, lse_ref, m_s, l_s):
        m_s[...] = jnp.full(m_s.shape, NEG_INF, F32)
        l_s[...] = jnp.zeros(l_s.shape, F32)
        y_ref[...] = jnp.zeros(y_ref.shape, F32)
        lane = lax.broadcasted_iota(jnp.int32, (B, lb), 1) // E

        for br, (_, dil) in enumerate(DILATED_PATTERNS):
            def step(i, carry, br=br, dil=dil):
                loaded = []
                for u in range(ATT_PAR):
                    start, pstart, first = _datt_rows(i + u * (nblk // ATT_PAR), br, nblk)
                    rows, prows = pl.ds(start, B, stride=dil), pl.ds(pstart, B, stride=dil)
                    loaded.append((rows, first, q_ref[rows, :] * scale, k_ref[rows, :].astype(BF16), k_ref[prows, :].astype(BF16),
                                   v_ref[rows, :].astype(BF16), v_ref[prows, :].astype(BF16), m_s[rows, :], l_s[rows, :], y_ref[rows, :]))
                done = []
                for rows, first, q, kc, kp, vc, vp, m_old, l_old, a_old in loaded:
                    m_new, l_new, a_new = m_old, l_old, a_old
                    for a in range(hps):
                        sel = lane == a
                        qm = jnp.where(sel, q, 0.0).astype(BF16)
                        s_p = _dot(qm, kp, _NT) + b_ref[br, a, :, 0:B]
                        s_c = _dot(qm, kc, _NT) + b_ref[br, a, :, B:2 * B]
                        s_p = jnp.where(first, NEG_INF, s_p)
                        mo, lo = m_old[:, a * E:a * E + 1], l_old[:, a * E:a * E + 1]
                        mn = jnp.maximum(mo, jnp.maximum(jnp.max(s_p, axis=-1, keepdims=True), jnp.max(s_c, axis=-1, keepdims=True)))
                        alpha = jnp.exp(mo - mn)
                        p_p, p_c = jnp.exp(s_p - mn), jnp.exp(s_c - mn)
                        ln = alpha * lo + jnp.sum(p_p, axis=-1, keepdims=True) + jnp.sum(p_c, axis=-1, keepdims=True)
                        pv = _dot(p_p.astype(BF16), vp, _NN) + _dot(p_c.astype(BF16), vc, _NN)
                        m_new = jnp.where(sel, mn, m_new)
                        l_new = jnp.where(sel, ln, l_new)
                        a_new = jnp.where(sel, alpha * a_old + pv, a_new)
                    done.append((rows, m_new, l_new, a_new))
                for rows, m_new, l_new, a_new in done:
                    m_s[rows, :] = m_new
                    l_s[rows, :] = l_new
                    y_ref[rows, :] = a_new
                return carry

            lax.fori_loop(0, nblk // ATT_PAR, step, 0)
        y_ref[...] = y_ref[...] / l_s[...]
        lse_ref[...] = m_s[...] + jnp.log(l_s[...])

    col = lambda off: pl.BlockSpec((L, lb), lambda h: (0, off * (GW // lb) + h))
    own = pl.BlockSpec((L, lb), lambda h: (0, h))
    return pl.pallas_call(
        body, grid=(GW // lb,),
        in_specs=[col(4), col(5), col(6), pl.BlockSpec((len(DILATED_PATTERNS), hps, B, 2 * B), lambda h: (0, h, 0, 0))],
        out_specs=[own, own], out_shape=[jax.ShapeDtypeStruct((L, GW), F32)] * 2,
        scratch_shapes=[pltpu.VMEM((L, lb), F32), pltpu.VMEM((L, lb), F32)],
        compiler_params=_cparams(("parallel",), VMEM_LIMIT), name=name)(proj, proj, proj, bias)


def _datt_bwd(proj, bias, y, lse, dy, *, name):
    L = proj.shape[0]
    GW = proj.shape[1] // 7
    lb, hps, E = _datt_lanes(GW)
    B = ATT_BLOCK
    nblk = L // B
    scale = E ** -0.5

    def body(q_ref, k_ref, v_ref, b_ref, y_ref, lse_ref, dy_ref, dq_ref, dk_ref, dv_ref, db_ref):
        for r in (dq_ref, dk_ref, dv_ref, db_ref):
            r[...] = jnp.zeros(r.shape, F32)
        lane = lax.broadcasted_iota(jnp.int32, (B, lb), 1) // E

        for br, (_, dil) in enumerate(DILATED_PATTERNS):
            def step(i, carry, br=br, dil=dil):
                loaded = []
                for u in range(ATT_PAR):
                    start, pstart, first = _datt_rows(i + u * (nblk // ATT_PAR), br, nblk)
                    rows, prows = pl.ds(start, B, stride=dil), pl.ds(pstart, B, stride=dil)
                    dyr = dy_ref[rows, :]
                    loaded.append((rows, prows, first, q_ref[rows, :] * scale, k_ref[rows, :].astype(BF16), k_ref[prows, :].astype(BF16),
                                   v_ref[rows, :].astype(BF16), v_ref[prows, :].astype(BF16), dyr, lse_ref[rows, :], dyr * y_ref[rows, :]))
                done = []
                dbias = [None] * hps
                for rows, prows, first, q, kc, kp, vc, vp, dyr, lser, dyy in loaded:
                    dq = jnp.zeros((B, lb), F32)
                    dkc, dkp, dvc, dvp = dq, dq, dq, dq
                    for a in range(hps):
                        sel = lane == a
                        qm = jnp.where(sel, q, 0.0).astype(BF16)
                        dym = jnp.where(sel, dyr, 0.0).astype(BF16)
                        dm = jnp.sum(jnp.where(sel, dyy, 0.0), axis=-1, keepdims=True)
                        lse_a = lser[:, a * E:a * E + 1]
                        s_p = _dot(qm, kp, _NT) + b_ref[br, a, :, 0:B]
                        s_c = _dot(qm, kc, _NT) + b_ref[br, a, :, B:2 * B]
                        s_p = jnp.where(first, NEG_INF, s_p)
                        p_p, p_c = jnp.exp(s_p - lse_a), jnp.exp(s_c - lse_a)
                        ds_p = p_p * (_dot(dym, vp, _NT) - dm)
                        ds_c = p_c * (_dot(dym, vc, _NT) - dm)
                        dbias[a] = (ds_p, ds_c) if dbias[a] is None else (dbias[a][0] + ds_p, dbias[a][1] + ds_c)
                        dsb_p, dsb_c = ds_p.astype(BF16), ds_c.astype(BF16)
                        dq = dq + jnp.where(sel, _dot(dsb_p, kp, _NN) + _dot(dsb_c, kc, _NN), 0.0)
                        dkp = dkp + _dot(dsb_p, qm, _TN)
                        dkc = dkc + _dot(dsb_c, qm, _TN)
                        dvp = dvp + _dot(p_p.astype(BF16), dym, _TN)
                        dvc = dvc + _dot(p_c.astype(BF16), dym, _TN)
                    done.append((rows, prows, dq, dkp, dkc, dvp, dvc))
                for a in range(hps):
                    db_ref[br, a, :, 0:B] += dbias[a][0]
                    db_ref[br, a, :, B:2 * B] += dbias[a][1]
                for rows, prows, dq, dkp, dkc, dvp, dvc in done:
                    dq_ref[rows, :] += dq * scale
                    dk_ref[prows, :] += dkp
                    dk_ref[rows, :] += dkc
                    dv_ref[prows, :] += dvp
                    dv_ref[rows, :] += dvc
                return carry

            lax.fori_loop(0, nblk // ATT_PAR, step, 0)

    col = lambda off: pl.BlockSpec((L, lb), lambda h: (0, off * (GW // lb) + h))
    own = pl.BlockSpec((L, lb), lambda h: (0, h))
    bsp = pl.BlockSpec((len(DILATED_PATTERNS), hps, B, 2 * B), lambda h: (0, h, 0, 0))
    return pl.pallas_call(
        body, grid=(GW // lb,),
        in_specs=[col(4), col(5), col(6), bsp, own, own, own], out_specs=[own, own, own, bsp],
        out_shape=[jax.ShapeDtypeStruct((L, GW), F32)] * 3 + [jax.ShapeDtypeStruct((len(DILATED_PATTERNS), ATT_HEADS, B, 2 * B), F32)],
        compiler_params=_cparams(("parallel",), VMEM_LIMIT), name=name)(proj, proj, proj, bias, y, lse, dy)


def _t5_bucket(dist):
    n = np.maximum(dist, 0)
    max_exact = REL_BUCKETS // 2
    large = max_exact + (np.log(np.maximum(n, 1) / max_exact) / np.log(REL_MAX_DIST / max_exact)
                         * (REL_BUCKETS - max_exact)).astype(np.int64)
    large = np.minimum(large, REL_BUCKETS - 1)
    return np.where(n < max_exact, n, large).astype(np.int32)


def _att_tables():
    a = np.arange(ATT_BLOCK)[:, None]
    b = np.arange(2 * ATT_BLOCK)[None, :]
    sub = a + ATT_BLOCK - b
    buckets, valids = [], []
    for window, dil in DILATED_PATTERNS:
        buckets.append(_t5_bucket(sub * dil))
        valids.append((sub >= 0) & (sub <= window // dil))
    return np.stack(buckets), np.stack(valids)


def _xatt_fwd(q, k, v, *, name):
    L, XW = q.shape
    M = k.shape[0]
    tr = _tile(L, 512)
    scale = X_HEAD_DIM ** -0.5

    def body(q_ref, k_ref, v_ref, o_ref):
        s = _dot(q_ref[...], k_ref[...], _NT) * scale
        p = jnp.exp(s - jnp.max(s, axis=-1, keepdims=True))
        den = jnp.sum(p, axis=-1, keepdims=True)
        o_ref[...] = (_dot(p.astype(BF16), v_ref[...], _NN) / den).astype(o_ref.dtype)

    qs = pl.BlockSpec((tr, X_HEAD_DIM), lambda h, i: (i, h))
    ks = pl.BlockSpec((M, X_HEAD_DIM), lambda h, i: (0, h))
    return pl.pallas_call(
        body, grid=(XW // X_HEAD_DIM, L // tr), in_specs=[qs, ks, ks], out_specs=qs,
        out_shape=jax.ShapeDtypeStruct((L, XW), BF16),
        compiler_params=_cparams(("parallel", "parallel")), name=name)(q, k, v)


def _xatt_bwd(q, k, v, do, *, name):
    L, XW = q.shape
    M = k.shape[0]
    tr = _tile(L, 512)
    scale = X_HEAD_DIM ** -0.5

    def body(q_ref, k_ref, v_ref, do_ref, dq_ref, dk_ref, dv_ref):
        @pl.when(pl.program_id(1) == 0)
        def _():
            dk_ref[...] = jnp.zeros_like(dk_ref)
            dv_ref[...] = jnp.zeros_like(dv_ref)

        qv, kv, vv = q_ref[...], k_ref[...], v_ref[...]
        s = _dot(qv, kv, _NT) * scale
        p = jnp.exp(s - jnp.max(s, axis=-1, keepdims=True))
        p = p / jnp.sum(p, axis=-1, keepdims=True)
        dob = do_ref[...].astype(BF16)
        pb = p.astype(BF16)
        dv_ref[...] += _dot(pb, dob, _TN)
        dp = _dot(dob, vv, _NT)
        ds = (p * (dp - jnp.sum(dp * p, axis=-1, keepdims=True)) * scale).astype(BF16)
        dq_ref[...] = _dot(ds, kv, _NN)
        dk_ref[...] += _dot(ds, qv, _TN)

    qs = pl.BlockSpec((tr, X_HEAD_DIM), lambda h, i: (i, h))
    ks = pl.BlockSpec((M, X_HEAD_DIM), lambda h, i: (0, h))
    return pl.pallas_call(
        body, grid=(XW // X_HEAD_DIM, L // tr), in_specs=[qs, ks, ks, qs], out_specs=[qs, ks, ks],
        out_shape=[jax.ShapeDtypeStruct((L, XW), F32), jax.ShapeDtypeStruct((M, XW), F32), jax.ShapeDtypeStruct((M, XW), F32)],
        compiler_params=_cparams(("parallel", "arbitrary")), name=name)(q, k, v, do)


_ANY = pl.BlockSpec(memory_space=pl.ANY)


def _place():
    mx, my, mc = lax.axis_index("x"), lax.axis_index("y"), lax.axis_index("c")
    chips = [(1 - mx, my), (mx, 1 - my), (1 - mx, 1 - my)]
    return mx, my, mc, chips


def _rcopy(src, dst, ssem, rsem, dev):
    return pltpu.make_async_remote_copy(src_ref=src, dst_ref=dst, send_sem=ssem, recv_sem=rsem, device_id=dev,
                                        device_id_type=MESH)


STAGE_BYTES = 2 * 1024 * 1024


def _staged_copy(pairs, buf, isem, osem):
    n = len(pairs)
    ins = [pltpu.make_async_copy(src, buf.at[i % 2], isem.at[i % 2]) for i, (src, _) in enumerate(pairs)]
    outs = [pltpu.make_async_copy(buf.at[i % 2], dst, osem.at[i % 2]) for i, (_, dst) in enumerate(pairs)]
    ins[0].start()
    for i in range(n):
        if i + 1 < n:
            if i >= 1:
                outs[i - 1].wait()
            ins[i + 1].start()
        ins[i].wait()
        outs[i].start()
    for i in range(max(n - 2, 0), n):
        outs[i].wait()


_HBM = pl.BlockSpec(memory_space=pltpu.HBM)
_SEMS = pl.BlockSpec(memory_space=pltpu.SEMAPHORE)
_VM = pl.BlockSpec(memory_space=pltpu.VMEM)
_DATAFLOW = pltpu.SideEffectType.DATAFLOW_SIDE_EFFECTING


def _ag_copies(x_refs, land_refs, ssem, rsem, inbound, which=None):
    mx, my, mc, chips = _place()
    me = 2 * mx + my
    cps = []
    for i, (x_ref, land_ref) in enumerate(zip(x_refs, land_refs)):
        w = i if which is None else which[i]
        R2 = x_ref.shape[0] // 2
        mine = x_ref.at[pl.ds(mc * R2, R2)]
        for j, (px, py) in enumerate(chips):
            dst = land_ref.at[2 * px + py, mc] if inbound else land_ref.at[me, mc]
            cps.append(_rcopy(mine, dst, ssem.at[3 * w + j], rsem.at[3 * w + j], (px, py, mc)))
    return cps


def _ag_start(xs, token, *, name):
    n = len(xs)
    lands = [pltpu.with_memory_space_constraint(lax.empty((N_CHIPS, 2, x.shape[0] // 2, x.shape[1]), x.dtype), pltpu.HBM) for x in xs]
    xs = [pltpu.with_memory_space_constraint(x, pltpu.HBM) for x in xs]

    def body(*refs):
        x_refs, land_refs, tok_ref = refs[:n], refs[n:2 * n], refs[2 * n]
        ssem, rsem, tok_out = refs[2 * n + 1], refs[2 * n + 2], refs[-1]
        for cp in _ag_copies(x_refs, land_refs, ssem, rsem, False):
            cp.start()
        tok_out[...] = tok_ref[...]

    outs = pl.pallas_call(
        body, name=name,
        out_shape=(pltpu.SemaphoreType.DMA((3 * n,)), pltpu.SemaphoreType.DMA((3 * n,)),
                   *[pltpu.HBM(x.shape, x.dtype) for x in xs], *[pltpu.HBM(t.shape, t.dtype) for t in lands],
                   jax.ShapeDtypeStruct(token.shape, token.dtype)),
        in_specs=[_HBM] * (2 * n) + [_VM], out_specs=(_SEMS, _SEMS, *[_HBM] * (2 * n), _VM),
        input_output_aliases={i: 2 + i for i in range(2 * n)},
        compiler_params=pltpu.CompilerParams(has_side_effects=_DATAFLOW),
    )(*xs, *lands, token)
    return outs[0], outs[1], list(outs[2:2 + n]), list(outs[2 + n:2 + 2 * n]), outs[-1]


def _ag_wait(ssem, rsem, xs, lands, after, which, *, name):
    n = len(xs)

    def body(*refs):
        x_refs, land_refs, ssem_ref, rsem_ref = refs[:n], refs[n:2 * n], refs[2 * n], refs[2 * n + 1]
        for cp in _ag_copies(x_refs, land_refs, ssem_ref, rsem_ref, True, which):
            cp.wait_send()
            cp.wait_recv()

    after = list(after) if isinstance(after, (list, tuple)) else [after]
    outs = pl.pallas_call(
        body, name=name,
        out_shape=(*[pltpu.HBM(x.shape, x.dtype) for x in xs], *[pltpu.HBM(t.shape, t.dtype) for t in lands]),
        in_specs=[_HBM] * (2 * n) + [_SEMS, _SEMS] + [_ANY] * len(after), out_specs=tuple([_HBM] * (2 * n)),
        input_output_aliases={i: i for i in range(2 * n)},
        compiler_params=pltpu.CompilerParams(has_side_effects=_DATAFLOW),
    )(*xs, *lands, ssem, rsem, *after)
    return list(outs[:n]), list(outs[n:])


def _ag_finish(lands, xs, *, name):
    n = len(xs)

    def body(*refs):
        x_refs, o_refs, (ssem, rsem) = refs[n:2 * n], refs[2 * n:3 * n], refs[3 * n:]
        mx, my, mc, chips = _place()
        me = 2 * mx + my
        sib = (mx, my, 1 - mc)
        passed = []
        for w in range(n):
            for j, (px, py) in enumerate(chips):
                got = o_refs[w].at[2 * px + py, mc]
                passed.append(_rcopy(got, got, ssem.at[3 * w + j], rsem.at[3 * w + j], sib))
        for cp in passed:
            cp.start()
        for w in range(n):
            R, C = x_refs[w].shape
            R2 = R // 2
            ch = _rows_tile(R2, max(8, STAGE_BYTES // (C * x_refs[w].dtype.itemsize)))
            pairs = [(x_refs[w].at[pl.ds(h * R2 + r, ch)], o_refs[w].at[me, h, pl.ds(r, ch)]) for h in range(2) for r in range(0, R2, ch)]
            pl.run_scoped(functools.partial(_staged_copy, pairs), pltpu.VMEM((2, ch, C), x_refs[w].dtype),
                          pltpu.SemaphoreType.DMA((2,)), pltpu.SemaphoreType.DMA((2,)))
        for w in range(n):
            for j, (px, py) in enumerate(chips):
                theirs = o_refs[w].at[2 * px + py, 1 - mc]
                _rcopy(theirs, theirs, ssem.at[3 * w + j], rsem.at[3 * w + j], sib).wait_recv()
        for cp in passed:
            cp.wait_send()

    return pl.pallas_call(
        body, in_specs=[_ANY] * (2 * n), out_specs=[_ANY] * n,
        out_shape=[jax.ShapeDtypeStruct(t.shape, t.dtype) for t in lands],
        input_output_aliases={i: i for i in range(n)},
        scratch_shapes=[pltpu.SemaphoreType.DMA((3 * n,)), pltpu.SemaphoreType.DMA((3 * n,))],
        name=name)(*lands, *xs)


def _ag4(x, *, name):
    def body(x_ref, o_ref, ssem, rsem, lsem):
        mx, my, mc, chips = _place()
        me = 2 * mx + my
        loc = pltpu.make_async_copy(x_ref, o_ref.at[me], lsem)
        loc.start()
        sends = [_rcopy(x_ref, o_ref.at[me], ssem.at[j], rsem.at[j], (px, py, mc)) for j, (px, py) in enumerate(chips)]
        for cp in sends:
            cp.start()
        for j, (px, py) in enumerate(chips):
            _rcopy(x_ref, o_ref.at[2 * px + py], ssem.at[j], rsem.at[j], (px, py, mc)).wait_recv()
        for cp in sends:
            cp.wait_send()
        loc.wait()

    return pl.pallas_call(
        body, in_specs=[_ANY], out_specs=_ANY, out_shape=jax.ShapeDtypeStruct((N_CHIPS,) + x.shape, x.dtype),
        scratch_shapes=[pltpu.SemaphoreType.DMA((3,)), pltpu.SemaphoreType.DMA((3,)), pltpu.SemaphoreType.DMA(())],
        name=name)(x)


def _rs_sib(gs, after=(), *, name):
    n, na = len(gs), len(after)

    def body(*refs):
        g_refs, t_refs, (ssem, rsem) = refs[:n], refs[n + na:2 * n + na], refs[2 * n + na:]
        mx, my, mc, _ = _place()
        sib = (mx, my, 1 - mc)
        sends = []
        for w in range(n):
            R2 = g_refs[w].shape[1] // 2
            for k in range(N_CHIPS):
                sends.append(_rcopy(g_refs[w].at[k, pl.ds((1 - mc) * R2, R2)], t_refs[w].at[k], ssem.at[N_CHIPS * w + k],
                                    rsem.at[N_CHIPS * w + k], sib))
        for cp in sends:
            cp.start()
        for w in range(n):
            for k in range(N_CHIPS):
                t = t_refs[w].at[k]
                _rcopy(t, t, ssem.at[N_CHIPS * w + k], rsem.at[N_CHIPS * w + k], sib).wait_recv()
        for cp in sends:
            cp.wait_send()

    return pl.pallas_call(
        body, in_specs=[_ANY] * (n + na), out_specs=[_ANY] * n,
        out_shape=[jax.ShapeDtypeStruct((N_CHIPS, g.shape[1] // 2, g.shape[2]), g.dtype) for g in gs],
        scratch_shapes=[pltpu.SemaphoreType.DMA((N_CHIPS * n,)), pltpu.SemaphoreType.DMA((N_CHIPS * n,))],
        name=name)(*gs, *after)


def _a2a_copies(h_refs, got_refs, ssem, rsem, inbound):
    mx, my, mc, chips = _place()
    me = 2 * mx + my
    cps = []
    for w, (h_ref, got_ref) in enumerate(zip(h_refs, got_refs)):
        for j, (px, py) in enumerate(chips):
            dst = got_ref.at[2 * px + py] if inbound else got_ref.at[me]
            cps.append(_rcopy(h_ref.at[2 * px + py], dst, ssem.at[3 * w + j], rsem.at[3 * w + j], (px, py, mc)))
    return cps


def _a2a_start(hs, *, name):
    n = len(hs)
    gots = [pltpu.with_memory_space_constraint(lax.empty(h.shape, h.dtype), pltpu.HBM) for h in hs]
    hs = [pltpu.with_memory_space_constraint(h, pltpu.HBM) for h in hs]

    def body(*refs):
        h_refs, got_refs = refs[:n], refs[n:2 * n]
        ssem, rsem, tok_out = refs[2 * n], refs[2 * n + 1], refs[-1]
        for cp in _a2a_copies(h_refs, got_refs, ssem, rsem, False):
            cp.start()
        tok_out[...] = jnp.zeros_like(tok_out)

    outs = pl.pallas_call(
        body, name=name,
        out_shape=(pltpu.SemaphoreType.DMA((3 * n,)), pltpu.SemaphoreType.DMA((3 * n,)),
                   *[pltpu.HBM(h.shape, h.dtype) for h in hs], *[pltpu.HBM(h.shape, h.dtype) for h in hs],
                   jax.ShapeDtypeStruct((8, 128), F32)),
        in_specs=[_HBM] * (2 * n), out_specs=(_SEMS, _SEMS, *[_HBM] * (2 * n), _VM),
        input_output_aliases={i: 2 + i for i in range(2 * n)},
        compiler_params=pltpu.CompilerParams(has_side_effects=_DATAFLOW),
    )(*hs, *gots)
    return outs[0], outs[1], list(outs[2:2 + n]), list(outs[2 + n:2 + 2 * n]), outs[-1]


def _a2a_wait(ssem, rsem, hs, gots, after, *, name):
    n = len(hs)

    def body(*refs):
        h_refs, got_refs, ssem_ref, rsem_ref = refs[:n], refs[n:2 * n], refs[2 * n], refs[2 * n + 1]
        for cp in _a2a_copies(h_refs, got_refs, ssem_ref, rsem_ref, True):
            cp.wait_send()
            cp.wait_recv()

    after = list(after) if isinstance(after, (list, tuple)) else [after]
    outs = pl.pallas_call(
        body, name=name,
        out_shape=tuple(pltpu.HBM(h.shape, h.dtype) for h in hs + gots),
        in_specs=[_HBM] * (2 * n) + [_SEMS, _SEMS] + [_ANY] * len(after), out_specs=tuple([_HBM] * (2 * n)),
        input_output_aliases={i: i for i in range(2 * n)},
        compiler_params=pltpu.CompilerParams(has_side_effects=_DATAFLOW),
    )(*hs, *gots, ssem, rsem, *after)
    return list(outs[:n]), list(outs[n:])


def _sib_fill(bufs, after=(), *, name):
    n, na = len(bufs), len(after)

    def body(*refs):
        o_refs, (ssem, rsem) = refs[n + na:2 * n + na], refs[2 * n + na:]
        mx, my, mc, _ = _place()
        sib = (mx, my, 1 - mc)
        sends = [_rcopy(o_refs[w].at[mc], o_refs[w].at[mc], ssem.at[w], rsem.at[w], sib) for w in range(n)]
        for cp in sends:
            cp.start()
        for w in range(n):
            t = o_refs[w].at[1 - mc]
            _rcopy(t, t, ssem.at[w], rsem.at[w], sib).wait_recv()
        for cp in sends:
            cp.wait_send()

    return pl.pallas_call(
        body, in_specs=[_ANY] * (n + na), out_specs=[_ANY] * n, out_shape=[jax.ShapeDtypeStruct(b.shape, b.dtype) for b in bufs],
        input_output_aliases={i: i for i in range(n)},
        scratch_shapes=[pltpu.SemaphoreType.DMA((n,)), pltpu.SemaphoreType.DMA((n,))], name=name)(*bufs, *after)


def _rows_tile(n, pref=512):
    t = min(n, pref)
    while n % t or (t % 8 and t != n):
        t -= 1
    return t


def _rs_add(g, theirs, c_arr, payload, *, name):
    _, R, C = g.shape
    R2 = R // 2
    tr = _rows_tile(R2, 256)
    nb = R2 // tr

    def body(c_ref, g_ref, t_ref, o_ref):
        o_ref[...] = (g_ref[...].astype(F32) + t_ref[...].astype(F32)).astype(o_ref.dtype)

    blk = pl.BlockSpec((None, tr, C), lambda k, i, c: (k, i, 0))
    return pl.pallas_call(
        body,
        grid_spec=pltpu.PrefetchScalarGridSpec(
            num_scalar_prefetch=1, grid=(N_CHIPS, nb),
            in_specs=[pl.BlockSpec((None, tr, C), lambda k, i, c: (k, c[0] * nb + i, 0)), blk], out_specs=blk),
        out_shape=jax.ShapeDtypeStruct((N_CHIPS, R2, C), payload),
        compiler_params=_cparams(("arbitrary", "arbitrary")), name=name)(c_arr, g, theirs)


def _rs_sum(chip_sum, got, mc_arr, *, name):
    _, R2, C = chip_sum.shape
    tr = _rows_tile(R2, 256)

    def body(s_ref, own_ref, g0, g1, g2, g3, o_ref):
        me = s_ref[0]
        acc = jnp.zeros((tr, C), F32)
        for k, g_ref in enumerate((g0, g1, g2, g3)):
            acc = acc + jnp.where(me == k, own_ref[...], g_ref[...]).astype(F32)
        o_ref[...] = acc

    def got_spec(k):
        return pl.BlockSpec((None, tr, C), lambda i, s: (jnp.where(s[0] == k, (k + 1) % N_CHIPS, k), i, 0))

    return pl.pallas_call(
        body,
        grid_spec=pltpu.PrefetchScalarGridSpec(
            num_scalar_prefetch=1, grid=(R2 // tr,),
            in_specs=[pl.BlockSpec((None, tr, C), lambda i, s: (s[0], i, 0))] + [got_spec(k) for k in range(N_CHIPS)],
            out_specs=pl.BlockSpec((None, tr, C), lambda i, s: (s[1], i, 0))),
        out_shape=jax.ShapeDtypeStruct((2, R2, C), F32),
        compiler_params=_cparams(("arbitrary",)), name=name)(mc_arr, chip_sum, got, got, got, got)


def _adamw(w, m, v, gs, *, name, first=0, prev=None):
    _, _, R2, C = w.shape
    nl = len(gs)
    tr = _rows_tile(R2, 256)
    c1 = 1.0 / (1.0 - ADAM_B1 ** ADAM_STEP)
    c2 = 1.0 / (1.0 - ADAM_B2 ** ADAM_STEP)

    def body(w_ref, m_ref, v_ref, *refs):
        g_refs, (go_ref, d_ref, mo_ref, vo_ref) = refs[:nl], refs[-4:]
        l = pl.program_id(0)
        for ll in range(nl):
            @pl.when(l == ll)
            def _(ll=ll):
                gv = g_refs[ll][...]
                mn = ADAM_B1 * m_ref[...] + (1.0 - ADAM_B1) * gv
                vn = ADAM_B2 * v_ref[...] + (1.0 - ADAM_B2) * (gv * gv)
                go_ref[...] = gv
                mo_ref[...] = mn
                vo_ref[...] = vn
                d_ref[...] = -ADAM_LR * ((mn * c1) / (jnp.sqrt(vn * c2) + ADAM_EPS) + ADAM_WD * w_ref[...])

    def g_spec(ll):
        return pl.BlockSpec((None, tr, C), lambda l, h, i: (jnp.where(l == ll, h, 0), jnp.where(l == ll, i, 0), 0))

    ws = pl.BlockSpec((None, None, tr, C), lambda l, h, i: (l + first, h, i, 0))
    shp = jax.ShapeDtypeStruct(w.shape, F32)
    prev = list(prev) if prev is not None else []
    return pl.pallas_call(
        body, grid=(nl, 2, R2 // tr), in_specs=[ws, ws, ws] + [g_spec(ll) for ll in range(nl)] + [_ANY] * len(prev),
        out_specs=[ws] * 4, out_shape=[shp] * 4,
        input_output_aliases={3 + nl + k: k for k in range(len(prev))},
        compiler_params=_cparams(("arbitrary", "arbitrary", "arbitrary")), name=name)(w, m, v, *gs, *prev)


class _GradReducer:
    def __init__(self, tag, payload):
        self.tag, self.payload = tag, payload
        self.me = (2 * lax.axis_index("x") + lax.axis_index("y")).astype(jnp.int32)
        self.c = lax.axis_index("c").astype(jnp.int32)

    def start(self, names, gs, after=()):
        theirs = _rs_sib(gs, after, name=f"rs_sib_{self.tag}")
        hs = [_rs_add(g, t, self.c.reshape(1), self.payload, name=f"rs_add_{n}") for n, g, t in zip(names, gs, theirs)]
        self.names = names
        self.ssem, self.rsem, self.hs, self.gots, token = _a2a_start(hs, name=f"rs_a2a_start_{self.tag}")
        return token

    def finish(self, after):
        hs, gots = _a2a_wait(self.ssem, self.rsem, self.hs, self.gots, after, name=f"rs_a2a_wait_{self.tag}")
        mc = jnp.stack([self.me, self.c])
        return {n: _rs_sum(h, g, mc, name=f"rs_sum_{n}") for n, h, g in zip(self.names, hs, gots)}


WEIGHTS = ["rel_bias", "mem_norm_g", "norm_mix_g", "w_in", "s5_lam_re", "s5_lam_im", "s5_log_dt", "s5_b_re", "s5_b_im",
           "s5_c_re", "s5_c_im", "s5_d", "s5_w_glu", "pool_w", "pool_scale", "conv_w_dw", "conv_b_dw", "conv_ln_g",
           "conv_ln_b", "conv_w_pw", "grp_norm_g", "w_out", "norm_x_g", "w_xq", "w_xk", "w_xv", "w_xo", "norm_mlp_g",
           "w_up", "w_down", "norm_final_g"]
SHARDED = {"w_in": "col", "s5_w_glu": "row", "conv_w_dw": "col", "conv_w_pw": "row", "w_out": "row", "w_xq": "row",
           "w_xk": "row", "w_xv": "row", "w_xo": "col", "w_up": "col", "w_down": "row"}
AG_FIRST = ("w_in", "s5_w_glu", "conv_w_dw", "conv_w_pw")
SMALL = [n for n in WEIGHTS if n not in SHARDED]
SMALL_ALIGN = N_CHIPS * 2 * 256 * 128


def _mat(w, kind):
    return w.reshape(w.shape[0] * w.shape[1], w.shape[2]) if kind == "row" else w


def _att_bias(rel_bias):
    bucket, valid = _att_tables()
    onehot = (jnp.asarray(bucket.reshape(-1))[:, None] == jnp.arange(REL_BUCKETS)[None, :]).astype(F32)
    b = jnp.dot(onehot, rel_bias, precision=lax.Precision.HIGHEST).reshape(bucket.shape + (rel_bias.shape[1],))
    return jnp.where(jnp.asarray(valid)[:, None], jnp.transpose(b, (0, 3, 1, 2)), NEG_INF)


def _rel_bias_grad(dbias):
    bucket, _ = _att_tables()
    nb, H = dbias.shape[0], dbias.shape[1]
    onehot = (jnp.asarray(bucket.reshape(-1))[:, None] == jnp.arange(REL_BUCKETS)[None, :]).astype(BF16)
    flat = jnp.transpose(dbias.reshape(nb, H, -1), (1, 0, 2)).reshape(H, -1)
    return _mm(flat, onehot, "nn", tk=16384, name="rel_bias_grad").T


def _layer_fwd(h, mem_n, bias, sp, bw, l):
    L, D = h.shape
    GW = D // N_MIXERS
    G = GW // S5_CH
    E = GW // ATT_HEADS
    sv = {"h": h}
    xn = _rms_fwd(h, sp["norm_mix_g"][l][None], name="norm_mix")
    proj = _mm(xn, bw["w_in"], "nn", name="proj_in")
    sv.update(xn=xn, proj=proj)
    disc, disc_vjp = jax.vjp(_s5_disc, sp["s5_lam_re"][l], sp["s5_lam_im"][l], sp["s5_log_dt"][l], sp["s5_b_re"][l], sp["s5_b_im"][l])
    ab_r, ab_i, bb_r, bb_i = disc
    s5 = dict(
        bdr=_bd(jnp.transpose(bb_r, (0, 2, 1))).astype(BF16), bdi=_bd(jnp.transpose(bb_i, (0, 2, 1))).astype(BF16),
        cdr=_bd(jnp.transpose(sp["s5_c_re"][l], (0, 2, 1))).astype(BF16), cdi=_bd(jnp.transpose(sp["s5_c_im"][l], (0, 2, 1))).astype(BF16),
        d=sp["s5_d"][l][None], wglu=bw["s5_w_glu"], ab=(ab_r.reshape(-1), ab_i.reshape(-1)), vjp=disc_vjp)
    pw, tr, ti = _cpow_tables(*s5["ab"])
    y_a, xr, xi = _s5_fwd(proj, s5["bdr"], s5["bdi"], pw, tr, ti, s5["cdr"], s5["cdi"], s5["d"], s5["wglu"], name="s5_fwd")
    sv.update(s5=s5, xr=xr, xi=xi, y_a=y_a)
    pool_s = sp["pool_scale"][l].reshape(len(POOL_WINDOWS), 1, -1)
    y_b = _pool_fwd(proj, sp["pool_w"][l], pool_s, name="pool_fwd")
    sv.update(y_b=y_b, pool_s=pool_s)
    hc = _conv1_fwd(proj, bw["conv_w_dw"], sp["conv_b_dw"][l][None], name="conv_dw_fwd")
    y_c = _conv2_fwd(hc, sp["conv_ln_g"][l][None], sp["conv_ln_b"][l][None], bw["conv_w_pw"], name="conv_pw_fwd")
    sv.update(hc=hc, y_c=y_c)
    y_d, lse_d = _datt_fwd(proj, bias, name="att_fwd")
    sv.update(y_d=y_d, lse_d=lse_d)
    yn = _grp_fwd([y_a, y_b, y_c, y_d], sp["grp_norm_g"][l][None], name="grp_fwd")
    bw.rest(yn)
    h1 = _mm(yn, bw["w_out"], "nn", res=h, name="proj_out")
    sv.update(yn=yn, h1=h1)
    hx = _rms_fwd(h1, sp["norm_x_g"][l][None], name="norm_x")
    q_x = _mm(hx, bw["w_xq"], "nn", out_dtype=BF16, name="xq")
    k_x = _mm(mem_n, bw["w_xk"], "nn", out_dtype=BF16, name="xk")
    v_x = _mm(mem_n, bw["w_xv"], "nn", out_dtype=BF16, name="xv")
    o_x = _xatt_fwd(q_x, k_x, v_x, name="xatt_fwd")
    h2 = _mm(o_x, bw["w_xo"], "nn", res=h1, name="xo")
    sv.update(hx=hx, q_x=q_x, k_x=k_x, v_x=v_x, o_x=o_x, h2=h2)
    hm = _rms_fwd(h2, sp["norm_mlp_g"][l][None], name="norm_mlp")
    up, act = _mm(hm, bw["w_up"], "nn", epi="relu2", out_dtype=BF16, name="mlp_up")
    h3 = _mm(act, bw["w_down"], "nn", res=h2, name="mlp_down")
    sv.update(hm=hm, up=up, act=act)
    return h3, sv


def _stack_rows(g):
    return g.reshape(N_CHIPS, g.shape[0] // N_CHIPS, g.shape[1])


def _layer_bwd(dh3, d_memn, mem_n, bias, sp, bw, sv, l):
    L, D = dh3.shape
    GW = D // N_MIXERS
    G = GW // S5_CH
    E = GW // ATT_HEADS
    gs, gb = {}, {}
    d_up = _mm(dh3, bw["w_down"], "nt", epi="relu2_bwd", aux=sv["up"], out_dtype=BF16, name="mlp_down_dx")
    gb["w_down"] = _stack_rows(_mm(sv["act"], dh3, "tn", out_dtype=BF16, name="mlp_down_dw"))
    gb["w_up"] = _mm(sv["hm"], d_up, "tn", out_dtype=BF16, out_stack=N_CHIPS, name="mlp_up_dw")
    d_hm = _mm(d_up, bw["w_up"], "nt", name="mlp_up_dx")
    dh2, gs["norm_mlp_g"] = _rms_bwd(sv["h2"], sp["norm_mlp_g"][l][None], d_hm, dh3, name="norm_mlp_bwd")
    d_ox = _mm(dh2, bw["w_xo"], "nt", name="xo_dx")
    gb["w_xo"] = _mm(sv["o_x"], dh2, "tn", out_dtype=BF16, out_stack=N_CHIPS, name="xo_dw")
    dq_x, dk_x, dv_x = _xatt_bwd(sv["q_x"], sv["k_x"], sv["v_x"], d_ox, name="xatt_bwd")
    gb["w_xq"] = _stack_rows(_mm(sv["hx"], dq_x, "tn", out_dtype=BF16, name="xq_dw"))
    gb["w_xk"] = _stack_rows(_mm(mem_n, dk_x, "tn", out_dtype=BF16, name="xk_dw"))
    gb["w_xv"] = _stack_rows(_mm(mem_n, dv_x, "tn", out_dtype=BF16, name="xv_dw"))
    d_memn = _mm(dk_x, bw["w_xk"], "nt", res=d_memn, name="xk_dx")
    d_memn = _mm(dv_x, bw["w_xv"], "nt", res=d_memn, name="xv_dx")
    d_hx = _mm(dq_x, bw["w_xq"], "nt", name="xq_dx")
    dh1, gs["norm_x_g"] = _rms_bwd(sv["h1"], sp["norm_x_g"][l][None], d_hx, dh2, name="norm_x_bwd")
    d_yn = _mm(dh1, bw["w_out"], "nt", name="proj_out_dx")
    gb["w_out"] = _stack_rows(_mm(sv["yn"], dh1, "tn", out_dtype=BF16, name="proj_out_dw"))
    ys = [sv["y_a"], sv["y_b"], sv["y_c"], sv["y_d"]]
    dya, dyb, dyc, dyd, gs["grp_norm_g"] = _grp_bwd(ys, sp["grp_norm_g"][l][None], d_yn, name="grp_bwd")
    dq, dk, dv, dbias = _datt_bwd(sv["proj"], bias, sv["y_d"], sv["lse_d"], dyd, name="att_bwd")
    dhc, gs["conv_ln_g"], gs["conv_ln_b"], g_pw = _conv2_bwd(sv["hc"], sp["conv_ln_g"][l][None], sp["conv_ln_b"][l][None],
                                                               bw["conv_w_pw"], dyc, name="conv_pw_bwd")
    gb["conv_w_pw"] = _stack_rows(g_pw)
    dval, dgate, g_dw, gs["conv_b_dw"] = _conv1_bwd(sv["proj"], dhc, bw["conv_w_dw"], name="conv_dw_bwd")
    gb["conv_w_dw"] = jnp.transpose(g_dw.reshape(CONV_PAD, N_CHIPS, GW // N_CHIPS), (1, 0, 2))
    du_b, gs["pool_w"], g_ps = _pool_bwd(sv["proj"], dyb, sp["pool_w"][l], sv["pool_s"], name="pool_bwd")
    gs["pool_scale"] = g_ps.reshape(-1)
    s5 = sv["s5"]
    conj = (s5["ab"][0], -s5["ab"][1])
    pw, tr, ti = _cpow_tables(*conj)
    du_a, g_glu, g_d, dcdr, dcdi, dbdr, dbdi, dar, dai = _s5_bwd(
        sv["proj"], sv["xr"], sv["xi"], dya, s5["bdr"], s5["bdi"], pw, tr[::-1], ti[::-1], s5["cdr"], s5["cdi"], s5["d"], s5["wglu"],
        name="s5_bwd")
    gb["s5_w_glu"] = _stack_rows(g_glu)
    gs["s5_d"] = g_d.reshape(-1)
    gs["s5_c_re"] = jnp.transpose(_bd_extract(dcdr, G), (0, 2, 1))
    gs["s5_c_im"] = jnp.transpose(_bd_extract(dcdi, G), (0, 2, 1))
    d_bb_r = jnp.transpose(_bd_extract(dbdr, G), (0, 2, 1))
    d_bb_i = jnp.transpose(_bd_extract(dbdi, G), (0, 2, 1))
    d_ab_r = jnp.sum(dar, axis=0).reshape(G, S5_STATE)
    d_ab_i = jnp.sum(dai, axis=0).reshape(G, S5_STATE)
    gs["s5_lam_re"], gs["s5_lam_im"], gs["s5_log_dt"], gs["s5_b_re"], gs["s5_b_im"] = s5["vjp"]((d_ab_r, d_ab_i, d_bb_r, d_bb_i))
    d_proj = jnp.concatenate([du_a, du_b, dval, dgate, dq, dk, dv], axis=1)
    gb["w_in"] = _mm(sv["xn"], d_proj, "tn", out_dtype=BF16, out_stack=N_CHIPS, name="proj_in_dw")
    d_xn = _mm(d_proj, bw["w_in"], "nt", name="proj_in_dx")
    dh0, gs["norm_mix_g"] = _rms_bwd(sv["h"], sp["norm_mix_g"][l][None], d_xn, dh1, name="norm_mix_bwd")
    gs = {n: g.reshape(sp[n].shape[1:]) for n, g in gs.items()}
    return dh0, d_memn, dbias, gs, gb


def _device_step(x, mem, target, sp, get_big, on_grads):
    nl = sp["norm_mix_g"].shape[0]
    mem_n = _rms_fwd(mem, sp["mem_norm_g"][None], name="norm_mem")
    bias = _att_bias(sp["rel_bias"])
    h, saved, big = x, [], {n: [] for n in SHARDED}
    for l in range(nl):
        bw = get_big(l, h)
        h, sv = _layer_fwd(h, mem_n, bias, sp, bw, l)
        saved.append(sv)
        for n in SHARDED:
            big[n].append(bw[n])
    loss, dh, g_final = _loss_head(h, sp["norm_final_g"][None], target, name="loss_head")
    d_memn = jnp.zeros(mem.shape, F32)
    dbias = jnp.zeros(bias.shape, F32)
    sg = {n: [None] * nl for n in SMALL}
    for l in reversed(range(nl)):
        dh, d_memn, dbias_l, gs, gb = _layer_bwd(dh, d_memn, mem_n, bias, sp, {n: big[n][l] for n in SHARDED}, saved[l], l)
        dbias = dbias + dbias_l
        for n, g in gs.items():
            sg[n][l] = g
        dh = on_grads(l, gb, dh)
    small = {n: jnp.stack(g) for n, g in sg.items() if g[0] is not None}
    small["norm_final_g"] = g_final.reshape(-1)
    _, g_mem = _rms_bwd(mem, sp["mem_norm_g"][None], d_memn, None, name="norm_mem_bwd")
    small["mem_norm_g"] = g_mem.reshape(-1)
    small["rel_bias"] = _rel_bias_grad(dbias)
    return loss, dh, small


def _gather_big(shards, prep):
    nl = shards["w_in"].shape[0]
    token = jnp.zeros((8, 128), F32)
    pending = []
    for l in range(nl):
        xs = [jnp.pad(shards[n][l], ((0, CONV_PAD - CONV_WIDTH), (0, 0))) if n == "conv_w_dw" else shards[n][l].astype(BF16)
              for n in SHARDED]
        ssem, rsem, xs, lands, token = _ag_start(xs, token, name=f"ag_start_l{l}")
        pending.append((ssem, rsem, xs, lands))

    names = list(SHARDED)

    class LayerWeights(dict):
        def __init__(self, l, after):
            super().__init__()
            self.l, self.state = l, pending[l]
            first = [i for i, n in enumerate(names) if n in AG_FIRST] if l == 0 else list(range(len(names)))
            self.left = [i for i in range(len(names)) if i not in first]
            self._complete(first, [token] + list(prep) if l == 0 else [after], "a")

        def _complete(self, which, after, tag):
            ssem, rsem, xs, lands = self.state
            xs, lands = _ag_wait(ssem, rsem, [xs[i] for i in which], [lands[i] for i in which], after, which,
                                 name=f"ag_wait_l{self.l}{tag}")
            fulls = _ag_finish(lands, xs, name=f"ag_finish_{tag if self.l == 0 else 'all'}")
            for i, x, full in zip(which, xs, fulls):
                n = names[i]
                full = full.reshape(N_CHIPS, x.shape[0], x.shape[1])
                self[n] = jnp.transpose(full, (1, 0, 2)).reshape(CONV_PAD, -1) if n == "conv_w_dw" else _mat(full, SHARDED[n])

        def rest(self, after):
            if self.left:
                self._complete(self.left, [after], "b")
                self.left = []

    return LayerWeights


def _pack_small(vals, extra=None):
    flat = [vals[n].reshape(-1).astype(F32) for n in SMALL]
    flat.append(jnp.zeros((1,), F32) if extra is None else extra.reshape(1))
    v = jnp.concatenate(flat)
    n_pad = -(-v.shape[0] // SMALL_ALIGN) * SMALL_ALIGN
    return jnp.pad(v, (0, n_pad - v.shape[0]))


def _unpack_small(flat, like):
    out, pos = {}, 0
    for n in SMALL:
        size = math.prod(like[n].shape)
        out[n] = flat[pos:pos + size].reshape(like[n].shape)
        pos += size
    return out, flat[pos]


def kernel(x, mem, rel_bias, mem_norm_g, norm_mix_g, w_in, s5_lam_re, s5_lam_im, s5_log_dt, s5_b_re, s5_b_im, s5_c_re, s5_c_im, s5_d, s5_w_glu, pool_w, pool_scale, conv_w_dw, conv_b_dw, conv_ln_g, conv_ln_b, conv_w_pw, grp_norm_g, w_out, norm_x_g, w_xq, w_xk, w_xv, w_xo, norm_mlp_g, w_up, w_down, norm_final_g, loss_target, m_rel_bias, m_mem_norm_g, m_norm_mix_g, m_w_in, m_s5_lam_re, m_s5_lam_im, m_s5_log_dt, m_s5_b_re, m_s5_b_im, m_s5_c_re, m_s5_c_im, m_s5_d, m_s5_w_glu, m_pool_w, m_pool_scale, m_conv_w_dw, m_conv_b_dw, m_conv_ln_g, m_conv_ln_b, m_conv_w_pw, m_grp_norm_g, m_w_out, m_norm_x_g, m_w_xq, m_w_xk, m_w_xv, m_w_xo, m_norm_mlp_g, m_w_up, m_w_down, m_norm_final_g, v_rel_bias, v_mem_norm_g, v_norm_mix_g, v_w_in, v_s5_lam_re, v_s5_lam_im, v_s5_log_dt, v_s5_b_re, v_s5_b_im, v_s5_c_re, v_s5_c_im, v_s5_d, v_s5_w_glu, v_pool_w, v_pool_scale, v_conv_w_dw, v_conv_b_dw, v_conv_ln_g, v_conv_ln_b, v_conv_w_pw, v_grp_norm_g, v_w_out, v_norm_x_g, v_w_xq, v_w_xk, v_w_xv, v_w_xo, v_norm_mlp_g, v_w_up, v_w_down, v_norm_final_g):
    env = dict(locals())
    w = {n: env[n] for n in WEIGHTS}
    m = {n: env["m_" + n] for n in WEIGHTS}
    v = {n: env["v_" + n] for n in WEIGHTS}
    sp = {n: w[n] for n in SMALL}
    small_wmv = [_pack_small(t).reshape(1, 2, -1, 128) for t in (w, m, v)]
    get_big = _gather_big({n: w[n] for n in SHARDED}, small_wmv)
    nl = w["w_in"].shape[0]
    names = list(SHARDED)
    reducers, reduced, tokens = {}, {}, {}

    def on_grads(l, gb, dh):
        reducers[l] = _GradReducer(f"l{l}", BF16)
        token = tokens[l] = reducers[l].start(names, [gb[n] for n in names])
        if l + 1 in reducers:
            reduced[l + 1] = reducers[l + 1].finish(dh)
        return dh + token[0, 0]

    loss, grad_x, g_small = _device_step(x[0], mem[0], loss_target[0], sp, get_big, on_grads)

    def shard_views(n):
        pad = ((0, 0), (0, CONV_PAD - CONV_WIDTH), (0, 0)) if n == "conv_w_dw" else None
        wmv = [jnp.pad(t[n], pad) if pad else t[n] for t in (w, m, v)]
        _, R, C = wmv[0].shape
        return [t.reshape(nl, 2, R // 2, C) for t in wmv]

    grad, delta, new_m, new_v = {}, {}, {}, {}
    busy, upper = [grad_x], {}
    if nl > 1:
        filled = _sib_fill([reduced[l][n] for n in names for l in range(1, nl)], [tokens[0]], name="rs_fill_upper")
        for i, n in enumerate(names):
            upper[n] = _adamw(*shard_views(n), filled[i * (nl - 1):(i + 1) * (nl - 1)], first=1, name=f"adamw_upper_{n}")
            busy.append(upper[n][0])
    reduced[0] = reducers[0].finish(busy)
    packed = _pack_small(g_small, loss).reshape(N_CHIPS, -1, 128)
    small = _GradReducer("small", F32)
    token = small.start(["small"], [packed], after=[reduced[0][names[0]]])
    quarter = _sib_fill([small.finish(token)["small"]], name="rs_fill_small")[0]
    total = _ag4(quarter.reshape(-1, 128), name="ag_small").reshape(2, -1, 128)
    outs = _adamw(*small_wmv, [total], name="adamw_small")
    filled = _sib_fill([reduced[0][n] for n in names], name="rs_fill_first")
    for i, n in enumerate(names):
        res = _adamw(*shard_views(n), [filled[i]], prev=upper.get(n), name=f"adamw_first_{n}")
        R = w[n].shape[1]
        grad[n], delta[n], new_m[n], new_v[n] = [o.reshape(nl, -1, o.shape[-1])[:, :R] for o in res]
    loss_sum = None
    for o, dst in zip(outs, (grad, delta, new_m, new_v)):
        vals, last = _unpack_small(o.reshape(-1), sp)
        dst.update(vals)
        loss_sum = last if loss_sum is None else loss_sum
    return (loss_sum, grad_x[None], *[grad[n] for n in WEIGHTS], *[delta[n] for n in WEIGHTS],
            *[new_m[n] for n in WEIGHTS], *[new_v[n] for n in WEIGHTS])
```

```python
import functools
import math

import jax
import jax.numpy as jnp
import numpy as np
from jax import lax
from jax.experimental import pallas as pl
from jax.experimental.pallas import tpu as pltpu

F32 = jnp.float32
BF16 = jnp.bfloat16
MESH = pl.DeviceIdType.MESH

N_MIXERS = 4
S5_CH = 16
S5_STATE = 64
POOL_WINDOWS = (2, 4, 8, 16)
CONV_WIDTH = 31
CONV_PAD = 32
ATT_HEADS = 8
ATT_BLOCK = 128
DILATED_PATTERNS = ((128, 1), (512, 4), (2048, 16))
REL_BUCKETS = 32
REL_MAX_DIST = 2048
X_HEADS = 4
X_HEAD_DIM = 128
NORM_EPS = 1e-6
NEG_INF = -1e30
ADAM_LR, ADAM_B1, ADAM_B2, ADAM_EPS, ADAM_WD, ADAM_STEP = 0.001, 0.9, 0.999, 1e-08, 0.01, 10
N_CHIPS = 4
VMEM_LIMIT = 56 * 1024 * 1024


def _cparams(sem=None, vmem=None):
    return pltpu.CompilerParams(dimension_semantics=sem, vmem_limit_bytes=vmem)


def _tile(n, pref):
    if n <= pref:
        return n
    t = pref
    while t >= 128:
        if n % t == 0:
            return t
        t -= 128
    return n


def _spec(arr, tr, tc, rc):
    if arr.ndim == 2:
        return pl.BlockSpec((tr, tc), lambda *g: rc(*g))
    per = arr.shape[2] // tc
    assert arr.shape[2] % tc == 0

    def imap(*g):
        r, c = rc(*g)
        return (c // per, r, c % per)

    return pl.BlockSpec((None, tr, tc), imap)


def _lshape(arr):
    return arr.shape if arr.ndim == 2 else (arr.shape[1], arr.shape[0] * arr.shape[2])


def _mm(a, b, mode, *, name, out_dtype=F32, res=None, epi=None, aux=None, out_stack=None, tm=1024, tn=1024, tk=2048):
    la, lb = _lshape(a), _lshape(b)
    if mode == "nn":
        (M, K), (K2, N) = la, lb
    elif mode == "nt":
        (M, K), (N, K2) = la, lb
    else:
        (K, M), (K2, N) = la, lb
    assert K == K2, (la, lb, mode)
    lim = {"m": M, "n": N // out_stack if out_stack else N, "k": K}
    stacked = [(a, "m" if mode == "tn" else "k"), (b, "k" if mode == "nt" else "n"), (res, "n"), (aux, "n")]
    for arr, dim in stacked:
        if arr is not None and arr.ndim == 3:
            lim[dim] = math.gcd(lim[dim], arr.shape[2])
    tm, tn, tk = _tile(lim["m"], tm), _tile(lim["n"], tn), _tile(lim["k"], tk)
    nk = K // tk
    grid = (M // tm, N // tn, nk)
    a_spec = _spec(a, tk, tm, lambda i, j, k: (k, i)) if mode == "tn" else _spec(a, tm, tk, lambda i, j, k: (i, k))
    b_spec = _spec(b, tn, tk, lambda i, j, k: (j, k)) if mode == "nt" else _spec(b, tk, tn, lambda i, j, k: (k, j))
    dims = {"nn": ((1,), (0,)), "nt": ((1,), (1,)), "tn": ((0,), (0,))}[mode]
    ins, in_specs = [a, b], [a_spec, b_spec]
    for extra in (res, aux):
        if extra is not None:
            ins.append(extra)
            in_specs.append(_spec(extra, tm, tn, lambda i, j, k: (i, j)))
    if out_stack:
        o_shape = jax.ShapeDtypeStruct((out_stack, M, N // out_stack), out_dtype)
    else:
        o_shape = jax.ShapeDtypeStruct((M, N), out_dtype)
    o_spec = _spec(o_shape, tm, tn, lambda i, j, k: (i, j))
    n_out = 2 if epi == "relu2" else 1
    has_res, has_aux = res is not None, aux is not None

    def body(*refs):
        a_ref, b_ref = refs[0], refs[1]
        pos = 2
        res_ref = aux_ref = None
        if has_res:
            res_ref = refs[pos]
            pos += 1
        if has_aux:
            aux_ref = refs[pos]
            pos += 1
        outs = refs[pos:pos + n_out]
        part = lax.dot_general(a_ref[...].astype(BF16), b_ref[...].astype(BF16), (dims, ((), ())), preferred_element_type=F32)
        if nk > 1:
            acc_ref = refs[pos + n_out]
            k = pl.program_id(2)

            @pl.when(k == 0)
            def _():
                acc_ref[...] = part

            @pl.when(k > 0)
            def _():
                acc_ref[...] += part

        @pl.when(pl.program_id(2) == nk - 1)
        def _():
            o = acc_ref[...] if nk > 1 else part
            if has_res:
                o = o + res_ref[...].astype(F32)
            if epi == "relu2":
                outs[0][...] = o.astype(outs[0].dtype)
                r = jnp.maximum(o, 0.0)
                outs[1][...] = (r * r).astype(outs[1].dtype)
            elif epi == "relu2_bwd":
                outs[0][...] = (o * (2.0 * jnp.maximum(aux_ref[...].astype(F32), 0.0))).astype(outs[0].dtype)
            else:
                outs[0][...] = o.astype(outs[0].dtype)

    out = pl.pallas_call(
        body, grid=grid, in_specs=in_specs,
        out_specs=[o_spec] * n_out if n_out > 1 else o_spec,
        out_shape=[o_shape] * n_out if n_out > 1 else o_shape,
        scratch_shapes=[pltpu.VMEM((tm, tn), F32)] if nk > 1 else [],
        compiler_params=_cparams(("parallel", "parallel", "arbitrary"), VMEM_LIMIT), name=name,
    )(*ins)
    return out


def _rms_fwd(x, g, *, name, out_dtype=BF16):
    L, D = x.shape
    tr = _tile(L, 256)

    def body(x_ref, g_ref, o_ref):
        xv = x_ref[...]
        r = lax.rsqrt(jnp.mean(xv * xv, axis=-1, keepdims=True) + NORM_EPS)
        o_ref[...] = (xv * r * g_ref[...]).astype(o_ref.dtype)

    return pl.pallas_call(
        body, grid=(L // tr,),
        in_specs=[pl.BlockSpec((tr, D), lambda i: (i, 0)), pl.BlockSpec((1, D), lambda i: (0, 0))],
        out_specs=pl.BlockSpec((tr, D), lambda i: (i, 0)),
        out_shape=jax.ShapeDtypeStruct((L, D), out_dtype),
        compiler_params=_cparams(("parallel",)), name=name)(x, g)


def _rms_bwd(x, g, dy, dres, *, name):
    L, D = x.shape
    tr = _tile(L, 256)
    has_res = dres is not None

    def body(*refs):
        x_ref, g_ref, dy_ref = refs[:3]
        dres_ref = refs[3] if has_res else None
        dx_ref, dg_ref = refs[-2:]
        xv = x_ref[...]
        dyv = dy_ref[...].astype(F32)
        r = lax.rsqrt(jnp.mean(xv * xv, axis=-1, keepdims=True) + NORM_EPS)
        xh = xv * r
        dyg = dyv * g_ref[...]
        dx = r * (dyg - xh * jnp.mean(dyg * xh, axis=-1, keepdims=True))
        if has_res:
            dx = dx + dres_ref[...]
        dx_ref[...] = dx

        @pl.when(pl.program_id(0) == 0)
        def _():
            dg_ref[...] = jnp.zeros_like(dg_ref)

        dg_ref[...] += jnp.sum(dyv * xh, axis=0, keepdims=True)

    row = pl.BlockSpec((tr, D), lambda i: (i, 0))
    vec = pl.BlockSpec((1, D), lambda i: (0, 0))
    ins = [x, g, dy] + ([dres] if has_res else [])
    return pl.pallas_call(
        body, grid=(L // tr,), in_specs=[row, vec, row] + ([row] if has_res else []),
        out_specs=[row, vec],
        out_shape=[jax.ShapeDtypeStruct((L, D), F32), jax.ShapeDtypeStruct((1, D), F32)],
        compiler_params=_cparams(("arbitrary",)), name=name)(*ins)


def _loss_head(h, g, target, *, name):
    L, D = h.shape
    tr = _tile(L, 256)

    def body(x_ref, g_ref, t_ref, loss_ref, dx_ref, dg_ref):
        xv = x_ref[...]
        r = lax.rsqrt(jnp.mean(xv * xv, axis=-1, keepdims=True) + NORM_EPS)
        xh = xv * r
        err = xh * g_ref[...] - t_ref[...]
        dyv = err * (1.0 / D)
        dyg = dyv * g_ref[...]
        dx_ref[...] = r * (dyg - xh * jnp.mean(dyg * xh, axis=-1, keepdims=True))

        @pl.when(pl.program_id(0) == 0)
        def _():
            dg_ref[...] = jnp.zeros_like(dg_ref)
            loss_ref[...] = jnp.zeros_like(loss_ref)

        dg_ref[...] += jnp.sum(dyv * xh, axis=0, keepdims=True)
        loss_ref[...] += 0.5 * jnp.sum(jnp.sum(err * err, axis=1, keepdims=True), axis=0, keepdims=True) * (1.0 / D)

    row = pl.BlockSpec((tr, D), lambda i: (i, 0))
    vec = pl.BlockSpec((1, D), lambda i: (0, 0))
    return pl.pallas_call(
        body, grid=(L // tr,), in_specs=[row, vec, row],
        out_specs=[pl.BlockSpec((1, 1), lambda i: (0, 0)), row, vec],
        out_shape=[jax.ShapeDtypeStruct((1, 1), F32), jax.ShapeDtypeStruct((L, D), F32), jax.ShapeDtypeStruct((1, D), F32)],
        compiler_params=_cparams(("arbitrary",)), name=name)(h, g, target)


S5_TL = 256
S5_LW = 512


def _dot(a, b, dims):
    return lax.dot_general(a, b, (dims, ((), ())), preferred_element_type=F32)


_NN, _NT, _TN = ((1,), (0,)), ((1,), (1,)), ((0,), (0,))


def _gelu_and_grad(y):
    k = math.sqrt(2.0 / math.pi)
    y2 = y * y
    th = jnp.tanh(k * (y + 0.044715 * y * y2))
    g = 0.5 * y * (1.0 + th)
    gp = 0.5 * (1.0 + th) + 0.5 * y * (1.0 - th * th) * k * (1.0 + 3.0 * 0.044715 * y2)
    return g, gp


def _scan_tiles(re_s, im_s, ls, lw, pw_ref, tr_ref, ti_ref, carry_s, nt, reverse, extra=None):
    row = lax.broadcasted_iota(jnp.int32, (8, lw), 0)
    a = [pw_ref[k:k + 1, ls] for k in range(6)]
    tr_t, ti_t = tr_ref[:, ls], ti_ref[:, ls]
    edge = 0 if reverse else 7

    def tile(jj, carry):
        cr, ci, acc = carry
        j = (nt - 1 - jj) if reverse else jj
        off = pl.multiple_of(j * 8, 8)
        sr, si = re_s[pl.ds(off, 8), ls], im_s[pl.ds(off, 8), ls]
        for n, k in enumerate((1, 2, 4)):
            akr, aki = a[2 * n], a[2 * n + 1]
            if reverse:
                keep, sh = row < 8 - k, 8 - k
            else:
                keep, sh = row >= k, k
            shr = jnp.where(keep, pltpu.roll(sr, sh, 0), 0.0)
            shi = jnp.where(keep, pltpu.roll(si, sh, 0), 0.0)
            sr, si = sr + akr * shr - aki * shi, si + akr * shi + aki * shr
        xr = sr + tr_t * cr - ti_t * ci
        xi = si + tr_t * ci + ti_t * cr
        re_s[pl.ds(off, 8), ls] = xr
        im_s[pl.ds(off, 8), ls] = xi
        if extra is not None:
            acc = extra(off, xr, xi, acc, row)
        return xr[edge:edge + 1, :], xi[edge:edge + 1, :], acc

    acc0 = (jnp.zeros((8, lw), F32), jnp.zeros((8, lw), F32)) if extra is not None else 0
    cr, ci, acc = lax.fori_loop(0, nt, tile, (carry_s[0:1, ls], carry_s[1:2, ls], acc0))
    carry_s[0:1, ls] = cr
    carry_s[1:2, ls] = ci
    return acc


def _s5_fwd(proj, bdr, bdi, pw, tr, ti, cdr, cdi, dskip, wglu, *, name):
    L = proj.shape[0]
    GW, S = bdr.shape
    TL = _tile(L, S5_TL)
    lw = min(S, S5_LW)

    def body(u_ref, bdr_ref, bdi_ref, pw_ref, tr_ref, ti_ref, cdr_ref, cdi_ref, d_ref, w_ref,
             y_ref, xr_ref, xi_ref, re_s, im_s, carry_s):
        @pl.when(pl.program_id(0) == 0)
        def _():
            carry_s[...] = jnp.zeros_like(carry_s)

        u = u_ref[...]
        ub = u.astype(BF16)
        re_s[...] = _dot(ub, bdr_ref[...], _NN)
        im_s[...] = _dot(ub, bdi_ref[...], _NN)
        for lc in range(S // lw):
            _scan_tiles(re_s, im_s, slice(lc * lw, (lc + 1) * lw), lw, pw_ref, tr_ref, ti_ref, carry_s, TL // 8, False)
        xrb, xib = re_s[...].astype(BF16), im_s[...].astype(BF16)
        xr_ref[...] = xrb
        xi_ref[...] = xib
        y = _dot(xrb, cdr_ref[...], _NN) - _dot(xib, cdi_ref[...], _NN) + d_ref[...] * u
        g, _ = _gelu_and_grad(y)
        z = _dot(g.astype(BF16), w_ref[...], _NN)
        y_ref[...] = g * jax.nn.sigmoid(z)

    full = lambda arr: pl.BlockSpec(arr.shape, lambda i: (0,) * arr.ndim)
    return pl.pallas_call(
        body, grid=(L // TL,),
        in_specs=[pl.BlockSpec((TL, GW), lambda i: (i, 0))] + [full(t) for t in (bdr, bdi, pw, tr, ti, cdr, cdi, dskip, wglu)],
        out_specs=[pl.BlockSpec((TL, GW), lambda i: (i, 0)), pl.BlockSpec((TL, S), lambda i: (i, 0)), pl.BlockSpec((TL, S), lambda i: (i, 0))],
        out_shape=[jax.ShapeDtypeStruct((L, GW), F32), jax.ShapeDtypeStruct((L, S), BF16), jax.ShapeDtypeStruct((L, S), BF16)],
        scratch_shapes=[pltpu.VMEM((TL, S), F32), pltpu.VMEM((TL, S), F32), pltpu.VMEM((8, S), F32)],
        compiler_params=_cparams(("arbitrary",), VMEM_LIMIT), name=name,
    )(proj, bdr, bdi, pw, tr, ti, cdr, cdi, dskip, wglu)


def _s5_bwd(proj, xr, xi, dout, bdr, bdi, pwc, trc, tic, cdr, cdi, dskip, wglu, *, name):
    L = proj.shape[0]
    GW, S = bdr.shape
    TL = _tile(L, S5_TL)
    nc = L // TL
    lw = min(S, S5_LW)

    def body(u_ref, xr_ref, xi_ref, xrp_ref, xip_ref, do_ref, bdr_ref, bdi_ref, pw_ref, tr_ref, ti_ref, cdr_ref, cdi_ref,
             d_ref, w_ref, du_ref, dw_ref, dd_ref, dcdr_ref, dcdi_ref, dbdr_ref, dbdi_ref, dar_ref, dai_ref,
             gr_s, gi_s, xfr, xfi, carry_s):
        i = pl.program_id(0)

        @pl.when(i == 0)
        def _():
            carry_s[...] = jnp.zeros_like(carry_s)
            for r in (dw_ref, dd_ref, dcdr_ref, dcdi_ref, dbdr_ref, dbdi_ref, dar_ref, dai_ref):
                r[...] = jnp.zeros_like(r)

        u = u_ref[...]
        ub = u.astype(BF16)
        xrb, xib = xr_ref[...], xi_ref[...]
        y = _dot(xrb, cdr_ref[...], _NN) - _dot(xib, cdi_ref[...], _NN) + d_ref[...] * u
        g, gp = _gelu_and_grad(y)
        gb = g.astype(BF16)
        sg = jax.nn.sigmoid(_dot(gb, w_ref[...], _NN))
        dout = do_ref[...]
        dz = (dout * g * sg * (1.0 - sg)).astype(BF16)
        dg = dout * sg + _dot(dz, w_ref[...], _NT)
        dw_ref[...] += _dot(gb, dz, _TN)
        dy = dg * gp
        dyb = dy.astype(BF16)
        dd_ref[...] += jnp.sum(dy * u, axis=0, keepdims=True)
        gr_s[...] = _dot(dyb, cdr_ref[...], _NT)
        gi_s[...] = -_dot(dyb, cdi_ref[...], _NT)
        dcdr_ref[...] += _dot(xrb, dyb, _TN)
        dcdi_ref[...] -= _dot(xib, dyb, _TN)
        live = (i < nc - 1).astype(F32)
        xfr[0:8, :] = xrp_ref[...].astype(F32) * live
        xfi[0:8, :] = xip_ref[...].astype(F32) * live
        xfr[8:, :] = xrb.astype(F32)
        xfi[8:, :] = xib.astype(F32)
        for lc in range(S // lw):
            ls = slice(lc * lw, (lc + 1) * lw)

            def extra(off, lr, li, acc, row, ls=ls):
                cur_r, cur_i = xfr[pl.ds(off + 8, 8), ls], xfi[pl.ds(off + 8, 8), ls]
                prv_r, prv_i = xfr[pl.ds(off, 8), ls], xfi[pl.ds(off, 8), ls]
                xpr = jnp.where(row >= 1, pltpu.roll(cur_r, 1, 0), prv_r[7:8, :])
                xpi = jnp.where(row >= 1, pltpu.roll(cur_i, 1, 0), prv_i[7:8, :])
                return acc[0] + lr * xpr + li * xpi, acc[1] - lr * xpi + li * xpr

            acc = _scan_tiles(gr_s, gi_s, ls, lw, pw_ref, tr_ref, ti_ref, carry_s, TL // 8, True, extra)
            dar_ref[:, ls] += acc[0]
            dai_ref[:, ls] += acc[1]
        lrb, lib = gr_s[...].astype(BF16), gi_s[...].astype(BF16)
        du_ref[...] = dy * d_ref[...] + _dot(lrb, bdr_ref[...], _NT) + _dot(lib, bdi_ref[...], _NT)
        dbdr_ref[...] += _dot(ub, lrb, _TN)
        dbdi_ref[...] += _dot(ub, lib, _TN)

    rev = lambda i: nc - 1 - i
    full = lambda arr: pl.BlockSpec(arr.shape, lambda i: (0,) * arr.ndim)
    chunk = lambda w: pl.BlockSpec((TL, w), lambda i: (rev(i), 0))
    prev8 = pl.BlockSpec((8, S), lambda i: (jnp.maximum(rev(i) * (TL // 8) - 1, 0), 0))
    acc_shapes = [(GW, GW), (1, GW), (S, GW), (S, GW), (GW, S), (GW, S), (8, S), (8, S)]
    return pl.pallas_call(
        body, grid=(nc,),
        in_specs=[chunk(GW), chunk(S), chunk(S), prev8, prev8, chunk(GW)] + [full(t) for t in (bdr, bdi, pwc, trc, tic, cdr, cdi, dskip, wglu)],
        out_specs=[chunk(GW)] + [pl.BlockSpec(s, lambda i: (0, 0)) for s in acc_shapes],
        out_shape=[jax.ShapeDtypeStruct((L, GW), F32)] + [jax.ShapeDtypeStruct(s, F32) for s in acc_shapes],
        scratch_shapes=[pltpu.VMEM((TL, S), F32), pltpu.VMEM((TL, S), F32), pltpu.VMEM((TL + 8, S), F32),
                        pltpu.VMEM((TL + 8, S), F32), pltpu.VMEM((8, S), F32)],
        compiler_params=_cparams(("arbitrary",), VMEM_LIMIT), name=name,
    )(proj, xr, xi, xr, xi, dout, bdr, bdi, pwc, trc, tic, cdr, cdi, dskip, wglu)


def _cpow_tables(ar, ai):
    def cm(a, b):
        return a[0] * b[0] - a[1] * b[1], a[0] * b[1] + a[1] * b[0]
    p = [(ar, ai)]
    for _ in range(7):
        p.append(cm(p[-1], p[0]))
    z = jnp.zeros_like(ar)
    pw = jnp.stack([p[0][0], p[0][1], p[1][0], p[1][1], p[3][0], p[3][1], z, z])
    return pw, jnp.stack([q[0] for q in p]), jnp.stack([q[1] for q in p])


def _s5_disc(lam_re, lam_im, log_dt, b_re, b_im):
    dt = jnp.exp(log_dt)[:, None]
    mag = jnp.exp(lam_re * dt)
    ab_r, ab_i = mag * jnp.cos(lam_im * dt), mag * jnp.sin(lam_im * dt)
    den = lam_re * lam_re + lam_im * lam_im
    nr, ni = ab_r - 1.0, ab_i
    f_r = ((nr * lam_re + ni * lam_im) / den)[..., None]
    f_i = ((ni * lam_re - nr * lam_im) / den)[..., None]
    return ab_r, ab_i, f_r * b_re - f_i * b_im, f_r * b_im + f_i * b_re


def _bd(t):
    G, A, B = t.shape
    return (t[:, :, None, :] * jnp.eye(G, dtype=t.dtype)[:, None, :, None]).reshape(G * A, G * B)


def _bd_extract(m, G):
    A, B = m.shape[0] // G, m.shape[1] // G
    return jnp.sum(m.reshape(G, A, G, B) * jnp.eye(G, dtype=m.dtype)[:, None, :, None], axis=2)


POOL_TQ = 256


def _band(shape, row0, col0, win, transpose):
    r = lax.broadcasted_iota(jnp.int32, shape, 0) + row0
    c = lax.broadcasted_iota(jnp.int32, shape, 1) + col0
    d = (c - r) if transpose else (r - c)
    return ((d >= 0) & (d < win)).astype(BF16)


def _band_dot(band, v):
    hi = v.astype(BF16)
    lo = (v - hi.astype(F32)).astype(BF16)
    return _dot(band, hi, _NN) + _dot(band, lo, _NN)


def _pool_p(u_prev, u_cur, i, win, tq):
    t0 = i * tq
    cnt = jnp.minimum(lax.broadcasted_iota(jnp.int32, (tq, 1), 0) + t0 + 1, win).astype(F32)
    live = (i > 0).astype(F32)
    acc = _band_dot(_band((tq, tq), t0, t0, win, False), u_cur)
    acc += _band_dot(_band((tq, tq), t0, t0 - tq, win, False), u_prev * live)
    return acc / cnt - u_cur


def _pool_fwd(proj, pool_w, pool_scale3, *, name):
    L = proj.shape[0]
    nw, PC, _ = pool_w.shape
    tq = _tile(L, POOL_TQ)

    def body(up_ref, uc_ref, w_ref, s_ref, y_ref):
        g, i = pl.program_id(0), pl.program_id(1)
        win = jnp.left_shift(2, g)
        p = _pool_p(up_ref[...], uc_ref[...], i, win, tq)
        y_ref[...] = _dot(p.astype(BF16), w_ref[...].astype(BF16), _NN) * s_ref[...]

    return pl.pallas_call(
        body, grid=(nw, L // tq),
        in_specs=[pl.BlockSpec((tq, PC), lambda g, i: (jnp.maximum(i - 1, 0), nw + g)),
                  pl.BlockSpec((tq, PC), lambda g, i: (i, nw + g)),
                  pl.BlockSpec((None, PC, PC), lambda g, i: (g, 0, 0)),
                  pl.BlockSpec((None, 1, PC), lambda g, i: (g, 0, 0))],
        out_specs=pl.BlockSpec((tq, PC), lambda g, i: (i, g)),
        out_shape=jax.ShapeDtypeStruct((L, nw * PC), F32),
        compiler_params=_cparams(("parallel", "parallel")), name=name)(proj, proj, pool_w, pool_scale3)


def _pool_bwd(proj, dy, pool_w, pool_scale3, *, name):
    L = proj.shape[0]
    nw, PC, _ = pool_w.shape
    tq = _tile(L, POOL_TQ)
    nq = L // tq

    def body(up_ref, uc_ref, dyc_ref, dyn_ref, w_ref, s_ref, du_ref, dw_ref, ds_ref):
        g, i = pl.program_id(0), pl.program_id(1)
        win = jnp.left_shift(2, g)

        @pl.when(i == 0)
        def _():
            dw_ref[...] = jnp.zeros_like(dw_ref)
            ds_ref[...] = jnp.zeros_like(ds_ref)

        wb = w_ref[...].astype(BF16)
        p = _pool_p(up_ref[...], uc_ref[...], i, win, tq).astype(BF16)
        dyc = dyc_ref[...]
        ds_ref[...] += jnp.sum(dyc * _dot(p, wb, _NN), axis=0, keepdims=True)
        dys = (dyc * s_ref[...]).astype(BF16)
        dw_ref[...] += _dot(p, dys, _TN)
        t0 = i * tq
        rows = lax.broadcasted_iota(jnp.int32, (tq, 1), 0)
        dp_c = _dot(dys, wb, _NT)
        q_c = dp_c / jnp.minimum(rows + t0 + 1, win).astype(F32)
        live = (i < nq - 1).astype(F32)
        dp_n = _dot((dyn_ref[...] * s_ref[...] * live).astype(BF16), wb, _NT)
        q_n = dp_n / jnp.minimum(rows + t0 + tq + 1, win).astype(F32)
        du = _band_dot(_band((tq, tq), t0, t0, win, True), q_c) + _band_dot(_band((tq, tq), t0, t0 + tq, win, True), q_n)
        du_ref[...] = du - dp_c

    return pl.pallas_call(
        body, grid=(nw, nq),
        in_specs=[pl.BlockSpec((tq, PC), lambda g, i: (jnp.maximum(i - 1, 0), nw + g)),
                  pl.BlockSpec((tq, PC), lambda g, i: (i, nw + g)),
                  pl.BlockSpec((tq, PC), lambda g, i: (i, g)),
                  pl.BlockSpec((tq, PC), lambda g, i: (jnp.minimum(i + 1, nq - 1), g)),
                  pl.BlockSpec((None, PC, PC), lambda g, i: (g, 0, 0)),
                  pl.BlockSpec((None, 1, PC), lambda g, i: (g, 0, 0))],
        out_specs=[pl.BlockSpec((tq, PC), lambda g, i: (i, g)),
                   pl.BlockSpec((None, PC, PC), lambda g, i: (g, 0, 0)),
                   pl.BlockSpec((None, 1, PC), lambda g, i: (g, 0, 0))],
        out_shape=[jax.ShapeDtypeStruct((L, nw * PC), F32), jax.ShapeDtypeStruct((nw, PC, PC), F32),
                   jax.ShapeDtypeStruct((nw, 1, PC), F32)],
        compiler_params=_cparams(("parallel", "arbitrary")), name=name)(proj, proj, dy, dy, pool_w, pool_scale3)


CONV_RC = 256


def _conv1_fwd(proj, w_dw, b_dw, *, name):
    L = proj.shape[0]
    GW = w_dw.shape[1]
    CB = min(GW, 128)
    nb = GW // CB
    RC = _tile(L, CONV_RC)
    n = RC + CONV_PAD

    def body(val_ref, gate_ref, w_ref, b_ref, o_ref, hp):
        hp[0:CONV_PAD, :] = jnp.zeros((CONV_PAD, CB), F32)
        hp[CONV_PAD:, :] = val_ref[...] * jax.nn.sigmoid(gate_ref[...])
        wv = w_ref[...]

        def chunk(ci, carry):
            c0 = pl.multiple_of(ci * RC, 8)
            win = hp[pl.ds(c0, n), :]
            acc = jnp.zeros((RC, CB), F32) + b_ref[...]
            for k in range(CONV_WIDTH):
                acc = acc + wv[k:k + 1, :] * pltpu.roll(win, n - (k + 2), 0)[0:RC]
            o_ref[pl.ds(c0, RC), :] = acc
            return carry

        lax.fori_loop(0, L // RC, chunk, 0)

    col = lambda off: pl.BlockSpec((L, CB), lambda c: (0, off * nb + c))
    return pl.pallas_call(
        body, grid=(nb,),
        in_specs=[col(2), col(3), pl.BlockSpec((CONV_PAD, CB), lambda c: (0, c)), pl.BlockSpec((1, CB), lambda c: (0, c))],
        out_specs=pl.BlockSpec((L, CB), lambda c: (0, c)),
        out_shape=jax.ShapeDtypeStruct((L, GW), F32),
        scratch_shapes=[pltpu.VMEM((L + CONV_PAD, CB), F32)],
        compiler_params=_cparams(("parallel",)), name=name)(proj, proj, w_dw, b_dw)


def _conv1_bwd(proj, dhc, w_dw, *, name):
    L = proj.shape[0]
    GW = w_dw.shape[1]
    CB = min(GW, 128)
    nb = GW // CB
    RC = _tile(L, CONV_RC)
    n = RC + CONV_PAD

    def body(val_ref, gate_ref, d_ref, w_ref, dval_ref, dgate_ref, dw_ref, db_ref, hp, dp):
        val = val_ref[...]
        sg = jax.nn.sigmoid(gate_ref[...])
        hp[0:CONV_PAD, :] = jnp.zeros((CONV_PAD, CB), F32)
        hp[CONV_PAD:, :] = val * sg
        dp[0:L, :] = d_ref[...]
        dp[L:, :] = jnp.zeros((CONV_PAD, CB), F32)
        dw_ref[...] = jnp.zeros_like(dw_ref)
        db_ref[...] = jnp.sum(d_ref[...], axis=0, keepdims=True)
        wv = w_ref[...]

        def chunk(ci, carry):
            c0 = pl.multiple_of(ci * RC, 8)
            win = hp[pl.ds(c0, n), :]
            dwin = dp[pl.ds(c0, n), :]
            d = dwin[0:RC]
            dh = jnp.zeros((RC, CB), F32)
            for k in range(CONV_WIDTH):
                dw_ref[k:k + 1, :] += jnp.sum(d * pltpu.roll(win, n - (k + 2), 0)[0:RC], axis=0, keepdims=True)
                sh = CONV_WIDTH - 1 - k
                dh = dh + wv[k:k + 1, :] * (pltpu.roll(dwin, n - sh, 0)[0:RC] if sh else d)
            v, s = val_ref[pl.ds(c0, RC), :], jax.nn.sigmoid(gate_ref[pl.ds(c0, RC), :])
            dval_ref[pl.ds(c0, RC), :] = dh * s
            dgate_ref[pl.ds(c0, RC), :] = dh * v * s * (1.0 - s)
            return carry

        lax.fori_loop(0, L // RC, chunk, 0)

    col = lambda off: pl.BlockSpec((L, CB), lambda c: (0, off * nb + c))
    own = pl.BlockSpec((L, CB), lambda c: (0, c))
    return pl.pallas_call(
        body, grid=(nb,),
        in_specs=[col(2), col(3), own, pl.BlockSpec((CONV_PAD, CB), lambda c: (0, c))],
        out_specs=[own, own, pl.BlockSpec((CONV_PAD, CB), lambda c: (0, c)), pl.BlockSpec((1, CB), lambda c: (0, c))],
        out_shape=[jax.ShapeDtypeStruct((L, GW), F32), jax.ShapeDtypeStruct((L, GW), F32),
                   jax.ShapeDtypeStruct((CONV_PAD, GW), F32), jax.ShapeDtypeStruct((1, GW), F32)],
        scratch_shapes=[pltpu.VMEM((L + CONV_PAD, CB), F32), pltpu.VMEM((L + CONV_PAD, CB), F32)],
        compiler_params=_cparams(("parallel",)), name=name)(proj, proj, dhc, w_dw)


def _ln_silu(hc, g, b):
    mu = jnp.mean(hc, axis=-1, keepdims=True)
    xc = hc - mu
    r = lax.rsqrt(jnp.mean(xc * xc, axis=-1, keepdims=True) + NORM_EPS)
    xh = xc * r
    a = xh * g + b
    sg = jax.nn.sigmoid(a)
    return xh, r, a, sg


def _conv2_fwd(hc, ln_g, ln_b, w_pw, *, name):
    L, GW = hc.shape
    tr = _tile(L, 256)

    def body(h_ref, g_ref, b_ref, w_ref, y_ref):
        _, _, a, sg = _ln_silu(h_ref[...], g_ref[...], b_ref[...])
        y_ref[...] = _dot((a * sg).astype(BF16), w_ref[...], _NN)

    row = pl.BlockSpec((tr, GW), lambda i: (i, 0))
    vec = pl.BlockSpec((1, GW), lambda i: (0, 0))
    return pl.pallas_call(
        body, grid=(L // tr,), in_specs=[row, vec, vec, pl.BlockSpec((GW, GW), lambda i: (0, 0))],
        out_specs=row, out_shape=jax.ShapeDtypeStruct((L, GW), F32),
        compiler_params=_cparams(("parallel",)), name=name)(hc, ln_g, ln_b, w_pw)


def _conv2_bwd(hc, ln_g, ln_b, w_pw, dy, *, name):
    L, GW = hc.shape
    tr = _tile(L, 256)

    def body(h_ref, g_ref, b_ref, w_ref, dy_ref, dh_ref, dg_ref, db_ref, dw_ref):
        @pl.when(pl.program_id(0) == 0)
        def _():
            for r in (dg_ref, db_ref, dw_ref):
                r[...] = jnp.zeros_like(r)

        xh, r, a, sg = _ln_silu(h_ref[...], g_ref[...], b_ref[...])
        dyb = dy_ref[...].astype(BF16)
        dw_ref[...] += _dot((a * sg).astype(BF16), dyb, _TN)
        da = _dot(dyb, w_ref[...], _NT) * (sg + a * sg * (1.0 - sg))
        dg_ref[...] += jnp.sum(da * xh, axis=0, keepdims=True)
        db_ref[...] += jnp.sum(da, axis=0, keepdims=True)
        dxh = da * g_ref[...]
        dh_ref[...] = r * (dxh - jnp.mean(dxh, axis=-1, keepdims=True) - xh * jnp.mean(dxh * xh, axis=-1, keepdims=True))

    row = pl.BlockSpec((tr, GW), lambda i: (i, 0))
    vec = pl.BlockSpec((1, GW), lambda i: (0, 0))
    mat = pl.BlockSpec((GW, GW), lambda i: (0, 0))
    return pl.pallas_call(
        body, grid=(L // tr,), in_specs=[row, vec, vec, mat, row],
        out_specs=[row, vec, vec, mat],
        out_shape=[jax.ShapeDtypeStruct((L, GW), F32), jax.ShapeDtypeStruct((1, GW), F32), jax.ShapeDtypeStruct((1, GW), F32),
                   jax.ShapeDtypeStruct((GW, GW), F32)],
        compiler_params=_cparams(("arbitrary",)), name=name)(hc, ln_g, ln_b, w_pw, dy)


def _grp_fwd(ys, g, *, name):
    L, GW = ys[0].shape
    tr = _tile(L, 256)

    def body(*refs):
        g_ref, o_ref = refs[4], refs[5]
        for m in range(N_MIXERS):
            yv = refs[m][...]
            r = lax.rsqrt(jnp.mean(yv * yv, axis=-1, keepdims=True) + NORM_EPS)
            o_ref[:, m * GW:(m + 1) * GW] = (yv * r * g_ref[:, m * GW:(m + 1) * GW]).astype(o_ref.dtype)

    row = pl.BlockSpec((tr, GW), lambda i: (i, 0))
    return pl.pallas_call(
        body, grid=(L // tr,), in_specs=[row] * 4 + [pl.BlockSpec((1, N_MIXERS * GW), lambda i: (0, 0))],
        out_specs=pl.BlockSpec((tr, N_MIXERS * GW), lambda i: (i, 0)),
        out_shape=jax.ShapeDtypeStruct((L, N_MIXERS * GW), BF16),
        compiler_params=_cparams(("parallel",)), name=name)(*ys, g)


def _grp_bwd(ys, g, dy, *, name):
    L, GW = ys[0].shape
    tr = _tile(L, 256)

    def body(*refs):
        g_ref, dy_ref = refs[4], refs[5]
        outs, dg_ref = refs[6:10], refs[10]

        @pl.when(pl.program_id(0) == 0)
        def _():
            dg_ref[...] = jnp.zeros_like(dg_ref)

        for m in range(N_MIXERS):
            cs = slice(m * GW, (m + 1) * GW)
            yv = refs[m][...]
            r = lax.rsqrt(jnp.mean(yv * yv, axis=-1, keepdims=True) + NORM_EPS)
            xh = yv * r
            dyv = dy_ref[:, cs]
            dyg = dyv * g_ref[:, cs]
            outs[m][...] = r * (dyg - xh * jnp.mean(dyg * xh, axis=-1, keepdims=True))
            dg_ref[:, cs] += jnp.sum(dyv * xh, axis=0, keepdims=True)

    row = pl.BlockSpec((tr, GW), lambda i: (i, 0))
    wide = pl.BlockSpec((tr, N_MIXERS * GW), lambda i: (i, 0))
    vec = pl.BlockSpec((1, N_MIXERS * GW), lambda i: (0, 0))
    return pl.pallas_call(
        body, grid=(L // tr,), in_specs=[row] * 4 + [vec, wide],
        out_specs=[row] * 4 + [vec],
        out_shape=[jax.ShapeDtypeStruct((L, GW), F32)] * 4 + [jax.ShapeDtypeStruct((1, N_MIXERS * GW), F32)],
        compiler_params=_cparams(("arbitrary",)), name=name)(*ys, g, dy)


ATT_PAR = 4


def _datt_lanes(GW):
    E = GW // ATT_HEADS
    lb = min(GW, 128)
    return lb, lb // E, E


def _datt_rows(c, br, nblk):
    dil = 4 ** br
    nbs = nblk // dil
    r, jb = c // nbs, c % nbs
    return r + dil * ATT_BLOCK * jb, r + dil * ATT_BLOCK * jnp.maximum(jb - 1, 0), jb == 0


def _datt_fwd(proj, bias, *, name):
    L = proj.shape[0]
    GW = proj.shape[1] // 7
    lb, hps, E = _datt_lanes(GW)
    B = ATT_BLOCK
    nblk = L // B
    scale = E ** -0.5

    def body(q_ref, k_ref, v_ref, b_ref, y_ref, lse_ref, m_s, l_s):
        m_s[...] = jnp.full(m_s.shape, NEG_INF, F32)
        l_s[...] = jnp.zeros(l_s.shape, F32)
        y_ref[...] = jnp.zeros(y_ref.shape, F32)
        lane = lax.broadcasted_iota(jnp.int32, (B, lb), 1) // E

        for br, (_, dil) in enumerate(DILATED_PATTERNS):
            def step(i, carry, br=br, dil=dil):
                loaded = []
                for u in range(ATT_PAR):
                    start, pstart, first = _datt_rows(i + u * (nblk // ATT_PAR), br, nblk)
                    rows, prows = pl.ds(start, B, stride=dil), pl.ds(pstart, B, stride=dil)
                    loaded.append((rows, first, q_ref[rows, :] * scale, k_ref[rows, :].astype(BF16), k_ref[prows, :].astype(BF16),
                                   v_ref[rows, :].astype(BF16), v_ref[prows, :].astype(BF16), m_s[rows, :], l_s[rows, :], y_ref[rows, :]))
                done = []
                for rows, first, q, kc, kp, vc, vp, m_old, l_old, a_old in loaded:
                    m_new, l_new, a_new = m_old, l_old, a_old
                    for a in range(hps):
                        sel = lane == a
                        qm = jnp.where(sel, q, 0.0).astype(BF16)
                        s_p = _dot(qm, kp, _NT) + b_ref[br, a, :, 0:B]
                        s_c = _dot(qm, kc, _NT) + b_ref[br, a, :, B:2 * B]
                        s_p = jnp.where(first, NEG_INF, s_p)
                        mo, lo = m_old[:, a * E:a * E + 1], l_old[:, a * E:a * E + 1]
                        mn = jnp.maximum(mo, jnp.maximum(jnp.max(s_p, axis=-1, keepdims=True), jnp.max(s_c, axis=-1, keepdims=True)))
                        alpha = jnp.exp(mo - mn)
                        p_p, p_c = jnp.exp(s_p - mn), jnp.exp(s_c - mn)
                        ln = alpha * lo + jnp.sum(p_p, axis=-1, keepdims=True) + jnp.sum(p_c, axis=-1, keepdims=True)
                        pv = _dot(p_p.astype(BF16), vp, _NN) + _dot(p_c.astype(BF16), vc, _NN)
                        m_new = jnp.where(sel, mn, m_new)
                        l_new = jnp.where(sel, ln, l_new)
                        a_new = jnp.where(sel, alpha * a_old + pv, a_new)
                    done.append((rows, m_new, l_new, a_new))
                for rows, m_new, l_new, a_new in done:
                    m_s[rows, :] = m_new
                    l_s[rows, :] = l_new
                    y_ref[rows, :] = a_new
                return carry

            lax.fori_loop(0, nblk // ATT_PAR, step, 0)
        y_ref[...] = y_ref[...] / l_s[...]
        lse_ref[...] = m_s[...] + jnp.log(l_s[...])

    col = lambda off: pl.BlockSpec((L, lb), lambda h: (0, off * (GW // lb) + h))
    own = pl.BlockSpec((L, lb), lambda h: (0, h))
    return pl.pallas_call(
        body, grid=(GW // lb,),
        in_specs=[col(4), col(5), col(6), pl.BlockSpec((len(DILATED_PATTERNS), hps, B, 2 * B), lambda h: (0, h, 0, 0))],
        out_specs=[own, own], out_shape=[jax.ShapeDtypeStruct((L, GW), F32)] * 2,
        scratch_shapes=[pltpu.VMEM((L, lb), F32), pltpu.VMEM((L, lb), F32)],
        compiler_params=_cparams(("parallel",), VMEM_LIMIT), name=name)(proj, proj, proj, bias)


def _datt_bwd(proj, bias, y, lse, dy, *, name):
    L = proj.shape[0]
    GW = proj.shape[1] // 7
    lb, hps, E = _datt_lanes(GW)
    B = ATT_BLOCK
    nblk = L // B
    scale = E ** -0.5

    def body(q_ref, k_ref, v_ref, b_ref, y_ref, lse_ref, dy_ref, dq_ref, dk_ref, dv_ref, db_ref):
        for r in (dq_ref, dk_ref, dv_ref, db_ref):
            r[...] = jnp.zeros(r.shape, F32)
        lane = lax.broadcasted_iota(jnp.int32, (B, lb), 1) // E

        for br, (_, dil) in enumerate(DILATED_PATTERNS):
            def step(i, carry, br=br, dil=dil):
                loaded = []
                for u in range(ATT_PAR):
                    start, pstart, first = _datt_rows(i + u * (nblk // ATT_PAR), br, nblk)
                    rows, prows = pl.ds(start, B, stride=dil), pl.ds(pstart, B, stride=dil)
                    dyr = dy_ref[rows, :]
                    loaded.append((rows, prows, first, q_ref[rows, :] * scale, k_ref[rows, :].astype(BF16), k_ref[prows, :].astype(BF16),
                                   v_ref[rows, :].astype(BF16), v_ref[prows, :].astype(BF16), dyr, lse_ref[rows, :], dyr * y_ref[rows, :]))
                done = []
                dbias = [None] * hps
                for rows, prows, first, q, kc, kp, vc, vp, dyr, lser, dyy in loaded:
                    dq = jnp.zeros((B, lb), F32)
                    dkc, dkp, dvc, dvp = dq, dq, dq, dq
                    for a in range(hps):
                        sel = lane == a
                        qm = jnp.where(sel, q, 0.0).astype(BF16)
                        dym = jnp.where(sel, dyr, 0.0).astype(BF16)
                        dm = jnp.sum(jnp.where(sel, dyy, 0.0), axis=-1, keepdims=True)
                        lse_a = lser[:, a * E:a * E + 1]
                        s_p = _dot(qm, kp, _NT) + b_ref[br, a, :, 0:B]
                        s_c = _dot(qm, kc, _NT) + b_ref[br, a, :, B:2 * B]
                        s_p = jnp.where(first, NEG_INF, s_p)
                        p_p, p_c = jnp.exp(s_p - lse_a), jnp.exp(s_c - lse_a)
                        ds_p = p_p * (_dot(dym, vp, _NT) - dm)
                        ds_c = p_c * (_dot(dym, vc, _NT) - dm)
                        dbias[a] = (ds_p, ds_c) if dbias[a] is None else (dbias[a][0] + ds_p, dbias[a][1] + ds_c)
                        dsb_p, dsb_c = ds_p.astype(BF16), ds_c.astype(BF16)
                        dq = dq + jnp.where(sel, _dot(dsb_p, kp, _NN) + _dot(dsb_c, kc, _NN), 0.0)
                        dkp = dkp + _dot(dsb_p, qm, _TN)
                        dkc = dkc + _dot(dsb_c, qm, _TN)
                        dvp = dvp + _dot(p_p.astype(BF16), dym, _TN)
                        dvc = dvc + _dot(p_c.astype(BF16), dym, _TN)
                    done.append((rows, prows, dq, dkp, dkc, dvp, dvc))
                for a in range(hps):
                    db_ref[br, a, :, 0:B] += dbias[a][0]
                    db_ref[br, a, :, B:2 * B] += dbias[a][1]
                for rows, prows, dq, dkp, dkc, dvp, dvc in done:
                    dq_ref[rows, :] += dq * scale
                    dk_ref[prows, :] += dkp
                    dk_ref[rows, :] += dkc
                    dv_ref[prows, :] += dvp
                    dv_ref[rows, :] += dvc
                return carry

            lax.fori_loop(0, nblk // ATT_PAR, step, 0)

    col = lambda off: pl.BlockSpec((L, lb), lambda h: (0, off * (GW // lb) + h))
    own = pl.BlockSpec((L, lb), lambda h: (0, h))
    bsp = pl.BlockSpec((len(DILATED_PATTERNS), hps, B, 2 * B), lambda h: (0, h, 0, 0))
    return pl.pallas_call(
        body, grid=(GW // lb,),
        in_specs=[col(4), col(5), col(6), bsp, own, own, own], out_specs=[own, own, own, bsp],
        out_shape=[jax.ShapeDtypeStruct((L, GW), F32)] * 3 + [jax.ShapeDtypeStruct((len(DILATED_PATTERNS), ATT_HEADS, B, 2 * B), F32)],
        compiler_params=_cparams(("parallel",), VMEM_LIMIT), name=name)(proj, proj, proj, bias, y, lse, dy)


def _t5_bucket(dist):
    n = np.maximum(dist, 0)
    max_exact = REL_BUCKETS // 2
    large = max_exact + (np.log(np.maximum(n, 1) / max_exact) / np.log(REL_MAX_DIST / max_exact)
                         * (REL_BUCKETS - max_exact)).astype(np.int64)
    large = np.minimum(large, REL_BUCKETS - 1)
    return np.where(n < max_exact, n, large).astype(np.int32)


def _att_tables():
    a = np.arange(ATT_BLOCK)[:, None]
    b = np.arange(2 * ATT_BLOCK)[None, :]
    sub = a + ATT_BLOCK - b
    buckets, valids = [], []
    for window, dil in DILATED_PATTERNS:
        buckets.append(_t5_bucket(sub * dil))
        valids.append((sub >= 0) & (sub <= window // dil))
    return np.stack(buckets), np.stack(valids)


def _xatt_fwd(q, k, v, *, name):
    L, XW = q.shape
    M = k.shape[0]
    tr = _tile(L, 512)
    scale = X_HEAD_DIM ** -0.5

    def body(q_ref, k_ref, v_ref, o_ref):
        s = _dot(q_ref[...], k_ref[...], _NT) * scale
        p = jnp.exp(s - jnp.max(s, axis=-1, keepdims=True))
        den = jnp.sum(p, axis=-1, keepdims=True)
        o_ref[...] = (_dot(p.astype(BF16), v_ref[...], _NN) / den).astype(o_ref.dtype)

    qs = pl.BlockSpec((tr, X_HEAD_DIM), lambda h, i: (i, h))
    ks = pl.BlockSpec((M, X_HEAD_DIM), lambda h, i: (0, h))
    return pl.pallas_call(
        body, grid=(XW // X_HEAD_DIM, L // tr), in_specs=[qs, ks, ks], out_specs=qs,
        out_shape=jax.ShapeDtypeStruct((L, XW), BF16),
        compiler_params=_cparams(("parallel", "parallel")), name=name)(q, k, v)


def _xatt_bwd(q, k, v, do, *, name):
    L, XW = q.shape
    M = k.shape[0]
    tr = _tile(L, 512)
    scale = X_HEAD_DIM ** -0.5

    def body(q_ref, k_ref, v_ref, do_ref, dq_ref, dk_ref, dv_ref):
        @pl.when(pl.program_id(1) == 0)
        def _():
            dk_ref[...] = jnp.zeros_like(dk_ref)
            dv_ref[...] = jnp.zeros_like(dv_ref)

        qv, kv, vv = q_ref[...], k_ref[...], v_ref[...]
        s = _dot(qv, kv, _NT) * scale
        p = jnp.exp(s - jnp.max(s, axis=-1, keepdims=True))
        p = p / jnp.sum(p, axis=-1, keepdims=True)
        dob = do_ref[...].astype(BF16)
        pb = p.astype(BF16)
        dv_ref[...] += _dot(pb, dob, _TN)
        dp = _dot(dob, vv, _NT)
        ds = (p * (dp - jnp.sum(dp * p, axis=-1, keepdims=True)) * scale).astype(BF16)
        dq_ref[...] = _dot(ds, kv, _NN)
        dk_ref[...] += _dot(ds, qv, _TN)

    qs = pl.BlockSpec((tr, X_HEAD_DIM), lambda h, i: (i, h))
    ks = pl.BlockSpec((M, X_HEAD_DIM), lambda h, i: (0, h))
    return pl.pallas_call(
        body, grid=(XW // X_HEAD_DIM, L // tr), in_specs=[qs, ks, ks, qs], out_specs=[qs, ks, ks],
        out_shape=[jax.ShapeDtypeStruct((L, XW), F32), jax.ShapeDtypeStruct((M, XW), F32), jax.ShapeDtypeStruct((M, XW), F32)],
        compiler_params=_cparams(("parallel", "arbitrary")), name=name)(q, k, v, do)


_ANY = pl.BlockSpec(memory_space=pl.ANY)


def _place():
    mx, my, mc = lax.axis_index("x"), lax.axis_index("y"), lax.axis_index("c")
    chips = [(1 - mx, my), (mx, 1 - my), (1 - mx, 1 - my)]
    return mx, my, mc, chips


def _rcopy(src, dst, ssem, rsem, dev):
    return pltpu.make_async_remote_copy(src_ref=src, dst_ref=dst, send_sem=ssem, recv_sem=rsem, device_id=dev,
                                        device_id_type=MESH)


STAGE_BYTES = 2 * 1024 * 1024


def _staged_copy(pairs, buf, isem, osem):
    n = len(pairs)
    ins = [pltpu.make_async_copy(src, buf.at[i % 2], isem.at[i % 2]) for i, (src, _) in enumerate(pairs)]
    outs = [pltpu.make_async_copy(buf.at[i % 2], dst, osem.at[i % 2]) for i, (_, dst) in enumerate(pairs)]
    ins[0].start()
    for i in range(n):
        if i + 1 < n:
            if i >= 1:
                outs[i - 1].wait()
            ins[i + 1].start()
        ins[i].wait()
        outs[i].start()
    for i in range(max(n - 2, 0), n):
        outs[i].wait()


_HBM = pl.BlockSpec(memory_space=pltpu.HBM)
_SEMS = pl.BlockSpec(memory_space=pltpu.SEMAPHORE)
_VM = pl.BlockSpec(memory_space=pltpu.VMEM)
_DATAFLOW = pltpu.SideEffectType.DATAFLOW_SIDE_EFFECTING


def _ag_copies(x_refs, land_refs, ssem, rsem, inbound, which=None):
    mx, my, mc, chips = _place()
    me = 2 * mx + my
    cps = []
    for i, (x_ref, land_ref) in enumerate(zip(x_refs, land_refs)):
        w = i if which is None else which[i]
        R2 = x_ref.shape[0] // 2
        mine = x_ref.at[pl.ds(mc * R2, R2)]
        for j, (px, py) in enumerate(chips):
            dst = land_ref.at[2 * px + py, mc] if inbound else land_ref.at[me, mc]
            cps.append(_rcopy(mine, dst, ssem.at[3 * w + j], rsem.at[3 * w + j], (px, py, mc)))
    return cps


def _ag_start(xs, token, *, name):
    n = len(xs)
    lands = [pltpu.with_memory_space_constraint(lax.empty((N_CHIPS, 2, x.shape[0] // 2, x.shape[1]), x.dtype), pltpu.HBM) for x in xs]
    xs = [pltpu.with_memory_space_constraint(x, pltpu.HBM) for x in xs]

    def body(*refs):
        x_refs, land_refs, tok_ref = refs[:n], refs[n:2 * n], refs[2 * n]
        ssem, rsem, tok_out = refs[2 * n + 1], refs[2 * n + 2], refs[-1]
        for cp in _ag_copies(x_refs, land_refs, ssem, rsem, False):
            cp.start()
        tok_out[...] = tok_ref[...]

    outs = pl.pallas_call(
        body, name=name,
        out_shape=(pltpu.SemaphoreType.DMA((3 * n,)), pltpu.SemaphoreType.DMA((3 * n,)),
                   *[pltpu.HBM(x.shape, x.dtype) for x in xs], *[pltpu.HBM(t.shape, t.dtype) for t in lands],
                   jax.ShapeDtypeStruct(token.shape, token.dtype)),
        in_specs=[_HBM] * (2 * n) + [_VM], out_specs=(_SEMS, _SEMS, *[_HBM] * (2 * n), _VM),
        input_output_aliases={i: 2 + i for i in range(2 * n)},
        compiler_params=pltpu.CompilerParams(has_side_effects=_DATAFLOW),
    )(*xs, *lands, token)
    return outs[0], outs[1], list(outs[2:2 + n]), list(outs[2 + n:2 + 2 * n]), outs[-1]


def _ag_wait(ssem, rsem, xs, lands, after, which, *, name):
    n = len(xs)

    def body(*refs):
        x_refs, land_refs, ssem_ref, rsem_ref = refs[:n], refs[n:2 * n], refs[2 * n], refs[2 * n + 1]
        for cp in _ag_copies(x_refs, land_refs, ssem_ref, rsem_ref, True, which):
            cp.wait_send()
            cp.wait_recv()

    after = list(after) if isinstance(after, (list, tuple)) else [after]
    outs = pl.pallas_call(
        body, name=name,
        out_shape=(*[pltpu.HBM(x.shape, x.dtype) for x in xs], *[pltpu.HBM(t.shape, t.dtype) for t in lands]),
        in_specs=[_HBM] * (2 * n) + [_SEMS, _SEMS] + [_ANY] * len(after), out_specs=tuple([_HBM] * (2 * n)),
        input_output_aliases={i: i for i in range(2 * n)},
        compiler_params=pltpu.CompilerParams(has_side_effects=_DATAFLOW),
    )(*xs, *lands, ssem, rsem, *after)
    return list(outs[:n]), list(outs[n:])


def _ag_finish(lands, xs, *, name):
    n = len(xs)

    def body(*refs):
        x_refs, o_refs, (ssem, rsem) = refs[n:2 * n], refs[2 * n:3 * n], refs[3 * n:]
        mx, my, mc, chips = _place()
        me = 2 * mx + my
        sib = (mx, my, 1 - mc)
        passed = []
        for w in range(n):
            for j, (px, py) in enumerate(chips):
                got = o_refs[w].at[2 * px + py, mc]
                passed.append(_rcopy(got, got, ssem.at[3 * w + j], rsem.at[3 * w + j], sib))
        for cp in passed:
            cp.start()
        for w in range(n):
            R, C = x_refs[w].shape
            R2 = R // 2
            ch = _rows_tile(R2, max(8, STAGE_BYTES // (C * x_refs[w].dtype.itemsize)))
            pairs = [(x_refs[w].at[pl.ds(h * R2 + r, ch)], o_refs[w].at[me, h, pl.ds(r, ch)]) for h in range(2) for r in range(0, R2, ch)]
            pl.run_scoped(functools.partial(_staged_copy, pairs), pltpu.VMEM((2, ch, C), x_refs[w].dtype),
                          pltpu.SemaphoreType.DMA((2,)), pltpu.SemaphoreType.DMA((2,)))
        for w in range(n):
            for j, (px, py) in enumerate(chips):
                theirs = o_refs[w].at[2 * px + py, 1 - mc]
                _rcopy(theirs, theirs, ssem.at[3 * w + j], rsem.at[3 * w + j], sib).wait_recv()
        for cp in passed:
            cp.wait_send()

    return pl.pallas_call(
        body, in_specs=[_ANY] * (2 * n), out_specs=[_ANY] * n,
        out_shape=[jax.ShapeDtypeStruct(t.shape, t.dtype) for t in lands],
        input_output_aliases={i: i for i in range(n)},
        scratch_shapes=[pltpu.SemaphoreType.DMA((3 * n,)), pltpu.SemaphoreType.DMA((3 * n,))],
        name=name)(*lands, *xs)


def _ag4(x, *, name):
    def body(x_ref, o_ref, ssem, rsem, lsem):
        mx, my, mc, chips = _place()
        me = 2 * mx + my
        loc = pltpu.make_async_copy(x_ref, o_ref.at[me], lsem)
        loc.start()
        sends = [_rcopy(x_ref, o_ref.at[me], ssem.at[j], rsem.at[j], (px, py, mc)) for j, (px, py) in enumerate(chips)]
        for cp in sends:
            cp.start()
        for j, (px, py) in enumerate(chips):
            _rcopy(x_ref, o_ref.at[2 * px + py], ssem.at[j], rsem.at[j], (px, py, mc)).wait_recv()
        for cp in sends:
            cp.wait_send()
        loc.wait()

    return pl.pallas_call(
        body, in_specs=[_ANY], out_specs=_ANY, out_shape=jax.ShapeDtypeStruct((N_CHIPS,) + x.shape, x.dtype),
        scratch_shapes=[pltpu.SemaphoreType.DMA((3,)), pltpu.SemaphoreType.DMA((3,)), pltpu.SemaphoreType.DMA(())],
        name=name)(x)


def _rs_sib(gs, after=(), *, name):
    n, na = len(gs), len(after)

    def body(*refs):
        g_refs, t_refs, (ssem, rsem) = refs[:n], refs[n + na:2 * n + na], refs[2 * n + na:]
        mx, my, mc, _ = _place()
        sib = (mx, my, 1 - mc)
        sends = []
        for w in range(n):
            R2 = g_refs[w].shape[1] // 2
            for k in range(N_CHIPS):
                sends.append(_rcopy(g_refs[w].at[k, pl.ds((1 - mc) * R2, R2)], t_refs[w].at[k], ssem.at[N_CHIPS * w + k],
                                    rsem.at[N_CHIPS * w + k], sib))
        for cp in sends:
            cp.start()
        for w in range(n):
            for k in range(N_CHIPS):
                t = t_refs[w].at[k]
                _rcopy(t, t, ssem.at[N_CHIPS * w + k], rsem.at[N_CHIPS * w + k], sib).wait_recv()
        for cp in sends:
            cp.wait_send()

    return pl.pallas_call(
        body, in_specs=[_ANY] * (n + na), out_specs=[_ANY] * n,
        out_shape=[jax.ShapeDtypeStruct((N_CHIPS, g.shape[1] // 2, g.shape[2]), g.dtype) for g in gs],
        scratch_shapes=[pltpu.SemaphoreType.DMA((N_CHIPS * n,)), pltpu.SemaphoreType.DMA((N_CHIPS * n,))],
        name=name)(*gs, *after)


def _a2a_copies(h_refs, got_refs, ssem, rsem, inbound):
    mx, my, mc, chips = _place()
    me = 2 * mx + my
    cps = []
    for w, (h_ref, got_ref) in enumerate(zip(h_refs, got_refs)):
        for j, (px, py) in enumerate(chips):
            dst = got_ref.at[2 * px + py] if inbound else got_ref.at[me]
            cps.append(_rcopy(h_ref.at[2 * px + py], dst, ssem.at[3 * w + j], rsem.at[3 * w + j], (px, py, mc)))
    return cps


def _a2a_start(hs, *, name):
    n = len(hs)
    gots = [pltpu.with_memory_space_constraint(lax.empty(h.shape, h.dtype), pltpu.HBM) for h in hs]
    hs = [pltpu.with_memory_space_constraint(h, pltpu.HBM) for h in hs]

    def body(*refs):
        h_refs, got_refs = refs[:n], refs[n:2 * n]
        ssem, rsem, tok_out = refs[2 * n], refs[2 * n + 1], refs[-1]
        for cp in _a2a_copies(h_refs, got_refs, ssem, rsem, False):
            cp.start()
        tok_out[...] = jnp.zeros_like(tok_out)

    outs = pl.pallas_call(
        body, name=name,
        out_shape=(pltpu.SemaphoreType.DMA((3 * n,)), pltpu.SemaphoreType.DMA((3 * n,)),
                   *[pltpu.HBM(h.shape, h.dtype) for h in hs], *[pltpu.HBM(h.shape, h.dtype) for h in hs],
                   jax.ShapeDtypeStruct((8, 128), F32)),
        in_specs=[_HBM] * (2 * n), out_specs=(_SEMS, _SEMS, *[_HBM] * (2 * n), _VM),
        input_output_aliases={i: 2 + i for i in range(2 * n)},
        compiler_params=pltpu.CompilerParams(has_side_effects=_DATAFLOW),
    )(*hs, *gots)
    return outs[0], outs[1], list(outs[2:2 + n]), list(outs[2 + n:2 + 2 * n]), outs[-1]


def _a2a_wait(ssem, rsem, hs, gots, after, *, name):
    n = len(hs)

    def body(*refs):
        h_refs, got_refs, ssem_ref, rsem_ref = refs[:n], refs[n:2 * n], refs[2 * n], refs[2 * n + 1]
        for cp in _a2a_copies(h_refs, got_refs, ssem_ref, rsem_ref, True):
            cp.wait_send()
            cp.wait_recv()

    after = list(after) if isinstance(after, (list, tuple)) else [after]
    outs = pl.pallas_call(
        body, name=name,
        out_shape=tuple(pltpu.HBM(h.shape, h.dtype) for h in hs + gots),
        in_specs=[_HBM] * (2 * n) + [_SEMS, _SEMS] + [_ANY] * len(after), out_specs=tuple([_HBM] * (2 * n)),
        input_output_aliases={i: i for i in range(2 * n)},
        compiler_params=pltpu.CompilerParams(has_side_effects=_DATAFLOW),
    )(*hs, *gots, ssem, rsem, *after)
    return list(outs[:n]), list(outs[n:])


def _sib_fill(bufs, after=(), *, name):
    n, na = len(bufs), len(after)

    def body(*refs):
        o_refs, (ssem, rsem) = refs[n + na:2 * n + na], refs[2 * n + na:]
        mx, my, mc, _ = _place()
        sib = (mx, my, 1 - mc)
        sends = [_rcopy(o_refs[w].at[mc], o_refs[w].at[mc], ssem.at[w], rsem.at[w], sib) for w in range(n)]
        for cp in sends:
            cp.start()
        for w in range(n):
            t = o_refs[w].at[1 - mc]
            _rcopy(t, t, ssem.at[w], rsem.at[w], sib).wait_recv()
        for cp in sends:
            cp.wait_send()

    return pl.pallas_call(
        body, in_specs=[_ANY] * (n + na), out_specs=[_ANY] * n, out_shape=[jax.ShapeDtypeStruct(b.shape, b.dtype) for b in bufs],
        input_output_aliases={i: i for i in range(n)},
        scratch_shapes=[pltpu.SemaphoreType.DMA((n,)), pltpu.SemaphoreType.DMA((n,))], name=name)(*bufs, *after)


def _rows_tile(n, pref=512):
    t = min(n, pref)
    while n % t or (t % 8 and t != n):
        t -= 1
    return t


def _rs_add(g, theirs, c_arr, payload, *, name):
    _, R, C = g.shape
    R2 = R // 2
    tr = _rows_tile(R2, 256)
    nb = R2 // tr

    def body(c_ref, g_ref, t_ref, o_ref):
        o_ref[...] = (g_ref[...].astype(F32) + t_ref[...].astype(F32)).astype(o_ref.dtype)

    blk = pl.BlockSpec((None, tr, C), lambda k, i, c: (k, i, 0))
    return pl.pallas_call(
        body,
        grid_spec=pltpu.PrefetchScalarGridSpec(
            num_scalar_prefetch=1, grid=(N_CHIPS, nb),
            in_specs=[pl.BlockSpec((None, tr, C), lambda k, i, c: (k, c[0] * nb + i, 0)), blk], out_specs=blk),
        out_shape=jax.ShapeDtypeStruct((N_CHIPS, R2, C), payload),
        compiler_params=_cparams(("arbitrary", "arbitrary")), name=name)(c_arr, g, theirs)


def _rs_sum(chip_sum, got, mc_arr, *, name):
    _, R2, C = chip_sum.shape
    tr = _rows_tile(R2, 256)

    def body(s_ref, own_ref, g0, g1, g2, g3, o_ref):
        me = s_ref[0]
        acc = jnp.zeros((tr, C), F32)
        for k, g_ref in enumerate((g0, g1, g2, g3)):
            acc = acc + jnp.where(me == k, own_ref[...], g_ref[...]).astype(F32)
        o_ref[...] = acc

    def got_spec(k):
        return pl.BlockSpec((None, tr, C), lambda i, s: (jnp.where(s[0] == k, (k + 1) % N_CHIPS, k), i, 0))

    return pl.pallas_call(
        body,
        grid_spec=pltpu.PrefetchScalarGridSpec(
            num_scalar_prefetch=1, grid=(R2 // tr,),
            in_specs=[pl.BlockSpec((None, tr, C), lambda i, s: (s[0], i, 0))] + [got_spec(k) for k in range(N_CHIPS)],
            out_specs=pl.BlockSpec((None, tr, C), lambda i, s: (s[1], i, 0))),
        out_shape=jax.ShapeDtypeStruct((2, R2, C), F32),
        compiler_params=_cparams(("arbitrary",)), name=name)(mc_arr, chip_sum, got, got, got, got)


def _adamw(w, m, v, gs, *, name, first=0, prev=None):
    _, _, R2, C = w.shape
    nl = len(gs)
    tr = _rows_tile(R2, 256)
    c1 = 1.0 / (1.0 - ADAM_B1 ** ADAM_STEP)
    c2 = 1.0 / (1.0 - ADAM_B2 ** ADAM_STEP)

    def body(w_ref, m_ref, v_ref, *refs):
        g_refs, (go_ref, d_ref, mo_ref, vo_ref) = refs[:nl], refs[-4:]
        l = pl.program_id(0)
        for ll in range(nl):
            @pl.when(l == ll)
            def _(ll=ll):
                gv = g_refs[ll][...]
                mn = ADAM_B1 * m_ref[...] + (1.0 - ADAM_B1) * gv
                vn = ADAM_B2 * v_ref[...] + (1.0 - ADAM_B2) * (gv * gv)
                go_ref[...] = gv
                mo_ref[...] = mn
                vo_ref[...] = vn
                d_ref[...] = -ADAM_LR * ((mn * c1) / (jnp.sqrt(vn * c2) + ADAM_EPS) + ADAM_WD * w_ref[...])

    def g_spec(ll):
        return pl.BlockSpec((None, tr, C), lambda l, h, i: (jnp.where(l == ll, h, 0), jnp.where(l == ll, i, 0), 0))

    ws = pl.BlockSpec((None, None, tr, C), lambda l, h, i: (l + first, h, i, 0))
    shp = jax.ShapeDtypeStruct(w.shape, F32)
    prev = list(prev) if prev is not None else []
    return pl.pallas_call(
        body, grid=(nl, 2, R2 // tr), in_specs=[ws, ws, ws] + [g_spec(ll) for ll in range(nl)] + [_ANY] * len(prev),
        out_specs=[ws] * 4, out_shape=[shp] * 4,
        input_output_aliases={3 + nl + k: k for k in range(len(prev))},
        compiler_params=_cparams(("arbitrary", "arbitrary", "arbitrary")), name=name)(w, m, v, *gs, *prev)


class _GradReducer:
    def __init__(self, tag, payload):
        self.tag, self.payload = tag, payload
        self.me = (2 * lax.axis_index("x") + lax.axis_index("y")).astype(jnp.int32)
        self.c = lax.axis_index("c").astype(jnp.int32)

    def start(self, names, gs, after=()):
        theirs = _rs_sib(gs, after, name=f"rs_sib_{self.tag}")
        hs = [_rs_add(g, t, self.c.reshape(1), self.payload, name=f"rs_add_{n}") for n, g, t in zip(names, gs, theirs)]
        self.names = names
        self.ssem, self.rsem, self.hs, self.gots, token = _a2a_start(hs, name=f"rs_a2a_start_{self.tag}")
        return token

    def finish(self, after):
        hs, gots = _a2a_wait(self.ssem, self.rsem, self.hs, self.gots, after, name=f"rs_a2a_wait_{self.tag}")
        mc = jnp.stack([self.me, self.c])
        return {n: _rs_sum(h, g, mc, name=f"rs_sum_{n}") for n, h, g in zip(self.names, hs, gots)}


WEIGHTS = ["rel_bias", "mem_norm_g", "norm_mix_g", "w_in", "s5_lam_re", "s5_lam_im", "s5_log_dt", "s5_b_re", "s5_b_im",
           "s5_c_re", "s5_c_im", "s5_d", "s5_w_glu", "pool_w", "pool_scale", "conv_w_dw", "conv_b_dw", "conv_ln_g",
           "conv_ln_b", "conv_w_pw", "grp_norm_g", "w_out", "norm_x_g", "w_xq", "w_xk", "w_xv", "w_xo", "norm_mlp_g",
           "w_up", "w_down", "norm_final_g"]
SHARDED = {"w_in": "col", "s5_w_glu": "row", "conv_w_dw": "col", "conv_w_pw": "row", "w_out": "row", "w_xq": "row",
           "w_xk": "row", "w_xv": "row", "w_xo": "col", "w_up": "col", "w_down": "row"}
AG_FIRST = ("w_in", "s5_w_glu", "conv_w_dw", "conv_w_pw")
SMALL = [n for n in WEIGHTS if n not in SHARDED]
SMALL_ALIGN = N_CHIPS * 2 * 256 * 128


def _mat(w, kind):
    return w.reshape(w.shape[0] * w.shape[1], w.shape[2]) if kind == "row" else w


def _att_bias(rel_bias):
    bucket, valid = _att_tables()
    onehot = (jnp.asarray(bucket.reshape(-1))[:, None] == jnp.arange(REL_BUCKETS)[None, :]).astype(F32)
    b = jnp.dot(onehot, rel_bias, precision=lax.Precision.HIGHEST).reshape(bucket.shape + (rel_bias.shape[1],))
    return jnp.where(jnp.asarray(valid)[:, None], jnp.transpose(b, (0, 3, 1, 2)), NEG_INF)


def _rel_bias_grad(dbias):
    bucket, _ = _att_tables()
    nb, H = dbias.shape[0], dbias.shape[1]
    onehot = (jnp.asarray(bucket.reshape(-1))[:, None] == jnp.arange(REL_BUCKETS)[None, :]).astype(BF16)
    flat = jnp.transpose(dbias.reshape(nb, H, -1), (1, 0, 2)).reshape(H, -1)
    return _mm(flat, onehot, "nn", tk=16384, name="rel_bias_grad").T


def _layer_fwd(h, mem_n, bias, sp, bw, l):
    L, D = h.shape
    GW = D // N_MIXERS
    G = GW // S5_CH
    E = GW // ATT_HEADS
    sv = {"h": h}
    xn = _rms_fwd(h, sp["norm_mix_g"][l][None], name="norm_mix")
    proj = _mm(xn, bw["w_in"], "nn", name="proj_in")
    sv.update(xn=xn, proj=proj)
    disc, disc_vjp = jax.vjp(_s5_disc, sp["s5_lam_re"][l], sp["s5_lam_im"][l], sp["s5_log_dt"][l], sp["s5_b_re"][l], sp["s5_b_im"][l])
    ab_r, ab_i, bb_r, bb_i = disc
    s5 = dict(
        bdr=_bd(jnp.transpose(bb_r, (0, 2, 1))).astype(BF16), bdi=_bd(jnp.transpose(bb_i, (0, 2, 1))).astype(BF16),
        cdr=_bd(jnp.transpose(sp["s5_c_re"][l], (0, 2, 1))).astype(BF16), cdi=_bd(jnp.transpose(sp["s5_c_im"][l], (0, 2, 1))).astype(BF16),
        d=sp["s5_d"][l][None], wglu=bw["s5_w_glu"], ab=(ab_r.reshape(-1), ab_i.reshape(-1)), vjp=disc_vjp)
    pw, tr, ti = _cpow_tables(*s5["ab"])
    y_a, xr, xi = _s5_fwd(proj, s5["bdr"], s5["bdi"], pw, tr, ti, s5["cdr"], s5["cdi"], s5["d"], s5["wglu"], name="s5_fwd")
    sv.update(s5=s5, xr=xr, xi=xi, y_a=y_a)
    pool_s = sp["pool_scale"][l].reshape(len(POOL_WINDOWS), 1, -1)
    y_b = _pool_fwd(proj, sp["pool_w"][l], pool_s, name="pool_fwd")
    sv.update(y_b=y_b, pool_s=pool_s)
    hc = _conv1_fwd(proj, bw["conv_w_dw"], sp["conv_b_dw"][l][None], name="conv_dw_fwd")
    y_c = _conv2_fwd(hc, sp["conv_ln_g"][l][None], sp["conv_ln_b"][l][None], bw["conv_w_pw"], name="conv_pw_fwd")
    sv.update(hc=hc, y_c=y_c)
    y_d, lse_d = _datt_fwd(proj, bias, name="att_fwd")
    sv.update(y_d=y_d, lse_d=lse_d)
    yn = _grp_fwd([y_a, y_b, y_c, y_d], sp["grp_norm_g"][l][None], name="grp_fwd")
    bw.rest(yn)
    h1 = _mm(yn, bw["w_out"], "nn", res=h, name="proj_out")
    sv.update(yn=yn, h1=h1)
    hx = _rms_fwd(h1, sp["norm_x_g"][l][None], name="norm_x")
    q_x = _mm(hx, bw["w_xq"], "nn", out_dtype=BF16, name="xq")
    k_x = _mm(mem_n, bw["w_xk"], "nn", out_dtype=BF16, name="xk")
    v_x = _mm(mem_n, bw["w_xv"], "nn", out_dtype=BF16, name="xv")
    o_x = _xatt_fwd(q_x, k_x, v_x, name="xatt_fwd")
    h2 = _mm(o_x, bw["w_xo"], "nn", res=h1, name="xo")
    sv.update(hx=hx, q_x=q_x, k_x=k_x, v_x=v_x, o_x=o_x, h2=h2)
    hm = _rms_fwd(h2, sp["norm_mlp_g"][l][None], name="norm_mlp")
    up, act = _mm(hm, bw["w_up"], "nn", epi="relu2", out_dtype=BF16, name="mlp_up")
    h3 = _mm(act, bw["w_down"], "nn", res=h2, name="mlp_down")
    sv.update(hm=hm, up=up, act=act)
    return h3, sv


def _stack_rows(g):
    return g.reshape(N_CHIPS, g.shape[0] // N_CHIPS, g.shape[1])


def _layer_bwd(dh3, d_memn, mem_n, bias, sp, bw, sv, l):
    L, D = dh3.shape
    GW = D // N_MIXERS
    G = GW // S5_CH
    E = GW // ATT_HEADS
    gs, gb = {}, {}
    d_up = _mm(dh3, bw["w_down"], "nt", epi="relu2_bwd", aux=sv["up"], out_dtype=BF16, name="mlp_down_dx")
    gb["w_down"] = _stack_rows(_mm(sv["act"], dh3, "tn", out_dtype=BF16, name="mlp_down_dw"))
    gb["w_up"] = _mm(sv["hm"], d_up, "tn", out_dtype=BF16, out_stack=N_CHIPS, name="mlp_up_dw")
    d_hm = _mm(d_up, bw["w_up"], "nt", name="mlp_up_dx")
    dh2, gs["norm_mlp_g"] = _rms_bwd(sv["h2"], sp["norm_mlp_g"][l][None], d_hm, dh3, name="norm_mlp_bwd")
    d_ox = _mm(dh2, bw["w_xo"], "nt", name="xo_dx")
    gb["w_xo"] = _mm(sv["o_x"], dh2, "tn", out_dtype=BF16, out_stack=N_CHIPS, name="xo_dw")
    dq_x, dk_x, dv_x = _xatt_bwd(sv["q_x"], sv["k_x"], sv["v_x"], d_ox, name="xatt_bwd")
    gb["w_xq"] = _stack_rows(_mm(sv["hx"], dq_x, "tn", out_dtype=BF16, name="xq_dw"))
    gb["w_xk"] = _stack_rows(_mm(mem_n, dk_x, "tn", out_dtype=BF16, name="xk_dw"))
    gb["w_xv"] = _stack_rows(_mm(mem_n, dv_x, "tn", out_dtype=BF16, name="xv_dw"))
    d_memn = _mm(dk_x, bw["w_xk"], "nt", res=d_memn, name="xk_dx")
    d_memn = _mm(dv_x, bw["w_xv"], "nt", res=d_memn, name="xv_dx")
    d_hx = _mm(dq_x, bw["w_xq"], "nt", name="xq_dx")
    dh1, gs["norm_x_g"] = _rms_bwd(sv["h1"], sp["norm_x_g"][l][None], d_hx, dh2, name="norm_x_bwd")
    d_yn = _mm(dh1, bw["w_out"], "nt", name="proj_out_dx")
    gb["w_out"] = _stack_rows(_mm(sv["yn"], dh1, "tn", out_dtype=BF16, name="proj_out_dw"))
    ys = [sv["y_a"], sv["y_b"], sv["y_c"], sv["y_d"]]
    dya, dyb, dyc, dyd, gs["grp_norm_g"] = _grp_bwd(ys, sp["grp_norm_g"][l][None], d_yn, name="grp_bwd")
    dq, dk, dv, dbias = _datt_bwd(sv["proj"], bias, sv["y_d"], sv["lse_d"], dyd, name="att_bwd")
    dhc, gs["conv_ln_g"], gs["conv_ln_b"], g_pw = _conv2_bwd(sv["hc"], sp["conv_ln_g"][l][None], sp["conv_ln_b"][l][None],
                                                               bw["conv_w_pw"], dyc, name="conv_pw_bwd")
    gb["conv_w_pw"] = _stack_rows(g_pw)
    dval, dgate, g_dw, gs["conv_b_dw"] = _conv1_bwd(sv["proj"], dhc, bw["conv_w_dw"], name="conv_dw_bwd")
    gb["conv_w_dw"] = jnp.transpose(g_dw.reshape(CONV_PAD, N_CHIPS, GW // N_CHIPS), (1, 0, 2))
    du_b, gs["pool_w"], g_ps = _pool_bwd(sv["proj"], dyb, sp["pool_w"][l], sv["pool_s"], name="pool_bwd")
    gs["pool_scale"] = g_ps.reshape(-1)
    s5 = sv["s5"]
    conj = (s5["ab"][0], -s5["ab"][1])
    pw, tr, ti = _cpow_tables(*conj)
    du_a, g_glu, g_d, dcdr, dcdi, dbdr, dbdi, dar, dai = _s5_bwd(
        sv["proj"], sv["xr"], sv["xi"], dya, s5["bdr"], s5["bdi"], pw, tr[::-1], ti[::-1], s5["cdr"], s5["cdi"], s5["d"], s5["wglu"],
        name="s5_bwd")
    gb["s5_w_glu"] = _stack_rows(g_glu)
    gs["s5_d"] = g_d.reshape(-1)
    gs["s5_c_re"] = jnp.transpose(_bd_extract(dcdr, G), (0, 2, 1))
    gs["s5_c_im"] = jnp.transpose(_bd_extract(dcdi, G), (0, 2, 1))
    d_bb_r = jnp.transpose(_bd_extract(dbdr, G), (0, 2, 1))
    d_bb_i = jnp.transpose(_bd_extract(dbdi, G), (0, 2, 1))
    d_ab_r = jnp.sum(dar, axis=0).reshape(G, S5_STATE)
    d_ab_i = jnp.sum(dai, axis=0).reshape(G, S5_STATE)
    gs["s5_lam_re"], gs["s5_lam_im"], gs["s5_log_dt"], gs["s5_b_re"], gs["s5_b_im"] = s5["vjp"]((d_ab_r, d_ab_i, d_bb_r, d_bb_i))
    d_proj = jnp.concatenate([du_a, du_b, dval, dgate, dq, dk, dv], axis=1)
    gb["w_in"] = _mm(sv["xn"], d_proj, "tn", out_dtype=BF16, out_stack=N_CHIPS, name="proj_in_dw")
    d_xn = _mm(d_proj, bw["w_in"], "nt", name="proj_in_dx")
    dh0, gs["norm_mix_g"] = _rms_bwd(sv["h"], sp["norm_mix_g"][l][None], d_xn, dh1, name="norm_mix_bwd")
    gs = {n: g.reshape(sp[n].shape[1:]) for n, g in gs.items()}
    return dh0, d_memn, dbias, gs, gb


def _device_step(x, mem, target, sp, get_big, on_grads):
    nl = sp["norm_mix_g"].shape[0]
    mem_n = _rms_fwd(mem, sp["mem_norm_g"][None], name="norm_mem")
    bias = _att_bias(sp["rel_bias"])
    h, saved, big = x, [], {n: [] for n in SHARDED}
    for l in range(nl):
        bw = get_big(l, h)
        h, sv = _layer_fwd(h, mem_n, bias, sp, bw, l)
        saved.append(sv)
        for n in SHARDED:
            big[n].append(bw[n])
    loss, dh, g_final = _loss_head(h, sp["norm_final_g"][None], target, name="loss_head")
    d_memn = jnp.zeros(mem.shape, F32)
    dbias = jnp.zeros(bias.shape, F32)
    sg = {n: [None] * nl for n in SMALL}
    for l in reversed(range(nl)):
        dh, d_memn, dbias_l, gs, gb = _layer_bwd(dh, d_memn, mem_n, bias, sp, {n: big[n][l] for n in SHARDED}, saved[l], l)
        dbias = dbias + dbias_l
        for n, g in gs.items():
            sg[n][l] = g
        dh = on_grads(l, gb, dh)
    small = {n: jnp.stack(g) for n, g in sg.items() if g[0] is not None}
    small["norm_final_g"] = g_final.reshape(-1)
    _, g_mem = _rms_bwd(mem, sp["mem_norm_g"][None], d_memn, None, name="norm_mem_bwd")
    small["mem_norm_g"] = g_mem.reshape(-1)
    small["rel_bias"] = _rel_bias_grad(dbias)
    return loss, dh, small


def _gather_big(shards, prep):
    nl = shards["w_in"].shape[0]
    token = jnp.zeros((8, 128), F32)
    pending = []
    for l in range(nl):
        xs = [jnp.pad(shards[n][l], ((0, CONV_PAD - CONV_WIDTH), (0, 0))) if n == "conv_w_dw" else shards[n][l].astype(BF16)
              for n in SHARDED]
        ssem, rsem, xs, lands, token = _ag_start(xs, token, name=f"ag_start_l{l}")
        pending.append((ssem, rsem, xs, lands))

    names = list(SHARDED)

    class LayerWeights(dict):
        def __init__(self, l, after):
            super().__init__()
            self.l, self.state = l, pending[l]
            first = [i for i, n in enumerate(names) if n in AG_FIRST] if l == 0 else list(range(len(names)))
            self.left = [i for i in range(len(names)) if i not in first]
            self._complete(first, [token] + list(prep) if l == 0 else [after], "a")

        def _complete(self, which, after, tag):
            ssem, rsem, xs, lands = self.state
            xs, lands = _ag_wait(ssem, rsem, [xs[i] for i in which], [lands[i] for i in which], after, which,
                                 name=f"ag_wait_l{self.l}{tag}")
            fulls = _ag_finish(lands, xs, name=f"ag_finish_{tag if self.l == 0 else 'all'}")
            for i, x, full in zip(which, xs, fulls):
                n = names[i]
                full = full.reshape(N_CHIPS, x.shape[0], x.shape[1])
                self[n] = jnp.transpose(full, (1, 0, 2)).reshape(CONV_PAD, -1) if n == "conv_w_dw" else _mat(full, SHARDED[n])

        def rest(self, after):
            if self.left:
                self._complete(self.left, [after], "b")
                self.left = []

    return LayerWeights


def _pack_small(vals, extra=None):
    flat = [vals[n].reshape(-1).astype(F32) for n in SMALL]
    flat.append(jnp.zeros((1,), F32) if extra is None else extra.reshape(1))
    v = jnp.concatenate(flat)
    n_pad = -(-v.shape[0] // SMALL_ALIGN) * SMALL_ALIGN
    return jnp.pad(v, (0, n_pad - v.shape[0]))


def _unpack_small(flat, like):
    out, pos = {}, 0
    for n in SMALL:
        size = math.prod(like[n].shape)
        out[n] = flat[pos:pos + size].reshape(like[n].shape)
        pos += size
    return out, flat[pos]


def kernel(x, mem, rel_bias, mem_norm_g, norm_mix_g, w_in, s5_lam_re, s5_lam_im, s5_log_dt, s5_b_re, s5_b_im, s5_c_re, s5_c_im, s5_d, s5_w_glu, pool_w, pool_scale, conv_w_dw, conv_b_dw, conv_ln_g, conv_ln_b, conv_w_pw, grp_norm_g, w_out, norm_x_g, w_xq, w_xk, w_xv, w_xo, norm_mlp_g, w_up, w_down, norm_final_g, loss_target, m_rel_bias, m_mem_norm_g, m_norm_mix_g, m_w_in, m_s5_lam_re, m_s5_lam_im, m_s5_log_dt, m_s5_b_re, m_s5_b_im, m_s5_c_re, m_s5_c_im, m_s5_d, m_s5_w_glu, m_pool_w, m_pool_scale, m_conv_w_dw, m_conv_b_dw, m_conv_ln_g, m_conv_ln_b, m_conv_w_pw, m_grp_norm_g, m_w_out, m_norm_x_g, m_w_xq, m_w_xk, m_w_xv, m_w_xo, m_norm_mlp_g, m_w_up, m_w_down, m_norm_final_g, v_rel_bias, v_mem_norm_g, v_norm_mix_g, v_w_in, v_s5_lam_re, v_s5_lam_im, v_s5_log_dt, v_s5_b_re, v_s5_b_im, v_s5_c_re, v_s5_c_im, v_s5_d, v_s5_w_glu, v_pool_w, v_pool_scale, v_conv_w_dw, v_conv_b_dw, v_conv_ln_g, v_conv_ln_b, v_conv_w_pw, v_grp_norm_g, v_w_out, v_norm_x_g, v_w_xq, v_w_xk, v_w_xv, v_w_xo, v_norm_mlp_g, v_w_up, v_w_down, v_norm_final_g):
    env = dict(locals())
    w = {n: env[n] for n in WEIGHTS}
    m = {n: env["m_" + n] for n in WEIGHTS}
    v = {n: env["v_" + n] for n in WEIGHTS}
    sp = {n: w[n] for n in SMALL}
    small_wmv = [_pack_small(t).reshape(1, 2, -1, 128) for t in (w, m, v)]
    get_big = _gather_big({n: w[n] for n in SHARDED}, small_wmv)
    nl = w["w_in"].shape[0]
    names = list(SHARDED)
    reducers, reduced, tokens = {}, {}, {}

    def on_grads(l, gb, dh):
        reducers[l] = _GradReducer(f"l{l}", BF16)
        token = tokens[l] = reducers[l].start(names, [gb[n] for n in names])
        if l + 1 in reducers:
            reduced[l + 1] = reducers[l + 1].finish(dh)
        return dh + token[0, 0]

    loss, grad_x, g_small = _device_step(x[0], mem[0], loss_target[0], sp, get_big, on_grads)

    def shard_views(n):
        pad = ((0, 0), (0, CONV_PAD - CONV_WIDTH), (0, 0)) if n == "conv_w_dw" else None
        wmv = [jnp.pad(t[n], pad) if pad else t[n] for t in (w, m, v)]
        _, R, C = wmv[0].shape
        return [t.reshape(nl, 2, R // 2, C) for t in wmv]

    grad, delta, new_m, new_v = {}, {}, {}, {}
    busy, upper = [grad_x], {}
    if nl > 1:
        filled = _sib_fill([reduced[l][n] for n in names for l in range(1, nl)], [tokens[0]], name="rs_fill_upper")
        for i, n in enumerate(names):
            upper[n] = _adamw(*shard_views(n), filled[i * (nl - 1):(i + 1) * (nl - 1)], first=1, name=f"adamw_upper_{n}")
            busy.append(upper[n][0])
    reduced[0] = reducers[0].finish(busy)
    packed = _pack_small(g_small, loss).reshape(N_CHIPS, -1, 128)
    small = _GradReducer("small", F32)
    token = small.start(["small"], [packed], after=[reduced[0][names[0]]])
    quarter = _sib_fill([small.finish(token)["small"]], name="rs_fill_small")[0]
    total = _ag4(quarter.reshape(-1, 128), name="ag_small").reshape(2, -1, 128)
    outs = _adamw(*small_wmv, [total], name="adamw_small")
    filled = _sib_fill([reduced[0][n] for n in names], name="rs_fill_first")
    for i, n in enumerate(names):
        res = _adamw(*shard_views(n), [filled[i]], prev=upper.get(n), name=f"adamw_first_{n}")
        R = w[n].shape[1]
        grad[n], delta[n], new_m[n], new_v[n] = [o.reshape(nl, -1, o.shape[-1])[:, :R] for o in res]
    loss_sum = None
    for o, dst in zip(outs, (grad, delta, new_m, new_v)):
        vals, last = _unpack_small(o.reshape(-1), sp)
        dst.update(vals)
        loss_sum = last if loss_sum is None else loss_sum
    return (loss_sum, grad_x[None], *[grad[n] for n in WEIGHTS], *[delta[n] for n in WEIGHTS],
            *[new_m[n] for n in WEIGHTS], *[new_v[n] for n in WEIGHTS])
```

```python
import functools
import math

import jax
import jax.numpy as jnp
import numpy as np
from jax import lax
from jax.experimental import pallas as pl
from jax.experimental.pallas import tpu as pltpu

F32 = jnp.float32
BF16 = jnp.bfloat16
MESH = pl.DeviceIdType.MESH

N_MIXERS = 4
S5_CH = 16
S5_STATE = 64
POOL_WINDOWS = (2, 4, 8, 16)
CONV_WIDTH = 31
CONV_PAD = 32
ATT_HEADS = 8
ATT_BLOCK = 128
DILATED_PATTERNS = ((128, 1), (512, 4), (2048, 16))
REL_BUCKETS = 32
REL_MAX_DIST = 2048
X_HEADS = 4
X_HEAD_DIM = 128
NORM_EPS = 1e-6
NEG_INF = -1e30
ADAM_LR, ADAM_B1, ADAM_B2, ADAM_EPS, ADAM_WD, ADAM_STEP = 0.001, 0.9, 0.999, 1e-08, 0.01, 10
N_CHIPS = 4
VMEM_LIMIT = 56 * 1024 * 1024


def _cparams(sem=None, vmem=None):
    return pltpu.CompilerParams(dimension_semantics=sem, vmem_limit_bytes=vmem)


def _tile(n, pref):
    if n <= pref:
        return n
    t = pref
    while t >= 128:
        if n % t == 0:
            return t
        t -= 128
    return n


def _spec(arr, tr, tc, rc):
    if arr.ndim == 2:
        return pl.BlockSpec((tr, tc), lambda *g: rc(*g))
    per = arr.shape[2] // tc
    assert arr.shape[2] % tc == 0

    def imap(*g):
        r, c = rc(*g)
        return (c // per, r, c % per)

    return pl.BlockSpec((None, tr, tc), imap)


def _lshape(arr):
    return arr.shape if arr.ndim == 2 else (arr.shape[1], arr.shape[0] * arr.shape[2])


def _mm(a, b, mode, *, name, out_dtype=F32, res=None, epi=None, aux=None, out_stack=None, tm=1024, tn=1024, tk=2048):
    la, lb = _lshape(a), _lshape(b)
    if mode == "nn":
        (M, K), (K2, N) = la, lb
    elif mode == "nt":
        (M, K), (N, K2) = la, lb
    else:
        (K, M), (K2, N) = la, lb
    assert K == K2, (la, lb, mode)
    lim = {"m": M, "n": N // out_stack if out_stack else N, "k": K}
    stacked = [(a, "m" if mode == "tn" else "k"), (b, "k" if mode == "nt" else "n"), (res, "n"), (aux, "n")]
    for arr, dim in stacked:
        if arr is not None and arr.ndim == 3:
            lim[dim] = math.gcd(lim[dim], arr.shape[2])
    tm, tn, tk = _tile(lim["m"], tm), _tile(lim["n"], tn), _tile(lim["k"], tk)
    nk = K // tk
    grid = (M // tm, N // tn, nk)
    a_spec = _spec(a, tk, tm, lambda i, j, k: (k, i)) if mode == "tn" else _spec(a, tm, tk, lambda i, j, k: (i, k))
    b_spec = _spec(b, tn, tk, lambda i, j, k: (j, k)) if mode == "nt" else _spec(b, tk, tn, lambda i, j, k: (k, j))
    dims = {"nn": ((1,), (0,)), "nt": ((1,), (1,)), "tn": ((0,), (0,))}[mode]
    ins, in_specs = [a, b], [a_spec, b_spec]
    for extra in (res, aux):
        if extra is not None:
            ins.append(extra)
            in_specs.append(_spec(extra, tm, tn, lambda i, j, k: (i, j)))
    if out_stack:
        o_shape = jax.ShapeDtypeStruct((out_stack, M, N // out_stack), out_dtype)
    else:
        o_shape = jax.ShapeDtypeStruct((M, N), out_dtype)
    o_spec = _spec(o_shape, tm, tn, lambda i, j, k: (i, j))
    n_out = 2 if epi == "relu2" else 1
    has_res, has_aux = res is not None, aux is not None

    def body(*refs):
        a_ref, b_ref = refs[0], refs[1]
        pos = 2
        res_ref = aux_ref = None
        if has_res:
            res_ref = refs[pos]
            pos += 1
        if has_aux:
            aux_ref = refs[pos]
            pos += 1
        outs = refs[pos:pos + n_out]
        part = lax.dot_general(a_ref[...].astype(BF16), b_ref[...].astype(BF16), (dims, ((), ())), preferred_element_type=F32)
        if nk > 1:
            acc_ref = refs[pos + n_out]
            k = pl.program_id(2)

            @pl.when(k == 0)
            def _():
                acc_ref[...] = part

            @pl.when(k > 0)
            def _():
                acc_ref[...] += part

        @pl.when(pl.program_id(2) == nk - 1)
        def _():
            o = acc_ref[...] if nk > 1 else part
            if has_res:
                o = o + res_ref[...].astype(F32)
            if epi == "relu2":
                outs[0][...] = o.astype(outs[0].dtype)
                r = jnp.maximum(o, 0.0)
                outs[1][...] = (r * r).astype(outs[1].dtype)
            elif epi == "relu2_bwd":
                outs[0][...] = (o * (2.0 * jnp.maximum(aux_ref[...].astype(F32), 0.0))).astype(outs[0].dtype)
            else:
                outs[0][...] = o.astype(outs[0].dtype)

    out = pl.pallas_call(
        body, grid=grid, in_specs=in_specs,
        out_specs=[o_spec] * n_out if n_out > 1 else o_spec,
        out_shape=[o_shape] * n_out if n_out > 1 else o_shape,
        scratch_shapes=[pltpu.VMEM((tm, tn), F32)] if nk > 1 else [],
        compiler_params=_cparams(("parallel", "parallel", "arbitrary"), VMEM_LIMIT), name=name,
    )(*ins)
    return out


def _rms_fwd(x, g, *, name, out_dtype=BF16):
    L, D = x.shape
    tr = _tile(L, 256)

    def body(x_ref, g_ref, o_ref):
        xv = x_ref[...]
        r = lax.rsqrt(jnp.mean(xv * xv, axis=-1, keepdims=True) + NORM_EPS)
        o_ref[...] = (xv * r * g_ref[...]).astype(o_ref.dtype)

    return pl.pallas_call(
        body, grid=(L // tr,),
        in_specs=[pl.BlockSpec((tr, D), lambda i: (i, 0)), pl.BlockSpec((1, D), lambda i: (0, 0))],
        out_specs=pl.BlockSpec((tr, D), lambda i: (i, 0)),
        out_shape=jax.ShapeDtypeStruct((L, D), out_dtype),
        compiler_params=_cparams(("parallel",)), name=name)(x, g)


def _rms_bwd(x, g, dy, dres, *, name):
    L, D = x.shape
    tr = _tile(L, 256)
    has_res = dres is not None

    def body(*refs):
        x_ref, g_ref, dy_ref = refs[:3]
        dres_ref = refs[3] if has_res else None
        dx_ref, dg_ref = refs[-2:]
        xv = x_ref[...]
        dyv = dy_ref[...].astype(F32)
        r = lax.rsqrt(jnp.mean(xv * xv, axis=-1, keepdims=True) + NORM_EPS)
        xh = xv * r
        dyg = dyv * g_ref[...]
        dx = r * (dyg - xh * jnp.mean(dyg * xh, axis=-1, keepdims=True))
        if has_res:
            dx = dx + dres_ref[...]
        dx_ref[...] = dx

        @pl.when(pl.program_id(0) == 0)
        def _():
            dg_ref[...] = jnp.zeros_like(dg_ref)

        dg_ref[...] += jnp.sum(dyv * xh, axis=0, keepdims=True)

    row = pl.BlockSpec((tr, D), lambda i: (i, 0))
    vec = pl.BlockSpec((1, D), lambda i: (0, 0))
    ins = [x, g, dy] + ([dres] if has_res else [])
    return pl.pallas_call(
        body, grid=(L // tr,), in_specs=[row, vec, row] + ([row] if has_res else []),
        out_specs=[row, vec],
        out_shape=[jax.ShapeDtypeStruct((L, D), F32), jax.ShapeDtypeStruct((1, D), F32)],
        compiler_params=_cparams(("arbitrary",)), name=name)(*ins)


def _loss_head(h, g, target, *, name):
    L, D = h.shape
    tr = _tile(L, 256)

    def body(x_ref, g_ref, t_ref, loss_ref, dx_ref, dg_ref):
        xv = x_ref[...]
        r = lax.rsqrt(jnp.mean(xv * xv, axis=-1, keepdims=True) + NORM_EPS)
        xh = xv * r
        err = xh * g_ref[...] - t_ref[...]
        dyv = err * (1.0 / D)
        dyg = dyv * g_ref[...]
        dx_ref[...] = r * (dyg - xh * jnp.mean(dyg * xh, axis=-1, keepdims=True))

        @pl.when(pl.program_id(0) == 0)
        def _():
            dg_ref[...] = jnp.zeros_like(dg_ref)
            loss_ref[...] = jnp.zeros_like(loss_ref)

        dg_ref[...] += jnp.sum(dyv * xh, axis=0, keepdims=True)
        loss_ref[...] += 0.5 * jnp.sum(jnp.sum(err * err, axis=1, keepdims=True), axis=0, keepdims=True) * (1.0 / D)

    row = pl.BlockSpec((tr, D), lambda i: (i, 0))
    vec = pl.BlockSpec((1, D), lambda i: (0, 0))
    return pl.pallas_call(
        body, grid=(L // tr,), in_specs=[row, vec, row],
        out_specs=[pl.BlockSpec((1, 1), lambda i: (0, 0)), row, vec],
        out_shape=[jax.ShapeDtypeStruct((1, 1), F32), jax.ShapeDtypeStruct((L, D), F32), jax.ShapeDtypeStruct((1, D), F32)],
        compiler_params=_cparams(("arbitrary",)), name=name)(h, g, target)


S5_TL = 256
S5_LW = 512


def _dot(a, b, dims):
    return lax.dot_general(a, b, (dims, ((), ())), preferred_element_type=F32)


_NN, _NT, _TN = ((1,), (0,)), ((1,), (1,)), ((0,), (0,))


def _gelu_and_grad(y):
    k = math.sqrt(2.0 / math.pi)
    y2 = y * y
    th = jnp.tanh(k * (y + 0.044715 * y * y2))
    g = 0.5 * y * (1.0 + th)
    gp = 0.5 * (1.0 + th) + 0.5 * y * (1.0 - th * th) * k * (1.0 + 3.0 * 0.044715 * y2)
    return g, gp


def _scan_tiles(re_s, im_s, ls, lw, pw_ref, tr_ref, ti_ref, carry_s, nt, reverse, extra=None):
    row = lax.broadcasted_iota(jnp.int32, (8, lw), 0)
    a = [pw_ref[k:k + 1, ls] for k in range(6)]
    tr_t, ti_t = tr_ref[:, ls], ti_ref[:, ls]
    edge = 0 if reverse else 7

    def tile(jj, carry):
        cr, ci, acc = carry
        j = (nt - 1 - jj) if reverse else jj
        off = pl.multiple_of(j * 8, 8)
        sr, si = re_s[pl.ds(off, 8), ls], im_s[pl.ds(off, 8), ls]
        for n, k in enumerate((1, 2, 4)):
            akr, aki = a[2 * n], a[2 * n + 1]
            if reverse:
                keep, sh = row < 8 - k, 8 - k
            else:
                keep, sh = row >= k, k
            shr = jnp.where(keep, pltpu.roll(sr, sh, 0), 0.0)
            shi = jnp.where(keep, pltpu.roll(si, sh, 0), 0.0)
            sr, si = sr + akr * shr - aki * shi, si + akr * shi + aki * shr
        xr = sr + tr_t * cr - ti_t * ci
        xi = si + tr_t * ci + ti_t * cr
        re_s[pl.ds(off, 8), ls] = xr
        im_s[pl.ds(off, 8), ls] = xi
        if extra is not None:
            acc = extra(off, xr, xi, acc, row)
        return xr[edge:edge + 1, :], xi[edge:edge + 1, :], acc

    acc0 = (jnp.zeros((8, lw), F32), jnp.zeros((8, lw), F32)) if extra is not None else 0
    cr, ci, acc = lax.fori_loop(0, nt, tile, (carry_s[0:1, ls], carry_s[1:2, ls], acc0))
    carry_s[0:1, ls] = cr
    carry_s[1:2, ls] = ci
    return acc


def _s5_fwd(proj, bdr, bdi, pw, tr, ti, cdr, cdi, dskip, wglu, *, name):
    L = proj.shape[0]
    GW, S = bdr.shape
    TL = _tile(L, S5_TL)
    lw = min(S, S5_LW)

    def body(u_ref, bdr_ref, bdi_ref, pw_ref, tr_ref, ti_ref, cdr_ref, cdi_ref, d_ref, w_ref,
             y_ref, xr_ref, xi_ref, re_s, im_s, carry_s):
        @pl.when(pl.program_id(0) == 0)
        def _():
            carry_s[...] = jnp.zeros_like(carry_s)

        u = u_ref[...]
        ub = u.astype(BF16)
        re_s[...] = _dot(ub, bdr_ref[...], _NN)
        im_s[...] = _dot(ub, bdi_ref[...], _NN)
        for lc in range(S // lw):
            _scan_tiles(re_s, im_s, slice(lc * lw, (lc + 1) * lw), lw, pw_ref, tr_ref, ti_ref, carry_s, TL // 8, False)
        xrb, xib = re_s[...].astype(BF16), im_s[...].astype(BF16)
        xr_ref[...] = xrb
        xi_ref[...] = xib
        y = _dot(xrb, cdr_ref[...], _NN) - _dot(xib, cdi_ref[...], _NN) + d_ref[...] * u
        g, _ = _gelu_and_grad(y)
        z = _dot(g.astype(BF16), w_ref[...], _NN)
        y_ref[...] = g * jax.nn.sigmoid(z)

    full = lambda arr: pl.BlockSpec(arr.shape, lambda i: (0,) * arr.ndim)
    return pl.pallas_call(
        body, grid=(L // TL,),
        in_specs=[pl.BlockSpec((TL, GW), lambda i: (i, 0))] + [full(t) for t in (bdr, bdi, pw, tr, ti, cdr, cdi, dskip, wglu)],
        out_specs=[pl.BlockSpec((TL, GW), lambda i: (i, 0)), pl.BlockSpec((TL, S), lambda i: (i, 0)), pl.BlockSpec((TL, S), lambda i: (i, 0))],
        out_shape=[jax.ShapeDtypeStruct((L, GW), F32), jax.ShapeDtypeStruct((L, S), BF16), jax.ShapeDtypeStruct((L, S), BF16)],
        scratch_shapes=[pltpu.VMEM((TL, S), F32), pltpu.VMEM((TL, S), F32), pltpu.VMEM((8, S), F32)],
        compiler_params=_cparams(("arbitrary",), VMEM_LIMIT), name=name,
    )(proj, bdr, bdi, pw, tr, ti, cdr, cdi, dskip, wglu)


def _s5_bwd(proj, xr, xi, dout, bdr, bdi, pwc, trc, tic, cdr, cdi, dskip, wglu, *, name):
    L = proj.shape[0]
    GW, S = bdr.shape
    TL = _tile(L, S5_TL)
    nc = L // TL
    lw = min(S, S5_LW)

    def body(u_ref, xr_ref, xi_ref, xrp_ref, xip_ref, do_ref, bdr_ref, bdi_ref, pw_ref, tr_ref, ti_ref, cdr_ref, cdi_ref,
             d_ref, w_ref, du_ref, dw_ref, dd_ref, dcdr_ref, dcdi_ref, dbdr_ref, dbdi_ref, dar_ref, dai_ref,
             gr_s, gi_s, xfr, xfi, carry_s):
        i = pl.program_id(0)

        @pl.when(i == 0)
        def _():
            carry_s[...] = jnp.zeros_like(carry_s)
            for r in (dw_ref, dd_ref, dcdr_ref, dcdi_ref, dbdr_ref, dbdi_ref, dar_ref, dai_ref):
                r[...] = jnp.zeros_like(r)

        u = u_ref[...]
        ub = u.astype(BF16)
        xrb, xib = xr_ref[...], xi_ref[...]
        y = _dot(xrb, cdr_ref[...], _NN) - _dot(xib, cdi_ref[...], _NN) + d_ref[...] * u
        g, gp = _gelu_and_grad(y)
        gb = g.astype(BF16)
        sg = jax.nn.sigmoid(_dot(gb, w_ref[...], _NN))
        dout = do_ref[...]
        dz = (dout * g * sg * (1.0 - sg)).astype(BF16)
        dg = dout * sg + _dot(dz, w_ref[...], _NT)
        dw_ref[...] += _dot(gb, dz, _TN)
        dy = dg * gp
        dyb = dy.astype(BF16)
        dd_ref[...] += jnp.sum(dy * u, axis=0, keepdims=True)
        gr_s[...] = _dot(dyb, cdr_ref[...], _NT)
        gi_s[...] = -_dot(dyb, cdi_ref[...], _NT)
        dcdr_ref[...] += _dot(xrb, dyb, _TN)
        dcdi_ref[...] -= _dot(xib, dyb, _TN)
        live = (i < nc - 1).astype(F32)
        xfr[0:8, :] = xrp_ref[...].astype(F32) * live
        xfi[0:8, :] = xip_ref[...].astype(F32) * live
        xfr[8:, :] = xrb.astype(F32)
        xfi[8:, :] = xib.astype(F32)
        for lc in range(S // lw):
            ls = slice(lc * lw, (lc + 1) * lw)

            def extra(off, lr, li, acc, row, ls=ls):
                cur_r, cur_i = xfr[pl.ds(off + 8, 8), ls], xfi[pl.ds(off + 8, 8), ls]
                prv_r, prv_i = xfr[pl.ds(off, 8), ls], xfi[pl.ds(off, 8), ls]
                xpr = jnp.where(row >= 1, pltpu.roll(cur_r, 1, 0), prv_r[7:8, :])
                xpi = jnp.where(row >= 1, pltpu.roll(cur_i, 1, 0), prv_i[7:8, :])
                return acc[0] + lr * xpr + li * xpi, acc[1] - lr * xpi + li * xpr

            acc = _scan_tiles(gr_s, gi_s, ls, lw, pw_ref, tr_ref, ti_ref, carry_s, TL // 8, True, extra)
            dar_ref[:, ls] += acc[0]
            dai_ref[:, ls] += acc[1]
        lrb, lib = gr_s[...].astype(BF16), gi_s[...].astype(BF16)
        du_ref[...] = dy * d_ref[...] + _dot(lrb, bdr_ref[...], _NT) + _dot(lib, bdi_ref[...], _NT)
        dbdr_ref[...] += _dot(ub, lrb, _TN)
        dbdi_ref[...] += _dot(ub, lib, _TN)

    rev = lambda i: nc - 1 - i
    full = lambda arr: pl.BlockSpec(arr.shape, lambda i: (0,) * arr.ndim)
    chunk = lambda w: pl.BlockSpec((TL, w), lambda i: (rev(i), 0))
    prev8 = pl.BlockSpec((8, S), lambda i: (jnp.maximum(rev(i) * (TL // 8) - 1, 0), 0))
    acc_shapes = [(GW, GW), (1, GW), (S, GW), (S, GW), (GW, S), (GW, S), (8, S), (8, S)]
    return pl.pallas_call(
        body, grid=(nc,),
        in_specs=[chunk(GW), chunk(S), chunk(S), prev8, prev8, chunk(GW)] + [full(t) for t in (bdr, bdi, pwc, trc, tic, cdr, cdi, dskip, wglu)],
        out_specs=[chunk(GW)] + [pl.BlockSpec(s, lambda i: (0, 0)) for s in acc_shapes],
        out_shape=[jax.ShapeDtypeStruct((L, GW), F32)] + [jax.ShapeDtypeStruct(s, F32) for s in acc_shapes],
        scratch_shapes=[pltpu.VMEM((TL, S), F32), pltpu.VMEM((TL, S), F32), pltpu.VMEM((TL + 8, S), F32),
                        pltpu.VMEM((TL + 8, S), F32), pltpu.VMEM((8, S), F32)],
        compiler_params=_cparams(("arbitrary",), VMEM_LIMIT), name=name,
    )(proj, xr, xi, xr, xi, dout, bdr, bdi, pwc, trc, tic, cdr, cdi, dskip, wglu)


def _cpow_tables(ar, ai):
    def cm(a, b):
        return a[0] * b[0] - a[1] * b[1], a[0] * b[1] + a[1] * b[0]
    p = [(ar, ai)]
    for _ in range(7):
        p.append(cm(p[-1], p[0]))
    z = jnp.zeros_like(ar)
    pw = jnp.stack([p[0][0], p[0][1], p[1][0], p[1][1], p[3][0], p[3][1], z, z])
    return pw, jnp.stack([q[0] for q in p]), jnp.stack([q[1] for q in p])


def _s5_disc(lam_re, lam_im, log_dt, b_re, b_im):
    dt = jnp.exp(log_dt)[:, None]
    mag = jnp.exp(lam_re * dt)
    ab_r, ab_i = mag * jnp.cos(lam_im * dt), mag * jnp.sin(lam_im * dt)
    den = lam_re * lam_re + lam_im * lam_im
    nr, ni = ab_r - 1.0, ab_i
    f_r = ((nr * lam_re + ni * lam_im) / den)[..., None]
    f_i = ((ni * lam_re - nr * lam_im) / den)[..., None]
    return ab_r, ab_i, f_r * b_re - f_i * b_im, f_r * b_im + f_i * b_re


def _bd(t):
    G, A, B = t.shape
    return (t[:, :, None, :] * jnp.eye(G, dtype=t.dtype)[:, None, :, None]).reshape(G * A, G * B)


def _bd_extract(m, G):
    A, B = m.shape[0] // G, m.shape[1] // G
    return jnp.sum(m.reshape(G, A, G, B) * jnp.eye(G, dtype=m.dtype)[:, None, :, None], axis=2)


POOL_TQ = 256


def _band(shape, row0, col0, win, transpose):
    r = lax.broadcasted_iota(jnp.int32, shape, 0) + row0
    c = lax.broadcasted_iota(jnp.int32, shape, 1) + col0
    d = (c - r) if transpose else (r - c)
    return ((d >= 0) & (d < win)).astype(BF16)


def _band_dot(band, v):
    hi = v.astype(BF16)
    lo = (v - hi.astype(F32)).astype(BF16)
    return _dot(band, hi, _NN) + _dot(band, lo, _NN)


def _pool_p(u_prev, u_cur, i, win, tq):
    t0 = i * tq
    cnt = jnp.minimum(lax.broadcasted_iota(jnp.int32, (tq, 1), 0) + t0 + 1, win).astype(F32)
    live = (i > 0).astype(F32)
    acc = _band_dot(_band((tq, tq), t0, t0, win, False), u_cur)
    acc += _band_dot(_band((tq, tq), t0, t0 - tq, win, False), u_prev * live)
    return acc / cnt - u_cur


def _pool_fwd(proj, pool_w, pool_scale3, *, name):
    L = proj.shape[0]
    nw, PC, _ = pool_w.shape
    tq = _tile(L, POOL_TQ)

    def body(up_ref, uc_ref, w_ref, s_ref, y_ref):
        g, i = pl.program_id(0), pl.program_id(1)
        win = jnp.left_shift(2, g)
        p = _pool_p(up_ref[...], uc_ref[...], i, win, tq)
        y_ref[...] = _dot(p.astype(BF16), w_ref[...].astype(BF16), _NN) * s_ref[...]

    return pl.pallas_call(
        body, grid=(nw, L // tq),
        in_specs=[pl.BlockSpec((tq, PC), lambda g, i: (jnp.maximum(i - 1, 0), nw + g)),
                  pl.BlockSpec((tq, PC), lambda g, i: (i, nw + g)),
                  pl.BlockSpec((None, PC, PC), lambda g, i: (g, 0, 0)),
                  pl.BlockSpec((None, 1, PC), lambda g, i: (g, 0, 0))],
        out_specs=pl.BlockSpec((tq, PC), lambda g, i: (i, g)),
        out_shape=jax.ShapeDtypeStruct((L, nw * PC), F32),
        compiler_params=_cparams(("parallel", "parallel")), name=name)(proj, proj, pool_w, pool_scale3)


def _pool_bwd(proj, dy, pool_w, pool_scale3, *, name):
    L = proj.shape[0]
    nw, PC, _ = pool_w.shape
    tq = _tile(L, POOL_TQ)
    nq = L // tq

    def body(up_ref, uc_ref, dyc_ref, dyn_ref, w_ref, s_ref, du_ref, dw_ref, ds_ref):
        g, i = pl.program_id(0), pl.program_id(1)
        win = jnp.left_shift(2, g)

        @pl.when(i == 0)
        def _():
            dw_ref[...] = jnp.zeros_like(dw_ref)
            ds_ref[...] = jnp.zeros_like(ds_ref)

        wb = w_ref[...].astype(BF16)
        p = _pool_p(up_ref[...], uc_ref[...], i, win, tq).astype(BF16)
        dyc = dyc_ref[...]
        ds_ref[...] += jnp.sum(dyc * _dot(p, wb, _NN), axis=0, keepdims=True)
        dys = (dyc * s_ref[...]).astype(BF16)
        dw_ref[...] += _dot(p, dys, _TN)
        t0 = i * tq
        rows = lax.broadcasted_iota(jnp.int32, (tq, 1), 0)
        dp_c = _dot(dys, wb, _NT)
        q_c = dp_c / jnp.minimum(rows + t0 + 1, win).astype(F32)
        live = (i < nq - 1).astype(F32)
        dp_n = _dot((dyn_ref[...] * s_ref[...] * live).astype(BF16), wb, _NT)
        q_n = dp_n / jnp.minimum(rows + t0 + tq + 1, win).astype(F32)
        du = _band_dot(_band((tq, tq), t0, t0, win, True), q_c) + _band_dot(_band((tq, tq), t0, t0 + tq, win, True), q_n)
        du_ref[...] = du - dp_c

    return pl.pallas_call(
        body, grid=(nw, nq),
        in_specs=[pl.BlockSpec((tq, PC), lambda g, i: (jnp.maximum(i - 1, 0), nw + g)),
                  pl.BlockSpec((tq, PC), lambda g, i: (i, nw + g)),
                  pl.BlockSpec((tq, PC), lambda g, i: (i, g)),
                  pl.BlockSpec((tq, PC), lambda g, i: (jnp.minimum(i + 1, nq - 1), g)),
                  pl.BlockSpec((None, PC, PC), lambda g, i: (g, 0, 0)),
                  pl.BlockSpec((None, 1, PC), lambda g, i: (g, 0, 0))],
        out_specs=[pl.BlockSpec((tq, PC), lambda g, i: (i, g)),
                   pl.BlockSpec((None, PC, PC), lambda g, i: (g, 0, 0)),
                   pl.BlockSpec((None, 1, PC), lambda g, i: (g, 0, 0))],
        out_shape=[jax.ShapeDtypeStruct((L, nw * PC), F32), jax.ShapeDtypeStruct((nw, PC, PC), F32),
                   jax.ShapeDtypeStruct((nw, 1, PC), F32)],
        compiler_params=_cparams(("parallel", "arbitrary")), name=name)(proj, proj, dy, dy, pool_w, pool_scale3)


CONV_RC = 256


def _conv1_fwd(proj, w_dw, b_dw, *, name):
    L = proj.shape[0]
    GW = w_dw.shape[1]
    CB = min(GW, 128)
    nb = GW // CB
    RC = _tile(L, CONV_RC)
    n = RC + CONV_PAD

    def body(val_ref, gate_ref, w_ref, b_ref, o_ref, hp):
        hp[0:CONV_PAD, :] = jnp.zeros((CONV_PAD, CB), F32)
        hp[CONV_PAD:, :] = val_ref[...] * jax.nn.sigmoid(gate_ref[...])
        wv = w_ref[...]

        def chunk(ci, carry):
            c0 = pl.multiple_of(ci * RC, 8)
            win = hp[pl.ds(c0, n), :]
            acc = jnp.zeros((RC, CB), F32) + b_ref[...]
            for k in range(CONV_WIDTH):
                acc = acc + wv[k:k + 1, :] * pltpu.roll(win, n - (k + 2), 0)[0:RC]
            o_ref[pl.ds(c0, RC), :] = acc
            return carry

        lax.fori_loop(0, L // RC, chunk, 0)

    col = lambda off: pl.BlockSpec((L, CB), lambda c: (0, off * nb + c))
    return pl.pallas_call(
        body, grid=(nb,),
        in_specs=[col(2), col(3), pl.BlockSpec((CONV_PAD, CB), lambda c: (0, c)), pl.BlockSpec((1, CB), lambda c: (0, c))],
        out_specs=pl.BlockSpec((L, CB), lambda c: (0, c)),
        out_shape=jax.ShapeDtypeStruct((L, GW), F32),
        scratch_shapes=[pltpu.VMEM((L + CONV_PAD, CB), F32)],
        compiler_params=_cparams(("parallel",)), name=name)(proj, proj, w_dw, b_dw)


def _conv1_bwd(proj, dhc, w_dw, *, name):
    L = proj.shape[0]
    GW = w_dw.shape[1]
    CB = min(GW, 128)
    nb = GW // CB
    RC = _tile(L, CONV_RC)
    n = RC + CONV_PAD

    def body(val_ref, gate_ref, d_ref, w_ref, dval_ref, dgate_ref, dw_ref, db_ref, hp, dp):
        val = val_ref[...]
        sg = jax.nn.sigmoid(gate_ref[...])
        hp[0:CONV_PAD, :] = jnp.zeros((CONV_PAD, CB), F32)
        hp[CONV_PAD:, :] = val * sg
        dp[0:L, :] = d_ref[...]
        dp[L:, :] = jnp.zeros((CONV_PAD, CB), F32)
        dw_ref[...] = jnp.zeros_like(dw_ref)
        db_ref[...] = jnp.sum(d_ref[...], axis=0, keepdims=True)
        wv = w_ref[...]

        def chunk(ci, carry):
            c0 = pl.multiple_of(ci * RC, 8)
            win = hp[pl.ds(c0, n), :]
            dwin = dp[pl.ds(c0, n), :]
            d = dwin[0:RC]
            dh = jnp.zeros((RC, CB), F32)
            for k in range(CONV_WIDTH):
                dw_ref[k:k + 1, :] += jnp.sum(d * pltpu.roll(win, n - (k + 2), 0)[0:RC], axis=0, keepdims=True)
                sh = CONV_WIDTH - 1 - k
                dh = dh + wv[k:k + 1, :] * (pltpu.roll(dwin, n - sh, 0)[0:RC] if sh else d)
            v, s = val_ref[pl.ds(c0, RC), :], jax.nn.sigmoid(gate_ref[pl.ds(c0, RC), :])
            dval_ref[pl.ds(c0, RC), :] = dh * s
            dgate_ref[pl.ds(c0, RC), :] = dh * v * s * (1.0 - s)
            return carry

        lax.fori_loop(0, L // RC, chunk, 0)

    col = lambda off: pl.BlockSpec((L, CB), lambda c: (0, off * nb + c))
    own = pl.BlockSpec((L, CB), lambda c: (0, c))
    return pl.pallas_call(
        body, grid=(nb,),
        in_specs=[col(2), col(3), own, pl.BlockSpec((CONV_PAD, CB), lambda c: (0, c))],
        out_specs=[own, own, pl.BlockSpec((CONV_PAD, CB), lambda c: (0, c)), pl.BlockSpec((1, CB), lambda c: (0, c))],
        out_shape=[jax.ShapeDtypeStruct((L, GW), F32), jax.ShapeDtypeStruct((L, GW), F32),
                   jax.ShapeDtypeStruct((CONV_PAD, GW), F32), jax.ShapeDtypeStruct((1, GW), F32)],
        scratch_shapes=[pltpu.VMEM((L + CONV_PAD, CB), F32), pltpu.VMEM((L + CONV_PAD, CB), F32)],
        compiler_params=_cparams(("parallel",)), name=name)(proj, proj, dhc, w_dw)


def _ln_silu(hc, g, b):
    mu = jnp.mean(hc, axis=-1, keepdims=True)
    xc = hc - mu
    r = lax.rsqrt(jnp.mean(xc * xc, axis=-1, keepdims=True) + NORM_EPS)
    xh = xc * r
    a = xh * g + b
    sg = jax.nn.sigmoid(a)
    return xh, r, a, sg


def _conv2_fwd(hc, ln_g, ln_b, w_pw, *, name):
    L, GW = hc.shape
    tr = _tile(L, 256)

    def body(h_ref, g_ref, b_ref, w_ref, y_ref):
        _, _, a, sg = _ln_silu(h_ref[...], g_ref[...], b_ref[...])
        y_ref[...] = _dot((a * sg).astype(BF16), w_ref[...], _NN)

    row = pl.BlockSpec((tr, GW), lambda i: (i, 0))
    vec = pl.BlockSpec((1, GW), lambda i: (0, 0))
    return pl.pallas_call(
        body, grid=(L // tr,), in_specs=[row, vec, vec, pl.BlockSpec((GW, GW), lambda i: (0, 0))],
        out_specs=row, out_shape=jax.ShapeDtypeStruct((L, GW), F32),
        compiler_params=_cparams(("parallel",)), name=name)(hc, ln_g, ln_b, w_pw)


def _conv2_bwd(hc, ln_g, ln_b, w_pw, dy, *, name):
    L, GW = hc.shape
    tr = _tile(L, 256)

    def body(h_ref, g_ref, b_ref, w_ref, dy_ref, dh_ref, dg_ref, db_ref, dw_ref):
        @pl.when(pl.program_id(0) == 0)
        def _():
            for r in (dg_ref, db_ref, dw_ref):
                r[...] = jnp.zeros_like(r)

        xh, r, a, sg = _ln_silu(h_ref[...], g_ref[...], b_ref[...])
        dyb = dy_ref[...].astype(BF16)
        dw_ref[...] += _dot((a * sg).astype(BF16), dyb, _TN)
        da = _dot(dyb, w_ref[...], _NT) * (sg + a * sg * (1.0 - sg))
        dg_ref[...] += jnp.sum(da * xh, axis=0, keepdims=True)
        db_ref[...] += jnp.sum(da, axis=0, keepdims=True)
        dxh = da * g_ref[...]
        dh_ref[...] = r * (dxh - jnp.mean(dxh, axis=-1, keepdims=True) - xh * jnp.mean(dxh * xh, axis=-1, keepdims=True))

    row = pl.BlockSpec((tr, GW), lambda i: (i, 0))
    vec = pl.BlockSpec((1, GW), lambda i: (0, 0))
    mat = pl.BlockSpec((GW, GW), lambda i: (0, 0))
    return pl.pallas_call(
        body, grid=(L // tr,), in_specs=[row, vec, vec, mat, row],
        out_specs=[row, vec, vec, mat],
        out_shape=[jax.ShapeDtypeStruct((L, GW), F32), jax.ShapeDtypeStruct((1, GW), F32), jax.ShapeDtypeStruct((1, GW), F32),
                   jax.ShapeDtypeStruct((GW, GW), F32)],
        compiler_params=_cparams(("arbitrary",)), name=name)(hc, ln_g, ln_b, w_pw, dy)


def _grp_fwd(ys, g, *, name):
    L, GW = ys[0].shape
    tr = _tile(L, 256)

    def body(*refs):
        g_ref, o_ref = refs[4], refs[5]
        for m in range(N_MIXERS):
            yv = refs[m][...]
            r = lax.rsqrt(jnp.mean(yv * yv, axis=-1, keepdims=True) + NORM_EPS)
            o_ref[:, m * GW:(m + 1) * GW] = (yv * r * g_ref[:, m * GW:(m + 1) * GW]).astype(o_ref.dtype)

    row = pl.BlockSpec((tr, GW), lambda i: (i, 0))
    return pl.pallas_call(
        body, grid=(L // tr,), in_specs=[row] * 4 + [pl.BlockSpec((1, N_MIXERS * GW), lambda i: (0, 0))],
        out_specs=pl.BlockSpec((tr, N_MIXERS * GW), lambda i: (i, 0)),
        out_shape=jax.ShapeDtypeStruct((L, N_MIXERS * GW), BF16),
        compiler_params=_cparams(("parallel",)), name=name)(*ys, g)


def _grp_bwd(ys, g, dy, *, name):
    L, GW = ys[0].shape
    tr = _tile(L, 256)

    def body(*refs):
        g_ref, dy_ref = refs[4], refs[5]
        outs, dg_ref = refs[6:10], refs[10]

        @pl.when(pl.program_id(0) == 0)
        def _():
            dg_ref[...] = jnp.zeros_like(dg_ref)

        for m in range(N_MIXERS):
            cs = slice(m * GW, (m + 1) * GW)
            yv = refs[m][...]
            r = lax.rsqrt(jnp.mean(yv * yv, axis=-1, keepdims=True) + NORM_EPS)
            xh = yv * r
            dyv = dy_ref[:, cs]
            dyg = dyv * g_ref[:, cs]
            outs[m][...] = r * (dyg - xh * jnp.mean(dyg * xh, axis=-1, keepdims=True))
            dg_ref[:, cs] += jnp.sum(dyv * xh, axis=0, keepdims=True)

    row = pl.BlockSpec((tr, GW), lambda i: (i, 0))
    wide = pl.BlockSpec((tr, N_MIXERS * GW), lambda i: (i, 0))
    vec = pl.BlockSpec((1, N_MIXERS * GW), lambda i: (0, 0))
    return pl.pallas_call(
        body, grid=(L // tr,), in_specs=[row] * 4 + [vec, wide],
        out_specs=[row] * 4 + [vec],
        out_shape=[jax.ShapeDtypeStruct((L, GW), F32)] * 4 + [jax.ShapeDtypeStruct((1, N_MIXERS * GW), F32)],
        compiler_params=_cparams(("arbitrary",)), name=name)(*ys, g, dy)


ATT_PAR = 4


def _datt_lanes(GW):
    E = GW // ATT_HEADS
    lb = min(GW, 128)
    return lb, lb // E, E


def _datt_rows(c, br, nblk):
    dil = 4 ** br
    nbs = nblk // dil
    r, jb = c // nbs, c % nbs
    return r + dil * ATT_BLOCK * jb, r + dil * ATT_BLOCK * jnp.maximum(jb - 1, 0), jb == 0


def _datt_fwd(proj, bias, *, name):
    L = proj.shape[0]
    GW = proj.shape[1] // 7
    lb, hps, E = _datt_lanes(GW)
    B = ATT_BLOCK
    nblk = L // B
    scale = E ** -0.5

    def body(q_ref, k_ref, v_ref, b_ref, y_ref, lse_ref, m_s, l_s):
        m_s[...] = jnp.full(m_s.shape, NEG_INF, F32)
        l_s[...] = jnp.zeros(l_s.shape, F32)
        y_ref[...] = jnp.zeros(y_ref.shape, F32)
        lane = lax.broadcasted_iota(jnp.int32, (B, lb), 1) // E

        for br, (_, dil) in enumerate(DILATED_PATTERNS):
            def step(i, carry, br=br, dil=dil):
                loaded = []
                for u in range(ATT_PAR):
                    start, pstart, first = _datt_rows(i + u * (nblk // ATT_PAR), br, nblk)
                    rows, prows = pl.ds(start, B, stride=dil), pl.ds(pstart, B, stride=dil)
                    loaded.append((rows, first, q_ref[rows, :] * scale, k_ref[rows, :].astype(BF16), k_ref[prows, :].astype(BF16),
                                   v_ref[rows, :].astype(BF16), v_ref[prows, :].astype(BF16), m_s[rows, :], l_s[rows, :], y_ref[rows, :]))
                done = []
                for rows, first, q, kc, kp, vc, vp, m_old, l_old, a_old in loaded:
                    m_new, l_new, a_new = m_old, l_old, a_old
                    for a in range(hps):
                        sel = lane == a
                        qm = jnp.where(sel, q, 0.0).astype(BF16)
                        s_p = _dot(qm, kp, _NT) + b_ref[br, a, :, 0:B]
                        s_c = _dot(qm, kc, _NT) + b_ref[br, a, :, B:2 * B]
                        s_p = jnp.where(first, NEG_INF, s_p)
                        mo, lo = m_old[:, a * E:a * E + 1], l_old[:, a * E:a * E + 1]
                        mn = jnp.maximum(mo, jnp.maximum(jnp.max(s_p, axis=-1, keepdims=True), jnp.max(s_c, axis=-1, keepdims=True)))
                        alpha = jnp.exp(mo - mn)
                        p_p, p_c = jnp.exp(s_p - mn), jnp.exp(s_c - mn)
                        ln = alpha * lo + jnp.sum(p_p, axis=-1, keepdims=True) + jnp.sum(p_c, axis=-1, keepdims=True)
                        pv = _dot(p_p.astype(BF16), vp, _NN) + _dot(p_c.astype(BF16), vc, _NN)
                        m_new = jnp.where(sel, mn, m_new)
                        l_new = jnp.where(sel, ln, l_new)
                        a_new = jnp.where(sel, alpha * a_old + pv, a_new)
                    done.append((rows, m_new, l_new, a_new))
                for rows, m_new, l_new, a_new in done:
                    m_s[rows, :] = m_new
                    l_s[rows, :] = l_new
                    y_ref[rows, :] = a_new
                return carry

            lax.fori_loop(0, nblk // ATT_PAR, step, 0)
        y_ref[...] = y_ref[...] / l_s[...]
        lse_ref[...] = m_s[...] + jnp.log(l_s[...])

    col = lambda off: pl.BlockSpec((L, lb), lambda h: (0, off * (GW // lb) + h))
    own = pl.BlockSpec((L, lb), lambda h: (0, h))
    return pl.pallas_call(
        body, grid=(GW // lb,),
        in_specs=[col(4), col(5), col(6), pl.BlockSpec((len(DILATED_PATTERNS), hps, B, 2 * B), lambda h: (0, h, 0, 0))],
        out_specs=[own, own], out_shape=[jax.ShapeDtypeStruct((L, GW), F32)] * 2,
        scratch_shapes=[pltpu.VMEM((L, lb), F32), pltpu.VMEM((L, lb), F32)],
        compiler_params=_cparams(("parallel",), VMEM_LIMIT), name=name)(proj, proj, proj, bias)


def _datt_bwd(proj, bias, y, lse, dy, *, name):
    L = proj.shape[0]
    GW = proj.shape[1] // 7
    lb, hps, E = _datt_lanes(GW)
    B = ATT_BLOCK
    nblk = L // B
    scale = E ** -0.5

    def body(q_ref, k_ref, v_ref, b_ref, y_ref, lse_ref, dy_ref, dq_ref, dk_ref, dv_ref, db_ref):
        for r in (dq_ref, dk_ref, dv_ref, db_ref):
            r[...] = jnp.zeros(r.shape, F32)
        lane = lax.broadcasted_iota(jnp.int32, (B, lb), 1) // E

        for br, (_, dil) in enumerate(DILATED_PATTERNS):
            def step(i, carry, br=br, dil=dil):
                loaded = []
                for u in range(ATT_PAR):
                    start, pstart, first = _datt_rows(i + u * (nblk // ATT_PAR), br, nblk)
                    rows, prows = pl.ds(start, B, stride=dil), pl.ds(pstart, B, stride=dil)
                    dyr = dy_ref[rows, :]
                    loaded.append((rows, prows, first, q_ref[rows, :] * scale, k_ref[rows, :].astype(BF16), k_ref[prows, :].astype(BF16),
                                   v_ref[rows, :].astype(BF16), v_ref[prows, :].astype(BF16), dyr, lse_ref[rows, :], dyr * y_ref[rows, :]))
                done = []
                dbias = [None] * hps
                for rows, prows, first, q, kc, kp, vc, vp, dyr, lser, dyy in loaded:
                    dq = jnp.zeros((B, lb), F32)
                    dkc, dkp, dvc, dvp = dq, dq, dq, dq
                    for a in range(hps):
                        sel = lane == a
                        qm = jnp.where(sel, q, 0.0).astype(BF16)
                        dym = jnp.where(sel, dyr, 0.0).astype(BF16)
                        dm = jnp.sum(jnp.where(sel, dyy, 0.0), axis=-1, keepdims=True)
                        lse_a = lser[:, a * E:a * E + 1]
                        s_p = _dot(qm, kp, _NT) + b_ref[br, a, :, 0:B]
                        s_c = _dot(qm, kc, _NT) + b_ref[br, a, :, B:2 * B]
                        s_p = jnp.where(first, NEG_INF, s_p)
                        p_p, p_c = jnp.exp(s_p - lse_a), jnp.exp(s_c - lse_a)
                        ds_p = p_p * (_dot(dym, vp, _NT) - dm)
                        ds_c = p_c * (_dot(dym, vc, _NT) - dm)
                        dbias[a] = (ds_p, ds_c) if dbias[a] is None else (dbias[a][0] + ds_p, dbias[a][1] + ds_c)
                        dsb_p, dsb_c = ds_p.astype(BF16), ds_c.astype(BF16)
                        dq = dq + jnp.where(sel, _dot(dsb_p, kp, _NN) + _dot(dsb_c, kc, _NN), 0.0)
                        dkp = dkp + _dot(dsb_p, qm, _TN)
                        dkc = dkc + _dot(dsb_c, qm, _TN)
                        dvp = dvp + _dot(p_p.astype(BF16), dym, _TN)
                        dvc = dvc + _dot(p_c.astype(BF16), dym, _TN)
                    done.append((rows, prows, dq, dkp, dkc, dvp, dvc))
                for a in range(hps):
                    db_ref[br, a, :, 0:B] += dbias[a][0]
                    db_ref[br, a, :, B:2 * B] += dbias[a][1]
                for rows, prows, dq, dkp, dkc, dvp, dvc in done:
                    dq_ref[rows, :] += dq * scale
                    dk_ref[prows, :] += dkp
                    dk_ref[rows, :] += dkc
                    dv_ref[prows, :] += dvp
                    dv_ref[rows, :] += dvc
                return carry

            lax.fori_loop(0, nblk // ATT_PAR, step, 0)

    col = lambda off: pl.BlockSpec((L, lb), lambda h: (0, off * (GW // lb) + h))
    own = pl.BlockSpec((L, lb), lambda h: (0, h))
    bsp = pl.BlockSpec((len(DILATED_PATTERNS), hps, B, 2 * B), lambda h: (0, h, 0, 0))
    return pl.pallas_call(
        body, grid=(GW // lb,),
        in_specs=[col(4), col(5), col(6), bsp, own, own, own], out_specs=[own, own, own, bsp],
        out_shape=[jax.ShapeDtypeStruct((L, GW), F32)] * 3 + [jax.ShapeDtypeStruct((len(DILATED_PATTERNS), ATT_HEADS, B, 2 * B), F32)],
        compiler_params=_cparams(("parallel",), VMEM_LIMIT), name=name)(proj, proj, proj, bias, y, lse, dy)


def _t5_bucket(dist):
    n = np.maximum(dist, 0)
    max_exact = REL_BUCKETS // 2
    large = max_exact + (np.log(np.maximum(n, 1) / max_exact) / np.log(REL_MAX_DIST / max_exact)
                         * (REL_BUCKETS - max_exact)).astype(np.int64)
    large = np.minimum(large, REL_BUCKETS - 1)
    return np.where(n < max_exact, n, large).astype(np.int32)


def _att_tables():
    a = np.arange(ATT_BLOCK)[:, None]
    b = np.arange(2 * ATT_BLOCK)[None, :]
    sub = a + ATT_BLOCK - b
    buckets, valids = [], []
    for window, dil in DILATED_PATTERNS:
        buckets.append(_t5_bucket(sub * dil))
        valids.append((sub >= 0) & (sub <= window // dil))
    return np.stack(buckets), np.stack(valids)


def _xatt_fwd(q, k, v, *, name):
    L, XW = q.shape
    M = k.shape[0]
    tr = _tile(L, 512)
    scale = X_HEAD_DIM ** -0.5

    def body(q_ref, k_ref, v_ref, o_ref):
        s = _dot(q_ref[...], k_ref[...], _NT) * scale
        p = jnp.exp(s - jnp.max(s, axis=-1, keepdims=True))
        den = jnp.sum(p, axis=-1, keepdims=True)
        o_ref[...] = (_dot(p.astype(BF16), v_ref[...], _NN) / den).astype(o_ref.dtype)

    qs = pl.BlockSpec((tr, X_HEAD_DIM), lambda h, i: (i, h))
    ks = pl.BlockSpec((M, X_HEAD_DIM), lambda h, i: (0, h))
    return pl.pallas_call(
        body, grid=(XW // X_HEAD_DIM, L // tr), in_specs=[qs, ks, ks], out_specs=qs,
        out_shape=jax.ShapeDtypeStruct((L, XW), BF16),
        compiler_params=_cparams(("parallel", "parallel")), name=name)(q, k, v)


def _xatt_bwd(q, k, v, do, *, name):
    L, XW = q.shape
    M = k.shape[0]
    tr = _tile(L, 512)
    scale = X_HEAD_DIM ** -0.5

    def body(q_ref, k_ref, v_ref, do_ref, dq_ref, dk_ref, dv_ref):
        @pl.when(pl.program_id(1) == 0)
        def _():
            dk_ref[...] = jnp.zeros_like(dk_ref)
            dv_ref[...] = jnp.zeros_like(dv_ref)

        qv, kv, vv = q_ref[...], k_ref[...], v_ref[...]
        s = _dot(qv, kv, _NT) * scale
        p = jnp.exp(s - jnp.max(s, axis=-1, keepdims=True))
        p = p / jnp.sum(p, axis=-1, keepdims=True)
        dob = do_ref[...].astype(BF16)
        pb = p.astype(BF16)
        dv_ref[...] += _dot(pb, dob, _TN)
        dp = _dot(dob, vv, _NT)
        ds = (p * (dp - jnp.sum(dp * p, axis=-1, keepdims=True)) * scale).astype(BF16)
        dq_ref[...] = _dot(ds, kv, _NN)
        dk_ref[...] += _dot(ds, qv, _TN)

    qs = pl.BlockSpec((tr, X_HEAD_DIM), lambda h, i: (i, h))
    ks = pl.BlockSpec((M, X_HEAD_DIM), lambda h, i: (0, h))
    return pl.pallas_call(
        body, grid=(XW // X_HEAD_DIM, L // tr), in_specs=[qs, ks, ks, qs], out_specs=[qs, ks, ks],
        out_shape=[jax.ShapeDtypeStruct((L, XW), F32), jax.ShapeDtypeStruct((M, XW), F32), jax.ShapeDtypeStruct((M, XW), F32)],
        compiler_params=_cparams(("parallel", "arbitrary")), name=name)(q, k, v, do)


_ANY = pl.BlockSpec(memory_space=pl.ANY)


def _place():
    mx, my, mc = lax.axis_index("x"), lax.axis_index("y"), lax.axis_index("c")
    chips = [(1 - mx, my), (mx, 1 - my), (1 - mx, 1 - my)]
    return mx, my, mc, chips


def _rcopy(src, dst, ssem, rsem, dev):
    return pltpu.make_async_remote_copy(src_ref=src, dst_ref=dst, send_sem=ssem, recv_sem=rsem, device_id=dev,
                                        device_id_type=MESH)


STAGE_BYTES = 2 * 1024 * 1024


def _staged_copy(pairs, buf, isem, osem):
    n = len(pairs)
    ins = [pltpu.make_async_copy(src, buf.at[i % 2], isem.at[i % 2]) for i, (src, _) in enumerate(pairs)]
    outs = [pltpu.make_async_copy(buf.at[i % 2], dst, osem.at[i % 2]) for i, (_, dst) in enumerate(pairs)]
    ins[0].start()
    for i in range(n):
        if i + 1 < n:
            if i >= 1:
                outs[i - 1].wait()
            ins[i + 1].start()
        ins[i].wait()
        outs[i].start()
    for i in range(max(n - 2, 0), n):
        outs[i].wait()


_HBM = pl.BlockSpec(memory_space=pltpu.HBM)
_SEMS = pl.BlockSpec(memory_space=pltpu.SEMAPHORE)
_VM = pl.BlockSpec(memory_space=pltpu.VMEM)
_DATAFLOW = pltpu.SideEffectType.DATAFLOW_SIDE_EFFECTING


def _ag_copies(x_refs, land_refs, ssem, rsem, inbound, layer, which=None):
    mx, my, mc, chips = _place()
    me = 2 * mx + my
    cps = []
    for i, (x_ref, land_ref) in enumerate(zip(x_refs, land_refs)):
        w = i if which is None else which[i]
        R2 = x_ref.shape[1] // 2
        mine = x_ref.at[layer, pl.ds(mc * R2, R2)]
        for j, (px, py) in enumerate(chips):
            dst = land_ref.at[2 * px + py, mc] if inbound else land_ref.at[me, mc]
            cps.append(_rcopy(mine, dst, ssem.at[3 * w + j], rsem.at[3 * w + j], (px, py, mc)))
    return cps


def _ag_start(xs, layer, token, *, name):
    n = len(xs)
    lands = [pltpu.with_memory_space_constraint(lax.empty((N_CHIPS, 2, x.shape[1] // 2, x.shape[2]), x.dtype), pltpu.HBM) for x in xs]

    def body(*refs):
        x_refs, land_refs, tok_ref = refs[:n], refs[n:2 * n], refs[2 * n]
        ssem, rsem, tok_out = refs[2 * n + 1], refs[2 * n + 2], refs[-1]
        for cp in _ag_copies(x_refs, land_refs, ssem, rsem, False, layer):
            cp.start()
        tok_out[...] = tok_ref[...]

    outs = pl.pallas_call(
        body, name=name,
        out_shape=(pltpu.SemaphoreType.DMA((3 * n,)), pltpu.SemaphoreType.DMA((3 * n,)),
                   *[pltpu.HBM(t.shape, t.dtype) for t in lands], jax.ShapeDtypeStruct(token.shape, token.dtype)),
        in_specs=[_HBM] * (2 * n) + [_VM], out_specs=(_SEMS, _SEMS, *[_HBM] * n, _VM),
        input_output_aliases={n + i: 2 + i for i in range(n)},
        compiler_params=pltpu.CompilerParams(has_side_effects=_DATAFLOW),
    )(*xs, *lands, token)
    return outs[0], outs[1], list(outs[2:2 + n]), outs[-1]


def _ag_wait(ssem, rsem, xs, lands, after, which, layer, *, name):
    n = len(xs)

    def body(*refs):
        x_refs, land_refs, ssem_ref, rsem_ref = refs[:n], refs[n:2 * n], refs[2 * n], refs[2 * n + 1]
        for cp in _ag_copies(x_refs, land_refs, ssem_ref, rsem_ref, True, layer, which):
            cp.wait_send()
            cp.wait_recv()

    after = list(after) if isinstance(after, (list, tuple)) else [after]
    outs = pl.pallas_call(
        body, name=name,
        out_shape=tuple(pltpu.HBM(t.shape, t.dtype) for t in lands),
        in_specs=[_HBM] * (2 * n) + [_SEMS, _SEMS] + [_ANY] * len(after), out_specs=tuple([_HBM] * n),
        input_output_aliases={n + i: i for i in range(n)},
        compiler_params=pltpu.CompilerParams(has_side_effects=_DATAFLOW),
    )(*xs, *lands, ssem, rsem, *after)
    return list(outs)


def _ag_finish(lands, xs, layer, *, name):
    n = len(xs)

    def body(*refs):
        x_refs, o_refs, (ssem, rsem) = refs[n:2 * n], refs[2 * n:3 * n], refs[3 * n:]
        mx, my, mc, chips = _place()
        me = 2 * mx + my
        sib = (mx, my, 1 - mc)
        passed = []
        for w in range(n):
            for j, (px, py) in enumerate(chips):
                got = o_refs[w].at[2 * px + py, mc]
                passed.append(_rcopy(got, got, ssem.at[3 * w + j], rsem.at[3 * w + j], sib))
        for cp in passed:
            cp.start()
        for w in range(n):
            _, R, C = x_refs[w].shape
            R2 = R // 2
            ch = _rows_tile(R2, max(8, STAGE_BYTES // (C * x_refs[w].dtype.itemsize)))
            pairs = [(x_refs[w].at[layer, pl.ds(h * R2 + r, ch)], o_refs[w].at[me, h, pl.ds(r, ch)])
                     for h in range(2) for r in range(0, R2, ch)]
            pl.run_scoped(functools.partial(_staged_copy, pairs), pltpu.VMEM((2, ch, C), x_refs[w].dtype),
                          pltpu.SemaphoreType.DMA((2,)), pltpu.SemaphoreType.DMA((2,)))
        for w in range(n):
            for j, (px, py) in enumerate(chips):
                theirs = o_refs[w].at[2 * px + py, 1 - mc]
                _rcopy(theirs, theirs, ssem.at[3 * w + j], rsem.at[3 * w + j], sib).wait_recv()
        for cp in passed:
            cp.wait_send()

    return pl.pallas_call(
        body, in_specs=[_ANY] * (2 * n), out_specs=[_ANY] * n,
        out_shape=[jax.ShapeDtypeStruct(t.shape, t.dtype) for t in lands],
        input_output_aliases={i: i for i in range(n)},
        scratch_shapes=[pltpu.SemaphoreType.DMA((3 * n,)), pltpu.SemaphoreType.DMA((3 * n,))],
        name=name)(*lands, *xs)


def _ag4(x, *, name):
    def body(x_ref, o_ref, ssem, rsem, lsem):
        mx, my, mc, chips = _place()
        me = 2 * mx + my
        loc = pltpu.make_async_copy(x_ref, o_ref.at[me], lsem)
        loc.start()
        sends = [_rcopy(x_ref, o_ref.at[me], ssem.at[j], rsem.at[j], (px, py, mc)) for j, (px, py) in enumerate(chips)]
        for cp in sends:
            cp.start()
        for j, (px, py) in enumerate(chips):
            _rcopy(x_ref, o_ref.at[2 * px + py], ssem.at[j], rsem.at[j], (px, py, mc)).wait_recv()
        for cp in sends:
            cp.wait_send()
        loc.wait()

    return pl.pallas_call(
        body, in_specs=[_ANY], out_specs=_ANY, out_shape=jax.ShapeDtypeStruct((N_CHIPS,) + x.shape, x.dtype),
        scratch_shapes=[pltpu.SemaphoreType.DMA((3,)), pltpu.SemaphoreType.DMA((3,)), pltpu.SemaphoreType.DMA(())],
        name=name)(x)


def _rs_sib(gs, after=(), *, name):
    n, na = len(gs), len(after)

    def body(*refs):
        g_refs, t_refs, (ssem, rsem) = refs[:n], refs[n + na:2 * n + na], refs[2 * n + na:]
        mx, my, mc, _ = _place()
        sib = (mx, my, 1 - mc)
        sends = []
        for w in range(n):
            R2 = g_refs[w].shape[1] // 2
            for k in range(N_CHIPS):
                sends.append(_rcopy(g_refs[w].at[k, pl.ds((1 - mc) * R2, R2)], t_refs[w].at[k], ssem.at[N_CHIPS * w + k],
                                    rsem.at[N_CHIPS * w + k], sib))
        for cp in sends:
            cp.start()
        for w in range(n):
            for k in range(N_CHIPS):
                t = t_refs[w].at[k]
                _rcopy(t, t, ssem.at[N_CHIPS * w + k], rsem.at[N_CHIPS * w + k], sib).wait_recv()
        for cp in sends:
            cp.wait_send()

    return pl.pallas_call(
        body, in_specs=[_ANY] * (n + na), out_specs=[_ANY] * n,
        out_shape=[jax.ShapeDtypeStruct((N_CHIPS, g.shape[1] // 2, g.shape[2]), g.dtype) for g in gs],
        scratch_shapes=[pltpu.SemaphoreType.DMA((N_CHIPS * n,)), pltpu.SemaphoreType.DMA((N_CHIPS * n,))],
        name=name)(*gs, *after)


def _a2a_copies(h_refs, got_refs, ssem, rsem, inbound):
    mx, my, mc, chips = _place()
    me = 2 * mx + my
    cps = []
    for w, (h_ref, got_ref) in enumerate(zip(h_refs, got_refs)):
        for j, (px, py) in enumerate(chips):
            dst = got_ref.at[2 * px + py] if inbound else got_ref.at[me]
            cps.append(_rcopy(h_ref.at[2 * px + py], dst, ssem.at[3 * w + j], rsem.at[3 * w + j], (px, py, mc)))
    return cps


def _a2a_start(hs, *, name):
    n = len(hs)
    gots = [pltpu.with_memory_space_constraint(lax.empty(h.shape, h.dtype), pltpu.HBM) for h in hs]
    hs = [pltpu.with_memory_space_constraint(h, pltpu.HBM) for h in hs]

    def body(*refs):
        h_refs, got_refs = refs[:n], refs[n:2 * n]
        ssem, rsem, tok_out = refs[2 * n], refs[2 * n + 1], refs[-1]
        for cp in _a2a_copies(h_refs, got_refs, ssem, rsem, False):
            cp.start()
        tok_out[...] = jnp.zeros_like(tok_out)

    outs = pl.pallas_call(
        body, name=name,
        out_shape=(pltpu.SemaphoreType.DMA((3 * n,)), pltpu.SemaphoreType.DMA((3 * n,)),
                   *[pltpu.HBM(h.shape, h.dtype) for h in hs], *[pltpu.HBM(h.shape, h.dtype) for h in hs],
                   jax.ShapeDtypeStruct((8, 128), F32)),
        in_specs=[_HBM] * (2 * n), out_specs=(_SEMS, _SEMS, *[_HBM] * (2 * n), _VM),
        input_output_aliases={i: 2 + i for i in range(2 * n)},
        compiler_params=pltpu.CompilerParams(has_side_effects=_DATAFLOW),
    )(*hs, *gots)
    return outs[0], outs[1], list(outs[2:2 + n]), list(outs[2 + n:2 + 2 * n]), outs[-1]


def _a2a_wait(ssem, rsem, hs, gots, after, *, name):
    n = len(hs)

    def body(*refs):
        h_refs, got_refs, ssem_ref, rsem_ref = refs[:n], refs[n:2 * n], refs[2 * n], refs[2 * n + 1]
        for cp in _a2a_copies(h_refs, got_refs, ssem_ref, rsem_ref, True):
            cp.wait_send()
            cp.wait_recv()

    after = list(after) if isinstance(after, (list, tuple)) else [after]
    outs = pl.pallas_call(
        body, name=name,
        out_shape=tuple(pltpu.HBM(h.shape, h.dtype) for h in hs + gots),
        in_specs=[_HBM] * (2 * n) + [_SEMS, _SEMS] + [_ANY] * len(after), out_specs=tuple([_HBM] * (2 * n)),
        input_output_aliases={i: i for i in range(2 * n)},
        compiler_params=pltpu.CompilerParams(has_side_effects=_DATAFLOW),
    )(*hs, *gots, ssem, rsem, *after)
    return list(outs[:n]), list(outs[n:])


def _sib_fill(bufs, after=(), *, name):
    n, na = len(bufs), len(after)

    def body(*refs):
        o_refs, (ssem, rsem) = refs[n + na:2 * n + na], refs[2 * n + na:]
        mx, my, mc, _ = _place()
        sib = (mx, my, 1 - mc)
        sends = [_rcopy(o_refs[w].at[mc], o_refs[w].at[mc], ssem.at[w], rsem.at[w], sib) for w in range(n)]
        for cp in sends:
            cp.start()
        for w in range(n):
            t = o_refs[w].at[1 - mc]
            _rcopy(t, t, ssem.at[w], rsem.at[w], sib).wait_recv()
        for cp in sends:
            cp.wait_send()

    return pl.pallas_call(
        body, in_specs=[_ANY] * (n + na), out_specs=[_ANY] * n, out_shape=[jax.ShapeDtypeStruct(b.shape, b.dtype) for b in bufs],
        input_output_aliases={i: i for i in range(n)},
        scratch_shapes=[pltpu.SemaphoreType.DMA((n,)), pltpu.SemaphoreType.DMA((n,))], name=name)(*bufs, *after)


def _rows_tile(n, pref=512):
    t = min(n, pref)
    while n % t or (t % 8 and t != n):
        t -= 1
    return t


def _rs_add(g, theirs, c_arr, payload, *, name):
    _, R, C = g.shape
    R2 = R // 2
    tr = _rows_tile(R2, 256)
    nb = R2 // tr

    def body(c_ref, g_ref, t_ref, o_ref):
        o_ref[...] = (g_ref[...].astype(F32) + t_ref[...].astype(F32)).astype(o_ref.dtype)

    blk = pl.BlockSpec((None, tr, C), lambda k, i, c: (k, i, 0))
    return pl.pallas_call(
        body,
        grid_spec=pltpu.PrefetchScalarGridSpec(
            num_scalar_prefetch=1, grid=(N_CHIPS, nb),
            in_specs=[pl.BlockSpec((None, tr, C), lambda k, i, c: (k, c[0] * nb + i, 0)), blk], out_specs=blk),
        out_shape=jax.ShapeDtypeStruct((N_CHIPS, R2, C), payload),
        compiler_params=_cparams(("arbitrary", "arbitrary")), name=name)(c_arr, g, theirs)


def _rs_sum(chip_sum, got, mc_arr, *, name):
    _, R2, C = chip_sum.shape
    tr = _rows_tile(R2, 256)

    def body(s_ref, own_ref, g0, g1, g2, g3, o_ref):
        me = s_ref[0]
        acc = jnp.zeros((tr, C), F32)
        for k, g_ref in enumerate((g0, g1, g2, g3)):
            acc = acc + jnp.where(me == k, own_ref[...], g_ref[...]).astype(F32)
        o_ref[...] = acc

    def got_spec(k):
        return pl.BlockSpec((None, tr, C), lambda i, s: (jnp.where(s[0] == k, (k + 1) % N_CHIPS, k), i, 0))

    return pl.pallas_call(
        body,
        grid_spec=pltpu.PrefetchScalarGridSpec(
            num_scalar_prefetch=1, grid=(R2 // tr,),
            in_specs=[pl.BlockSpec((None, tr, C), lambda i, s: (s[0], i, 0))] + [got_spec(k) for k in range(N_CHIPS)],
            out_specs=pl.BlockSpec((None, tr, C), lambda i, s: (s[1], i, 0))),
        out_shape=jax.ShapeDtypeStruct((2, R2, C), F32),
        compiler_params=_cparams(("arbitrary",)), name=name)(mc_arr, chip_sum, got, got, got, got)


def _adamw(w, m, v, gs, *, name, first=0, prev=None):
    _, _, R2, C = w.shape
    nl = len(gs)
    tr = _rows_tile(R2, 256)
    c1 = 1.0 / (1.0 - ADAM_B1 ** ADAM_STEP)
    c2 = 1.0 / (1.0 - ADAM_B2 ** ADAM_STEP)

    def body(w_ref, m_ref, v_ref, *refs):
        g_refs, (go_ref, d_ref, mo_ref, vo_ref) = refs[:nl], refs[-4:]
        l = pl.program_id(0)
        for ll in range(nl):
            @pl.when(l == ll)
            def _(ll=ll):
                gv = g_refs[ll][...]
                mn = ADAM_B1 * m_ref[...] + (1.0 - ADAM_B1) * gv
                vn = ADAM_B2 * v_ref[...] + (1.0 - ADAM_B2) * (gv * gv)
                go_ref[...] = gv
                mo_ref[...] = mn
                vo_ref[...] = vn
                d_ref[...] = -ADAM_LR * ((mn * c1) / (jnp.sqrt(vn * c2) + ADAM_EPS) + ADAM_WD * w_ref[...])

    def g_spec(ll):
        return pl.BlockSpec((None, tr, C), lambda l, h, i: (jnp.where(l == ll, h, 0), jnp.where(l == ll, i, 0), 0))

    ws = pl.BlockSpec((None, None, tr, C), lambda l, h, i: (l + first, h, i, 0))
    shp = jax.ShapeDtypeStruct(w.shape, F32)
    prev = list(prev) if prev is not None else []
    return pl.pallas_call(
        body, grid=(nl, 2, R2 // tr), in_specs=[ws, ws, ws] + [g_spec(ll) for ll in range(nl)] + [_ANY] * len(prev),
        out_specs=[ws] * 4, out_shape=[shp] * 4,
        input_output_aliases={3 + nl + k: k for k in range(len(prev))},
        compiler_params=_cparams(("arbitrary", "arbitrary", "arbitrary")), name=name)(w, m, v, *gs, *prev)


class _GradReducer:
    def __init__(self, tag, payload):
        self.tag, self.payload = tag, payload
        self.me = (2 * lax.axis_index("x") + lax.axis_index("y")).astype(jnp.int32)
        self.c = lax.axis_index("c").astype(jnp.int32)

    def start(self, names, gs, after=()):
        theirs = _rs_sib(gs, after, name=f"rs_sib_{self.tag}")
        hs = [_rs_add(g, t, self.c.reshape(1), self.payload, name=f"rs_add_{n}") for n, g, t in zip(names, gs, theirs)]
        self.names = names
        self.ssem, self.rsem, self.hs, self.gots, token = _a2a_start(hs, name=f"rs_a2a_start_{self.tag}")
        return token

    def finish(self, after):
        hs, gots = _a2a_wait(self.ssem, self.rsem, self.hs, self.gots, after, name=f"rs_a2a_wait_{self.tag}")
        mc = jnp.stack([self.me, self.c])
        return {n: _rs_sum(h, g, mc, name=f"rs_sum_{n}") for n, h, g in zip(self.names, hs, gots)}


WEIGHTS = ["rel_bias", "mem_norm_g", "norm_mix_g", "w_in", "s5_lam_re", "s5_lam_im", "s5_log_dt", "s5_b_re", "s5_b_im",
           "s5_c_re", "s5_c_im", "s5_d", "s5_w_glu", "pool_w", "pool_scale", "conv_w_dw", "conv_b_dw", "conv_ln_g",
           "conv_ln_b", "conv_w_pw", "grp_norm_g", "w_out", "norm_x_g", "w_xq", "w_xk", "w_xv", "w_xo", "norm_mlp_g",
           "w_up", "w_down", "norm_final_g"]
SHARDED = {"w_in": "col", "s5_w_glu": "row", "conv_w_dw": "col", "conv_w_pw": "row", "w_out": "row", "w_xq": "row",
           "w_xk": "row", "w_xv": "row", "w_xo": "col", "w_up": "col", "w_down": "row"}
AG_FIRST = ("w_in", "s5_w_glu", "conv_w_dw", "conv_w_pw")
SMALL = [n for n in WEIGHTS if n not in SHARDED]
SMALL_ALIGN = N_CHIPS * 2 * 256 * 128


def _mat(w, kind):
    return w.reshape(w.shape[0] * w.shape[1], w.shape[2]) if kind == "row" else w


def _att_bias(rel_bias):
    bucket, valid = _att_tables()
    onehot = (jnp.asarray(bucket.reshape(-1))[:, None] == jnp.arange(REL_BUCKETS)[None, :]).astype(F32)
    b = jnp.dot(onehot, rel_bias, precision=lax.Precision.HIGHEST).reshape(bucket.shape + (rel_bias.shape[1],))
    return jnp.where(jnp.asarray(valid)[:, None], jnp.transpose(b, (0, 3, 1, 2)), NEG_INF)


def _rel_bias_grad(dbias):
    bucket, _ = _att_tables()
    nb, H = dbias.shape[0], dbias.shape[1]
    onehot = (jnp.asarray(bucket.reshape(-1))[:, None] == jnp.arange(REL_BUCKETS)[None, :]).astype(BF16)
    flat = jnp.transpose(dbias.reshape(nb, H, -1), (1, 0, 2)).reshape(H, -1)
    return _mm(flat, onehot, "nn", tk=16384, name="rel_bias_grad").T


def _layer_fwd(h, mem_n, bias, sp, bw, l):
    L, D = h.shape
    GW = D // N_MIXERS
    G = GW // S5_CH
    E = GW // ATT_HEADS
    sv = {"h": h}
    xn = _rms_fwd(h, sp["norm_mix_g"][l][None], name="norm_mix")
    proj = _mm(xn, bw["w_in"], "nn", name="proj_in")
    sv.update(xn=xn, proj=proj)
    disc, disc_vjp = jax.vjp(_s5_disc, sp["s5_lam_re"][l], sp["s5_lam_im"][l], sp["s5_log_dt"][l], sp["s5_b_re"][l], sp["s5_b_im"][l])
    ab_r, ab_i, bb_r, bb_i = disc
    s5 = dict(
        bdr=_bd(jnp.transpose(bb_r, (0, 2, 1))).astype(BF16), bdi=_bd(jnp.transpose(bb_i, (0, 2, 1))).astype(BF16),
        cdr=_bd(jnp.transpose(sp["s5_c_re"][l], (0, 2, 1))).astype(BF16), cdi=_bd(jnp.transpose(sp["s5_c_im"][l], (0, 2, 1))).astype(BF16),
        d=sp["s5_d"][l][None], wglu=bw["s5_w_glu"], ab=(ab_r.reshape(-1), ab_i.reshape(-1)), vjp=disc_vjp)
    pw, tr, ti = _cpow_tables(*s5["ab"])
    y_a, xr, xi = _s5_fwd(proj, s5["bdr"], s5["bdi"], pw, tr, ti, s5["cdr"], s5["cdi"], s5["d"], s5["wglu"], name="s5_fwd")
    sv.update(s5=s5, xr=xr, xi=xi, y_a=y_a)
    pool_s = sp["pool_scale"][l].reshape(len(POOL_WINDOWS), 1, -1)
    y_b = _pool_fwd(proj, sp["pool_w"][l], pool_s, name="pool_fwd")
    sv.update(y_b=y_b, pool_s=pool_s)
    hc = _conv1_fwd(proj, bw["conv_w_dw"], sp["conv_b_dw"][l][None], name="conv_dw_fwd")
    y_c = _conv2_fwd(hc, sp["conv_ln_g"][l][None], sp["conv_ln_b"][l][None], bw["conv_w_pw"], name="conv_pw_fwd")
    sv.update(hc=hc, y_c=y_c)
    y_d, lse_d = _datt_fwd(proj, bias, name="att_fwd")
    sv.update(y_d=y_d, lse_d=lse_d)
    yn = _grp_fwd([y_a, y_b, y_c, y_d], sp["grp_norm_g"][l][None], name="grp_fwd")
    bw.rest(yn)
    h1 = _mm(yn, bw["w_out"], "nn", res=h, name="proj_out")
    sv.update(yn=yn, h1=h1)
    hx = _rms_fwd(h1, sp["norm_x_g"][l][None], name="norm_x")
    q_x = _mm(hx, bw["w_xq"], "nn", out_dtype=BF16, name="xq")
    k_x = _mm(mem_n, bw["w_xk"], "nn", out_dtype=BF16, name="xk")
    v_x = _mm(mem_n, bw["w_xv"], "nn", out_dtype=BF16, name="xv")
    o_x = _xatt_fwd(q_x, k_x, v_x, name="xatt_fwd")
    h2 = _mm(o_x, bw["w_xo"], "nn", res=h1, name="xo")
    sv.update(hx=hx, q_x=q_x, k_x=k_x, v_x=v_x, o_x=o_x, h2=h2)
    hm = _rms_fwd(h2, sp["norm_mlp_g"][l][None], name="norm_mlp")
    up, act = _mm(hm, bw["w_up"], "nn", epi="relu2", out_dtype=BF16, name="mlp_up")
    h3 = _mm(act, bw["w_down"], "nn", res=h2, name="mlp_down")
    sv.update(hm=hm, up=up, act=act)
    return h3, sv


def _stack_rows(g):
    return g.reshape(N_CHIPS, g.shape[0] // N_CHIPS, g.shape[1])


def _layer_bwd(dh3, d_memn, mem_n, bias, sp, bw, sv, l):
    L, D = dh3.shape
    GW = D // N_MIXERS
    G = GW // S5_CH
    E = GW // ATT_HEADS
    gs, gb = {}, {}
    d_up = _mm(dh3, bw["w_down"], "nt", epi="relu2_bwd", aux=sv["up"], out_dtype=BF16, name="mlp_down_dx")
    gb["w_down"] = _stack_rows(_mm(sv["act"], dh3, "tn", out_dtype=BF16, name="mlp_down_dw"))
    gb["w_up"] = _mm(sv["hm"], d_up, "tn", out_dtype=BF16, out_stack=N_CHIPS, name="mlp_up_dw")
    d_hm = _mm(d_up, bw["w_up"], "nt", name="mlp_up_dx")
    dh2, gs["norm_mlp_g"] = _rms_bwd(sv["h2"], sp["norm_mlp_g"][l][None], d_hm, dh3, name="norm_mlp_bwd")
    d_ox = _mm(dh2, bw["w_xo"], "nt", name="xo_dx")
    gb["w_xo"] = _mm(sv["o_x"], dh2, "tn", out_dtype=BF16, out_stack=N_CHIPS, name="xo_dw")
    dq_x, dk_x, dv_x = _xatt_bwd(sv["q_x"], sv["k_x"], sv["v_x"], d_ox, name="xatt_bwd")
    gb["w_xq"] = _stack_rows(_mm(sv["hx"], dq_x, "tn", out_dtype=BF16, name="xq_dw"))
    gb["w_xk"] = _stack_rows(_mm(mem_n, dk_x, "tn", out_dtype=BF16, name="xk_dw"))
    gb["w_xv"] = _stack_rows(_mm(mem_n, dv_x, "tn", out_dtype=BF16, name="xv_dw"))
    d_memn = _mm(dk_x, bw["w_xk"], "nt", res=d_memn, name="xk_dx")
    d_memn = _mm(dv_x, bw["w_xv"], "nt", res=d_memn, name="xv_dx")
    d_hx = _mm(dq_x, bw["w_xq"], "nt", name="xq_dx")
    dh1, gs["norm_x_g"] = _rms_bwd(sv["h1"], sp["norm_x_g"][l][None], d_hx, dh2, name="norm_x_bwd")
    d_yn = _mm(dh1, bw["w_out"], "nt", name="proj_out_dx")
    gb["w_out"] = _stack_rows(_mm(sv["yn"], dh1, "tn", out_dtype=BF16, name="proj_out_dw"))
    ys = [sv["y_a"], sv["y_b"], sv["y_c"], sv["y_d"]]
    dya, dyb, dyc, dyd, gs["grp_norm_g"] = _grp_bwd(ys, sp["grp_norm_g"][l][None], d_yn, name="grp_bwd")
    dq, dk, dv, dbias = _datt_bwd(sv["proj"], bias, sv["y_d"], sv["lse_d"], dyd, name="att_bwd")
    dhc, gs["conv_ln_g"], gs["conv_ln_b"], g_pw = _conv2_bwd(sv["hc"], sp["conv_ln_g"][l][None], sp["conv_ln_b"][l][None],
                                                               bw["conv_w_pw"], dyc, name="conv_pw_bwd")
    gb["conv_w_pw"] = _stack_rows(g_pw)
    dval, dgate, g_dw, gs["conv_b_dw"] = _conv1_bwd(sv["proj"], dhc, bw["conv_w_dw"], name="conv_dw_bwd")
    gb["conv_w_dw"] = jnp.transpose(g_dw.reshape(CONV_PAD, N_CHIPS, GW // N_CHIPS), (1, 0, 2))
    du_b, gs["pool_w"], g_ps = _pool_bwd(sv["proj"], dyb, sp["pool_w"][l], sv["pool_s"], name="pool_bwd")
    gs["pool_scale"] = g_ps.reshape(-1)
    s5 = sv["s5"]
    conj = (s5["ab"][0], -s5["ab"][1])
    pw, tr, ti = _cpow_tables(*conj)
    du_a, g_glu, g_d, dcdr, dcdi, dbdr, dbdi, dar, dai = _s5_bwd(
        sv["proj"], sv["xr"], sv["xi"], dya, s5["bdr"], s5["bdi"], pw, tr[::-1], ti[::-1], s5["cdr"], s5["cdi"], s5["d"], s5["wglu"],
        name="s5_bwd")
    gb["s5_w_glu"] = _stack_rows(g_glu)
    gs["s5_d"] = g_d.reshape(-1)
    gs["s5_c_re"] = jnp.transpose(_bd_extract(dcdr, G), (0, 2, 1))
    gs["s5_c_im"] = jnp.transpose(_bd_extract(dcdi, G), (0, 2, 1))
    d_bb_r = jnp.transpose(_bd_extract(dbdr, G), (0, 2, 1))
    d_bb_i = jnp.transpose(_bd_extract(dbdi, G), (0, 2, 1))
    d_ab_r = jnp.sum(dar, axis=0).reshape(G, S5_STATE)
    d_ab_i = jnp.sum(dai, axis=0).reshape(G, S5_STATE)
    gs["s5_lam_re"], gs["s5_lam_im"], gs["s5_log_dt"], gs["s5_b_re"], gs["s5_b_im"] = s5["vjp"]((d_ab_r, d_ab_i, d_bb_r, d_bb_i))
    d_proj = jnp.concatenate([du_a, du_b, dval, dgate, dq, dk, dv], axis=1)
    gb["w_in"] = _mm(sv["xn"], d_proj, "tn", out_dtype=BF16, out_stack=N_CHIPS, name="proj_in_dw")
    d_xn = _mm(d_proj, bw["w_in"], "nt", name="proj_in_dx")
    dh0, gs["norm_mix_g"] = _rms_bwd(sv["h"], sp["norm_mix_g"][l][None], d_xn, dh1, name="norm_mix_bwd")
    gs = {n: g.reshape(sp[n].shape[1:]) for n, g in gs.items()}
    return dh0, d_memn, dbias, gs, gb


def _device_step(x, mem, target, sp, get_big, on_grads):
    nl = sp["norm_mix_g"].shape[0]
    mem_n = _rms_fwd(mem, sp["mem_norm_g"][None], name="norm_mem")
    bias = _att_bias(sp["rel_bias"])
    h, saved, big = x, [], {n: [] for n in SHARDED}
    for l in range(nl):
        bw = get_big(l, h)
        h, sv = _layer_fwd(h, mem_n, bias, sp, bw, l)
        saved.append(sv)
        for n in SHARDED:
            big[n].append(bw[n])
    loss, dh, g_final = _loss_head(h, sp["norm_final_g"][None], target, name="loss_head")
    d_memn = jnp.zeros(mem.shape, F32)
    dbias = jnp.zeros(bias.shape, F32)
    sg = {n: [None] * nl for n in SMALL}
    for l in reversed(range(nl)):
        dh, d_memn, dbias_l, gs, gb = _layer_bwd(dh, d_memn, mem_n, bias, sp, {n: big[n][l] for n in SHARDED}, saved[l], l)
        dbias = dbias + dbias_l
        for n, g in gs.items():
            sg[n][l] = g
        dh = on_grads(l, gb, dh)
    small = {n: jnp.stack(g) for n, g in sg.items() if g[0] is not None}
    small["norm_final_g"] = g_final.reshape(-1)
    _, g_mem = _rms_bwd(mem, sp["mem_norm_g"][None], d_memn, None, name="norm_mem_bwd")
    small["mem_norm_g"] = g_mem.reshape(-1)
    small["rel_bias"] = _rel_bias_grad(dbias)
    return loss, dh, small


def _gather_big(shards, prep):
    nl = shards["w_in"].shape[0]
    token = jnp.zeros((8, 128), F32)
    stacked = [pltpu.with_memory_space_constraint(
        jnp.pad(shards[n], ((0, 0), (0, CONV_PAD - CONV_WIDTH), (0, 0))) if n == "conv_w_dw" else shards[n].astype(BF16), pltpu.HBM)
        for n in SHARDED]
    pending = []
    for l in range(nl):
        ssem, rsem, lands, token = _ag_start(stacked, l, token, name=f"ag_start_l{l}")
        pending.append((ssem, rsem, lands))

    names = list(SHARDED)

    class LayerWeights(dict):
        def __init__(self, l, after):
            super().__init__()
            self.l, self.state = l, pending[l]
            first = [i for i, n in enumerate(names) if n in AG_FIRST] if l == 0 else list(range(len(names)))
            self.left = [i for i in range(len(names)) if i not in first]
            self._complete(first, [token] + list(prep) if l == 0 else [after], "a")

        def _complete(self, which, after, tag):
            ssem, rsem, lands = self.state
            xs = [stacked[i] for i in which]
            lands = _ag_wait(ssem, rsem, xs, [lands[i] for i in which], after, which, self.l, name=f"ag_wait_l{self.l}{tag}")
            fulls = _ag_finish(lands, xs, self.l, name=f"ag_finish_l{self.l}{tag}")
            for i, x, full in zip(which, xs, fulls):
                n = names[i]
                full = full.reshape(N_CHIPS, x.shape[1], x.shape[2])
                self[n] = jnp.transpose(full, (1, 0, 2)).reshape(CONV_PAD, -1) if n == "conv_w_dw" else _mat(full, SHARDED[n])

        def rest(self, after):
            if self.left:
                self._complete(self.left, [after], "b")
                self.left = []

    return LayerWeights


def _pack_small(vals, extra=None):
    flat = [vals[n].reshape(-1).astype(F32) for n in SMALL]
    flat.append(jnp.zeros((1,), F32) if extra is None else extra.reshape(1))
    v = jnp.concatenate(flat)
    n_pad = -(-v.shape[0] // SMALL_ALIGN) * SMALL_ALIGN
    return jnp.pad(v, (0, n_pad - v.shape[0]))


def _unpack_small(flat, like):
    out, pos = {}, 0
    for n in SMALL:
        size = math.prod(like[n].shape)
        out[n] = flat[pos:pos + size].reshape(like[n].shape)
        pos += size
    return out, flat[pos]


def kernel(x, mem, rel_bias, mem_norm_g, norm_mix_g, w_in, s5_lam_re, s5_lam_im, s5_log_dt, s5_b_re, s5_b_im, s5_c_re, s5_c_im, s5_d, s5_w_glu, pool_w, pool_scale, conv_w_dw, conv_b_dw, conv_ln_g, conv_ln_b, conv_w_pw, grp_norm_g, w_out, norm_x_g, w_xq, w_xk, w_xv, w_xo, norm_mlp_g, w_up, w_down, norm_final_g, loss_target, m_rel_bias, m_mem_norm_g, m_norm_mix_g, m_w_in, m_s5_lam_re, m_s5_lam_im, m_s5_log_dt, m_s5_b_re, m_s5_b_im, m_s5_c_re, m_s5_c_im, m_s5_d, m_s5_w_glu, m_pool_w, m_pool_scale, m_conv_w_dw, m_conv_b_dw, m_conv_ln_g, m_conv_ln_b, m_conv_w_pw, m_grp_norm_g, m_w_out, m_norm_x_g, m_w_xq, m_w_xk, m_w_xv, m_w_xo, m_norm_mlp_g, m_w_up, m_w_down, m_norm_final_g, v_rel_bias, v_mem_norm_g, v_norm_mix_g, v_w_in, v_s5_lam_re, v_s5_lam_im, v_s5_log_dt, v_s5_b_re, v_s5_b_im, v_s5_c_re, v_s5_c_im, v_s5_d, v_s5_w_glu, v_pool_w, v_pool_scale, v_conv_w_dw, v_conv_b_dw, v_conv_ln_g, v_conv_ln_b, v_conv_w_pw, v_grp_norm_g, v_w_out, v_norm_x_g, v_w_xq, v_w_xk, v_w_xv, v_w_xo, v_norm_mlp_g, v_w_up, v_w_down, v_norm_final_g):
    env = dict(locals())
    w = {n: env[n] for n in WEIGHTS}
    m = {n: env["m_" + n] for n in WEIGHTS}
    v = {n: env["v_" + n] for n in WEIGHTS}
    sp = {n: w[n] for n in SMALL}
    small_wmv = [_pack_small(t).reshape(1, 2, -1, 128) for t in (w, m, v)]
    get_big = _gather_big({n: w[n] for n in SHARDED}, small_wmv)
    nl = w["w_in"].shape[0]
    names = list(SHARDED)
    reducers, reduced, tokens = {}, {}, {}

    def on_grads(l, gb, dh):
        reducers[l] = _GradReducer(f"l{l}", BF16)
        token = tokens[l] = reducers[l].start(names, [gb[n] for n in names])
        if l + 1 in reducers:
            reduced[l + 1] = reducers[l + 1].finish(dh)
        return dh + token[0, 0]

    loss, grad_x, g_small = _device_step(x[0], mem[0], loss_target[0], sp, get_big, on_grads)

    def shard_views(n):
        pad = ((0, 0), (0, CONV_PAD - CONV_WIDTH), (0, 0)) if n == "conv_w_dw" else None
        wmv = [jnp.pad(t[n], pad) if pad else t[n] for t in (w, m, v)]
        _, R, C = wmv[0].shape
        return [t.reshape(nl, 2, R // 2, C) for t in wmv]

    grad, delta, new_m, new_v = {}, {}, {}, {}
    busy, upper = [grad_x], {}
    if nl > 1:
        filled = _sib_fill([reduced[l][n] for n in names for l in range(1, nl)], [tokens[0]], name="rs_fill_upper")
        for i, n in enumerate(names):
            upper[n] = _adamw(*shard_views(n), filled[i * (nl - 1):(i + 1) * (nl - 1)], first=1, name=f"adamw_upper_{n}")
            busy.append(upper[n][0])
    reduced[0] = reducers[0].finish(busy)
    packed = _pack_small(g_small, loss).reshape(N_CHIPS, -1, 128)
    small = _GradReducer("small", F32)
    token = small.start(["small"], [packed], after=[reduced[0][names[0]]])
    quarter = _sib_fill([small.finish(token)["small"]], name="rs_fill_small")[0]
    total = _ag4(quarter.reshape(-1, 128), name="ag_small").reshape(2, -1, 128)
    outs = _adamw(*small_wmv, [total], name="adamw_small")
    filled = _sib_fill([reduced[0][n] for n in names], name="rs_fill_first")
    for i, n in enumerate(names):
        res = _adamw(*shard_views(n), [filled[i]], prev=upper.get(n), name=f"adamw_first_{n}")
        R = w[n].shape[1]
        grad[n], delta[n], new_m[n], new_v[n] = [o.reshape(nl, -1, o.shape[-1])[:, :R] for o in res]
    loss_sum = None
    for o, dst in zip(outs, (grad, delta, new_m, new_v)):
        vals, last = _unpack_small(o.reshape(-1), sp)
        dst.update(vals)
        loss_sum = last if loss_sum is None else loss_sum
    return (loss_sum, grad_x[None], *[grad[n] for n in WEIGHTS], *[delta[n] for n in WEIGHTS],
            *[new_m[n] for n in WEIGHTS], *[new_v[n] for n in WEIGHTS])
```

```python
import functools
import math

import jax
import jax.numpy as jnp
import numpy as np
from jax import lax
from jax.experimental import pallas as pl
from jax.experimental.pallas import tpu as pltpu

F32 = jnp.float32
BF16 = jnp.bfloat16
MESH = pl.DeviceIdType.MESH

N_MIXERS = 4
S5_CH = 16
S5_STATE = 64
POOL_WINDOWS = (2, 4, 8, 16)
CONV_WIDTH = 31
CONV_PAD = 32
ATT_HEADS = 8
ATT_BLOCK = 128
DILATED_PATTERNS = ((128, 1), (512, 4), (2048, 16))
REL_BUCKETS = 32
REL_MAX_DIST = 2048
X_HEADS = 4
X_HEAD_DIM = 128
NORM_EPS = 1e-6
NEG_INF = -1e30
ADAM_LR, ADAM_B1, ADAM_B2, ADAM_EPS, ADAM_WD, ADAM_STEP = 0.001, 0.9, 0.999, 1e-08, 0.01, 10
N_CHIPS = 4
VMEM_LIMIT = 56 * 1024 * 1024


def _cparams(sem=None, vmem=None):
    return pltpu.CompilerParams(dimension_semantics=sem, vmem_limit_bytes=vmem)


def _tile(n, pref):
    if n <= pref:
        return n
    t = pref
    while t >= 128:
        if n % t == 0:
            return t
        t -= 128
    return n


def _spec(arr, tr, tc, rc):
    if arr.ndim == 2:
        return pl.BlockSpec((tr, tc), lambda *g: rc(*g))
    per = arr.shape[2] // tc
    assert arr.shape[2] % tc == 0

    def imap(*g):
        r, c = rc(*g)
        return (c // per, r, c % per)

    return pl.BlockSpec((None, tr, tc), imap)


def _lshape(arr):
    return arr.shape if arr.ndim == 2 else (arr.shape[1], arr.shape[0] * arr.shape[2])


def _mm(a, b, mode, *, name, out_dtype=F32, res=None, epi=None, aux=None, out_stack=None, tm=1024, tn=1024, tk=2048):
    la, lb = _lshape(a), _lshape(b)
    if mode == "nn":
        (M, K), (K2, N) = la, lb
    elif mode == "nt":
        (M, K), (N, K2) = la, lb
    else:
        (K, M), (K2, N) = la, lb
    assert K == K2, (la, lb, mode)
    lim = {"m": M, "n": N // out_stack if out_stack else N, "k": K}
    stacked = [(a, "m" if mode == "tn" else "k"), (b, "k" if mode == "nt" else "n"), (res, "n"), (aux, "n")]
    for arr, dim in stacked:
        if arr is not None and arr.ndim == 3:
            lim[dim] = math.gcd(lim[dim], arr.shape[2])
    tm, tn, tk = _tile(lim["m"], tm), _tile(lim["n"], tn), _tile(lim["k"], tk)
    nk = K // tk
    grid = (M // tm, N // tn, nk)
    a_spec = _spec(a, tk, tm, lambda i, j, k: (k, i)) if mode == "tn" else _spec(a, tm, tk, lambda i, j, k: (i, k))
    b_spec = _spec(b, tn, tk, lambda i, j, k: (j, k)) if mode == "nt" else _spec(b, tk, tn, lambda i, j, k: (k, j))
    dims = {"nn": ((1,), (0,)), "nt": ((1,), (1,)), "tn": ((0,), (0,))}[mode]
    ins, in_specs = [a, b], [a_spec, b_spec]
    for extra in (res, aux):
        if extra is not None:
            ins.append(extra)
            in_specs.append(_spec(extra, tm, tn, lambda i, j, k: (i, j)))
    if out_stack:
        o_shape = jax.ShapeDtypeStruct((out_stack, M, N // out_stack), out_dtype)
    else:
        o_shape = jax.ShapeDtypeStruct((M, N), out_dtype)
    o_spec = _spec(o_shape, tm, tn, lambda i, j, k: (i, j))
    n_out = 2 if epi == "relu2" else 1
    has_res, has_aux = res is not None, aux is not None

    def body(*refs):
        a_ref, b_ref = refs[0], refs[1]
        pos = 2
        res_ref = aux_ref = None
        if has_res:
            res_ref = refs[pos]
            pos += 1
        if has_aux:
            aux_ref = refs[pos]
            pos += 1
        outs = refs[pos:pos + n_out]
        part = lax.dot_general(a_ref[...].astype(BF16), b_ref[...].astype(BF16), (dims, ((), ())), preferred_element_type=F32)
        if nk > 1:
            acc_ref = refs[pos + n_out]
            k = pl.program_id(2)

            @pl.when(k == 0)
            def _():
                acc_ref[...] = part

            @pl.when(k > 0)
            def _():
                acc_ref[...] += part

        @pl.when(pl.program_id(2) == nk - 1)
        def _():
            o = acc_ref[...] if nk > 1 else part
            if has_res:
                o = o + res_ref[...].astype(F32)
            if epi == "relu2":
                outs[0][...] = o.astype(outs[0].dtype)
                r = jnp.maximum(o, 0.0)
                outs[1][...] = (r * r).astype(outs[1].dtype)
            elif epi == "relu2_bwd":
                outs[0][...] = (o * (2.0 * jnp.maximum(aux_ref[...].astype(F32), 0.0))).astype(outs[0].dtype)
            else:
                outs[0][...] = o.astype(outs[0].dtype)

    out = pl.pallas_call(
        body, grid=grid, in_specs=in_specs,
        out_specs=[o_spec] * n_out if n_out > 1 else o_spec,
        out_shape=[o_shape] * n_out if n_out > 1 else o_shape,
        scratch_shapes=[pltpu.VMEM((tm, tn), F32)] if nk > 1 else [],
        compiler_params=_cparams(("parallel", "parallel", "arbitrary"), VMEM_LIMIT), name=name,
    )(*ins)
    return out


def _rms_fwd(x, g, *, name, out_dtype=BF16):
    L, D = x.shape
    tr = _tile(L, 256)

    def body(x_ref, g_ref, o_ref):
        xv = x_ref[...]
        r = lax.rsqrt(jnp.mean(xv * xv, axis=-1, keepdims=True) + NORM_EPS)
        o_ref[...] = (xv * r * g_ref[...]).astype(o_ref.dtype)

    return pl.pallas_call(
        body, grid=(L // tr,),
        in_specs=[pl.BlockSpec((tr, D), lambda i: (i, 0)), pl.BlockSpec((1, D), lambda i: (0, 0))],
        out_specs=pl.BlockSpec((tr, D), lambda i: (i, 0)),
        out_shape=jax.ShapeDtypeStruct((L, D), out_dtype),
        compiler_params=_cparams(("parallel",)), name=name)(x, g)


def _rms_bwd(x, g, dy, dres, *, name):
    L, D = x.shape
    tr = _tile(L, 256)
    has_res = dres is not None

    def body(*refs):
        x_ref, g_ref, dy_ref = refs[:3]
        dres_ref = refs[3] if has_res else None
        dx_ref, dg_ref = refs[-2:]
        xv = x_ref[...]
        dyv = dy_ref[...].astype(F32)
        r = lax.rsqrt(jnp.mean(xv * xv, axis=-1, keepdims=True) + NORM_EPS)
        xh = xv * r
        dyg = dyv * g_ref[...]
        dx = r * (dyg - xh * jnp.mean(dyg * xh, axis=-1, keepdims=True))
        if has_res:
            dx = dx + dres_ref[...]
        dx_ref[...] = dx

        @pl.when(pl.program_id(0) == 0)
        def _():
            dg_ref[...] = jnp.zeros_like(dg_ref)

        dg_ref[...] += jnp.sum(dyv * xh, axis=0, keepdims=True)

    row = pl.BlockSpec((tr, D), lambda i: (i, 0))
    vec = pl.BlockSpec((1, D), lambda i: (0, 0))
    ins = [x, g, dy] + ([dres] if has_res else [])
    return pl.pallas_call(
        body, grid=(L // tr,), in_specs=[row, vec, row] + ([row] if has_res else []),
        out_specs=[row, vec],
        out_shape=[jax.ShapeDtypeStruct((L, D), F32), jax.ShapeDtypeStruct((1, D), F32)],
        compiler_params=_cparams(("arbitrary",)), name=name)(*ins)


def _loss_head(h, g, target, *, name):
    L, D = h.shape
    tr = _tile(L, 256)

    def body(x_ref, g_ref, t_ref, loss_ref, dx_ref, dg_ref):
        xv = x_ref[...]
        r = lax.rsqrt(jnp.mean(xv * xv, axis=-1, keepdims=True) + NORM_EPS)
        xh = xv * r
        err = xh * g_ref[...] - t_ref[...]
        dyv = err * (1.0 / D)
        dyg = dyv * g_ref[...]
        dx_ref[...] = r * (dyg - xh * jnp.mean(dyg * xh, axis=-1, keepdims=True))

        @pl.when(pl.program_id(0) == 0)
        def _():
            dg_ref[...] = jnp.zeros_like(dg_ref)
            loss_ref[...] = jnp.zeros_like(loss_ref)

        dg_ref[...] += jnp.sum(dyv * xh, axis=0, keepdims=True)
        loss_ref[...] += 0.5 * jnp.sum(jnp.sum(err * err, axis=1, keepdims=True), axis=0, keepdims=True) * (1.0 / D)

    row = pl.BlockSpec((tr, D), lambda i: (i, 0))
    vec = pl.BlockSpec((1, D), lambda i: (0, 0))
    return pl.pallas_call(
        body, grid=(L // tr,), in_specs=[row, vec, row],
        out_specs=[pl.BlockSpec((1, 1), lambda i: (0, 0)), row, vec],
        out_shape=[jax.ShapeDtypeStruct((1, 1), F32), jax.ShapeDtypeStruct((L, D), F32), jax.ShapeDtypeStruct((1, D), F32)],
        compiler_params=_cparams(("arbitrary",)), name=name)(h, g, target)


S5_TL = 256
S5_LW = 512


def _dot(a, b, dims):
    return lax.dot_general(a, b, (dims, ((), ())), preferred_element_type=F32)


_NN, _NT, _TN = ((1,), (0,)), ((1,), (1,)), ((0,), (0,))


def _gelu_and_grad(y):
    k = math.sqrt(2.0 / math.pi)
    y2 = y * y
    th = jnp.tanh(k * (y + 0.044715 * y * y2))
    g = 0.5 * y * (1.0 + th)
    gp = 0.5 * (1.0 + th) + 0.5 * y * (1.0 - th * th) * k * (1.0 + 3.0 * 0.044715 * y2)
    return g, gp


def _scan_tiles(re_s, im_s, ls, lw, pw_ref, tr_ref, ti_ref, carry_s, nt, reverse, extra=None):
    row = lax.broadcasted_iota(jnp.int32, (8, lw), 0)
    a = [pw_ref[k:k + 1, ls] for k in range(6)]
    tr_t, ti_t = tr_ref[:, ls], ti_ref[:, ls]
    edge = 0 if reverse else 7

    def tile(jj, carry):
        cr, ci, acc = carry
        j = (nt - 1 - jj) if reverse else jj
        off = pl.multiple_of(j * 8, 8)
        sr, si = re_s[pl.ds(off, 8), ls], im_s[pl.ds(off, 8), ls]
        for n, k in enumerate((1, 2, 4)):
            akr, aki = a[2 * n], a[2 * n + 1]
            if reverse:
                keep, sh = row < 8 - k, 8 - k
            else:
                keep, sh = row >= k, k
            shr = jnp.where(keep, pltpu.roll(sr, sh, 0), 0.0)
            shi = jnp.where(keep, pltpu.roll(si, sh, 0), 0.0)
            sr, si = sr + akr * shr - aki * shi, si + akr * shi + aki * shr
        xr = sr + tr_t * cr - ti_t * ci
        xi = si + tr_t * ci + ti_t * cr
        re_s[pl.ds(off, 8), ls] = xr
        im_s[pl.ds(off, 8), ls] = xi
        if extra is not None:
            acc = extra(off, xr, xi, acc, row)
        return xr[edge:edge + 1, :], xi[edge:edge + 1, :], acc

    acc0 = (jnp.zeros((8, lw), F32), jnp.zeros((8, lw), F32)) if extra is not None else 0
    cr, ci, acc = lax.fori_loop(0, nt, tile, (carry_s[0:1, ls], carry_s[1:2, ls], acc0))
    carry_s[0:1, ls] = cr
    carry_s[1:2, ls] = ci
    return acc


def _s5_fwd(proj, bdr, bdi, pw, tr, ti, cdr, cdi, dskip, wglu, *, name):
    L = proj.shape[0]
    GW, S = bdr.shape
    TL = _tile(L, S5_TL)
    lw = min(S, S5_LW)

    def body(u_ref, bdr_ref, bdi_ref, pw_ref, tr_ref, ti_ref, cdr_ref, cdi_ref, d_ref, w_ref,
             y_ref, xr_ref, xi_ref, re_s, im_s, carry_s):
        @pl.when(pl.program_id(0) == 0)
        def _():
            carry_s[...] = jnp.zeros_like(carry_s)

        u = u_ref[...]
        ub = u.astype(BF16)
        re_s[...] = _dot(ub, bdr_ref[...], _NN)
        im_s[...] = _dot(ub, bdi_ref[...], _NN)
        for lc in range(S // lw):
            _scan_tiles(re_s, im_s, slice(lc * lw, (lc + 1) * lw), lw, pw_ref, tr_ref, ti_ref, carry_s, TL // 8, False)
        xrb, xib = re_s[...].astype(BF16), im_s[...].astype(BF16)
        xr_ref[...] = xrb
        xi_ref[...] = xib
        y = _dot(xrb, cdr_ref[...], _NN) - _dot(xib, cdi_ref[...], _NN) + d_ref[...] * u
        g, _ = _gelu_and_grad(y)
        z = _dot(g.astype(BF16), w_ref[...], _NN)
        y_ref[...] = g * jax.nn.sigmoid(z)

    full = lambda arr: pl.BlockSpec(arr.shape, lambda i: (0,) * arr.ndim)
    return pl.pallas_call(
        body, grid=(L // TL,),
        in_specs=[pl.BlockSpec((TL, GW), lambda i: (i, 0))] + [full(t) for t in (bdr, bdi, pw, tr, ti, cdr, cdi, dskip, wglu)],
        out_specs=[pl.BlockSpec((TL, GW), lambda i: (i, 0)), pl.BlockSpec((TL, S), lambda i: (i, 0)), pl.BlockSpec((TL, S), lambda i: (i, 0))],
        out_shape=[jax.ShapeDtypeStruct((L, GW), F32), jax.ShapeDtypeStruct((L, S), BF16), jax.ShapeDtypeStruct((L, S), BF16)],
        scratch_shapes=[pltpu.VMEM((TL, S), F32), pltpu.VMEM((TL, S), F32), pltpu.VMEM((8, S), F32)],
        compiler_params=_cparams(("arbitrary",), VMEM_LIMIT), name=name,
    )(proj, bdr, bdi, pw, tr, ti, cdr, cdi, dskip, wglu)


def _s5_bwd(proj, xr, xi, dout, bdr, bdi, pwc, trc, tic, cdr, cdi, dskip, wglu, *, name):
    L = proj.shape[0]
    GW, S = bdr.shape
    TL = _tile(L, S5_TL)
    nc = L // TL
    lw = min(S, S5_LW)

    def body(u_ref, xr_ref, xi_ref, xrp_ref, xip_ref, do_ref, bdr_ref, bdi_ref, pw_ref, tr_ref, ti_ref, cdr_ref, cdi_ref,
             d_ref, w_ref, du_ref, dw_ref, dd_ref, dcdr_ref, dcdi_ref, dbdr_ref, dbdi_ref, dar_ref, dai_ref,
             gr_s, gi_s, xfr, xfi, carry_s):
        i = pl.program_id(0)

        @pl.when(i == 0)
        def _():
            carry_s[...] = jnp.zeros_like(carry_s)
            for r in (dw_ref, dd_ref, dcdr_ref, dcdi_ref, dbdr_ref, dbdi_ref, dar_ref, dai_ref):
                r[...] = jnp.zeros_like(r)

        u = u_ref[...]
        ub = u.astype(BF16)
        xrb, xib = xr_ref[...], xi_ref[...]
        y = _dot(xrb, cdr_ref[...], _NN) - _dot(xib, cdi_ref[...], _NN) + d_ref[...] * u
        g, gp = _gelu_and_grad(y)
        gb = g.astype(BF16)
        sg = jax.nn.sigmoid(_dot(gb, w_ref[...], _NN))
        dout = do_ref[...]
        dz = (dout * g * sg * (1.0 - sg)).astype(BF16)
        dg = dout * sg + _dot(dz, w_ref[...], _NT)
        dw_ref[...] += _dot(gb, dz, _TN)
        dy = dg * gp
        dyb = dy.astype(BF16)
        dd_ref[...] += jnp.sum(dy * u, axis=0, keepdims=True)
        gr_s[...] = _dot(dyb, cdr_ref[...], _NT)
        gi_s[...] = -_dot(dyb, cdi_ref[...], _NT)
        dcdr_ref[...] += _dot(xrb, dyb, _TN)
        dcdi_ref[...] -= _dot(xib, dyb, _TN)
        live = (i < nc - 1).astype(F32)
        xfr[0:8, :] = xrp_ref[...].astype(F32) * live
        xfi[0:8, :] = xip_ref[...].astype(F32) * live
        xfr[8:, :] = xrb.astype(F32)
        xfi[8:, :] = xib.astype(F32)
        for lc in range(S // lw):
            ls = slice(lc * lw, (lc + 1) * lw)

            def extra(off, lr, li, acc, row, ls=ls):
                cur_r, cur_i = xfr[pl.ds(off + 8, 8), ls], xfi[pl.ds(off + 8, 8), ls]
                prv_r, prv_i = xfr[pl.ds(off, 8), ls], xfi[pl.ds(off, 8), ls]
                xpr = jnp.where(row >= 1, pltpu.roll(cur_r, 1, 0), prv_r[7:8, :])
                xpi = jnp.where(row >= 1, pltpu.roll(cur_i, 1, 0), prv_i[7:8, :])
                return acc[0] + lr * xpr + li * xpi, acc[1] - lr * xpi + li * xpr

            acc = _scan_tiles(gr_s, gi_s, ls, lw, pw_ref, tr_ref, ti_ref, carry_s, TL // 8, True, extra)
            dar_ref[:, ls] += acc[0]
            dai_ref[:, ls] += acc[1]
        lrb, lib = gr_s[...].astype(BF16), gi_s[...].astype(BF16)
        du_ref[...] = dy * d_ref[...] + _dot(lrb, bdr_ref[...], _NT) + _dot(lib, bdi_ref[...], _NT)
        dbdr_ref[...] += _dot(ub, lrb, _TN)
        dbdi_ref[...] += _dot(ub, lib, _TN)

    rev = lambda i: nc - 1 - i
    full = lambda arr: pl.BlockSpec(arr.shape, lambda i: (0,) * arr.ndim)
    chunk = lambda w: pl.BlockSpec((TL, w), lambda i: (rev(i), 0))
    prev8 = pl.BlockSpec((8, S), lambda i: (jnp.maximum(rev(i) * (TL // 8) - 1, 0), 0))
    acc_shapes = [(GW, GW), (1, GW), (S, GW), (S, GW), (GW, S), (GW, S), (8, S), (8, S)]
    return pl.pallas_call(
        body, grid=(nc,),
        in_specs=[chunk(GW), chunk(S), chunk(S), prev8, prev8, chunk(GW)] + [full(t) for t in (bdr, bdi, pwc, trc, tic, cdr, cdi, dskip, wglu)],
        out_specs=[chunk(GW)] + [pl.BlockSpec(s, lambda i: (0, 0)) for s in acc_shapes],
        out_shape=[jax.ShapeDtypeStruct((L, GW), F32)] + [jax.ShapeDtypeStruct(s, F32) for s in acc_shapes],
        scratch_shapes=[pltpu.VMEM((TL, S), F32), pltpu.VMEM((TL, S), F32), pltpu.VMEM((TL + 8, S), F32),
                        pltpu.VMEM((TL + 8, S), F32), pltpu.VMEM((8, S), F32)],
        compiler_params=_cparams(("arbitrary",), VMEM_LIMIT), name=name,
    )(proj, xr, xi, xr, xi, dout, bdr, bdi, pwc, trc, tic, cdr, cdi, dskip, wglu)


def _cpow_tables(ar, ai):
    def cm(a, b):
        return a[0] * b[0] - a[1] * b[1], a[0] * b[1] + a[1] * b[0]
    p = [(ar, ai)]
    for _ in range(7):
        p.append(cm(p[-1], p[0]))
    z = jnp.zeros_like(ar)
    pw = jnp.stack([p[0][0], p[0][1], p[1][0], p[1][1], p[3][0], p[3][1], z, z])
    return pw, jnp.stack([q[0] for q in p]), jnp.stack([q[1] for q in p])


def _s5_disc(lam_re, lam_im, log_dt, b_re, b_im):
    dt = jnp.exp(log_dt)[:, None]
    mag = jnp.exp(lam_re * dt)
    ab_r, ab_i = mag * jnp.cos(lam_im * dt), mag * jnp.sin(lam_im * dt)
    den = lam_re * lam_re + lam_im * lam_im
    nr, ni = ab_r - 1.0, ab_i
    f_r = ((nr * lam_re + ni * lam_im) / den)[..., None]
    f_i = ((ni * lam_re - nr * lam_im) / den)[..., None]
    return ab_r, ab_i, f_r * b_re - f_i * b_im, f_r * b_im + f_i * b_re


def _bd(t):
    G, A, B = t.shape
    return (t[:, :, None, :] * jnp.eye(G, dtype=t.dtype)[:, None, :, None]).reshape(G * A, G * B)


def _bd_extract(m, G):
    A, B = m.shape[0] // G, m.shape[1] // G
    return jnp.sum(m.reshape(G, A, G, B) * jnp.eye(G, dtype=m.dtype)[:, None, :, None], axis=2)


POOL_TQ = 256


def _band(shape, row0, col0, win, transpose):
    r = lax.broadcasted_iota(jnp.int32, shape, 0) + row0
    c = lax.broadcasted_iota(jnp.int32, shape, 1) + col0
    d = (c - r) if transpose else (r - c)
    return ((d >= 0) & (d < win)).astype(BF16)


def _band_dot(band, v):
    hi = v.astype(BF16)
    lo = (v - hi.astype(F32)).astype(BF16)
    return _dot(band, hi, _NN) + _dot(band, lo, _NN)


def _pool_p(u_prev, u_cur, i, win, tq):
    t0 = i * tq
    cnt = jnp.minimum(lax.broadcasted_iota(jnp.int32, (tq, 1), 0) + t0 + 1, win).astype(F32)
    live = (i > 0).astype(F32)
    acc = _band_dot(_band((tq, tq), t0, t0, win, False), u_cur)
    acc += _band_dot(_band((tq, tq), t0, t0 - tq, win, False), u_prev * live)
    return acc / cnt - u_cur


def _pool_fwd(proj, pool_w, pool_scale3, *, name):
    L = proj.shape[0]
    nw, PC, _ = pool_w.shape
    tq = _tile(L, POOL_TQ)

    def body(up_ref, uc_ref, w_ref, s_ref, y_ref):
        g, i = pl.program_id(0), pl.program_id(1)
        win = jnp.left_shift(2, g)
        p = _pool_p(up_ref[...], uc_ref[...], i, win, tq)
        y_ref[...] = _dot(p.astype(BF16), w_ref[...].astype(BF16), _NN) * s_ref[...]

    return pl.pallas_call(
        body, grid=(nw, L // tq),
        in_specs=[pl.BlockSpec((tq, PC), lambda g, i: (jnp.maximum(i - 1, 0), nw + g)),
                  pl.BlockSpec((tq, PC), lambda g, i: (i, nw + g)),
                  pl.BlockSpec((None, PC, PC), lambda g, i: (g, 0, 0)),
                  pl.BlockSpec((None, 1, PC), lambda g, i: (g, 0, 0))],
        out_specs=pl.BlockSpec((tq, PC), lambda g, i: (i, g)),
        out_shape=jax.ShapeDtypeStruct((L, nw * PC), F32),
        compiler_params=_cparams(("parallel", "parallel")), name=name)(proj, proj, pool_w, pool_scale3)


def _pool_bwd(proj, dy, pool_w, pool_scale3, *, name):
    L = proj.shape[0]
    nw, PC, _ = pool_w.shape
    tq = _tile(L, POOL_TQ)
    nq = L // tq

    def body(up_ref, uc_ref, dyc_ref, dyn_ref, w_ref, s_ref, du_ref, dw_ref, ds_ref):
        g, i = pl.program_id(0), pl.program_id(1)
        win = jnp.left_shift(2, g)

        @pl.when(i == 0)
        def _():
            dw_ref[...] = jnp.zeros_like(dw_ref)
            ds_ref[...] = jnp.zeros_like(ds_ref)

        wb = w_ref[...].astype(BF16)
        p = _pool_p(up_ref[...], uc_ref[...], i, win, tq).astype(BF16)
        dyc = dyc_ref[...]
        ds_ref[...] += jnp.sum(dyc * _dot(p, wb, _NN), axis=0, keepdims=True)
        dys = (dyc * s_ref[...]).astype(BF16)
        dw_ref[...] += _dot(p, dys, _TN)
        t0 = i * tq
        rows = lax.broadcasted_iota(jnp.int32, (tq, 1), 0)
        dp_c = _dot(dys, wb, _NT)
        q_c = dp_c / jnp.minimum(rows + t0 + 1, win).astype(F32)
        live = (i < nq - 1).astype(F32)
        dp_n = _dot((dyn_ref[...] * s_ref[...] * live).astype(BF16), wb, _NT)
        q_n = dp_n / jnp.minimum(rows + t0 + tq + 1, win).astype(F32)
        du = _band_dot(_band((tq, tq), t0, t0, win, True), q_c) + _band_dot(_band((tq, tq), t0, t0 + tq, win, True), q_n)
        du_ref[...] = du - dp_c

    return pl.pallas_call(
        body, grid=(nw, nq),
        in_specs=[pl.BlockSpec((tq, PC), lambda g, i: (jnp.maximum(i - 1, 0), nw + g)),
                  pl.BlockSpec((tq, PC), lambda g, i: (i, nw + g)),
                  pl.BlockSpec((tq, PC), lambda g, i: (i, g)),
                  pl.BlockSpec((tq, PC), lambda g, i: (jnp.minimum(i + 1, nq - 1), g)),
                  pl.BlockSpec((None, PC, PC), lambda g, i: (g, 0, 0)),
                  pl.BlockSpec((None, 1, PC), lambda g, i: (g, 0, 0))],
        out_specs=[pl.BlockSpec((tq, PC), lambda g, i: (i, g)),
                   pl.BlockSpec((None, PC, PC), lambda g, i: (g, 0, 0)),
                   pl.BlockSpec((None, 1, PC), lambda g, i: (g, 0, 0))],
        out_shape=[jax.ShapeDtypeStruct((L, nw * PC), F32), jax.ShapeDtypeStruct((nw, PC, PC), F32),
                   jax.ShapeDtypeStruct((nw, 1, PC), F32)],
        compiler_params=_cparams(("parallel", "arbitrary")), name=name)(proj, proj, dy, dy, pool_w, pool_scale3)


CONV_RC = 256


def _conv1_fwd(proj, w_dw, b_dw, *, name):
    L = proj.shape[0]
    GW = w_dw.shape[1]
    CB = min(GW, 128)
    nb = GW // CB
    RC = _tile(L, CONV_RC)
    n = RC + CONV_PAD

    def body(val_ref, gate_ref, w_ref, b_ref, o_ref, hp):
        hp[0:CONV_PAD, :] = jnp.zeros((CONV_PAD, CB), F32)
        hp[CONV_PAD:, :] = val_ref[...] * jax.nn.sigmoid(gate_ref[...])
        wv = w_ref[...]

        def chunk(ci, carry):
            c0 = pl.multiple_of(ci * RC, 8)
            win = hp[pl.ds(c0, n), :]
            acc = jnp.zeros((RC, CB), F32) + b_ref[...]
            for k in range(CONV_WIDTH):
                acc = acc + wv[k:k + 1, :] * pltpu.roll(win, n - (k + 2), 0)[0:RC]
            o_ref[pl.ds(c0, RC), :] = acc
            return carry

        lax.fori_loop(0, L // RC, chunk, 0)

    col = lambda off: pl.BlockSpec((L, CB), lambda c: (0, off * nb + c))
    return pl.pallas_call(
        body, grid=(nb,),
        in_specs=[col(2), col(3), pl.BlockSpec((CONV_PAD, CB), lambda c: (0, c)), pl.BlockSpec((1, CB), lambda c: (0, c))],
        out_specs=pl.BlockSpec((L, CB), lambda c: (0, c)),
        out_shape=jax.ShapeDtypeStruct((L, GW), F32),
        scratch_shapes=[pltpu.VMEM((L + CONV_PAD, CB), F32)],
        compiler_params=_cparams(("parallel",)), name=name)(proj, proj, w_dw, b_dw)


def _conv1_bwd(proj, dhc, w_dw, *, name):
    L = proj.shape[0]
    GW = w_dw.shape[1]
    CB = min(GW, 128)
    nb = GW // CB
    RC = _tile(L, CONV_RC)
    n = RC + CONV_PAD

    def body(val_ref, gate_ref, d_ref, w_ref, dval_ref, dgate_ref, dw_ref, db_ref, hp, dp):
        val = val_ref[...]
        sg = jax.nn.sigmoid(gate_ref[...])
        hp[0:CONV_PAD, :] = jnp.zeros((CONV_PAD, CB), F32)
        hp[CONV_PAD:, :] = val * sg
        dp[0:L, :] = d_ref[...]
        dp[L:, :] = jnp.zeros((CONV_PAD, CB), F32)
        dw_ref[...] = jnp.zeros_like(dw_ref)
        db_ref[...] = jnp.sum(d_ref[...], axis=0, keepdims=True)
        wv = w_ref[...]

        def chunk(ci, carry):
            c0 = pl.multiple_of(ci * RC, 8)
            win = hp[pl.ds(c0, n), :]
            dwin = dp[pl.ds(c0, n), :]
            d = dwin[0:RC]
            dh = jnp.zeros((RC, CB), F32)
            for k in range(CONV_WIDTH):
                dw_ref[k:k + 1, :] += jnp.sum(d * pltpu.roll(win, n - (k + 2), 0)[0:RC], axis=0, keepdims=True)
                sh = CONV_WIDTH - 1 - k
                dh = dh + wv[k:k + 1, :] * (pltpu.roll(dwin, n - sh, 0)[0:RC] if sh else d)
            v, s = val_ref[pl.ds(c0, RC), :], jax.nn.sigmoid(gate_ref[pl.ds(c0, RC), :])
            dval_ref[pl.ds(c0, RC), :] = dh * s
            dgate_ref[pl.ds(c0, RC), :] = dh * v * s * (1.0 - s)
            return carry

        lax.fori_loop(0, L // RC, chunk, 0)

    col = lambda off: pl.BlockSpec((L, CB), lambda c: (0, off * nb + c))
    own = pl.BlockSpec((L, CB), lambda c: (0, c))
    return pl.pallas_call(
        body, grid=(nb,),
        in_specs=[col(2), col(3), own, pl.BlockSpec((CONV_PAD, CB), lambda c: (0, c))],
        out_specs=[own, own, pl.BlockSpec((CONV_PAD, CB), lambda c: (0, c)), pl.BlockSpec((1, CB), lambda c: (0, c))],
        out_shape=[jax.ShapeDtypeStruct((L, GW), F32), jax.ShapeDtypeStruct((L, GW), F32),
                   jax.ShapeDtypeStruct((CONV_PAD, GW), F32), jax.ShapeDtypeStruct((1, GW), F32)],
        scratch_shapes=[pltpu.VMEM((L + CONV_PAD, CB), F32), pltpu.VMEM((L + CONV_PAD, CB), F32)],
        compiler_params=_cparams(("parallel",)), name=name)(proj, proj, dhc, w_dw)


def _ln_silu(hc, g, b):
    mu = jnp.mean(hc, axis=-1, keepdims=True)
    xc = hc - mu
    r = lax.rsqrt(jnp.mean(xc * xc, axis=-1, keepdims=True) + NORM_EPS)
    xh = xc * r
    a = xh * g + b
    sg = jax.nn.sigmoid(a)
    return xh, r, a, sg


def _conv2_fwd(hc, ln_g, ln_b, w_pw, *, name):
    L, GW = hc.shape
    tr = _tile(L, 256)

    def body(h_ref, g_ref, b_ref, w_ref, y_ref):
        _, _, a, sg = _ln_silu(h_ref[...], g_ref[...], b_ref[...])
        y_ref[...] = _dot((a * sg).astype(BF16), w_ref[...], _NN)

    row = pl.BlockSpec((tr, GW), lambda i: (i, 0))
    vec = pl.BlockSpec((1, GW), lambda i: (0, 0))
    return pl.pallas_call(
        body, grid=(L // tr,), in_specs=[row, vec, vec, pl.BlockSpec((GW, GW), lambda i: (0, 0))],
        out_specs=row, out_shape=jax.ShapeDtypeStruct((L, GW), F32),
        compiler_params=_cparams(("parallel",)), name=name)(hc, ln_g, ln_b, w_pw)


def _conv2_bwd(hc, ln_g, ln_b, w_pw, dy, *, name):
    L, GW = hc.shape
    tr = _tile(L, 256)

    def body(h_ref, g_ref, b_ref, w_ref, dy_ref, dh_ref, dg_ref, db_ref, dw_ref):
        @pl.when(pl.program_id(0) == 0)
        def _():
            for r in (dg_ref, db_ref, dw_ref):
                r[...] = jnp.zeros_like(r)

        xh, r, a, sg = _ln_silu(h_ref[...], g_ref[...], b_ref[...])
        dyb = dy_ref[...].astype(BF16)
        dw_ref[...] += _dot((a * sg).astype(BF16), dyb, _TN)
        da = _dot(dyb, w_ref[...], _NT) * (sg + a * sg * (1.0 - sg))
        dg_ref[...] += jnp.sum(da * xh, axis=0, keepdims=True)
        db_ref[...] += jnp.sum(da, axis=0, keepdims=True)
        dxh = da * g_ref[...]
        dh_ref[...] = r * (dxh - jnp.mean(dxh, axis=-1, keepdims=True) - xh * jnp.mean(dxh * xh, axis=-1, keepdims=True))

    row = pl.BlockSpec((tr, GW), lambda i: (i, 0))
    vec = pl.BlockSpec((1, GW), lambda i: (0, 0))
    mat = pl.BlockSpec((GW, GW), lambda i: (0, 0))
    return pl.pallas_call(
        body, grid=(L // tr,), in_specs=[row, vec, vec, mat, row],
        out_specs=[row, vec, vec, mat],
        out_shape=[jax.ShapeDtypeStruct((L, GW), F32), jax.ShapeDtypeStruct((1, GW), F32), jax.ShapeDtypeStruct((1, GW), F32),
                   jax.ShapeDtypeStruct((GW, GW), F32)],
        compiler_params=_cparams(("arbitrary",)), name=name)(hc, ln_g, ln_b, w_pw, dy)


def _grp_fwd(ys, g, *, name):
    L, GW = ys[0].shape
    tr = _tile(L, 256)

    def body(*refs):
        g_ref, o_ref = refs[4], refs[5]
        for m in range(N_MIXERS):
            yv = refs[m][...]
            r = lax.rsqrt(jnp.mean(yv * yv, axis=-1, keepdims=True) + NORM_EPS)
            o_ref[:, m * GW:(m + 1) * GW] = (yv * r * g_ref[:, m * GW:(m + 1) * GW]).astype(o_ref.dtype)

    row = pl.BlockSpec((tr, GW), lambda i: (i, 0))
    return pl.pallas_call(
        body, grid=(L // tr,), in_specs=[row] * 4 + [pl.BlockSpec((1, N_MIXERS * GW), lambda i: (0, 0))],
        out_specs=pl.BlockSpec((tr, N_MIXERS * GW), lambda i: (i, 0)),
        out_shape=jax.ShapeDtypeStruct((L, N_MIXERS * GW), BF16),
        compiler_params=_cparams(("parallel",)), name=name)(*ys, g)


def _grp_bwd(ys, g, dy, *, name):
    L, GW = ys[0].shape
    tr = _tile(L, 256)

    def body(*refs):
        g_ref, dy_ref = refs[4], refs[5]
        outs, dg_ref = refs[6:10], refs[10]

        @pl.when(pl.program_id(0) == 0)
        def _():
            dg_ref[...] = jnp.zeros_like(dg_ref)

        for m in range(N_MIXERS):
            cs = slice(m * GW, (m + 1) * GW)
            yv = refs[m][...]
            r = lax.rsqrt(jnp.mean(yv * yv, axis=-1, keepdims=True) + NORM_EPS)
            xh = yv * r
            dyv = dy_ref[:, cs]
            dyg = dyv * g_ref[:, cs]
            outs[m][...] = r * (dyg - xh * jnp.mean(dyg * xh, axis=-1, keepdims=True))
            dg_ref[:, cs] += jnp.sum(dyv * xh, axis=0, keepdims=True)

    row = pl.BlockSpec((tr, GW), lambda i: (i, 0))
    wide = pl.BlockSpec((tr, N_MIXERS * GW), lambda i: (i, 0))
    vec = pl.BlockSpec((1, N_MIXERS * GW), lambda i: (0, 0))
    return pl.pallas_call(
        body, grid=(L // tr,), in_specs=[row] * 4 + [vec, wide],
        out_specs=[row] * 4 + [vec],
        out_shape=[jax.ShapeDtypeStruct((L, GW), F32)] * 4 + [jax.ShapeDtypeStruct((1, N_MIXERS * GW), F32)],
        compiler_params=_cparams(("arbitrary",)), name=name)(*ys, g, dy)


ATT_PAR = 4


def _datt_lanes(GW):
    E = GW // ATT_HEADS
    lb = min(GW, 128)
    return lb, lb // E, E


def _datt_rows(c, br, nblk):
    dil = 4 ** br
    nbs = nblk // dil
    r, jb = c // nbs, c % nbs
    return r + dil * ATT_BLOCK * jb, r + dil * ATT_BLOCK * jnp.maximum(jb - 1, 0), jb == 0


def _datt_fwd(proj, bias, *, name):
    L = proj.shape[0]
    GW = proj.shape[1] // 7
    lb, hps, E = _datt_lanes(GW)
    B = ATT_BLOCK
    nblk = L // B
    scale = E ** -0.5

    def body(q_ref, k_ref, v_ref, b_ref, y_ref, lse_ref, m_s, l_s):
        m_s[...] = jnp.full(m_s.shape, NEG_INF, F32)
        l_s[...] = jnp.zeros(l_s.shape, F32)
        y_ref[...] = jnp.zeros(y_ref.shape, F32)
        lane = lax.broadcasted_iota(jnp.int32, (B, lb), 1) // E

        for br, (_, dil) in enumerate(DILATED_PATTERNS):
            def step(i, carry, br=br, dil=dil):
                loaded = []
                for u in range(ATT_PAR):
                    start, pstart, first = _datt_rows(i + u * (nblk // ATT_PAR), br, nblk)
                    rows, prows = pl.ds(start, B, stride=dil), pl.ds(pstart, B, stride=dil)
                    loaded.append((rows, first, q_ref[rows, :] * scale, k_ref[rows, :].astype(BF16), k_ref[prows, :].astype(BF16),
                                   v_ref[rows, :].astype(BF16), v_ref[prows, :].astype(BF16), m_s[rows, :], l_s[rows, :], y_ref[rows, :]))
                done = []
                for rows, first, q, kc, kp, vc, vp, m_old, l_old, a_old in loaded:
                    m_new, l_new, a_new = m_old, l_old, a_old
                    for a in range(hps):
                        sel = lane == a
                        qm = jnp.where(sel, q, 0.0).astype(BF16)
                        s_p = _dot(qm, kp, _NT) + b_ref[br, a, :, 0:B]
                        s_c = _dot(qm, kc, _NT) + b_ref[br, a, :, B:2 * B]
                        s_p = jnp.where(first, NEG_INF, s_p)
                        mo, lo = m_old[:, a * E:a * E + 1], l_old[:, a * E:a * E + 1]
                        mn = jnp.maximum(mo, jnp.maximum(jnp.max(s_p, axis=-1, keepdims=True), jnp.max(s_c, axis=-1, keepdims=True)))
                        alpha = jnp.exp(mo - mn)
                        p_p, p_c = jnp.exp(s_p - mn), jnp.exp(s_c - mn)
                        ln = alpha * lo + jnp.sum(p_p, axis=-1, keepdims=True) + jnp.sum(p_c, axis=-1, keepdims=True)
                        pv = _dot(p_p.astype(BF16), vp, _NN) + _dot(p_c.astype(BF16), vc, _NN)
                        m_new = jnp.where(sel, mn, m_new)
                        l_new = jnp.where(sel, ln, l_new)
                        a_new = jnp.where(sel, alpha * a_old + pv, a_new)
                    done.append((rows, m_new, l_new, a_new))
                for rows, m_new, l_new, a_new in done:
                    m_s[rows, :] = m_new
                    l_s[rows, :] = l_new
                    y_ref[rows, :] = a_new
                return carry

            lax.fori_loop(0, nblk // ATT_PAR, step, 0)
        y_ref[...] = y_ref[...] / l_s[...]
        lse_ref[...] = m_s[...] + jnp.log(l_s[...])

    col = lambda off: pl.BlockSpec((L, lb), lambda h: (0, off * (GW // lb) + h))
    own = pl.BlockSpec((L, lb), lambda h: (0, h))
    return pl.pallas_call(
        body, grid=(GW // lb,),
        in_specs=[col(4), col(5), col(6), pl.BlockSpec((len(DILATED_PATTERNS), hps, B, 2 * B), lambda h: (0, h, 0, 0))],
        out_specs=[own, own], out_shape=[jax.ShapeDtypeStruct((L, GW), F32)] * 2,
        scratch_shapes=[pltpu.VMEM((L, lb), F32), pltpu.VMEM((L, lb), F32)],
        compiler_params=_cparams(("parallel",), VMEM_LIMIT), name=name)(proj, proj, proj, bias)


def _datt_bwd(proj, bias, y, lse, dy, *, name):
    L = proj.shape[0]
    GW = proj.shape[1] // 7
    lb, hps, E = _datt_lanes(GW)
    B = ATT_BLOCK
    nblk = L // B
    scale = E ** -0.5

    def body(q_ref, k_ref, v_ref, b_ref, y_ref, lse_ref, dy_ref, dq_ref, dk_ref, dv_ref, db_ref):
        for r in (dq_ref, dk_ref, dv_ref, db_ref):
            r[...] = jnp.zeros(r.shape, F32)
        lane = lax.broadcasted_iota(jnp.int32, (B, lb), 1) // E

        for br, (_, dil) in enumerate(DILATED_PATTERNS):
            def step(i, carry, br=br, dil=dil):
                loaded = []
                for u in range(ATT_PAR):
                    start, pstart, first = _datt_rows(i + u * (nblk // ATT_PAR), br, nblk)
                    rows, prows = pl.ds(start, B, stride=dil), pl.ds(pstart, B, stride=dil)
                    dyr = dy_ref[rows, :]
                    loaded.append((rows, prows, first, q_ref[rows, :] * scale, k_ref[rows, :].astype(BF16), k_ref[prows, :].astype(BF16),
                                   v_ref[rows, :].astype(BF16), v_ref[prows, :].astype(BF16), dyr, lse_ref[rows, :], dyr * y_ref[rows, :]))
                done = []
                dbias = [None] * hps
                for rows, prows, first, q, kc, kp, vc, vp, dyr, lser, dyy in loaded:
                    dq = jnp.zeros((B, lb), F32)
                    dkc, dkp, dvc, dvp = dq, dq, dq, dq
                    for a in range(hps):
                        sel = lane == a
                        qm = jnp.where(sel, q, 0.0).astype(BF16)
                        dym = jnp.where(sel, dyr, 0.0).astype(BF16)
                        dm = jnp.sum(jnp.where(sel, dyy, 0.0), axis=-1, keepdims=True)
                        lse_a = lser[:, a * E:a * E + 1]
                        s_p = _dot(qm, kp, _NT) + b_ref[br, a, :, 0:B]
                        s_c = _dot(qm, kc, _NT) + b_ref[br, a, :, B:2 * B]
                        s_p = jnp.where(first, NEG_INF, s_p)
                        p_p, p_c = jnp.exp(s_p - lse_a), jnp.exp(s_c - lse_a)
                        ds_p = p_p * (_dot(dym, vp, _NT) - dm)
                        ds_c = p_c * (_dot(dym, vc, _NT) - dm)
                        dbias[a] = (ds_p, ds_c) if dbias[a] is None else (dbias[a][0] + ds_p, dbias[a][1] + ds_c)
                        dsb_p, dsb_c = ds_p.astype(BF16), ds_c.astype(BF16)
                        dq = dq + jnp.where(sel, _dot(dsb_p, kp, _NN) + _dot(dsb_c, kc, _NN), 0.0)
                        dkp = dkp + _dot(dsb_p, qm, _TN)
                        dkc = dkc + _dot(dsb_c, qm, _TN)
                        dvp = dvp + _dot(p_p.astype(BF16), dym, _TN)
                        dvc = dvc + _dot(p_c.astype(BF16), dym, _TN)
                    done.append((rows, prows, dq, dkp, dkc, dvp, dvc))
                for a in range(hps):
                    db_ref[br, a, :, 0:B] += dbias[a][0]
                    db_ref[br, a, :, B:2 * B] += dbias[a][1]
                for rows, prows, dq, dkp, dkc, dvp, dvc in done:
                    dq_ref[rows, :] += dq * scale
                    dk_ref[prows, :] += dkp
                    dk_ref[rows, :] += dkc
                    dv_ref[prows, :] += dvp
                    dv_ref[rows, :] += dvc
                return carry

            lax.fori_loop(0, nblk // ATT_PAR, step, 0)

    col = lambda off: pl.BlockSpec((L, lb), lambda h: (0, off * (GW // lb) + h))
    own = pl.BlockSpec((L, lb), lambda h: (0, h))
    bsp = pl.BlockSpec((len(DILATED_PATTERNS), hps, B, 2 * B), lambda h: (0, h, 0, 0))
    return pl.pallas_call(
        body, grid=(GW // lb,),
        in_specs=[col(4), col(5), col(6), bsp, own, own, own], out_specs=[own, own, own, bsp],
        out_shape=[jax.ShapeDtypeStruct((L, GW), F32)] * 3 + [jax.ShapeDtypeStruct((len(DILATED_PATTERNS), ATT_HEADS, B, 2 * B), F32)],
        compiler_params=_cparams(("parallel",), VMEM_LIMIT), name=name)(proj, proj, proj, bias, y, lse, dy)


def _t5_bucket(dist):
    n = np.maximum(dist, 0)
    max_exact = REL_BUCKETS // 2
    large = max_exact + (np.log(np.maximum(n, 1) / max_exact) / np.log(REL_MAX_DIST / max_exact)
                         * (REL_BUCKETS - max_exact)).astype(np.int64)
    large = np.minimum(large, REL_BUCKETS - 1)
    return np.where(n < max_exact, n, large).astype(np.int32)


def _att_tables():
    a = np.arange(ATT_BLOCK)[:, None]
    b = np.arange(2 * ATT_BLOCK)[None, :]
    sub = a + ATT_BLOCK - b
    buckets, valids = [], []
    for window, dil in DILATED_PATTERNS:
        buckets.append(_t5_bucket(sub * dil))
        valids.append((sub >= 0) & (sub <= window // dil))
    return np.stack(buckets), np.stack(valids)


def _xatt_fwd(q, k, v, *, name):
    L, XW = q.shape
    M = k.shape[0]
    tr = _tile(L, 512)
    scale = X_HEAD_DIM ** -0.5

    def body(q_ref, k_ref, v_ref, o_ref):
        s = _dot(q_ref[...], k_ref[...], _NT) * scale
        p = jnp.exp(s - jnp.max(s, axis=-1, keepdims=True))
        den = jnp.sum(p, axis=-1, keepdims=True)
        o_ref[...] = (_dot(p.astype(BF16), v_ref[...], _NN) / den).astype(o_ref.dtype)

    qs = pl.BlockSpec((tr, X_HEAD_DIM), lambda h, i: (i, h))
    ks = pl.BlockSpec((M, X_HEAD_DIM), lambda h, i: (0, h))
    return pl.pallas_call(
        body, grid=(XW // X_HEAD_DIM, L // tr), in_specs=[qs, ks, ks], out_specs=qs,
        out_shape=jax.ShapeDtypeStruct((L, XW), BF16),
        compiler_params=_cparams(("parallel", "parallel")), name=name)(q, k, v)


def _xatt_bwd(q, k, v, do, *, name):
    L, XW = q.shape
    M = k.shape[0]
    tr = _tile(L, 512)
    scale = X_HEAD_DIM ** -0.5

    def body(q_ref, k_ref, v_ref, do_ref, dq_ref, dk_ref, dv_ref):
        @pl.when(pl.program_id(1) == 0)
        def _():
            dk_ref[...] = jnp.zeros_like(dk_ref)
            dv_ref[...] = jnp.zeros_like(dv_ref)

        qv, kv, vv = q_ref[...], k_ref[...], v_ref[...]
        s = _dot(qv, kv, _NT) * scale
        p = jnp.exp(s - jnp.max(s, axis=-1, keepdims=True))
        p = p / jnp.sum(p, axis=-1, keepdims=True)
        dob = do_ref[...].astype(BF16)
        pb = p.astype(BF16)
        dv_ref[...] += _dot(pb, dob, _TN)
        dp = _dot(dob, vv, _NT)
        ds = (p * (dp - jnp.sum(dp * p, axis=-1, keepdims=True)) * scale).astype(BF16)
        dq_ref[...] = _dot(ds, kv, _NN)
        dk_ref[...] += _dot(ds, qv, _TN)

    qs = pl.BlockSpec((tr, X_HEAD_DIM), lambda h, i: (i, h))
    ks = pl.BlockSpec((M, X_HEAD_DIM), lambda h, i: (0, h))
    return pl.pallas_call(
        body, grid=(XW // X_HEAD_DIM, L // tr), in_specs=[qs, ks, ks, qs], out_specs=[qs, ks, ks],
        out_shape=[jax.ShapeDtypeStruct((L, XW), F32), jax.ShapeDtypeStruct((M, XW), F32), jax.ShapeDtypeStruct((M, XW), F32)],
        compiler_params=_cparams(("parallel", "arbitrary")), name=name)(q, k, v, do)


_ANY = pl.BlockSpec(memory_space=pl.ANY)


def _place():
    mx, my, mc = lax.axis_index("x"), lax.axis_index("y"), lax.axis_index("c")
    chips = [(1 - mx, my), (mx, 1 - my), (1 - mx, 1 - my)]
    return mx, my, mc, chips


def _rcopy(src, dst, ssem, rsem, dev):
    return pltpu.make_async_remote_copy(src_ref=src, dst_ref=dst, send_sem=ssem, recv_sem=rsem, device_id=dev,
                                        device_id_type=MESH)


STAGE_BYTES = 2 * 1024 * 1024


def _staged_copy(pairs, buf, isem, osem):
    n = len(pairs)
    ins = [pltpu.make_async_copy(src, buf.at[i % 2], isem.at[i % 2]) for i, (src, _) in enumerate(pairs)]
    outs = [pltpu.make_async_copy(buf.at[i % 2], dst, osem.at[i % 2]) for i, (_, dst) in enumerate(pairs)]
    ins[0].start()
    for i in range(n):
        if i + 1 < n:
            if i >= 1:
                outs[i - 1].wait()
            ins[i + 1].start()
        ins[i].wait()
        outs[i].start()
    for i in range(max(n - 2, 0), n):
        outs[i].wait()


_HBM = pl.BlockSpec(memory_space=pltpu.HBM)
_SEMS = pl.BlockSpec(memory_space=pltpu.SEMAPHORE)
_VM = pl.BlockSpec(memory_space=pltpu.VMEM)
_DATAFLOW = pltpu.SideEffectType.DATAFLOW_SIDE_EFFECTING


def _ag_copies(x_refs, land_refs, ssem, rsem, inbound, layer, which=None):
    mx, my, mc, chips = _place()
    me = 2 * mx + my
    cps = []
    for i, (x_ref, land_ref) in enumerate(zip(x_refs, land_refs)):
        w = i if which is None else which[i]
        R2 = x_ref.shape[1] // 2
        mine = x_ref.at[layer, pl.ds(mc * R2, R2)]
        for j, (px, py) in enumerate(chips):
            dst = land_ref.at[2 * px + py, mc] if inbound else land_ref.at[me, mc]
            cps.append(_rcopy(mine, dst, ssem.at[3 * w + j], rsem.at[3 * w + j], (px, py, mc)))
    return cps


def _ag_start(xs, layer, token, *, name):
    n = len(xs)
    lands = [pltpu.with_memory_space_constraint(lax.empty((N_CHIPS, 2, x.shape[1] // 2, x.shape[2]), x.dtype), pltpu.HBM) for x in xs]

    def body(*refs):
        x_refs, land_refs, tok_ref = refs[:n], refs[n:2 * n], refs[2 * n]
        ssem, rsem, tok_out = refs[2 * n + 1], refs[2 * n + 2], refs[-1]
        for cp in _ag_copies(x_refs, land_refs, ssem, rsem, False, layer):
            cp.start()
        tok_out[...] = tok_ref[...]

    outs = pl.pallas_call(
        body, name=name,
        out_shape=(pltpu.SemaphoreType.DMA((3 * n,)), pltpu.SemaphoreType.DMA((3 * n,)),
                   *[pltpu.HBM(t.shape, t.dtype) for t in lands], jax.ShapeDtypeStruct(token.shape, token.dtype)),
        in_specs=[_HBM] * (2 * n) + [_VM], out_specs=(_SEMS, _SEMS, *[_HBM] * n, _VM),
        input_output_aliases={n + i: 2 + i for i in range(n)},
        compiler_params=pltpu.CompilerParams(has_side_effects=_DATAFLOW),
    )(*xs, *lands, token)
    return outs[0], outs[1], list(outs[2:2 + n]), outs[-1]


def _ag_wait(ssem, rsem, xs, lands, after, which, layer, *, name):
    n = len(xs)

    def body(*refs):
        x_refs, land_refs, ssem_ref, rsem_ref = refs[:n], refs[n:2 * n], refs[2 * n], refs[2 * n + 1]
        for cp in _ag_copies(x_refs, land_refs, ssem_ref, rsem_ref, True, layer, which):
            cp.wait_send()
            cp.wait_recv()

    after = list(after) if isinstance(after, (list, tuple)) else [after]
    outs = pl.pallas_call(
        body, name=name,
        out_shape=tuple(pltpu.HBM(t.shape, t.dtype) for t in lands),
        in_specs=[_HBM] * (2 * n) + [_SEMS, _SEMS] + [_ANY] * len(after), out_specs=tuple([_HBM] * n),
        input_output_aliases={n + i: i for i in range(n)},
        compiler_params=pltpu.CompilerParams(has_side_effects=_DATAFLOW),
    )(*xs, *lands, ssem, rsem, *after)
    return list(outs)


def _ag_finish(lands, xs, layer, *, name):
    n = len(xs)

    def body(*refs):
        x_refs, o_refs, (ssem, rsem) = refs[n:2 * n], refs[2 * n:3 * n], refs[3 * n:]
        mx, my, mc, chips = _place()
        me = 2 * mx + my
        sib = (mx, my, 1 - mc)
        passed = []
        for w in range(n):
            for j, (px, py) in enumerate(chips):
                got = o_refs[w].at[2 * px + py, mc]
                passed.append(_rcopy(got, got, ssem.at[3 * w + j], rsem.at[3 * w + j], sib))
        for cp in passed:
            cp.start()
        for w in range(n):
            _, R, C = x_refs[w].shape
            R2 = R // 2
            ch = _rows_tile(R2, max(8, STAGE_BYTES // (C * x_refs[w].dtype.itemsize)))
            pairs = [(x_refs[w].at[layer, pl.ds(h * R2 + r, ch)], o_refs[w].at[me, h, pl.ds(r, ch)])
                     for h in range(2) for r in range(0, R2, ch)]
            pl.run_scoped(functools.partial(_staged_copy, pairs), pltpu.VMEM((2, ch, C), x_refs[w].dtype),
                          pltpu.SemaphoreType.DMA((2,)), pltpu.SemaphoreType.DMA((2,)))
        for w in range(n):
            for j, (px, py) in enumerate(chips):
                theirs = o_refs[w].at[2 * px + py, 1 - mc]
                _rcopy(theirs, theirs, ssem.at[3 * w + j], rsem.at[3 * w + j], sib).wait_recv()
        for cp in passed:
            cp.wait_send()

    return pl.pallas_call(
        body, in_specs=[_ANY] * (2 * n), out_specs=[_ANY] * n,
        out_shape=[jax.ShapeDtypeStruct(t.shape, t.dtype) for t in lands],
        input_output_aliases={i: i for i in range(n)},
        scratch_shapes=[pltpu.SemaphoreType.DMA((3 * n,)), pltpu.SemaphoreType.DMA((3 * n,))],
        name=name)(*lands, *xs)


def _ag4(x, *, name):
    def body(x_ref, o_ref, ssem, rsem, lsem):
        mx, my, mc, chips = _place()
        me = 2 * mx + my
        loc = pltpu.make_async_copy(x_ref, o_ref.at[me], lsem)
        loc.start()
        sends = [_rcopy(x_ref, o_ref.at[me], ssem.at[j], rsem.at[j], (px, py, mc)) for j, (px, py) in enumerate(chips)]
        for cp in sends:
            cp.start()
        for j, (px, py) in enumerate(chips):
            _rcopy(x_ref, o_ref.at[2 * px + py], ssem.at[j], rsem.at[j], (px, py, mc)).wait_recv()
        for cp in sends:
            cp.wait_send()
        loc.wait()

    return pl.pallas_call(
        body, in_specs=[_ANY], out_specs=_ANY, out_shape=jax.ShapeDtypeStruct((N_CHIPS,) + x.shape, x.dtype),
        scratch_shapes=[pltpu.SemaphoreType.DMA((3,)), pltpu.SemaphoreType.DMA((3,)), pltpu.SemaphoreType.DMA(())],
        name=name)(x)


def _rs_sib(gs, after=(), *, name):
    n, na = len(gs), len(after)

    def body(*refs):
        g_refs, t_refs, (ssem, rsem) = refs[:n], refs[n + na:2 * n + na], refs[2 * n + na:]
        mx, my, mc, _ = _place()
        sib = (mx, my, 1 - mc)
        sends = []
        for w in range(n):
            R2 = g_refs[w].shape[1] // 2
            for k in range(N_CHIPS):
                sends.append(_rcopy(g_refs[w].at[k, pl.ds((1 - mc) * R2, R2)], t_refs[w].at[k], ssem.at[N_CHIPS * w + k],
                                    rsem.at[N_CHIPS * w + k], sib))
        for cp in sends:
            cp.start()
        for w in range(n):
            for k in range(N_CHIPS):
                t = t_refs[w].at[k]
                _rcopy(t, t, ssem.at[N_CHIPS * w + k], rsem.at[N_CHIPS * w + k], sib).wait_recv()
        for cp in sends:
            cp.wait_send()

    return pl.pallas_call(
        body, in_specs=[_ANY] * (n + na), out_specs=[_ANY] * n,
        out_shape=[jax.ShapeDtypeStruct((N_CHIPS, g.shape[1] // 2, g.shape[2]), g.dtype) for g in gs],
        scratch_shapes=[pltpu.SemaphoreType.DMA((N_CHIPS * n,)), pltpu.SemaphoreType.DMA((N_CHIPS * n,))],
        name=name)(*gs, *after)


def _a2a_copies(h_refs, got_refs, ssem, rsem, inbound):
    mx, my, mc, chips = _place()
    me = 2 * mx + my
    cps = []
    for w, (h_ref, got_ref) in enumerate(zip(h_refs, got_refs)):
        for j, (px, py) in enumerate(chips):
            dst = got_ref.at[2 * px + py] if inbound else got_ref.at[me]
            cps.append(_rcopy(h_ref.at[2 * px + py], dst, ssem.at[3 * w + j], rsem.at[3 * w + j], (px, py, mc)))
    return cps


def _a2a_start(hs, *, name):
    n = len(hs)
    gots = [pltpu.with_memory_space_constraint(lax.empty(h.shape, h.dtype), pltpu.HBM) for h in hs]
    hs = [pltpu.with_memory_space_constraint(h, pltpu.HBM) for h in hs]

    def body(*refs):
        h_refs, got_refs = refs[:n], refs[n:2 * n]
        ssem, rsem, tok_out = refs[2 * n], refs[2 * n + 1], refs[-1]
        for cp in _a2a_copies(h_refs, got_refs, ssem, rsem, False):
            cp.start()
        tok_out[...] = jnp.zeros_like(tok_out)

    outs = pl.pallas_call(
        body, name=name,
        out_shape=(pltpu.SemaphoreType.DMA((3 * n,)), pltpu.SemaphoreType.DMA((3 * n,)),
                   *[pltpu.HBM(h.shape, h.dtype) for h in hs], *[pltpu.HBM(h.shape, h.dtype) for h in hs],
                   jax.ShapeDtypeStruct((8, 128), F32)),
        in_specs=[_HBM] * (2 * n), out_specs=(_SEMS, _SEMS, *[_HBM] * (2 * n), _VM),
        input_output_aliases={i: 2 + i for i in range(2 * n)},
        compiler_params=pltpu.CompilerParams(has_side_effects=_DATAFLOW),
    )(*hs, *gots)
    return outs[0], outs[1], list(outs[2:2 + n]), list(outs[2 + n:2 + 2 * n]), outs[-1]


def _a2a_wait(ssem, rsem, hs, gots, after, *, name):
    n = len(hs)

    def body(*refs):
        h_refs, got_refs, ssem_ref, rsem_ref = refs[:n], refs[n:2 * n], refs[2 * n], refs[2 * n + 1]
        for cp in _a2a_copies(h_refs, got_refs, ssem_ref, rsem_ref, True):
            cp.wait_send()
            cp.wait_recv()

    after = list(after) if isinstance(after, (list, tuple)) else [after]
    outs = pl.pallas_call(
        body, name=name,
        out_shape=tuple(pltpu.HBM(h.shape, h.dtype) for h in hs + gots),
        in_specs=[_HBM] * (2 * n) + [_SEMS, _SEMS] + [_ANY] * len(after), out_specs=tuple([_HBM] * (2 * n)),
        input_output_aliases={i: i for i in range(2 * n)},
        compiler_params=pltpu.CompilerParams(has_side_effects=_DATAFLOW),
    )(*hs, *gots, ssem, rsem, *after)
    return list(outs[:n]), list(outs[n:])


def _sib_fill(bufs, after=(), *, name):
    n, na = len(bufs), len(after)

    def body(*refs):
        o_refs, (ssem, rsem) = refs[n + na:2 * n + na], refs[2 * n + na:]
        mx, my, mc, _ = _place()
        sib = (mx, my, 1 - mc)
        sends = [_rcopy(o_refs[w].at[mc], o_refs[w].at[mc], ssem.at[w], rsem.at[w], sib) for w in range(n)]
        for cp in sends:
            cp.start()
        for w in range(n):
            t = o_refs[w].at[1 - mc]
            _rcopy(t, t, ssem.at[w], rsem.at[w], sib).wait_recv()
        for cp in sends:
            cp.wait_send()

    return pl.pallas_call(
        body, in_specs=[_ANY] * (n + na), out_specs=[_ANY] * n, out_shape=[jax.ShapeDtypeStruct(b.shape, b.dtype) for b in bufs],
        input_output_aliases={i: i for i in range(n)},
        scratch_shapes=[pltpu.SemaphoreType.DMA((n,)), pltpu.SemaphoreType.DMA((n,))], name=name)(*bufs, *after)


def _rows_tile(n, pref=512):
    t = min(n, pref)
    while n % t or (t % 8 and t != n):
        t -= 1
    return t


def _rs_add(g, theirs, c_arr, payload, *, name):
    _, R, C = g.shape
    R2 = R // 2
    tr = _rows_tile(R2, 256)
    nb = R2 // tr

    def body(c_ref, g_ref, t_ref, o_ref):
        o_ref[...] = (g_ref[...].astype(F32) + t_ref[...].astype(F32)).astype(o_ref.dtype)

    blk = pl.BlockSpec((None, tr, C), lambda k, i, c: (k, i, 0))
    return pl.pallas_call(
        body,
        grid_spec=pltpu.PrefetchScalarGridSpec(
            num_scalar_prefetch=1, grid=(N_CHIPS, nb),
            in_specs=[pl.BlockSpec((None, tr, C), lambda k, i, c: (k, c[0] * nb + i, 0)), blk], out_specs=blk),
        out_shape=jax.ShapeDtypeStruct((N_CHIPS, R2, C), payload),
        compiler_params=_cparams(("arbitrary", "arbitrary")), name=name)(c_arr, g, theirs)


def _rs_sum(chip_sum, got, mc_arr, *, name):
    _, R2, C = chip_sum.shape
    tr = _rows_tile(R2, 256)

    def body(s_ref, own_ref, g0, g1, g2, g3, o_ref):
        me = s_ref[0]
        acc = jnp.zeros((tr, C), F32)
        for k, g_ref in enumerate((g0, g1, g2, g3)):
            acc = acc + jnp.where(me == k, own_ref[...], g_ref[...]).astype(F32)
        o_ref[...] = acc

    def got_spec(k):
        return pl.BlockSpec((None, tr, C), lambda i, s: (jnp.where(s[0] == k, (k + 1) % N_CHIPS, k), i, 0))

    return pl.pallas_call(
        body,
        grid_spec=pltpu.PrefetchScalarGridSpec(
            num_scalar_prefetch=1, grid=(R2 // tr,),
            in_specs=[pl.BlockSpec((None, tr, C), lambda i, s: (s[0], i, 0))] + [got_spec(k) for k in range(N_CHIPS)],
            out_specs=pl.BlockSpec((None, tr, C), lambda i, s: (s[1], i, 0))),
        out_shape=jax.ShapeDtypeStruct((2, R2, C), F32),
        compiler_params=_cparams(("arbitrary",)), name=name)(mc_arr, chip_sum, got, got, got, got)


def _adamw(w, m, v, gs, *, name, first=0, prev=None):
    _, _, R2, C = w.shape
    nl = len(gs)
    tr = _rows_tile(R2, 256)
    c1 = 1.0 / (1.0 - ADAM_B1 ** ADAM_STEP)
    c2 = 1.0 / (1.0 - ADAM_B2 ** ADAM_STEP)

    def body(w_ref, m_ref, v_ref, *refs):
        g_refs, (go_ref, d_ref, mo_ref, vo_ref) = refs[:nl], refs[-4:]
        l = pl.program_id(0)
        for ll in range(nl):
            @pl.when(l == ll)
            def _(ll=ll):
                gv = g_refs[ll][...]
                mn = ADAM_B1 * m_ref[...] + (1.0 - ADAM_B1) * gv
                vn = ADAM_B2 * v_ref[...] + (1.0 - ADAM_B2) * (gv * gv)
                go_ref[...] = gv
                mo_ref[...] = mn
                vo_ref[...] = vn
                d_ref[...] = -ADAM_LR * ((mn * c1) / (jnp.sqrt(vn * c2) + ADAM_EPS) + ADAM_WD * w_ref[...])

    def g_spec(ll):
        return pl.BlockSpec((None, tr, C), lambda l, h, i: (jnp.where(l == ll, h, 0), jnp.where(l == ll, i, 0), 0))

    ws = pl.BlockSpec((None, None, tr, C), lambda l, h, i: (l + first, h, i, 0))
    shp = jax.ShapeDtypeStruct(w.shape, F32)
    prev = list(prev) if prev is not None else []
    return pl.pallas_call(
        body, grid=(nl, 2, R2 // tr), in_specs=[ws, ws, ws] + [g_spec(ll) for ll in range(nl)] + [_ANY] * len(prev),
        out_specs=[ws] * 4, out_shape=[shp] * 4,
        input_output_aliases={3 + nl + k: k for k in range(len(prev))},
        compiler_params=_cparams(("arbitrary", "arbitrary", "arbitrary")), name=name)(w, m, v, *gs, *prev)


class _GradReducer:
    def __init__(self, tag, payload):
        self.tag, self.payload = tag, payload
        self.me = (2 * lax.axis_index("x") + lax.axis_index("y")).astype(jnp.int32)
        self.c = lax.axis_index("c").astype(jnp.int32)

    def start(self, names, gs, after=()):
        theirs = _rs_sib(gs, after, name=f"rs_sib_{self.tag}")
        hs = [_rs_add(g, t, self.c.reshape(1), self.payload, name=f"rs_add_{n}") for n, g, t in zip(names, gs, theirs)]
        self.names = names
        self.ssem, self.rsem, self.hs, self.gots, token = _a2a_start(hs, name=f"rs_a2a_start_{self.tag}")
        return token

    def finish(self, after):
        hs, gots = _a2a_wait(self.ssem, self.rsem, self.hs, self.gots, after, name=f"rs_a2a_wait_{self.tag}")
        mc = jnp.stack([self.me, self.c])
        return {n: _rs_sum(h, g, mc, name=f"rs_sum_{n}") for n, h, g in zip(self.names, hs, gots)}


WEIGHTS = ["rel_bias", "mem_norm_g", "norm_mix_g", "w_in", "s5_lam_re", "s5_lam_im", "s5_log_dt", "s5_b_re", "s5_b_im",
           "s5_c_re", "s5_c_im", "s5_d", "s5_w_glu", "pool_w", "pool_scale", "conv_w_dw", "conv_b_dw", "conv_ln_g",
           "conv_ln_b", "conv_w_pw", "grp_norm_g", "w_out", "norm_x_g", "w_xq", "w_xk", "w_xv", "w_xo", "norm_mlp_g",
           "w_up", "w_down", "norm_final_g"]
SHARDED = {"w_in": "col", "s5_w_glu": "row", "conv_w_dw": "col", "conv_w_pw": "row", "w_out": "row", "w_xq": "row",
           "w_xk": "row", "w_xv": "row", "w_xo": "col", "w_up": "col", "w_down": "row"}
AG_FIRST = ("w_in", "s5_w_glu", "conv_w_dw", "conv_w_pw")
SMALL = [n for n in WEIGHTS if n not in SHARDED]
SMALL_ALIGN = N_CHIPS * 2 * 256 * 128


def _mat(w, kind):
    return w.reshape(w.shape[0] * w.shape[1], w.shape[2]) if kind == "row" else w


def _att_bias(rel_bias):
    bucket, valid = _att_tables()
    onehot = (jnp.asarray(bucket.reshape(-1))[:, None] == jnp.arange(REL_BUCKETS)[None, :]).astype(F32)
    b = jnp.dot(onehot, rel_bias, precision=lax.Precision.HIGHEST).reshape(bucket.shape + (rel_bias.shape[1],))
    return jnp.where(jnp.asarray(valid)[:, None], jnp.transpose(b, (0, 3, 1, 2)), NEG_INF)


def _rel_bias_grad(dbias):
    bucket, _ = _att_tables()
    nb, H = dbias.shape[0], dbias.shape[1]
    onehot = (jnp.asarray(bucket.reshape(-1))[:, None] == jnp.arange(REL_BUCKETS)[None, :]).astype(BF16)
    flat = jnp.transpose(dbias.reshape(nb, H, -1), (1, 0, 2)).reshape(H, -1)
    return _mm(flat, onehot, "nn", tk=16384, name="rel_bias_grad").T


def _layer_fwd(h, mem_n, bias, sp, bw, l):
    L, D = h.shape
    GW = D // N_MIXERS
    G = GW // S5_CH
    E = GW // ATT_HEADS
    sv = {"h": h}
    xn = _rms_fwd(h, sp["norm_mix_g"][l][None], name="norm_mix")
    proj = _mm(xn, bw["w_in"], "nn", name="proj_in")
    sv.update(xn=xn, proj=proj)
    disc, disc_vjp = jax.vjp(_s5_disc, sp["s5_lam_re"][l], sp["s5_lam_im"][l], sp["s5_log_dt"][l], sp["s5_b_re"][l], sp["s5_b_im"][l])
    ab_r, ab_i, bb_r, bb_i = disc
    s5 = dict(
        bdr=_bd(jnp.transpose(bb_r, (0, 2, 1))).astype(BF16), bdi=_bd(jnp.transpose(bb_i, (0, 2, 1))).astype(BF16),
        cdr=_bd(jnp.transpose(sp["s5_c_re"][l], (0, 2, 1))).astype(BF16), cdi=_bd(jnp.transpose(sp["s5_c_im"][l], (0, 2, 1))).astype(BF16),
        d=sp["s5_d"][l][None], wglu=bw["s5_w_glu"], ab=(ab_r.reshape(-1), ab_i.reshape(-1)), vjp=disc_vjp)
    pw, tr, ti = _cpow_tables(*s5["ab"])
    y_a, xr, xi = _s5_fwd(proj, s5["bdr"], s5["bdi"], pw, tr, ti, s5["cdr"], s5["cdi"], s5["d"], s5["wglu"], name="s5_fwd")
    sv.update(s5=s5, xr=xr, xi=xi, y_a=y_a)
    pool_s = sp["pool_scale"][l].reshape(len(POOL_WINDOWS), 1, -1)
    y_b = _pool_fwd(proj, sp["pool_w"][l], pool_s, name="pool_fwd")
    sv.update(y_b=y_b, pool_s=pool_s)
    hc = _conv1_fwd(proj, bw["conv_w_dw"], sp["conv_b_dw"][l][None], name="conv_dw_fwd")
    y_c = _conv2_fwd(hc, sp["conv_ln_g"][l][None], sp["conv_ln_b"][l][None], bw["conv_w_pw"], name="conv_pw_fwd")
    sv.update(hc=hc, y_c=y_c)
    y_d, lse_d = _datt_fwd(proj, bias, name="att_fwd")
    sv.update(y_d=y_d, lse_d=lse_d)
    yn = _grp_fwd([y_a, y_b, y_c, y_d], sp["grp_norm_g"][l][None], name="grp_fwd")
    bw.rest(yn)
    h1 = _mm(yn, bw["w_out"], "nn", res=h, name="proj_out")
    sv.update(yn=yn, h1=h1)
    hx = _rms_fwd(h1, sp["norm_x_g"][l][None], name="norm_x")
    q_x = _mm(hx, bw["w_xq"], "nn", out_dtype=BF16, name="xq")
    k_x = _mm(mem_n, bw["w_xk"], "nn", out_dtype=BF16, name="xk")
    v_x = _mm(mem_n, bw["w_xv"], "nn", out_dtype=BF16, name="xv")
    o_x = _xatt_fwd(q_x, k_x, v_x, name="xatt_fwd")
    h2 = _mm(o_x, bw["w_xo"], "nn", res=h1, name="xo")
    sv.update(hx=hx, q_x=q_x, k_x=k_x, v_x=v_x, o_x=o_x, h2=h2)
    hm = _rms_fwd(h2, sp["norm_mlp_g"][l][None], name="norm_mlp")
    up, act = _mm(hm, bw["w_up"], "nn", epi="relu2", out_dtype=BF16, name="mlp_up")
    h3 = _mm(act, bw["w_down"], "nn", res=h2, name="mlp_down")
    sv.update(hm=hm, up=up, act=act)
    return h3, sv


def _stack_rows(g):
    return g.reshape(N_CHIPS, g.shape[0] // N_CHIPS, g.shape[1])


def _layer_bwd(dh3, d_memn, mem_n, bias, sp, bw, sv, l):
    L, D = dh3.shape
    GW = D // N_MIXERS
    G = GW // S5_CH
    E = GW // ATT_HEADS
    gs, gb = {}, {}
    d_up = _mm(dh3, bw["w_down"], "nt", epi="relu2_bwd", aux=sv["up"], out_dtype=BF16, name="mlp_down_dx")
    gb["w_down"] = _stack_rows(_mm(sv["act"], dh3, "tn", out_dtype=BF16, name="mlp_down_dw"))
    gb["w_up"] = _mm(sv["hm"], d_up, "tn", out_dtype=BF16, out_stack=N_CHIPS, name="mlp_up_dw")
    d_hm = _mm(d_up, bw["w_up"], "nt", name="mlp_up_dx")
    dh2, gs["norm_mlp_g"] = _rms_bwd(sv["h2"], sp["norm_mlp_g"][l][None], d_hm, dh3, name="norm_mlp_bwd")
    d_ox = _mm(dh2, bw["w_xo"], "nt", name="xo_dx")
    gb["w_xo"] = _mm(sv["o_x"], dh2, "tn", out_dtype=BF16, out_stack=N_CHIPS, name="xo_dw")
    dq_x, dk_x, dv_x = _xatt_bwd(sv["q_x"], sv["k_x"], sv["v_x"], d_ox, name="xatt_bwd")
    gb["w_xq"] = _stack_rows(_mm(sv["hx"], dq_x, "tn", out_dtype=BF16, name="xq_dw"))
    gb["w_xk"] = _stack_rows(_mm(mem_n, dk_x, "tn", out_dtype=BF16, name="xk_dw"))
    gb["w_xv"] = _stack_rows(_mm(mem_n, dv_x, "tn", out_dtype=BF16, name="xv_dw"))
    d_memn = _mm(dk_x, bw["w_xk"], "nt", res=d_memn, name="xk_dx")
    d_memn = _mm(dv_x, bw["w_xv"], "nt", res=d_memn, name="xv_dx")
    d_hx = _mm(dq_x, bw["w_xq"], "nt", name="xq_dx")
    dh1, gs["norm_x_g"] = _rms_bwd(sv["h1"], sp["norm_x_g"][l][None], d_hx, dh2, name="norm_x_bwd")
    d_yn = _mm(dh1, bw["w_out"], "nt", name="proj_out_dx")
    gb["w_out"] = _stack_rows(_mm(sv["yn"], dh1, "tn", out_dtype=BF16, name="proj_out_dw"))
    ys = [sv["y_a"], sv["y_b"], sv["y_c"], sv["y_d"]]
    dya, dyb, dyc, dyd, gs["grp_norm_g"] = _grp_bwd(ys, sp["grp_norm_g"][l][None], d_yn, name="grp_bwd")
    dq, dk, dv, dbias = _datt_bwd(sv["proj"], bias, sv["y_d"], sv["lse_d"], dyd, name="att_bwd")
    dhc, gs["conv_ln_g"], gs["conv_ln_b"], g_pw = _conv2_bwd(sv["hc"], sp["conv_ln_g"][l][None], sp["conv_ln_b"][l][None],
                                                               bw["conv_w_pw"], dyc, name="conv_pw_bwd")
    gb["conv_w_pw"] = _stack_rows(g_pw)
    dval, dgate, g_dw, gs["conv_b_dw"] = _conv1_bwd(sv["proj"], dhc, bw["conv_w_dw"], name="conv_dw_bwd")
    gb["conv_w_dw"] = jnp.transpose(g_dw.reshape(CONV_PAD, N_CHIPS, GW // N_CHIPS), (1, 0, 2))
    du_b, gs["pool_w"], g_ps = _pool_bwd(sv["proj"], dyb, sp["pool_w"][l], sv["pool_s"], name="pool_bwd")
    gs["pool_scale"] = g_ps.reshape(-1)
    s5 = sv["s5"]
    conj = (s5["ab"][0], -s5["ab"][1])
    pw, tr, ti = _cpow_tables(*conj)
    du_a, g_glu, g_d, dcdr, dcdi, dbdr, dbdi, dar, dai = _s5_bwd(
        sv["proj"], sv["xr"], sv["xi"], dya, s5["bdr"], s5["bdi"], pw, tr[::-1], ti[::-1], s5["cdr"], s5["cdi"], s5["d"], s5["wglu"],
        name="s5_bwd")
    gb["s5_w_glu"] = _stack_rows(g_glu)
    gs["s5_d"] = g_d.reshape(-1)
    gs["s5_c_re"] = jnp.transpose(_bd_extract(dcdr, G), (0, 2, 1))
    gs["s5_c_im"] = jnp.transpose(_bd_extract(dcdi, G), (0, 2, 1))
    d_bb_r = jnp.transpose(_bd_extract(dbdr, G), (0, 2, 1))
    d_bb_i = jnp.transpose(_bd_extract(dbdi, G), (0, 2, 1))
    d_ab_r = jnp.sum(dar, axis=0).reshape(G, S5_STATE)
    d_ab_i = jnp.sum(dai, axis=0).reshape(G, S5_STATE)
    gs["s5_lam_re"], gs["s5_lam_im"], gs["s5_log_dt"], gs["s5_b_re"], gs["s5_b_im"] = s5["vjp"]((d_ab_r, d_ab_i, d_bb_r, d_bb_i))
    d_proj = jnp.concatenate([du_a, du_b, dval, dgate, dq, dk, dv], axis=1)
    gb["w_in"] = _mm(sv["xn"], d_proj, "tn", out_dtype=BF16, out_stack=N_CHIPS, name="proj_in_dw")
    d_xn = _mm(d_proj, bw["w_in"], "nt", name="proj_in_dx")
    dh0, gs["norm_mix_g"] = _rms_bwd(sv["h"], sp["norm_mix_g"][l][None], d_xn, dh1, name="norm_mix_bwd")
    gs = {n: g.reshape(sp[n].shape[1:]) for n, g in gs.items()}
    return dh0, d_memn, dbias, gs, gb


def _device_step(x, mem, target, sp, get_big, on_grads):
    nl = sp["norm_mix_g"].shape[0]
    mem_n = _rms_fwd(mem, sp["mem_norm_g"][None], name="norm_mem")
    bias = _att_bias(sp["rel_bias"])
    h, saved, big = x, [], {n: [] for n in SHARDED}
    for l in range(nl):
        bw = get_big(l, h)
        h, sv = _layer_fwd(h, mem_n, bias, sp, bw, l)
        saved.append(sv)
        for n in SHARDED:
            big[n].append(bw[n])
    loss, dh, g_final = _loss_head(h, sp["norm_final_g"][None], target, name="loss_head")
    d_memn = jnp.zeros(mem.shape, F32)
    dbias = jnp.zeros(bias.shape, F32)
    sg = {n: [None] * nl for n in SMALL}
    for l in reversed(range(nl)):
        dh, d_memn, dbias_l, gs, gb = _layer_bwd(dh, d_memn, mem_n, bias, sp, {n: big[n][l] for n in SHARDED}, saved[l], l)
        dbias = dbias + dbias_l
        for n, g in gs.items():
            sg[n][l] = g
        dh = on_grads(l, gb, dh)
    small = {n: jnp.stack(g) for n, g in sg.items() if g[0] is not None}
    small["norm_final_g"] = g_final.reshape(-1)
    _, g_mem = _rms_bwd(mem, sp["mem_norm_g"][None], d_memn, None, name="norm_mem_bwd")
    small["mem_norm_g"] = g_mem.reshape(-1)
    small["rel_bias"] = _rel_bias_grad(dbias)
    return loss, dh, small


def _gather_big(shards, prep):
    nl = shards["w_in"].shape[0]
    token = jnp.zeros((8, 128), F32)
    stacked = [pltpu.with_memory_space_constraint(
        jnp.pad(shards[n], ((0, 0), (0, CONV_PAD - CONV_WIDTH), (0, 0))) if n == "conv_w_dw" else shards[n].astype(BF16), pltpu.HBM)
        for n in SHARDED]
    pending = []
    for l in range(nl):
        ssem, rsem, lands, token = _ag_start(stacked, l, token, name=f"ag_start_l{l}")
        pending.append((ssem, rsem, lands))

    names = list(SHARDED)

    class LayerWeights(dict):
        def __init__(self, l, after):
            super().__init__()
            self.l, self.state = l, pending[l]
            first = [i for i, n in enumerate(names) if n in AG_FIRST] if l == 0 else list(range(len(names)))
            self.left = [i for i in range(len(names)) if i not in first]
            self._complete(first, [token] + list(prep) if l == 0 else [after], "a")

        def _complete(self, which, after, tag):
            ssem, rsem, lands = self.state
            xs = [stacked[i] for i in which]
            lands = _ag_wait(ssem, rsem, xs, [lands[i] for i in which], after, which, self.l, name=f"ag_wait_l{self.l}{tag}")
            fulls = _ag_finish(lands, xs, self.l, name=f"ag_finish_l{self.l}{tag}")
            for i, x, full in zip(which, xs, fulls):
                n = names[i]
                full = full.reshape(N_CHIPS, x.shape[1], x.shape[2])
                self[n] = jnp.transpose(full, (1, 0, 2)).reshape(CONV_PAD, -1) if n == "conv_w_dw" else _mat(full, SHARDED[n])

        def rest(self, after):
            if self.left:
                self._complete(self.left, [after], "b")
                self.left = []

    return LayerWeights


def _pack_small(vals, extra=None):
    flat = [vals[n].reshape(-1).astype(F32) for n in SMALL]
    flat.append(jnp.zeros((1,), F32) if extra is None else extra.reshape(1))
    v = jnp.concatenate(flat)
    n_pad = -(-v.shape[0] // SMALL_ALIGN) * SMALL_ALIGN
    return jnp.pad(v, (0, n_pad - v.shape[0]))


def _unpack_small(flat, like):
    def cut(pos, size):
        r0 = pos // 128
        piece = flat[r0:-(-(pos + size) // 128)].reshape(-1)
        return piece[pos - r0 * 128:pos - r0 * 128 + size]

    out, pos = {}, 0
    for n in SMALL:
        size = math.prod(like[n].shape)
        out[n] = cut(pos, size).reshape(like[n].shape)
        pos += size
    return out, cut(pos, 1)[0]


def kernel(x, mem, rel_bias, mem_norm_g, norm_mix_g, w_in, s5_lam_re, s5_lam_im, s5_log_dt, s5_b_re, s5_b_im, s5_c_re, s5_c_im, s5_d, s5_w_glu, pool_w, pool_scale, conv_w_dw, conv_b_dw, conv_ln_g, conv_ln_b, conv_w_pw, grp_norm_g, w_out, norm_x_g, w_xq, w_xk, w_xv, w_xo, norm_mlp_g, w_up, w_down, norm_final_g, loss_target, m_rel_bias, m_mem_norm_g, m_norm_mix_g, m_w_in, m_s5_lam_re, m_s5_lam_im, m_s5_log_dt, m_s5_b_re, m_s5_b_im, m_s5_c_re, m_s5_c_im, m_s5_d, m_s5_w_glu, m_pool_w, m_pool_scale, m_conv_w_dw, m_conv_b_dw, m_conv_ln_g, m_conv_ln_b, m_conv_w_pw, m_grp_norm_g, m_w_out, m_norm_x_g, m_w_xq, m_w_xk, m_w_xv, m_w_xo, m_norm_mlp_g, m_w_up, m_w_down, m_norm_final_g, v_rel_bias, v_mem_norm_g, v_norm_mix_g, v_w_in, v_s5_lam_re, v_s5_lam_im, v_s5_log_dt, v_s5_b_re, v_s5_b_im, v_s5_c_re, v_s5_c_im, v_s5_d, v_s5_w_glu, v_pool_w, v_pool_scale, v_conv_w_dw, v_conv_b_dw, v_conv_ln_g, v_conv_ln_b, v_conv_w_pw, v_grp_norm_g, v_w_out, v_norm_x_g, v_w_xq, v_w_xk, v_w_xv, v_w_xo, v_norm_mlp_g, v_w_up, v_w_down, v_norm_final_g):
    env = dict(locals())
    w = {n: env[n] for n in WEIGHTS}
    m = {n: env["m_" + n] for n in WEIGHTS}
    v = {n: env["v_" + n] for n in WEIGHTS}
    sp = {n: w[n] for n in SMALL}
    small_wmv = [_pack_small(t).reshape(1, 2, -1, 128) for t in (w, m, v)]
    get_big = _gather_big({n: w[n] for n in SHARDED}, small_wmv)
    nl = w["w_in"].shape[0]
    names = list(SHARDED)
    reducers, reduced, tokens = {}, {}, {}

    def on_grads(l, gb, dh):
        reducers[l] = _GradReducer(f"l{l}", BF16)
        token = tokens[l] = reducers[l].start(names, [gb[n] for n in names])
        if l + 1 in reducers:
            reduced[l + 1] = reducers[l + 1].finish(dh)
        return dh + token[0, 0]

    loss, grad_x, g_small = _device_step(x[0], mem[0], loss_target[0], sp, get_big, on_grads)

    def shard_views(n):
        pad = ((0, 0), (0, CONV_PAD - CONV_WIDTH), (0, 0)) if n == "conv_w_dw" else None
        wmv = [jnp.pad(t[n], pad) if pad else t[n] for t in (w, m, v)]
        _, R, C = wmv[0].shape
        return [t.reshape(nl, 2, R // 2, C) for t in wmv]

    grad, delta, new_m, new_v = {}, {}, {}, {}
    busy, upper = [grad_x], {}
    if nl > 1:
        filled = _sib_fill([reduced[l][n] for n in names for l in range(1, nl)], [tokens[0]], name="rs_fill_upper")
        for i, n in enumerate(names):
            upper[n] = _adamw(*shard_views(n), filled[i * (nl - 1):(i + 1) * (nl - 1)], first=1, name=f"adamw_upper_{n}")
            busy.append(upper[n][0])
    reduced[0] = reducers[0].finish(busy)
    packed = _pack_small(g_small, loss).reshape(N_CHIPS, -1, 128)
    small = _GradReducer("small", F32)
    token = small.start(["small"], [packed], after=[reduced[0][names[0]]])
    quarter = _sib_fill([small.finish(token)["small"]], name="rs_fill_small")[0]
    total = _ag4(quarter.reshape(-1, 128), name="ag_small").reshape(2, -1, 128)
    outs = _adamw(*small_wmv, [total], name="adamw_small")
    filled = _sib_fill([reduced[0][n] for n in names], name="rs_fill_first")
    for i, n in enumerate(names):
        res = _adamw(*shard_views(n), [filled[i]], prev=upper.get(n), name=f"adamw_first_{n}")
        R = w[n].shape[1]
        grad[n], delta[n], new_m[n], new_v[n] = [o.reshape(nl, -1, o.shape[-1])[:, :R] for o in res]
    loss_sum = None
    for o, dst in zip(outs, (grad, delta, new_m, new_v)):
        vals, last = _unpack_small(o.reshape(-1, 128), sp)
        dst.update(vals)
        loss_sum = last if loss_sum is None else loss_sum
    return (loss_sum, grad_x[None], *[grad[n] for n in WEIGHTS], *[delta[n] for n in WEIGHTS],
            *[new_m[n] for n in WEIGHTS], *[new_v[n] for n in WEIGHTS])
```

```python
import functools
import math

import jax
import jax.numpy as jnp
import numpy as np
from jax import lax
from jax.experimental import pallas as pl
from jax.experimental.pallas import tpu as pltpu

F32 = jnp.float32
BF16 = jnp.bfloat16
MESH = pl.DeviceIdType.MESH

N_MIXERS = 4
S5_CH = 16
S5_STATE = 64
POOL_WINDOWS = (2, 4, 8, 16)
CONV_WIDTH = 31
CONV_PAD = 32
ATT_HEADS = 8
ATT_BLOCK = 128
DILATED_PATTERNS = ((128, 1), (512, 4), (2048, 16))
REL_BUCKETS = 32
REL_MAX_DIST = 2048
X_HEADS = 4
X_HEAD_DIM = 128
NORM_EPS = 1e-6
NEG_INF = -1e30
ADAM_LR, ADAM_B1, ADAM_B2, ADAM_EPS, ADAM_WD, ADAM_STEP = 0.001, 0.9, 0.999, 1e-08, 0.01, 10
N_CHIPS = 4
VMEM_LIMIT = 56 * 1024 * 1024


def _cparams(sem=None, vmem=None):
    return pltpu.CompilerParams(dimension_semantics=sem, vmem_limit_bytes=vmem)


def _tile(n, pref):
    if n <= pref:
        return n
    t = pref
    while t >= 128:
        if n % t == 0:
            return t
        t -= 128
    return n


def _spec(arr, tr, tc, rc):
    if arr.ndim == 2:
        return pl.BlockSpec((tr, tc), lambda *g: rc(*g))
    per = arr.shape[2] // tc
    assert arr.shape[2] % tc == 0

    def imap(*g):
        r, c = rc(*g)
        return (c // per, r, c % per)

    return pl.BlockSpec((None, tr, tc), imap)


def _lshape(arr):
    return arr.shape if arr.ndim == 2 else (arr.shape[1], arr.shape[0] * arr.shape[2])


def _mm(a, b, mode, *, name, out_dtype=F32, res=None, epi=None, aux=None, out_stack=None, tm=1024, tn=1024, tk=2048):
    la, lb = _lshape(a), _lshape(b)
    if mode == "nn":
        (M, K), (K2, N) = la, lb
    elif mode == "nt":
        (M, K), (N, K2) = la, lb
    else:
        (K, M), (K2, N) = la, lb
    assert K == K2, (la, lb, mode)
    lim = {"m": M, "n": N // out_stack if out_stack else N, "k": K}
    stacked = [(a, "m" if mode == "tn" else "k"), (b, "k" if mode == "nt" else "n"), (res, "n"), (aux, "n")]
    for arr, dim in stacked:
        if arr is not None and arr.ndim == 3:
            lim[dim] = math.gcd(lim[dim], arr.shape[2])
    tm, tn, tk = _tile(lim["m"], tm), _tile(lim["n"], tn), _tile(lim["k"], tk)
    nk = K // tk
    grid = (M // tm, N // tn, nk)
    a_spec = _spec(a, tk, tm, lambda i, j, k: (k, i)) if mode == "tn" else _spec(a, tm, tk, lambda i, j, k: (i, k))
    b_spec = _spec(b, tn, tk, lambda i, j, k: (j, k)) if mode == "nt" else _spec(b, tk, tn, lambda i, j, k: (k, j))
    dims = {"nn": ((1,), (0,)), "nt": ((1,), (1,)), "tn": ((0,), (0,))}[mode]
    ins, in_specs = [a, b], [a_spec, b_spec]
    for extra in (res, aux):
        if extra is not None:
            ins.append(extra)
            in_specs.append(_spec(extra, tm, tn, lambda i, j, k: (i, j)))
    if out_stack:
        o_shape = jax.ShapeDtypeStruct((out_stack, M, N // out_stack), out_dtype)
    else:
        o_shape = jax.ShapeDtypeStruct((M, N), out_dtype)
    o_spec = _spec(o_shape, tm, tn, lambda i, j, k: (i, j))
    n_out = 2 if epi == "relu2" else 1
    has_res, has_aux = res is not None, aux is not None

    def body(*refs):
        a_ref, b_ref = refs[0], refs[1]
        pos = 2
        res_ref = aux_ref = None
        if has_res:
            res_ref = refs[pos]
            pos += 1
        if has_aux:
            aux_ref = refs[pos]
            pos += 1
        outs = refs[pos:pos + n_out]
        part = lax.dot_general(a_ref[...].astype(BF16), b_ref[...].astype(BF16), (dims, ((), ())), preferred_element_type=F32)
        if nk > 1:
            acc_ref = refs[pos + n_out]
            k = pl.program_id(2)

            @pl.when(k == 0)
            def _():
                acc_ref[...] = part

            @pl.when(k > 0)
            def _():
                acc_ref[...] += part

        @pl.when(pl.program_id(2) == nk - 1)
        def _():
            o = acc_ref[...] if nk > 1 else part
            if has_res:
                o = o + res_ref[...].astype(F32)
            if epi == "relu2":
                outs[0][...] = o.astype(outs[0].dtype)
                r = jnp.maximum(o, 0.0)
                outs[1][...] = (r * r).astype(outs[1].dtype)
            elif epi == "relu2_bwd":
                outs[0][...] = (o * (2.0 * jnp.maximum(aux_ref[...].astype(F32), 0.0))).astype(outs[0].dtype)
            else:
                outs[0][...] = o.astype(outs[0].dtype)

    out = pl.pallas_call(
        body, grid=grid, in_specs=in_specs,
        out_specs=[o_spec] * n_out if n_out > 1 else o_spec,
        out_shape=[o_shape] * n_out if n_out > 1 else o_shape,
        scratch_shapes=[pltpu.VMEM((tm, tn), F32)] if nk > 1 else [],
        compiler_params=_cparams(("parallel", "parallel", "arbitrary"), VMEM_LIMIT), name=name,
    )(*ins)
    return out


def _rms_fwd(x, g, *, name, out_dtype=BF16):
    L, D = x.shape
    tr = _tile(L, 256)

    def body(x_ref, g_ref, o_ref):
        xv = x_ref[...]
        r = lax.rsqrt(jnp.mean(xv * xv, axis=-1, keepdims=True) + NORM_EPS)
        o_ref[...] = (xv * r * g_ref[...]).astype(o_ref.dtype)

    return pl.pallas_call(
        body, grid=(L // tr,),
        in_specs=[pl.BlockSpec((tr, D), lambda i: (i, 0)), pl.BlockSpec((1, D), lambda i: (0, 0))],
        out_specs=pl.BlockSpec((tr, D), lambda i: (i, 0)),
        out_shape=jax.ShapeDtypeStruct((L, D), out_dtype),
        compiler_params=_cparams(("parallel",)), name=name)(x, g)


def _rms_bwd(x, g, dy, dres, *, name, token=None):
    L, D = x.shape
    tr = _tile(L, 256)
    has_res, has_tok = dres is not None, token is not None

    def body(*refs):
        x_ref, g_ref, dy_ref = refs[:3]
        dres_ref = refs[3] if has_res else None
        tok_ref = refs[3 + has_res] if has_tok else None
        dx_ref, dg_ref = refs[-2:]
        xv = x_ref[...]
        dyv = dy_ref[...].astype(F32)
        r = lax.rsqrt(jnp.mean(xv * xv, axis=-1, keepdims=True) + NORM_EPS)
        xh = xv * r
        dyg = dyv * g_ref[...]
        dx = r * (dyg - xh * jnp.mean(dyg * xh, axis=-1, keepdims=True))
        if has_res:
            dx = dx + dres_ref[...]
        if has_tok:
            dx = dx + tok_ref[0:1, 0:1]
        dx_ref[...] = dx

        @pl.when(pl.program_id(0) == 0)
        def _():
            dg_ref[...] = jnp.zeros_like(dg_ref)

        dg_ref[...] += jnp.sum(dyv * xh, axis=0, keepdims=True)

    row = pl.BlockSpec((tr, D), lambda i: (i, 0))
    vec = pl.BlockSpec((1, D), lambda i: (0, 0))
    ins = [x, g, dy] + ([dres] if has_res else []) + ([token] if has_tok else [])
    return pl.pallas_call(
        body, grid=(L // tr,),
        in_specs=[row, vec, row] + ([row] if has_res else []) + ([pl.BlockSpec(token.shape, lambda i: (0, 0))] if has_tok else []),
        out_specs=[row, vec],
        out_shape=[jax.ShapeDtypeStruct((L, D), F32), jax.ShapeDtypeStruct((1, D), F32)],
        compiler_params=_cparams(("arbitrary",)), name=name)(*ins)


def _loss_head(h, g, target, *, name):
    L, D = h.shape
    tr = _tile(L, 256)

    def body(x_ref, g_ref, t_ref, loss_ref, dx_ref, dg_ref):
        xv = x_ref[...]
        r = lax.rsqrt(jnp.mean(xv * xv, axis=-1, keepdims=True) + NORM_EPS)
        xh = xv * r
        err = xh * g_ref[...] - t_ref[...]
        dyv = err * (1.0 / D)
        dyg = dyv * g_ref[...]
        dx_ref[...] = r * (dyg - xh * jnp.mean(dyg * xh, axis=-1, keepdims=True))

        @pl.when(pl.program_id(0) == 0)
        def _():
            dg_ref[...] = jnp.zeros_like(dg_ref)
            loss_ref[...] = jnp.zeros_like(loss_ref)

        dg_ref[...] += jnp.sum(dyv * xh, axis=0, keepdims=True)
        loss_ref[...] += 0.5 * jnp.sum(jnp.sum(err * err, axis=1, keepdims=True), axis=0, keepdims=True) * (1.0 / D)

    row = pl.BlockSpec((tr, D), lambda i: (i, 0))
    vec = pl.BlockSpec((1, D), lambda i: (0, 0))
    return pl.pallas_call(
        body, grid=(L // tr,), in_specs=[row, vec, row],
        out_specs=[pl.BlockSpec((1, 1), lambda i: (0, 0)), row, vec],
        out_shape=[jax.ShapeDtypeStruct((1, 1), F32), jax.ShapeDtypeStruct((L, D), F32), jax.ShapeDtypeStruct((1, D), F32)],
        compiler_params=_cparams(("arbitrary",)), name=name)(h, g, target)


S5_TL = 256
S5_LW = 512


def _dot(a, b, dims):
    return lax.dot_general(a, b, (dims, ((), ())), preferred_element_type=F32)


_NN, _NT, _TN = ((1,), (0,)), ((1,), (1,)), ((0,), (0,))


def _gelu_and_grad(y):
    k = math.sqrt(2.0 / math.pi)
    y2 = y * y
    th = jnp.tanh(k * (y + 0.044715 * y * y2))
    g = 0.5 * y * (1.0 + th)
    gp = 0.5 * (1.0 + th) + 0.5 * y * (1.0 - th * th) * k * (1.0 + 3.0 * 0.044715 * y2)
    return g, gp


def _scan_tiles(re_s, im_s, ls, lw, pw_ref, tr_ref, ti_ref, carry_s, nt, reverse, extra=None):
    row = lax.broadcasted_iota(jnp.int32, (8, lw), 0)
    a = [pw_ref[k:k + 1, ls] for k in range(6)]
    tr_t, ti_t = tr_ref[:, ls], ti_ref[:, ls]
    edge = 0 if reverse else 7

    def tile(jj, carry):
        cr, ci, acc = carry
        j = (nt - 1 - jj) if reverse else jj
        off = pl.multiple_of(j * 8, 8)
        sr, si = re_s[pl.ds(off, 8), ls], im_s[pl.ds(off, 8), ls]
        for n, k in enumerate((1, 2, 4)):
            akr, aki = a[2 * n], a[2 * n + 1]
            if reverse:
                keep, sh = row < 8 - k, 8 - k
            else:
                keep, sh = row >= k, k
            shr = jnp.where(keep, pltpu.roll(sr, sh, 0), 0.0)
            shi = jnp.where(keep, pltpu.roll(si, sh, 0), 0.0)
            sr, si = sr + akr * shr - aki * shi, si + akr * shi + aki * shr
        xr = sr + tr_t * cr - ti_t * ci
        xi = si + tr_t * ci + ti_t * cr
        re_s[pl.ds(off, 8), ls] = xr
        im_s[pl.ds(off, 8), ls] = xi
        if extra is not None:
            acc = extra(off, xr, xi, acc, row)
        return xr[edge:edge + 1, :], xi[edge:edge + 1, :], acc

    acc0 = (jnp.zeros((8, lw), F32), jnp.zeros((8, lw), F32)) if extra is not None else 0
    cr, ci, acc = lax.fori_loop(0, nt, tile, (carry_s[0:1, ls], carry_s[1:2, ls], acc0))
    carry_s[0:1, ls] = cr
    carry_s[1:2, ls] = ci
    return acc


def _s5_fwd(proj, bdr, bdi, pw, tr, ti, cdr, cdi, dskip, wglu, *, name):
    L = proj.shape[0]
    GW, S = bdr.shape
    TL = _tile(L, S5_TL)
    lw = min(S, S5_LW)

    def body(u_ref, bdr_ref, bdi_ref, pw_ref, tr_ref, ti_ref, cdr_ref, cdi_ref, d_ref, w_ref,
             y_ref, xr_ref, xi_ref, re_s, im_s, carry_s):
        @pl.when(pl.program_id(0) == 0)
        def _():
            carry_s[...] = jnp.zeros_like(carry_s)

        u = u_ref[...]
        ub = u.astype(BF16)
        re_s[...] = _dot(ub, bdr_ref[...], _NN)
        im_s[...] = _dot(ub, bdi_ref[...], _NN)
        for lc in range(S // lw):
            _scan_tiles(re_s, im_s, slice(lc * lw, (lc + 1) * lw), lw, pw_ref, tr_ref, ti_ref, carry_s, TL // 8, False)
        xrb, xib = re_s[...].astype(BF16), im_s[...].astype(BF16)
        xr_ref[...] = xrb
        xi_ref[...] = xib
        y = _dot(xrb, cdr_ref[...], _NN) - _dot(xib, cdi_ref[...], _NN) + d_ref[...] * u
        g, _ = _gelu_and_grad(y)
        z = _dot(g.astype(BF16), w_ref[...], _NN)
        y_ref[...] = g * jax.nn.sigmoid(z)

    full = lambda arr: pl.BlockSpec(arr.shape, lambda i: (0,) * arr.ndim)
    return pl.pallas_call(
        body, grid=(L // TL,),
        in_specs=[pl.BlockSpec((TL, GW), lambda i: (i, 0))] + [full(t) for t in (bdr, bdi, pw, tr, ti, cdr, cdi, dskip, wglu)],
        out_specs=[pl.BlockSpec((TL, GW), lambda i: (i, 0)), pl.BlockSpec((TL, S), lambda i: (i, 0)), pl.BlockSpec((TL, S), lambda i: (i, 0))],
        out_shape=[jax.ShapeDtypeStruct((L, GW), F32), jax.ShapeDtypeStruct((L, S), BF16), jax.ShapeDtypeStruct((L, S), BF16)],
        scratch_shapes=[pltpu.VMEM((TL, S), F32), pltpu.VMEM((TL, S), F32), pltpu.VMEM((8, S), F32)],
        compiler_params=_cparams(("arbitrary",), VMEM_LIMIT), name=name,
    )(proj, bdr, bdi, pw, tr, ti, cdr, cdi, dskip, wglu)


def _s5_bwd(proj, xr, xi, dout, bdr, bdi, pwc, trc, tic, cdr, cdi, dskip, wglu, *, name):
    L = proj.shape[0]
    GW, S = bdr.shape
    TL = _tile(L, S5_TL)
    nc = L // TL
    lw = min(S, S5_LW)

    def body(u_ref, xr_ref, xi_ref, xrp_ref, xip_ref, do_ref, bdr_ref, bdi_ref, pw_ref, tr_ref, ti_ref, cdr_ref, cdi_ref,
             d_ref, w_ref, du_ref, dw_ref, dd_ref, dcdr_ref, dcdi_ref, dbdr_ref, dbdi_ref, dar_ref, dai_ref,
             gr_s, gi_s, xfr, xfi, carry_s):
        i = pl.program_id(0)

        @pl.when(i == 0)
        def _():
            carry_s[...] = jnp.zeros_like(carry_s)
            for r in (dw_ref, dd_ref, dcdr_ref, dcdi_ref, dbdr_ref, dbdi_ref, dar_ref, dai_ref):
                r[...] = jnp.zeros_like(r)

        u = u_ref[...]
        ub = u.astype(BF16)
        xrb, xib = xr_ref[...], xi_ref[...]
        y = _dot(xrb, cdr_ref[...], _NN) - _dot(xib, cdi_ref[...], _NN) + d_ref[...] * u
        g, gp = _gelu_and_grad(y)
        gb = g.astype(BF16)
        sg = jax.nn.sigmoid(_dot(gb, w_ref[...], _NN))
        dout = do_ref[...]
        dz = (dout * g * sg * (1.0 - sg)).astype(BF16)
        dg = dout * sg + _dot(dz, w_ref[...], _NT)
        dw_ref[...] += _dot(gb, dz, _TN)
        dy = dg * gp
        dyb = dy.astype(BF16)
        dd_ref[...] += jnp.sum(dy * u, axis=0, keepdims=True)
        gr_s[...] = _dot(dyb, cdr_ref[...], _NT)
        gi_s[...] = -_dot(dyb, cdi_ref[...], _NT)
        dcdr_ref[...] += _dot(xrb, dyb, _TN)
        dcdi_ref[...] -= _dot(xib, dyb, _TN)
        live = (i < nc - 1).astype(F32)
        xfr[0:8, :] = xrp_ref[...].astype(F32) * live
        xfi[0:8, :] = xip_ref[...].astype(F32) * live
        xfr[8:, :] = xrb.astype(F32)
        xfi[8:, :] = xib.astype(F32)
        for lc in range(S // lw):
            ls = slice(lc * lw, (lc + 1) * lw)

            def extra(off, lr, li, acc, row, ls=ls):
                cur_r, cur_i = xfr[pl.ds(off + 8, 8), ls], xfi[pl.ds(off + 8, 8), ls]
                prv_r, prv_i = xfr[pl.ds(off, 8), ls], xfi[pl.ds(off, 8), ls]
                xpr = jnp.where(row >= 1, pltpu.roll(cur_r, 1, 0), prv_r[7:8, :])
                xpi = jnp.where(row >= 1, pltpu.roll(cur_i, 1, 0), prv_i[7:8, :])
                return acc[0] + lr * xpr + li * xpi, acc[1] - lr * xpi + li * xpr

            acc = _scan_tiles(gr_s, gi_s, ls, lw, pw_ref, tr_ref, ti_ref, carry_s, TL // 8, True, extra)
            dar_ref[:, ls] += acc[0]
            dai_ref[:, ls] += acc[1]
        lrb, lib = gr_s[...].astype(BF16), gi_s[...].astype(BF16)
        du_ref[...] = dy * d_ref[...] + _dot(lrb, bdr_ref[...], _NT) + _dot(lib, bdi_ref[...], _NT)
        dbdr_ref[...] += _dot(ub, lrb, _TN)
        dbdi_ref[...] += _dot(ub, lib, _TN)

    rev = lambda i: nc - 1 - i
    full = lambda arr: pl.BlockSpec(arr.shape, lambda i: (0,) * arr.ndim)
    chunk = lambda w: pl.BlockSpec((TL, w), lambda i: (rev(i), 0))
    prev8 = pl.BlockSpec((8, S), lambda i: (jnp.maximum(rev(i) * (TL // 8) - 1, 0), 0))
    acc_shapes = [(GW, GW), (1, GW), (S, GW), (S, GW), (GW, S), (GW, S), (8, S), (8, S)]
    return pl.pallas_call(
        body, grid=(nc,),
        in_specs=[chunk(GW), chunk(S), chunk(S), prev8, prev8, chunk(GW)] + [full(t) for t in (bdr, bdi, pwc, trc, tic, cdr, cdi, dskip, wglu)],
        out_specs=[chunk(GW)] + [pl.BlockSpec(s, lambda i: (0, 0)) for s in acc_shapes],
        out_shape=[jax.ShapeDtypeStruct((L, GW), F32)] + [jax.ShapeDtypeStruct(s, F32) for s in acc_shapes],
        scratch_shapes=[pltpu.VMEM((TL, S), F32), pltpu.VMEM((TL, S), F32), pltpu.VMEM((TL + 8, S), F32),
                        pltpu.VMEM((TL + 8, S), F32), pltpu.VMEM((8, S), F32)],
        compiler_params=_cparams(("arbitrary",), VMEM_LIMIT), name=name,
    )(proj, xr, xi, xr, xi, dout, bdr, bdi, pwc, trc, tic, cdr, cdi, dskip, wglu)


def _cpow_tables(ar, ai):
    def cm(a, b):
        return a[0] * b[0] - a[1] * b[1], a[0] * b[1] + a[1] * b[0]
    p = [(ar, ai)]
    for _ in range(7):
        p.append(cm(p[-1], p[0]))
    z = jnp.zeros_like(ar)
    pw = jnp.stack([p[0][0], p[0][1], p[1][0], p[1][1], p[3][0], p[3][1], z, z])
    return pw, jnp.stack([q[0] for q in p]), jnp.stack([q[1] for q in p])


def _s5_disc(lam_re, lam_im, log_dt, b_re, b_im):
    dt = jnp.exp(log_dt)[:, None]
    mag = jnp.exp(lam_re * dt)
    ab_r, ab_i = mag * jnp.cos(lam_im * dt), mag * jnp.sin(lam_im * dt)
    den = lam_re * lam_re + lam_im * lam_im
    nr, ni = ab_r - 1.0, ab_i
    f_r = ((nr * lam_re + ni * lam_im) / den)[..., None]
    f_i = ((ni * lam_re - nr * lam_im) / den)[..., None]
    return ab_r, ab_i, f_r * b_re - f_i * b_im, f_r * b_im + f_i * b_re


def _bd(t):
    G, A, B = t.shape
    return (t[:, :, None, :] * jnp.eye(G, dtype=t.dtype)[:, None, :, None]).reshape(G * A, G * B)


def _bd_extract(m, G):
    A, B = m.shape[0] // G, m.shape[1] // G
    return jnp.sum(m.reshape(G, A, G, B) * jnp.eye(G, dtype=m.dtype)[:, None, :, None], axis=2)


POOL_TQ = 256


def _band(shape, row0, col0, win, transpose):
    r = lax.broadcasted_iota(jnp.int32, shape, 0) + row0
    c = lax.broadcasted_iota(jnp.int32, shape, 1) + col0
    d = (c - r) if transpose else (r - c)
    return ((d >= 0) & (d < win)).astype(BF16)


def _band_dot(band, v):
    hi = v.astype(BF16)
    lo = (v - hi.astype(F32)).astype(BF16)
    return _dot(band, hi, _NN) + _dot(band, lo, _NN)


def _pool_p(u_prev, u_cur, i, win, tq):
    t0 = i * tq
    cnt = jnp.minimum(lax.broadcasted_iota(jnp.int32, (tq, 1), 0) + t0 + 1, win).astype(F32)
    live = (i > 0).astype(F32)
    acc = _band_dot(_band((tq, tq), t0, t0, win, False), u_cur)
    acc += _band_dot(_band((tq, tq), t0, t0 - tq, win, False), u_prev * live)
    return acc / cnt - u_cur


def _pool_fwd(proj, pool_w, pool_scale3, *, name):
    L = proj.shape[0]
    nw, PC, _ = pool_w.shape
    tq = _tile(L, POOL_TQ)

    def body(up_ref, uc_ref, w_ref, s_ref, y_ref):
        g, i = pl.program_id(0), pl.program_id(1)
        win = jnp.left_shift(2, g)
        p = _pool_p(up_ref[...], uc_ref[...], i, win, tq)
        y_ref[...] = _dot(p.astype(BF16), w_ref[...].astype(BF16), _NN) * s_ref[...]

    return pl.pallas_call(
        body, grid=(nw, L // tq),
        in_specs=[pl.BlockSpec((tq, PC), lambda g, i: (jnp.maximum(i - 1, 0), nw + g)),
                  pl.BlockSpec((tq, PC), lambda g, i: (i, nw + g)),
                  pl.BlockSpec((None, PC, PC), lambda g, i: (g, 0, 0)),
                  pl.BlockSpec((None, 1, PC), lambda g, i: (g, 0, 0))],
        out_specs=pl.BlockSpec((tq, PC), lambda g, i: (i, g)),
        out_shape=jax.ShapeDtypeStruct((L, nw * PC), F32),
        compiler_params=_cparams(("parallel", "parallel")), name=name)(proj, proj, pool_w, pool_scale3)


def _pool_bwd(proj, dy, pool_w, pool_scale3, *, name):
    L = proj.shape[0]
    nw, PC, _ = pool_w.shape
    tq = _tile(L, POOL_TQ)
    nq = L // tq

    def body(up_ref, uc_ref, dyc_ref, dyn_ref, w_ref, s_ref, du_ref, dw_ref, ds_ref):
        g, i = pl.program_id(0), pl.program_id(1)
        win = jnp.left_shift(2, g)

        @pl.when(i == 0)
        def _():
            dw_ref[...] = jnp.zeros_like(dw_ref)
            ds_ref[...] = jnp.zeros_like(ds_ref)

        wb = w_ref[...].astype(BF16)
        p = _pool_p(up_ref[...], uc_ref[...], i, win, tq).astype(BF16)
        dyc = dyc_ref[...]
        ds_ref[...] += jnp.sum(dyc * _dot(p, wb, _NN), axis=0, keepdims=True)
        dys = (dyc * s_ref[...]).astype(BF16)
        dw_ref[...] += _dot(p, dys, _TN)
        t0 = i * tq
        rows = lax.broadcasted_iota(jnp.int32, (tq, 1), 0)
        dp_c = _dot(dys, wb, _NT)
        q_c = dp_c / jnp.minimum(rows + t0 + 1, win).astype(F32)
        live = (i < nq - 1).astype(F32)
        dp_n = _dot((dyn_ref[...] * s_ref[...] * live).astype(BF16), wb, _NT)
        q_n = dp_n / jnp.minimum(rows + t0 + tq + 1, win).astype(F32)
        du = _band_dot(_band((tq, tq), t0, t0, win, True), q_c) + _band_dot(_band((tq, tq), t0, t0 + tq, win, True), q_n)
        du_ref[...] = du - dp_c

    return pl.pallas_call(
        body, grid=(nw, nq),
        in_specs=[pl.BlockSpec((tq, PC), lambda g, i: (jnp.maximum(i - 1, 0), nw + g)),
                  pl.BlockSpec((tq, PC), lambda g, i: (i, nw + g)),
                  pl.BlockSpec((tq, PC), lambda g, i: (i, g)),
                  pl.BlockSpec((tq, PC), lambda g, i: (jnp.minimum(i + 1, nq - 1), g)),
                  pl.BlockSpec((None, PC, PC), lambda g, i: (g, 0, 0)),
                  pl.BlockSpec((None, 1, PC), lambda g, i: (g, 0, 0))],
        out_specs=[pl.BlockSpec((tq, PC), lambda g, i: (i, g)),
                   pl.BlockSpec((None, PC, PC), lambda g, i: (g, 0, 0)),
                   pl.BlockSpec((None, 1, PC), lambda g, i: (g, 0, 0))],
        out_shape=[jax.ShapeDtypeStruct((L, nw * PC), F32), jax.ShapeDtypeStruct((nw, PC, PC), F32),
                   jax.ShapeDtypeStruct((nw, 1, PC), F32)],
        compiler_params=_cparams(("parallel", "arbitrary")), name=name)(proj, proj, dy, dy, pool_w, pool_scale3)


CONV_RC = 256


def _conv1_fwd(proj, w_dw, b_dw, *, name):
    L = proj.shape[0]
    GW = w_dw.shape[1]
    CB = min(GW, 128)
    nb = GW // CB
    RC = _tile(L, CONV_RC)
    n = RC + CONV_PAD

    def body(val_ref, gate_ref, w_ref, b_ref, o_ref, hp):
        hp[0:CONV_PAD, :] = jnp.zeros((CONV_PAD, CB), F32)
        hp[CONV_PAD:, :] = val_ref[...] * jax.nn.sigmoid(gate_ref[...])
        wv = w_ref[...]

        def chunk(ci, carry):
            c0 = pl.multiple_of(ci * RC, 8)
            win = hp[pl.ds(c0, n), :]
            acc = jnp.zeros((RC, CB), F32) + b_ref[...]
            for k in range(CONV_WIDTH):
                acc = acc + wv[k:k + 1, :] * pltpu.roll(win, n - (k + 2), 0)[0:RC]
            o_ref[pl.ds(c0, RC), :] = acc
            return carry

        lax.fori_loop(0, L // RC, chunk, 0)

    col = lambda off: pl.BlockSpec((L, CB), lambda c: (0, off * nb + c))
    return pl.pallas_call(
        body, grid=(nb,),
        in_specs=[col(2), col(3), pl.BlockSpec((CONV_PAD, CB), lambda c: (0, c)), pl.BlockSpec((1, CB), lambda c: (0, c))],
        out_specs=pl.BlockSpec((L, CB), lambda c: (0, c)),
        out_shape=jax.ShapeDtypeStruct((L, GW), F32),
        scratch_shapes=[pltpu.VMEM((L + CONV_PAD, CB), F32)],
        compiler_params=_cparams(("parallel",)), name=name)(proj, proj, w_dw, b_dw)


def _conv1_bwd(proj, dhc, w_dw, *, name):
    L = proj.shape[0]
    GW = w_dw.shape[1]
    CB = min(GW, 128)
    nb = GW // CB
    RC = _tile(L, CONV_RC)
    n = RC + CONV_PAD

    def body(val_ref, gate_ref, d_ref, w_ref, dval_ref, dgate_ref, dw_ref, db_ref, hp, dp):
        val = val_ref[...]
        sg = jax.nn.sigmoid(gate_ref[...])
        hp[0:CONV_PAD, :] = jnp.zeros((CONV_PAD, CB), F32)
        hp[CONV_PAD:, :] = val * sg
        dp[0:L, :] = d_ref[...]
        dp[L:, :] = jnp.zeros((CONV_PAD, CB), F32)
        dw_ref[...] = jnp.zeros_like(dw_ref)
        db_ref[...] = jnp.sum(d_ref[...], axis=0, keepdims=True)
        wv = w_ref[...]

        def chunk(ci, carry):
            c0 = pl.multiple_of(ci * RC, 8)
            win = hp[pl.ds(c0, n), :]
            dwin = dp[pl.ds(c0, n), :]
            d = dwin[0:RC]
            dh = jnp.zeros((RC, CB), F32)
            for k in range(CONV_WIDTH):
                dw_ref[k:k + 1, :] += jnp.sum(d * pltpu.roll(win, n - (k + 2), 0)[0:RC], axis=0, keepdims=True)
                sh = CONV_WIDTH - 1 - k
                dh = dh + wv[k:k + 1, :] * (pltpu.roll(dwin, n - sh, 0)[0:RC] if sh else d)
            v, s = val_ref[pl.ds(c0, RC), :], jax.nn.sigmoid(gate_ref[pl.ds(c0, RC), :])
            dval_ref[pl.ds(c0, RC), :] = dh * s
            dgate_ref[pl.ds(c0, RC), :] = dh * v * s * (1.0 - s)
            return carry

        lax.fori_loop(0, L // RC, chunk, 0)

    col = lambda off: pl.BlockSpec((L, CB), lambda c: (0, off * nb + c))
    own = pl.BlockSpec((L, CB), lambda c: (0, c))
    return pl.pallas_call(
        body, grid=(nb,),
        in_specs=[col(2), col(3), own, pl.BlockSpec((CONV_PAD, CB), lambda c: (0, c))],
        out_specs=[own, own, pl.BlockSpec((CONV_PAD, CB), lambda c: (0, c)), pl.BlockSpec((1, CB), lambda c: (0, c))],
        out_shape=[jax.ShapeDtypeStruct((L, GW), F32), jax.ShapeDtypeStruct((L, GW), F32),
                   jax.ShapeDtypeStruct((CONV_PAD, GW), F32), jax.ShapeDtypeStruct((1, GW), F32)],
        scratch_shapes=[pltpu.VMEM((L + CONV_PAD, CB), F32), pltpu.VMEM((L + CONV_PAD, CB), F32)],
        compiler_params=_cparams(("parallel",)), name=name)(proj, proj, dhc, w_dw)


def _ln_silu(hc, g, b):
    mu = jnp.mean(hc, axis=-1, keepdims=True)
    xc = hc - mu
    r = lax.rsqrt(jnp.mean(xc * xc, axis=-1, keepdims=True) + NORM_EPS)
    xh = xc * r
    a = xh * g + b
    sg = jax.nn.sigmoid(a)
    return xh, r, a, sg


def _conv2_fwd(hc, ln_g, ln_b, w_pw, *, name):
    L, GW = hc.shape
    tr = _tile(L, 256)

    def body(h_ref, g_ref, b_ref, w_ref, y_ref):
        _, _, a, sg = _ln_silu(h_ref[...], g_ref[...], b_ref[...])
        y_ref[...] = _dot((a * sg).astype(BF16), w_ref[...], _NN)

    row = pl.BlockSpec((tr, GW), lambda i: (i, 0))
    vec = pl.BlockSpec((1, GW), lambda i: (0, 0))
    return pl.pallas_call(
        body, grid=(L // tr,), in_specs=[row, vec, vec, pl.BlockSpec((GW, GW), lambda i: (0, 0))],
        out_specs=row, out_shape=jax.ShapeDtypeStruct((L, GW), F32),
        compiler_params=_cparams(("parallel",)), name=name)(hc, ln_g, ln_b, w_pw)


def _conv2_bwd(hc, ln_g, ln_b, w_pw, dy, *, name):
    L, GW = hc.shape
    tr = _tile(L, 256)

    def body(h_ref, g_ref, b_ref, w_ref, dy_ref, dh_ref, dg_ref, db_ref, dw_ref):
        @pl.when(pl.program_id(0) == 0)
        def _():
            for r in (dg_ref, db_ref, dw_ref):
                r[...] = jnp.zeros_like(r)

        xh, r, a, sg = _ln_silu(h_ref[...], g_ref[...], b_ref[...])
        dyb = dy_ref[...].astype(BF16)
        dw_ref[...] += _dot((a * sg).astype(BF16), dyb, _TN)
        da = _dot(dyb, w_ref[...], _NT) * (sg + a * sg * (1.0 - sg))
        dg_ref[...] += jnp.sum(da * xh, axis=0, keepdims=True)
        db_ref[...] += jnp.sum(da, axis=0, keepdims=True)
        dxh = da * g_ref[...]
        dh_ref[...] = r * (dxh - jnp.mean(dxh, axis=-1, keepdims=True) - xh * jnp.mean(dxh * xh, axis=-1, keepdims=True))

    row = pl.BlockSpec((tr, GW), lambda i: (i, 0))
    vec = pl.BlockSpec((1, GW), lambda i: (0, 0))
    mat = pl.BlockSpec((GW, GW), lambda i: (0, 0))
    return pl.pallas_call(
        body, grid=(L // tr,), in_specs=[row, vec, vec, mat, row],
        out_specs=[row, vec, vec, mat],
        out_shape=[jax.ShapeDtypeStruct((L, GW), F32), jax.ShapeDtypeStruct((1, GW), F32), jax.ShapeDtypeStruct((1, GW), F32),
                   jax.ShapeDtypeStruct((GW, GW), F32)],
        compiler_params=_cparams(("arbitrary",)), name=name)(hc, ln_g, ln_b, w_pw, dy)


def _grp_fwd(ys, g, *, name):
    L, GW = ys[0].shape
    tr = _tile(L, 256)

    def body(*refs):
        g_ref, o_ref = refs[4], refs[5]
        for m in range(N_MIXERS):
            yv = refs[m][...]
            r = lax.rsqrt(jnp.mean(yv * yv, axis=-1, keepdims=True) + NORM_EPS)
            o_ref[:, m * GW:(m + 1) * GW] = (yv * r * g_ref[:, m * GW:(m + 1) * GW]).astype(o_ref.dtype)

    row = pl.BlockSpec((tr, GW), lambda i: (i, 0))
    return pl.pallas_call(
        body, grid=(L // tr,), in_specs=[row] * 4 + [pl.BlockSpec((1, N_MIXERS * GW), lambda i: (0, 0))],
        out_specs=pl.BlockSpec((tr, N_MIXERS * GW), lambda i: (i, 0)),
        out_shape=jax.ShapeDtypeStruct((L, N_MIXERS * GW), BF16),
        compiler_params=_cparams(("parallel",)), name=name)(*ys, g)


def _grp_bwd(ys, g, dy, *, name):
    L, GW = ys[0].shape
    tr = _tile(L, 256)

    def body(*refs):
        g_ref, dy_ref = refs[4], refs[5]
        outs, dg_ref = refs[6:10], refs[10]

        @pl.when(pl.program_id(0) == 0)
        def _():
            dg_ref[...] = jnp.zeros_like(dg_ref)

        for m in range(N_MIXERS):
            cs = slice(m * GW, (m + 1) * GW)
            yv = refs[m][...]
            r = lax.rsqrt(jnp.mean(yv * yv, axis=-1, keepdims=True) + NORM_EPS)
            xh = yv * r
            dyv = dy_ref[:, cs]
            dyg = dyv * g_ref[:, cs]
            outs[m][...] = r * (dyg - xh * jnp.mean(dyg * xh, axis=-1, keepdims=True))
            dg_ref[:, cs] += jnp.sum(dyv * xh, axis=0, keepdims=True)

    row = pl.BlockSpec((tr, GW), lambda i: (i, 0))
    wide = pl.BlockSpec((tr, N_MIXERS * GW), lambda i: (i, 0))
    vec = pl.BlockSpec((1, N_MIXERS * GW), lambda i: (0, 0))
    return pl.pallas_call(
        body, grid=(L // tr,), in_specs=[row] * 4 + [vec, wide],
        out_specs=[row] * 4 + [vec],
        out_shape=[jax.ShapeDtypeStruct((L, GW), F32)] * 4 + [jax.ShapeDtypeStruct((1, N_MIXERS * GW), F32)],
        compiler_params=_cparams(("arbitrary",)), name=name)(*ys, g, dy)


ATT_PAR = 4


def _datt_lanes(GW):
    E = GW // ATT_HEADS
    lb = min(GW, 128)
    return lb, lb // E, E


def _datt_rows(c, br, nblk):
    dil = 4 ** br
    nbs = nblk // dil
    r, jb = c // nbs, c % nbs
    return r + dil * ATT_BLOCK * jb, r + dil * ATT_BLOCK * jnp.maximum(jb - 1, 0), jb == 0


def _datt_fwd(proj, bias, *, name):
    L = proj.shape[0]
    GW = proj.shape[1] // 7
    lb, hps, E = _datt_lanes(GW)
    B = ATT_BLOCK
    nblk = L // B
    scale = E ** -0.5

    def body(q_ref, k_ref, v_ref, b_ref, y_ref, lse_ref, m_s, l_s):
        m_s[...] = jnp.full(m_s.shape, NEG_INF, F32)
        l_s[...] = jnp.zeros(l_s.shape, F32)
        y_ref[...] = jnp.zeros(y_ref.shape, F32)
        lane = lax.broadcasted_iota(jnp.int32, (B, lb), 1) // E

        for br, (_, dil) in enumerate(DILATED_PATTERNS):
            def step(i, carry, br=br, dil=dil):
                loaded = []
                for u in range(ATT_PAR):
                    start, pstart, first = _datt_rows(i + u * (nblk // ATT_PAR), br, nblk)
                    rows, prows = pl.ds(start, B, stride=dil), pl.ds(pstart, B, stride=dil)
                    loaded.append((rows, first, q_ref[rows, :] * scale, k_ref[rows, :].astype(BF16), k_ref[prows, :].astype(BF16),
                                   v_ref[rows, :].astype(BF16), v_ref[prows, :].astype(BF16), m_s[rows, :], l_s[rows, :], y_ref[rows, :]))
                done = []
                for rows, first, q, kc, kp, vc, vp, m_old, l_old, a_old in loaded:
                    m_new, l_new, a_new = m_old, l_old, a_old
                    for a in range(hps):
                        sel = lane == a
                        qm = jnp.where(sel, q, 0.0).astype(BF16)
                        s_p = _dot(qm, kp, _NT) + b_ref[br, a, :, 0:B]
                        s_c = _dot(qm, kc, _NT) + b_ref[br, a, :, B:2 * B]
                        s_p = jnp.where(first, NEG_INF, s_p)
                        mo, lo = m_old[:, a * E:a * E + 1], l_old[:, a * E:a * E + 1]
                        mn = jnp.maximum(mo, jnp.maximum(jnp.max(s_p, axis=-1, keepdims=True), jnp.max(s_c, axis=-1, keepdims=True)))
                        alpha = jnp.exp(mo - mn)
                        p_p, p_c = jnp.exp(s_p - mn), jnp.exp(s_c - mn)
                        ln = alpha * lo + jnp.sum(p_p, axis=-1, keepdims=True) + jnp.sum(p_c, axis=-1, keepdims=True)
                        pv = _dot(p_p.astype(BF16), vp, _NN) + _dot(p_c.astype(BF16), vc, _NN)
                        m_new = jnp.where(sel, mn, m_new)
                        l_new = jnp.where(sel, ln, l_new)
                        a_new = jnp.where(sel, alpha * a_old + pv, a_new)
                    done.append((rows, m_new, l_new, a_new))
                for rows, m_new, l_new, a_new in done:
                    m_s[rows, :] = m_new
                    l_s[rows, :] = l_new
                    y_ref[rows, :] = a_new
                return carry

            lax.fori_loop(0, nblk // ATT_PAR, step, 0)
        y_ref[...] = y_ref[...] / l_s[...]
        lse_ref[...] = m_s[...] + jnp.log(l_s[...])

    col = lambda off: pl.BlockSpec((L, lb), lambda h: (0, off * (GW // lb) + h))
    own = pl.BlockSpec((L, lb), lambda h: (0, h))
    return pl.pallas_call(
        body, grid=(GW // lb,),
        in_specs=[col(4), col(5), col(6), pl.BlockSpec((len(DILATED_PATTERNS), hps, B, 2 * B), lambda h: (0, h, 0, 0))],
        out_specs=[own, own], out_shape=[jax.ShapeDtypeStruct((L, GW), F32)] * 2,
        scratch_shapes=[pltpu.VMEM((L, lb), F32), pltpu.VMEM((L, lb), F32)],
        compiler_params=_cparams(("parallel",), VMEM_LIMIT), name=name)(proj, proj, proj, bias)


def _datt_bwd(proj, bias, y, lse, dy, *, name):
    L = proj.shape[0]
    GW = proj.shape[1] // 7
    lb, hps, E = _datt_lanes(GW)
    B = ATT_BLOCK
    nblk = L // B
    scale = E ** -0.5

    def body(q_ref, k_ref, v_ref, b_ref, y_ref, lse_ref, dy_ref, dq_ref, dk_ref, dv_ref, db_ref):
        for r in (dq_ref, dk_ref, dv_ref, db_ref):
            r[...] = jnp.zeros(r.shape, F32)
        lane = lax.broadcasted_iota(jnp.int32, (B, lb), 1) // E

        for br, (_, dil) in enumerate(DILATED_PATTERNS):
            def step(i, carry, br=br, dil=dil):
                loaded = []
                for u in range(ATT_PAR):
                    start, pstart, first = _datt_rows(i + u * (nblk // ATT_PAR), br, nblk)
                    rows, prows = pl.ds(start, B, stride=dil), pl.ds(pstart, B, stride=dil)
                    dyr = dy_ref[rows, :]
                    loaded.append((rows, prows, first, q_ref[rows, :] * scale, k_ref[rows, :].astype(BF16), k_ref[prows, :].astype(BF16),
                                   v_ref[rows, :].astype(BF16), v_ref[prows, :].astype(BF16), dyr, lse_ref[rows, :], dyr * y_ref[rows, :]))
                done = []
                dbias = [None] * hps
                for rows, prows, first, q, kc, kp, vc, vp, dyr, lser, dyy in loaded:
                    dq = jnp.zeros((B, lb), F32)
                    dkc, dkp, dvc, dvp = dq, dq, dq, dq
                    for a in range(hps):
                        sel = lane == a
                        qm = jnp.where(sel, q, 0.0).astype(BF16)
                        dym = jnp.where(sel, dyr, 0.0).astype(BF16)
                        dm = jnp.sum(jnp.where(sel, dyy, 0.0), axis=-1, keepdims=True)
                        lse_a = lser[:, a * E:a * E + 1]
                        s_p = _dot(qm, kp, _NT) + b_ref[br, a, :, 0:B]
                        s_c = _dot(qm, kc, _NT) + b_ref[br, a, :, B:2 * B]
                        s_p = jnp.where(first, NEG_INF, s_p)
                        p_p, p_c = jnp.exp(s_p - lse_a), jnp.exp(s_c - lse_a)
                        ds_p = p_p * (_dot(dym, vp, _NT) - dm)
                        ds_c = p_c * (_dot(dym, vc, _NT) - dm)
                        dbias[a] = (ds_p, ds_c) if dbias[a] is None else (dbias[a][0] + ds_p, dbias[a][1] + ds_c)
                        dsb_p, dsb_c = ds_p.astype(BF16), ds_c.astype(BF16)
                        dq = dq + jnp.where(sel, _dot(dsb_p, kp, _NN) + _dot(dsb_c, kc, _NN), 0.0)
                        dkp = dkp + _dot(dsb_p, qm, _TN)
                        dkc = dkc + _dot(dsb_c, qm, _TN)
                        dvp = dvp + _dot(p_p.astype(BF16), dym, _TN)
                        dvc = dvc + _dot(p_c.astype(BF16), dym, _TN)
                    done.append((rows, prows, dq, dkp, dkc, dvp, dvc))
                for a in range(hps):
                    db_ref[br, a, :, 0:B] += dbias[a][0]
                    db_ref[br, a, :, B:2 * B] += dbias[a][1]
                for rows, prows, dq, dkp, dkc, dvp, dvc in done:
                    dq_ref[rows, :] += dq * scale
                    dk_ref[prows, :] += dkp
                    dk_ref[rows, :] += dkc
                    dv_ref[prows, :] += dvp
                    dv_ref[rows, :] += dvc
                return carry

            lax.fori_loop(0, nblk // ATT_PAR, step, 0)

    col = lambda off: pl.BlockSpec((L, lb), lambda h: (0, off * (GW // lb) + h))
    own = pl.BlockSpec((L, lb), lambda h: (0, h))
    bsp = pl.BlockSpec((len(DILATED_PATTERNS), hps, B, 2 * B), lambda h: (0, h, 0, 0))
    return pl.pallas_call(
        body, grid=(GW // lb,),
        in_specs=[col(4), col(5), col(6), bsp, own, own, own], out_specs=[own, own, own, bsp],
        out_shape=[jax.ShapeDtypeStruct((L, GW), F32)] * 3 + [jax.ShapeDtypeStruct((len(DILATED_PATTERNS), ATT_HEADS, B, 2 * B), F32)],
        compiler_params=_cparams(("parallel",), VMEM_LIMIT), name=name)(proj, proj, proj, bias, y, lse, dy)


def _t5_bucket(dist):
    n = np.maximum(dist, 0)
    max_exact = REL_BUCKETS // 2
    large = max_exact + (np.log(np.maximum(n, 1) / max_exact) / np.log(REL_MAX_DIST / max_exact)
                         * (REL_BUCKETS - max_exact)).astype(np.int64)
    large = np.minimum(large, REL_BUCKETS - 1)
    return np.where(n < max_exact, n, large).astype(np.int32)


def _att_tables():
    a = np.arange(ATT_BLOCK)[:, None]
    b = np.arange(2 * ATT_BLOCK)[None, :]
    sub = a + ATT_BLOCK - b
    buckets, valids = [], []
    for window, dil in DILATED_PATTERNS:
        buckets.append(_t5_bucket(sub * dil))
        valids.append((sub >= 0) & (sub <= window // dil))
    return np.stack(buckets), np.stack(valids)


def _xatt_fwd(q, k, v, *, name):
    L, XW = q.shape
    M = k.shape[0]
    tr = _tile(L, 512)
    scale = X_HEAD_DIM ** -0.5

    def body(q_ref, k_ref, v_ref, o_ref):
        s = _dot(q_ref[...], k_ref[...], _NT) * scale
        p = jnp.exp(s - jnp.max(s, axis=-1, keepdims=True))
        den = jnp.sum(p, axis=-1, keepdims=True)
        o_ref[...] = (_dot(p.astype(BF16), v_ref[...], _NN) / den).astype(o_ref.dtype)

    qs = pl.BlockSpec((tr, X_HEAD_DIM), lambda h, i: (i, h))
    ks = pl.BlockSpec((M, X_HEAD_DIM), lambda h, i: (0, h))
    return pl.pallas_call(
        body, grid=(XW // X_HEAD_DIM, L // tr), in_specs=[qs, ks, ks], out_specs=qs,
        out_shape=jax.ShapeDtypeStruct((L, XW), BF16),
        compiler_params=_cparams(("parallel", "parallel")), name=name)(q, k, v)


def _xatt_bwd(q, k, v, do, *, name):
    L, XW = q.shape
    M = k.shape[0]
    tr = _tile(L, 512)
    scale = X_HEAD_DIM ** -0.5

    def body(q_ref, k_ref, v_ref, do_ref, dq_ref, dk_ref, dv_ref):
        @pl.when(pl.program_id(1) == 0)
        def _():
            dk_ref[...] = jnp.zeros_like(dk_ref)
            dv_ref[...] = jnp.zeros_like(dv_ref)

        qv, kv, vv = q_ref[...], k_ref[...], v_ref[...]
        s = _dot(qv, kv, _NT) * scale
        p = jnp.exp(s - jnp.max(s, axis=-1, keepdims=True))
        p = p / jnp.sum(p, axis=-1, keepdims=True)
        dob = do_ref[...].astype(BF16)
        pb = p.astype(BF16)
        dv_ref[...] += _dot(pb, dob, _TN)
        dp = _dot(dob, vv, _NT)
        ds = (p * (dp - jnp.sum(dp * p, axis=-1, keepdims=True)) * scale).astype(BF16)
        dq_ref[...] = _dot(ds, kv, _NN)
        dk_ref[...] += _dot(ds, qv, _TN)

    qs = pl.BlockSpec((tr, X_HEAD_DIM), lambda h, i: (i, h))
    ks = pl.BlockSpec((M, X_HEAD_DIM), lambda h, i: (0, h))
    return pl.pallas_call(
        body, grid=(XW // X_HEAD_DIM, L // tr), in_specs=[qs, ks, ks, qs], out_specs=[qs, ks, ks],
        out_shape=[jax.ShapeDtypeStruct((L, XW), F32), jax.ShapeDtypeStruct((M, XW), F32), jax.ShapeDtypeStruct((M, XW), F32)],
        compiler_params=_cparams(("parallel", "arbitrary")), name=name)(q, k, v, do)


_ANY = pl.BlockSpec(memory_space=pl.ANY)


def _place():
    mx, my, mc = lax.axis_index("x"), lax.axis_index("y"), lax.axis_index("c")
    chips = [(1 - mx, my), (mx, 1 - my), (1 - mx, 1 - my)]
    return mx, my, mc, chips


def _rcopy(src, dst, ssem, rsem, dev):
    return pltpu.make_async_remote_copy(src_ref=src, dst_ref=dst, send_sem=ssem, recv_sem=rsem, device_id=dev,
                                        device_id_type=MESH)


STAGE_BYTES = 2 * 1024 * 1024


def _staged_copy(pairs, buf, isem, osem):
    n = len(pairs)
    ins = [pltpu.make_async_copy(src, buf.at[i % 2], isem.at[i % 2]) for i, (src, _) in enumerate(pairs)]
    outs = [pltpu.make_async_copy(buf.at[i % 2], dst, osem.at[i % 2]) for i, (_, dst) in enumerate(pairs)]
    ins[0].start()
    for i in range(n):
        if i + 1 < n:
            if i >= 1:
                outs[i - 1].wait()
            ins[i + 1].start()
        ins[i].wait()
        outs[i].start()
    for i in range(max(n - 2, 0), n):
        outs[i].wait()


_HBM = pl.BlockSpec(memory_space=pltpu.HBM)
_SEMS = pl.BlockSpec(memory_space=pltpu.SEMAPHORE)
_VM = pl.BlockSpec(memory_space=pltpu.VMEM)
_DATAFLOW = pltpu.SideEffectType.DATAFLOW_SIDE_EFFECTING


def _ag_copies(x_refs, land_refs, ssem, rsem, inbound, layer, which=None):
    mx, my, mc, chips = _place()
    me = 2 * mx + my
    cps = []
    for i, (x_ref, land_ref) in enumerate(zip(x_refs, land_refs)):
        w = i if which is None else which[i]
        R2 = x_ref.shape[1] // 2
        mine = x_ref.at[layer, pl.ds(mc * R2, R2)]
        for j, (px, py) in enumerate(chips):
            dst = land_ref.at[2 * px + py, mc] if inbound else land_ref.at[me, mc]
            cps.append(_rcopy(mine, dst, ssem.at[3 * w + j], rsem.at[3 * w + j], (px, py, mc)))
    return cps


def _ag_start(xs, layer, token, *, name):
    n = len(xs)
    lands = [pltpu.with_memory_space_constraint(lax.empty((N_CHIPS, 2, x.shape[1] // 2, x.shape[2]), x.dtype), pltpu.HBM) for x in xs]

    def body(*refs):
        x_refs, land_refs, tok_ref = refs[:n], refs[n:2 * n], refs[2 * n]
        ssem, rsem, tok_out = refs[2 * n + 1], refs[2 * n + 2], refs[-1]
        for cp in _ag_copies(x_refs, land_refs, ssem, rsem, False, layer):
            cp.start()
        tok_out[...] = tok_ref[...]

    outs = pl.pallas_call(
        body, name=name,
        out_shape=(pltpu.SemaphoreType.DMA((3 * n,)), pltpu.SemaphoreType.DMA((3 * n,)),
                   *[pltpu.HBM(t.shape, t.dtype) for t in lands], jax.ShapeDtypeStruct(token.shape, token.dtype)),
        in_specs=[_HBM] * (2 * n) + [_VM], out_specs=(_SEMS, _SEMS, *[_HBM] * n, _VM),
        input_output_aliases={n + i: 2 + i for i in range(n)},
        compiler_params=pltpu.CompilerParams(has_side_effects=_DATAFLOW),
    )(*xs, *lands, token)
    return outs[0], outs[1], list(outs[2:2 + n]), outs[-1]


def _ag_wait(ssem, rsem, xs, lands, after, which, layer, *, name):
    n = len(xs)

    def body(*refs):
        x_refs, land_refs, ssem_ref, rsem_ref = refs[:n], refs[n:2 * n], refs[2 * n], refs[2 * n + 1]
        for cp in _ag_copies(x_refs, land_refs, ssem_ref, rsem_ref, True, layer, which):
            cp.wait_send()
            cp.wait_recv()

    after = list(after) if isinstance(after, (list, tuple)) else [after]
    outs = pl.pallas_call(
        body, name=name,
        out_shape=tuple(pltpu.HBM(t.shape, t.dtype) for t in lands),
        in_specs=[_HBM] * (2 * n) + [_SEMS, _SEMS] + [_ANY] * len(after), out_specs=tuple([_HBM] * n),
        input_output_aliases={n + i: i for i in range(n)},
        compiler_params=pltpu.CompilerParams(has_side_effects=_DATAFLOW),
    )(*xs, *lands, ssem, rsem, *after)
    return list(outs)


def _ag_finish(lands, xs, layer, *, name):
    n = len(xs)

    def body(*refs):
        x_refs, o_refs, (ssem, rsem) = refs[n:2 * n], refs[2 * n:3 * n], refs[3 * n:]
        mx, my, mc, chips = _place()
        me = 2 * mx + my
        sib = (mx, my, 1 - mc)
        passed = []
        for w in range(n):
            for j, (px, py) in enumerate(chips):
                got = o_refs[w].at[2 * px + py, mc]
                passed.append(_rcopy(got, got, ssem.at[3 * w + j], rsem.at[3 * w + j], sib))
        for cp in passed:
            cp.start()
        for w in range(n):
            _, R, C = x_refs[w].shape
            R2 = R // 2
            ch = _rows_tile(R2, max(8, STAGE_BYTES // (C * x_refs[w].dtype.itemsize)))
            pairs = [(x_refs[w].at[layer, pl.ds(h * R2 + r, ch)], o_refs[w].at[me, h, pl.ds(r, ch)])
                     for h in range(2) for r in range(0, R2, ch)]
            pl.run_scoped(functools.partial(_staged_copy, pairs), pltpu.VMEM((2, ch, C), x_refs[w].dtype),
                          pltpu.SemaphoreType.DMA((2,)), pltpu.SemaphoreType.DMA((2,)))
        for w in range(n):
            for j, (px, py) in enumerate(chips):
                theirs = o_refs[w].at[2 * px + py, 1 - mc]
                _rcopy(theirs, theirs, ssem.at[3 * w + j], rsem.at[3 * w + j], sib).wait_recv()
        for cp in passed:
            cp.wait_send()

    return pl.pallas_call(
        body, in_specs=[_ANY] * (2 * n), out_specs=[_ANY] * n,
        out_shape=[jax.ShapeDtypeStruct(t.shape, t.dtype) for t in lands],
        input_output_aliases={i: i for i in range(n)},
        scratch_shapes=[pltpu.SemaphoreType.DMA((3 * n,)), pltpu.SemaphoreType.DMA((3 * n,))],
        name=name)(*lands, *xs)


def _ag4(x, *, name):
    def body(x_ref, o_ref, ssem, rsem, lsem):
        mx, my, mc, chips = _place()
        me = 2 * mx + my
        loc = pltpu.make_async_copy(x_ref, o_ref.at[me], lsem)
        loc.start()
        sends = [_rcopy(x_ref, o_ref.at[me], ssem.at[j], rsem.at[j], (px, py, mc)) for j, (px, py) in enumerate(chips)]
        for cp in sends:
            cp.start()
        for j, (px, py) in enumerate(chips):
            _rcopy(x_ref, o_ref.at[2 * px + py], ssem.at[j], rsem.at[j], (px, py, mc)).wait_recv()
        for cp in sends:
            cp.wait_send()
        loc.wait()

    return pl.pallas_call(
        body, in_specs=[_ANY], out_specs=_ANY, out_shape=jax.ShapeDtypeStruct((N_CHIPS,) + x.shape, x.dtype),
        scratch_shapes=[pltpu.SemaphoreType.DMA((3,)), pltpu.SemaphoreType.DMA((3,)), pltpu.SemaphoreType.DMA(())],
        name=name)(x)


def _rs_sib(gs, after=(), *, name):
    n, na = len(gs), len(after)

    def body(*refs):
        g_refs, t_refs, (ssem, rsem) = refs[:n], refs[n + na:2 * n + na], refs[2 * n + na:]
        mx, my, mc, _ = _place()
        sib = (mx, my, 1 - mc)
        sends = []
        for w in range(n):
            R2 = g_refs[w].shape[1] // 2
            for k in range(N_CHIPS):
                sends.append(_rcopy(g_refs[w].at[k, pl.ds((1 - mc) * R2, R2)], t_refs[w].at[k], ssem.at[N_CHIPS * w + k],
                                    rsem.at[N_CHIPS * w + k], sib))
        for cp in sends:
            cp.start()
        for w in range(n):
            for k in range(N_CHIPS):
                t = t_refs[w].at[k]
                _rcopy(t, t, ssem.at[N_CHIPS * w + k], rsem.at[N_CHIPS * w + k], sib).wait_recv()
        for cp in sends:
            cp.wait_send()

    return pl.pallas_call(
        body, in_specs=[_ANY] * (n + na), out_specs=[_ANY] * n,
        out_shape=[jax.ShapeDtypeStruct((N_CHIPS, g.shape[1] // 2, g.shape[2]), g.dtype) for g in gs],
        scratch_shapes=[pltpu.SemaphoreType.DMA((N_CHIPS * n,)), pltpu.SemaphoreType.DMA((N_CHIPS * n,))],
        name=name)(*gs, *after)


def _a2a_copies(h_refs, got_refs, ssem, rsem, inbound):
    mx, my, mc, chips = _place()
    me = 2 * mx + my
    cps = []
    for w, (h_ref, got_ref) in enumerate(zip(h_refs, got_refs)):
        for j, (px, py) in enumerate(chips):
            dst = got_ref.at[2 * px + py] if inbound else got_ref.at[me]
            cps.append(_rcopy(h_ref.at[2 * px + py], dst, ssem.at[3 * w + j], rsem.at[3 * w + j], (px, py, mc)))
    return cps


def _a2a_start(hs, *, name):
    n = len(hs)
    gots = [pltpu.with_memory_space_constraint(lax.empty(h.shape, h.dtype), pltpu.HBM) for h in hs]
    hs = [pltpu.with_memory_space_constraint(h, pltpu.HBM) for h in hs]

    def body(*refs):
        h_refs, got_refs = refs[:n], refs[n:2 * n]
        ssem, rsem, tok_out = refs[2 * n], refs[2 * n + 1], refs[-1]
        for cp in _a2a_copies(h_refs, got_refs, ssem, rsem, False):
            cp.start()
        tok_out[...] = jnp.zeros_like(tok_out)

    outs = pl.pallas_call(
        body, name=name,
        out_shape=(pltpu.SemaphoreType.DMA((3 * n,)), pltpu.SemaphoreType.DMA((3 * n,)),
                   *[pltpu.HBM(h.shape, h.dtype) for h in hs], *[pltpu.HBM(h.shape, h.dtype) for h in hs],
                   jax.ShapeDtypeStruct((8, 128), F32)),
        in_specs=[_HBM] * (2 * n), out_specs=(_SEMS, _SEMS, *[_HBM] * (2 * n), _VM),
        input_output_aliases={i: 2 + i for i in range(2 * n)},
        compiler_params=pltpu.CompilerParams(has_side_effects=_DATAFLOW),
    )(*hs, *gots)
    return outs[0], outs[1], list(outs[2:2 + n]), list(outs[2 + n:2 + 2 * n]), outs[-1]


def _a2a_wait(ssem, rsem, hs, gots, after, *, name):
    n = len(hs)

    def body(*refs):
        h_refs, got_refs, ssem_ref, rsem_ref = refs[:n], refs[n:2 * n], refs[2 * n], refs[2 * n + 1]
        for cp in _a2a_copies(h_refs, got_refs, ssem_ref, rsem_ref, True):
            cp.wait_send()
            cp.wait_recv()

    after = list(after) if isinstance(after, (list, tuple)) else [after]
    outs = pl.pallas_call(
        body, name=name,
        out_shape=tuple(pltpu.HBM(h.shape, h.dtype) for h in hs + gots),
        in_specs=[_HBM] * (2 * n) + [_SEMS, _SEMS] + [_ANY] * len(after), out_specs=tuple([_HBM] * (2 * n)),
        input_output_aliases={i: i for i in range(2 * n)},
        compiler_params=pltpu.CompilerParams(has_side_effects=_DATAFLOW),
    )(*hs, *gots, ssem, rsem, *after)
    return list(outs[:n]), list(outs[n:])


def _sib_fill(bufs, after=(), *, name):
    n, na = len(bufs), len(after)

    def body(*refs):
        o_refs, (ssem, rsem) = refs[n + na:2 * n + na], refs[2 * n + na:]
        mx, my, mc, _ = _place()
        sib = (mx, my, 1 - mc)
        sends = [_rcopy(o_refs[w].at[mc], o_refs[w].at[mc], ssem.at[w], rsem.at[w], sib) for w in range(n)]
        for cp in sends:
            cp.start()
        for w in range(n):
            t = o_refs[w].at[1 - mc]
            _rcopy(t, t, ssem.at[w], rsem.at[w], sib).wait_recv()
        for cp in sends:
            cp.wait_send()

    return pl.pallas_call(
        body, in_specs=[_ANY] * (n + na), out_specs=[_ANY] * n, out_shape=[jax.ShapeDtypeStruct(b.shape, b.dtype) for b in bufs],
        input_output_aliases={i: i for i in range(n)},
        scratch_shapes=[pltpu.SemaphoreType.DMA((n,)), pltpu.SemaphoreType.DMA((n,))], name=name)(*bufs, *after)


def _rows_tile(n, pref=512):
    t = min(n, pref)
    while n % t or (t % 8 and t != n):
        t -= 1
    return t


def _rs_add(g, theirs, c_arr, payload, *, name):
    _, R, C = g.shape
    R2 = R // 2
    tr = _rows_tile(R2, 256)
    nb = R2 // tr

    def body(c_ref, g_ref, t_ref, o_ref):
        o_ref[...] = (g_ref[...].astype(F32) + t_ref[...].astype(F32)).astype(o_ref.dtype)

    blk = pl.BlockSpec((None, tr, C), lambda k, i, c: (k, i, 0))
    return pl.pallas_call(
        body,
        grid_spec=pltpu.PrefetchScalarGridSpec(
            num_scalar_prefetch=1, grid=(N_CHIPS, nb),
            in_specs=[pl.BlockSpec((None, tr, C), lambda k, i, c: (k, c[0] * nb + i, 0)), blk], out_specs=blk),
        out_shape=jax.ShapeDtypeStruct((N_CHIPS, R2, C), payload),
        compiler_params=_cparams(("arbitrary", "arbitrary")), name=name)(c_arr, g, theirs)


def _rs_sum(chip_sum, got, mc_arr, *, name):
    _, R2, C = chip_sum.shape
    tr = _rows_tile(R2, 256)

    def body(s_ref, own_ref, g0, g1, g2, g3, o_ref):
        me = s_ref[0]
        acc = jnp.zeros((tr, C), F32)
        for k, g_ref in enumerate((g0, g1, g2, g3)):
            acc = acc + jnp.where(me == k, own_ref[...], g_ref[...]).astype(F32)
        o_ref[...] = acc

    def got_spec(k):
        return pl.BlockSpec((None, tr, C), lambda i, s: (jnp.where(s[0] == k, (k + 1) % N_CHIPS, k), i, 0))

    return pl.pallas_call(
        body,
        grid_spec=pltpu.PrefetchScalarGridSpec(
            num_scalar_prefetch=1, grid=(R2 // tr,),
            in_specs=[pl.BlockSpec((None, tr, C), lambda i, s: (s[0], i, 0))] + [got_spec(k) for k in range(N_CHIPS)],
            out_specs=pl.BlockSpec((None, tr, C), lambda i, s: (s[1], i, 0))),
        out_shape=jax.ShapeDtypeStruct((2, R2, C), F32),
        compiler_params=_cparams(("arbitrary",)), name=name)(mc_arr, chip_sum, got, got, got, got)


def _adamw(w, m, v, gs, *, name, first=0, prev=None):
    _, _, R2, C = w.shape
    nl = len(gs)
    tr = _rows_tile(R2, 256)
    c1 = 1.0 / (1.0 - ADAM_B1 ** ADAM_STEP)
    c2 = 1.0 / (1.0 - ADAM_B2 ** ADAM_STEP)

    def body(w_ref, m_ref, v_ref, *refs):
        g_refs, (go_ref, d_ref, mo_ref, vo_ref) = refs[:nl], refs[-4:]
        l = pl.program_id(0)
        for ll in range(nl):
            @pl.when(l == ll)
            def _(ll=ll):
                gv = g_refs[ll][...]
                mn = ADAM_B1 * m_ref[...] + (1.0 - ADAM_B1) * gv
                vn = ADAM_B2 * v_ref[...] + (1.0 - ADAM_B2) * (gv * gv)
                go_ref[...] = gv
                mo_ref[...] = mn
                vo_ref[...] = vn
                d_ref[...] = -ADAM_LR * ((mn * c1) / (jnp.sqrt(vn * c2) + ADAM_EPS) + ADAM_WD * w_ref[...])

    def g_spec(ll):
        return pl.BlockSpec((None, tr, C), lambda l, h, i: (jnp.where(l == ll, h, 0), jnp.where(l == ll, i, 0), 0))

    ws = pl.BlockSpec((None, None, tr, C), lambda l, h, i: (l + first, h, i, 0))
    shp = jax.ShapeDtypeStruct(w.shape, F32)
    prev = list(prev) if prev is not None else []
    return pl.pallas_call(
        body, grid=(nl, 2, R2 // tr), in_specs=[ws, ws, ws] + [g_spec(ll) for ll in range(nl)] + [_ANY] * len(prev),
        out_specs=[ws] * 4, out_shape=[shp] * 4,
        input_output_aliases={3 + nl + k: k for k in range(len(prev))},
        compiler_params=_cparams(("arbitrary", "arbitrary", "arbitrary")), name=name)(w, m, v, *gs, *prev)


class _GradReducer:
    def __init__(self, tag, payload):
        self.tag, self.payload = tag, payload
        self.me = (2 * lax.axis_index("x") + lax.axis_index("y")).astype(jnp.int32)
        self.c = lax.axis_index("c").astype(jnp.int32)

    def start(self, names, gs, after=()):
        theirs = _rs_sib(gs, after, name=f"rs_sib_{self.tag}")
        hs = [_rs_add(g, t, self.c.reshape(1), self.payload, name=f"rs_add_{n}") for n, g, t in zip(names, gs, theirs)]
        self.names = names
        self.ssem, self.rsem, self.hs, self.gots, token = _a2a_start(hs, name=f"rs_a2a_start_{self.tag}")
        return token

    def finish(self, after):
        hs, gots = _a2a_wait(self.ssem, self.rsem, self.hs, self.gots, after, name=f"rs_a2a_wait_{self.tag}")
        mc = jnp.stack([self.me, self.c])
        return {n: _rs_sum(h, g, mc, name=f"rs_sum_{n}") for n, h, g in zip(self.names, hs, gots)}


WEIGHTS = ["rel_bias", "mem_norm_g", "norm_mix_g", "w_in", "s5_lam_re", "s5_lam_im", "s5_log_dt", "s5_b_re", "s5_b_im",
           "s5_c_re", "s5_c_im", "s5_d", "s5_w_glu", "pool_w", "pool_scale", "conv_w_dw", "conv_b_dw", "conv_ln_g",
           "conv_ln_b", "conv_w_pw", "grp_norm_g", "w_out", "norm_x_g", "w_xq", "w_xk", "w_xv", "w_xo", "norm_mlp_g",
           "w_up", "w_down", "norm_final_g"]
SHARDED = {"w_in": "col", "s5_w_glu": "row", "conv_w_dw": "col", "conv_w_pw": "row", "w_out": "row", "w_xq": "row",
           "w_xk": "row", "w_xv": "row", "w_xo": "col", "w_up": "col", "w_down": "row"}
AG_FIRST = ("w_in", "s5_w_glu", "conv_w_dw", "conv_w_pw")
SMALL = [n for n in WEIGHTS if n not in SHARDED]
SMALL_ALIGN = N_CHIPS * 2 * 256 * 128


def _mat(w, kind):
    return w.reshape(w.shape[0] * w.shape[1], w.shape[2]) if kind == "row" else w


def _att_bias(rel_bias):
    bucket, valid = _att_tables()
    onehot = (jnp.asarray(bucket.reshape(-1))[:, None] == jnp.arange(REL_BUCKETS)[None, :]).astype(F32)
    b = jnp.dot(onehot, rel_bias, precision=lax.Precision.HIGHEST).reshape(bucket.shape + (rel_bias.shape[1],))
    return jnp.where(jnp.asarray(valid)[:, None], jnp.transpose(b, (0, 3, 1, 2)), NEG_INF)


def _rel_bias_grad(dbias):
    bucket, _ = _att_tables()
    nb, H = dbias.shape[0], dbias.shape[1]
    onehot = (jnp.asarray(bucket.reshape(-1))[:, None] == jnp.arange(REL_BUCKETS)[None, :]).astype(BF16)
    flat = jnp.transpose(dbias.reshape(nb, H, -1), (1, 0, 2)).reshape(H, -1)
    return _mm(flat, onehot, "nn", tk=16384, name="rel_bias_grad").T


def _layer_fwd(h, mem_n, bias, sp, bw, l):
    L, D = h.shape
    GW = D // N_MIXERS
    G = GW // S5_CH
    E = GW // ATT_HEADS
    sv = {"h": h}
    xn = _rms_fwd(h, sp["norm_mix_g"][l][None], name="norm_mix")
    proj = _mm(xn, bw["w_in"], "nn", name="proj_in")
    sv.update(xn=xn, proj=proj)
    disc, disc_vjp = jax.vjp(_s5_disc, sp["s5_lam_re"][l], sp["s5_lam_im"][l], sp["s5_log_dt"][l], sp["s5_b_re"][l], sp["s5_b_im"][l])
    ab_r, ab_i, bb_r, bb_i = disc
    s5 = dict(
        bdr=_bd(jnp.transpose(bb_r, (0, 2, 1))).astype(BF16), bdi=_bd(jnp.transpose(bb_i, (0, 2, 1))).astype(BF16),
        cdr=_bd(jnp.transpose(sp["s5_c_re"][l], (0, 2, 1))).astype(BF16), cdi=_bd(jnp.transpose(sp["s5_c_im"][l], (0, 2, 1))).astype(BF16),
        d=sp["s5_d"][l][None], wglu=bw["s5_w_glu"], ab=(ab_r.reshape(-1), ab_i.reshape(-1)), vjp=disc_vjp)
    pw, tr, ti = _cpow_tables(*s5["ab"])
    y_a, xr, xi = _s5_fwd(proj, s5["bdr"], s5["bdi"], pw, tr, ti, s5["cdr"], s5["cdi"], s5["d"], s5["wglu"], name="s5_fwd")
    sv.update(s5=s5, xr=xr, xi=xi, y_a=y_a)
    pool_s = sp["pool_scale"][l].reshape(len(POOL_WINDOWS), 1, -1)
    y_b = _pool_fwd(proj, sp["pool_w"][l], pool_s, name="pool_fwd")
    sv.update(y_b=y_b, pool_s=pool_s)
    hc = _conv1_fwd(proj, bw["conv_w_dw"], sp["conv_b_dw"][l][None], name="conv_dw_fwd")
    y_c = _conv2_fwd(hc, sp["conv_ln_g"][l][None], sp["conv_ln_b"][l][None], bw["conv_w_pw"], name="conv_pw_fwd")
    sv.update(hc=hc, y_c=y_c)
    y_d, lse_d = _datt_fwd(proj, bias, name="att_fwd")
    sv.update(y_d=y_d, lse_d=lse_d)
    yn = _grp_fwd([y_a, y_b, y_c, y_d], sp["grp_norm_g"][l][None], name="grp_fwd")
    bw.rest(yn)
    h1 = _mm(yn, bw["w_out"], "nn", res=h, name="proj_out")
    sv.update(yn=yn, h1=h1)
    hx = _rms_fwd(h1, sp["norm_x_g"][l][None], name="norm_x")
    q_x = _mm(hx, bw["w_xq"], "nn", out_dtype=BF16, name="xq")
    k_x = _mm(mem_n, bw["w_xk"], "nn", out_dtype=BF16, name="xk")
    v_x = _mm(mem_n, bw["w_xv"], "nn", out_dtype=BF16, name="xv")
    o_x = _xatt_fwd(q_x, k_x, v_x, name="xatt_fwd")
    h2 = _mm(o_x, bw["w_xo"], "nn", res=h1, name="xo")
    sv.update(hx=hx, q_x=q_x, k_x=k_x, v_x=v_x, o_x=o_x, h2=h2)
    hm = _rms_fwd(h2, sp["norm_mlp_g"][l][None], name="norm_mlp")
    up, act = _mm(hm, bw["w_up"], "nn", epi="relu2", out_dtype=BF16, name="mlp_up")
    h3 = _mm(act, bw["w_down"], "nn", res=h2, name="mlp_down")
    sv.update(hm=hm, up=up, act=act)
    return h3, sv


def _stack_rows(g):
    return g.reshape(N_CHIPS, g.shape[0] // N_CHIPS, g.shape[1])


def _layer_bwd(dh3, d_memn, mem_n, bias, sp, bw, sv, l, on_grads):
    L, D = dh3.shape
    GW = D // N_MIXERS
    G = GW // S5_CH
    E = GW // ATT_HEADS
    gs, gb = {}, {}
    d_up = _mm(dh3, bw["w_down"], "nt", epi="relu2_bwd", aux=sv["up"], out_dtype=BF16, name="mlp_down_dx")
    gb["w_down"] = _stack_rows(_mm(sv["act"], dh3, "tn", out_dtype=BF16, name="mlp_down_dw"))
    gb["w_up"] = _mm(sv["hm"], d_up, "tn", out_dtype=BF16, out_stack=N_CHIPS, name="mlp_up_dw")
    d_hm = _mm(d_up, bw["w_up"], "nt", name="mlp_up_dx")
    dh2, gs["norm_mlp_g"] = _rms_bwd(sv["h2"], sp["norm_mlp_g"][l][None], d_hm, dh3, name="norm_mlp_bwd")
    d_ox = _mm(dh2, bw["w_xo"], "nt", name="xo_dx")
    gb["w_xo"] = _mm(sv["o_x"], dh2, "tn", out_dtype=BF16, out_stack=N_CHIPS, name="xo_dw")
    dq_x, dk_x, dv_x = _xatt_bwd(sv["q_x"], sv["k_x"], sv["v_x"], d_ox, name="xatt_bwd")
    gb["w_xq"] = _stack_rows(_mm(sv["hx"], dq_x, "tn", out_dtype=BF16, name="xq_dw"))
    gb["w_xk"] = _stack_rows(_mm(mem_n, dk_x, "tn", out_dtype=BF16, name="xk_dw"))
    gb["w_xv"] = _stack_rows(_mm(mem_n, dv_x, "tn", out_dtype=BF16, name="xv_dw"))
    d_memn = _mm(dk_x, bw["w_xk"], "nt", res=d_memn, name="xk_dx")
    d_memn = _mm(dv_x, bw["w_xv"], "nt", res=d_memn, name="xv_dx")
    d_hx = _mm(dq_x, bw["w_xq"], "nt", name="xq_dx")
    dh1, gs["norm_x_g"] = _rms_bwd(sv["h1"], sp["norm_x_g"][l][None], d_hx, dh2, name="norm_x_bwd")
    d_yn = _mm(dh1, bw["w_out"], "nt", name="proj_out_dx")
    gb["w_out"] = _stack_rows(_mm(sv["yn"], dh1, "tn", out_dtype=BF16, name="proj_out_dw"))
    ys = [sv["y_a"], sv["y_b"], sv["y_c"], sv["y_d"]]
    dya, dyb, dyc, dyd, gs["grp_norm_g"] = _grp_bwd(ys, sp["grp_norm_g"][l][None], d_yn, name="grp_bwd")
    dq, dk, dv, dbias = _datt_bwd(sv["proj"], bias, sv["y_d"], sv["lse_d"], dyd, name="att_bwd")
    dhc, gs["conv_ln_g"], gs["conv_ln_b"], g_pw = _conv2_bwd(sv["hc"], sp["conv_ln_g"][l][None], sp["conv_ln_b"][l][None],
                                                               bw["conv_w_pw"], dyc, name="conv_pw_bwd")
    gb["conv_w_pw"] = _stack_rows(g_pw)
    dval, dgate, g_dw, gs["conv_b_dw"] = _conv1_bwd(sv["proj"], dhc, bw["conv_w_dw"], name="conv_dw_bwd")
    gb["conv_w_dw"] = jnp.transpose(g_dw.reshape(CONV_PAD, N_CHIPS, GW // N_CHIPS), (1, 0, 2))
    du_b, gs["pool_w"], g_ps = _pool_bwd(sv["proj"], dyb, sp["pool_w"][l], sv["pool_s"], name="pool_bwd")
    gs["pool_scale"] = g_ps.reshape(-1)
    s5 = sv["s5"]
    conj = (s5["ab"][0], -s5["ab"][1])
    pw, tr, ti = _cpow_tables(*conj)
    du_a, g_glu, g_d, dcdr, dcdi, dbdr, dbdi, dar, dai = _s5_bwd(
        sv["proj"], sv["xr"], sv["xi"], dya, s5["bdr"], s5["bdi"], pw, tr[::-1], ti[::-1], s5["cdr"], s5["cdi"], s5["d"], s5["wglu"],
        name="s5_bwd")
    gb["s5_w_glu"] = _stack_rows(g_glu)
    gs["s5_d"] = g_d.reshape(-1)
    gs["s5_c_re"] = jnp.transpose(_bd_extract(dcdr, G), (0, 2, 1))
    gs["s5_c_im"] = jnp.transpose(_bd_extract(dcdi, G), (0, 2, 1))
    d_bb_r = jnp.transpose(_bd_extract(dbdr, G), (0, 2, 1))
    d_bb_i = jnp.transpose(_bd_extract(dbdi, G), (0, 2, 1))
    d_ab_r = jnp.sum(dar, axis=0).reshape(G, S5_STATE)
    d_ab_i = jnp.sum(dai, axis=0).reshape(G, S5_STATE)
    gs["s5_lam_re"], gs["s5_lam_im"], gs["s5_log_dt"], gs["s5_b_re"], gs["s5_b_im"] = s5["vjp"]((d_ab_r, d_ab_i, d_bb_r, d_bb_i))
    d_proj = jnp.concatenate([du_a, du_b, dval, dgate, dq, dk, dv], axis=1)
    gb["w_in"] = _mm(sv["xn"], d_proj, "tn", out_dtype=BF16, out_stack=N_CHIPS, name="proj_in_dw")
    token = on_grads(l, gb)
    d_xn = _mm(d_proj, bw["w_in"], "nt", name="proj_in_dx")
    dh0, gs["norm_mix_g"] = _rms_bwd(sv["h"], sp["norm_mix_g"][l][None], d_xn, dh1, token=token, name="norm_mix_bwd")
    gs = {n: g.reshape(sp[n].shape[1:]) for n, g in gs.items()}
    return dh0, d_memn, dbias, gs


def _device_step(x, mem, target, sp, get_big, on_grads):
    nl = sp["norm_mix_g"].shape[0]
    mem_n = _rms_fwd(mem, sp["mem_norm_g"][None], name="norm_mem")
    bias = _att_bias(sp["rel_bias"])
    h, saved, big = x, [], {n: [] for n in SHARDED}
    for l in range(nl):
        bw = get_big(l, h)
        h, sv = _layer_fwd(h, mem_n, bias, sp, bw, l)
        saved.append(sv)
        for n in SHARDED:
            big[n].append(bw[n])
    loss, dh, g_final = _loss_head(h, sp["norm_final_g"][None], target, name="loss_head")
    d_memn = jnp.zeros(mem.shape, F32)
    dbias = jnp.zeros(bias.shape, F32)
    sg = {n: [None] * nl for n in SMALL}
    for l in reversed(range(nl)):
        dh, d_memn, dbias_l, gs = _layer_bwd(dh, d_memn, mem_n, bias, sp, {n: big[n][l] for n in SHARDED}, saved[l], l, on_grads)
        dbias = dbias + dbias_l
        for n, g in gs.items():
            sg[n][l] = g
    small = {n: jnp.stack(g) for n, g in sg.items() if g[0] is not None}
    small["norm_final_g"] = g_final.reshape(-1)
    _, g_mem = _rms_bwd(mem, sp["mem_norm_g"][None], d_memn, None, name="norm_mem_bwd")
    small["mem_norm_g"] = g_mem.reshape(-1)
    small["rel_bias"] = _rel_bias_grad(dbias)
    return loss, dh, small


def _gather_big(shards, prep):
    nl = shards["w_in"].shape[0]
    token = jnp.zeros((8, 128), F32)
    stacked = [pltpu.with_memory_space_constraint(
        jnp.pad(shards[n], ((0, 0), (0, CONV_PAD - CONV_WIDTH), (0, 0))) if n == "conv_w_dw" else shards[n].astype(BF16), pltpu.HBM)
        for n in SHARDED]
    pending = []
    for l in range(nl):
        ssem, rsem, lands, token = _ag_start(stacked, l, token, name=f"ag_start_l{l}")
        pending.append((ssem, rsem, lands))

    names = list(SHARDED)

    class LayerWeights(dict):
        def __init__(self, l, after):
            super().__init__()
            self.l, self.state = l, pending[l]
            first = [i for i, n in enumerate(names) if n in AG_FIRST] if l == 0 else list(range(len(names)))
            self.left = [i for i in range(len(names)) if i not in first]
            self._complete(first, [token] + list(prep) if l == 0 else [after], "a")

        def _complete(self, which, after, tag):
            ssem, rsem, lands = self.state
            xs = [stacked[i] for i in which]
            lands = _ag_wait(ssem, rsem, xs, [lands[i] for i in which], after, which, self.l, name=f"ag_wait_l{self.l}{tag}")
            fulls = _ag_finish(lands, xs, self.l, name=f"ag_finish_l{self.l}{tag}")
            for i, x, full in zip(which, xs, fulls):
                n = names[i]
                full = full.reshape(N_CHIPS, x.shape[1], x.shape[2])
                self[n] = jnp.transpose(full, (1, 0, 2)).reshape(CONV_PAD, -1) if n == "conv_w_dw" else _mat(full, SHARDED[n])

        def rest(self, after):
            if self.left:
                self._complete(self.left, [after], "b")
                self.left = []

    return LayerWeights


def _pack_small(vals, extra=None):
    flat = [vals[n].reshape(-1).astype(F32) for n in SMALL]
    flat.append(jnp.zeros((1,), F32) if extra is None else extra.reshape(1))
    v = jnp.concatenate(flat)
    n_pad = -(-v.shape[0] // SMALL_ALIGN) * SMALL_ALIGN
    return jnp.pad(v, (0, n_pad - v.shape[0]))


def _unpack_small(flat, like):
    def cut(pos, size):
        r0 = pos // 128
        piece = flat[r0:-(-(pos + size) // 128)].reshape(-1)
        return piece[pos - r0 * 128:pos - r0 * 128 + size]

    out, pos = {}, 0
    for n in SMALL:
        size = math.prod(like[n].shape)
        out[n] = cut(pos, size).reshape(like[n].shape)
        pos += size
    return out, cut(pos, 1)[0]


def kernel(x, mem, rel_bias, mem_norm_g, norm_mix_g, w_in, s5_lam_re, s5_lam_im, s5_log_dt, s5_b_re, s5_b_im, s5_c_re, s5_c_im, s5_d, s5_w_glu, pool_w, pool_scale, conv_w_dw, conv_b_dw, conv_ln_g, conv_ln_b, conv_w_pw, grp_norm_g, w_out, norm_x_g, w_xq, w_xk, w_xv, w_xo, norm_mlp_g, w_up, w_down, norm_final_g, loss_target, m_rel_bias, m_mem_norm_g, m_norm_mix_g, m_w_in, m_s5_lam_re, m_s5_lam_im, m_s5_log_dt, m_s5_b_re, m_s5_b_im, m_s5_c_re, m_s5_c_im, m_s5_d, m_s5_w_glu, m_pool_w, m_pool_scale, m_conv_w_dw, m_conv_b_dw, m_conv_ln_g, m_conv_ln_b, m_conv_w_pw, m_grp_norm_g, m_w_out, m_norm_x_g, m_w_xq, m_w_xk, m_w_xv, m_w_xo, m_norm_mlp_g, m_w_up, m_w_down, m_norm_final_g, v_rel_bias, v_mem_norm_g, v_norm_mix_g, v_w_in, v_s5_lam_re, v_s5_lam_im, v_s5_log_dt, v_s5_b_re, v_s5_b_im, v_s5_c_re, v_s5_c_im, v_s5_d, v_s5_w_glu, v_pool_w, v_pool_scale, v_conv_w_dw, v_conv_b_dw, v_conv_ln_g, v_conv_ln_b, v_conv_w_pw, v_grp_norm_g, v_w_out, v_norm_x_g, v_w_xq, v_w_xk, v_w_xv, v_w_xo, v_norm_mlp_g, v_w_up, v_w_down, v_norm_final_g):
    env = dict(locals())
    w = {n: env[n] for n in WEIGHTS}
    m = {n: env["m_" + n] for n in WEIGHTS}
    v = {n: env["v_" + n] for n in WEIGHTS}
    sp = {n: w[n] for n in SMALL}
    small_wmv = [_pack_small(t).reshape(1, 2, -1, 128) for t in (w, m, v)]
    get_big = _gather_big({n: w[n] for n in SHARDED}, small_wmv)
    nl = w["w_in"].shape[0]
    names = list(SHARDED)
    reducers, reduced, tokens = {}, {}, {}

    def on_grads(l, gb):
        reducers[l] = _GradReducer(f"l{l}", BF16)
        tokens[l] = reducers[l].start(names, [gb[n] for n in names])
        if l + 1 in reducers:
            reduced[l + 1] = reducers[l + 1].finish(gb["w_in"])
        return tokens[l]

    loss, grad_x, g_small = _device_step(x[0], mem[0], loss_target[0], sp, get_big, on_grads)

    def shard_views(n):
        pad = ((0, 0), (0, CONV_PAD - CONV_WIDTH), (0, 0)) if n == "conv_w_dw" else None
        wmv = [jnp.pad(t[n], pad) if pad else t[n] for t in (w, m, v)]
        _, R, C = wmv[0].shape
        return [t.reshape(nl, 2, R // 2, C) for t in wmv]

    grad, delta, new_m, new_v = {}, {}, {}, {}
    busy, upper = [grad_x], {}
    if nl > 1:
        filled = _sib_fill([reduced[l][n] for n in names for l in range(1, nl)], [tokens[0]], name="rs_fill_upper")
        for i, n in enumerate(names):
            upper[n] = _adamw(*shard_views(n), filled[i * (nl - 1):(i + 1) * (nl - 1)], first=1, name=f"adamw_upper_{n}")
            busy.append(upper[n][0])
    reduced[0] = reducers[0].finish(busy)
    packed = _pack_small(g_small, loss).reshape(N_CHIPS, -1, 128)
    small = _GradReducer("small", F32)
    token = small.start(["small"], [packed], after=[reduced[0][names[0]]])
    quarter = _sib_fill([small.finish(token)["small"]], name="rs_fill_small")[0]
    total = _ag4(quarter.reshape(-1, 128), name="ag_small").reshape(2, -1, 128)
    outs = _adamw(*small_wmv, [total], name="adamw_small")
    filled = _sib_fill([reduced[0][n] for n in names], name="rs_fill_first")
    for i, n in enumerate(names):
        res = _adamw(*shard_views(n), [filled[i]], prev=upper.get(n), name=f"adamw_first_{n}")
        R = w[n].shape[1]
        grad[n], delta[n], new_m[n], new_v[n] = [o.reshape(nl, -1, o.shape[-1])[:, :R] for o in res]
    loss_sum = None
    for o, dst in zip(outs, (grad, delta, new_m, new_v)):
        vals, last = _unpack_small(o.reshape(-1, 128), sp)
        dst.update(vals)
        loss_sum = last if loss_sum is None else loss_sum
    return (loss_sum, grad_x[None], *[grad[n] for n in WEIGHTS], *[delta[n] for n in WEIGHTS],
            *[new_m[n] for n in WEIGHTS], *[new_v[n] for n in WEIGHTS])
```
